```python
import jax, jax.numpy as jnp
from jax import lax
import numpy as np

D_MODEL = 2048
BATCH = 8
SEQ = 4096
DEPTH = 1

GRID_W = 64
CTX_LEN = 256
MLA_HEADS = 8
Q_LORA = 512
KV_LORA = 256
NOPE_DIM = 128
ROPE_DIM = 64
V_DIM = 128
QK_DIM = NOPE_DIM + ROPE_DIM
ROPE_THETA = 10000.0
Q_BLOCK = 128
CHUNK = 128
SGU_GROUPS = 8
SGU_WIDTH = 1024
SGU_GROUP_DIM = SGU_WIDTH // SGU_GROUPS
D_FF = ((8 * D_MODEL // 3 + 255) // 256) * 256
N_BRANCH = 2
N_MOD = 6
EPS = 1e-6
OFF_KVC = Q_LORA
OFF_U = OFF_KVC + KV_LORA + ROPE_DIM
OFF_V = OFF_U + SGU_WIDTH
OFF_GATE = OFF_V + SGU_WIDTH
IN_COLS = OFF_GATE + N_BRANCH * D_MODEL

kernel_name = "hybrid_mla_sgu_prefix_dit_block"


def _rms(x, g):
    xf = x.astype(jnp.float32)
    y = xf * lax.rsqrt(jnp.mean(xf * xf, axis=-1, keepdims=True) + EPS)
    return (y * g.astype(jnp.float32)).astype(x.dtype)


def _modulate(h, shift, scale):
    return h * (1 + scale) + shift


def _axial_angles(n):
    rows = n // GRID_W
    row = jnp.repeat(jnp.arange(rows, dtype=jnp.float32), GRID_W)
    col = jnp.tile(jnp.arange(GRID_W, dtype=jnp.float32), rows)
    nf = ROPE_DIM // 4
    freqs = ROPE_THETA ** (-jnp.arange(nf, dtype=jnp.float32) / nf)
    return row[:, None] * freqs[None, :], col[:, None] * freqs[None, :]


def _rotate(t, ang):
    t1, t2 = jnp.split(t, 2, axis=-1)
    cos = jnp.cos(ang)[:, None, :]
    sin = jnp.sin(ang)[:, None, :]
    return jnp.concatenate([t1 * cos - t2 * sin, t2 * cos + t1 * sin], axis=-1)


def _rope_tail(t, ang_r, ang_c):
    nope, rope = t[..., :NOPE_DIM], t[..., NOPE_DIM:].astype(jnp.float32)
    rot = jnp.concatenate([_rotate(rope[..., :ROPE_DIM // 2], ang_r),
                           _rotate(rope[..., ROPE_DIM // 2:], ang_c)], axis=-1)
    return jnp.concatenate([nope, rot.astype(t.dtype)], axis=-1)


def _mla_queries(qc, q_norm_g, w_uq, qk_norm_q):
    q = _rms(qc, q_norm_g) @ w_uq
    q = q.reshape(q.shape[:-1] + (MLA_HEADS, QK_DIM))
    return _rms(q, qk_norm_q)


def _mla_keys_values(kv_in, kv_norm_g, w_ukv, qk_norm_k):
    kvc, k_rope = kv_in[..., :KV_LORA], kv_in[..., KV_LORA:]
    kv = _rms(kvc, kv_norm_g) @ w_ukv
    kv = kv.reshape(kv.shape[:-1] + (MLA_HEADS, NOPE_DIM + V_DIM))
    k_nope, v = kv[..., :NOPE_DIM], kv[..., NOPE_DIM:]
    k_rope = jnp.broadcast_to(k_rope[..., None, :], k_nope.shape[:-1] + (ROPE_DIM,))
    k = _rms(jnp.concatenate([k_nope, k_rope], axis=-1), qk_norm_k)
    return k, v


def _attention(q, k, v):
    b, n = q.shape[0], q.shape[1]
    nblk = n // Q_BLOCK
    qb = q.reshape(b, nblk, Q_BLOCK, MLA_HEADS, QK_DIM).transpose(1, 0, 2, 3, 4)
    scale = QK_DIM ** -0.5

    def one_block(qi):
        s = jnp.einsum('bqhd,bkhd->bhqk', qi, k, preferred_element_type=jnp.float32) * scale
        p = jax.nn.softmax(s, axis=-1)
        o = jnp.einsum('bhqk,bkhd->bqhd', p.astype(v.dtype), v, preferred_element_type=jnp.float32)
        return o.astype(v.dtype)

    o = lax.map(one_block, qb)
    return o.transpose(1, 0, 2, 3, 4).reshape(b, n, MLA_HEADS * V_DIM)


def _sgu(u_in, v_in, norm_g, norm_b, w_s, b_s):
    u = jax.nn.gelu(u_in, approximate=False)
    v = jax.nn.gelu(v_in, approximate=False)
    vf = v.astype(jnp.float32)
    mu = jnp.mean(vf, axis=-1, keepdims=True)
    var = jnp.mean(jnp.square(vf - mu), axis=-1, keepdims=True)
    vn = ((vf - mu) * lax.rsqrt(var + EPS) * norm_g.astype(jnp.float32)
          + norm_b.astype(jnp.float32)).astype(v.dtype)
    b, n = v.shape[0], v.shape[1]
    nc = n // CHUNK
    vs = vn.reshape(b, nc, CHUNK, SGU_GROUPS, SGU_GROUP_DIM)
    mixed = jnp.einsum('gij,bnjgc->bnigc', w_s, vs) + b_s.T[:, :, None]
    out = u.reshape(b, nc, CHUNK, SGU_GROUPS, SGU_GROUP_DIM) * mixed
    return out.reshape(b, n, SGU_WIDTH)


def _merge(attn_o, sgu_o, gate_in, w_br_attn, w_br_sgu, w_out):
    g = jax.nn.sigmoid(gate_in.astype(jnp.float32)).astype(attn_o.dtype)
    merged = g[..., :D_MODEL] * (attn_o @ w_br_attn) + g[..., D_MODEL:] * (sgu_o @ w_br_sgu)
    return merged @ w_out


def _swiglu(h, w_ffn_in, w_ffn_out):
    a, b = jnp.split(h @ w_ffn_in, 2, axis=-1)
    return (jax.nn.silu(a) * b) @ w_ffn_out


def _fwd_setup_inputs(seed: int = 0) -> dict:
    key = jax.random.key(seed)
    ks = jax.random.split(key, 26)
    f32 = jnp.float32

    def dense(k, shape, fan_in, gain=1.0):
        return jax.random.normal(k, shape, f32) * (gain * fan_in ** -0.5)

    def gain_vec(k, shape):
        return 1.0 + 0.02 * jax.random.normal(k, shape, f32)

    def bias_vec(k, shape):
        return 0.02 * jax.random.normal(k, shape, f32)

    L = DEPTH
    return {
        "x": jax.random.normal(ks[0], (BATCH, SEQ, D_MODEL), f32),
        "c": jax.random.normal(ks[1], (BATCH, D_MODEL), f32),
        "ctx": jax.random.normal(ks[2], (BATCH, CTX_LEN, D_MODEL), f32),
        "c_ctx": jax.random.normal(ks[3], (D_MODEL,), f32),
        "w_mod": dense(ks[4], (L, D_MODEL, N_MOD * D_MODEL), D_MODEL, 0.5),
        "b_mod": bias_vec(ks[5], (L, N_MOD * D_MODEL)),
        "norm1_g": gain_vec(ks[6], (L, D_MODEL)),
        "w_in": dense(ks[7], (L, D_MODEL, IN_COLS), D_MODEL),
        "q_norm_g": gain_vec(ks[8], (L, Q_LORA)),
        "kv_norm_g": gain_vec(ks[9], (L, KV_LORA)),
        "w_uq": dense(ks[10], (L, Q_LORA, MLA_HEADS * QK_DIM), Q_LORA),
        "w_ukv": dense(ks[11], (L, KV_LORA, MLA_HEADS * (NOPE_DIM + V_DIM)), KV_LORA),
        "qk_norm_q": gain_vec(ks[12], (L, QK_DIM)),
        "qk_norm_k": gain_vec(ks[13], (L, QK_DIM)),
        "sgu_norm_g": gain_vec(ks[14], (L, SGU_WIDTH)),
        "sgu_norm_b": bias_vec(ks[15], (L, SGU_WIDTH)),
        "w_spatial": dense(ks[16], (L, SGU_GROUPS, CHUNK, CHUNK), CHUNK),
        "b_spatial": gain_vec(ks[17], (L, SGU_GROUPS, CHUNK)),
        "w_br_attn": dense(ks[18], (L, MLA_HEADS * V_DIM, D_MODEL), MLA_HEADS * V_DIM),
        "w_br_sgu": dense(ks[19], (L, SGU_WIDTH, D_MODEL), SGU_WIDTH),
        "w_out": dense(ks[20], (L, D_MODEL, D_MODEL), D_MODEL),
        "norm2_g": gain_vec(ks[21], (L, D_MODEL)),
        "w_ffn_in": dense(ks[22], (L, D_MODEL, 2 * D_FF), D_MODEL),
        "w_ffn_out": dense(ks[23], (L, D_FF, D_MODEL), D_FF),
    }


def _fwd_reference(x, c, ctx, c_ctx, w_mod, b_mod, norm1_g, w_in, q_norm_g, kv_norm_g, w_uq,
              w_ukv, qk_norm_q, qk_norm_k, sgu_norm_g, sgu_norm_b, w_spatial, b_spatial,
              w_br_attn, w_br_sgu, w_out, norm2_g, w_ffn_in, w_ffn_out):
    n = x.shape[1]
    ang_r, ang_c = _axial_angles(n)
    silu_c = jax.nn.silu(c)
    silu_cc = jax.nn.silu(c_ctx)
    for l in range(DEPTH):
        mod = silu_c @ w_mod[l] + b_mod[l]
        sh1, sc1, g1, sh2, sc2, g2 = [m[:, None, :] for m in jnp.split(mod, N_MOD, axis=-1)]
        mod_c = silu_cc @ w_mod[l][:, :2 * D_MODEL] + b_mod[l][:2 * D_MODEL]
        sh1c, sc1c = jnp.split(mod_c, 2)
        ctx_h = _modulate(_rms(ctx, norm1_g[l]), sh1c, sc1c)
        k_ctx, v_ctx = _mla_keys_values(ctx_h @ w_in[l][:, OFF_KVC:OFF_U],
                                        kv_norm_g[l], w_ukv[l], qk_norm_k[l])

        h = _modulate(_rms(x, norm1_g[l]), sh1, sc1)
        proj = h @ w_in[l]
        q = _rope_tail(_mla_queries(proj[..., :OFF_KVC], q_norm_g[l], w_uq[l], qk_norm_q[l]),
                       ang_r, ang_c)
        k_lat, v_lat = _mla_keys_values(proj[..., OFF_KVC:OFF_U], kv_norm_g[l], w_ukv[l],
                                        qk_norm_k[l])
        k_lat = _rope_tail(k_lat, ang_r, ang_c)
        attn_o = _attention(q, jnp.concatenate([k_lat, k_ctx], axis=1),
                            jnp.concatenate([v_lat, v_ctx], axis=1))
        sgu_o = _sgu(proj[..., OFF_U:OFF_V], proj[..., OFF_V:OFF_GATE], sgu_norm_g[l],
                     sgu_norm_b[l], w_spatial[l], b_spatial[l])
        x_new = x + g1 * _merge(attn_o, sgu_o, proj[..., OFF_GATE:], w_br_attn[l],
                                w_br_sgu[l], w_out[l])
        h2 = _modulate(_rms(x_new, norm2_g[l]), sh2, sc2)
        x_new = x_new + g2 * _swiglu(h2, w_ffn_in[l], w_ffn_out[l])

        if l + 1 < DEPTH:
            mod_r = silu_cc @ w_mod[l][:, 2 * D_MODEL:] + b_mod[l][2 * D_MODEL:]
            g1c, sh2c, sc2c, g2c = jnp.split(mod_r, 4)
            proj_c = ctx_h @ w_in[l]
            q_c = _mla_queries(proj_c[..., :OFF_KVC], q_norm_g[l], w_uq[l], qk_norm_q[l])
            attn_c = _attention(q_c, k_ctx, v_ctx)
            sgu_c = _sgu(proj_c[..., OFF_U:OFF_V], proj_c[..., OFF_V:OFF_GATE], sgu_norm_g[l],
                         sgu_norm_b[l], w_spatial[l], b_spatial[l])
            ctx = ctx + g1c * _merge(attn_c, sgu_c, proj_c[..., OFF_GATE:], w_br_attn[l],
                                     w_br_sgu[l], w_out[l])
            ctx = ctx + g2c * _swiglu(_modulate(_rms(ctx, norm2_g[l]), sh2c, sc2c),
                                      w_ffn_in[l], w_ffn_out[l])
        x = x_new
    return x


import jax as _jax
import jax.numpy as _jnp

TWIN_FORMAT = 'train_step'
FWD_PARAMS = ['x', 'c', 'ctx', 'c_ctx', 'w_mod', 'b_mod', 'norm1_g', 'w_in', 'q_norm_g', 'kv_norm_g', 'w_uq', 'w_ukv', 'qk_norm_q', 'qk_norm_k', 'sgu_norm_g', 'sgu_norm_b', 'w_spatial', 'b_spatial', 'w_br_attn', 'w_br_sgu', 'w_out', 'norm2_g', 'w_ffn_in', 'w_ffn_out']
TWIN_WEIGHTS = ['c_ctx', 'w_mod', 'b_mod', 'norm1_g', 'w_in', 'q_norm_g', 'kv_norm_g', 'w_uq', 'w_ukv', 'qk_norm_q', 'qk_norm_k', 'sgu_norm_g', 'sgu_norm_b', 'w_spatial', 'b_spatial', 'w_br_attn', 'w_br_sgu', 'w_out', 'norm2_g', 'w_ffn_in', 'w_ffn_out']
TWIN_DIFF_INPUT = 'x'
TWIN_INPUTS = ['x', 'c', 'ctx', 'c_ctx', 'w_mod', 'b_mod', 'norm1_g', 'w_in', 'q_norm_g', 'kv_norm_g', 'w_uq', 'w_ukv', 'qk_norm_q', 'qk_norm_k', 'sgu_norm_g', 'sgu_norm_b', 'w_spatial', 'b_spatial', 'w_br_attn', 'w_br_sgu', 'w_out', 'norm2_g', 'w_ffn_in', 'w_ffn_out', 'loss_target', 'm_c_ctx', 'm_w_mod', 'm_b_mod', 'm_norm1_g', 'm_w_in', 'm_q_norm_g', 'm_kv_norm_g', 'm_w_uq', 'm_w_ukv', 'm_qk_norm_q', 'm_qk_norm_k', 'm_sgu_norm_g', 'm_sgu_norm_b', 'm_w_spatial', 'm_b_spatial', 'm_w_br_attn', 'm_w_br_sgu', 'm_w_out', 'm_norm2_g', 'm_w_ffn_in', 'm_w_ffn_out', 'v_c_ctx', 'v_w_mod', 'v_b_mod', 'v_norm1_g', 'v_w_in', 'v_q_norm_g', 'v_kv_norm_g', 'v_w_uq', 'v_w_ukv', 'v_qk_norm_q', 'v_qk_norm_k', 'v_sgu_norm_g', 'v_sgu_norm_b', 'v_w_spatial', 'v_b_spatial', 'v_w_br_attn', 'v_w_br_sgu', 'v_w_out', 'v_norm2_g', 'v_w_ffn_in', 'v_w_ffn_out']
TWIN_OUTPUTS = ['loss', 'grad_x', 'grad_c_ctx', 'grad_w_mod', 'grad_b_mod', 'grad_norm1_g', 'grad_w_in', 'grad_q_norm_g', 'grad_kv_norm_g', 'grad_w_uq', 'grad_w_ukv', 'grad_qk_norm_q', 'grad_qk_norm_k', 'grad_sgu_norm_g', 'grad_sgu_norm_b', 'grad_w_spatial', 'grad_b_spatial', 'grad_w_br_attn', 'grad_w_br_sgu', 'grad_w_out', 'grad_norm2_g', 'grad_w_ffn_in', 'grad_w_ffn_out', 'delta_c_ctx', 'delta_w_mod', 'delta_b_mod', 'delta_norm1_g', 'delta_w_in', 'delta_q_norm_g', 'delta_kv_norm_g', 'delta_w_uq', 'delta_w_ukv', 'delta_qk_norm_q', 'delta_qk_norm_k', 'delta_sgu_norm_g', 'delta_sgu_norm_b', 'delta_w_spatial', 'delta_b_spatial', 'delta_w_br_attn', 'delta_w_br_sgu', 'delta_w_out', 'delta_norm2_g', 'delta_w_ffn_in', 'delta_w_ffn_out', 'new_m_c_ctx', 'new_m_w_mod', 'new_m_b_mod', 'new_m_norm1_g', 'new_m_w_in', 'new_m_q_norm_g', 'new_m_kv_norm_g', 'new_m_w_uq', 'new_m_w_ukv', 'new_m_qk_norm_q', 'new_m_qk_norm_k', 'new_m_sgu_norm_g', 'new_m_sgu_norm_b', 'new_m_w_spatial', 'new_m_b_spatial', 'new_m_w_br_attn', 'new_m_w_br_sgu', 'new_m_w_out', 'new_m_norm2_g', 'new_m_w_ffn_in', 'new_m_w_ffn_out', 'new_v_c_ctx', 'new_v_w_mod', 'new_v_b_mod', 'new_v_norm1_g', 'new_v_w_in', 'new_v_q_norm_g', 'new_v_kv_norm_g', 'new_v_w_uq', 'new_v_w_ukv', 'new_v_qk_norm_q', 'new_v_qk_norm_k', 'new_v_sgu_norm_g', 'new_v_sgu_norm_b', 'new_v_w_spatial', 'new_v_b_spatial', 'new_v_w_br_attn', 'new_v_w_br_sgu', 'new_v_w_out', 'new_v_norm2_g', 'new_v_w_ffn_in', 'new_v_w_ffn_out']
TWIN_LEAF_KINDS = {'loss': 'loss', 'grad_x': 'grad_x', 'grad_c_ctx': 'grad_w', 'grad_w_mod': 'grad_w', 'grad_b_mod': 'grad_w', 'grad_norm1_g': 'grad_w', 'grad_w_in': 'grad_w', 'grad_q_norm_g': 'grad_w', 'grad_kv_norm_g': 'grad_w', 'grad_w_uq': 'grad_w', 'grad_w_ukv': 'grad_w', 'grad_qk_norm_q': 'grad_w', 'grad_qk_norm_k': 'grad_w', 'grad_sgu_norm_g': 'grad_w', 'grad_sgu_norm_b': 'grad_w', 'grad_w_spatial': 'grad_w', 'grad_b_spatial': 'grad_w', 'grad_w_br_attn': 'grad_w', 'grad_w_br_sgu': 'grad_w', 'grad_w_out': 'grad_w', 'grad_norm2_g': 'grad_w', 'grad_w_ffn_in': 'grad_w', 'grad_w_ffn_out': 'grad_w', 'delta_c_ctx': 'delta_w', 'delta_w_mod': 'delta_w', 'delta_b_mod': 'delta_w', 'delta_norm1_g': 'delta_w', 'delta_w_in': 'delta_w', 'delta_q_norm_g': 'delta_w', 'delta_kv_norm_g': 'delta_w', 'delta_w_uq': 'delta_w', 'delta_w_ukv': 'delta_w', 'delta_qk_norm_q': 'delta_w', 'delta_qk_norm_k': 'delta_w', 'delta_sgu_norm_g': 'delta_w', 'delta_sgu_norm_b': 'delta_w', 'delta_w_spatial': 'delta_w', 'delta_b_spatial': 'delta_w', 'delta_w_br_attn': 'delta_w', 'delta_w_br_sgu': 'delta_w', 'delta_w_out': 'delta_w', 'delta_norm2_g': 'delta_w', 'delta_w_ffn_in': 'delta_w', 'delta_w_ffn_out': 'delta_w', 'new_m_c_ctx': 'new_m', 'new_m_w_mod': 'new_m', 'new_m_b_mod': 'new_m', 'new_m_norm1_g': 'new_m', 'new_m_w_in': 'new_m', 'new_m_q_norm_g': 'new_m', 'new_m_kv_norm_g': 'new_m', 'new_m_w_uq': 'new_m', 'new_m_w_ukv': 'new_m', 'new_m_qk_norm_q': 'new_m', 'new_m_qk_norm_k': 'new_m', 'new_m_sgu_norm_g': 'new_m', 'new_m_sgu_norm_b': 'new_m', 'new_m_w_spatial': 'new_m', 'new_m_b_spatial': 'new_m', 'new_m_w_br_attn': 'new_m', 'new_m_w_br_sgu': 'new_m', 'new_m_w_out': 'new_m', 'new_m_norm2_g': 'new_m', 'new_m_w_ffn_in': 'new_m', 'new_m_w_ffn_out': 'new_m', 'new_v_c_ctx': 'new_v', 'new_v_w_mod': 'new_v', 'new_v_b_mod': 'new_v', 'new_v_norm1_g': 'new_v', 'new_v_w_in': 'new_v', 'new_v_q_norm_g': 'new_v', 'new_v_kv_norm_g': 'new_v', 'new_v_w_uq': 'new_v', 'new_v_w_ukv': 'new_v', 'new_v_qk_norm_q': 'new_v', 'new_v_qk_norm_k': 'new_v', 'new_v_sgu_norm_g': 'new_v', 'new_v_sgu_norm_b': 'new_v', 'new_v_w_spatial': 'new_v', 'new_v_b_spatial': 'new_v', 'new_v_w_br_attn': 'new_v', 'new_v_w_br_sgu': 'new_v', 'new_v_w_out': 'new_v', 'new_v_norm2_g': 'new_v', 'new_v_w_ffn_in': 'new_v', 'new_v_w_ffn_out': 'new_v'}


def _forward(args):
    return _fwd_reference(*[args[k] for k in FWD_PARAMS])


def _output_shape():
    def fwd():
        inp = _fwd_setup_inputs(0)
        return _fwd_reference(*[inp[k] for k in FWD_PARAMS])
    out = _jax.eval_shape(fwd)
    return out.shape, out.dtype

N_MICROBATCH = 1
ADAM_LR = 0.001
ADAM_B1 = 0.9
ADAM_B2 = 0.999
ADAM_EPS = 1e-08
ADAM_WD = 0.01
ADAM_STEP = 10
PER_EXAMPLE_BATCH_AXIS = {'x': 0, 'c': 0, 'ctx': 0, 'loss_target': 0}
SHARED_INPUTS = []
_WEIGHT_DTYPES = {'c_ctx': _jnp.float32, 'w_mod': _jnp.float32, 'b_mod': _jnp.float32, 'norm1_g': _jnp.float32, 'w_in': _jnp.float32, 'q_norm_g': _jnp.float32, 'kv_norm_g': _jnp.float32, 'w_uq': _jnp.float32, 'w_ukv': _jnp.float32, 'qk_norm_q': _jnp.float32, 'qk_norm_k': _jnp.float32, 'sgu_norm_g': _jnp.float32, 'sgu_norm_b': _jnp.float32, 'w_spatial': _jnp.float32, 'b_spatial': _jnp.float32, 'w_br_attn': _jnp.float32, 'w_br_sgu': _jnp.float32, 'w_out': _jnp.float32, 'norm2_g': _jnp.float32, 'w_ffn_in': _jnp.float32, 'w_ffn_out': _jnp.float32}
MOMENT_SCALE = {'c_ctx': 1.095089e-02, 'w_mod': 3.133159e-01, 'b_mod': 8.380354e-01, 'norm1_g': 4.820562e-01, 'w_in': 4.124905e-02, 'q_norm_g': 4.944901e-03, 'kv_norm_g': 2.475635e-01, 'w_uq': 2.679535e-03, 'w_ukv': 2.776887e-02, 'qk_norm_q': 1.153813e-02, 'qk_norm_k': 1.156467e-02, 'sgu_norm_g': 4.522472e-01, 'sgu_norm_b': 7.562135e-02, 'w_spatial': 4.196485e-02, 'b_spatial': 4.192489e-01, 'w_br_attn': 2.711174e-02, 'w_br_sgu': 6.116512e-02, 'w_out': 5.121439e-02, 'norm2_g': 1.506951e+00, 'w_ffn_in': 2.515071e-02, 'w_ffn_out': 3.188434e-02}


def _to_microbatches(a, axis):
    t = _jnp.moveaxis(a, axis, 0)
    t = t.reshape((N_MICROBATCH, t.shape[0] // N_MICROBATCH) + t.shape[1:])
    return _jnp.moveaxis(t, 1, axis + 1)


def setup_inputs(seed: int = 0) -> dict:
    inp = _fwd_setup_inputs(seed)
    key = _jax.random.fold_in(_jax.random.key(seed), 7919)
    shape, _ = _output_shape()
    out = dict(inp)
    out["loss_target"] = _jax.random.normal(_jax.random.fold_in(key, 0), shape, _jnp.float32)
    for i, name in enumerate(TWIN_WEIGHTS):
        w = inp[name].astype(_jnp.float32)
        if MOMENT_SCALE is None:
            s = _jnp.sqrt(_jnp.mean(_jnp.square(w)) + 1e-30)
        else:
            s = MOMENT_SCALE[name]
        km, kv = _jax.random.split(_jax.random.fold_in(key, i + 1))
        out[name] = w
        out["m_" + name] = s * _jax.random.normal(km, w.shape, _jnp.float32)
        out["v_" + name] = (s * s) * _jax.random.uniform(kv, w.shape, _jnp.float32, 0.5, 1.5)
    if N_MICROBATCH > 1:
        for name, axis in PER_EXAMPLE_BATCH_AXIS.items():
            out[name] = _to_microbatches(out[name], axis)
    return {'x': out['x'], 'c': out['c'], 'ctx': out['ctx'], 'c_ctx': out['c_ctx'], 'w_mod': out['w_mod'], 'b_mod': out['b_mod'], 'norm1_g': out['norm1_g'], 'w_in': out['w_in'], 'q_norm_g': out['q_norm_g'], 'kv_norm_g': out['kv_norm_g'], 'w_uq': out['w_uq'], 'w_ukv': out['w_ukv'], 'qk_norm_q': out['qk_norm_q'], 'qk_norm_k': out['qk_norm_k'], 'sgu_norm_g': out['sgu_norm_g'], 'sgu_norm_b': out['sgu_norm_b'], 'w_spatial': out['w_spatial'], 'b_spatial': out['b_spatial'], 'w_br_attn': out['w_br_attn'], 'w_br_sgu': out['w_br_sgu'], 'w_out': out['w_out'], 'norm2_g': out['norm2_g'], 'w_ffn_in': out['w_ffn_in'], 'w_ffn_out': out['w_ffn_out'], 'loss_target': out['loss_target'], 'm_c_ctx': out['m_c_ctx'], 'm_w_mod': out['m_w_mod'], 'm_b_mod': out['m_b_mod'], 'm_norm1_g': out['m_norm1_g'], 'm_w_in': out['m_w_in'], 'm_q_norm_g': out['m_q_norm_g'], 'm_kv_norm_g': out['m_kv_norm_g'], 'm_w_uq': out['m_w_uq'], 'm_w_ukv': out['m_w_ukv'], 'm_qk_norm_q': out['m_qk_norm_q'], 'm_qk_norm_k': out['m_qk_norm_k'], 'm_sgu_norm_g': out['m_sgu_norm_g'], 'm_sgu_norm_b': out['m_sgu_norm_b'], 'm_w_spatial': out['m_w_spatial'], 'm_b_spatial': out['m_b_spatial'], 'm_w_br_attn': out['m_w_br_attn'], 'm_w_br_sgu': out['m_w_br_sgu'], 'm_w_out': out['m_w_out'], 'm_norm2_g': out['m_norm2_g'], 'm_w_ffn_in': out['m_w_ffn_in'], 'm_w_ffn_out': out['m_w_ffn_out'], 'v_c_ctx': out['v_c_ctx'], 'v_w_mod': out['v_w_mod'], 'v_b_mod': out['v_b_mod'], 'v_norm1_g': out['v_norm1_g'], 'v_w_in': out['v_w_in'], 'v_q_norm_g': out['v_q_norm_g'], 'v_kv_norm_g': out['v_kv_norm_g'], 'v_w_uq': out['v_w_uq'], 'v_w_ukv': out['v_w_ukv'], 'v_qk_norm_q': out['v_qk_norm_q'], 'v_qk_norm_k': out['v_qk_norm_k'], 'v_sgu_norm_g': out['v_sgu_norm_g'], 'v_sgu_norm_b': out['v_sgu_norm_b'], 'v_w_spatial': out['v_w_spatial'], 'v_b_spatial': out['v_b_spatial'], 'v_w_br_attn': out['v_w_br_attn'], 'v_w_br_sgu': out['v_w_br_sgu'], 'v_w_out': out['v_w_out'], 'v_norm2_g': out['v_norm2_g'], 'v_w_ffn_in': out['v_w_ffn_in'], 'v_w_ffn_out': out['v_w_ffn_out']}


def _loss(weights, diff, rest, loss_target):
    with _jax.named_scope("forward"):
        args = {**rest, TWIN_DIFF_INPUT: diff, **{k: w.astype(_WEIGHT_DTYPES[k]) for k, w in weights.items()}}
        y = _forward(args)
    with _jax.named_scope("loss_head"):
        err = _jnp.square(y.astype(_jnp.float32) - loss_target)
        return 0.5 * _jnp.sum(_jnp.mean(err, axis=-1)) if err.ndim else 0.5 * err


def _adamw(w, g, m, v):
    m = ADAM_B1 * m + (1.0 - ADAM_B1) * g
    v = ADAM_B2 * v + (1.0 - ADAM_B2) * _jnp.square(g)
    m_hat = m / (1.0 - ADAM_B1 ** ADAM_STEP)
    v_hat = v / (1.0 - ADAM_B2 ** ADAM_STEP)
    delta = -ADAM_LR * (m_hat / (_jnp.sqrt(v_hat) + ADAM_EPS) + ADAM_WD * w)
    return delta, m, v


def reference(x, c, ctx, c_ctx, w_mod, b_mod, norm1_g, w_in, q_norm_g, kv_norm_g, w_uq, w_ukv, qk_norm_q, qk_norm_k, sgu_norm_g, sgu_norm_b, w_spatial, b_spatial, w_br_attn, w_br_sgu, w_out, norm2_g, w_ffn_in, w_ffn_out, loss_target, m_c_ctx, m_w_mod, m_b_mod, m_norm1_g, m_w_in, m_q_norm_g, m_kv_norm_g, m_w_uq, m_w_ukv, m_qk_norm_q, m_qk_norm_k, m_sgu_norm_g, m_sgu_norm_b, m_w_spatial, m_b_spatial, m_w_br_attn, m_w_br_sgu, m_w_out, m_norm2_g, m_w_ffn_in, m_w_ffn_out, v_c_ctx, v_w_mod, v_b_mod, v_norm1_g, v_w_in, v_q_norm_g, v_kv_norm_g, v_w_uq, v_w_ukv, v_qk_norm_q, v_qk_norm_k, v_sgu_norm_g, v_sgu_norm_b, v_w_spatial, v_b_spatial, v_w_br_attn, v_w_br_sgu, v_w_out, v_norm2_g, v_w_ffn_in, v_w_ffn_out):
    given = dict(x=x, c=c, ctx=ctx, c_ctx=c_ctx, w_mod=w_mod, b_mod=b_mod, norm1_g=norm1_g, w_in=w_in, q_norm_g=q_norm_g, kv_norm_g=kv_norm_g, w_uq=w_uq, w_ukv=w_ukv, qk_norm_q=qk_norm_q, qk_norm_k=qk_norm_k, sgu_norm_g=sgu_norm_g, sgu_norm_b=sgu_norm_b, w_spatial=w_spatial, b_spatial=b_spatial, w_br_attn=w_br_attn, w_br_sgu=w_br_sgu, w_out=w_out, norm2_g=norm2_g, w_ffn_in=w_ffn_in, w_ffn_out=w_ffn_out, loss_target=loss_target, m_c_ctx=m_c_ctx, m_w_mod=m_w_mod, m_b_mod=m_b_mod, m_norm1_g=m_norm1_g, m_w_in=m_w_in, m_q_norm_g=m_q_norm_g, m_kv_norm_g=m_kv_norm_g, m_w_uq=m_w_uq, m_w_ukv=m_w_ukv, m_qk_norm_q=m_qk_norm_q, m_qk_norm_k=m_qk_norm_k, m_sgu_norm_g=m_sgu_norm_g, m_sgu_norm_b=m_sgu_norm_b, m_w_spatial=m_w_spatial, m_b_spatial=m_b_spatial, m_w_br_attn=m_w_br_attn, m_w_br_sgu=m_w_br_sgu, m_w_out=m_w_out, m_norm2_g=m_norm2_g, m_w_ffn_in=m_w_ffn_in, m_w_ffn_out=m_w_ffn_out, v_c_ctx=v_c_ctx, v_w_mod=v_w_mod, v_b_mod=v_b_mod, v_norm1_g=v_norm1_g, v_w_in=v_w_in, v_q_norm_g=v_q_norm_g, v_kv_norm_g=v_kv_norm_g, v_w_uq=v_w_uq, v_w_ukv=v_w_ukv, v_qk_norm_q=v_qk_norm_q, v_qk_norm_k=v_qk_norm_k, v_sgu_norm_g=v_sgu_norm_g, v_sgu_norm_b=v_sgu_norm_b, v_w_spatial=v_w_spatial, v_b_spatial=v_b_spatial, v_w_br_attn=v_w_br_attn, v_w_br_sgu=v_w_br_sgu, v_w_out=v_w_out, v_norm2_g=v_norm2_g, v_w_ffn_in=v_w_ffn_in, v_w_ffn_out=v_w_ffn_out)
    weights = {n: given[n] for n in TWIN_WEIGHTS}
    shared = {n: given[n] for n in SHARED_INPUTS}
    per_example = {n: given[n] for n in ['x', 'c', 'ctx']}
    grad_fn = _jax.value_and_grad(_loss, argnums=(0, 1))

    def one_microbatch(ex, loss_target):
        ex = dict(ex)
        diff = ex.pop(TWIN_DIFF_INPUT)
        return grad_fn(weights, diff, {**shared, **ex}, loss_target)

    if N_MICROBATCH == 1:
        loss, (grad_w, grad_x) = one_microbatch(per_example, given["loss_target"])
    else:
        def body(carry, xs):
            loss_sum, grad_sum = carry
            l_k, (gw_k, gx_k) = one_microbatch(xs[0], xs[1])
            with _jax.named_scope("update"):
                return (loss_sum + l_k, _jax.tree.map(_jnp.add, grad_sum, gw_k)), gx_k

        init = (_jnp.zeros((), _jnp.float32), _jax.tree.map(_jnp.zeros_like, weights))
        (loss, grad_w), grad_x = _jax.lax.scan(body, init, (per_example, given["loss_target"]))
    with _jax.named_scope("update"):
        delta_w, new_m, new_v = {}, {}, {}
        for n in TWIN_WEIGHTS:
            delta_w[n], new_m[n], new_v[n] = _adamw(weights[n], grad_w[n], given["m_" + n], given["v_" + n])
    return (loss, grad_x, *[grad_w[n] for n in TWIN_WEIGHTS], *[delta_w[n] for n in TWIN_WEIGHTS],
            *[new_m[n] for n in TWIN_WEIGHTS], *[new_v[n] for n in TWIN_WEIGHTS])
```

```python
import math

import jax
import jax.numpy as jnp
from jax import lax
from jax.experimental import pallas as pl
from jax.experimental.pallas import tpu as pltpu

F32, BF16 = jnp.float32, jnp.bfloat16
MESH = pl.DeviceIdType.MESH

LANES = 128
BF16_SUBLANES = 16
VMEM_LIMIT_BYTES = 56 * 1024 * 1024

EPS = 1e-6
ROPE_DIM = 64
ROPE_THETA = 10000.0
GRID_W = 64
HEAD_PAD = 256
ADAM_LR, ADAM_B1, ADAM_B2, ADAM_EPS, ADAM_WD, ADAM_STEP = 0.001, 0.9, 0.999, 1e-08, 0.01, 10


def _tile(dim, pref, align=LANES):
    if dim <= pref:
        return dim
    t = (pref // align) * align
    while t >= align:
        if dim % t == 0:
            return t
        t -= align
    return dim


def _params(sem=None):
    return pltpu.CompilerParams(dimension_semantics=sem, vmem_limit_bytes=VMEM_LIMIT_BYTES)


def _sds(shape, dtype):
    return jax.ShapeDtypeStruct(tuple(shape), dtype)


def _mm(pairs, *, name, ta=False, tb=False, outs=(F32,), tm=1024, tn=1024, tk=512, extras=(), epi=None):
    dual = len(pairs[0]) == 3
    a0, b0 = pairs[0][0], pairs[0][1]
    M = a0.shape[1] if ta else a0.shape[0]
    N = b0.shape[0] if tb else b0.shape[1]
    tm, tn = _tile(M, tm), _tile(N, tn)
    ks = [(p[0].shape[0] if ta else p[0].shape[1]) for p in pairs]
    tks = [_tile(k, tk) for k in ks]
    nks = [k // t for k, t in zip(ks, tks)]
    offs = [sum(nks[:i]) for i in range(len(pairs))]
    nk_total = sum(nks)
    single = len(pairs) == 1

    def kidx(kk, p):
        return kk if single else jnp.clip(kk - offs[p], 0, nks[p] - 1)

    in_specs, operands = [], []
    for p, pr in enumerate(pairs):
        if ta:
            in_specs.append(pl.BlockSpec((tks[p], tm), lambda i, j, kk, p=p: (kidx(kk, p), i)))
        else:
            in_specs.append(pl.BlockSpec((tm, tks[p]), lambda i, j, kk, p=p: (i, kidx(kk, p))))
        operands.append(pr[0])
        for b in pr[1:]:
            if tb:
                in_specs.append(pl.BlockSpec((tn, tks[p]), lambda i, j, kk, p=p: (j, kidx(kk, p))))
            else:
                in_specs.append(pl.BlockSpec((tks[p], tn), lambda i, j, kk, p=p: (kidx(kk, p), j)))
            operands.append(b)
    for arr, kind in extras:
        if kind == "mn":
            in_specs.append(pl.BlockSpec((tm, tn), lambda i, j, kk: (i, j)))
        else:
            in_specs.append(pl.BlockSpec((1, tn), lambda i, j, kk: (0, j)))
        operands.append(arr)
    n_in = len(operands)
    n_ex = len(extras)
    per = 3 if dual else 2
    dims = (((0 if ta else 1,), (1 if tb else 0,)), ((), ()))

    def body(*refs):
        ins, out_refs, accs = refs[:n_in], refs[n_in:n_in + len(outs)], refs[n_in + len(outs):]
        kk = pl.program_id(2)

        @pl.when(kk == 0)
        def _():
            for acc in accs:
                acc[...] = jnp.zeros_like(acc)

        for p in range(len(pairs)):
            def step(p=p):
                a = ins[per * p][...].astype(BF16)
                for q, acc in enumerate(accs):
                    b = ins[per * p + 1 + q][...].astype(BF16)
                    acc[...] += lax.dot_general(a, b, dims, preferred_element_type=F32)
            if single:
                step()
            else:
                pl.when((kk >= offs[p]) & (kk < offs[p] + nks[p]))(step)

        @pl.when(kk == nk_total - 1)
        def _():
            vals = [acc[...] for acc in accs] + [r[...] for r in ins[n_in - n_ex:]]
            res = epi(*vals) if epi is not None else (vals[0],)
            for o, r in zip(out_refs, res):
                o[...] = r.astype(o.dtype)

    res = pl.pallas_call(
        body, name=name, grid=(M // tm, N // tn, nk_total), in_specs=in_specs,
        out_specs=[pl.BlockSpec((tm, tn), lambda i, j, kk: (i, j)) for _ in outs],
        out_shape=[_sds((M, N), d) for d in outs],
        scratch_shapes=[pltpu.VMEM((tm, tn), F32) for _ in range(2 if dual else 1)],
        compiler_params=_params(("parallel", "parallel", "arbitrary")),
    )(*operands)
    return res[0] if len(outs) == 1 else res


def _rowwise(fn, rows, vecs, out_rows, out_accs=(), *, name, tm=256, tc=None):
    M = rows[0].shape[0]
    tm = _tile(M, tm, BF16_SUBLANES)
    nrow = M // tm
    C = rows[0].shape[1]
    ncol = 1 if tc is None else C // _tile(C, tc)
    tcol = None if tc is None else _tile(C, tc)

    def colwise(shape):
        return tc is not None and len(shape) == 2 and shape[0] == 1 and shape[1] == C

    def vspec(shape):
        if colwise(shape):
            return pl.BlockSpec((1, tcol), lambda j, i: (0, j))
        return pl.BlockSpec(tuple(shape), lambda j, i, n=len(shape): (0,) * n)

    def rspec(width):
        if tc is None:
            return pl.BlockSpec((tm, width), lambda j, i: (i, 0))
        return pl.BlockSpec((tm, tcol), lambda j, i: (i, j))

    in_specs = [rspec(r.shape[1]) for r in rows] + [vspec(v.shape) for v in vecs]
    out_specs = [rspec(c) for c, _ in out_rows] + [vspec(s) for s in out_accs]
    out_shape = [_sds((M, c), d) for c, d in out_rows] + [_sds(s, F32) for s in out_accs]
    n_in, n_or = len(rows) + len(vecs), len(out_rows)

    def body(*refs):
        ins, o_rows, o_accs = refs[:n_in], refs[n_in:n_in + n_or], refs[n_in + n_or:]
        r_out, a_out = fn(*[r[...] for r in ins])
        for o, r in zip(o_rows, r_out):
            o[...] = r.astype(o.dtype)
        i = pl.program_id(1)

        @pl.when(i == 0)
        def _():
            for o, a in zip(o_accs, a_out):
                o[...] = a

        @pl.when(i > 0)
        def _():
            for o, a in zip(o_accs, a_out):
                o[...] += a

    res = pl.pallas_call(
        body, name=name, grid=(ncol, nrow), in_specs=in_specs, out_specs=out_specs, out_shape=out_shape,
        compiler_params=_params(("parallel", "arbitrary")),
    )(*rows, *vecs)
    return res[:n_or], res[n_or:]


def _colsum(t):
    return jnp.sum(t, axis=0, keepdims=True)


def _gelu(t):
    return 0.5 * t * (1.0 + lax.erf(t * math.sqrt(0.5)))


def _gelu_grad(t):
    return 0.5 * (1.0 + lax.erf(t * math.sqrt(0.5))) + t * jnp.exp(-0.5 * t * t) * (1.0 / math.sqrt(2.0 * math.pi))


def _sigmoid(t):
    return 1.0 / (1.0 + jnp.exp(-t))


def _rms_stats(t, width):
    return lax.rsqrt(jnp.sum(t * t, axis=-1, keepdims=True) * (1.0 / width) + EPS)


def _rms_bwd(dn, tn, r, width):
    return r * (dn - tn * (jnp.sum(dn * tn, axis=-1, keepdims=True) * (1.0 / width)))


def _attn_fwd(q, k, v, *, heads, scale, tq=256):
    N, M = q.shape[0], k.shape[0]
    tq = _tile(N, tq)
    vd = v.shape[1] // heads

    def body(q_ref, k_ref, v_ref, o_ref, lse_ref):
        s = lax.dot_general(q_ref[...], k_ref[...], (((1,), (1,)), ((), ())), preferred_element_type=F32) * scale
        m = jnp.max(s, axis=-1, keepdims=True)
        p = jnp.exp(s - m)
        l = jnp.sum(p, axis=-1, keepdims=True)
        o = jnp.dot(p.astype(BF16), v_ref[...], preferred_element_type=F32) / l
        o_ref[...] = o.astype(o_ref.dtype)
        lse_ref[...] = jnp.broadcast_to(m + jnp.log(l), lse_ref.shape)

    return pl.pallas_call(
        body, name="attn_fwd", grid=(heads, N // tq),
        in_specs=[pl.BlockSpec((tq, HEAD_PAD), lambda h, i: (i, h)),
                  pl.BlockSpec((M, HEAD_PAD), lambda h, i: (0, h)),
                  pl.BlockSpec((M, vd), lambda h, i: (0, h))],
        out_specs=[pl.BlockSpec((tq, vd), lambda h, i: (i, h)), pl.BlockSpec((tq, vd), lambda h, i: (i, h))],
        out_shape=[_sds((N, heads * vd), BF16), _sds((N, heads * vd), F32)],
        compiler_params=_params(("parallel", "arbitrary")),
    )(q, k, v)


def _attn_bwd(q, k, v, o, lse, do, *, heads, scale, tq=256):
    N, M = q.shape[0], k.shape[0]
    tq = _tile(N, tq)
    vd = v.shape[1] // heads
    nt = (((1,), (1,)), ((), ()))
    tn = (((0,), (0,)), ((), ()))

    def body(q_ref, k_ref, v_ref, o_ref, lse_ref, do_ref, dq_ref, dk_ref, dv_ref):
        i = pl.program_id(1)
        qb, kb, dob = q_ref[...], k_ref[...], do_ref[...]
        s = lax.dot_general(qb, kb, nt, preferred_element_type=F32) * scale
        p = jnp.exp(s - lse_ref[...][:, :1])
        dp = lax.dot_general(dob, v_ref[...], nt, preferred_element_type=F32)
        delta = jnp.sum(dob.astype(F32) * o_ref[...].astype(F32), axis=-1, keepdims=True)
        ds = (p * (dp - delta) * scale).astype(BF16)
        pb = p.astype(BF16)
        dq_ref[...] = jnp.dot(ds, kb, preferred_element_type=F32)
        dk_part = lax.dot_general(ds, qb, tn, preferred_element_type=F32)
        dv_part = lax.dot_general(pb, dob, tn, preferred_element_type=F32)

        @pl.when(i == 0)
        def _():
            dk_ref[...] = dk_part
            dv_ref[...] = dv_part

        @pl.when(i > 0)
        def _():
            dk_ref[...] += dk_part
            dv_ref[...] += dv_part

    return pl.pallas_call(
        body, name="attn_bwd", grid=(heads, N // tq),
        in_specs=[pl.BlockSpec((tq, HEAD_PAD), lambda h, i: (i, h)),
                  pl.BlockSpec((M, HEAD_PAD), lambda h, i: (0, h)),
                  pl.BlockSpec((M, vd), lambda h, i: (0, h)),
                  pl.BlockSpec((tq, vd), lambda h, i: (i, h)),
                  pl.BlockSpec((tq, vd), lambda h, i: (i, h)),
                  pl.BlockSpec((tq, vd), lambda h, i: (i, h))],
        out_specs=[pl.BlockSpec((tq, HEAD_PAD), lambda h, i: (i, h)),
                   pl.BlockSpec((M, HEAD_PAD), lambda h, i: (0, h)),
                   pl.BlockSpec((M, vd), lambda h, i: (0, h))],
        out_shape=[_sds((N, heads * HEAD_PAD), F32), _sds((M, heads * HEAD_PAD), F32), _sds((M, heads * vd), F32)],
        compiler_params=_params(("parallel", "arbitrary")),
    )(q, k, v, o, lse, do)


def _place():
    return lax.axis_index("x"), lax.axis_index("y"), lax.axis_index("c")


def _comm_call(body, xs, out_shapes, n_sems, name, in_vmem):
    space = pltpu.VMEM if in_vmem else pl.ANY
    n = len(xs)

    def wrapped(*refs):
        body(refs[:n], refs[n:2 * n], *refs[2 * n:])

    return pl.pallas_call(
        wrapped, name=name, out_shape=list(out_shapes),
        in_specs=[pl.BlockSpec(memory_space=space)] * n, out_specs=[pl.BlockSpec(memory_space=space)] * n,
        scratch_shapes=[pltpu.SemaphoreType.DMA((n, n_sems)), pltpu.SemaphoreType.DMA((n, n_sems)),
                        pltpu.SemaphoreType.DMA((n,))],
        compiler_params=pltpu.CompilerParams(vmem_limit_bytes=VMEM_LIMIT_BYTES),
    )(*xs)


def _all_gather8(blks, *, name, in_vmem):
    def body(x_refs, out_refs, send_sems, recv_sems, local_sems):
        x, y, c = _place()
        me, sibling = (x, y, c), (x, y, 1 - c)
        chips = [(1 - x, y), (x, 1 - y), (1 - x, 1 - y)]
        waits = []
        for w, (x_ref, out_ref) in enumerate(zip(x_refs, out_refs)):
            def slot(px, py, pc, out_ref=out_ref):
                return out_ref.at[4 * px + 2 * py + pc]

            def copy(k, block, to, src=None, w=w, slot=slot):
                return pltpu.make_async_remote_copy(
                    src_ref=slot(*block) if src is None else src, dst_ref=slot(*block),
                    send_sem=send_sems.at[w, k], recv_sem=recv_sems.at[w, k], device_id=to, device_id_type=MESH)

            mine = pltpu.make_async_copy(x_ref, slot(*me), local_sems.at[w])
            mine.start()
            first = [copy(0, me, sibling, src=x_ref)]
            first += [copy(1 + j, me, (*chip, c), src=x_ref) for j, chip in enumerate(chips)]
            for cp in first:
                cp.start()
            waits.append((copy, mine, first))
        for copy, mine, first in waits:
            passed = [copy(4 + j, (*chip, c), sibling) for j, chip in enumerate(chips)]
            for j, chip in enumerate(chips):
                copy(1 + j, (*chip, c), me).wait_recv()
                passed[j].start()
            copy(0, sibling, me).wait_recv()
            for j, chip in enumerate(chips):
                copy(4 + j, (*chip, 1 - c), me).wait_recv()
            for cp in first + passed:
                cp.wait_send()
            mine.wait()

    return _comm_call(body, blks, [_sds((8,) + b.shape, b.dtype) for b in blks], 7, name, in_vmem)


def _pair_gather(blks, *, name):
    def body(x_refs, out_refs, send_sems, recv_sems, local_sems):
        x, y, c = _place()
        started = []
        for w, (x_ref, out_ref) in enumerate(zip(x_refs, out_refs)):
            mine = pltpu.make_async_copy(x_ref, out_ref.at[c], local_sems.at[w])
            mine.start()
            cp = pltpu.make_async_remote_copy(src_ref=x_ref, dst_ref=out_ref.at[c], send_sem=send_sems.at[w, 0],
                                              recv_sem=recv_sems.at[w, 0], device_id=(x, y, 1 - c), device_id_type=MESH)
            cp.start()
            started.append((mine, cp))
        for w, (x_ref, out_ref) in enumerate(zip(x_refs, out_refs)):
            mine, cp = started[w]
            pltpu.make_async_remote_copy(src_ref=x_ref, dst_ref=out_ref.at[1 - c], send_sem=send_sems.at[w, 0],
                                         recv_sem=recv_sems.at[w, 0], device_id=(x, y, 1 - c),
                                         device_id_type=MESH).wait_recv()
            cp.wait_send()
            mine.wait()

    return _comm_call(body, blks, [_sds((2,) + b.shape, b.dtype) for b in blks], 1, name, False)


def _pair_swap_other(bufs, *, name):
    def body(x_refs, out_refs, send_sems, recv_sems, local_sems):
        x, y, c = _place()
        for w, (x_ref, out_ref) in enumerate(zip(x_refs, out_refs)):
            for q in range(4):
                pltpu.make_async_remote_copy(src_ref=x_ref.at[q, 1 - c], dst_ref=out_ref.at[q],
                                             send_sem=send_sems.at[w, 0], recv_sem=recv_sems.at[w, 0],
                                             device_id=(x, y, 1 - c), device_id_type=MESH).start()
        for w, out_ref in enumerate(out_refs):
            pltpu.make_async_remote_copy(src_ref=out_ref, dst_ref=out_ref, send_sem=send_sems.at[w, 0],
                                         recv_sem=recv_sems.at[w, 0], device_id=(x, y, 1 - c),
                                         device_id_type=MESH).wait()

    return _comm_call(body, bufs, [_sds((4,) + b.shape[2:], b.dtype) for b in bufs], 1, name, False)


def _chip_exchange(bufs, *, name):
    def body(x_refs, out_refs, send_sems, recv_sems, local_sems):
        x, y, c = _place()
        p = 2 * x + y
        chips = [(1 - x, y), (x, 1 - y), (1 - x, 1 - y)]
        started = []
        for w, (x_ref, out_ref) in enumerate(zip(x_refs, out_refs)):
            mine = pltpu.make_async_copy(x_ref.at[p], out_ref.at[p], local_sems.at[w])
            mine.start()
            sends = [pltpu.make_async_remote_copy(src_ref=x_ref.at[2 * qx + qy], dst_ref=out_ref.at[p],
                                                  send_sem=send_sems.at[w, k], recv_sem=recv_sems.at[w, k],
                                                  device_id=(qx, qy, c), device_id_type=MESH)
                     for k, (qx, qy) in enumerate(chips)]
            for cp in sends:
                cp.start()
            started.append((mine, sends))
        for w, (x_ref, out_ref) in enumerate(zip(x_refs, out_refs)):
            mine, sends = started[w]
            for k, (qx, qy) in enumerate(chips):
                pltpu.make_async_remote_copy(src_ref=x_ref.at[p], dst_ref=out_ref.at[2 * qx + qy],
                                             send_sem=send_sems.at[w, k], recv_sem=recv_sems.at[w, k],
                                             device_id=(qx, qy, c), device_id_type=MESH).wait_recv()
            for cp in sends:
                cp.wait_send()
            mine.wait()

    return _comm_call(body, bufs, [_sds(b.shape, b.dtype) for b in bufs], 3, name, False)


def _block_rows(rows, row_bytes, target=1 << 20, align=BF16_SUBLANES):
    return _tile(rows, max(align, target // row_bytes // align * align), align)


def _sum_blocks(buf, *, name, out_dtype):
    B, R, C = buf.shape
    tm = _block_rows(R, B * C * buf.dtype.itemsize)

    def body(x_ref, o_ref):
        acc = x_ref[0].astype(F32)
        for b in range(1, B):
            acc = acc + x_ref[b].astype(F32)
        o_ref[...] = acc.astype(o_ref.dtype)

    return pl.pallas_call(
        body, name=name, grid=(R // tm,), in_specs=[pl.BlockSpec((B, tm, C), lambda i: (0, i, 0))],
        out_specs=pl.BlockSpec((tm, C), lambda i: (i, 0)), out_shape=_sds((R, C), out_dtype),
        compiler_params=_params(("parallel",)),
    )(buf)


def _pair_add(mine, theirs, core, *, name):
    _, _, R, C = mine.shape
    tm = _block_rows(R, C * 2)

    def body(core_ref, a_ref, b_ref, o_ref):
        o_ref[...] = (a_ref[...].astype(F32) + b_ref[...].astype(F32)).astype(o_ref.dtype)

    return pl.pallas_call(
        body, name=name, out_shape=_sds(theirs.shape, BF16),
        grid_spec=pltpu.PrefetchScalarGridSpec(
            num_scalar_prefetch=1, grid=(4, R // tm),
            in_specs=[pl.BlockSpec((None, None, tm, C), lambda q, i, core_ref: (q, core_ref[0], i, 0)),
                      pl.BlockSpec((None, tm, C), lambda q, i, core_ref: (q, i, 0))],
            out_specs=pl.BlockSpec((None, tm, C), lambda q, i, core_ref: (q, i, 0))),
        compiler_params=_params(("parallel", "parallel")),
    )(core, mine, theirs)


def _split_lanes(row, widths):
    out, off = [], 0
    for wd in widths:
        out.append(row[:, off:off + wd])
        off += wd
    return out


def _adamw(w, g, m, v, *, name):
    C = w.shape[1]

    def fn(w, g, m, v):
        m = ADAM_B1 * m + (1.0 - ADAM_B1) * g
        v = ADAM_B2 * v + (1.0 - ADAM_B2) * (g * g)
        m_hat = m / (1.0 - ADAM_B1 ** ADAM_STEP)
        v_hat = v / (1.0 - ADAM_B2 ** ADAM_STEP)
        delta = -ADAM_LR * (m_hat / (jnp.sqrt(v_hat) + ADAM_EPS) + ADAM_WD * w)
        return (delta, m, v), ()

    tm = max(8, min(512, (1 << 20) // (4 * C) // 8 * 8))
    (d, nm, nv), _ = _rowwise(fn, [w, g, m, v], [], [(C, F32)] * 3, name=name, tm=tm)
    return d, nm, nv


def _rope_tables(n):
    rows = n // GRID_W
    row = jnp.repeat(jnp.arange(rows, dtype=F32), GRID_W)
    col = jnp.tile(jnp.arange(GRID_W, dtype=F32), rows)
    nf = ROPE_DIM // 4
    freqs = ROPE_THETA ** (-jnp.arange(nf, dtype=F32) / nf)
    ang_r, ang_c = row[:, None] * freqs[None, :], col[:, None] * freqs[None, :]
    cr, sr, cc, sc = jnp.cos(ang_r), jnp.sin(ang_r), jnp.cos(ang_c), jnp.sin(ang_c)
    nope = HEAD_PAD - 2 * ROPE_DIM
    one, zero, z = jnp.ones((n, nope), F32), jnp.zeros((n, nope), F32), jnp.zeros((n, nf), F32)
    pad = jnp.zeros((n, ROPE_DIM), F32)
    cos = jnp.concatenate([one, cr, cr, cc, cc, pad], axis=1)
    s_lo = jnp.concatenate([zero, -sr, z, -sc, z, pad], axis=1)
    s_hi = jnp.concatenate([zero, z, sr, z, sc, pad], axis=1)
    return cos, s_lo, s_hi


def _rope(n, cos, s_lo, s_hi):
    q = ROPE_DIM // 4
    return n * cos + pltpu.roll(n, HEAD_PAD - q, 1) * s_lo + pltpu.roll(n, q, 1) * s_hi


def _rope_t(d, cos, s_lo, s_hi):
    q = ROPE_DIM // 4
    return d * cos + pltpu.roll(d * s_lo, q, 1) + pltpu.roll(d * s_hi, HEAD_PAD - q, 1)


def kernel(x, c, ctx, c_ctx, w_mod, b_mod, norm1_g, w_in, q_norm_g, kv_norm_g, w_uq, w_ukv, qk_norm_q, qk_norm_k, sgu_norm_g, sgu_norm_b, w_spatial, b_spatial, w_br_attn, w_br_sgu, w_out, norm2_g, w_ffn_in, w_ffn_out, loss_target, m_c_ctx, m_w_mod, m_b_mod, m_norm1_g, m_w_in, m_q_norm_g, m_kv_norm_g, m_w_uq, m_w_ukv, m_qk_norm_q, m_qk_norm_k, m_sgu_norm_g, m_sgu_norm_b, m_w_spatial, m_b_spatial, m_w_br_attn, m_w_br_sgu, m_w_out, m_norm2_g, m_w_ffn_in, m_w_ffn_out, v_c_ctx, v_w_mod, v_b_mod, v_norm1_g, v_w_in, v_q_norm_g, v_kv_norm_g, v_w_uq, v_w_ukv, v_qk_norm_q, v_qk_norm_k, v_sgu_norm_g, v_sgu_norm_b, v_w_spatial, v_b_spatial, v_w_br_attn, v_w_br_sgu, v_w_out, v_norm2_g, v_w_ffn_in, v_w_ffn_out):
    ax, ay, ac = _place()
    my_chip = 2 * ax + ay
    my_dev = 4 * ax + 2 * ay + ac

    N, D = x.shape[1], x.shape[2]
    CT = ctx.shape[1]
    M = N + CT
    QL, KVL, QK = q_norm_g.shape[-1], kv_norm_g.shape[-1], qk_norm_q.shape[-1]
    NOPE = QK - ROPE_DIM
    VD = NOPE
    H = 4 * w_uq.shape[-1] // QK
    SW, G, CH = sgu_norm_g.shape[-1], w_spatial.shape[1], w_spatial.shape[2]
    GD = SW // G
    DFF = 4 * w_ffn_out.shape[1]
    NMOD = 4 * w_mod.shape[-1]
    NM = w_mod.shape[-1]
    KVP = KVL + 2 * ROPE_DIM
    assert NOPE == LANES and GD == LANES and HEAD_PAD == NOPE + 2 * ROPE_DIM and CH == LANES
    scale = QK ** -0.5

    x2, ctx2, tgt2 = x[0], ctx[0], loss_target[0]

    c_all = _all_gather8([c], name="ag_c", in_vmem=True)[0][:, 0, :]
    c_rows = jnp.concatenate([c_all, c_ctx[None, :], jnp.zeros((BF16_SUBLANES - 9, D), F32)], axis=0)

    def silu_fn(t):
        s = _sigmoid(t)
        return (t * s, s * (1.0 + t * (1.0 - s))), ()

    (silu_c, dsilu_c), _ = _rowwise(silu_fn, [c_rows], [], [(D, F32), (D, F32)], name="silu_c", tm=16)
    wm = w_mod[0]
    mod_loc = _mm([(silu_c, wm)], name="mod_fwd", outs=(F32,), tn=512, tk=512,
                  extras=[(lax.dynamic_slice_in_dim(b_mod, my_chip * NM, NM, axis=1), "n")],
                  epi=lambda acc, b: (acc + b,))
    mod_all = _all_gather8([mod_loc], name="ag_mod", in_vmem=True)[0]
    mod_full = jnp.concatenate([mod_all[0], mod_all[2], mod_all[4], mod_all[6]], axis=1)
    mod_me = lax.dynamic_slice_in_dim(mod_full, my_dev, 1, axis=0)
    sh1, sc1, g1, sh2, sc2, g2 = [mod_me[:, i * D:(i + 1) * D] for i in range(6)]
    sh1c, sc1c = mod_full[8:9, :D], mod_full[8:9, D:2 * D]

    big = [w_in[0], w_uq[0], w_ukv[0], w_br_attn[0], w_br_sgu[0], w_out[0], w_ffn_in[0], w_ffn_out[0]]
    col_sharded = [True, True, True, True, True, False, True, False]
    halves = [lax.dynamic_slice_in_dim(a, ac * (a.shape[0] // 2), a.shape[0] // 2, axis=0).astype(BF16) for a in big]
    gathered = _all_gather8(halves, name="ag_weights", in_vmem=False)
    full = []
    for seg, a, cs in zip(gathered, big, col_sharded):
        seg = seg.reshape((4,) + a.shape)
        full.append(seg.transpose(1, 0, 2).reshape(a.shape[0], 4 * a.shape[1]) if cs else seg.reshape(4 * a.shape[0], a.shape[1]))
    w_in_f, w_uq_f, w_ukv_f, w_bra, w_brs, w_out_f, w_ffi, w_ffo = full
    o_kv, o_u = QL, QL + KVL + ROPE_DIM
    o_v, o_g = o_u + SW, o_u + 2 * SW
    w_q = w_in_f[:, :QL]
    w_kv = jnp.pad(w_in_f[:, o_kv:o_u], ((0, 0), (0, ROPE_DIM)))
    w_u, w_v = w_in_f[:, o_u:o_v], w_in_f[:, o_v:o_g]
    w_g1, w_g2 = w_in_f[:, o_g:o_g + D], w_in_f[:, o_g + D:]
    w_uq_p = jnp.pad(w_uq_f.reshape(QL, H, QK), ((0, 0), (0, 0), (0, HEAD_PAD - QK))).reshape(QL, H * HEAD_PAD)
    w_fa, w_fb = w_ffi[:, :DFF], w_ffi[:, DFF:]

    cos_t, slo_t, shi_t = _rope_tables(N)
    ones_c = jnp.concatenate([jnp.ones((CT, NOPE + ROPE_DIM), F32), jnp.zeros((CT, ROPE_DIM), F32)], axis=1)
    cos_k = jnp.concatenate([cos_t, ones_c], axis=0)
    slo_k = jnp.concatenate([slo_t, jnp.zeros((CT, HEAD_PAD), F32)], axis=0)
    shi_k = jnp.concatenate([shi_t, jnp.zeros((CT, HEAD_PAD), F32)], axis=0)
    gq_p = jnp.pad(qk_norm_q, ((0, 0), (0, HEAD_PAD - QK)))
    gk_p = jnp.pad(qk_norm_k, ((0, 0), (0, HEAD_PAD - QK)))

    def norm_mod_fn(t, g, sh, sc):
        r = _rms_stats(t, D)
        return (((t * r) * g) * (1.0 + sc) + sh,), ()

    (h,), _ = _rowwise(norm_mod_fn, [x2], [norm1_g, sh1, sc1], [(D, BF16)], name="norm1_x")
    (ctx_h,), _ = _rowwise(norm_mod_fn, [ctx2], [norm1_g, sh1c, sc1c], [(D, BF16)], name="norm1_ctx")

    qc = _mm([(h, w_q)], name="proj_q", outs=(F32,))
    kvin = jnp.concatenate([_mm([(h, w_kv)], name="proj_kv", outs=(F32,)),
                            _mm([(ctx_h, w_kv)], name="proj_kv_ctx", outs=(F32,))], axis=0)
    u_in = _mm([(h, w_u)], name="proj_u", outs=(BF16,))
    v_in = _mm([(h, w_v)], name="proj_v", outs=(BF16,))
    g1_in = _mm([(h, w_g1)], name="proj_g1", outs=(BF16,))
    g2_in = _mm([(h, w_g2)], name="proj_g2", outs=(BF16,))

    def rms_gain_fn(width):
        def fn(t, g):
            return (((t * _rms_stats(t, width)) * g),), ()
        return fn

    (qn,), _ = _rowwise(rms_gain_fn(QL), [qc], [q_norm_g], [(QL, BF16)], name="q_norm")

    def kv_norm_fn(t, g):
        kvc = t[:, :KVL]
        return (((kvc * _rms_stats(kvc, KVL)) * g),), ()

    (kvn,), _ = _rowwise(kv_norm_fn, [kvin], [kv_norm_g], [(KVL, BF16)], name="kv_norm")
    q_raw = _mm([(qn, w_uq_p)], name="q_up", outs=(F32,))
    kv_raw = _mm([(kvn, w_ukv_f)], name="kv_up", outs=(F32,))

    def q_post_fn(t, cos, slo, shi, g):
        outs = []
        for hd in range(H):
            th = t[:, hd * HEAD_PAD:(hd + 1) * HEAD_PAD]
            outs.append(_rope((th * _rms_stats(th, QK)) * g, cos, slo, shi))
        return (jnp.concatenate(outs, axis=1),), ()

    (q_att,), _ = _rowwise(q_post_fn, [q_raw, cos_t, slo_t, shi_t], [gq_p], [(H * HEAD_PAD, BF16)], name="q_post")

    def k_post_fn(t, kvi, cos, slo, shi, g):
        kr = kvi[:, KVL:]
        ks, vs = [], []
        for hd in range(H):
            th = jnp.concatenate([t[:, hd * HEAD_PAD:hd * HEAD_PAD + NOPE], kr], axis=1)
            ks.append(_rope((th * _rms_stats(th, QK)) * g, cos, slo, shi))
            vs.append(t[:, hd * HEAD_PAD + NOPE:(hd + 1) * HEAD_PAD])
        return (jnp.concatenate(ks, axis=1), jnp.concatenate(vs, axis=1)), ()

    (k_att, v_att), _ = _rowwise(k_post_fn, [kv_raw, kvin, cos_k, slo_k, shi_k], [gk_p],
                                 [(H * HEAD_PAD, BF16), (H * VD, BF16)], name="k_post")
    attn_o, lse = _attn_fwd(q_att, k_att, v_att, heads=H, scale=scale)

    ws3 = w_spatial[0]
    bs_t = jnp.pad(b_spatial[0].T, ((0, 0), (0, LANES - G)))

    def sgu_parts(u_in, v_in, ng, nb):
        u, v = _gelu(u_in.astype(F32)), _gelu(v_in.astype(F32))
        mu = jnp.mean(v, axis=-1, keepdims=True)
        vc = v - mu
        rs = lax.rsqrt(jnp.mean(vc * vc, axis=-1, keepdims=True) + EPS)
        xhat = vc * rs
        return u, xhat, rs, (xhat * ng + nb).astype(BF16)

    def sgu_fwd_fn(u_in, v_in, ng, nb, ws, bst):
        u, _, _, vnb = sgu_parts(u_in, v_in, ng, nb)
        outs = []
        for g in range(G):
            sl = slice(g * GD, (g + 1) * GD)
            mixed = jnp.dot(ws[g].astype(BF16), vnb[:, sl], preferred_element_type=F32) + bst[:, g:g + 1]
            outs.append(u[:, sl] * mixed)
        return (jnp.concatenate(outs, axis=1),), ()

    (sgu_o,), _ = _rowwise(sgu_fwd_fn, [u_in, v_in], [sgu_norm_g, sgu_norm_b, ws3, bs_t], [(SW, BF16)],
                           name="sgu_fwd", tm=CH)

    a1 = _mm([(attn_o, w_bra)], name="br_attn", outs=(BF16,))
    a2 = _mm([(sgu_o, w_brs)], name="br_sgu", outs=(BF16,))

    def merge_fn(a1, a2, gi1, gi2):
        return ((_sigmoid(gi1.astype(F32)) * a1.astype(F32) + _sigmoid(gi2.astype(F32)) * a2.astype(F32)),), ()

    (merged,), _ = _rowwise(merge_fn, [a1, a2, g1_in, g2_in], [], [(D, BF16)], name="merge", tc=1024)

    def res_gate(acc, res, gate):
        return res + gate * acc, acc

    x1, mo = _mm([(merged, w_out_f)], name="out_proj", outs=(F32, BF16), tn=512,
                 extras=[(x2, "mn"), (g1, "n")], epi=res_gate)
    (h2,), _ = _rowwise(norm_mod_fn, [x1], [norm2_g, sh2, sc2], [(D, BF16)], name="norm2")

    def swiglu_epi(a, b):
        return a, b, (a * _sigmoid(a)) * b

    fa, fb, act = _mm([(h2, w_fa, w_fb)], name="ffn_in", outs=(BF16, BF16, BF16), tn=512, epi=swiglu_epi)
    y, f_out = _mm([(act, w_ffo)], name="ffn_out", outs=(F32, BF16), tn=512,
                   extras=[(x1, "mn"), (g2, "n")], epi=res_gate)

    def loss_fn(y, t, fo, g2v):
        e = y - t
        dy = e * (1.0 / D)
        return (dy, g2v * dy), (_colsum(e * e) * (0.5 / D), _colsum(dy * fo.astype(F32)))

    (dy, df), (loss_cols, dg2) = _rowwise(loss_fn, [y, tgt2, f_out], [g2], [(D, F32), (D, BF16)],
                                          [(1, D), (1, D)], name="loss")

    def swiglu_bwd_epi(dact, a, b):
        a, b = a.astype(F32), b.astype(F32)
        s = _sigmoid(a)
        return dact * b * (s * (1.0 + a * (1.0 - s))), dact * (a * s)

    da, db = _mm([(df, w_ffo)], tb=True, name="ffn_out_dx", outs=(BF16, BF16), tn=512,
                 extras=[(fa, "mn"), (fb, "mn")], epi=swiglu_bwd_epi)
    dw_ffo = _mm([(act, df)], ta=True, name="ffn_out_dw", outs=(BF16,))
    dh2 = _mm([(da, w_fa), (db, w_fb)], tb=True, name="ffn_in_dx", outs=(F32,))
    dw_fa = _mm([(h2, da)], ta=True, name="ffn_in_dw_a", outs=(BF16,))
    dw_fb = _mm([(h2, db)], ta=True, name="ffn_in_dw_b", outs=(BF16,))

    def norm2_bwd_fn(dh, t, dyv, mov, g, sc, g1v):
        r = _rms_stats(t, D)
        tn = t * r
        dxg = dh * (1.0 + sc)
        dt = dyv + _rms_bwd(dxg * g, tn, r, D)
        return (dt, g1v * dt), (_colsum(dh), _colsum(dh * (tn * g)), _colsum(dxg * tn), _colsum(dt * mov.astype(F32)))

    (dx1, dmo), (dsh2, dsc2, dn2g, dg1) = _rowwise(
        norm2_bwd_fn, [dh2, x1, dy, mo], [norm2_g, sc2, g1], [(D, F32), (D, BF16)], [(1, D)] * 4, name="norm2_bwd")

    def merge_bwd_epi(dm, a1, a2, gi1, gi2):
        s1, s2 = _sigmoid(gi1.astype(F32)), _sigmoid(gi2.astype(F32))
        a1, a2 = a1.astype(F32), a2.astype(F32)
        return dm * s1, dm * s2, dm * a1 * (s1 * (1.0 - s1)), dm * a2 * (s2 * (1.0 - s2))

    da1, da2, dgi1, dgi2 = _mm([(dmo, w_out_f)], tb=True, name="out_proj_dx", outs=(BF16,) * 4, tn=512,
                               extras=[(a1, "mn"), (a2, "mn"), (g1_in, "mn"), (g2_in, "mn")], epi=merge_bwd_epi)
    dw_out = _mm([(merged, dmo)], ta=True, name="out_proj_dw", outs=(BF16,))
    dattn = _mm([(da1, w_bra)], tb=True, name="br_attn_dx", outs=(BF16,))
    dw_bra = _mm([(attn_o, da1)], ta=True, name="br_attn_dw", outs=(BF16,))
    dsgu = _mm([(da2, w_brs)], tb=True, name="br_sgu_dx", outs=(BF16,))
    dw_brs = _mm([(sgu_o, da2)], ta=True, name="br_sgu_dw", outs=(BF16,))

    def sgu_bwd_fn(dso, u_in, v_in, ng, nb, ws, bst):
        u, xhat, rs, vnb = sgu_parts(u_in, v_in, ng, nb)
        dso = dso.astype(F32)
        lane = lax.broadcasted_iota(jnp.int32, (CH, LANES), 1)
        du, dvn, dws, dbs = [], [], [], jnp.zeros((CH, LANES), F32)
        for g in range(G):
            sl = slice(g * GD, (g + 1) * GD)
            wg = ws[g].astype(BF16)
            mixed = jnp.dot(wg, vnb[:, sl], preferred_element_type=F32) + bst[:, g:g + 1]
            du.append(dso[:, sl] * mixed)
            dmix = dso[:, sl] * u[:, sl]
            dmb = dmix.astype(BF16)
            dws.append(lax.dot_general(dmb, vnb[:, sl], (((1,), (1,)), ((), ())), preferred_element_type=F32))
            dbs = dbs + jnp.where(lane == g, jnp.sum(dmix, axis=1, keepdims=True), 0.0)
            dvn.append(lax.dot_general(wg, dmb, (((0,), (0,)), ((), ())), preferred_element_type=F32))
        du, dvn = jnp.concatenate(du, axis=1), jnp.concatenate(dvn, axis=1)
        dxh = dvn * ng
        dv = rs * (dxh - jnp.mean(dxh, axis=-1, keepdims=True) - xhat * jnp.mean(dxh * xhat, axis=-1, keepdims=True))
        return ((du * _gelu_grad(u_in.astype(F32)), dv * _gelu_grad(v_in.astype(F32))),
                (_colsum(dvn * xhat), _colsum(dvn), jnp.stack(dws), dbs))

    (du_in, dv_in), (d_sng, d_snb, d_ws, d_bs) = _rowwise(
        sgu_bwd_fn, [dsgu, u_in, v_in], [sgu_norm_g, sgu_norm_b, ws3, bs_t], [(SW, BF16), (SW, BF16)],
        [(1, SW), (1, SW), (G, CH, CH), (CH, LANES)], name="sgu_bwd", tm=CH)

    dq_att, dk_att, dv_att = _attn_bwd(q_att, k_att, v_att, attn_o, lse, dattn, heads=H, scale=scale)

    def q_post_bwd_fn(dq, t, cos, slo, shi, g):
        outs, dg = [], jnp.zeros((1, HEAD_PAD), F32)
        for hd in range(H):
            sl = slice(hd * HEAD_PAD, (hd + 1) * HEAD_PAD)
            th = t[:, sl]
            r = _rms_stats(th, QK)
            tn = th * r
            dn = _rope_t(dq[:, sl], cos, slo, shi)
            dg = dg + _colsum(dn * tn)
            outs.append(_rms_bwd(dn * g, tn, r, QK))
        return (jnp.concatenate(outs, axis=1),), (dg,)

    (dq_raw,), (d_gq,) = _rowwise(q_post_bwd_fn, [dq_att, q_raw, cos_t, slo_t, shi_t], [gq_p],
                                  [(H * HEAD_PAD, BF16)], [(1, HEAD_PAD)], name="q_post_bwd")

    def k_post_bwd_fn(dk, dv, t, kvi, cos, slo, shi, g):
        kr = kvi[:, KVL:]
        outs, dg, dkr = [], jnp.zeros((1, HEAD_PAD), F32), jnp.zeros_like(kr)
        for hd in range(H):
            th = jnp.concatenate([t[:, hd * HEAD_PAD:hd * HEAD_PAD + NOPE], kr], axis=1)
            r = _rms_stats(th, QK)
            tn = th * r
            dn = _rope_t(dk[:, hd * HEAD_PAD:(hd + 1) * HEAD_PAD], cos, slo, shi)
            dg = dg + _colsum(dn * tn)
            dt = _rms_bwd(dn * g, tn, r, QK)
            dkr = dkr + dt[:, NOPE:]
            outs += [dt[:, :NOPE], dv[:, hd * VD:(hd + 1) * VD]]
        return (jnp.concatenate(outs, axis=1), dkr), (dg,)

    (dkv_raw, dkrope), (d_gk,) = _rowwise(
        k_post_bwd_fn, [dk_att, dv_att, kv_raw, kvin, cos_k, slo_k, shi_k], [gk_p],
        [(H * HEAD_PAD, BF16), (2 * ROPE_DIM, F32)], [(1, HEAD_PAD)], name="k_post_bwd")

    dqn = _mm([(dq_raw, w_uq_p)], tb=True, name="q_up_dx", outs=(F32,))
    dw_uq_p = _mm([(qn, dq_raw)], ta=True, name="q_up_dw", outs=(BF16,))
    dkvn = _mm([(dkv_raw, w_ukv_f)], tb=True, name="kv_up_dx", outs=(F32,))
    dw_ukv = _mm([(kvn, dkv_raw)], ta=True, name="kv_up_dw", outs=(BF16,))

    def q_norm_bwd_fn(dn, t, g):
        r = _rms_stats(t, QL)
        tn = t * r
        return (_rms_bwd(dn * g, tn, r, QL),), (_colsum(dn * tn),)

    (dqc,), (d_qng,) = _rowwise(q_norm_bwd_fn, [dqn, qc], [q_norm_g], [(QL, BF16)], [(1, QL)], name="q_norm_bwd")

    def kv_norm_bwd_fn(dn, dkr, t, g):
        kvc = t[:, :KVL]
        r = _rms_stats(kvc, KVL)
        tn = kvc * r
        return (jnp.concatenate([_rms_bwd(dn * g, tn, r, KVL), dkr], axis=1),), (_colsum(dn * tn),)

    (dkvin,), (d_kvng,) = _rowwise(kv_norm_bwd_fn, [dkvn, dkrope, kvin], [kv_norm_g], [(KVP, BF16)], [(1, KVL)],
                                   name="kv_norm_bwd")
    dkvin_x, dkvin_c = dkvin[:N], dkvin[N:]

    dh = _mm([(dqc, w_q), (dkvin_x, w_kv), (du_in, w_u), (dv_in, w_v), (dgi1, w_g1), (dgi2, w_g2)], tb=True,
             name="proj_dx", outs=(F32,), tn=512)
    dctx_h = _mm([(dkvin_c, w_kv)], tb=True, name="proj_kv_ctx_dx", outs=(F32,))
    dw_q = _mm([(h, dqc)], ta=True, name="proj_q_dw", outs=(BF16,))
    dw_kv = _mm([(h, dkvin_x), (ctx_h, dkvin_c)], ta=True, name="proj_kv_dw", outs=(BF16,))
    dw_u = _mm([(h, du_in)], ta=True, name="proj_u_dw", outs=(BF16,))
    dw_v = _mm([(h, dv_in)], ta=True, name="proj_v_dw", outs=(BF16,))
    dw_g1 = _mm([(h, dgi1)], ta=True, name="proj_g1_dw", outs=(BF16,))
    dw_g2 = _mm([(h, dgi2)], ta=True, name="proj_g2_dw", outs=(BF16,))

    def norm1_bwd_fn(dhv, t, dres, g, sc):
        r = _rms_stats(t, D)
        tn = t * r
        dxg = dhv * (1.0 + sc)
        return (dres + _rms_bwd(dxg * g, tn, r, D),), (_colsum(dhv), _colsum(dhv * (tn * g)), _colsum(dxg * tn))

    (grad_x,), (dsh1, dsc1, dn1g_x) = _rowwise(norm1_bwd_fn, [dh, x2, dx1], [norm1_g, sc1], [(D, F32)], [(1, D)] * 3,
                                               name="norm1_bwd")
    _, (dsh1c, dsc1c, dn1g_c) = _rowwise(norm1_bwd_fn, [dctx_h, ctx2, jnp.zeros_like(ctx2)], [norm1_g, sc1c],
                                         [(D, F32)], [(1, D)] * 3, name="norm1_ctx_bwd")

    small = [dsh1, dsc1, dg1, dsh2, dsc2, dg2,
             dsh1c, dsc1c, dn1g_x, dn1g_c, d_qng, d_kvng, d_gq, d_gk, d_sng, d_snb, dn2g, loss_cols]
    small_sizes = [a.shape[1] for a in small]
    sm_row = jnp.concatenate(small, axis=1)
    sm_mat = jnp.concatenate([d_ws.reshape(G * CH, CH), d_bs], axis=0)
    row_all, mat_all = _all_gather8([sm_row, sm_mat], name="ag_small", in_vmem=True)
    row_sum = _sum_blocks(row_all, name="sum_small_rows", out_dtype=F32)
    mat_sum = _sum_blocks(mat_all, name="sum_small_mats", out_dtype=F32)
    dmod_rows = row_all[:, 0, :NMOD]
    (_, _, _, _, _, _, t_sh1c, t_sc1c, t_n1x, t_n1c, g_qng, g_kvng, t_gq, t_gk, g_sng, g_snb, g_n2g,
     t_loss) = _split_lanes(row_sum, small_sizes)
    g_ws, t_bs = mat_sum[:G * CH], mat_sum[G * CH:]
    dmodc_row = jnp.concatenate([t_sh1c, t_sc1c, jnp.zeros((1, NMOD - 2 * D), F32)], axis=1)
    dmod16 = jnp.concatenate([dmod_rows, dmodc_row, jnp.zeros((BF16_SUBLANES - 9, NMOD), F32)], axis=0)

    def small_fn(rows, n1x, n1c, lossv):
        return (), (_colsum(rows), n1x + n1c, jnp.sum(lossv, axis=1, keepdims=True))

    _, (g_bmod, g_n1g, loss11) = _rowwise(small_fn, [dmod16], [t_n1x, t_n1c, t_loss], [], [(1, NMOD), (1, D), (1, 1)],
                                          name="small_reduce", tm=16)
    dmod_loc = lax.dynamic_slice_in_dim(dmod16, my_chip * NM, NM, axis=1)
    g_wmod = _mm([(silu_c, dmod_loc)], ta=True, name="mod_dw", outs=(F32,), tn=512)
    dsilu_part = _mm([(dmod_loc, wm)], tb=True, name="mod_dx", outs=(F32,), tk=512)
    part_all = _all_gather8([dsilu_part[8:9]], name="ag_cctx", in_vmem=True)[0]

    def cctx_fn(parts, dsl):
        return (), ((parts[0:1] + parts[2:3] + parts[4:5] + parts[6:7]) * dsl,)

    _, (g_cctx,) = _rowwise(cctx_fn, [part_all[:, 0, :]], [dsilu_c[8:9]], [], [(1, D)], name="cctx_grad", tm=8)

    dw_in_f = jnp.concatenate([dw_q, dw_kv[:, :KVL + ROPE_DIM], dw_u, dw_v, dw_g1, dw_g2], axis=1)
    dw_uq_f = dw_uq_p.reshape(QL, H, HEAD_PAD)[:, :, :QK].reshape(QL, H * QK)
    dw_ffi = jnp.concatenate([dw_fa, dw_fb], axis=1)
    dfull = [dw_in_f, dw_uq_f, dw_ukv, dw_bra, dw_brs, dw_out, dw_ffi, dw_ffo]
    tags = ["w_in", "w_uq", "w_ukv", "w_br_attn", "w_br_sgu", "w_out", "w_ffn_in", "w_ffn_out"]
    core = jnp.reshape(ac, (1,)).astype(jnp.int32)
    g4 = []
    for dwf, a, cs in zip(dfull, big, col_sharded):
        K, Ns = a.shape
        t = dwf.reshape(K, 4, Ns).transpose(1, 0, 2) if cs else dwf.reshape(4, K, Ns)
        g4.append(t.reshape(4, 2, K // 2, Ns))
    sib = _pair_swap_other(g4, name="rs_pair")
    pair = [_pair_add(g, s, core, name="rs_pair_add_" + t) for g, s, t in zip(g4, sib, tags)]
    xchg = _chip_exchange(pair, name="rs_chips")
    red_half = [_sum_blocks(t4, name="rs_sum_" + t, out_dtype=F32) for t4, t in zip(xchg, tags)]
    big_grads = [r.reshape(a.shape) for r, a in zip(_pair_gather(red_half, name="rs_halves"), big)]

    def upd(w, g, m, v, nm):
        shape = w.shape
        w2, g2_, m2, v2 = [t.reshape(-1, shape[-1]) for t in (w, g, m, v)]
        d_, m_, v_ = _adamw(w2, g2_, m2, v2, name="adamw_" + nm)
        return g.reshape(shape), d_.reshape(shape), m_.reshape(shape), v_.reshape(shape)

    g_in, g_uq, g_ukv, g_bra, g_brs, g_out, g_ffi, g_ffo = big_grads
    grads = dict(
        c_ctx=g_cctx.reshape(D), w_mod=g_wmod[None], b_mod=g_bmod, norm1_g=g_n1g, w_in=g_in[None],
        q_norm_g=g_qng, kv_norm_g=g_kvng, w_uq=g_uq[None], w_ukv=g_ukv[None],
        qk_norm_q=t_gq[:, :QK], qk_norm_k=t_gk[:, :QK], sgu_norm_g=g_sng, sgu_norm_b=g_snb,
        w_spatial=g_ws.reshape(w_spatial.shape), b_spatial=t_bs[:, :G].T[None],
        w_br_attn=g_bra[None], w_br_sgu=g_brs[None], w_out=g_out[None], norm2_g=g_n2g,
        w_ffn_in=g_ffi[None], w_ffn_out=g_ffo[None])
    weights = dict(c_ctx=c_ctx, w_mod=w_mod, b_mod=b_mod, norm1_g=norm1_g, w_in=w_in, q_norm_g=q_norm_g,
                   kv_norm_g=kv_norm_g, w_uq=w_uq, w_ukv=w_ukv, qk_norm_q=qk_norm_q, qk_norm_k=qk_norm_k,
                   sgu_norm_g=sgu_norm_g, sgu_norm_b=sgu_norm_b, w_spatial=w_spatial, b_spatial=b_spatial,
                   w_br_attn=w_br_attn, w_br_sgu=w_br_sgu, w_out=w_out, norm2_g=norm2_g, w_ffn_in=w_ffn_in,
                   w_ffn_out=w_ffn_out)
    m_in = dict(c_ctx=m_c_ctx, w_mod=m_w_mod, b_mod=m_b_mod, norm1_g=m_norm1_g, w_in=m_w_in, q_norm_g=m_q_norm_g,
                kv_norm_g=m_kv_norm_g, w_uq=m_w_uq, w_ukv=m_w_ukv, qk_norm_q=m_qk_norm_q, qk_norm_k=m_qk_norm_k,
                sgu_norm_g=m_sgu_norm_g, sgu_norm_b=m_sgu_norm_b, w_spatial=m_w_spatial, b_spatial=m_b_spatial,
                w_br_attn=m_w_br_attn, w_br_sgu=m_w_br_sgu, w_out=m_w_out, norm2_g=m_norm2_g, w_ffn_in=m_w_ffn_in,
                w_ffn_out=m_w_ffn_out)
    v_in_ = dict(c_ctx=v_c_ctx, w_mod=v_w_mod, b_mod=v_b_mod, norm1_g=v_norm1_g, w_in=v_w_in, q_norm_g=v_q_norm_g,
                 kv_norm_g=v_kv_norm_g, w_uq=v_w_uq, w_ukv=v_w_ukv, qk_norm_q=v_qk_norm_q, qk_norm_k=v_qk_norm_k,
                 sgu_norm_g=v_sgu_norm_g, sgu_norm_b=v_sgu_norm_b, w_spatial=v_w_spatial, b_spatial=v_b_spatial,
                 w_br_attn=v_w_br_attn, w_br_sgu=v_w_br_sgu, w_out=v_w_out, norm2_g=v_norm2_g, w_ffn_in=v_w_ffn_in,
                 w_ffn_out=v_w_ffn_out)
    names = list(weights)
    big_names = ("w_mod", "w_in", "w_uq", "w_ukv", "w_br_attn", "w_br_sgu", "w_out", "w_ffn_in", "w_ffn_out")
    out_g, out_d, out_m, out_v = {}, {}, {}, {}
    for nm in big_names:
        out_g[nm], out_d[nm], out_m[nm], out_v[nm] = upd(weights[nm], grads[nm], m_in[nm], v_in_[nm], nm)
    row_names = [nm for nm in names if nm not in big_names and nm not in ("w_spatial", "b_spatial")]
    widths = [-(-weights[nm].size // LANES) * LANES for nm in row_names]

    def as_row(d):
        return jnp.concatenate([jnp.pad(d[nm].reshape(1, -1), ((0, 0), (0, wd - d[nm].size)))
                                for nm, wd in zip(row_names, widths)], axis=1)

    def as_mat(d):
        return jnp.concatenate([d["w_spatial"].reshape(G * CH, CH), d["b_spatial"].reshape(G, CH)], axis=0)

    row_res = _adamw(as_row(weights), as_row(grads), as_row(m_in), as_row(v_in_), name="adamw_rows")
    mat_res = _adamw(as_mat(weights), as_mat(grads), as_mat(m_in), as_mat(v_in_), name="adamw_spatial")
    for tgt, row, mat in zip((out_d, out_m, out_v), row_res, mat_res):
        for nm, seg in zip(row_names, _split_lanes(row, widths)):
            tgt[nm] = seg[:, :weights[nm].size].reshape(weights[nm].shape)
        tgt["w_spatial"] = mat[:G * CH].reshape(w_spatial.shape)
        tgt["b_spatial"] = mat[G * CH:].reshape(b_spatial.shape)
    for nm in row_names + ["w_spatial", "b_spatial"]:
        out_g[nm] = grads[nm].reshape(weights[nm].shape)

    loss = loss11.reshape(())
    return (loss, grad_x[None], *[out_g[n] for n in names], *[out_d[n] for n in names],
            *[out_m[n] for n in names], *[out_v[n] for n in names])
```

```python
import math

import jax
import jax.numpy as jnp
from jax import lax
from jax.experimental import pallas as pl
from jax.experimental.pallas import tpu as pltpu

F32, BF16 = jnp.float32, jnp.bfloat16
MESH = pl.DeviceIdType.MESH

LANES = 128
BF16_SUBLANES = 16
VMEM_LIMIT_BYTES = 56 * 1024 * 1024

EPS = 1e-6
ROPE_DIM = 64
ROPE_THETA = 10000.0
GRID_W = 64
HEAD_PAD = 256
ADAM_LR, ADAM_B1, ADAM_B2, ADAM_EPS, ADAM_WD, ADAM_STEP = 0.001, 0.9, 0.999, 1e-08, 0.01, 10


def _tile(dim, pref, align=LANES):
    if dim <= pref:
        return dim
    t = (pref // align) * align
    while t >= align:
        if dim % t == 0:
            return t
        t -= align
    return dim


def _params(sem=None):
    return pltpu.CompilerParams(dimension_semantics=sem, vmem_limit_bytes=VMEM_LIMIT_BYTES)


def _sds(shape, dtype):
    return jax.ShapeDtypeStruct(tuple(shape), dtype)


def _mm(pairs, *, name, ta=False, tb=False, outs=(F32,), tm=1024, tn=1024, tk=2048, extras=(), epi=None):
    dual = len(pairs[0]) == 3
    a0, b0 = pairs[0][0], pairs[0][1]
    M = a0.shape[1] if ta else a0.shape[0]
    N = b0.shape[0] if tb else b0.shape[1]
    tm, tn = _tile(M, tm), _tile(N, tn)
    ks = [(p[0].shape[0] if ta else p[0].shape[1]) for p in pairs]
    tks = [_tile(k, tk) for k in ks]
    nks = [k // t for k, t in zip(ks, tks)]
    offs = [sum(nks[:i]) for i in range(len(pairs))]
    nk_total = sum(nks)
    single = len(pairs) == 1

    def kidx(kk, p):
        return kk if single else jnp.clip(kk - offs[p], 0, nks[p] - 1)

    in_specs, operands = [], []
    for p, pr in enumerate(pairs):
        if ta:
            in_specs.append(pl.BlockSpec((tks[p], tm), lambda i, j, kk, p=p: (kidx(kk, p), i)))
        else:
            in_specs.append(pl.BlockSpec((tm, tks[p]), lambda i, j, kk, p=p: (i, kidx(kk, p))))
        operands.append(pr[0])
        for b in pr[1:]:
            if tb:
                in_specs.append(pl.BlockSpec((tn, tks[p]), lambda i, j, kk, p=p: (j, kidx(kk, p))))
            else:
                in_specs.append(pl.BlockSpec((tks[p], tn), lambda i, j, kk, p=p: (kidx(kk, p), j)))
            operands.append(b)
    for arr, kind in extras:
        if kind == "mn":
            in_specs.append(pl.BlockSpec((tm, tn), lambda i, j, kk: (i, j)))
        else:
            in_specs.append(pl.BlockSpec((1, tn), lambda i, j, kk: (0, j)))
        operands.append(arr)
    n_in = len(operands)
    n_ex = len(extras)
    per = 3 if dual else 2
    dims = (((0 if ta else 1,), (1 if tb else 0,)), ((), ()))

    n_acc = 2 if dual else 1

    def products(ins, p):
        a = ins[per * p][...].astype(BF16)
        return [lax.dot_general(a, ins[per * p + 1 + q][...].astype(BF16), dims, preferred_element_type=F32)
                for q in range(n_acc)]

    def finish(ins, out_refs, acc_vals):
        vals = acc_vals + [r[...] for r in ins[n_in - n_ex:]]
        res = epi(*vals) if epi is not None else (vals[0],)
        for o, r in zip(out_refs, res):
            o[...] = r.astype(o.dtype)

    def body(*refs):
        ins, out_refs, accs = refs[:n_in], refs[n_in:n_in + len(outs)], refs[n_in + len(outs):]
        if nk_total == 1:
            finish(ins, out_refs, products(ins, 0))
            return
        kk = pl.program_id(2)

        @pl.when(kk == 0)
        def _():
            for acc, v in zip(accs, products(ins, 0)):
                acc[...] = v

        for p in range(len(pairs)):
            lo = max(offs[p], 1)

            @pl.when((kk >= lo) & (kk < offs[p] + nks[p]))
            def _(p=p):
                for acc, v in zip(accs, products(ins, p)):
                    acc[...] += v

        @pl.when(kk == nk_total - 1)
        def _():
            finish(ins, out_refs, [acc[...] for acc in accs])

    res = pl.pallas_call(
        body, name=name, grid=(M // tm, N // tn, nk_total), in_specs=in_specs,
        out_specs=[pl.BlockSpec((tm, tn), lambda i, j, kk: (i, j)) for _ in outs],
        out_shape=[_sds((M, N), d) for d in outs],
        scratch_shapes=[pltpu.VMEM((tm, tn), F32) for _ in range(n_acc if nk_total > 1 else 0)],
        compiler_params=_params(("parallel", "parallel", "arbitrary")),
    )(*operands)
    return res[0] if len(outs) == 1 else res


def _rowwise(fn, rows, vecs, out_rows, out_accs=(), *, name, tm=256, tc=None):
    M = rows[0].shape[0]
    tm = _tile(M, tm, BF16_SUBLANES)
    nrow = M // tm
    C = rows[0].shape[1]
    ncol = 1 if tc is None else C // _tile(C, tc)
    tcol = None if tc is None else _tile(C, tc)

    def colwise(shape):
        return tc is not None and len(shape) == 2 and shape[0] == 1 and shape[1] == C

    def vspec(shape):
        if colwise(shape):
            return pl.BlockSpec((1, tcol), lambda j, i: (0, j))
        return pl.BlockSpec(tuple(shape), lambda j, i, n=len(shape): (0,) * n)

    def rspec(width):
        if tc is None:
            return pl.BlockSpec((tm, width), lambda j, i: (i, 0))
        return pl.BlockSpec((tm, tcol), lambda j, i: (i, j))

    in_specs = [rspec(r.shape[1]) for r in rows] + [vspec(v.shape) for v in vecs]
    out_specs = [rspec(c) for c, _ in out_rows] + [vspec(s) for s in out_accs]
    out_shape = [_sds((M, c), d) for c, d in out_rows] + [_sds(s, F32) for s in out_accs]
    n_in, n_or = len(rows) + len(vecs), len(out_rows)

    def body(*refs):
        ins, o_rows, o_accs = refs[:n_in], refs[n_in:n_in + n_or], refs[n_in + n_or:]
        r_out, a_out = fn(*[r[...] for r in ins])
        for o, r in zip(o_rows, r_out):
            o[...] = r.astype(o.dtype)
        i = pl.program_id(1)

        @pl.when(i == 0)
        def _():
            for o, a in zip(o_accs, a_out):
                o[...] = a

        @pl.when(i > 0)
        def _():
            for o, a in zip(o_accs, a_out):
                o[...] += a

    res = pl.pallas_call(
        body, name=name, grid=(ncol, nrow), in_specs=in_specs, out_specs=out_specs, out_shape=out_shape,
        compiler_params=_params(("parallel", "arbitrary")),
    )(*rows, *vecs)
    return res[:n_or], res[n_or:]


def _colsum(t):
    return jnp.sum(t, axis=0, keepdims=True)


def _gelu(t):
    return 0.5 * t * (1.0 + lax.erf(t * math.sqrt(0.5)))


def _gelu_grad(t):
    return 0.5 * (1.0 + lax.erf(t * math.sqrt(0.5))) + t * jnp.exp(-0.5 * t * t) * (1.0 / math.sqrt(2.0 * math.pi))


def _sigmoid(t):
    return 1.0 / (1.0 + jnp.exp(-t))


def _rms_stats(t, width):
    return lax.rsqrt(jnp.sum(t * t, axis=-1, keepdims=True) * (1.0 / width) + EPS)


def _rms_bwd(dn, tn, r, width):
    return r * (dn - tn * (jnp.sum(dn * tn, axis=-1, keepdims=True) * (1.0 / width)))


def _attn_fwd(q, k, v, *, heads, scale, tq=256):
    N, M = q.shape[0], k.shape[0]
    tq = _tile(N, tq)
    vd = v.shape[1] // heads

    def body(q_ref, k_ref, v_ref, o_ref, lse_ref):
        s = lax.dot_general(q_ref[...], k_ref[...], (((1,), (1,)), ((), ())), preferred_element_type=F32) * scale
        m = jnp.max(s, axis=-1, keepdims=True)
        p = jnp.exp(s - m)
        l = jnp.sum(p, axis=-1, keepdims=True)
        o = jnp.dot(p.astype(BF16), v_ref[...], preferred_element_type=F32) / l
        o_ref[...] = o.astype(o_ref.dtype)
        lse_ref[...] = jnp.broadcast_to(m + jnp.log(l), lse_ref.shape)

    return pl.pallas_call(
        body, name="attn_fwd", grid=(heads, N // tq),
        in_specs=[pl.BlockSpec((tq, HEAD_PAD), lambda h, i: (i, h)),
                  pl.BlockSpec((M, HEAD_PAD), lambda h, i: (0, h)),
                  pl.BlockSpec((M, vd), lambda h, i: (0, h))],
        out_specs=[pl.BlockSpec((tq, vd), lambda h, i: (i, h)), pl.BlockSpec((tq, vd), lambda h, i: (i, h))],
        out_shape=[_sds((N, heads * vd), BF16), _sds((N, heads * vd), F32)],
        compiler_params=_params(("parallel", "arbitrary")),
    )(q, k, v)


def _attn_bwd(q, k, v, o, lse, do, *, heads, scale, tq=256):
    N, M = q.shape[0], k.shape[0]
    tq = _tile(N, tq)
    vd = v.shape[1] // heads
    nt = (((1,), (1,)), ((), ()))
    tn = (((0,), (0,)), ((), ()))

    def body(q_ref, k_ref, v_ref, o_ref, lse_ref, do_ref, dq_ref, dk_ref, dv_ref):
        i = pl.program_id(1)
        qb, kb, dob = q_ref[...], k_ref[...], do_ref[...]
        s = lax.dot_general(qb, kb, nt, preferred_element_type=F32) * scale
        p = jnp.exp(s - lse_ref[...][:, :1])
        dp = lax.dot_general(dob, v_ref[...], nt, preferred_element_type=F32)
        delta = jnp.sum(dob.astype(F32) * o_ref[...].astype(F32), axis=-1, keepdims=True)
        ds = (p * (dp - delta) * scale).astype(BF16)
        pb = p.astype(BF16)
        dq_ref[...] = jnp.dot(ds, kb, preferred_element_type=F32)
        dk_part = lax.dot_general(ds, qb, tn, preferred_element_type=F32)
        dv_part = lax.dot_general(pb, dob, tn, preferred_element_type=F32)

        @pl.when(i == 0)
        def _():
            dk_ref[...] = dk_part
            dv_ref[...] = dv_part

        @pl.when(i > 0)
        def _():
            dk_ref[...] += dk_part
            dv_ref[...] += dv_part

    return pl.pallas_call(
        body, name="attn_bwd", grid=(heads, N // tq),
        in_specs=[pl.BlockSpec((tq, HEAD_PAD), lambda h, i: (i, h)),
                  pl.BlockSpec((M, HEAD_PAD), lambda h, i: (0, h)),
                  pl.BlockSpec((M, vd), lambda h, i: (0, h)),
                  pl.BlockSpec((tq, vd), lambda h, i: (i, h)),
                  pl.BlockSpec((tq, vd), lambda h, i: (i, h)),
                  pl.BlockSpec((tq, vd), lambda h, i: (i, h))],
        out_specs=[pl.BlockSpec((tq, HEAD_PAD), lambda h, i: (i, h)),
                   pl.BlockSpec((M, HEAD_PAD), lambda h, i: (0, h)),
                   pl.BlockSpec((M, vd), lambda h, i: (0, h))],
        out_shape=[_sds((N, heads * HEAD_PAD), F32), _sds((M, heads * HEAD_PAD), F32), _sds((M, heads * vd), F32)],
        compiler_params=_params(("parallel", "arbitrary")),
    )(q, k, v, o, lse, do)


def _place():
    return lax.axis_index("x"), lax.axis_index("y"), lax.axis_index("c")


def _comm_call(body, xs, out_shapes, n_sems, name, in_vmem):
    space = pltpu.VMEM if in_vmem else pl.ANY
    n = len(xs)

    def wrapped(*refs):
        body(refs[:n], refs[n:2 * n], *refs[2 * n:])

    return pl.pallas_call(
        wrapped, name=name, out_shape=list(out_shapes),
        in_specs=[pl.BlockSpec(memory_space=space)] * n, out_specs=[pl.BlockSpec(memory_space=space)] * n,
        scratch_shapes=[pltpu.SemaphoreType.DMA((n, n_sems)), pltpu.SemaphoreType.DMA((n, n_sems)),
                        pltpu.SemaphoreType.DMA((n,))],
        compiler_params=pltpu.CompilerParams(vmem_limit_bytes=VMEM_LIMIT_BYTES),
    )(*xs)


def _all_gather8(blks, *, name, in_vmem, others_only=False):
    def body(x_refs, out_refs, send_sems, recv_sems, local_sems):
        x, y, c = _place()
        me, sibling = (x, y, c), (x, y, 1 - c)
        chips = [(1 - x, y), (x, 1 - y), (1 - x, 1 - y)]
        waits = []
        for w, (x_ref, out_ref) in enumerate(zip(x_refs, out_refs)):
            def slot(px, py, pc, out_ref=out_ref):
                return out_ref.at[4 * px + 2 * py + pc]

            def copy(k, block, to, src=None, w=w, slot=slot):
                return pltpu.make_async_remote_copy(
                    src_ref=slot(*block) if src is None else src, dst_ref=slot(*block),
                    send_sem=send_sems.at[w, k], recv_sem=recv_sems.at[w, k], device_id=to, device_id_type=MESH)

            mine = None
            first = []
            if not others_only:
                mine = pltpu.make_async_copy(x_ref, slot(*me), local_sems.at[w])
                mine.start()
                first.append(copy(0, me, sibling, src=x_ref))
            first += [copy(1 + j, me, (*chip, c), src=x_ref) for j, chip in enumerate(chips)]
            for cp in first:
                cp.start()
            waits.append((copy, mine, first))
        for copy, mine, first in waits:
            passed = [copy(4 + j, (*chip, c), sibling) for j, chip in enumerate(chips)]
            for j, chip in enumerate(chips):
                copy(1 + j, (*chip, c), me).wait_recv()
                passed[j].start()
            if not others_only:
                copy(0, sibling, me).wait_recv()
            for j, chip in enumerate(chips):
                copy(4 + j, (*chip, 1 - c), me).wait_recv()
            for cp in first + passed:
                cp.wait_send()
            if mine is not None:
                mine.wait()

    return _comm_call(body, blks, [_sds((8,) + b.shape, b.dtype) for b in blks], 7, name, in_vmem)


def _pair_gather(blks, *, name):
    def body(x_refs, out_refs, send_sems, recv_sems, local_sems):
        x, y, c = _place()
        for w, (x_ref, out_ref) in enumerate(zip(x_refs, out_refs)):
            pltpu.make_async_remote_copy(src_ref=x_ref, dst_ref=out_ref.at[c], send_sem=send_sems.at[w, 0],
                                         recv_sem=recv_sems.at[w, 0], device_id=(x, y, 1 - c),
                                         device_id_type=MESH).start()
        for w, (x_ref, out_ref) in enumerate(zip(x_refs, out_refs)):
            cp = pltpu.make_async_remote_copy(src_ref=x_ref, dst_ref=out_ref.at[1 - c], send_sem=send_sems.at[w, 0],
                                              recv_sem=recv_sems.at[w, 0], device_id=(x, y, 1 - c), device_id_type=MESH)
            cp.wait_recv()
            cp.wait_send()

    return _comm_call(body, blks, [_sds((2,) + b.shape, b.dtype) for b in blks], 1, name, False)


def _pair_swap_other(bufs, *, name):
    def body(x_refs, out_refs, send_sems, recv_sems, local_sems):
        x, y, c = _place()
        for w, (x_ref, out_ref) in enumerate(zip(x_refs, out_refs)):
            for q in range(4):
                pltpu.make_async_remote_copy(src_ref=x_ref.at[q, 1 - c], dst_ref=out_ref.at[q],
                                             send_sem=send_sems.at[w, 0], recv_sem=recv_sems.at[w, 0],
                                             device_id=(x, y, 1 - c), device_id_type=MESH).start()
        for w, out_ref in enumerate(out_refs):
            pltpu.make_async_remote_copy(src_ref=out_ref, dst_ref=out_ref, send_sem=send_sems.at[w, 0],
                                         recv_sem=recv_sems.at[w, 0], device_id=(x, y, 1 - c),
                                         device_id_type=MESH).wait()

    return _comm_call(body, bufs, [_sds((4,) + b.shape[2:], b.dtype) for b in bufs], 1, name, False)


def _chip_exchange(bufs, *, name):
    def body(x_refs, out_refs, send_sems, recv_sems, local_sems):
        x, y, c = _place()
        p = 2 * x + y
        chips = [(1 - x, y), (x, 1 - y), (1 - x, 1 - y)]
        started = []
        for w, (x_ref, out_ref) in enumerate(zip(x_refs, out_refs)):
            sends = [pltpu.make_async_remote_copy(src_ref=x_ref.at[2 * qx + qy], dst_ref=out_ref.at[p],
                                                  send_sem=send_sems.at[w, k], recv_sem=recv_sems.at[w, k],
                                                  device_id=(qx, qy, c), device_id_type=MESH)
                     for k, (qx, qy) in enumerate(chips)]
            for cp in sends:
                cp.start()
            started.append(sends)
        for w, (x_ref, out_ref) in enumerate(zip(x_refs, out_refs)):
            for k, (qx, qy) in enumerate(chips):
                pltpu.make_async_remote_copy(src_ref=x_ref.at[p], dst_ref=out_ref.at[2 * qx + qy],
                                             send_sem=send_sems.at[w, k], recv_sem=recv_sems.at[w, k],
                                             device_id=(qx, qy, c), device_id_type=MESH).wait_recv()
            for cp in started[w]:
                cp.wait_send()

    return _comm_call(body, bufs, [_sds(b.shape, b.dtype) for b in bufs], 3, name, False)


def _block_rows(rows, row_bytes, target=1 << 20, align=BF16_SUBLANES):
    return _tile(rows, max(align, target // row_bytes // align * align), align)


def _sum_blocks(buf, *, name, out_dtype):
    B, R, C = buf.shape
    tm = _block_rows(R, B * C * buf.dtype.itemsize)

    def body(x_ref, o_ref):
        acc = x_ref[0].astype(F32)
        for b in range(1, B):
            acc = acc + x_ref[b].astype(F32)
        o_ref[...] = acc.astype(o_ref.dtype)

    return pl.pallas_call(
        body, name=name, grid=(R // tm,), in_specs=[pl.BlockSpec((B, tm, C), lambda i: (0, i, 0))],
        out_specs=pl.BlockSpec((tm, C), lambda i: (i, 0)), out_shape=_sds((R, C), out_dtype),
        compiler_params=_params(("parallel",)),
    )(buf)


def _pair_add(mine, theirs, core, *, name):
    _, _, R, C = mine.shape
    tm = _block_rows(R, C * 2)

    def body(core_ref, a_ref, b_ref, o_ref):
        o_ref[...] = (a_ref[...].astype(F32) + b_ref[...].astype(F32)).astype(o_ref.dtype)

    return pl.pallas_call(
        body, name=name, out_shape=_sds(theirs.shape, BF16),
        grid_spec=pltpu.PrefetchScalarGridSpec(
            num_scalar_prefetch=1, grid=(4, R // tm),
            in_specs=[pl.BlockSpec((None, None, tm, C), lambda q, i, core_ref: (q, core_ref[0], i, 0)),
                      pl.BlockSpec((None, tm, C), lambda q, i, core_ref: (q, i, 0))],
            out_specs=pl.BlockSpec((None, tm, C), lambda q, i, core_ref: (q, i, 0))),
        compiler_params=_params(("parallel", "parallel")),
    )(core, mine, theirs)


def _split_lanes(row, widths):
    out, off = [], 0
    for wd in widths:
        out.append(row[:, off:off + wd])
        off += wd
    return out


def _adamw(w, g, m, v, *, name):
    C = w.shape[1]

    def fn(w, g, m, v):
        m = ADAM_B1 * m + (1.0 - ADAM_B1) * g
        v = ADAM_B2 * v + (1.0 - ADAM_B2) * (g * g)
        m_hat = m / (1.0 - ADAM_B1 ** ADAM_STEP)
        v_hat = v / (1.0 - ADAM_B2 ** ADAM_STEP)
        delta = -ADAM_LR * (m_hat / (jnp.sqrt(v_hat) + ADAM_EPS) + ADAM_WD * w)
        return (delta, m, v), ()

    tm = max(8, min(512, (1 << 20) // (4 * C) // 8 * 8))
    (d, nm, nv), _ = _rowwise(fn, [w, g, m, v], [], [(C, F32)] * 3, name=name, tm=tm)
    return d, nm, nv


def _rope_tables(n):
    rows = n // GRID_W
    row = jnp.repeat(jnp.arange(rows, dtype=F32), GRID_W)
    col = jnp.tile(jnp.arange(GRID_W, dtype=F32), rows)
    nf = ROPE_DIM // 4
    freqs = ROPE_THETA ** (-jnp.arange(nf, dtype=F32) / nf)
    ang_r, ang_c = row[:, None] * freqs[None, :], col[:, None] * freqs[None, :]
    cr, sr, cc, sc = jnp.cos(ang_r), jnp.sin(ang_r), jnp.cos(ang_c), jnp.sin(ang_c)
    nope = HEAD_PAD - 2 * ROPE_DIM
    one, zero, z = jnp.ones((n, nope), F32), jnp.zeros((n, nope), F32), jnp.zeros((n, nf), F32)
    pad = jnp.zeros((n, ROPE_DIM), F32)
    cos = jnp.concatenate([one, cr, cr, cc, cc, pad], axis=1)
    s_lo = jnp.concatenate([zero, -sr, z, -sc, z, pad], axis=1)
    s_hi = jnp.concatenate([zero, z, sr, z, sc, pad], axis=1)
    return cos, s_lo, s_hi


def _rope(n, cos, s_lo, s_hi):
    q = ROPE_DIM // 4
    return n * cos + pltpu.roll(n, HEAD_PAD - q, 1) * s_lo + pltpu.roll(n, q, 1) * s_hi


def _rope_t(d, cos, s_lo, s_hi):
    q = ROPE_DIM // 4
    return d * cos + pltpu.roll(d * s_lo, q, 1) + pltpu.roll(d * s_hi, HEAD_PAD - q, 1)


def kernel(x, c, ctx, c_ctx, w_mod, b_mod, norm1_g, w_in, q_norm_g, kv_norm_g, w_uq, w_ukv, qk_norm_q, qk_norm_k, sgu_norm_g, sgu_norm_b, w_spatial, b_spatial, w_br_attn, w_br_sgu, w_out, norm2_g, w_ffn_in, w_ffn_out, loss_target, m_c_ctx, m_w_mod, m_b_mod, m_norm1_g, m_w_in, m_q_norm_g, m_kv_norm_g, m_w_uq, m_w_ukv, m_qk_norm_q, m_qk_norm_k, m_sgu_norm_g, m_sgu_norm_b, m_w_spatial, m_b_spatial, m_w_br_attn, m_w_br_sgu, m_w_out, m_norm2_g, m_w_ffn_in, m_w_ffn_out, v_c_ctx, v_w_mod, v_b_mod, v_norm1_g, v_w_in, v_q_norm_g, v_kv_norm_g, v_w_uq, v_w_ukv, v_qk_norm_q, v_qk_norm_k, v_sgu_norm_g, v_sgu_norm_b, v_w_spatial, v_b_spatial, v_w_br_attn, v_w_br_sgu, v_w_out, v_norm2_g, v_w_ffn_in, v_w_ffn_out):
    ax, ay, ac = _place()
    my_chip = 2 * ax + ay
    my_dev = 4 * ax + 2 * ay + ac

    N, D = x.shape[1], x.shape[2]
    CT = ctx.shape[1]
    M = N + CT
    QL, KVL, QK = q_norm_g.shape[-1], kv_norm_g.shape[-1], qk_norm_q.shape[-1]
    NOPE = QK - ROPE_DIM
    VD = NOPE
    H = 4 * w_uq.shape[-1] // QK
    SW, G, CH = sgu_norm_g.shape[-1], w_spatial.shape[1], w_spatial.shape[2]
    GD = SW // G
    DFF = 4 * w_ffn_out.shape[1]
    NMOD = 4 * w_mod.shape[-1]
    NM = w_mod.shape[-1]
    KVP = KVL + 2 * ROPE_DIM
    assert NOPE == LANES and GD == LANES and HEAD_PAD == NOPE + 2 * ROPE_DIM and CH == LANES
    scale = QK ** -0.5

    x2, ctx2, tgt2 = x[0], ctx[0], loss_target[0]

    c_all = _all_gather8([c], name="ag_c", in_vmem=True)[0][:, 0, :]
    c_rows = jnp.concatenate([c_all, c_ctx[None, :], jnp.zeros((BF16_SUBLANES - 9, D), F32)], axis=0)

    def silu_fn(t):
        s = _sigmoid(t)
        return (t * s, s * (1.0 + t * (1.0 - s))), ()

    (silu_c, dsilu_c), _ = _rowwise(silu_fn, [c_rows], [], [(D, F32), (D, F32)], name="silu_c", tm=16)
    wm = w_mod[0]
    mod_loc = _mm([(silu_c, wm)], name="mod_fwd", outs=(F32,), tn=512, tk=512,
                  extras=[(lax.dynamic_slice_in_dim(b_mod, my_chip * NM, NM, axis=1), "n")],
                  epi=lambda acc, b: (acc + b,))
    mod_all = _all_gather8([mod_loc], name="ag_mod", in_vmem=True)[0]
    mod_full = jnp.concatenate([mod_all[0], mod_all[2], mod_all[4], mod_all[6]], axis=1)
    mod_me = lax.dynamic_slice_in_dim(mod_full, my_dev, 1, axis=0)
    sh1, sc1, g1, sh2, sc2, g2 = [mod_me[:, i * D:(i + 1) * D] for i in range(6)]
    sh1c, sc1c = mod_full[8:9, :D], mod_full[8:9, D:2 * D]

    big = [w_in[0], w_uq[0], w_ukv[0], w_br_attn[0], w_br_sgu[0], w_out[0], w_ffn_in[0], w_ffn_out[0]]
    col_sharded = [True, True, True, True, True, False, True, False]
    halves = [lax.dynamic_slice_in_dim(a, ac * (a.shape[0] // 2), a.shape[0] // 2, axis=0).astype(BF16) for a in big]
    gathered = _all_gather8(halves, name="ag_weights", in_vmem=False, others_only=True)
    full = []
    for seg, a, cs in zip(gathered, big, col_sharded):
        seg = lax.dynamic_update_index_in_dim(seg.reshape((4,) + a.shape), a.astype(BF16), my_chip, axis=0)
        full.append(seg.transpose(1, 0, 2).reshape(a.shape[0], 4 * a.shape[1]) if cs else seg.reshape(4 * a.shape[0], a.shape[1]))
    w_in_f, w_uq_f, w_ukv_f, w_bra, w_brs, w_out_f, w_ffi, w_ffo = full
    o_kv, o_u = QL, QL + KVL + ROPE_DIM
    o_v, o_g = o_u + SW, o_u + 2 * SW
    w_q = w_in_f[:, :QL]
    w_kv = jnp.pad(w_in_f[:, o_kv:o_u], ((0, 0), (0, ROPE_DIM)))
    w_u, w_v = w_in_f[:, o_u:o_v], w_in_f[:, o_v:o_g]
    w_g1, w_g2 = w_in_f[:, o_g:o_g + D], w_in_f[:, o_g + D:]
    w_uq_p = jnp.pad(w_uq_f.reshape(QL, H, QK), ((0, 0), (0, 0), (0, HEAD_PAD - QK))).reshape(QL, H * HEAD_PAD)
    w_fa, w_fb = w_ffi[:, :DFF], w_ffi[:, DFF:]

    cos_t, slo_t, shi_t = _rope_tables(N)
    ones_c = jnp.concatenate([jnp.ones((CT, NOPE + ROPE_DIM), F32), jnp.zeros((CT, ROPE_DIM), F32)], axis=1)
    cos_k = jnp.concatenate([cos_t, ones_c], axis=0)
    slo_k = jnp.concatenate([slo_t, jnp.zeros((CT, HEAD_PAD), F32)], axis=0)
    shi_k = jnp.concatenate([shi_t, jnp.zeros((CT, HEAD_PAD), F32)], axis=0)
    gq_p = jnp.pad(qk_norm_q, ((0, 0), (0, HEAD_PAD - QK)))
    gk_p = jnp.pad(qk_norm_k, ((0, 0), (0, HEAD_PAD - QK)))

    def norm_mod_fn(t, g, sh, sc):
        r = _rms_stats(t, D)
        return (((t * r) * g) * (1.0 + sc) + sh,), ()

    (h,), _ = _rowwise(norm_mod_fn, [x2], [norm1_g, sh1, sc1], [(D, BF16)], name="norm1_x")
    (ctx_h,), _ = _rowwise(norm_mod_fn, [ctx2], [norm1_g, sh1c, sc1c], [(D, BF16)], name="norm1_ctx")

    qc = _mm([(h, w_q)], name="proj_q", outs=(F32,))
    kvin = jnp.concatenate([_mm([(h, w_kv)], name="proj_kv", outs=(F32,)),
                            _mm([(ctx_h, w_kv)], name="proj_kv_ctx", outs=(F32,))], axis=0)
    u_in = _mm([(h, w_u)], name="proj_u", outs=(BF16,))
    v_in = _mm([(h, w_v)], name="proj_v", outs=(BF16,))
    g1_in = _mm([(h, w_g1)], name="proj_g1", outs=(BF16,))
    g2_in = _mm([(h, w_g2)], name="proj_g2", outs=(BF16,))

    def rms_gain_fn(width):
        def fn(t, g):
            return (((t * _rms_stats(t, width)) * g),), ()
        return fn

    (qn,), _ = _rowwise(rms_gain_fn(QL), [qc], [q_norm_g], [(QL, BF16)], name="q_norm")

    def kv_norm_fn(t, g):
        kvc = t[:, :KVL]
        return (((kvc * _rms_stats(kvc, KVL)) * g),), ()

    (kvn,), _ = _rowwise(kv_norm_fn, [kvin], [kv_norm_g], [(KVL, BF16)], name="kv_norm")
    q_raw = _mm([(qn, w_uq_p)], name="q_up", outs=(F32,))
    kv_raw = _mm([(kvn, w_ukv_f)], name="kv_up", outs=(F32,))

    def q_post_fn(t, cos, slo, shi, g):
        outs = []
        for hd in range(H):
            th = t[:, hd * HEAD_PAD:(hd + 1) * HEAD_PAD]
            outs.append(_rope((th * _rms_stats(th, QK)) * g, cos, slo, shi))
        return (jnp.concatenate(outs, axis=1),), ()

    (q_att,), _ = _rowwise(q_post_fn, [q_raw, cos_t, slo_t, shi_t], [gq_p], [(H * HEAD_PAD, BF16)], name="q_post")

    def k_post_fn(t, kvi, cos, slo, shi, g):
        kr = kvi[:, KVL:]
        ks, vs = [], []
        for hd in range(H):
            th = jnp.concatenate([t[:, hd * HEAD_PAD:hd * HEAD_PAD + NOPE], kr], axis=1)
            ks.append(_rope((th * _rms_stats(th, QK)) * g, cos, slo, shi))
            vs.append(t[:, hd * HEAD_PAD + NOPE:(hd + 1) * HEAD_PAD])
        return (jnp.concatenate(ks, axis=1), jnp.concatenate(vs, axis=1)), ()

    (k_att, v_att), _ = _rowwise(k_post_fn, [kv_raw, kvin, cos_k, slo_k, shi_k], [gk_p],
                                 [(H * HEAD_PAD, BF16), (H * VD, BF16)], name="k_post")
    attn_o, lse = _attn_fwd(q_att, k_att, v_att, heads=H, scale=scale)

    ws3 = w_spatial[0]
    bs_t = jnp.pad(b_spatial[0].T, ((0, 0), (0, LANES - G)))

    def sgu_parts(u_in, v_in, ng, nb):
        u, v = _gelu(u_in.astype(F32)), _gelu(v_in.astype(F32))
        mu = jnp.mean(v, axis=-1, keepdims=True)
        vc = v - mu
        rs = lax.rsqrt(jnp.mean(vc * vc, axis=-1, keepdims=True) + EPS)
        xhat = vc * rs
        return u, xhat, rs, (xhat * ng + nb).astype(BF16)

    def sgu_fwd_fn(u_in, v_in, ng, nb, ws, bst):
        u, _, _, vnb = sgu_parts(u_in, v_in, ng, nb)
        outs = []
        for g in range(G):
            sl = slice(g * GD, (g + 1) * GD)
            mixed = jnp.dot(ws[g].astype(BF16), vnb[:, sl], preferred_element_type=F32) + bst[:, g:g + 1]
            outs.append(u[:, sl] * mixed)
        return (jnp.concatenate(outs, axis=1),), ()

    (sgu_o,), _ = _rowwise(sgu_fwd_fn, [u_in, v_in], [sgu_norm_g, sgu_norm_b, ws3, bs_t], [(SW, BF16)],
                           name="sgu_fwd", tm=CH)

    a1 = _mm([(attn_o, w_bra)], name="br_attn", outs=(BF16,))
    a2 = _mm([(sgu_o, w_brs)], name="br_sgu", outs=(BF16,))

    def merge_fn(a1, a2, gi1, gi2):
        return ((_sigmoid(gi1.astype(F32)) * a1.astype(F32) + _sigmoid(gi2.astype(F32)) * a2.astype(F32)),), ()

    (merged,), _ = _rowwise(merge_fn, [a1, a2, g1_in, g2_in], [], [(D, BF16)], name="merge", tc=1024)

    def res_gate(acc, res, gate):
        return res + gate * acc, acc

    x1, mo = _mm([(merged, w_out_f)], name="out_proj", outs=(F32, BF16), tn=512,
                 extras=[(x2, "mn"), (g1, "n")], epi=res_gate)
    (h2,), _ = _rowwise(norm_mod_fn, [x1], [norm2_g, sh2, sc2], [(D, BF16)], name="norm2")

    def swiglu_epi(a, b):
        return a, b, (a * _sigmoid(a)) * b

    fa, fb, act = _mm([(h2, w_fa, w_fb)], name="ffn_in", outs=(BF16, BF16, BF16), tn=512, epi=swiglu_epi)
    y, f_out = _mm([(act, w_ffo)], name="ffn_out", outs=(F32, BF16), tn=512,
                   extras=[(x1, "mn"), (g2, "n")], epi=res_gate)

    def loss_fn(y, t, fo, g2v):
        e = y - t
        dy = e * (1.0 / D)
        return (dy, g2v * dy), (_colsum(e * e) * (0.5 / D), _colsum(dy * fo.astype(F32)))

    (dy, df), (loss_cols, dg2) = _rowwise(loss_fn, [y, tgt2, f_out], [g2], [(D, F32), (D, BF16)],
                                          [(1, D), (1, D)], name="loss")

    def swiglu_bwd_epi(dact, a, b):
        a, b = a.astype(F32), b.astype(F32)
        s = _sigmoid(a)
        return dact * b * (s * (1.0 + a * (1.0 - s))), dact * (a * s)

    da, db = _mm([(df, w_ffo)], tb=True, name="ffn_out_dx", outs=(BF16, BF16), tn=512,
                 extras=[(fa, "mn"), (fb, "mn")], epi=swiglu_bwd_epi)
    dw_ffo = _mm([(act, df)], ta=True, name="ffn_out_dw", outs=(BF16,))
    dh2 = _mm([(da, w_fa), (db, w_fb)], tb=True, name="ffn_in_dx", outs=(F32,))
    dw_fa = _mm([(h2, da)], ta=True, name="ffn_in_dw_a", outs=(BF16,))
    dw_fb = _mm([(h2, db)], ta=True, name="ffn_in_dw_b", outs=(BF16,))

    def norm2_bwd_fn(dh, t, dyv, mov, g, sc, g1v):
        r = _rms_stats(t, D)
        tn = t * r
        dxg = dh * (1.0 + sc)
        dt = dyv + _rms_bwd(dxg * g, tn, r, D)
        return (dt, g1v * dt), (_colsum(dh), _colsum(dh * (tn * g)), _colsum(dxg * tn), _colsum(dt * mov.astype(F32)))

    (dx1, dmo), (dsh2, dsc2, dn2g, dg1) = _rowwise(
        norm2_bwd_fn, [dh2, x1, dy, mo], [norm2_g, sc2, g1], [(D, F32), (D, BF16)], [(1, D)] * 4, name="norm2_bwd")

    def merge_bwd_epi(dm, a1, a2, gi1, gi2):
        s1, s2 = _sigmoid(gi1.astype(F32)), _sigmoid(gi2.astype(F32))
        a1, a2 = a1.astype(F32), a2.astype(F32)
        return dm * s1, dm * s2, dm * a1 * (s1 * (1.0 - s1)), dm * a2 * (s2 * (1.0 - s2))

    da1, da2, dgi1, dgi2 = _mm([(dmo, w_out_f)], tb=True, name="out_proj_dx", outs=(BF16,) * 4, tn=512,
                               extras=[(a1, "mn"), (a2, "mn"), (g1_in, "mn"), (g2_in, "mn")], epi=merge_bwd_epi)
    dw_out = _mm([(merged, dmo)], ta=True, name="out_proj_dw", outs=(BF16,))
    dattn = _mm([(da1, w_bra)], tb=True, name="br_attn_dx", outs=(BF16,))
    dw_bra = _mm([(attn_o, da1)], ta=True, name="br_attn_dw", outs=(BF16,))
    dsgu = _mm([(da2, w_brs)], tb=True, name="br_sgu_dx", outs=(BF16,))
    dw_brs = _mm([(sgu_o, da2)], ta=True, name="br_sgu_dw", outs=(BF16,))

    def sgu_bwd_fn(dso, u_in, v_in, ng, nb, ws, bst):
        u, xhat, rs, vnb = sgu_parts(u_in, v_in, ng, nb)
        dso = dso.astype(F32)
        lane = lax.broadcasted_iota(jnp.int32, (CH, LANES), 1)
        du, dvn, dws, dbs = [], [], [], jnp.zeros((CH, LANES), F32)
        for g in range(G):
            sl = slice(g * GD, (g + 1) * GD)
            wg = ws[g].astype(BF16)
            mixed = jnp.dot(wg, vnb[:, sl], preferred_element_type=F32) + bst[:, g:g + 1]
            du.append(dso[:, sl] * mixed)
            dmix = dso[:, sl] * u[:, sl]
            dmb = dmix.astype(BF16)
            dws.append(lax.dot_general(dmb, vnb[:, sl], (((1,), (1,)), ((), ())), preferred_element_type=F32))
            dbs = dbs + jnp.where(lane == g, jnp.sum(dmix, axis=1, keepdims=True), 0.0)
            dvn.append(lax.dot_general(wg, dmb, (((0,), (0,)), ((), ())), preferred_element_type=F32))
        du, dvn = jnp.concatenate(du, axis=1), jnp.concatenate(dvn, axis=1)
        dxh = dvn * ng
        dv = rs * (dxh - jnp.mean(dxh, axis=-1, keepdims=True) - xhat * jnp.mean(dxh * xhat, axis=-1, keepdims=True))
        return ((du * _gelu_grad(u_in.astype(F32)), dv * _gelu_grad(v_in.astype(F32))),
                (_colsum(dvn * xhat), _colsum(dvn), jnp.stack(dws), dbs))

    (du_in, dv_in), (d_sng, d_snb, d_ws, d_bs) = _rowwise(
        sgu_bwd_fn, [dsgu, u_in, v_in], [sgu_norm_g, sgu_norm_b, ws3, bs_t], [(SW, BF16), (SW, BF16)],
        [(1, SW), (1, SW), (G, CH, CH), (CH, LANES)], name="sgu_bwd", tm=CH)

    dq_att, dk_att, dv_att = _attn_bwd(q_att, k_att, v_att, attn_o, lse, dattn, heads=H, scale=scale)

    def q_post_bwd_fn(dq, t, cos, slo, shi, g):
        outs, dg = [], jnp.zeros((1, HEAD_PAD), F32)
        for hd in range(H):
            sl = slice(hd * HEAD_PAD, (hd + 1) * HEAD_PAD)
            th = t[:, sl]
            r = _rms_stats(th, QK)
            tn = th * r
            dn = _rope_t(dq[:, sl], cos, slo, shi)
            dg = dg + _colsum(dn * tn)
            outs.append(_rms_bwd(dn * g, tn, r, QK))
        return (jnp.concatenate(outs, axis=1),), (dg,)

    (dq_raw,), (d_gq,) = _rowwise(q_post_bwd_fn, [dq_att, q_raw, cos_t, slo_t, shi_t], [gq_p],
                                  [(H * HEAD_PAD, BF16)], [(1, HEAD_PAD)], name="q_post_bwd")

    def k_post_bwd_fn(dk, dv, t, kvi, cos, slo, shi, g):
        kr = kvi[:, KVL:]
        outs, dg, dkr = [], jnp.zeros((1, HEAD_PAD), F32), jnp.zeros_like(kr)
        for hd in range(H):
            th = jnp.concatenate([t[:, hd * HEAD_PAD:hd * HEAD_PAD + NOPE], kr], axis=1)
            r = _rms_stats(th, QK)
            tn = th * r
            dn = _rope_t(dk[:, hd * HEAD_PAD:(hd + 1) * HEAD_PAD], cos, slo, shi)
            dg = dg + _colsum(dn * tn)
            dt = _rms_bwd(dn * g, tn, r, QK)
            dkr = dkr + dt[:, NOPE:]
            outs += [dt[:, :NOPE], dv[:, hd * VD:(hd + 1) * VD]]
        return (jnp.concatenate(outs, axis=1), dkr), (dg,)

    (dkv_raw, dkrope), (d_gk,) = _rowwise(
        k_post_bwd_fn, [dk_att, dv_att, kv_raw, kvin, cos_k, slo_k, shi_k], [gk_p],
        [(H * HEAD_PAD, BF16), (2 * ROPE_DIM, F32)], [(1, HEAD_PAD)], name="k_post_bwd")

    dqn = _mm([(dq_raw, w_uq_p)], tb=True, name="q_up_dx", outs=(F32,))
    dw_uq_p = _mm([(qn, dq_raw)], ta=True, name="q_up_dw", outs=(BF16,))
    dkvn = _mm([(dkv_raw, w_ukv_f)], tb=True, name="kv_up_dx", outs=(F32,))
    dw_ukv = _mm([(kvn, dkv_raw)], ta=True, name="kv_up_dw", outs=(BF16,))

    def q_norm_bwd_fn(dn, t, g):
        r = _rms_stats(t, QL)
        tn = t * r
        return (_rms_bwd(dn * g, tn, r, QL),), (_colsum(dn * tn),)

    (dqc,), (d_qng,) = _rowwise(q_norm_bwd_fn, [dqn, qc], [q_norm_g], [(QL, BF16)], [(1, QL)], name="q_norm_bwd")

    def kv_norm_bwd_fn(dn, dkr, t, g):
        kvc = t[:, :KVL]
        r = _rms_stats(kvc, KVL)
        tn = kvc * r
        return (jnp.concatenate([_rms_bwd(dn * g, tn, r, KVL), dkr], axis=1),), (_colsum(dn * tn),)

    (dkvin,), (d_kvng,) = _rowwise(kv_norm_bwd_fn, [dkvn, dkrope, kvin], [kv_norm_g], [(KVP, BF16)], [(1, KVL)],
                                   name="kv_norm_bwd")
    dkvin_x, dkvin_c = dkvin[:N], dkvin[N:]

    dh = _mm([(dqc, w_q), (dkvin_x, w_kv), (du_in, w_u), (dv_in, w_v), (dgi1, w_g1), (dgi2, w_g2)], tb=True,
             name="proj_dx", outs=(F32,), tn=512, tk=1024)
    dctx_h = _mm([(dkvin_c, w_kv)], tb=True, name="proj_kv_ctx_dx", outs=(F32,))
    dw_q = _mm([(h, dqc)], ta=True, name="proj_q_dw", outs=(BF16,))
    dw_kv = _mm([(h, dkvin_x), (ctx_h, dkvin_c)], ta=True, name="proj_kv_dw", outs=(BF16,))
    dw_u = _mm([(h, du_in)], ta=True, name="proj_u_dw", outs=(BF16,))
    dw_v = _mm([(h, dv_in)], ta=True, name="proj_v_dw", outs=(BF16,))
    dw_g1 = _mm([(h, dgi1)], ta=True, name="proj_g1_dw", outs=(BF16,))
    dw_g2 = _mm([(h, dgi2)], ta=True, name="proj_g2_dw", outs=(BF16,))

    def norm1_bwd_fn(dhv, t, dres, g, sc):
        r = _rms_stats(t, D)
        tn = t * r
        dxg = dhv * (1.0 + sc)
        return (dres + _rms_bwd(dxg * g, tn, r, D),), (_colsum(dhv), _colsum(dhv * (tn * g)), _colsum(dxg * tn))

    (grad_x,), (dsh1, dsc1, dn1g_x) = _rowwise(norm1_bwd_fn, [dh, x2, dx1], [norm1_g, sc1], [(D, F32)], [(1, D)] * 3,
                                               name="norm1_bwd")
    _, (dsh1c, dsc1c, dn1g_c) = _rowwise(norm1_bwd_fn, [dctx_h, ctx2, jnp.zeros_like(ctx2)], [norm1_g, sc1c],
                                         [(D, F32)], [(1, D)] * 3, name="norm1_ctx_bwd")

    small = [dsh1, dsc1, dg1, dsh2, dsc2, dg2,
             dsh1c, dsc1c, dn1g_x, dn1g_c, d_qng, d_kvng, d_gq, d_gk, d_sng, d_snb, dn2g, loss_cols]
    small_sizes = [a.shape[1] for a in small]
    sm_row = jnp.concatenate(small, axis=1)
    sm_mat = jnp.concatenate([d_ws.reshape(G * CH, CH), d_bs], axis=0)
    row_all, mat_all = _all_gather8([sm_row, sm_mat], name="ag_small", in_vmem=True)
    row_sum = _sum_blocks(row_all, name="sum_small_rows", out_dtype=F32)
    mat_sum = _sum_blocks(mat_all, name="sum_small_mats", out_dtype=F32)
    dmod_rows = row_all[:, 0, :NMOD]
    (_, _, _, _, _, _, t_sh1c, t_sc1c, t_n1x, t_n1c, g_qng, g_kvng, t_gq, t_gk, g_sng, g_snb, g_n2g,
     t_loss) = _split_lanes(row_sum, small_sizes)
    g_ws, t_bs = mat_sum[:G * CH], mat_sum[G * CH:]
    dmodc_row = jnp.concatenate([t_sh1c, t_sc1c, jnp.zeros((1, NMOD - 2 * D), F32)], axis=1)
    dmod16 = jnp.concatenate([dmod_rows, dmodc_row, jnp.zeros((BF16_SUBLANES - 9, NMOD), F32)], axis=0)

    def small_fn(rows, n1x, n1c, lossv):
        return (), (_colsum(rows), n1x + n1c, jnp.sum(lossv, axis=1, keepdims=True))

    _, (g_bmod, g_n1g, loss11) = _rowwise(small_fn, [dmod16], [t_n1x, t_n1c, t_loss], [], [(1, NMOD), (1, D), (1, 1)],
                                          name="small_reduce", tm=16)
    dmod_loc = lax.dynamic_slice_in_dim(dmod16, my_chip * NM, NM, axis=1)
    g_wmod = _mm([(silu_c, dmod_loc)], ta=True, name="mod_dw", outs=(F32,), tn=512)
    dsilu_part = _mm([(dmod_loc, wm)], tb=True, name="mod_dx", outs=(F32,), tk=512)
    part_all = _all_gather8([dsilu_part[8:9]], name="ag_cctx", in_vmem=True)[0]

    def cctx_fn(parts, dsl):
        return (), ((parts[0:1] + parts[2:3] + parts[4:5] + parts[6:7]) * dsl,)

    _, (g_cctx,) = _rowwise(cctx_fn, [part_all[:, 0, :]], [dsilu_c[8:9]], [], [(1, D)], name="cctx_grad", tm=8)

    dw_in_f = jnp.concatenate([dw_q, dw_kv[:, :KVL + ROPE_DIM], dw_u, dw_v, dw_g1, dw_g2], axis=1)
    dw_uq_f = dw_uq_p.reshape(QL, H, HEAD_PAD)[:, :, :QK].reshape(QL, H * QK)
    dw_ffi = jnp.concatenate([dw_fa, dw_fb], axis=1)
    dfull = [dw_in_f, dw_uq_f, dw_ukv, dw_bra, dw_brs, dw_out, dw_ffi, dw_ffo]
    tags = ["w_in", "w_uq", "w_ukv", "w_br_attn", "w_br_sgu", "w_out", "w_ffn_in", "w_ffn_out"]
    core = jnp.reshape(ac, (1,)).astype(jnp.int32)
    g4 = []
    for dwf, a, cs in zip(dfull, big, col_sharded):
        K, Ns = a.shape
        t = dwf.reshape(K, 4, Ns).transpose(1, 0, 2) if cs else dwf.reshape(4, K, Ns)
        g4.append(t.reshape(4, 2, K // 2, Ns))
    sib = _pair_swap_other(g4, name="rs_pair")
    pair = [_pair_add(g, s, core, name="rs_pair_add_" + t) for g, s, t in zip(g4, sib, tags)]
    xchg = [lax.dynamic_update_index_in_dim(t4, lax.dynamic_index_in_dim(pr, my_chip, 0, keepdims=False), my_chip, 0)
            for t4, pr in zip(_chip_exchange(pair, name="rs_chips"), pair)]
    red_half = [_sum_blocks(t4, name="rs_sum_" + t, out_dtype=F32) for t4, t in zip(xchg, tags)]
    big_grads = [lax.dynamic_update_index_in_dim(r, mine, ac, 0).reshape(a.shape)
                 for r, mine, a in zip(_pair_gather(red_half, name="rs_halves"), red_half, big)]

    def upd(w, g, m, v, nm):
        shape = w.shape
        w2, g2_, m2, v2 = [t.reshape(-1, shape[-1]) for t in (w, g, m, v)]
        d_, m_, v_ = _adamw(w2, g2_, m2, v2, name="adamw_" + nm)
        return g.reshape(shape), d_.reshape(shape), m_.reshape(shape), v_.reshape(shape)

    g_in, g_uq, g_ukv, g_bra, g_brs, g_out, g_ffi, g_ffo = big_grads
    grads = dict(
        c_ctx=g_cctx.reshape(D), w_mod=g_wmod[None], b_mod=g_bmod, norm1_g=g_n1g, w_in=g_in[None],
        q_norm_g=g_qng, kv_norm_g=g_kvng, w_uq=g_uq[None], w_ukv=g_ukv[None],
        qk_norm_q=t_gq[:, :QK], qk_norm_k=t_gk[:, :QK], sgu_norm_g=g_sng, sgu_norm_b=g_snb,
        w_spatial=g_ws.reshape(w_spatial.shape), b_spatial=t_bs[:, :G].T[None],
        w_br_attn=g_bra[None], w_br_sgu=g_brs[None], w_out=g_out[None], norm2_g=g_n2g,
        w_ffn_in=g_ffi[None], w_ffn_out=g_ffo[None])
    weights = dict(c_ctx=c_ctx, w_mod=w_mod, b_mod=b_mod, norm1_g=norm1_g, w_in=w_in, q_norm_g=q_norm_g,
                   kv_norm_g=kv_norm_g, w_uq=w_uq, w_ukv=w_ukv, qk_norm_q=qk_norm_q, qk_norm_k=qk_norm_k,
                   sgu_norm_g=sgu_norm_g, sgu_norm_b=sgu_norm_b, w_spatial=w_spatial, b_spatial=b_spatial,
                   w_br_attn=w_br_attn, w_br_sgu=w_br_sgu, w_out=w_out, norm2_g=norm2_g, w_ffn_in=w_ffn_in,
                   w_ffn_out=w_ffn_out)
    m_in = dict(c_ctx=m_c_ctx, w_mod=m_w_mod, b_mod=m_b_mod, norm1_g=m_norm1_g, w_in=m_w_in, q_norm_g=m_q_norm_g,
                kv_norm_g=m_kv_norm_g, w_uq=m_w_uq, w_ukv=m_w_ukv, qk_norm_q=m_qk_norm_q, qk_norm_k=m_qk_norm_k,
                sgu_norm_g=m_sgu_norm_g, sgu_norm_b=m_sgu_norm_b, w_spatial=m_w_spatial, b_spatial=m_b_spatial,
                w_br_attn=m_w_br_attn, w_br_sgu=m_w_br_sgu, w_out=m_w_out, norm2_g=m_norm2_g, w_ffn_in=m_w_ffn_in,
                w_ffn_out=m_w_ffn_out)
    v_in_ = dict(c_ctx=v_c_ctx, w_mod=v_w_mod, b_mod=v_b_mod, norm1_g=v_norm1_g, w_in=v_w_in, q_norm_g=v_q_norm_g,
                 kv_norm_g=v_kv_norm_g, w_uq=v_w_uq, w_ukv=v_w_ukv, qk_norm_q=v_qk_norm_q, qk_norm_k=v_qk_norm_k,
                 sgu_norm_g=v_sgu_norm_g, sgu_norm_b=v_sgu_norm_b, w_spatial=v_w_spatial, b_spatial=v_b_spatial,
                 w_br_attn=v_w_br_attn, w_br_sgu=v_w_br_sgu, w_out=v_w_out, norm2_g=v_norm2_g, w_ffn_in=v_w_ffn_in,
                 w_ffn_out=v_w_ffn_out)
    names = list(weights)
    big_names = ("w_mod", "w_in", "w_uq", "w_ukv", "w_br_attn", "w_br_sgu", "w_out", "w_ffn_in", "w_ffn_out")
    out_g, out_d, out_m, out_v = {}, {}, {}, {}
    for nm in big_names:
        out_g[nm], out_d[nm], out_m[nm], out_v[nm] = upd(weights[nm], grads[nm], m_in[nm], v_in_[nm], nm)
    row_names = [nm for nm in names if nm not in big_names and nm not in ("w_spatial", "b_spatial")]
    widths = [-(-weights[nm].size // LANES) * LANES for nm in row_names]

    def as_row(d):
        return jnp.concatenate([jnp.pad(d[nm].reshape(1, -1), ((0, 0), (0, wd - d[nm].size)))
                                for nm, wd in zip(row_names, widths)], axis=1)

    def as_mat(d):
        return jnp.concatenate([d["w_spatial"].reshape(G * CH, CH), d["b_spatial"].reshape(G, CH)], axis=0)

    row_res = _adamw(as_row(weights), as_row(grads), as_row(m_in), as_row(v_in_), name="adamw_rows")
    mat_res = _adamw(as_mat(weights), as_mat(grads), as_mat(m_in), as_mat(v_in_), name="adamw_spatial")
    for tgt, row, mat in zip((out_d, out_m, out_v), row_res, mat_res):
        for nm, seg in zip(row_names, _split_lanes(row, widths)):
            tgt[nm] = seg[:, :weights[nm].size].reshape(weights[nm].shape)
        tgt["w_spatial"] = mat[:G * CH].reshape(w_spatial.shape)
        tgt["b_spatial"] = mat[G * CH:].reshape(b_spatial.shape)
    for nm in row_names + ["w_spatial", "b_spatial"]:
        out_g[nm] = grads[nm].reshape(weights[nm].shape)

    loss = loss11.reshape(())
    return (loss, grad_x[None], *[out_g[n] for n in names], *[out_d[n] for n in names],
            *[out_m[n] for n in names], *[out_v[n] for n in names])
```

```python
import math

import jax
import jax.numpy as jnp
from jax import lax
from jax.experimental import pallas as pl
from jax.experimental.pallas import tpu as pltpu

F32, BF16 = jnp.float32, jnp.bfloat16
MESH = pl.DeviceIdType.MESH

LANES = 128
BF16_SUBLANES = 16
VMEM_LIMIT_BYTES = 56 * 1024 * 1024

EPS = 1e-6
ROPE_DIM = 64
ROPE_THETA = 10000.0
GRID_W = 64
HEAD_PAD = 256
ADAM_LR, ADAM_B1, ADAM_B2, ADAM_EPS, ADAM_WD, ADAM_STEP = 0.001, 0.9, 0.999, 1e-08, 0.01, 10


def _tile(dim, pref, align=LANES):
    if dim <= pref:
        return dim
    t = (pref // align) * align
    while t >= align:
        if dim % t == 0:
            return t
        t -= align
    return dim


def _params(sem=None):
    return pltpu.CompilerParams(dimension_semantics=sem, vmem_limit_bytes=VMEM_LIMIT_BYTES)


def _sds(shape, dtype):
    return jax.ShapeDtypeStruct(tuple(shape), dtype)


def _mm(pairs, *, name, ta=False, tb=False, outs=(F32,), tm=1024, tn=1024, tk=2048, extras=(), epi=None,
        split=None, into=None):
    dual = len(pairs[0]) == 3
    a0, b0 = pairs[0][0], pairs[0][1]
    M = a0.shape[1] if ta else a0.shape[0]
    N = b0.shape[0] if tb else b0.shape[1]
    tm, tn = _tile(M, tm), _tile(N if split is None else split, tn)
    ks = [(p[0].shape[0] if ta else p[0].shape[1]) for p in pairs]
    tks = [_tile(k, tk) for k in ks]
    nks = [k // t for k, t in zip(ks, tks)]
    offs = [sum(nks[:i]) for i in range(len(pairs))]
    nk_total = sum(nks)
    single = len(pairs) == 1

    def kidx(kk, p):
        return kk if single else jnp.clip(kk - offs[p], 0, nks[p] - 1)

    in_specs, operands = [], []
    for p, pr in enumerate(pairs):
        if ta:
            in_specs.append(pl.BlockSpec((tks[p], tm), lambda i, j, kk, p=p: (kidx(kk, p), i)))
        else:
            in_specs.append(pl.BlockSpec((tm, tks[p]), lambda i, j, kk, p=p: (i, kidx(kk, p))))
        operands.append(pr[0])
        for b in pr[1:]:
            if tb:
                in_specs.append(pl.BlockSpec((tn, tks[p]), lambda i, j, kk, p=p: (j, kidx(kk, p))))
            else:
                in_specs.append(pl.BlockSpec((tks[p], tn), lambda i, j, kk, p=p: (kidx(kk, p), j)))
            operands.append(b)
    for arr, kind in extras:
        if kind == "mn":
            in_specs.append(pl.BlockSpec((tm, tn), lambda i, j, kk: (i, j)))
        else:
            in_specs.append(pl.BlockSpec((1, tn), lambda i, j, kk: (0, j)))
        operands.append(arr)
    n_in = len(operands)
    n_ex = len(extras)
    per = 3 if dual else 2
    dims = (((0 if ta else 1,), (1 if tb else 0,)), ((), ()))

    n_acc = 2 if dual else 1

    def products(ins, p):
        a = ins[per * p][...].astype(BF16)
        return [lax.dot_general(a, ins[per * p + 1 + q][...].astype(BF16), dims, preferred_element_type=F32)
                for q in range(n_acc)]

    def finish(ins, out_refs, acc_vals):
        vals = acc_vals + [r[...] for r in ins[n_in - n_ex:]]
        res = epi(*vals) if epi is not None else (vals[0],)
        for o, r in zip(out_refs, res):
            o[...] = r.astype(o.dtype)

    out_specs = [pl.BlockSpec((tm, tn), lambda i, j, kk: (i, j)) for _ in outs]
    out_shape = [_sds((M, N), d) for d in outs]
    aliases = {}
    n_alias = 0
    if split is not None:
        nps = split // tn
        lead = 0 if into is None else into[1]
        out_specs = [pl.BlockSpec((None, tm, tn), lambda i, j, kk: (j // nps + lead, i, j % nps))]
        out_shape = [_sds((N // split if into is None else into[0].shape[0], M, split), outs[0])]
        if into is not None:
            in_specs.append(pl.BlockSpec(memory_space=pl.ANY))
            operands.append(into[0])
            aliases, n_alias = {n_in: 0}, 1

    def body(*refs):
        ins, refs = refs[:n_in], refs[n_in + n_alias:]
        out_refs, accs = refs[:len(outs)], refs[len(outs):]
        if nk_total == 1:
            finish(ins, out_refs, products(ins, 0))
            return
        kk = pl.program_id(2)

        @pl.when(kk == 0)
        def _():
            for acc, v in zip(accs, products(ins, 0)):
                acc[...] = v

        for p in range(len(pairs)):
            lo = max(offs[p], 1)

            @pl.when((kk >= lo) & (kk < offs[p] + nks[p]))
            def _(p=p):
                for acc, v in zip(accs, products(ins, p)):
                    acc[...] += v

        @pl.when(kk == nk_total - 1)
        def _():
            finish(ins, out_refs, [acc[...] for acc in accs])

    res = pl.pallas_call(
        body, name=name, grid=(M // tm, N // tn, nk_total), in_specs=in_specs,
        out_specs=out_specs, out_shape=out_shape, input_output_aliases=aliases,
        scratch_shapes=[pltpu.VMEM((tm, tn), F32) for _ in range(n_acc if nk_total > 1 else 0)],
        compiler_params=_params(("parallel", "parallel", "arbitrary")),
    )(*operands)
    return res[0] if len(outs) == 1 else res


def _rowwise(fn, rows, vecs, out_rows, out_accs=(), *, name, tm=256, tc=None):
    M = rows[0].shape[0]
    tm = _tile(M, tm, BF16_SUBLANES)
    nrow = M // tm
    C = rows[0].shape[1]
    ncol = 1 if tc is None else C // _tile(C, tc)
    tcol = None if tc is None else _tile(C, tc)

    def colwise(shape):
        return tc is not None and len(shape) == 2 and shape[0] == 1 and shape[1] == C

    def vspec(shape):
        if colwise(shape):
            return pl.BlockSpec((1, tcol), lambda j, i: (0, j))
        return pl.BlockSpec(tuple(shape), lambda j, i, n=len(shape): (0,) * n)

    def rspec(width):
        if tc is None:
            return pl.BlockSpec((tm, width), lambda j, i: (i, 0))
        return pl.BlockSpec((tm, tcol), lambda j, i: (i, j))

    in_specs = [rspec(r.shape[1]) for r in rows] + [vspec(v.shape) for v in vecs]
    out_specs = [rspec(c) for c, _ in out_rows] + [vspec(s) for s in out_accs]
    out_shape = [_sds((M, c), d) for c, d in out_rows] + [_sds(s, F32) for s in out_accs]
    n_in, n_or = len(rows) + len(vecs), len(out_rows)

    def body(*refs):
        ins, o_rows, o_accs = refs[:n_in], refs[n_in:n_in + n_or], refs[n_in + n_or:]
        r_out, a_out = fn(*[r[...] for r in ins])
        for o, r in zip(o_rows, r_out):
            o[...] = r.astype(o.dtype)
        i = pl.program_id(1)

        @pl.when(i == 0)
        def _():
            for o, a in zip(o_accs, a_out):
                o[...] = a

        @pl.when(i > 0)
        def _():
            for o, a in zip(o_accs, a_out):
                o[...] += a

    res = pl.pallas_call(
        body, name=name, grid=(ncol, nrow), in_specs=in_specs, out_specs=out_specs, out_shape=out_shape,
        compiler_params=_params(("parallel", "arbitrary")),
    )(*rows, *vecs)
    return res[:n_or], res[n_or:]


def _colsum(t):
    return jnp.sum(t, axis=0, keepdims=True)


def _gelu(t):
    return 0.5 * t * (1.0 + lax.erf(t * math.sqrt(0.5)))


def _gelu_grad(t):
    return 0.5 * (1.0 + lax.erf(t * math.sqrt(0.5))) + t * jnp.exp(-0.5 * t * t) * (1.0 / math.sqrt(2.0 * math.pi))


def _sigmoid(t):
    return 1.0 / (1.0 + jnp.exp(-t))


def _rms_stats(t, width):
    return lax.rsqrt(jnp.sum(t * t, axis=-1, keepdims=True) * (1.0 / width) + EPS)


def _rms_bwd(dn, tn, r, width):
    return r * (dn - tn * (jnp.sum(dn * tn, axis=-1, keepdims=True) * (1.0 / width)))


def _attn_fwd(q, k, v, *, heads, scale, tq=256):
    N, M = q.shape[0], k.shape[0]
    tq = _tile(N, tq)
    vd = v.shape[1] // heads

    def body(q_ref, k_ref, v_ref, o_ref, lse_ref):
        s = lax.dot_general(q_ref[...], k_ref[...], (((1,), (1,)), ((), ())), preferred_element_type=F32) * scale
        m = jnp.max(s, axis=-1, keepdims=True)
        p = jnp.exp(s - m)
        l = jnp.sum(p, axis=-1, keepdims=True)
        o = jnp.dot(p.astype(BF16), v_ref[...], preferred_element_type=F32) / l
        o_ref[...] = o.astype(o_ref.dtype)
        lse_ref[...] = jnp.broadcast_to(m + jnp.log(l), lse_ref.shape)

    return pl.pallas_call(
        body, name="attn_fwd", grid=(heads, N // tq),
        in_specs=[pl.BlockSpec((tq, HEAD_PAD), lambda h, i: (i, h)),
                  pl.BlockSpec((M, HEAD_PAD), lambda h, i: (0, h)),
                  pl.BlockSpec((M, vd), lambda h, i: (0, h))],
        out_specs=[pl.BlockSpec((tq, vd), lambda h, i: (i, h)), pl.BlockSpec((tq, vd), lambda h, i: (i, h))],
        out_shape=[_sds((N, heads * vd), BF16), _sds((N, heads * vd), F32)],
        compiler_params=_params(("parallel", "arbitrary")),
    )(q, k, v)


def _attn_bwd(q, k, v, o, lse, do, *, heads, scale, tq=256):
    N, M = q.shape[0], k.shape[0]
    tq = _tile(N, tq)
    vd = v.shape[1] // heads
    nt = (((1,), (1,)), ((), ()))
    tn = (((0,), (0,)), ((), ()))

    def body(q_ref, k_ref, v_ref, o_ref, lse_ref, do_ref, dq_ref, dk_ref, dv_ref):
        i = pl.program_id(1)
        qb, kb, dob = q_ref[...], k_ref[...], do_ref[...]
        s = lax.dot_general(qb, kb, nt, preferred_element_type=F32) * scale
        p = jnp.exp(s - lse_ref[...][:, :1])
        dp = lax.dot_general(dob, v_ref[...], nt, preferred_element_type=F32)
        delta = jnp.sum(dob.astype(F32) * o_ref[...].astype(F32), axis=-1, keepdims=True)
        ds = (p * (dp - delta) * scale).astype(BF16)
        pb = p.astype(BF16)
        dq_ref[...] = jnp.dot(ds, kb, preferred_element_type=F32)
        dk_part = lax.dot_general(ds, qb, tn, preferred_element_type=F32)
        dv_part = lax.dot_general(pb, dob, tn, preferred_element_type=F32)

        @pl.when(i == 0)
        def _():
            dk_ref[...] = dk_part
            dv_ref[...] = dv_part

        @pl.when(i > 0)
        def _():
            dk_ref[...] += dk_part
            dv_ref[...] += dv_part

    return pl.pallas_call(
        body, name="attn_bwd", grid=(heads, N // tq),
        in_specs=[pl.BlockSpec((tq, HEAD_PAD), lambda h, i: (i, h)),
                  pl.BlockSpec((M, HEAD_PAD), lambda h, i: (0, h)),
                  pl.BlockSpec((M, vd), lambda h, i: (0, h)),
                  pl.BlockSpec((tq, vd), lambda h, i: (i, h)),
                  pl.BlockSpec((tq, vd), lambda h, i: (i, h)),
                  pl.BlockSpec((tq, vd), lambda h, i: (i, h))],
        out_specs=[pl.BlockSpec((tq, HEAD_PAD), lambda h, i: (i, h)),
                   pl.BlockSpec((M, HEAD_PAD), lambda h, i: (0, h)),
                   pl.BlockSpec((M, vd), lambda h, i: (0, h))],
        out_shape=[_sds((N, heads * HEAD_PAD), F32), _sds((M, heads * HEAD_PAD), F32), _sds((M, heads * vd), F32)],
        compiler_params=_params(("parallel", "arbitrary")),
    )(q, k, v, o, lse, do)


def _place():
    return lax.axis_index("x"), lax.axis_index("y"), lax.axis_index("c")


def _comm_call(body, xs, out_shapes, n_sems, name, in_vmem):
    space = pltpu.VMEM if in_vmem else pl.ANY
    n = len(xs)

    def wrapped(*refs):
        body(refs[:n], refs[n:2 * n], *refs[2 * n:])

    return pl.pallas_call(
        wrapped, name=name, out_shape=list(out_shapes),
        in_specs=[pl.BlockSpec(memory_space=space)] * n, out_specs=[pl.BlockSpec(memory_space=space)] * n,
        scratch_shapes=[pltpu.SemaphoreType.DMA((n, n_sems)), pltpu.SemaphoreType.DMA((n, n_sems)),
                        pltpu.SemaphoreType.DMA((n,))],
        compiler_params=pltpu.CompilerParams(vmem_limit_bytes=VMEM_LIMIT_BYTES),
    )(*xs)


def _all_gather8(blks, *, name, in_vmem, others_only=False):
    def body(x_refs, out_refs, send_sems, recv_sems, local_sems):
        x, y, c = _place()
        me, sibling = (x, y, c), (x, y, 1 - c)
        chips = [(1 - x, y), (x, 1 - y), (1 - x, 1 - y)]
        waits = []
        for w, (x_ref, out_ref) in enumerate(zip(x_refs, out_refs)):
            def slot(px, py, pc, out_ref=out_ref):
                return out_ref.at[4 * px + 2 * py + pc]

            def copy(k, block, to, src=None, w=w, slot=slot):
                return pltpu.make_async_remote_copy(
                    src_ref=slot(*block) if src is None else src, dst_ref=slot(*block),
                    send_sem=send_sems.at[w, k], recv_sem=recv_sems.at[w, k], device_id=to, device_id_type=MESH)

            mine = None
            first = []
            if not others_only:
                mine = pltpu.make_async_copy(x_ref, slot(*me), local_sems.at[w])
                mine.start()
                first.append(copy(0, me, sibling, src=x_ref))
            first += [copy(1 + j, me, (*chip, c), src=x_ref) for j, chip in enumerate(chips)]
            for cp in first:
                cp.start()
            waits.append((copy, mine, first))
        for copy, mine, first in waits:
            passed = [copy(4 + j, (*chip, c), sibling) for j, chip in enumerate(chips)]
            for j, chip in enumerate(chips):
                copy(1 + j, (*chip, c), me).wait_recv()
                passed[j].start()
            if not others_only:
                copy(0, sibling, me).wait_recv()
            for j, chip in enumerate(chips):
                copy(4 + j, (*chip, 1 - c), me).wait_recv()
            for cp in first + passed:
                cp.wait_send()
            if mine is not None:
                mine.wait()

    return _comm_call(body, blks, [_sds((8,) + b.shape, b.dtype) for b in blks], 7, name, in_vmem)


def _pair_gather(blks, *, name):
    def body(x_refs, out_refs, send_sems, recv_sems, local_sems):
        x, y, c = _place()
        for w, (x_ref, out_ref) in enumerate(zip(x_refs, out_refs)):
            pltpu.make_async_remote_copy(src_ref=x_ref, dst_ref=out_ref.at[c], send_sem=send_sems.at[w, 0],
                                         recv_sem=recv_sems.at[w, 0], device_id=(x, y, 1 - c),
                                         device_id_type=MESH).start()
        for w, (x_ref, out_ref) in enumerate(zip(x_refs, out_refs)):
            cp = pltpu.make_async_remote_copy(src_ref=x_ref, dst_ref=out_ref.at[1 - c], send_sem=send_sems.at[w, 0],
                                              recv_sem=recv_sems.at[w, 0], device_id=(x, y, 1 - c), device_id_type=MESH)
            cp.wait_recv()
            cp.wait_send()

    return _comm_call(body, blks, [_sds((2,) + b.shape, b.dtype) for b in blks], 1, name, False)


def _pair_swap_other(bufs, *, name):
    def body(x_refs, out_refs, send_sems, recv_sems, local_sems):
        x, y, c = _place()
        for w, (x_ref, out_ref) in enumerate(zip(x_refs, out_refs)):
            for q in range(4):
                pltpu.make_async_remote_copy(src_ref=x_ref.at[q, 1 - c], dst_ref=out_ref.at[q],
                                             send_sem=send_sems.at[w, 0], recv_sem=recv_sems.at[w, 0],
                                             device_id=(x, y, 1 - c), device_id_type=MESH).start()
        for w, out_ref in enumerate(out_refs):
            pltpu.make_async_remote_copy(src_ref=out_ref, dst_ref=out_ref, send_sem=send_sems.at[w, 0],
                                         recv_sem=recv_sems.at[w, 0], device_id=(x, y, 1 - c),
                                         device_id_type=MESH).wait()

    return _comm_call(body, bufs, [_sds((4,) + b.shape[2:], b.dtype) for b in bufs], 1, name, False)


def _chip_exchange(bufs, *, name):
    def body(x_refs, out_refs, send_sems, recv_sems, local_sems):
        x, y, c = _place()
        p = 2 * x + y
        chips = [(1 - x, y), (x, 1 - y), (1 - x, 1 - y)]
        started = []
        for w, (x_ref, out_ref) in enumerate(zip(x_refs, out_refs)):
            sends = [pltpu.make_async_remote_copy(src_ref=x_ref.at[2 * qx + qy], dst_ref=out_ref.at[p],
                                                  send_sem=send_sems.at[w, k], recv_sem=recv_sems.at[w, k],
                                                  device_id=(qx, qy, c), device_id_type=MESH)
                     for k, (qx, qy) in enumerate(chips)]
            for cp in sends:
                cp.start()
            started.append(sends)
        for w, (x_ref, out_ref) in enumerate(zip(x_refs, out_refs)):
            for k, (qx, qy) in enumerate(chips):
                pltpu.make_async_remote_copy(src_ref=x_ref.at[p], dst_ref=out_ref.at[2 * qx + qy],
                                             send_sem=send_sems.at[w, k], recv_sem=recv_sems.at[w, k],
                                             device_id=(qx, qy, c), device_id_type=MESH).wait_recv()
            for cp in started[w]:
                cp.wait_send()

    return _comm_call(body, bufs, [_sds(b.shape, b.dtype) for b in bufs], 3, name, False)


def _block_rows(rows, row_bytes, target=1 << 20, align=BF16_SUBLANES):
    return _tile(rows, max(align, target // row_bytes // align * align), align)


def _sum_blocks(buf, *, name, out_dtype):
    B, R, C = buf.shape
    tm = _block_rows(R, B * C * buf.dtype.itemsize)

    def body(x_ref, o_ref):
        acc = x_ref[0].astype(F32)
        for b in range(1, B):
            acc = acc + x_ref[b].astype(F32)
        o_ref[...] = acc.astype(o_ref.dtype)

    return pl.pallas_call(
        body, name=name, grid=(R // tm,), in_specs=[pl.BlockSpec((B, tm, C), lambda i: (0, i, 0))],
        out_specs=pl.BlockSpec((tm, C), lambda i: (i, 0)), out_shape=_sds((R, C), out_dtype),
        compiler_params=_params(("parallel",)),
    )(buf)


def _pair_add(mine, theirs, core, *, name):
    _, _, R, C = mine.shape
    tm = _block_rows(R, C * 2)

    def body(core_ref, a_ref, b_ref, o_ref):
        o_ref[...] = (a_ref[...].astype(F32) + b_ref[...].astype(F32)).astype(o_ref.dtype)

    return pl.pallas_call(
        body, name=name, out_shape=_sds(theirs.shape, BF16),
        grid_spec=pltpu.PrefetchScalarGridSpec(
            num_scalar_prefetch=1, grid=(4, R // tm),
            in_specs=[pl.BlockSpec((None, None, tm, C), lambda q, i, core_ref: (q, core_ref[0], i, 0)),
                      pl.BlockSpec((None, tm, C), lambda q, i, core_ref: (q, i, 0))],
            out_specs=pl.BlockSpec((None, tm, C), lambda q, i, core_ref: (q, i, 0))),
        compiler_params=_params(("parallel", "parallel")),
    )(core, mine, theirs)


def _assemble(gathered, own, chip, *, name, transpose):
    _, K, Ns = gathered.shape
    tm = _block_rows(K, Ns * 4)

    def body(chip_ref, g_ref, own_ref, o_ref):
        q = pl.program_id(0)

        @pl.when(q == chip_ref[0])
        def _():
            o_ref[...] = own_ref[...].astype(BF16)

        @pl.when(q != chip_ref[0])
        def _():
            o_ref[...] = g_ref[...]

    if transpose:
        out_spec = pl.BlockSpec((tm, Ns), lambda q, i, ch: (i, q))
        out_shape = _sds((K, 4 * Ns), BF16)
    else:
        out_spec = pl.BlockSpec((None, tm, Ns), lambda q, i, ch: (q, i, 0))
        out_shape = _sds((4, K, Ns), BF16)
    return pl.pallas_call(
        body, name=name, out_shape=out_shape,
        grid_spec=pltpu.PrefetchScalarGridSpec(
            num_scalar_prefetch=1, grid=(4, K // tm),
            in_specs=[pl.BlockSpec((None, tm, Ns), lambda q, i, ch: (jnp.where(q == ch[0], (q + 1) % 4, q), i, 0)),
                      pl.BlockSpec((tm, Ns), lambda q, i, ch: (jnp.where(q == ch[0], i, 0), 0))],
            out_specs=out_spec),
        compiler_params=_params(("arbitrary", "arbitrary")),
    )(chip, gathered, own)


def _split_lanes(row, widths):
    out, off = [], 0
    for wd in widths:
        out.append(row[:, off:off + wd])
        off += wd
    return out


def _adamw(w, g, m, v, *, name):
    C = w.shape[1]

    def fn(w, g, m, v):
        m = ADAM_B1 * m + (1.0 - ADAM_B1) * g
        v = ADAM_B2 * v + (1.0 - ADAM_B2) * (g * g)
        m_hat = m / (1.0 - ADAM_B1 ** ADAM_STEP)
        v_hat = v / (1.0 - ADAM_B2 ** ADAM_STEP)
        delta = -ADAM_LR * (m_hat / (jnp.sqrt(v_hat) + ADAM_EPS) + ADAM_WD * w)
        return (delta, m, v), ()

    tm = max(8, min(512, (1 << 20) // (4 * C) // 8 * 8))
    (d, nm, nv), _ = _rowwise(fn, [w, g, m, v], [], [(C, F32)] * 3, name=name, tm=tm)
    return d, nm, nv


def _rope_tables(n):
    rows = n // GRID_W
    row = jnp.repeat(jnp.arange(rows, dtype=F32), GRID_W)
    col = jnp.tile(jnp.arange(GRID_W, dtype=F32), rows)
    nf = ROPE_DIM // 4
    freqs = ROPE_THETA ** (-jnp.arange(nf, dtype=F32) / nf)
    ang_r, ang_c = row[:, None] * freqs[None, :], col[:, None] * freqs[None, :]
    cr, sr, cc, sc = jnp.cos(ang_r), jnp.sin(ang_r), jnp.cos(ang_c), jnp.sin(ang_c)
    nope = HEAD_PAD - 2 * ROPE_DIM
    one, zero, z = jnp.ones((n, nope), F32), jnp.zeros((n, nope), F32), jnp.zeros((n, nf), F32)
    pad = jnp.zeros((n, ROPE_DIM), F32)
    cos = jnp.concatenate([one, cr, cr, cc, cc, pad], axis=1)
    s_lo = jnp.concatenate([zero, -sr, z, -sc, z, pad], axis=1)
    s_hi = jnp.concatenate([zero, z, sr, z, sc, pad], axis=1)
    return cos, s_lo, s_hi


def _rope(n, cos, s_lo, s_hi):
    q = ROPE_DIM // 4
    return n * cos + pltpu.roll(n, HEAD_PAD - q, 1) * s_lo + pltpu.roll(n, q, 1) * s_hi


def _rope_t(d, cos, s_lo, s_hi):
    q = ROPE_DIM // 4
    return d * cos + pltpu.roll(d * s_lo, q, 1) + pltpu.roll(d * s_hi, HEAD_PAD - q, 1)


def kernel(x, c, ctx, c_ctx, w_mod, b_mod, norm1_g, w_in, q_norm_g, kv_norm_g, w_uq, w_ukv, qk_norm_q, qk_norm_k, sgu_norm_g, sgu_norm_b, w_spatial, b_spatial, w_br_attn, w_br_sgu, w_out, norm2_g, w_ffn_in, w_ffn_out, loss_target, m_c_ctx, m_w_mod, m_b_mod, m_norm1_g, m_w_in, m_q_norm_g, m_kv_norm_g, m_w_uq, m_w_ukv, m_qk_norm_q, m_qk_norm_k, m_sgu_norm_g, m_sgu_norm_b, m_w_spatial, m_b_spatial, m_w_br_attn, m_w_br_sgu, m_w_out, m_norm2_g, m_w_ffn_in, m_w_ffn_out, v_c_ctx, v_w_mod, v_b_mod, v_norm1_g, v_w_in, v_q_norm_g, v_kv_norm_g, v_w_uq, v_w_ukv, v_qk_norm_q, v_qk_norm_k, v_sgu_norm_g, v_sgu_norm_b, v_w_spatial, v_b_spatial, v_w_br_attn, v_w_br_sgu, v_w_out, v_norm2_g, v_w_ffn_in, v_w_ffn_out):
    ax, ay, ac = _place()
    my_chip = 2 * ax + ay
    my_dev = 4 * ax + 2 * ay + ac

    N, D = x.shape[1], x.shape[2]
    CT = ctx.shape[1]
    M = N + CT
    QL, KVL, QK = q_norm_g.shape[-1], kv_norm_g.shape[-1], qk_norm_q.shape[-1]
    NOPE = QK - ROPE_DIM
    VD = NOPE
    H = 4 * w_uq.shape[-1] // QK
    SW, G, CH = sgu_norm_g.shape[-1], w_spatial.shape[1], w_spatial.shape[2]
    GD = SW // G
    DFF = 4 * w_ffn_out.shape[1]
    NMOD = 4 * w_mod.shape[-1]
    NM = w_mod.shape[-1]
    KVP = KVL + 2 * ROPE_DIM
    assert NOPE == LANES and GD == LANES and HEAD_PAD == NOPE + 2 * ROPE_DIM and CH == LANES
    scale = QK ** -0.5

    x2, ctx2, tgt2 = x[0], ctx[0], loss_target[0]

    c_all = _all_gather8([c], name="ag_c", in_vmem=True)[0][:, 0, :]
    c_rows = jnp.concatenate([c_all, c_ctx[None, :], jnp.zeros((BF16_SUBLANES - 9, D), F32)], axis=0)

    def silu_fn(t):
        s = _sigmoid(t)
        return (t * s, s * (1.0 + t * (1.0 - s))), ()

    (silu_c, dsilu_c), _ = _rowwise(silu_fn, [c_rows], [], [(D, F32), (D, F32)], name="silu_c", tm=16)
    wm = w_mod[0]
    mod_loc = _mm([(silu_c, wm)], name="mod_fwd", outs=(F32,), tn=512, tk=512,
                  extras=[(lax.dynamic_slice_in_dim(b_mod, my_chip * NM, NM, axis=1), "n")],
                  epi=lambda acc, b: (acc + b,))
    mod_all = _all_gather8([mod_loc], name="ag_mod", in_vmem=True)[0]
    mod_full = jnp.concatenate([mod_all[0], mod_all[2], mod_all[4], mod_all[6]], axis=1)
    mod_me = lax.dynamic_slice_in_dim(mod_full, my_dev, 1, axis=0)
    sh1, sc1, g1, sh2, sc2, g2 = [mod_me[:, i * D:(i + 1) * D] for i in range(6)]
    sh1c, sc1c = mod_full[8:9, :D], mod_full[8:9, D:2 * D]

    big = [w_in[0], w_uq[0], w_ukv[0], w_br_attn[0], w_br_sgu[0], w_out[0], w_ffn_in[0], w_ffn_out[0]]
    col_sharded = [True, True, True, True, True, False, True, False]
    halves = [lax.dynamic_slice_in_dim(a, ac * (a.shape[0] // 2), a.shape[0] // 2, axis=0).astype(BF16) for a in big]
    gathered = _all_gather8(halves, name="ag_weights", in_vmem=False, others_only=True)
    tags = ["w_in", "w_uq", "w_ukv", "w_br_attn", "w_br_sgu", "w_out", "w_ffn_in", "w_ffn_out"]
    chip1 = jnp.reshape(my_chip, (1,)).astype(jnp.int32)
    full = []
    for seg, a, cs, tag in zip(gathered, big, col_sharded, tags):
        side_by_side = cs and a.shape[1] % LANES == 0
        seg = _assemble(seg.reshape((4,) + a.shape), a, chip1, name="assemble_" + tag, transpose=side_by_side)
        if cs and not side_by_side:
            seg = seg.transpose(1, 0, 2).reshape(a.shape[0], 4 * a.shape[1])
        elif not cs:
            seg = seg.reshape(4 * a.shape[0], a.shape[1])
        full.append(seg)
    w_in_f, w_uq_f, w_ukv_f, w_bra, w_brs, w_out_f, w_ffi, w_ffo = full
    o_kv, o_u = QL, QL + KVL + ROPE_DIM
    o_v, o_g = o_u + SW, o_u + 2 * SW
    w_q = w_in_f[:, :QL]
    w_kv = jnp.pad(w_in_f[:, o_kv:o_u], ((0, 0), (0, ROPE_DIM)))
    w_u, w_v = w_in_f[:, o_u:o_v], w_in_f[:, o_v:o_g]
    w_g1, w_g2 = w_in_f[:, o_g:o_g + D], w_in_f[:, o_g + D:]
    w_uq_p = jnp.pad(w_uq_f.reshape(QL, H, QK), ((0, 0), (0, 0), (0, HEAD_PAD - QK))).reshape(QL, H * HEAD_PAD)
    w_fa, w_fb = w_ffi[:, :DFF], w_ffi[:, DFF:]

    cos_t, slo_t, shi_t = _rope_tables(N)
    ones_c = jnp.concatenate([jnp.ones((CT, NOPE + ROPE_DIM), F32), jnp.zeros((CT, ROPE_DIM), F32)], axis=1)
    cos_k = jnp.concatenate([cos_t, ones_c], axis=0)
    slo_k = jnp.concatenate([slo_t, jnp.zeros((CT, HEAD_PAD), F32)], axis=0)
    shi_k = jnp.concatenate([shi_t, jnp.zeros((CT, HEAD_PAD), F32)], axis=0)
    gq_p = jnp.pad(qk_norm_q, ((0, 0), (0, HEAD_PAD - QK)))
    gk_p = jnp.pad(qk_norm_k, ((0, 0), (0, HEAD_PAD - QK)))

    def norm_mod_fn(t, g, sh, sc):
        r = _rms_stats(t, D)
        return (((t * r) * g) * (1.0 + sc) + sh,), ()

    (h,), _ = _rowwise(norm_mod_fn, [x2], [norm1_g, sh1, sc1], [(D, BF16)], name="norm1_x")
    (ctx_h,), _ = _rowwise(norm_mod_fn, [ctx2], [norm1_g, sh1c, sc1c], [(D, BF16)], name="norm1_ctx")

    qc = _mm([(h, w_q)], name="proj_q", outs=(F32,))
    kvin = jnp.concatenate([_mm([(h, w_kv)], name="proj_kv", outs=(F32,)),
                            _mm([(ctx_h, w_kv)], name="proj_kv_ctx", outs=(F32,))], axis=0)
    u_in = _mm([(h, w_u)], name="proj_u", outs=(BF16,))
    v_in = _mm([(h, w_v)], name="proj_v", outs=(BF16,))
    g1_in = _mm([(h, w_g1)], name="proj_g1", outs=(BF16,))
    g2_in = _mm([(h, w_g2)], name="proj_g2", outs=(BF16,))

    def rms_gain_fn(width):
        def fn(t, g):
            return (((t * _rms_stats(t, width)) * g),), ()
        return fn

    (qn,), _ = _rowwise(rms_gain_fn(QL), [qc], [q_norm_g], [(QL, BF16)], name="q_norm")

    def kv_norm_fn(t, g):
        kvc = t[:, :KVL]
        return (((kvc * _rms_stats(kvc, KVL)) * g),), ()

    (kvn,), _ = _rowwise(kv_norm_fn, [kvin], [kv_norm_g], [(KVL, BF16)], name="kv_norm")
    q_raw = _mm([(qn, w_uq_p)], name="q_up", outs=(F32,))
    kv_raw = _mm([(kvn, w_ukv_f)], name="kv_up", outs=(F32,))

    def q_post_fn(t, cos, slo, shi, g):
        outs = []
        for hd in range(H):
            th = t[:, hd * HEAD_PAD:(hd + 1) * HEAD_PAD]
            outs.append(_rope((th * _rms_stats(th, QK)) * g, cos, slo, shi))
        return (jnp.concatenate(outs, axis=1),), ()

    (q_att,), _ = _rowwise(q_post_fn, [q_raw, cos_t, slo_t, shi_t], [gq_p], [(H * HEAD_PAD, BF16)], name="q_post")

    def k_post_fn(t, kvi, cos, slo, shi, g):
        kr = kvi[:, KVL:]
        ks, vs = [], []
        for hd in range(H):
            th = jnp.concatenate([t[:, hd * HEAD_PAD:hd * HEAD_PAD + NOPE], kr], axis=1)
            ks.append(_rope((th * _rms_stats(th, QK)) * g, cos, slo, shi))
            vs.append(t[:, hd * HEAD_PAD + NOPE:(hd + 1) * HEAD_PAD])
        return (jnp.concatenate(ks, axis=1), jnp.concatenate(vs, axis=1)), ()

    (k_att, v_att), _ = _rowwise(k_post_fn, [kv_raw, kvin, cos_k, slo_k, shi_k], [gk_p],
                                 [(H * HEAD_PAD, BF16), (H * VD, BF16)], name="k_post")
    attn_o, lse = _attn_fwd(q_att, k_att, v_att, heads=H, scale=scale)

    ws3 = w_spatial[0]
    bs_t = jnp.pad(b_spatial[0].T, ((0, 0), (0, LANES - G)))

    def sgu_parts(u_in, v_in, ng, nb):
        u, v = _gelu(u_in.astype(F32)), _gelu(v_in.astype(F32))
        mu = jnp.mean(v, axis=-1, keepdims=True)
        vc = v - mu
        rs = lax.rsqrt(jnp.mean(vc * vc, axis=-1, keepdims=True) + EPS)
        xhat = vc * rs
        return u, xhat, rs, (xhat * ng + nb).astype(BF16)

    def sgu_fwd_fn(u_in, v_in, ng, nb, ws, bst):
        u, _, _, vnb = sgu_parts(u_in, v_in, ng, nb)
        outs = []
        for g in range(G):
            sl = slice(g * GD, (g + 1) * GD)
            mixed = jnp.dot(ws[g].astype(BF16), vnb[:, sl], preferred_element_type=F32) + bst[:, g:g + 1]
            outs.append(u[:, sl] * mixed)
        return (jnp.concatenate(outs, axis=1),), ()

    (sgu_o,), _ = _rowwise(sgu_fwd_fn, [u_in, v_in], [sgu_norm_g, sgu_norm_b, ws3, bs_t], [(SW, BF16)],
                           name="sgu_fwd", tm=CH)

    a1 = _mm([(attn_o, w_bra)], name="br_attn", outs=(BF16,))
    a2 = _mm([(sgu_o, w_brs)], name="br_sgu", outs=(BF16,))

    def merge_fn(a1, a2, gi1, gi2):
        return ((_sigmoid(gi1.astype(F32)) * a1.astype(F32) + _sigmoid(gi2.astype(F32)) * a2.astype(F32)),), ()

    (merged,), _ = _rowwise(merge_fn, [a1, a2, g1_in, g2_in], [], [(D, BF16)], name="merge", tc=1024)

    def res_gate(acc, res, gate):
        return res + gate * acc, acc

    x1, mo = _mm([(merged, w_out_f)], name="out_proj", outs=(F32, BF16), tn=512,
                 extras=[(x2, "mn"), (g1, "n")], epi=res_gate)
    (h2,), _ = _rowwise(norm_mod_fn, [x1], [norm2_g, sh2, sc2], [(D, BF16)], name="norm2")

    def swiglu_epi(a, b):
        return a, b, (a * _sigmoid(a)) * b

    fa, fb, act = _mm([(h2, w_fa, w_fb)], name="ffn_in", outs=(BF16, BF16, BF16), tn=512, epi=swiglu_epi)
    y, f_out = _mm([(act, w_ffo)], name="ffn_out", outs=(F32, BF16), tn=512,
                   extras=[(x1, "mn"), (g2, "n")], epi=res_gate)

    def loss_fn(y, t, fo, g2v):
        e = y - t
        dy = e * (1.0 / D)
        return (dy, g2v * dy), (_colsum(e * e) * (0.5 / D), _colsum(dy * fo.astype(F32)))

    (dy, df), (loss_cols, dg2) = _rowwise(loss_fn, [y, tgt2, f_out], [g2], [(D, F32), (D, BF16)],
                                          [(1, D), (1, D)], name="loss")

    def swiglu_bwd_epi(dact, a, b):
        a, b = a.astype(F32), b.astype(F32)
        s = _sigmoid(a)
        return dact * b * (s * (1.0 + a * (1.0 - s))), dact * (a * s)

    da, db = _mm([(df, w_ffo)], tb=True, name="ffn_out_dx", outs=(BF16, BF16), tn=512,
                 extras=[(fa, "mn"), (fb, "mn")], epi=swiglu_bwd_epi)
    dw_ffo = _mm([(act, df)], ta=True, name="ffn_out_dw", outs=(BF16,))
    dh2 = _mm([(da, w_fa), (db, w_fb)], tb=True, name="ffn_in_dx", outs=(F32,))
    ns_ffi = w_ffn_in.shape[-1]
    dw_ffi = _mm([(h2, da)], ta=True, name="ffn_in_dw_a", outs=(BF16,), tn=1408, split=ns_ffi,
                 into=(lax.empty((4, D, ns_ffi), BF16), 0))
    dw_ffi = _mm([(h2, db)], ta=True, name="ffn_in_dw_b", outs=(BF16,), tn=1408, split=ns_ffi, into=(dw_ffi, 2))

    def norm2_bwd_fn(dh, t, dyv, mov, g, sc, g1v):
        r = _rms_stats(t, D)
        tn = t * r
        dxg = dh * (1.0 + sc)
        dt = dyv + _rms_bwd(dxg * g, tn, r, D)
        return (dt, g1v * dt), (_colsum(dh), _colsum(dh * (tn * g)), _colsum(dxg * tn), _colsum(dt * mov.astype(F32)))

    (dx1, dmo), (dsh2, dsc2, dn2g, dg1) = _rowwise(
        norm2_bwd_fn, [dh2, x1, dy, mo], [norm2_g, sc2, g1], [(D, F32), (D, BF16)], [(1, D)] * 4, name="norm2_bwd")

    def merge_bwd_epi(dm, a1, a2, gi1, gi2):
        s1, s2 = _sigmoid(gi1.astype(F32)), _sigmoid(gi2.astype(F32))
        a1, a2 = a1.astype(F32), a2.astype(F32)
        return dm * s1, dm * s2, dm * a1 * (s1 * (1.0 - s1)), dm * a2 * (s2 * (1.0 - s2))

    da1, da2, dgi1, dgi2 = _mm([(dmo, w_out_f)], tb=True, name="out_proj_dx", outs=(BF16,) * 4, tn=512,
                               extras=[(a1, "mn"), (a2, "mn"), (g1_in, "mn"), (g2_in, "mn")], epi=merge_bwd_epi)
    dw_out = _mm([(merged, dmo)], ta=True, name="out_proj_dw", outs=(BF16,))
    dattn = _mm([(da1, w_bra)], tb=True, name="br_attn_dx", outs=(BF16,))
    dw_bra = _mm([(attn_o, da1)], ta=True, name="br_attn_dw", outs=(BF16,), split=w_br_attn.shape[-1])
    dsgu = _mm([(da2, w_brs)], tb=True, name="br_sgu_dx", outs=(BF16,))
    dw_brs = _mm([(sgu_o, da2)], ta=True, name="br_sgu_dw", outs=(BF16,), split=w_br_sgu.shape[-1])

    def sgu_bwd_fn(dso, u_in, v_in, ng, nb, ws, bst):
        u, xhat, rs, vnb = sgu_parts(u_in, v_in, ng, nb)
        dso = dso.astype(F32)
        lane = lax.broadcasted_iota(jnp.int32, (CH, LANES), 1)
        du, dvn, dws, dbs = [], [], [], jnp.zeros((CH, LANES), F32)
        for g in range(G):
            sl = slice(g * GD, (g + 1) * GD)
            wg = ws[g].astype(BF16)
            mixed = jnp.dot(wg, vnb[:, sl], preferred_element_type=F32) + bst[:, g:g + 1]
            du.append(dso[:, sl] * mixed)
            dmix = dso[:, sl] * u[:, sl]
            dmb = dmix.astype(BF16)
            dws.append(lax.dot_general(dmb, vnb[:, sl], (((1,), (1,)), ((), ())), preferred_element_type=F32))
            dbs = dbs + jnp.where(lane == g, jnp.sum(dmix, axis=1, keepdims=True), 0.0)
            dvn.append(lax.dot_general(wg, dmb, (((0,), (0,)), ((), ())), preferred_element_type=F32))
        du, dvn = jnp.concatenate(du, axis=1), jnp.concatenate(dvn, axis=1)
        dxh = dvn * ng
        dv = rs * (dxh - jnp.mean(dxh, axis=-1, keepdims=True) - xhat * jnp.mean(dxh * xhat, axis=-1, keepdims=True))
        return ((du * _gelu_grad(u_in.astype(F32)), dv * _gelu_grad(v_in.astype(F32))),
                (_colsum(dvn * xhat), _colsum(dvn), jnp.stack(dws), dbs))

    (du_in, dv_in), (d_sng, d_snb, d_ws, d_bs) = _rowwise(
        sgu_bwd_fn, [dsgu, u_in, v_in], [sgu_norm_g, sgu_norm_b, ws3, bs_t], [(SW, BF16), (SW, BF16)],
        [(1, SW), (1, SW), (G, CH, CH), (CH, LANES)], name="sgu_bwd", tm=CH)

    dq_att, dk_att, dv_att = _attn_bwd(q_att, k_att, v_att, attn_o, lse, dattn, heads=H, scale=scale)

    def q_post_bwd_fn(dq, t, cos, slo, shi, g):
        outs, dg = [], jnp.zeros((1, HEAD_PAD), F32)
        for hd in range(H):
            sl = slice(hd * HEAD_PAD, (hd + 1) * HEAD_PAD)
            th = t[:, sl]
            r = _rms_stats(th, QK)
            tn = th * r
            dn = _rope_t(dq[:, sl], cos, slo, shi)
            dg = dg + _colsum(dn * tn)
            outs.append(_rms_bwd(dn * g, tn, r, QK))
        return (jnp.concatenate(outs, axis=1),), (dg,)

    (dq_raw,), (d_gq,) = _rowwise(q_post_bwd_fn, [dq_att, q_raw, cos_t, slo_t, shi_t], [gq_p],
                                  [(H * HEAD_PAD, BF16)], [(1, HEAD_PAD)], name="q_post_bwd")

    def k_post_bwd_fn(dk, dv, t, kvi, cos, slo, shi, g):
        kr = kvi[:, KVL:]
        outs, dg, dkr = [], jnp.zeros((1, HEAD_PAD), F32), jnp.zeros_like(kr)
        for hd in range(H):
            th = jnp.concatenate([t[:, hd * HEAD_PAD:hd * HEAD_PAD + NOPE], kr], axis=1)
            r = _rms_stats(th, QK)
            tn = th * r
            dn = _rope_t(dk[:, hd * HEAD_PAD:(hd + 1) * HEAD_PAD], cos, slo, shi)
            dg = dg + _colsum(dn * tn)
            dt = _rms_bwd(dn * g, tn, r, QK)
            dkr = dkr + dt[:, NOPE:]
            outs += [dt[:, :NOPE], dv[:, hd * VD:(hd + 1) * VD]]
        return (jnp.concatenate(outs, axis=1), dkr), (dg,)

    (dkv_raw, dkrope), (d_gk,) = _rowwise(
        k_post_bwd_fn, [dk_att, dv_att, kv_raw, kvin, cos_k, slo_k, shi_k], [gk_p],
        [(H * HEAD_PAD, BF16), (2 * ROPE_DIM, F32)], [(1, HEAD_PAD)], name="k_post_bwd")

    dqn = _mm([(dq_raw, w_uq_p)], tb=True, name="q_up_dx", outs=(F32,))
    dw_uq_p = _mm([(qn, dq_raw)], ta=True, name="q_up_dw", outs=(BF16,))
    dkvn = _mm([(dkv_raw, w_ukv_f)], tb=True, name="kv_up_dx", outs=(F32,))
    dw_ukv = _mm([(kvn, dkv_raw)], ta=True, name="kv_up_dw", outs=(BF16,), split=w_ukv.shape[-1])

    def q_norm_bwd_fn(dn, t, g):
        r = _rms_stats(t, QL)
        tn = t * r
        return (_rms_bwd(dn * g, tn, r, QL),), (_colsum(dn * tn),)

    (dqc,), (d_qng,) = _rowwise(q_norm_bwd_fn, [dqn, qc], [q_norm_g], [(QL, BF16)], [(1, QL)], name="q_norm_bwd")

    def kv_norm_bwd_fn(dn, dkr, t, g):
        kvc = t[:, :KVL]
        r = _rms_stats(kvc, KVL)
        tn = kvc * r
        return (jnp.concatenate([_rms_bwd(dn * g, tn, r, KVL), dkr], axis=1),), (_colsum(dn * tn),)

    (dkvin,), (d_kvng,) = _rowwise(kv_norm_bwd_fn, [dkvn, dkrope, kvin], [kv_norm_g], [(KVP, BF16)], [(1, KVL)],
                                   name="kv_norm_bwd")
    dkvin_x, dkvin_c = dkvin[:N], dkvin[N:]

    dh = _mm([(dqc, w_q), (dkvin_x, w_kv), (du_in, w_u), (dv_in, w_v), (dgi1, w_g1), (dgi2, w_g2)], tb=True,
             name="proj_dx", outs=(F32,), tn=512, tk=1024)
    dctx_h = _mm([(dkvin_c, w_kv)], tb=True, name="proj_kv_ctx_dx", outs=(F32,))
    dw_q = _mm([(h, dqc)], ta=True, name="proj_q_dw", outs=(BF16,))
    dw_kv = _mm([(h, dkvin_x), (ctx_h, dkvin_c)], ta=True, name="proj_kv_dw", outs=(BF16,))
    dw_u = _mm([(h, du_in)], ta=True, name="proj_u_dw", outs=(BF16,))
    dw_v = _mm([(h, dv_in)], ta=True, name="proj_v_dw", outs=(BF16,))
    dw_g1 = _mm([(h, dgi1)], ta=True, name="proj_g1_dw", outs=(BF16,))
    dw_g2 = _mm([(h, dgi2)], ta=True, name="proj_g2_dw", outs=(BF16,))

    def norm1_bwd_fn(dhv, t, dres, g, sc):
        r = _rms_stats(t, D)
        tn = t * r
        dxg = dhv * (1.0 + sc)
        return (dres + _rms_bwd(dxg * g, tn, r, D),), (_colsum(dhv), _colsum(dhv * (tn * g)), _colsum(dxg * tn))

    (grad_x,), (dsh1, dsc1, dn1g_x) = _rowwise(norm1_bwd_fn, [dh, x2, dx1], [norm1_g, sc1], [(D, F32)], [(1, D)] * 3,
                                               name="norm1_bwd")
    _, (dsh1c, dsc1c, dn1g_c) = _rowwise(norm1_bwd_fn, [dctx_h, ctx2, jnp.zeros_like(ctx2)], [norm1_g, sc1c],
                                         [(D, F32)], [(1, D)] * 3, name="norm1_ctx_bwd")

    small = [dsh1, dsc1, dg1, dsh2, dsc2, dg2,
             dsh1c, dsc1c, dn1g_x, dn1g_c, d_qng, d_kvng, d_gq, d_gk, d_sng, d_snb, dn2g, loss_cols]
    small_sizes = [a.shape[1] for a in small]
    sm_row = jnp.concatenate(small, axis=1)
    sm_mat = jnp.concatenate([d_ws.reshape(G * CH, CH), d_bs], axis=0)
    row_all, mat_all = _all_gather8([sm_row, sm_mat], name="ag_small", in_vmem=True)
    row_sum = _sum_blocks(row_all, name="sum_small_rows", out_dtype=F32)
    mat_sum = _sum_blocks(mat_all, name="sum_small_mats", out_dtype=F32)
    dmod_rows = row_all[:, 0, :NMOD]
    (_, _, _, _, _, _, t_sh1c, t_sc1c, t_n1x, t_n1c, g_qng, g_kvng, t_gq, t_gk, g_sng, g_snb, g_n2g,
     t_loss) = _split_lanes(row_sum, small_sizes)
    g_ws, t_bs = mat_sum[:G * CH], mat_sum[G * CH:]
    dmodc_row = jnp.concatenate([t_sh1c, t_sc1c, jnp.zeros((1, NMOD - 2 * D), F32)], axis=1)
    dmod16 = jnp.concatenate([dmod_rows, dmodc_row, jnp.zeros((BF16_SUBLANES - 9, NMOD), F32)], axis=0)

    def small_fn(rows, n1x, n1c, lossv):
        return (), (_colsum(rows), n1x + n1c, jnp.sum(lossv, axis=1, keepdims=True))

    _, (g_bmod, g_n1g, loss11) = _rowwise(small_fn, [dmod16], [t_n1x, t_n1c, t_loss], [], [(1, NMOD), (1, D), (1, 1)],
                                          name="small_reduce", tm=16)
    dmod_loc = lax.dynamic_slice_in_dim(dmod16, my_chip * NM, NM, axis=1)
    g_wmod = _mm([(silu_c, dmod_loc)], ta=True, name="mod_dw", outs=(F32,), tn=512)
    dsilu_part = _mm([(dmod_loc, wm)], tb=True, name="mod_dx", outs=(F32,), tk=512)
    part_all = _all_gather8([dsilu_part[8:9]], name="ag_cctx", in_vmem=True)[0]

    def cctx_fn(parts, dsl):
        return (), ((parts[0:1] + parts[2:3] + parts[4:5] + parts[6:7]) * dsl,)

    _, (g_cctx,) = _rowwise(cctx_fn, [part_all[:, 0, :]], [dsilu_c[8:9]], [], [(1, D)], name="cctx_grad", tm=8)

    dw_in_f = jnp.concatenate([dw_q, dw_kv[:, :KVL + ROPE_DIM], dw_u, dw_v, dw_g1, dw_g2], axis=1)
    dw_uq_f = dw_uq_p.reshape(QL, H, HEAD_PAD)[:, :, :QK].reshape(QL, H * QK)
    dfull = [dw_in_f, dw_uq_f, dw_ukv, dw_bra, dw_brs, dw_out, dw_ffi, dw_ffo]
    core = jnp.reshape(ac, (1,)).astype(jnp.int32)
    g4 = []
    for dwf, a, cs in zip(dfull, big, col_sharded):
        K, Ns = a.shape
        if dwf.ndim == 2:
            dwf = dwf.reshape(K, 4, Ns).transpose(1, 0, 2) if cs else dwf.reshape(4, K, Ns)
        g4.append(dwf.reshape(4, 2, K // 2, Ns))
    sib = _pair_swap_other(g4, name="rs_pair")
    pair = [_pair_add(g, s, core, name="rs_pair_add_" + t) for g, s, t in zip(g4, sib, tags)]
    xchg = [lax.dynamic_update_index_in_dim(t4, lax.dynamic_index_in_dim(pr, my_chip, 0, keepdims=False), my_chip, 0)
            for t4, pr in zip(_chip_exchange(pair, name="rs_chips"), pair)]
    red_half = [_sum_blocks(t4, name="rs_sum_" + t, out_dtype=F32) for t4, t in zip(xchg, tags)]
    big_grads = [lax.dynamic_update_index_in_dim(r, mine, ac, 0).reshape(a.shape)
                 for r, mine, a in zip(_pair_gather(red_half, name="rs_halves"), red_half, big)]

    def upd(w, g, m, v, nm):
        shape = w.shape
        w2, g2_, m2, v2 = [t.reshape(-1, shape[-1]) for t in (w, g, m, v)]
        d_, m_, v_ = _adamw(w2, g2_, m2, v2, name="adamw_" + nm)
        return g.reshape(shape), d_.reshape(shape), m_.reshape(shape), v_.reshape(shape)

    g_in, g_uq, g_ukv, g_bra, g_brs, g_out, g_ffi, g_ffo = big_grads
    grads = dict(
        c_ctx=g_cctx.reshape(D), w_mod=g_wmod[None], b_mod=g_bmod, norm1_g=g_n1g, w_in=g_in[None],
        q_norm_g=g_qng, kv_norm_g=g_kvng, w_uq=g_uq[None], w_ukv=g_ukv[None],
        qk_norm_q=t_gq[:, :QK], qk_norm_k=t_gk[:, :QK], sgu_norm_g=g_sng, sgu_norm_b=g_snb,
        w_spatial=g_ws.reshape(w_spatial.shape), b_spatial=t_bs[:, :G].T[None],
        w_br_attn=g_bra[None], w_br_sgu=g_brs[None], w_out=g_out[None], norm2_g=g_n2g,
        w_ffn_in=g_ffi[None], w_ffn_out=g_ffo[None])
    weights = dict(c_ctx=c_ctx, w_mod=w_mod, b_mod=b_mod, norm1_g=norm1_g, w_in=w_in, q_norm_g=q_norm_g,
                   kv_norm_g=kv_norm_g, w_uq=w_uq, w_ukv=w_ukv, qk_norm_q=qk_norm_q, qk_norm_k=qk_norm_k,
                   sgu_norm_g=sgu_norm_g, sgu_norm_b=sgu_norm_b, w_spatial=w_spatial, b_spatial=b_spatial,
                   w_br_attn=w_br_attn, w_br_sgu=w_br_sgu, w_out=w_out, norm2_g=norm2_g, w_ffn_in=w_ffn_in,
                   w_ffn_out=w_ffn_out)
    m_in = dict(c_ctx=m_c_ctx, w_mod=m_w_mod, b_mod=m_b_mod, norm1_g=m_norm1_g, w_in=m_w_in, q_norm_g=m_q_norm_g,
                kv_norm_g=m_kv_norm_g, w_uq=m_w_uq, w_ukv=m_w_ukv, qk_norm_q=m_qk_norm_q, qk_norm_k=m_qk_norm_k,
                sgu_norm_g=m_sgu_norm_g, sgu_norm_b=m_sgu_norm_b, w_spatial=m_w_spatial, b_spatial=m_b_spatial,
                w_br_attn=m_w_br_attn, w_br_sgu=m_w_br_sgu, w_out=m_w_out, norm2_g=m_norm2_g, w_ffn_in=m_w_ffn_in,
                w_ffn_out=m_w_ffn_out)
    v_in_ = dict(c_ctx=v_c_ctx, w_mod=v_w_mod, b_mod=v_b_mod, norm1_g=v_norm1_g, w_in=v_w_in, q_norm_g=v_q_norm_g,
                 kv_norm_g=v_kv_norm_g, w_uq=v_w_uq, w_ukv=v_w_ukv, qk_norm_q=v_qk_norm_q, qk_norm_k=v_qk_norm_k,
                 sgu_norm_g=v_sgu_norm_g, sgu_norm_b=v_sgu_norm_b, w_spatial=v_w_spatial, b_spatial=v_b_spatial,
                 w_br_attn=v_w_br_attn, w_br_sgu=v_w_br_sgu, w_out=v_w_out, norm2_g=v_norm2_g, w_ffn_in=v_w_ffn_in,
                 w_ffn_out=v_w_ffn_out)
    names = list(weights)
    big_names = ("w_mod", "w_in", "w_uq", "w_ukv", "w_br_attn", "w_br_sgu", "w_out", "w_ffn_in", "w_ffn_out")
    out_g, out_d, out_m, out_v = {}, {}, {}, {}
    for nm in big_names:
        out_g[nm], out_d[nm], out_m[nm], out_v[nm] = upd(weights[nm], grads[nm], m_in[nm], v_in_[nm], nm)
    row_names = [nm for nm in names if nm not in big_names and nm not in ("w_spatial", "b_spatial")]
    widths = [-(-weights[nm].size // LANES) * LANES for nm in row_names]

    def as_row(d):
        return jnp.concatenate([jnp.pad(d[nm].reshape(1, -1), ((0, 0), (0, wd - d[nm].size)))
                                for nm, wd in zip(row_names, widths)], axis=1)

    def as_mat(d):
        return jnp.concatenate([d["w_spatial"].reshape(G * CH, CH), d["b_spatial"].reshape(G, CH)], axis=0)

    row_res = _adamw(as_row(weights), as_row(grads), as_row(m_in), as_row(v_in_), name="adamw_rows")
    mat_res = _adamw(as_mat(weights), as_mat(grads), as_mat(m_in), as_mat(v_in_), name="adamw_spatial")
    for tgt, row, mat in zip((out_d, out_m, out_v), row_res, mat_res):
        for nm, seg in zip(row_names, _split_lanes(row, widths)):
            tgt[nm] = seg[:, :weights[nm].size].reshape(weights[nm].shape)
        tgt["w_spatial"] = mat[:G * CH].reshape(w_spatial.shape)
        tgt["b_spatial"] = mat[G * CH:].reshape(b_spatial.shape)
    for nm in row_names + ["w_spatial", "b_spatial"]:
        out_g[nm] = grads[nm].reshape(weights[nm].shape)

    loss = loss11.reshape(())
    return (loss, grad_x[None], *[out_g[n] for n in names], *[out_d[n] for n in names],
            *[out_m[n] for n in names], *[out_v[n] for n in names])
```

```python
import math

import jax
import jax.numpy as jnp
from jax import lax
from jax.experimental import pallas as pl
from jax.experimental.pallas import tpu as pltpu

F32, BF16 = jnp.float32, jnp.bfloat16
MESH = pl.DeviceIdType.MESH

LANES = 128
BF16_SUBLANES = 16
VMEM_LIMIT_BYTES = 56 * 1024 * 1024

EPS = 1e-6
ROPE_DIM = 64
ROPE_THETA = 10000.0
GRID_W = 64
HEAD_PAD = 256
ADAM_LR, ADAM_B1, ADAM_B2, ADAM_EPS, ADAM_WD, ADAM_STEP = 0.001, 0.9, 0.999, 1e-08, 0.01, 10


def _tile(dim, pref, align=LANES):
    if dim <= pref:
        return dim
    t = (pref // align) * align
    while t >= align:
        if dim % t == 0:
            return t
        t -= align
    return dim


def _params(sem=None):
    return pltpu.CompilerParams(dimension_semantics=sem, vmem_limit_bytes=VMEM_LIMIT_BYTES)


def _sds(shape, dtype):
    return jax.ShapeDtypeStruct(tuple(shape), dtype)


def _mm(pairs, *, name, ta=False, tb=False, outs=(F32,), tm=1024, tn=1024, tk=2048, extras=(), epi=None,
        split=None, into=None, carry=None):
    dual = len(pairs[0]) == 3
    a0, b0 = pairs[0][0], pairs[0][1]
    M = a0.shape[1] if ta else a0.shape[0]
    N = b0.shape[0] if tb else b0.shape[1]
    tm, tn = _tile(M, tm), _tile(N if split is None else split, tn)
    ks = [(p[0].shape[0] if ta else p[0].shape[1]) for p in pairs]
    tks = [_tile(k, tk) for k in ks]
    nks = [k // t for k, t in zip(ks, tks)]
    offs = [sum(nks[:i]) for i in range(len(pairs))]
    nk_total = sum(nks)
    single = len(pairs) == 1

    def kidx(kk, p):
        return kk if single else jnp.clip(kk - offs[p], 0, nks[p] - 1)

    in_specs, operands = [], []
    for p, pr in enumerate(pairs):
        if ta:
            in_specs.append(pl.BlockSpec((tks[p], tm), lambda i, j, kk, p=p: (kidx(kk, p), i)))
        else:
            in_specs.append(pl.BlockSpec((tm, tks[p]), lambda i, j, kk, p=p: (i, kidx(kk, p))))
        operands.append(pr[0])
        for b in pr[1:]:
            if tb:
                in_specs.append(pl.BlockSpec((tn, tks[p]), lambda i, j, kk, p=p: (j, kidx(kk, p))))
            else:
                in_specs.append(pl.BlockSpec((tks[p], tn), lambda i, j, kk, p=p: (kidx(kk, p), j)))
            operands.append(b)
    for arr, kind in extras:
        if kind == "mn":
            in_specs.append(pl.BlockSpec((tm, tn), lambda i, j, kk: (i, j)))
        else:
            in_specs.append(pl.BlockSpec((1, tn), lambda i, j, kk: (0, j)))
        operands.append(arr)
    n_in = len(operands)
    n_ex = len(extras)
    per = 3 if dual else 2
    dims = (((0 if ta else 1,), (1 if tb else 0,)), ((), ()))

    n_acc = 2 if dual else 1

    def products(ins, p):
        a = ins[per * p][...].astype(BF16)
        return [lax.dot_general(a, ins[per * p + 1 + q][...].astype(BF16), dims, preferred_element_type=F32)
                for q in range(n_acc)]

    def finish(ins, out_refs, acc_vals):
        vals = acc_vals + [r[...] for r in ins[n_in - n_ex:]]
        res = epi(*vals) if epi is not None else (vals[0],)
        for o, r in zip(out_refs, res):
            o[...] = r.astype(o.dtype)

    out_specs = [pl.BlockSpec((tm, tn), lambda i, j, kk: (i, j)) for _ in outs]
    out_shape = [_sds((M, N), d) for d in outs]
    aliases = {}
    n_alias = 0
    if split is not None:
        nps = split // tn
        lead = 0 if into is None else into[1]
        out_specs = [pl.BlockSpec((None, tm, tn), lambda i, j, kk: (j // nps + lead, i, j % nps))]
        out_shape = [_sds((N // split if into is None else into[0].shape[0], M, split), outs[0])]
        if into is not None:
            in_specs.append(pl.BlockSpec(memory_space=pl.ANY))
            operands.append(into[0])
            aliases, n_alias = {n_in: 0}, 1

    grid = (M // tm, N // tn, nk_total)

    def at_step(first):
        ids = [pl.program_id(d) for d in range(3)]
        cond = None
        for d, g in zip(ids, grid):
            t = d == (0 if first else g - 1)
            cond = t if cond is None else cond & t
        return cond

    def body(*refs):
        ins, out_refs, accs, start, wait = _split_refs(refs, n_in + n_alias, len(outs), carry)
        ins = ins[:n_in]
        if carry is not None:
            pl.when(at_step(True))(start)
        if nk_total == 1:
            finish(ins, out_refs, products(ins, 0))
        else:
            kk = pl.program_id(2)

            @pl.when(kk == 0)
            def _():
                for acc, v in zip(accs, products(ins, 0)):
                    acc[...] = v

            for p in range(len(pairs)):
                lo = max(offs[p], 1)

                @pl.when((kk >= lo) & (kk < offs[p] + nks[p]))
                def _(p=p):
                    for acc, v in zip(accs, products(ins, p)):
                        acc[...] += v

            @pl.when(kk == nk_total - 1)
            def _():
                finish(ins, out_refs, [acc[...] for acc in accs])
        if carry is not None:
            pl.when(at_step(False))(wait)

    ex = carry
    res = pl.pallas_call(
        body, name=name, grid=grid, in_specs=in_specs + ([] if ex is None else ex.in_specs),
        out_specs=out_specs + ([] if ex is None else ex.out_specs),
        out_shape=out_shape + ([] if ex is None else ex.out_shape), input_output_aliases=aliases,
        scratch_shapes=[pltpu.VMEM((tm, tn), F32) for _ in range(n_acc if nk_total > 1 else 0)]
        + ([] if ex is None else ex.scratch),
        compiler_params=_params(("arbitrary",) * 3 if ex is not None else ("parallel", "parallel", "arbitrary")),
    )(*operands, *([] if ex is None else ex.xs))
    if ex is not None:
        return (res[0] if len(outs) == 1 else res[:len(outs)]), list(res[len(outs):])
    return res[0] if len(outs) == 1 else res


def _rowwise(fn, rows, vecs, out_rows, out_accs=(), *, name, tm=256, tc=None):
    M = rows[0].shape[0]
    tm = _tile(M, tm, BF16_SUBLANES)
    nrow = M // tm
    C = rows[0].shape[1]
    ncol = 1 if tc is None else C // _tile(C, tc)
    tcol = None if tc is None else _tile(C, tc)

    def colwise(shape):
        return tc is not None and len(shape) == 2 and shape[0] == 1 and shape[1] == C

    def vspec(shape):
        if colwise(shape):
            return pl.BlockSpec((1, tcol), lambda j, i: (0, j))
        return pl.BlockSpec(tuple(shape), lambda j, i, n=len(shape): (0,) * n)

    def rspec(width):
        if tc is None:
            return pl.BlockSpec((tm, width), lambda j, i: (i, 0))
        return pl.BlockSpec((tm, tcol), lambda j, i: (i, j))

    in_specs = [rspec(r.shape[1]) for r in rows] + [vspec(v.shape) for v in vecs]
    out_specs = [rspec(c) for c, _ in out_rows] + [vspec(s) for s in out_accs]
    out_shape = [_sds((M, c), d) for c, d in out_rows] + [_sds(s, F32) for s in out_accs]
    n_in, n_or = len(rows) + len(vecs), len(out_rows)

    def body(*refs):
        ins, o_rows, o_accs = refs[:n_in], refs[n_in:n_in + n_or], refs[n_in + n_or:]
        r_out, a_out = fn(*[r[...] for r in ins])
        for o, r in zip(o_rows, r_out):
            o[...] = r.astype(o.dtype)
        i = pl.program_id(1)

        @pl.when(i == 0)
        def _():
            for o, a in zip(o_accs, a_out):
                o[...] = a

        @pl.when(i > 0)
        def _():
            for o, a in zip(o_accs, a_out):
                o[...] += a

    res = pl.pallas_call(
        body, name=name, grid=(ncol, nrow), in_specs=in_specs, out_specs=out_specs, out_shape=out_shape,
        compiler_params=_params(("parallel", "arbitrary")),
    )(*rows, *vecs)
    return res[:n_or], res[n_or:]


def _colsum(t):
    return jnp.sum(t, axis=0, keepdims=True)


def _gelu(t):
    return 0.5 * t * (1.0 + lax.erf(t * math.sqrt(0.5)))


def _gelu_grad(t):
    return 0.5 * (1.0 + lax.erf(t * math.sqrt(0.5))) + t * jnp.exp(-0.5 * t * t) * (1.0 / math.sqrt(2.0 * math.pi))


def _sigmoid(t):
    return 1.0 / (1.0 + jnp.exp(-t))


def _rms_stats(t, width):
    return lax.rsqrt(jnp.sum(t * t, axis=-1, keepdims=True) * (1.0 / width) + EPS)


def _rms_bwd(dn, tn, r, width):
    return r * (dn - tn * (jnp.sum(dn * tn, axis=-1, keepdims=True) * (1.0 / width)))


def _place():
    return lax.axis_index("x"), lax.axis_index("y"), lax.axis_index("c")


class _ChipExchange:
    def __init__(self, xs, gather):
        self.xs, self.gather, self.n = list(xs), gather, len(xs)
        self.in_specs = [pl.BlockSpec(memory_space=pl.ANY)] * self.n
        self.out_specs = [pl.BlockSpec(memory_space=pl.ANY)] * self.n
        self.out_shape = [_sds((4,) + (x.shape if gather else x.shape[1:]), x.dtype) for x in self.xs]
        self.scratch = [pltpu.SemaphoreType.DMA((self.n, 3)), pltpu.SemaphoreType.DMA((self.n, 3))]

    def bind(self, x_refs, out_refs, send_sems, recv_sems):
        x, y, c = _place()
        p = 2 * x + y
        chips = [(1 - x, y), (x, 1 - y), (1 - x, 1 - y)]

        def copy(w, k, outgoing):
            qx, qy = chips[k]
            there = 2 * qx + qy
            if self.gather:
                src = x_refs[w]
            else:
                src = x_refs[w].at[there if outgoing else p]
            return pltpu.make_async_remote_copy(
                src_ref=src, dst_ref=out_refs[w].at[p if outgoing else there], send_sem=send_sems.at[w, k],
                recv_sem=recv_sems.at[w, k], device_id=(qx, qy, c), device_id_type=MESH)

        def start():
            for w in range(self.n):
                for k in range(3):
                    copy(w, k, True).start()

        def wait():
            for w in range(self.n):
                for k in range(3):
                    copy(w, k, False).wait_recv()
            for w in range(self.n):
                for k in range(3):
                    copy(w, k, True).wait_send()

        return start, wait


def _split_refs(refs, n_in, n_out, ex):
    ne = 0 if ex is None else ex.n
    ins, xin = refs[:n_in], refs[n_in:n_in + ne]
    outs, xout = refs[n_in + ne:n_in + ne + n_out], refs[n_in + ne + n_out:n_in + 2 * ne + n_out]
    rest = refs[n_in + 2 * ne + n_out:]
    if ex is None:
        return ins, outs, rest, None, None
    start, wait = ex.bind(xin, xout, rest[-2], rest[-1])
    return ins, outs, rest[:-2], start, wait


def _attn_fwd(q, k, v, *, heads, scale, tq=256, carry=None):
    N, M = q.shape[0], k.shape[0]
    tq = _tile(N, tq)
    vd = v.shape[1] // heads
    nq = N // tq

    def body(*refs):
        (q_ref, k_ref, v_ref), (o_ref, lse_ref), _, start, wait = _split_refs(refs, 3, 2, carry)
        if carry is not None:
            pl.when((pl.program_id(0) == 0) & (pl.program_id(1) == 0))(start)
        s = lax.dot_general(q_ref[...], k_ref[...], (((1,), (1,)), ((), ())), preferred_element_type=F32) * scale
        m = jnp.max(s, axis=-1, keepdims=True)
        p = jnp.exp(s - m)
        l = jnp.sum(p, axis=-1, keepdims=True)
        o = jnp.dot(p.astype(BF16), v_ref[...], preferred_element_type=F32) / l
        o_ref[...] = o.astype(o_ref.dtype)
        lse_ref[...] = jnp.broadcast_to(m + jnp.log(l), lse_ref.shape)
        if carry is not None:
            pl.when((pl.program_id(0) == heads - 1) & (pl.program_id(1) == nq - 1))(wait)

    ex = carry
    res = pl.pallas_call(
        body, name="attn_fwd", grid=(heads, nq),
        in_specs=[pl.BlockSpec((tq, HEAD_PAD), lambda h, i: (i, h)),
                  pl.BlockSpec((M, HEAD_PAD), lambda h, i: (0, h)),
                  pl.BlockSpec((M, vd), lambda h, i: (0, h))] + ([] if ex is None else ex.in_specs),
        out_specs=[pl.BlockSpec((tq, vd), lambda h, i: (i, h)),
                   pl.BlockSpec((tq, vd), lambda h, i: (i, h))] + ([] if ex is None else ex.out_specs),
        out_shape=[_sds((N, heads * vd), BF16), _sds((N, heads * vd), F32)] + ([] if ex is None else ex.out_shape),
        scratch_shapes=[] if ex is None else ex.scratch,
        compiler_params=_params(("arbitrary", "arbitrary")),
    )(q, k, v, *([] if ex is None else ex.xs))
    return res[0], res[1], list(res[2:])


def _attn_bwd(q, k, v, o, lse, do, *, heads, scale, tq=256, carry=None):
    N, M = q.shape[0], k.shape[0]
    tq = _tile(N, tq)
    vd = v.shape[1] // heads
    nq = N // tq
    nt = (((1,), (1,)), ((), ()))
    tn = (((0,), (0,)), ((), ()))

    def body(*refs):
        (q_ref, k_ref, v_ref, o_ref, lse_ref, do_ref), (dq_ref, dk_ref, dv_ref), _, start, wait = _split_refs(
            refs, 6, 3, carry)
        if carry is not None:
            pl.when((pl.program_id(0) == 0) & (pl.program_id(1) == 0))(start)
        i = pl.program_id(1)
        qb, kb, dob = q_ref[...], k_ref[...], do_ref[...]
        s = lax.dot_general(qb, kb, nt, preferred_element_type=F32) * scale
        p = jnp.exp(s - lse_ref[...][:, :1])
        dp = lax.dot_general(dob, v_ref[...], nt, preferred_element_type=F32)
        delta = jnp.sum(dob.astype(F32) * o_ref[...].astype(F32), axis=-1, keepdims=True)
        ds = (p * (dp - delta) * scale).astype(BF16)
        pb = p.astype(BF16)
        dq_ref[...] = jnp.dot(ds, kb, preferred_element_type=F32)
        dk_part = lax.dot_general(ds, qb, tn, preferred_element_type=F32)
        dv_part = lax.dot_general(pb, dob, tn, preferred_element_type=F32)

        @pl.when(i == 0)
        def _():
            dk_ref[...] = dk_part
            dv_ref[...] = dv_part

        @pl.when(i > 0)
        def _():
            dk_ref[...] += dk_part
            dv_ref[...] += dv_part

        if carry is not None:
            pl.when((pl.program_id(0) == heads - 1) & (pl.program_id(1) == nq - 1))(wait)

    ex = carry
    res = pl.pallas_call(
        body, name="attn_bwd", grid=(heads, nq),
        in_specs=[pl.BlockSpec((tq, HEAD_PAD), lambda h, i: (i, h)),
                  pl.BlockSpec((M, HEAD_PAD), lambda h, i: (0, h)),
                  pl.BlockSpec((M, vd), lambda h, i: (0, h)),
                  pl.BlockSpec((tq, vd), lambda h, i: (i, h)),
                  pl.BlockSpec((tq, vd), lambda h, i: (i, h)),
                  pl.BlockSpec((tq, vd), lambda h, i: (i, h))] + ([] if ex is None else ex.in_specs),
        out_specs=[pl.BlockSpec((tq, HEAD_PAD), lambda h, i: (i, h)),
                   pl.BlockSpec((M, HEAD_PAD), lambda h, i: (0, h)),
                   pl.BlockSpec((M, vd), lambda h, i: (0, h))] + ([] if ex is None else ex.out_specs),
        out_shape=[_sds((N, heads * HEAD_PAD), F32), _sds((M, heads * HEAD_PAD), F32),
                   _sds((M, heads * vd), F32)] + ([] if ex is None else ex.out_shape),
        scratch_shapes=[] if ex is None else ex.scratch,
        compiler_params=_params(("arbitrary", "arbitrary")),
    )(q, k, v, o, lse, do, *([] if ex is None else ex.xs))
    return res[0], res[1], res[2], list(res[3:])


def _comm_call(body, xs, out_shapes, n_sems, name, in_vmem):
    space = pltpu.VMEM if in_vmem else pl.ANY
    n = len(xs)

    def wrapped(*refs):
        body(refs[:n], refs[n:2 * n], *refs[2 * n:])

    return pl.pallas_call(
        wrapped, name=name, out_shape=list(out_shapes),
        in_specs=[pl.BlockSpec(memory_space=space)] * n, out_specs=[pl.BlockSpec(memory_space=space)] * n,
        scratch_shapes=[pltpu.SemaphoreType.DMA((n, n_sems)), pltpu.SemaphoreType.DMA((n, n_sems)),
                        pltpu.SemaphoreType.DMA((n,))],
        compiler_params=pltpu.CompilerParams(vmem_limit_bytes=VMEM_LIMIT_BYTES),
    )(*xs)


def _all_gather8(blks, *, name, in_vmem, others_only=False):
    def body(x_refs, out_refs, send_sems, recv_sems, local_sems):
        x, y, c = _place()
        me, sibling = (x, y, c), (x, y, 1 - c)
        chips = [(1 - x, y), (x, 1 - y), (1 - x, 1 - y)]
        waits = []
        for w, (x_ref, out_ref) in enumerate(zip(x_refs, out_refs)):
            def slot(px, py, pc, out_ref=out_ref):
                return out_ref.at[4 * px + 2 * py + pc]

            def copy(k, block, to, src=None, w=w, slot=slot):
                return pltpu.make_async_remote_copy(
                    src_ref=slot(*block) if src is None else src, dst_ref=slot(*block),
                    send_sem=send_sems.at[w, k], recv_sem=recv_sems.at[w, k], device_id=to, device_id_type=MESH)

            mine = None
            first = []
            if not others_only:
                mine = pltpu.make_async_copy(x_ref, slot(*me), local_sems.at[w])
                mine.start()
                first.append(copy(0, me, sibling, src=x_ref))
            first += [copy(1 + j, me, (*chip, c), src=x_ref) for j, chip in enumerate(chips)]
            for cp in first:
                cp.start()
            waits.append((copy, mine, first))
        for copy, mine, first in waits:
            passed = [copy(4 + j, (*chip, c), sibling) for j, chip in enumerate(chips)]
            for j, chip in enumerate(chips):
                copy(1 + j, (*chip, c), me).wait_recv()
                passed[j].start()
            if not others_only:
                copy(0, sibling, me).wait_recv()
            for j, chip in enumerate(chips):
                copy(4 + j, (*chip, 1 - c), me).wait_recv()
            for cp in first + passed:
                cp.wait_send()
            if mine is not None:
                mine.wait()

    return _comm_call(body, blks, [_sds((8,) + b.shape, b.dtype) for b in blks], 7, name, in_vmem)


def _pair_gather(blks, *, name):
    def body(x_refs, out_refs, send_sems, recv_sems, local_sems):
        x, y, c = _place()
        for w, (x_ref, out_ref) in enumerate(zip(x_refs, out_refs)):
            pltpu.make_async_remote_copy(src_ref=x_ref, dst_ref=out_ref.at[c], send_sem=send_sems.at[w, 0],
                                         recv_sem=recv_sems.at[w, 0], device_id=(x, y, 1 - c),
                                         device_id_type=MESH).start()
        for w, (x_ref, out_ref) in enumerate(zip(x_refs, out_refs)):
            cp = pltpu.make_async_remote_copy(src_ref=x_ref, dst_ref=out_ref.at[1 - c], send_sem=send_sems.at[w, 0],
                                              recv_sem=recv_sems.at[w, 0], device_id=(x, y, 1 - c), device_id_type=MESH)
            cp.wait_recv()
            cp.wait_send()

    return _comm_call(body, blks, [_sds((2,) + b.shape, b.dtype) for b in blks], 1, name, False)


def _pair_swap_other(bufs, *, name):
    def body(x_refs, out_refs, send_sems, recv_sems, local_sems):
        x, y, c = _place()
        for w, (x_ref, out_ref) in enumerate(zip(x_refs, out_refs)):
            for q in range(4):
                pltpu.make_async_remote_copy(src_ref=x_ref.at[q, 1 - c], dst_ref=out_ref.at[q],
                                             send_sem=send_sems.at[w, 0], recv_sem=recv_sems.at[w, 0],
                                             device_id=(x, y, 1 - c), device_id_type=MESH).start()
        for w, out_ref in enumerate(out_refs):
            pltpu.make_async_remote_copy(src_ref=out_ref, dst_ref=out_ref, send_sem=send_sems.at[w, 0],
                                         recv_sem=recv_sems.at[w, 0], device_id=(x, y, 1 - c),
                                         device_id_type=MESH).wait()

    return _comm_call(body, bufs, [_sds((4,) + b.shape[2:], b.dtype) for b in bufs], 1, name, False)


def _chip_exchange(bufs, *, name):
    ex = _ChipExchange(bufs, gather=False)

    def body(x_refs, out_refs, send_sems, recv_sems, local_sems):
        start, wait = ex.bind(x_refs, out_refs, send_sems, recv_sems)
        start()
        wait()

    return _comm_call(body, bufs, ex.out_shape, 3, name, False)


def _pair_forward(bufs, *, name):
    def body(x_refs, out_refs, send_sems, recv_sems, local_sems):
        x, y, c = _place()
        chips = [(1 - x, y), (x, 1 - y), (1 - x, 1 - y)]
        cps = [pltpu.make_async_remote_copy(src_ref=x_ref.at[2 * qx + qy], dst_ref=out_ref.at[2 * qx + qy],
                                            send_sem=send_sems.at[w, k], recv_sem=recv_sems.at[w, k],
                                            device_id=(x, y, 1 - c), device_id_type=MESH)
               for w, (x_ref, out_ref) in enumerate(zip(x_refs, out_refs)) for k, (qx, qy) in enumerate(chips)]
        for cp in cps:
            cp.start()
        for cp in cps:
            cp.wait()

    return _comm_call(body, bufs, [_sds(b.shape, b.dtype) for b in bufs], 3, name, False)


def _block_rows(rows, row_bytes, target=1 << 20, align=BF16_SUBLANES):
    return _tile(rows, max(align, target // row_bytes // align * align), align)


def _sum_blocks(buf, *, name, out_dtype):
    B, R, C = buf.shape
    tm = _block_rows(R, B * C * buf.dtype.itemsize)

    def body(x_ref, o_ref):
        acc = x_ref[0].astype(F32)
        for b in range(1, B):
            acc = acc + x_ref[b].astype(F32)
        o_ref[...] = acc.astype(o_ref.dtype)

    return pl.pallas_call(
        body, name=name, grid=(R // tm,), in_specs=[pl.BlockSpec((B, tm, C), lambda i: (0, i, 0))],
        out_specs=pl.BlockSpec((tm, C), lambda i: (i, 0)), out_shape=_sds((R, C), out_dtype),
        compiler_params=_params(("parallel",)),
    )(buf)


def _pair_add(mine, theirs, core, *, name):
    _, _, R, C = mine.shape
    tm = _block_rows(R, C * 2)

    def body(core_ref, a_ref, b_ref, o_ref):
        o_ref[...] = (a_ref[...].astype(F32) + b_ref[...].astype(F32)).astype(o_ref.dtype)

    return pl.pallas_call(
        body, name=name, out_shape=_sds(theirs.shape, BF16),
        grid_spec=pltpu.PrefetchScalarGridSpec(
            num_scalar_prefetch=1, grid=(4, R // tm),
            in_specs=[pl.BlockSpec((None, None, tm, C), lambda q, i, core_ref: (q, core_ref[0], i, 0)),
                      pl.BlockSpec((None, tm, C), lambda q, i, core_ref: (q, i, 0))],
            out_specs=pl.BlockSpec((None, tm, C), lambda q, i, core_ref: (q, i, 0))),
        compiler_params=_params(("parallel", "parallel")),
    )(core, mine, theirs)


def _assemble(gathered, own, chip, *, name, transpose):
    _, K, Ns = gathered.shape
    tm = _block_rows(K, Ns * 4)

    def body(chip_ref, g_ref, own_ref, o_ref):
        q = pl.program_id(0)

        @pl.when(q == chip_ref[0])
        def _():
            o_ref[...] = own_ref[...].astype(BF16)

        @pl.when(q != chip_ref[0])
        def _():
            o_ref[...] = g_ref[...]

    if transpose:
        out_spec = pl.BlockSpec((tm, Ns), lambda q, i, ch: (i, q))
        out_shape = _sds((K, 4 * Ns), BF16)
    else:
        out_spec = pl.BlockSpec((None, tm, Ns), lambda q, i, ch: (q, i, 0))
        out_shape = _sds((4, K, Ns), BF16)
    return pl.pallas_call(
        body, name=name, out_shape=out_shape,
        grid_spec=pltpu.PrefetchScalarGridSpec(
            num_scalar_prefetch=1, grid=(4, K // tm),
            in_specs=[pl.BlockSpec((None, tm, Ns), lambda q, i, ch: (jnp.where(q == ch[0], (q + 1) % 4, q), i, 0)),
                      pl.BlockSpec((tm, Ns), lambda q, i, ch: (jnp.where(q == ch[0], i, 0), 0))],
            out_specs=out_spec),
        compiler_params=_params(("arbitrary", "arbitrary")),
    )(chip, gathered, own)


def _assemble_halves(mine, theirs, own, place, *, name, transpose):
    _, K2, Ns = mine.shape
    tm = _block_rows(K2, Ns * 4)
    nb = K2 // tm

    def body(place_ref, m_ref, t_ref, own_ref, o_ref):
        q, hb = pl.program_id(0), pl.program_id(1)
        is_own = q == place_ref[0]
        is_mine = hb == place_ref[1]

        @pl.when(is_own)
        def _():
            o_ref[...] = own_ref[...].astype(BF16)

        @pl.when(jnp.logical_not(is_own) & is_mine)
        def _():
            o_ref[...] = m_ref[...]

        @pl.when(jnp.logical_not(is_own) & jnp.logical_not(is_mine))
        def _():
            o_ref[...] = t_ref[...]

    def other(q, pr):
        return jnp.where(q == pr[0], (q + 1) % 4, q)

    if transpose:
        out_spec = pl.BlockSpec((tm, Ns), lambda q, hb, i, pr: (hb * nb + i, q))
        out_shape = _sds((2 * K2, 4 * Ns), BF16)
    else:
        out_spec = pl.BlockSpec((None, tm, Ns), lambda q, hb, i, pr: (q, hb * nb + i, 0))
        out_shape = _sds((4, 2 * K2, Ns), BF16)
    return pl.pallas_call(
        body, name=name, out_shape=out_shape,
        grid_spec=pltpu.PrefetchScalarGridSpec(
            num_scalar_prefetch=1, grid=(4, 2, nb),
            in_specs=[pl.BlockSpec((None, tm, Ns), lambda q, hb, i, pr: (other(q, pr), jnp.where(hb == pr[1], i, 0), 0)),
                      pl.BlockSpec((None, tm, Ns), lambda q, hb, i, pr: (other(q, pr), jnp.where(hb == pr[1], 0, i), 0)),
                      pl.BlockSpec((tm, Ns), lambda q, hb, i, pr: (jnp.where(q == pr[0], hb * nb + i, 0), 0))],
            out_specs=out_spec),
        compiler_params=_params(("arbitrary",) * 3),
    )(place, mine, theirs, own)


def _split_lanes(row, widths):
    out, off = [], 0
    for wd in widths:
        out.append(row[:, off:off + wd])
        off += wd
    return out


def _adamw(w, g, m, v, *, name):
    C = w.shape[1]

    def fn(w, g, m, v):
        m = ADAM_B1 * m + (1.0 - ADAM_B1) * g
        v = ADAM_B2 * v + (1.0 - ADAM_B2) * (g * g)
        m_hat = m / (1.0 - ADAM_B1 ** ADAM_STEP)
        v_hat = v / (1.0 - ADAM_B2 ** ADAM_STEP)
        delta = -ADAM_LR * (m_hat / (jnp.sqrt(v_hat) + ADAM_EPS) + ADAM_WD * w)
        return (delta, m, v), ()

    tm = max(8, min(512, (1 << 20) // (4 * C) // 8 * 8))
    (d, nm, nv), _ = _rowwise(fn, [w, g, m, v], [], [(C, F32)] * 3, name=name, tm=tm)
    return d, nm, nv


def _rope_tables(n):
    rows = n // GRID_W
    row = jnp.repeat(jnp.arange(rows, dtype=F32), GRID_W)
    col = jnp.tile(jnp.arange(GRID_W, dtype=F32), rows)
    nf = ROPE_DIM // 4
    freqs = ROPE_THETA ** (-jnp.arange(nf, dtype=F32) / nf)
    ang_r, ang_c = row[:, None] * freqs[None, :], col[:, None] * freqs[None, :]
    cr, sr, cc, sc = jnp.cos(ang_r), jnp.sin(ang_r), jnp.cos(ang_c), jnp.sin(ang_c)
    nope = HEAD_PAD - 2 * ROPE_DIM
    one, zero, z = jnp.ones((n, nope), F32), jnp.zeros((n, nope), F32), jnp.zeros((n, nf), F32)
    pad = jnp.zeros((n, ROPE_DIM), F32)
    cos = jnp.concatenate([one, cr, cr, cc, cc, pad], axis=1)
    s_lo = jnp.concatenate([zero, -sr, z, -sc, z, pad], axis=1)
    s_hi = jnp.concatenate([zero, z, sr, z, sc, pad], axis=1)
    return cos, s_lo, s_hi


def _rope(n, cos, s_lo, s_hi):
    q = ROPE_DIM // 4
    return n * cos + pltpu.roll(n, HEAD_PAD - q, 1) * s_lo + pltpu.roll(n, q, 1) * s_hi


def _rope_t(d, cos, s_lo, s_hi):
    q = ROPE_DIM // 4
    return d * cos + pltpu.roll(d * s_lo, q, 1) + pltpu.roll(d * s_hi, HEAD_PAD - q, 1)


def kernel(x, c, ctx, c_ctx, w_mod, b_mod, norm1_g, w_in, q_norm_g, kv_norm_g, w_uq, w_ukv, qk_norm_q, qk_norm_k, sgu_norm_g, sgu_norm_b, w_spatial, b_spatial, w_br_attn, w_br_sgu, w_out, norm2_g, w_ffn_in, w_ffn_out, loss_target, m_c_ctx, m_w_mod, m_b_mod, m_norm1_g, m_w_in, m_q_norm_g, m_kv_norm_g, m_w_uq, m_w_ukv, m_qk_norm_q, m_qk_norm_k, m_sgu_norm_g, m_sgu_norm_b, m_w_spatial, m_b_spatial, m_w_br_attn, m_w_br_sgu, m_w_out, m_norm2_g, m_w_ffn_in, m_w_ffn_out, v_c_ctx, v_w_mod, v_b_mod, v_norm1_g, v_w_in, v_q_norm_g, v_kv_norm_g, v_w_uq, v_w_ukv, v_qk_norm_q, v_qk_norm_k, v_sgu_norm_g, v_sgu_norm_b, v_w_spatial, v_b_spatial, v_w_br_attn, v_w_br_sgu, v_w_out, v_norm2_g, v_w_ffn_in, v_w_ffn_out):
    ax, ay, ac = _place()
    my_chip = 2 * ax + ay
    my_dev = 4 * ax + 2 * ay + ac

    N, D = x.shape[1], x.shape[2]
    CT = ctx.shape[1]
    M = N + CT
    QL, KVL, QK = q_norm_g.shape[-1], kv_norm_g.shape[-1], qk_norm_q.shape[-1]
    NOPE = QK - ROPE_DIM
    VD = NOPE
    H = 4 * w_uq.shape[-1] // QK
    SW, G, CH = sgu_norm_g.shape[-1], w_spatial.shape[1], w_spatial.shape[2]
    GD = SW // G
    DFF = 4 * w_ffn_out.shape[1]
    NMOD = 4 * w_mod.shape[-1]
    NM = w_mod.shape[-1]
    KVP = KVL + 2 * ROPE_DIM
    assert NOPE == LANES and GD == LANES and HEAD_PAD == NOPE + 2 * ROPE_DIM and CH == LANES
    scale = QK ** -0.5

    x2, ctx2, tgt2 = x[0], ctx[0], loss_target[0]

    c_all = _all_gather8([c], name="ag_c", in_vmem=True)[0][:, 0, :]
    c_rows = jnp.concatenate([c_all, c_ctx[None, :], jnp.zeros((BF16_SUBLANES - 9, D), F32)], axis=0)

    def silu_fn(t):
        s = _sigmoid(t)
        return (t * s, s * (1.0 + t * (1.0 - s))), ()

    (silu_c, dsilu_c), _ = _rowwise(silu_fn, [c_rows], [], [(D, F32), (D, F32)], name="silu_c", tm=16)
    wm = w_mod[0]
    mod_loc = _mm([(silu_c, wm)], name="mod_fwd", outs=(F32,), tn=512, tk=512,
                  extras=[(lax.dynamic_slice_in_dim(b_mod, my_chip * NM, NM, axis=1), "n")],
                  epi=lambda acc, b: (acc + b,))
    mod_all = _all_gather8([mod_loc], name="ag_mod", in_vmem=True)[0]
    mod_full = jnp.concatenate([mod_all[0], mod_all[2], mod_all[4], mod_all[6]], axis=1)
    mod_me = lax.dynamic_slice_in_dim(mod_full, my_dev, 1, axis=0)
    sh1, sc1, g1, sh2, sc2, g2 = [mod_me[:, i * D:(i + 1) * D] for i in range(6)]
    sh1c, sc1c = mod_full[8:9, :D], mod_full[8:9, D:2 * D]

    big = [w_in[0], w_uq[0], w_ukv[0], w_br_attn[0], w_br_sgu[0], w_out[0], w_ffn_in[0], w_ffn_out[0]]
    col_sharded = [True, True, True, True, True, False, True, False]
    halves = [lax.dynamic_slice_in_dim(a, ac * (a.shape[0] // 2), a.shape[0] // 2, axis=0).astype(BF16) for a in big]
    tags = ["w_in", "w_uq", "w_ukv", "w_br_attn", "w_br_sgu", "w_out", "w_ffn_in", "w_ffn_out"]
    first_group, attn_group, ffn_group = [0, 1, 2], [3, 4, 5, 6], [7]
    chip1 = jnp.reshape(my_chip, (1,)).astype(jnp.int32)
    place2 = jnp.stack([my_chip, ac]).astype(jnp.int32)

    def laid_out(seg, i):
        a = big[i]
        if col_sharded[i] and seg.ndim == 3:
            return seg.transpose(1, 0, 2).reshape(a.shape[0], 4 * a.shape[1])
        return seg if col_sharded[i] else seg.reshape(4 * a.shape[0], a.shape[1])

    def side_by_side(i):
        return col_sharded[i] and big[i].shape[1] % LANES == 0

    def finish_gather(idx, mine4, name):
        theirs4 = _pair_forward(mine4, name=name)
        return [laid_out(_assemble_halves(m, t, big[i], place2, name="assemble_" + tags[i], transpose=side_by_side(i)), i)
                for i, m, t in zip(idx, mine4, theirs4)]

    gathered = _all_gather8([halves[i] for i in first_group], name="ag_weights", in_vmem=False, others_only=True)
    w_in_f, w_uq_f, w_ukv_f = [
        laid_out(_assemble(seg.reshape((4,) + big[i].shape), big[i], chip1, name="assemble_" + tags[i],
                           transpose=side_by_side(i)), i) for i, seg in zip(first_group, gathered)]
    o_kv, o_u = QL, QL + KVL + ROPE_DIM
    o_v, o_g = o_u + SW, o_u + 2 * SW
    w_q = w_in_f[:, :QL]
    w_kv = jnp.pad(w_in_f[:, o_kv:o_u], ((0, 0), (0, ROPE_DIM)))
    w_u, w_v = w_in_f[:, o_u:o_v], w_in_f[:, o_v:o_g]
    w_g1, w_g2 = w_in_f[:, o_g:o_g + D], w_in_f[:, o_g + D:]
    w_uq_p = jnp.pad(w_uq_f.reshape(QL, H, QK), ((0, 0), (0, 0), (0, HEAD_PAD - QK))).reshape(QL, H * HEAD_PAD)

    cos_t, slo_t, shi_t = _rope_tables(N)
    ones_c = jnp.concatenate([jnp.ones((CT, NOPE + ROPE_DIM), F32), jnp.zeros((CT, ROPE_DIM), F32)], axis=1)
    cos_k = jnp.concatenate([cos_t, ones_c], axis=0)
    slo_k = jnp.concatenate([slo_t, jnp.zeros((CT, HEAD_PAD), F32)], axis=0)
    shi_k = jnp.concatenate([shi_t, jnp.zeros((CT, HEAD_PAD), F32)], axis=0)
    gq_p = jnp.pad(qk_norm_q, ((0, 0), (0, HEAD_PAD - QK)))
    gk_p = jnp.pad(qk_norm_k, ((0, 0), (0, HEAD_PAD - QK)))

    def norm_mod_fn(t, g, sh, sc):
        r = _rms_stats(t, D)
        return (((t * r) * g) * (1.0 + sc) + sh,), ()

    (h,), _ = _rowwise(norm_mod_fn, [x2], [norm1_g, sh1, sc1], [(D, BF16)], name="norm1_x")
    (ctx_h,), _ = _rowwise(norm_mod_fn, [ctx2], [norm1_g, sh1c, sc1c], [(D, BF16)], name="norm1_ctx")

    qc = _mm([(h, w_q)], name="proj_q", outs=(F32,))
    kvin = jnp.concatenate([_mm([(h, w_kv)], name="proj_kv", outs=(F32,)),
                            _mm([(ctx_h, w_kv)], name="proj_kv_ctx", outs=(F32,))], axis=0)
    u_in = _mm([(h, w_u)], name="proj_u", outs=(BF16,))
    v_in = _mm([(h, w_v)], name="proj_v", outs=(BF16,))
    g1_in = _mm([(h, w_g1)], name="proj_g1", outs=(BF16,))
    g2_in = _mm([(h, w_g2)], name="proj_g2", outs=(BF16,))

    def rms_gain_fn(width):
        def fn(t, g):
            return (((t * _rms_stats(t, width)) * g),), ()
        return fn

    (qn,), _ = _rowwise(rms_gain_fn(QL), [qc], [q_norm_g], [(QL, BF16)], name="q_norm")

    def kv_norm_fn(t, g):
        kvc = t[:, :KVL]
        return (((kvc * _rms_stats(kvc, KVL)) * g),), ()

    (kvn,), _ = _rowwise(kv_norm_fn, [kvin], [kv_norm_g], [(KVL, BF16)], name="kv_norm")
    q_raw = _mm([(qn, w_uq_p)], name="q_up", outs=(F32,))
    kv_raw = _mm([(kvn, w_ukv_f)], name="kv_up", outs=(F32,))

    def q_post_fn(t, cos, slo, shi, g):
        outs = []
        for hd in range(H):
            th = t[:, hd * HEAD_PAD:(hd + 1) * HEAD_PAD]
            outs.append(_rope((th * _rms_stats(th, QK)) * g, cos, slo, shi))
        return (jnp.concatenate(outs, axis=1),), ()

    (q_att,), _ = _rowwise(q_post_fn, [q_raw, cos_t, slo_t, shi_t], [gq_p], [(H * HEAD_PAD, BF16)], name="q_post")

    def k_post_fn(t, kvi, cos, slo, shi, g):
        kr = kvi[:, KVL:]
        ks, vs = [], []
        for hd in range(H):
            th = jnp.concatenate([t[:, hd * HEAD_PAD:hd * HEAD_PAD + NOPE], kr], axis=1)
            ks.append(_rope((th * _rms_stats(th, QK)) * g, cos, slo, shi))
            vs.append(t[:, hd * HEAD_PAD + NOPE:(hd + 1) * HEAD_PAD])
        return (jnp.concatenate(ks, axis=1), jnp.concatenate(vs, axis=1)), ()

    (k_att, v_att), _ = _rowwise(k_post_fn, [kv_raw, kvin, cos_k, slo_k, shi_k], [gk_p],
                                 [(H * HEAD_PAD, BF16), (H * VD, BF16)], name="k_post")
    attn_o, lse, mine4 = _attn_fwd(q_att, k_att, v_att, heads=H, scale=scale,
                                   carry=_ChipExchange([halves[i] for i in attn_group], gather=True))
    w_bra, w_brs, w_out_f, w_ffi = finish_gather(attn_group, mine4, "ag_forward_attn")
    w_fa, w_fb = w_ffi[:, :DFF], w_ffi[:, DFF:]

    ws3 = w_spatial[0]
    bs_t = jnp.pad(b_spatial[0].T, ((0, 0), (0, LANES - G)))

    def sgu_parts(u_in, v_in, ng, nb):
        u, v = _gelu(u_in.astype(F32)), _gelu(v_in.astype(F32))
        mu = jnp.mean(v, axis=-1, keepdims=True)
        vc = v - mu
        rs = lax.rsqrt(jnp.mean(vc * vc, axis=-1, keepdims=True) + EPS)
        xhat = vc * rs
        return u, xhat, rs, (xhat * ng + nb).astype(BF16)

    def sgu_fwd_fn(u_in, v_in, ng, nb, ws, bst):
        u, _, _, vnb = sgu_parts(u_in, v_in, ng, nb)
        outs = []
        for g in range(G):
            sl = slice(g * GD, (g + 1) * GD)
            mixed = jnp.dot(ws[g].astype(BF16), vnb[:, sl], preferred_element_type=F32) + bst[:, g:g + 1]
            outs.append(u[:, sl] * mixed)
        return (jnp.concatenate(outs, axis=1),), ()

    (sgu_o,), _ = _rowwise(sgu_fwd_fn, [u_in, v_in], [sgu_norm_g, sgu_norm_b, ws3, bs_t], [(SW, BF16)],
                           name="sgu_fwd", tm=CH)

    a1 = _mm([(attn_o, w_bra)], name="br_attn", outs=(BF16,))
    a2 = _mm([(sgu_o, w_brs)], name="br_sgu", outs=(BF16,))

    def merge_fn(a1, a2, gi1, gi2):
        return ((_sigmoid(gi1.astype(F32)) * a1.astype(F32) + _sigmoid(gi2.astype(F32)) * a2.astype(F32)),), ()

    (merged,), _ = _rowwise(merge_fn, [a1, a2, g1_in, g2_in], [], [(D, BF16)], name="merge", tc=1024)

    def res_gate(acc, res, gate):
        return res + gate * acc, acc

    x1, mo = _mm([(merged, w_out_f)], name="out_proj", outs=(F32, BF16), tn=512,
                 extras=[(x2, "mn"), (g1, "n")], epi=res_gate)
    (h2,), _ = _rowwise(norm_mod_fn, [x1], [norm2_g, sh2, sc2], [(D, BF16)], name="norm2")

    def swiglu_epi(a, b):
        return a, b, (a * _sigmoid(a)) * b

    (fa, fb, act), mine4 = _mm([(h2, w_fa, w_fb)], name="ffn_in", outs=(BF16, BF16, BF16), tn=512, epi=swiglu_epi,
                               carry=_ChipExchange([halves[i] for i in ffn_group], gather=True))
    (w_ffo,) = finish_gather(ffn_group, mine4, "ag_forward_ffn")
    y, f_out = _mm([(act, w_ffo)], name="ffn_out", outs=(F32, BF16), tn=512,
                   extras=[(x1, "mn"), (g2, "n")], epi=res_gate)

    def loss_fn(y, t, fo, g2v):
        e = y - t
        dy = e * (1.0 / D)
        return (dy, g2v * dy), (_colsum(e * e) * (0.5 / D), _colsum(dy * fo.astype(F32)))

    (dy, df), (loss_cols, dg2) = _rowwise(loss_fn, [y, tgt2, f_out], [g2], [(D, F32), (D, BF16)],
                                          [(1, D), (1, D)], name="loss")

    def swiglu_bwd_epi(dact, a, b):
        a, b = a.astype(F32), b.astype(F32)
        s = _sigmoid(a)
        return dact * b * (s * (1.0 + a * (1.0 - s))), dact * (a * s)

    da, db = _mm([(df, w_ffo)], tb=True, name="ffn_out_dx", outs=(BF16, BF16), tn=512,
                 extras=[(fa, "mn"), (fb, "mn")], epi=swiglu_bwd_epi)
    dw_ffo = _mm([(act, df)], ta=True, name="ffn_out_dw", outs=(BF16,))
    dh2 = _mm([(da, w_fa), (db, w_fb)], tb=True, name="ffn_in_dx", outs=(F32,))
    ns_ffi = w_ffn_in.shape[-1]
    dw_ffi = _mm([(h2, da)], ta=True, name="ffn_in_dw_a", outs=(BF16,), tn=1408, split=ns_ffi,
                 into=(lax.empty((4, D, ns_ffi), BF16), 0))
    dw_ffi = _mm([(h2, db)], ta=True, name="ffn_in_dw_b", outs=(BF16,), tn=1408, split=ns_ffi, into=(dw_ffi, 2))

    def norm2_bwd_fn(dh, t, dyv, mov, g, sc, g1v):
        r = _rms_stats(t, D)
        tn = t * r
        dxg = dh * (1.0 + sc)
        dt = dyv + _rms_bwd(dxg * g, tn, r, D)
        return (dt, g1v * dt), (_colsum(dh), _colsum(dh * (tn * g)), _colsum(dxg * tn), _colsum(dt * mov.astype(F32)))

    (dx1, dmo), (dsh2, dsc2, dn2g, dg1) = _rowwise(
        norm2_bwd_fn, [dh2, x1, dy, mo], [norm2_g, sc2, g1], [(D, F32), (D, BF16)], [(1, D)] * 4, name="norm2_bwd")

    def merge_bwd_epi(dm, a1, a2, gi1, gi2):
        s1, s2 = _sigmoid(gi1.astype(F32)), _sigmoid(gi2.astype(F32))
        a1, a2 = a1.astype(F32), a2.astype(F32)
        return dm * s1, dm * s2, dm * a1 * (s1 * (1.0 - s1)), dm * a2 * (s2 * (1.0 - s2))

    da1, da2, dgi1, dgi2 = _mm([(dmo, w_out_f)], tb=True, name="out_proj_dx", outs=(BF16,) * 4, tn=512,
                               extras=[(a1, "mn"), (a2, "mn"), (g1_in, "mn"), (g2_in, "mn")], epi=merge_bwd_epi)
    dw_out = _mm([(merged, dmo)], ta=True, name="out_proj_dw", outs=(BF16,))
    dattn = _mm([(da1, w_bra)], tb=True, name="br_attn_dx", outs=(BF16,))
    dw_bra = _mm([(attn_o, da1)], ta=True, name="br_attn_dw", outs=(BF16,), split=w_br_attn.shape[-1])
    dsgu = _mm([(da2, w_brs)], tb=True, name="br_sgu_dx", outs=(BF16,))
    dw_brs = _mm([(sgu_o, da2)], ta=True, name="br_sgu_dw", outs=(BF16,), split=w_br_sgu.shape[-1])

    def sgu_bwd_fn(dso, u_in, v_in, ng, nb, ws, bst):
        u, xhat, rs, vnb = sgu_parts(u_in, v_in, ng, nb)
        dso = dso.astype(F32)
        lane = lax.broadcasted_iota(jnp.int32, (CH, LANES), 1)
        du, dvn, dws, dbs = [], [], [], jnp.zeros((CH, LANES), F32)
        for g in range(G):
            sl = slice(g * GD, (g + 1) * GD)
            wg = ws[g].astype(BF16)
            mixed = jnp.dot(wg, vnb[:, sl], preferred_element_type=F32) + bst[:, g:g + 1]
            du.append(dso[:, sl] * mixed)
            dmix = dso[:, sl] * u[:, sl]
            dmb = dmix.astype(BF16)
            dws.append(lax.dot_general(dmb, vnb[:, sl], (((1,), (1,)), ((), ())), preferred_element_type=F32))
            dbs = dbs + jnp.where(lane == g, jnp.sum(dmix, axis=1, keepdims=True), 0.0)
            dvn.append(lax.dot_general(wg, dmb, (((0,), (0,)), ((), ())), preferred_element_type=F32))
        du, dvn = jnp.concatenate(du, axis=1), jnp.concatenate(dvn, axis=1)
        dxh = dvn * ng
        dv = rs * (dxh - jnp.mean(dxh, axis=-1, keepdims=True) - xhat * jnp.mean(dxh * xhat, axis=-1, keepdims=True))
        return ((du * _gelu_grad(u_in.astype(F32)), dv * _gelu_grad(v_in.astype(F32))),
                (_colsum(dvn * xhat), _colsum(dvn), jnp.stack(dws), dbs))

    (du_in, dv_in), (d_sng, d_snb, d_ws, d_bs) = _rowwise(
        sgu_bwd_fn, [dsgu, u_in, v_in], [sgu_norm_g, sgu_norm_b, ws3, bs_t], [(SW, BF16), (SW, BF16)],
        [(1, SW), (1, SW), (G, CH, CH), (CH, LANES)], name="sgu_bwd", tm=CH)

    core = jnp.reshape(ac, (1,)).astype(jnp.int32)

    def dest_layout(dwf, i):
        K, Ns = big[i].shape
        if dwf.ndim == 2:
            dwf = dwf.reshape(K, 4, Ns).transpose(1, 0, 2) if col_sharded[i] else dwf.reshape(4, K, Ns)
        return dwf.reshape(4, 2, K // 2, Ns)

    def pair_reduce(idx, dws, name):
        g4 = [dest_layout(d, i) for d, i in zip(dws, idx)]
        sib = _pair_swap_other(g4, name=name)
        return [_pair_add(g, s, core, name="rs_pair_add_" + tags[i]) for g, s, i in zip(g4, sib, idx)]

    early = [3, 4, 5, 6, 7]
    pair_early = pair_reduce(early, [dw_bra, dw_brs, dw_out, dw_ffi, dw_ffo], "rs_pair_early")
    dq_att, dk_att, dv_att, xchg_early = _attn_bwd(q_att, k_att, v_att, attn_o, lse, dattn, heads=H, scale=scale,
                                                   carry=_ChipExchange(pair_early, gather=False))

    def q_post_bwd_fn(dq, t, cos, slo, shi, g):
        outs, dg = [], jnp.zeros((1, HEAD_PAD), F32)
        for hd in range(H):
            sl = slice(hd * HEAD_PAD, (hd + 1) * HEAD_PAD)
            th = t[:, sl]
            r = _rms_stats(th, QK)
            tn = th * r
            dn = _rope_t(dq[:, sl], cos, slo, shi)
            dg = dg + _colsum(dn * tn)
            outs.append(_rms_bwd(dn * g, tn, r, QK))
        return (jnp.concatenate(outs, axis=1),), (dg,)

    (dq_raw,), (d_gq,) = _rowwise(q_post_bwd_fn, [dq_att, q_raw, cos_t, slo_t, shi_t], [gq_p],
                                  [(H * HEAD_PAD, BF16)], [(1, HEAD_PAD)], name="q_post_bwd")

    def k_post_bwd_fn(dk, dv, t, kvi, cos, slo, shi, g):
        kr = kvi[:, KVL:]
        outs, dg, dkr = [], jnp.zeros((1, HEAD_PAD), F32), jnp.zeros_like(kr)
        for hd in range(H):
            th = jnp.concatenate([t[:, hd * HEAD_PAD:hd * HEAD_PAD + NOPE], kr], axis=1)
            r = _rms_stats(th, QK)
            tn = th * r
            dn = _rope_t(dk[:, hd * HEAD_PAD:(hd + 1) * HEAD_PAD], cos, slo, shi)
            dg = dg + _colsum(dn * tn)
            dt = _rms_bwd(dn * g, tn, r, QK)
            dkr = dkr + dt[:, NOPE:]
            outs += [dt[:, :NOPE], dv[:, hd * VD:(hd + 1) * VD]]
        return (jnp.concatenate(outs, axis=1), dkr), (dg,)

    (dkv_raw, dkrope), (d_gk,) = _rowwise(
        k_post_bwd_fn, [dk_att, dv_att, kv_raw, kvin, cos_k, slo_k, shi_k], [gk_p],
        [(H * HEAD_PAD, BF16), (2 * ROPE_DIM, F32)], [(1, HEAD_PAD)], name="k_post_bwd")

    dqn = _mm([(dq_raw, w_uq_p)], tb=True, name="q_up_dx", outs=(F32,))
    dw_uq_p = _mm([(qn, dq_raw)], ta=True, name="q_up_dw", outs=(BF16,))
    dkvn = _mm([(dkv_raw, w_ukv_f)], tb=True, name="kv_up_dx", outs=(F32,))
    dw_ukv = _mm([(kvn, dkv_raw)], ta=True, name="kv_up_dw", outs=(BF16,), split=w_ukv.shape[-1])

    def q_norm_bwd_fn(dn, t, g):
        r = _rms_stats(t, QL)
        tn = t * r
        return (_rms_bwd(dn * g, tn, r, QL),), (_colsum(dn * tn),)

    (dqc,), (d_qng,) = _rowwise(q_norm_bwd_fn, [dqn, qc], [q_norm_g], [(QL, BF16)], [(1, QL)], name="q_norm_bwd")

    def kv_norm_bwd_fn(dn, dkr, t, g):
        kvc = t[:, :KVL]
        r = _rms_stats(kvc, KVL)
        tn = kvc * r
        return (jnp.concatenate([_rms_bwd(dn * g, tn, r, KVL), dkr], axis=1),), (_colsum(dn * tn),)

    (dkvin,), (d_kvng,) = _rowwise(kv_norm_bwd_fn, [dkvn, dkrope, kvin], [kv_norm_g], [(KVP, BF16)], [(1, KVL)],
                                   name="kv_norm_bwd")
    dkvin_x, dkvin_c = dkvin[:N], dkvin[N:]

    dh = _mm([(dqc, w_q), (dkvin_x, w_kv), (du_in, w_u), (dv_in, w_v), (dgi1, w_g1), (dgi2, w_g2)], tb=True,
             name="proj_dx", outs=(F32,), tn=512, tk=1024)
    dctx_h = _mm([(dkvin_c, w_kv)], tb=True, name="proj_kv_ctx_dx", outs=(F32,))
    dw_q = _mm([(h, dqc)], ta=True, name="proj_q_dw", outs=(BF16,))
    dw_kv = _mm([(h, dkvin_x), (ctx_h, dkvin_c)], ta=True, name="proj_kv_dw", outs=(BF16,))
    dw_u = _mm([(h, du_in)], ta=True, name="proj_u_dw", outs=(BF16,))
    dw_v = _mm([(h, dv_in)], ta=True, name="proj_v_dw", outs=(BF16,))
    dw_g1 = _mm([(h, dgi1)], ta=True, name="proj_g1_dw", outs=(BF16,))
    dw_g2 = _mm([(h, dgi2)], ta=True, name="proj_g2_dw", outs=(BF16,))

    def norm1_bwd_fn(dhv, t, dres, g, sc):
        r = _rms_stats(t, D)
        tn = t * r
        dxg = dhv * (1.0 + sc)
        return (dres + _rms_bwd(dxg * g, tn, r, D),), (_colsum(dhv), _colsum(dhv * (tn * g)), _colsum(dxg * tn))

    (grad_x,), (dsh1, dsc1, dn1g_x) = _rowwise(norm1_bwd_fn, [dh, x2, dx1], [norm1_g, sc1], [(D, F32)], [(1, D)] * 3,
                                               name="norm1_bwd")
    _, (dsh1c, dsc1c, dn1g_c) = _rowwise(norm1_bwd_fn, [dctx_h, ctx2, jnp.zeros_like(ctx2)], [norm1_g, sc1c],
                                         [(D, F32)], [(1, D)] * 3, name="norm1_ctx_bwd")

    small = [dsh1, dsc1, dg1, dsh2, dsc2, dg2,
             dsh1c, dsc1c, dn1g_x, dn1g_c, d_qng, d_kvng, d_gq, d_gk, d_sng, d_snb, dn2g, loss_cols]
    small_sizes = [a.shape[1] for a in small]
    sm_row = jnp.concatenate(small, axis=1)
    sm_mat = jnp.concatenate([d_ws.reshape(G * CH, CH), d_bs], axis=0)
    row_all, mat_all = _all_gather8([sm_row, sm_mat], name="ag_small", in_vmem=True)
    row_sum = _sum_blocks(row_all, name="sum_small_rows", out_dtype=F32)
    mat_sum = _sum_blocks(mat_all, name="sum_small_mats", out_dtype=F32)
    dmod_rows = row_all[:, 0, :NMOD]
    (_, _, _, _, _, _, t_sh1c, t_sc1c, t_n1x, t_n1c, g_qng, g_kvng, t_gq, t_gk, g_sng, g_snb, g_n2g,
     t_loss) = _split_lanes(row_sum, small_sizes)
    g_ws, t_bs = mat_sum[:G * CH], mat_sum[G * CH:]
    dmodc_row = jnp.concatenate([t_sh1c, t_sc1c, jnp.zeros((1, NMOD - 2 * D), F32)], axis=1)
    dmod16 = jnp.concatenate([dmod_rows, dmodc_row, jnp.zeros((BF16_SUBLANES - 9, NMOD), F32)], axis=0)

    def small_fn(rows, n1x, n1c, lossv):
        return (), (_colsum(rows), n1x + n1c, jnp.sum(lossv, axis=1, keepdims=True))

    _, (g_bmod, g_n1g, loss11) = _rowwise(small_fn, [dmod16], [t_n1x, t_n1c, t_loss], [], [(1, NMOD), (1, D), (1, 1)],
                                          name="small_reduce", tm=16)
    dmod_loc = lax.dynamic_slice_in_dim(dmod16, my_chip * NM, NM, axis=1)
    g_wmod = _mm([(silu_c, dmod_loc)], ta=True, name="mod_dw", outs=(F32,), tn=512)
    dsilu_part = _mm([(dmod_loc, wm)], tb=True, name="mod_dx", outs=(F32,), tk=512)
    part_all = _all_gather8([dsilu_part[8:9]], name="ag_cctx", in_vmem=True)[0]

    def cctx_fn(parts, dsl):
        return (), ((parts[0:1] + parts[2:3] + parts[4:5] + parts[6:7]) * dsl,)

    _, (g_cctx,) = _rowwise(cctx_fn, [part_all[:, 0, :]], [dsilu_c[8:9]], [], [(1, D)], name="cctx_grad", tm=8)

    dw_in_f = jnp.concatenate([dw_q, dw_kv[:, :KVL + ROPE_DIM], dw_u, dw_v, dw_g1, dw_g2], axis=1)
    dw_uq_f = dw_uq_p.reshape(QL, H, HEAD_PAD)[:, :, :QK].reshape(QL, H * QK)
    late = [0, 1, 2]
    pair_late = pair_reduce(late, [dw_in_f, dw_uq_f, dw_ukv], "rs_pair_late")
    pair = pair_late + pair_early
    xchg = [lax.dynamic_update_index_in_dim(t4, lax.dynamic_index_in_dim(pr, my_chip, 0, keepdims=False), my_chip, 0)
            for t4, pr in zip(list(_chip_exchange(pair_late, name="rs_chips_late")) + xchg_early, pair)]
    red_half = [_sum_blocks(t4, name="rs_sum_" + t, out_dtype=F32) for t4, t in zip(xchg, tags)]
    big_grads = [lax.dynamic_update_index_in_dim(r, mine, ac, 0).reshape(a.shape)
                 for r, mine, a in zip(_pair_gather(red_half, name="rs_halves"), red_half, big)]

    def upd(w, g, m, v, nm):
        shape = w.shape
        w2, g2_, m2, v2 = [t.reshape(-1, shape[-1]) for t in (w, g, m, v)]
        d_, m_, v_ = _adamw(w2, g2_, m2, v2, name="adamw_" + nm)
        return g.reshape(shape), d_.reshape(shape), m_.reshape(shape), v_.reshape(shape)

    g_in, g_uq, g_ukv, g_bra, g_brs, g_out, g_ffi, g_ffo = big_grads
    grads = dict(
        c_ctx=g_cctx.reshape(D), w_mod=g_wmod[None], b_mod=g_bmod, norm1_g=g_n1g, w_in=g_in[None],
        q_norm_g=g_qng, kv_norm_g=g_kvng, w_uq=g_uq[None], w_ukv=g_ukv[None],
        qk_norm_q=t_gq[:, :QK], qk_norm_k=t_gk[:, :QK], sgu_norm_g=g_sng, sgu_norm_b=g_snb,
        w_spatial=g_ws.reshape(w_spatial.shape), b_spatial=t_bs[:, :G].T[None],
        w_br_attn=g_bra[None], w_br_sgu=g_brs[None], w_out=g_out[None], norm2_g=g_n2g,
        w_ffn_in=g_ffi[None], w_ffn_out=g_ffo[None])
    weights = dict(c_ctx=c_ctx, w_mod=w_mod, b_mod=b_mod, norm1_g=norm1_g, w_in=w_in, q_norm_g=q_norm_g,
                   kv_norm_g=kv_norm_g, w_uq=w_uq, w_ukv=w_ukv, qk_norm_q=qk_norm_q, qk_norm_k=qk_norm_k,
                   sgu_norm_g=sgu_norm_g, sgu_norm_b=sgu_norm_b, w_spatial=w_spatial, b_spatial=b_spatial,
                   w_br_attn=w_br_attn, w_br_sgu=w_br_sgu, w_out=w_out, norm2_g=norm2_g, w_ffn_in=w_ffn_in,
                   w_ffn_out=w_ffn_out)
    m_in = dict(c_ctx=m_c_ctx, w_mod=m_w_mod, b_mod=m_b_mod, norm1_g=m_norm1_g, w_in=m_w_in, q_norm_g=m_q_norm_g,
                kv_norm_g=m_kv_norm_g, w_uq=m_w_uq, w_ukv=m_w_ukv, qk_norm_q=m_qk_norm_q, qk_norm_k=m_qk_norm_k,
                sgu_norm_g=m_sgu_norm_g, sgu_norm_b=m_sgu_norm_b, w_spatial=m_w_spatial, b_spatial=m_b_spatial,
                w_br_attn=m_w_br_attn, w_br_sgu=m_w_br_sgu, w_out=m_w_out, norm2_g=m_norm2_g, w_ffn_in=m_w_ffn_in,
                w_ffn_out=m_w_ffn_out)
    v_in_ = dict(c_ctx=v_c_ctx, w_mod=v_w_mod, b_mod=v_b_mod, norm1_g=v_norm1_g, w_in=v_w_in, q_norm_g=v_q_norm_g,
                 kv_norm_g=v_kv_norm_g, w_uq=v_w_uq, w_ukv=v_w_ukv, qk_norm_q=v_qk_norm_q, qk_norm_k=v_qk_norm_k,
                 sgu_norm_g=v_sgu_norm_g, sgu_norm_b=v_sgu_norm_b, w_spatial=v_w_spatial, b_spatial=v_b_spatial,
                 w_br_attn=v_w_br_attn, w_br_sgu=v_w_br_sgu, w_out=v_w_out, norm2_g=v_norm2_g, w_ffn_in=v_w_ffn_in,
                 w_ffn_out=v_w_ffn_out)
    names = list(weights)
    big_names = ("w_mod", "w_in", "w_uq", "w_ukv", "w_br_attn", "w_br_sgu", "w_out", "w_ffn_in", "w_ffn_out")
    out_g, out_d, out_m, out_v = {}, {}, {}, {}
    for nm in big_names:
        out_g[nm], out_d[nm], out_m[nm], out_v[nm] = upd(weights[nm], grads[nm], m_in[nm], v_in_[nm], nm)
    row_names = [nm for nm in names if nm not in big_names and nm not in ("w_spatial", "b_spatial")]
    widths = [-(-weights[nm].size // LANES) * LANES for nm in row_names]

    def as_row(d):
        return jnp.concatenate([jnp.pad(d[nm].reshape(1, -1), ((0, 0), (0, wd - d[nm].size)))
                                for nm, wd in zip(row_names, widths)], axis=1)

    def as_mat(d):
        return jnp.concatenate([d["w_spatial"].reshape(G * CH, CH), d["b_spatial"].reshape(G, CH)], axis=0)

    row_res = _adamw(as_row(weights), as_row(grads), as_row(m_in), as_row(v_in_), name="adamw_rows")
    mat_res = _adamw(as_mat(weights), as_mat(grads), as_mat(m_in), as_mat(v_in_), name="adamw_spatial")
    for tgt, row, mat in zip((out_d, out_m, out_v), row_res, mat_res):
        for nm, seg in zip(row_names, _split_lanes(row, widths)):
            tgt[nm] = seg[:, :weights[nm].size].reshape(weights[nm].shape)
        tgt["w_spatial"] = mat[:G * CH].reshape(w_spatial.shape)
        tgt["b_spatial"] = mat[G * CH:].reshape(b_spatial.shape)
    for nm in row_names + ["w_spatial", "b_spatial"]:
        out_g[nm] = grads[nm].reshape(weights[nm].shape)

    loss = loss11.reshape(())
    return (loss, grad_x[None], *[out_g[n] for n in names], *[out_d[n] for n in names],
            *[out_m[n] for n in names], *[out_v[n] for n in names])
```

```python
import math

import jax
import jax.numpy as jnp
from jax import lax
from jax.experimental import pallas as pl
from jax.experimental.pallas import tpu as pltpu

F32, BF16 = jnp.float32, jnp.bfloat16
MESH = pl.DeviceIdType.MESH

LANES = 128
BF16_SUBLANES = 16
VMEM_LIMIT_BYTES = 56 * 1024 * 1024

EPS = 1e-6
ROPE_DIM = 64
ROPE_THETA = 10000.0
GRID_W = 64
HEAD_PAD = 256
ADAM_LR, ADAM_B1, ADAM_B2, ADAM_EPS, ADAM_WD, ADAM_STEP = 0.001, 0.9, 0.999, 1e-08, 0.01, 10


def _tile(dim, pref, align=LANES):
    if dim <= pref:
        return dim
    t = (pref // align) * align
    while t >= align:
        if dim % t == 0:
            return t
        t -= align
    return dim


def _params(sem=None):
    return pltpu.CompilerParams(dimension_semantics=sem, vmem_limit_bytes=VMEM_LIMIT_BYTES)


def _sds(shape, dtype):
    return jax.ShapeDtypeStruct(tuple(shape), dtype)


def _mm(pairs, *, name, ta=False, tb=False, outs=(F32,), tm=1024, tn=1024, tk=2048, extras=(), epi=None,
        split=None, into=None, carry=None):
    dual = len(pairs[0]) == 3
    a0, b0 = pairs[0][0], pairs[0][1]
    M = a0.shape[1] if ta else a0.shape[0]
    N = b0.shape[0] if tb else b0.shape[1]
    tm, tn = _tile(M, tm), _tile(N if split is None else split, tn)
    ks = [(p[0].shape[0] if ta else p[0].shape[1]) for p in pairs]
    tks = [_tile(k, tk) for k in ks]
    nks = [k // t for k, t in zip(ks, tks)]
    offs = [sum(nks[:i]) for i in range(len(pairs))]
    nk_total = sum(nks)
    single = len(pairs) == 1

    def kidx(kk, p):
        return kk if single else jnp.clip(kk - offs[p], 0, nks[p] - 1)

    in_specs, operands = [], []
    for p, pr in enumerate(pairs):
        if ta:
            in_specs.append(pl.BlockSpec((tks[p], tm), lambda i, j, kk, p=p: (kidx(kk, p), i)))
        else:
            in_specs.append(pl.BlockSpec((tm, tks[p]), lambda i, j, kk, p=p: (i, kidx(kk, p))))
        operands.append(pr[0])
        for b in pr[1:]:
            if tb:
                in_specs.append(pl.BlockSpec((tn, tks[p]), lambda i, j, kk, p=p: (j, kidx(kk, p))))
            else:
                in_specs.append(pl.BlockSpec((tks[p], tn), lambda i, j, kk, p=p: (kidx(kk, p), j)))
            operands.append(b)
    for arr, kind in extras:
        if kind == "mn":
            in_specs.append(pl.BlockSpec((tm, tn), lambda i, j, kk: (i, j)))
        else:
            in_specs.append(pl.BlockSpec((1, tn), lambda i, j, kk: (0, j)))
        operands.append(arr)
    n_in = len(operands)
    n_ex = len(extras)
    per = 3 if dual else 2
    dims = (((0 if ta else 1,), (1 if tb else 0,)), ((), ()))

    n_acc = 2 if dual else 1

    def products(ins, p):
        a = ins[per * p][...].astype(BF16)
        return [lax.dot_general(a, ins[per * p + 1 + q][...].astype(BF16), dims, preferred_element_type=F32)
                for q in range(n_acc)]

    def finish(ins, out_refs, acc_vals):
        vals = acc_vals + [r[...] for r in ins[n_in - n_ex:]]
        res = epi(*vals) if epi is not None else (vals[0],)
        for o, r in zip(out_refs, res):
            o[...] = r.astype(o.dtype)

    out_specs = [pl.BlockSpec((tm, tn), lambda i, j, kk: (i, j)) for _ in outs]
    out_shape = [_sds((M, N), d) for d in outs]
    aliases = {}
    n_alias = 0
    if split is not None:
        nps = split // tn
        lead = 0 if into is None else into[1]
        out_specs = [pl.BlockSpec((None, tm, tn), lambda i, j, kk: (j // nps + lead, i, j % nps))]
        out_shape = [_sds((N // split if into is None else into[0].shape[0], M, split), outs[0])]
        if into is not None:
            in_specs.append(pl.BlockSpec(memory_space=pl.ANY))
            operands.append(into[0])
            aliases, n_alias = {n_in: 0}, 1

    grid = (M // tm, N // tn, nk_total)

    def at_step(first):
        ids = [pl.program_id(d) for d in range(3)]
        cond = None
        for d, g in zip(ids, grid):
            t = d == (0 if first else g - 1)
            cond = t if cond is None else cond & t
        return cond

    def body(*refs):
        ins, out_refs, accs, start, wait = _split_refs(refs, n_in + n_alias, len(outs), carry)
        ins = ins[:n_in]
        if carry is not None:
            pl.when(at_step(True))(start)
        if nk_total == 1:
            finish(ins, out_refs, products(ins, 0))
        else:
            kk = pl.program_id(2)

            @pl.when(kk == 0)
            def _():
                for acc, v in zip(accs, products(ins, 0)):
                    acc[...] = v

            for p in range(len(pairs)):
                lo = max(offs[p], 1)

                @pl.when((kk >= lo) & (kk < offs[p] + nks[p]))
                def _(p=p):
                    for acc, v in zip(accs, products(ins, p)):
                        acc[...] += v

            @pl.when(kk == nk_total - 1)
            def _():
                finish(ins, out_refs, [acc[...] for acc in accs])
        if carry is not None:
            pl.when(at_step(False))(wait)

    ex = carry
    res = pl.pallas_call(
        body, name=name, grid=grid, in_specs=in_specs + ([] if ex is None else ex.in_specs),
        out_specs=out_specs + ([] if ex is None else ex.out_specs),
        out_shape=out_shape + ([] if ex is None else ex.out_shape), input_output_aliases=aliases,
        scratch_shapes=[pltpu.VMEM((tm, tn), F32) for _ in range(n_acc if nk_total > 1 else 0)]
        + ([] if ex is None else ex.scratch),
        compiler_params=_params(("arbitrary",) * 3 if ex is not None else ("parallel", "parallel", "arbitrary")),
    )(*operands, *([] if ex is None else ex.xs))
    if ex is not None:
        return (res[0] if len(outs) == 1 else res[:len(outs)]), list(res[len(outs):])
    return res[0] if len(outs) == 1 else res


def _rowwise(fn, rows, vecs, out_rows, out_accs=(), *, name, tm=256, tc=None, carry=None):
    M = rows[0].shape[0]
    tm = _tile(M, tm, BF16_SUBLANES)
    nrow = M // tm
    C = rows[0].shape[1]
    ncol = 1 if tc is None else C // _tile(C, tc)
    tcol = None if tc is None else _tile(C, tc)

    def colwise(shape):
        return tc is not None and len(shape) == 2 and shape[0] == 1 and shape[1] == C

    def vspec(shape):
        if colwise(shape):
            return pl.BlockSpec((1, tcol), lambda j, i: (0, j))
        return pl.BlockSpec(tuple(shape), lambda j, i, n=len(shape): (0,) * n)

    def rspec(width):
        if tc is None:
            return pl.BlockSpec((tm, width), lambda j, i: (i, 0))
        return pl.BlockSpec((tm, tcol), lambda j, i: (i, j))

    in_specs = [rspec(r.shape[1]) for r in rows] + [vspec(v.shape) for v in vecs]
    out_specs = [rspec(c) for c, _ in out_rows] + [vspec(s) for s in out_accs]
    out_shape = [_sds((M, c), d) for c, d in out_rows] + [_sds(s, F32) for s in out_accs]
    n_in, n_or = len(rows) + len(vecs), len(out_rows)

    n_out = n_or + len(out_accs)
    ex = carry

    def body(*refs):
        ins, outs, _, start, wait = _split_refs(refs, n_in, n_out, ex)
        o_rows, o_accs = outs[:n_or], outs[n_or:]
        if ex is not None:
            pl.when((pl.program_id(0) == 0) & (pl.program_id(1) == 0))(start)
        r_out, a_out = fn(*[r[...] for r in ins])
        for o, r in zip(o_rows, r_out):
            o[...] = r.astype(o.dtype)
        i = pl.program_id(1)

        @pl.when(i == 0)
        def _():
            for o, a in zip(o_accs, a_out):
                o[...] = a

        @pl.when(i > 0)
        def _():
            for o, a in zip(o_accs, a_out):
                o[...] += a

        if ex is not None:
            pl.when((pl.program_id(0) == ncol - 1) & (pl.program_id(1) == nrow - 1))(wait)

    res = pl.pallas_call(
        body, name=name, grid=(ncol, nrow), in_specs=in_specs + ([] if ex is None else ex.in_specs),
        out_specs=out_specs + ([] if ex is None else ex.out_specs),
        out_shape=out_shape + ([] if ex is None else ex.out_shape),
        scratch_shapes=[] if ex is None else ex.scratch,
        compiler_params=_params(("arbitrary", "arbitrary") if ex is not None else ("parallel", "arbitrary")),
    )(*rows, *vecs, *([] if ex is None else ex.xs))
    if ex is not None:
        return res[:n_or], res[n_or:n_out], list(res[n_out:])
    return res[:n_or], res[n_or:]


def _colsum(t):
    return jnp.sum(t, axis=0, keepdims=True)


def _gelu(t):
    return 0.5 * t * (1.0 + lax.erf(t * math.sqrt(0.5)))


def _gelu_grad(t):
    return 0.5 * (1.0 + lax.erf(t * math.sqrt(0.5))) + t * jnp.exp(-0.5 * t * t) * (1.0 / math.sqrt(2.0 * math.pi))


def _sigmoid(t):
    return 1.0 / (1.0 + jnp.exp(-t))


def _rms_stats(t, width):
    return lax.rsqrt(jnp.sum(t * t, axis=-1, keepdims=True) * (1.0 / width) + EPS)


def _rms_bwd(dn, tn, r, width):
    return r * (dn - tn * (jnp.sum(dn * tn, axis=-1, keepdims=True) * (1.0 / width)))


def _place():
    return lax.axis_index("x"), lax.axis_index("y"), lax.axis_index("c")


class _ChipExchange:
    def __init__(self, xs, gather):
        self.xs, self.gather, self.n = list(xs), gather, len(xs)
        self.in_specs = [pl.BlockSpec(memory_space=pl.ANY)] * self.n
        self.out_specs = [pl.BlockSpec(memory_space=pl.ANY)] * self.n
        self.out_shape = [_sds((4,) + (x.shape if gather else x.shape[1:]), x.dtype) for x in self.xs]
        self.scratch = [pltpu.SemaphoreType.DMA((self.n, 3)), pltpu.SemaphoreType.DMA((self.n, 3))]

    def bind(self, x_refs, out_refs, send_sems, recv_sems):
        x, y, c = _place()
        p = 2 * x + y
        chips = [(1 - x, y), (x, 1 - y), (1 - x, 1 - y)]

        def copy(w, k, outgoing):
            qx, qy = chips[k]
            there = 2 * qx + qy
            if self.gather:
                src = x_refs[w]
            else:
                src = x_refs[w].at[there if outgoing else p]
            return pltpu.make_async_remote_copy(
                src_ref=src, dst_ref=out_refs[w].at[p if outgoing else there], send_sem=send_sems.at[w, k],
                recv_sem=recv_sems.at[w, k], device_id=(qx, qy, c), device_id_type=MESH)

        def start():
            for w in range(self.n):
                for k in range(3):
                    copy(w, k, True).start()

        def wait():
            for w in range(self.n):
                for k in range(3):
                    copy(w, k, False).wait_recv()
            for w in range(self.n):
                for k in range(3):
                    copy(w, k, True).wait_send()

        return start, wait


class _PairExchange:
    def __init__(self, xs, mode):
        self.xs, self.mode, self.n = list(xs), mode, len(xs)
        self.in_specs = [pl.BlockSpec(memory_space=pl.ANY)] * self.n
        self.out_specs = [pl.BlockSpec(memory_space=pl.ANY)] * self.n
        shape = {"halves": lambda s: (4,) + s[2:], "forward": lambda s: s, "gather": lambda s: (2,) + s}[mode]
        self.out_shape = [_sds(shape(x.shape), x.dtype) for x in self.xs]
        self.scratch = [pltpu.SemaphoreType.DMA((self.n, 3)), pltpu.SemaphoreType.DMA((self.n, 3))]

    def bind(self, x_refs, out_refs, send_sems, recv_sems):
        x, y, c = _place()
        chips = [(1 - x, y), (x, 1 - y), (1 - x, 1 - y)]

        def copy(w, src, dst, k):
            return pltpu.make_async_remote_copy(src_ref=src, dst_ref=dst, send_sem=send_sems.at[w, k],
                                                recv_sem=recv_sems.at[w, k], device_id=(x, y, 1 - c),
                                                device_id_type=MESH)

        def start():
            for w, (xr, orf) in enumerate(zip(x_refs, out_refs)):
                if self.mode == "halves":
                    for q in range(4):
                        copy(w, xr.at[q, 1 - c], orf.at[q], 0).start()
                elif self.mode == "forward":
                    for k, (qx, qy) in enumerate(chips):
                        copy(w, xr.at[2 * qx + qy], orf.at[2 * qx + qy], k).start()
                else:
                    copy(w, xr, orf.at[c], 0).start()

        def wait():
            for w, (xr, orf) in enumerate(zip(x_refs, out_refs)):
                if self.mode == "halves":
                    copy(w, orf, orf, 0).wait()
                elif self.mode == "forward":
                    for k, (qx, qy) in enumerate(chips):
                        copy(w, xr.at[2 * qx + qy], orf.at[2 * qx + qy], k).wait()
                else:
                    cp = copy(w, xr, orf.at[1 - c], 0)
                    cp.wait_recv()
                    cp.wait_send()

        return start, wait


def _split_refs(refs, n_in, n_out, ex):
    ne = 0 if ex is None else ex.n
    ins, xin = refs[:n_in], refs[n_in:n_in + ne]
    outs, xout = refs[n_in + ne:n_in + ne + n_out], refs[n_in + ne + n_out:n_in + 2 * ne + n_out]
    rest = refs[n_in + 2 * ne + n_out:]
    if ex is None:
        return ins, outs, rest, None, None
    start, wait = ex.bind(xin, xout, rest[-2], rest[-1])
    return ins, outs, rest[:-2], start, wait


def _attn_fwd(q, k, v, *, heads, tq=256, carry=None):
    N, M = q.shape[0], k.shape[0]
    tq = _tile(N, tq)
    vd = v.shape[1] // heads
    nq = N // tq

    def body(*refs):
        (q_ref, k_ref, v_ref), (o_ref, lse_ref), _, start, wait = _split_refs(refs, 3, 2, carry)
        if carry is not None:
            pl.when((pl.program_id(0) == 0) & (pl.program_id(1) == 0))(start)
        s = lax.dot_general(q_ref[...], k_ref[...], (((1,), (1,)), ((), ())), preferred_element_type=F32)
        m = jnp.max(s, axis=-1, keepdims=True)
        p = jnp.exp(s - m)
        l = jnp.sum(p, axis=-1, keepdims=True)
        o = jnp.dot(p.astype(BF16), v_ref[...], preferred_element_type=F32) / l
        o_ref[...] = o.astype(o_ref.dtype)
        lse_ref[...] = jnp.broadcast_to(m + jnp.log(l), lse_ref.shape)
        if carry is not None:
            pl.when((pl.program_id(0) == heads - 1) & (pl.program_id(1) == nq - 1))(wait)

    ex = carry
    res = pl.pallas_call(
        body, name="attn_fwd", grid=(heads, nq),
        in_specs=[pl.BlockSpec((tq, HEAD_PAD), lambda h, i: (i, h)),
                  pl.BlockSpec((M, HEAD_PAD), lambda h, i: (0, h)),
                  pl.BlockSpec((M, vd), lambda h, i: (0, h))] + ([] if ex is None else ex.in_specs),
        out_specs=[pl.BlockSpec((tq, vd), lambda h, i: (i, h)),
                   pl.BlockSpec((tq, vd), lambda h, i: (i, h))] + ([] if ex is None else ex.out_specs),
        out_shape=[_sds((N, heads * vd), BF16), _sds((N, heads * vd), F32)] + ([] if ex is None else ex.out_shape),
        scratch_shapes=[] if ex is None else ex.scratch,
        compiler_params=_params(("arbitrary", "arbitrary")),
    )(q, k, v, *([] if ex is None else ex.xs))
    return res[0], res[1], list(res[2:])


def _attn_bwd(q, k, v, o, lse, do, *, heads, tq=256, carry=None):
    N, M = q.shape[0], k.shape[0]
    tq = _tile(N, tq)
    vd = v.shape[1] // heads
    nq = N // tq
    nt = (((1,), (1,)), ((), ()))
    tn = (((0,), (0,)), ((), ()))

    def body(*refs):
        (q_ref, k_ref, v_ref, o_ref, lse_ref, do_ref), (dq_ref, dk_ref, dv_ref), _, start, wait = _split_refs(
            refs, 6, 3, carry)
        if carry is not None:
            pl.when((pl.program_id(0) == 0) & (pl.program_id(1) == 0))(start)
        i = pl.program_id(1)
        qb, kb, dob = q_ref[...], k_ref[...], do_ref[...]
        s = lax.dot_general(qb, kb, nt, preferred_element_type=F32)
        p = jnp.exp(s - lse_ref[...][:, :1])
        dp = lax.dot_general(dob, v_ref[...], nt, preferred_element_type=F32)
        delta = jnp.sum(dob.astype(F32) * o_ref[...].astype(F32), axis=-1, keepdims=True)
        ds = (p * (dp - delta)).astype(BF16)
        pb = p.astype(BF16)
        dq_ref[...] = jnp.dot(ds, kb, preferred_element_type=F32)
        dk_part = lax.dot_general(ds, qb, tn, preferred_element_type=F32)
        dv_part = lax.dot_general(pb, dob, tn, preferred_element_type=F32)

        @pl.when(i == 0)
        def _():
            dk_ref[...] = dk_part
            dv_ref[...] = dv_part

        @pl.when(i > 0)
        def _():
            dk_ref[...] += dk_part
            dv_ref[...] += dv_part

        if carry is not None:
            pl.when((pl.program_id(0) == heads - 1) & (pl.program_id(1) == nq - 1))(wait)

    ex = carry
    res = pl.pallas_call(
        body, name="attn_bwd", grid=(heads, nq),
        in_specs=[pl.BlockSpec((tq, HEAD_PAD), lambda h, i: (i, h)),
                  pl.BlockSpec((M, HEAD_PAD), lambda h, i: (0, h)),
                  pl.BlockSpec((M, vd), lambda h, i: (0, h)),
                  pl.BlockSpec((tq, vd), lambda h, i: (i, h)),
                  pl.BlockSpec((tq, vd), lambda h, i: (i, h)),
                  pl.BlockSpec((tq, vd), lambda h, i: (i, h))] + ([] if ex is None else ex.in_specs),
        out_specs=[pl.BlockSpec((tq, HEAD_PAD), lambda h, i: (i, h)),
                   pl.BlockSpec((M, HEAD_PAD), lambda h, i: (0, h)),
                   pl.BlockSpec((M, vd), lambda h, i: (0, h))] + ([] if ex is None else ex.out_specs),
        out_shape=[_sds((N, heads * HEAD_PAD), F32), _sds((M, heads * HEAD_PAD), F32),
                   _sds((M, heads * vd), F32)] + ([] if ex is None else ex.out_shape),
        scratch_shapes=[] if ex is None else ex.scratch,
        compiler_params=_params(("arbitrary", "arbitrary")),
    )(q, k, v, o, lse, do, *([] if ex is None else ex.xs))
    return res[0], res[1], res[2], list(res[3:])


def _comm_call(body, xs, out_shapes, n_sems, name, in_vmem):
    space = pltpu.VMEM if in_vmem else pl.ANY
    n = len(xs)

    def wrapped(*refs):
        body(refs[:n], refs[n:2 * n], *refs[2 * n:])

    return pl.pallas_call(
        wrapped, name=name, out_shape=list(out_shapes),
        in_specs=[pl.BlockSpec(memory_space=space)] * n, out_specs=[pl.BlockSpec(memory_space=space)] * n,
        scratch_shapes=[pltpu.SemaphoreType.DMA((n, n_sems)), pltpu.SemaphoreType.DMA((n, n_sems)),
                        pltpu.SemaphoreType.DMA((n,))],
        compiler_params=pltpu.CompilerParams(vmem_limit_bytes=VMEM_LIMIT_BYTES),
    )(*xs)


def _all_gather8(blks, *, name, in_vmem, others_only=False):
    def body(x_refs, out_refs, send_sems, recv_sems, local_sems):
        x, y, c = _place()
        me, sibling = (x, y, c), (x, y, 1 - c)
        chips = [(1 - x, y), (x, 1 - y), (1 - x, 1 - y)]
        waits = []
        for w, (x_ref, out_ref) in enumerate(zip(x_refs, out_refs)):
            def slot(px, py, pc, out_ref=out_ref):
                return out_ref.at[4 * px + 2 * py + pc]

            def copy(k, block, to, src=None, w=w, slot=slot):
                return pltpu.make_async_remote_copy(
                    src_ref=slot(*block) if src is None else src, dst_ref=slot(*block),
                    send_sem=send_sems.at[w, k], recv_sem=recv_sems.at[w, k], device_id=to, device_id_type=MESH)

            mine = None
            first = []
            if not others_only:
                mine = pltpu.make_async_copy(x_ref, slot(*me), local_sems.at[w])
                mine.start()
                first.append(copy(0, me, sibling, src=x_ref))
            first += [copy(1 + j, me, (*chip, c), src=x_ref) for j, chip in enumerate(chips)]
            for cp in first:
                cp.start()
            waits.append((copy, mine, first))
        for copy, mine, first in waits:
            passed = [copy(4 + j, (*chip, c), sibling) for j, chip in enumerate(chips)]
            for j, chip in enumerate(chips):
                copy(1 + j, (*chip, c), me).wait_recv()
                passed[j].start()
            if not others_only:
                copy(0, sibling, me).wait_recv()
            for j, chip in enumerate(chips):
                copy(4 + j, (*chip, 1 - c), me).wait_recv()
            for cp in first + passed:
                cp.wait_send()
            if mine is not None:
                mine.wait()

    return _comm_call(body, blks, [_sds((8,) + b.shape, b.dtype) for b in blks], 7, name, in_vmem)


def _exchange_alone(ex, *, name):
    def body(x_refs, out_refs, send_sems, recv_sems, local_sems):
        start, wait = ex.bind(x_refs, out_refs, send_sems, recv_sems)
        start()
        wait()

    return list(_comm_call(body, ex.xs, ex.out_shape, 3, name, False))


def _block_rows(rows, row_bytes, target=1 << 21, align=BF16_SUBLANES):
    return _tile(rows, max(align, target // row_bytes // align * align), align)


def _sum_blocks(buf, *, name, out_dtype):
    B, R, C = buf.shape
    tm = _block_rows(R, B * C * buf.dtype.itemsize)

    def body(x_ref, o_ref):
        acc = x_ref[0].astype(F32)
        for b in range(1, B):
            acc = acc + x_ref[b].astype(F32)
        o_ref[...] = acc.astype(o_ref.dtype)

    return pl.pallas_call(
        body, name=name, grid=(R // tm,), in_specs=[pl.BlockSpec((B, tm, C), lambda i: (0, i, 0))],
        out_specs=pl.BlockSpec((tm, C), lambda i: (i, 0)), out_shape=_sds((R, C), out_dtype),
        compiler_params=_params(("parallel",)),
    )(buf)


def _pair_add(mine, theirs, core, *, name):
    _, _, R, C = mine.shape
    tm = _block_rows(R, C * 2)

    def body(core_ref, a_ref, b_ref, o_ref):
        o_ref[...] = (a_ref[...].astype(F32) + b_ref[...].astype(F32)).astype(o_ref.dtype)

    return pl.pallas_call(
        body, name=name, out_shape=_sds(theirs.shape, BF16),
        grid_spec=pltpu.PrefetchScalarGridSpec(
            num_scalar_prefetch=1, grid=(4, R // tm),
            in_specs=[pl.BlockSpec((None, None, tm, C), lambda q, i, core_ref: (q, core_ref[0], i, 0)),
                      pl.BlockSpec((None, tm, C), lambda q, i, core_ref: (q, i, 0))],
            out_specs=pl.BlockSpec((None, tm, C), lambda q, i, core_ref: (q, i, 0))),
        compiler_params=_params(("parallel", "parallel")),
    )(core, mine, theirs)


def _assemble(gathered, own, chip, *, name, transpose):
    _, K, Ns = gathered.shape
    tm = _block_rows(K, Ns * 4)

    def body(chip_ref, g_ref, own_ref, o_ref):
        q = pl.program_id(0)

        @pl.when(q == chip_ref[0])
        def _():
            o_ref[...] = own_ref[...].astype(BF16)

        @pl.when(q != chip_ref[0])
        def _():
            o_ref[...] = g_ref[...]

    if transpose:
        out_spec = pl.BlockSpec((tm, Ns), lambda q, i, ch: (i, q))
        out_shape = _sds((K, 4 * Ns), BF16)
    else:
        out_spec = pl.BlockSpec((None, tm, Ns), lambda q, i, ch: (q, i, 0))
        out_shape = _sds((4, K, Ns), BF16)
    return pl.pallas_call(
        body, name=name, out_shape=out_shape,
        grid_spec=pltpu.PrefetchScalarGridSpec(
            num_scalar_prefetch=1, grid=(4, K // tm),
            in_specs=[pl.BlockSpec((None, tm, Ns), lambda q, i, ch: (jnp.where(q == ch[0], (q + 1) % 4, q), i, 0)),
                      pl.BlockSpec((tm, Ns), lambda q, i, ch: (jnp.where(q == ch[0], i, 0), 0))],
            out_specs=out_spec),
        compiler_params=_params(("arbitrary", "arbitrary")),
    )(chip, gathered, own)


def _assemble_halves(mine, theirs, own, place, *, name, transpose):
    _, K2, Ns = mine.shape
    tm = _block_rows(K2, Ns * 4)
    nb = K2 // tm

    def body(place_ref, m_ref, t_ref, own_ref, o_ref):
        q, hb = pl.program_id(0), pl.program_id(1)
        is_own = q == place_ref[0]
        is_mine = hb == place_ref[1]

        @pl.when(is_own)
        def _():
            o_ref[...] = own_ref[...].astype(BF16)

        @pl.when(jnp.logical_not(is_own) & is_mine)
        def _():
            o_ref[...] = m_ref[...]

        @pl.when(jnp.logical_not(is_own) & jnp.logical_not(is_mine))
        def _():
            o_ref[...] = t_ref[...]

    def other(q, pr):
        return jnp.where(q == pr[0], (q + 1) % 4, q)

    if transpose:
        out_spec = pl.BlockSpec((tm, Ns), lambda q, hb, i, pr: (hb * nb + i, q))
        out_shape = _sds((2 * K2, 4 * Ns), BF16)
    else:
        out_spec = pl.BlockSpec((None, tm, Ns), lambda q, hb, i, pr: (q, hb * nb + i, 0))
        out_shape = _sds((4, 2 * K2, Ns), BF16)
    return pl.pallas_call(
        body, name=name, out_shape=out_shape,
        grid_spec=pltpu.PrefetchScalarGridSpec(
            num_scalar_prefetch=1, grid=(4, 2, nb),
            in_specs=[pl.BlockSpec((None, tm, Ns), lambda q, hb, i, pr: (other(q, pr), jnp.where(hb == pr[1], i, 0), 0)),
                      pl.BlockSpec((None, tm, Ns), lambda q, hb, i, pr: (other(q, pr), jnp.where(hb == pr[1], 0, i), 0)),
                      pl.BlockSpec((tm, Ns), lambda q, hb, i, pr: (jnp.where(q == pr[0], hb * nb + i, 0), 0))],
            out_specs=out_spec),
        compiler_params=_params(("arbitrary",) * 3),
    )(place, mine, theirs, own)


def _split_lanes(row, widths):
    out, off = [], 0
    for wd in widths:
        out.append(row[:, off:off + wd])
        off += wd
    return out


def _adamw(w, g, m, v, *, name, carry=None):
    C = w.shape[1]

    def fn(w, g, m, v):
        m = ADAM_B1 * m + (1.0 - ADAM_B1) * g
        v = ADAM_B2 * v + (1.0 - ADAM_B2) * (g * g)
        m_hat = m / (1.0 - ADAM_B1 ** ADAM_STEP)
        v_hat = v / (1.0 - ADAM_B2 ** ADAM_STEP)
        delta = -ADAM_LR * (m_hat / (jnp.sqrt(v_hat) + ADAM_EPS) + ADAM_WD * w)
        return (delta, m, v), ()

    tm = max(8, min(1024, (1 << 21) // (4 * C) // 8 * 8))
    res = _rowwise(fn, [w, g, m, v], [], [(C, F32)] * 3, name=name, tm=tm, carry=carry)
    return tuple(res[0]) + ((res[2],) if carry is not None else ())


def _rope_tables(n):
    rows = n // GRID_W
    row = jnp.repeat(jnp.arange(rows, dtype=F32), GRID_W)
    col = jnp.tile(jnp.arange(GRID_W, dtype=F32), rows)
    nf = ROPE_DIM // 4
    freqs = ROPE_THETA ** (-jnp.arange(nf, dtype=F32) / nf)
    ang_r, ang_c = row[:, None] * freqs[None, :], col[:, None] * freqs[None, :]
    cr, sr, cc, sc = jnp.cos(ang_r), jnp.sin(ang_r), jnp.cos(ang_c), jnp.sin(ang_c)
    nope = HEAD_PAD - 2 * ROPE_DIM
    one, zero, z = jnp.ones((n, nope), F32), jnp.zeros((n, nope), F32), jnp.zeros((n, nf), F32)
    pad = jnp.zeros((n, ROPE_DIM), F32)
    cos = jnp.concatenate([one, cr, cr, cc, cc, pad], axis=1)
    s_lo = jnp.concatenate([zero, -sr, z, -sc, z, pad], axis=1)
    s_hi = jnp.concatenate([zero, z, sr, z, sc, pad], axis=1)
    return cos, s_lo, s_hi


def _rope(n, cos, s_lo, s_hi):
    q = ROPE_DIM // 4
    return n * cos + pltpu.roll(n, HEAD_PAD - q, 1) * s_lo + pltpu.roll(n, q, 1) * s_hi


def _rope_t(d, cos, s_lo, s_hi):
    q = ROPE_DIM // 4
    return d * cos + pltpu.roll(d * s_lo, q, 1) + pltpu.roll(d * s_hi, HEAD_PAD - q, 1)


def kernel(x, c, ctx, c_ctx, w_mod, b_mod, norm1_g, w_in, q_norm_g, kv_norm_g, w_uq, w_ukv, qk_norm_q, qk_norm_k, sgu_norm_g, sgu_norm_b, w_spatial, b_spatial, w_br_attn, w_br_sgu, w_out, norm2_g, w_ffn_in, w_ffn_out, loss_target, m_c_ctx, m_w_mod, m_b_mod, m_norm1_g, m_w_in, m_q_norm_g, m_kv_norm_g, m_w_uq, m_w_ukv, m_qk_norm_q, m_qk_norm_k, m_sgu_norm_g, m_sgu_norm_b, m_w_spatial, m_b_spatial, m_w_br_attn, m_w_br_sgu, m_w_out, m_norm2_g, m_w_ffn_in, m_w_ffn_out, v_c_ctx, v_w_mod, v_b_mod, v_norm1_g, v_w_in, v_q_norm_g, v_kv_norm_g, v_w_uq, v_w_ukv, v_qk_norm_q, v_qk_norm_k, v_sgu_norm_g, v_sgu_norm_b, v_w_spatial, v_b_spatial, v_w_br_attn, v_w_br_sgu, v_w_out, v_norm2_g, v_w_ffn_in, v_w_ffn_out):
    ax, ay, ac = _place()
    my_chip = 2 * ax + ay
    my_dev = 4 * ax + 2 * ay + ac

    N, D = x.shape[1], x.shape[2]
    CT = ctx.shape[1]
    M = N + CT
    QL, KVL, QK = q_norm_g.shape[-1], kv_norm_g.shape[-1], qk_norm_q.shape[-1]
    NOPE = QK - ROPE_DIM
    VD = NOPE
    H = 4 * w_uq.shape[-1] // QK
    SW, G, CH = sgu_norm_g.shape[-1], w_spatial.shape[1], w_spatial.shape[2]
    GD = SW // G
    DFF = 4 * w_ffn_out.shape[1]
    NMOD = 4 * w_mod.shape[-1]
    NM = w_mod.shape[-1]
    KVP = KVL + 2 * ROPE_DIM
    assert NOPE == LANES and GD == LANES and HEAD_PAD == NOPE + 2 * ROPE_DIM and CH == LANES
    scale = QK ** -0.5

    x2, ctx2, tgt2 = x[0], ctx[0], loss_target[0]

    c_all = _all_gather8([c], name="ag_c", in_vmem=True)[0][:, 0, :]
    c_rows = jnp.concatenate([c_all, c_ctx[None, :], jnp.zeros((BF16_SUBLANES - 9, D), F32)], axis=0)

    def silu_fn(t):
        s = _sigmoid(t)
        return (t * s, s * (1.0 + t * (1.0 - s))), ()

    (silu_c, dsilu_c), _ = _rowwise(silu_fn, [c_rows], [], [(D, F32), (D, F32)], name="silu_c", tm=16)
    wm = w_mod[0]
    mod_loc = _mm([(silu_c, wm)], name="mod_fwd", outs=(F32,), tn=512, tk=512,
                  extras=[(lax.dynamic_slice_in_dim(b_mod, my_chip * NM, NM, axis=1), "n")],
                  epi=lambda acc, b: (acc + b,))
    mod_all = _all_gather8([mod_loc], name="ag_mod", in_vmem=True)[0]
    mod_full = jnp.concatenate([mod_all[0], mod_all[2], mod_all[4], mod_all[6]], axis=1)
    mod_me = lax.dynamic_slice_in_dim(mod_full, my_dev, 1, axis=0)
    sh1, sc1, g1, sh2, sc2, g2 = [mod_me[:, i * D:(i + 1) * D] for i in range(6)]
    sh1c, sc1c = mod_full[8:9, :D], mod_full[8:9, D:2 * D]

    big = [w_in[0], w_uq[0], w_ukv[0], w_br_attn[0], w_br_sgu[0], w_out[0], w_ffn_in[0], w_ffn_out[0]]
    col_sharded = [True, True, True, True, True, False, True, False]
    halves = [lax.dynamic_slice_in_dim(a, ac * (a.shape[0] // 2), a.shape[0] // 2, axis=0).astype(BF16) for a in big]
    tags = ["w_in", "w_uq", "w_ukv", "w_br_attn", "w_br_sgu", "w_out", "w_ffn_in", "w_ffn_out"]
    first_group, attn_group, ffn_group = [0, 1, 2], [3, 4, 5, 6], [7]
    chip1 = jnp.reshape(my_chip, (1,)).astype(jnp.int32)
    place2 = jnp.stack([my_chip, ac]).astype(jnp.int32)

    def laid_out(seg, i):
        a = big[i]
        if col_sharded[i] and seg.ndim == 3:
            return seg.transpose(1, 0, 2).reshape(a.shape[0], 4 * a.shape[1])
        return seg if col_sharded[i] else seg.reshape(4 * a.shape[0], a.shape[1])

    def side_by_side(i):
        return col_sharded[i] and big[i].shape[1] % LANES == 0

    def finish_gather(idx, mine4, theirs4):
        return [laid_out(_assemble_halves(m, t, big[i], place2, name="assemble_" + tags[i], transpose=side_by_side(i)), i)
                for i, m, t in zip(idx, mine4, theirs4)]

    gathered = _all_gather8([halves[i] for i in first_group], name="ag_weights", in_vmem=False, others_only=True)
    w_in_f, w_uq_f, w_ukv_f = [
        laid_out(_assemble(seg.reshape((4,) + big[i].shape), big[i], chip1, name="assemble_" + tags[i],
                           transpose=side_by_side(i)), i) for i, seg in zip(first_group, gathered)]
    o_kv, o_u = QL, QL + KVL + ROPE_DIM
    o_v, o_g = o_u + SW, o_u + 2 * SW
    w_q = w_in_f[:, :QL]
    w_kv = jnp.pad(w_in_f[:, o_kv:o_u], ((0, 0), (0, ROPE_DIM)))
    w_u, w_v = w_in_f[:, o_u:o_v], w_in_f[:, o_v:o_g]
    w_g1, w_g2 = w_in_f[:, o_g:o_g + D], w_in_f[:, o_g + D:]
    w_uq_p = jnp.pad(w_uq_f.reshape(QL, H, QK), ((0, 0), (0, 0), (0, HEAD_PAD - QK))).reshape(QL, H * HEAD_PAD)

    cos_t, slo_t, shi_t = _rope_tables(N)
    ones_c = jnp.concatenate([jnp.ones((CT, NOPE + ROPE_DIM), F32), jnp.zeros((CT, ROPE_DIM), F32)], axis=1)
    cos_k = jnp.concatenate([cos_t, ones_c], axis=0)
    slo_k = jnp.concatenate([slo_t, jnp.zeros((CT, HEAD_PAD), F32)], axis=0)
    shi_k = jnp.concatenate([shi_t, jnp.zeros((CT, HEAD_PAD), F32)], axis=0)
    gq_p = jnp.pad(qk_norm_q, ((0, 0), (0, HEAD_PAD - QK)))
    gk_p = jnp.pad(qk_norm_k, ((0, 0), (0, HEAD_PAD - QK)))

    def norm_mod_fn(t, g, sh, sc):
        r = _rms_stats(t, D)
        return (((t * r) * g) * (1.0 + sc) + sh,), ()

    (h,), _ = _rowwise(norm_mod_fn, [x2], [norm1_g, sh1, sc1], [(D, BF16)], name="norm1_x")
    (ctx_h,), _ = _rowwise(norm_mod_fn, [ctx2], [norm1_g, sh1c, sc1c], [(D, BF16)], name="norm1_ctx")

    qc = _mm([(h, w_q)], name="proj_q", outs=(F32,))
    kvin = jnp.concatenate([_mm([(h, w_kv)], name="proj_kv", outs=(F32,)),
                            _mm([(ctx_h, w_kv)], name="proj_kv_ctx", outs=(F32,))], axis=0)
    u_in = _mm([(h, w_u)], name="proj_u", outs=(BF16,))
    v_in = _mm([(h, w_v)], name="proj_v", outs=(BF16,))
    g1_in = _mm([(h, w_g1)], name="proj_g1", outs=(BF16,))
    g2_in = _mm([(h, w_g2)], name="proj_g2", outs=(BF16,))

    def rms_gain_fn(width):
        def fn(t, g):
            return (((t * _rms_stats(t, width)) * g),), ()
        return fn

    (qn,), _ = _rowwise(rms_gain_fn(QL), [qc], [q_norm_g], [(QL, BF16)], name="q_norm")

    def kv_norm_fn(t, g):
        kvc = t[:, :KVL]
        return (((kvc * _rms_stats(kvc, KVL)) * g),), ()

    (kvn,), _ = _rowwise(kv_norm_fn, [kvin], [kv_norm_g], [(KVL, BF16)], name="kv_norm")
    q_raw = _mm([(qn, w_uq_p)], name="q_up", outs=(F32,))
    kv_raw = _mm([(kvn, w_ukv_f)], name="kv_up", outs=(F32,))

    def q_post_fn(t, cos, slo, shi, g):
        outs = []
        for hd in range(H):
            th = t[:, hd * HEAD_PAD:(hd + 1) * HEAD_PAD]
            outs.append(_rope((th * _rms_stats(th, QK)) * g, cos, slo, shi) * scale)
        return (jnp.concatenate(outs, axis=1),), ()

    (q_att,), _ = _rowwise(q_post_fn, [q_raw, cos_t, slo_t, shi_t], [gq_p], [(H * HEAD_PAD, BF16)], name="q_post")

    def k_post_fn(t, kvi, cos, slo, shi, g):
        kr = kvi[:, KVL:]
        ks, vs = [], []
        for hd in range(H):
            th = jnp.concatenate([t[:, hd * HEAD_PAD:hd * HEAD_PAD + NOPE], kr], axis=1)
            ks.append(_rope((th * _rms_stats(th, QK)) * g, cos, slo, shi))
            vs.append(t[:, hd * HEAD_PAD + NOPE:(hd + 1) * HEAD_PAD])
        return (jnp.concatenate(ks, axis=1), jnp.concatenate(vs, axis=1)), ()

    (k_att, v_att), _ = _rowwise(k_post_fn, [kv_raw, kvin, cos_k, slo_k, shi_k], [gk_p],
                                 [(H * HEAD_PAD, BF16), (H * VD, BF16)], name="k_post")
    attn_o, lse, mine4 = _attn_fwd(q_att, k_att, v_att, heads=H,
                                   carry=_ChipExchange([halves[i] for i in attn_group], gather=True))

    ws3 = w_spatial[0]
    bs_t = jnp.pad(b_spatial[0].T, ((0, 0), (0, LANES - G)))

    def sgu_parts(u_in, v_in, ng, nb):
        u, v = _gelu(u_in.astype(F32)), _gelu(v_in.astype(F32))
        mu = jnp.mean(v, axis=-1, keepdims=True)
        vc = v - mu
        rs = lax.rsqrt(jnp.mean(vc * vc, axis=-1, keepdims=True) + EPS)
        xhat = vc * rs
        return u, xhat, rs, (xhat * ng + nb).astype(BF16)

    def sgu_fwd_fn(u_in, v_in, ng, nb, ws, bst):
        u, _, _, vnb = sgu_parts(u_in, v_in, ng, nb)
        outs = []
        for g in range(G):
            sl = slice(g * GD, (g + 1) * GD)
            mixed = jnp.dot(ws[g].astype(BF16), vnb[:, sl], preferred_element_type=F32) + bst[:, g:g + 1]
            outs.append(u[:, sl] * mixed)
        return (jnp.concatenate(outs, axis=1),), ()

    (sgu_o,), _, theirs4 = _rowwise(sgu_fwd_fn, [u_in, v_in], [sgu_norm_g, sgu_norm_b, ws3, bs_t], [(SW, BF16)],
                                    name="sgu_fwd", tm=CH, carry=_PairExchange(mine4, "forward"))
    w_bra, w_brs, w_out_f, w_ffi = finish_gather(attn_group, mine4, theirs4)
    w_fa, w_fb = w_ffi[:, :DFF], w_ffi[:, DFF:]

    a1 = _mm([(attn_o, w_bra)], name="br_attn", outs=(BF16,))
    a2 = _mm([(sgu_o, w_brs)], name="br_sgu", outs=(BF16,))

    def merge_fn(a1, a2, gi1, gi2):
        return ((_sigmoid(gi1.astype(F32)) * a1.astype(F32) + _sigmoid(gi2.astype(F32)) * a2.astype(F32)),), ()

    (merged,), _ = _rowwise(merge_fn, [a1, a2, g1_in, g2_in], [], [(D, BF16)], name="merge", tc=1024)

    def res_gate(acc, res, gate):
        return res + gate * acc, acc

    x1, mo = _mm([(merged, w_out_f)], name="out_proj", outs=(F32, BF16), tn=512,
                 extras=[(x2, "mn"), (g1, "n")], epi=res_gate)
    (h2,), _ = _rowwise(norm_mod_fn, [x1], [norm2_g, sh2, sc2], [(D, BF16)], name="norm2")

    def swiglu_epi(a, b):
        return a, b, (a * _sigmoid(a)) * b

    (fa, fb, act), mine4 = _mm([(h2, w_fa, w_fb)], name="ffn_in", outs=(BF16, BF16, BF16), tn=512, epi=swiglu_epi,
                               carry=_ChipExchange([halves[i] for i in ffn_group], gather=True))
    (w_ffo,) = finish_gather(ffn_group, mine4, _exchange_alone(_PairExchange(mine4, "forward"), name="ag_forward_ffn"))
    y, f_out = _mm([(act, w_ffo)], name="ffn_out", outs=(F32, BF16), tn=512,
                   extras=[(x1, "mn"), (g2, "n")], epi=res_gate)

    def loss_fn(y, t, fo, g2v):
        e = y - t
        dy = e * (1.0 / D)
        return (dy, g2v * dy), (_colsum(e * e) * (0.5 / D), _colsum(dy * fo.astype(F32)))

    (dy, df), (loss_cols, dg2) = _rowwise(loss_fn, [y, tgt2, f_out], [g2], [(D, F32), (D, BF16)],
                                          [(1, D), (1, D)], name="loss")

    def swiglu_bwd_epi(dact, a, b):
        a, b = a.astype(F32), b.astype(F32)
        s = _sigmoid(a)
        return dact * b * (s * (1.0 + a * (1.0 - s))), dact * (a * s)

    da, db = _mm([(df, w_ffo)], tb=True, name="ffn_out_dx", outs=(BF16, BF16), tn=512,
                 extras=[(fa, "mn"), (fb, "mn")], epi=swiglu_bwd_epi)
    dw_ffo = _mm([(act, df)], ta=True, name="ffn_out_dw", outs=(BF16,))
    dh2 = _mm([(da, w_fa), (db, w_fb)], tb=True, name="ffn_in_dx", outs=(F32,))
    ns_ffi = w_ffn_in.shape[-1]
    dw_ffi = _mm([(h2, da)], ta=True, name="ffn_in_dw_a", outs=(BF16,), tn=1408, split=ns_ffi,
                 into=(lax.empty((4, D, ns_ffi), BF16), 0))
    dw_ffi = _mm([(h2, db)], ta=True, name="ffn_in_dw_b", outs=(BF16,), tn=1408, split=ns_ffi, into=(dw_ffi, 2))

    def norm2_bwd_fn(dh, t, dyv, mov, g, sc, g1v):
        r = _rms_stats(t, D)
        tn = t * r
        dxg = dh * (1.0 + sc)
        dt = dyv + _rms_bwd(dxg * g, tn, r, D)
        return (dt, g1v * dt), (_colsum(dh), _colsum(dh * (tn * g)), _colsum(dxg * tn), _colsum(dt * mov.astype(F32)))

    (dx1, dmo), (dsh2, dsc2, dn2g, dg1) = _rowwise(
        norm2_bwd_fn, [dh2, x1, dy, mo], [norm2_g, sc2, g1], [(D, F32), (D, BF16)], [(1, D)] * 4, name="norm2_bwd")

    def merge_bwd_epi(dm, a1, a2, gi1, gi2):
        s1, s2 = _sigmoid(gi1.astype(F32)), _sigmoid(gi2.astype(F32))
        a1, a2 = a1.astype(F32), a2.astype(F32)
        return dm * s1, dm * s2, dm * a1 * (s1 * (1.0 - s1)), dm * a2 * (s2 * (1.0 - s2))

    da1, da2, dgi1, dgi2 = _mm([(dmo, w_out_f)], tb=True, name="out_proj_dx", outs=(BF16,) * 4, tn=512,
                               extras=[(a1, "mn"), (a2, "mn"), (g1_in, "mn"), (g2_in, "mn")], epi=merge_bwd_epi)
    dw_out = _mm([(merged, dmo)], ta=True, name="out_proj_dw", outs=(BF16,))
    dattn = _mm([(da1, w_bra)], tb=True, name="br_attn_dx", outs=(BF16,))
    dw_bra = _mm([(attn_o, da1)], ta=True, name="br_attn_dw", outs=(BF16,), split=w_br_attn.shape[-1])
    dsgu = _mm([(da2, w_brs)], tb=True, name="br_sgu_dx", outs=(BF16,))
    dw_brs = _mm([(sgu_o, da2)], ta=True, name="br_sgu_dw", outs=(BF16,), split=w_br_sgu.shape[-1])

    def sgu_bwd_fn(dso, u_in, v_in, ng, nb, ws, bst):
        u, xhat, rs, vnb = sgu_parts(u_in, v_in, ng, nb)
        dso = dso.astype(F32)
        lane = lax.broadcasted_iota(jnp.int32, (CH, LANES), 1)
        du, dvn, dws, dbs = [], [], [], jnp.zeros((CH, LANES), F32)
        for g in range(G):
            sl = slice(g * GD, (g + 1) * GD)
            wg = ws[g].astype(BF16)
            mixed = jnp.dot(wg, vnb[:, sl], preferred_element_type=F32) + bst[:, g:g + 1]
            du.append(dso[:, sl] * mixed)
            dmix = dso[:, sl] * u[:, sl]
            dmb = dmix.astype(BF16)
            dws.append(lax.dot_general(dmb, vnb[:, sl], (((1,), (1,)), ((), ())), preferred_element_type=F32))
            dbs = dbs + jnp.where(lane == g, jnp.sum(dmix, axis=1, keepdims=True), 0.0)
            dvn.append(lax.dot_general(wg, dmb, (((0,), (0,)), ((), ())), preferred_element_type=F32))
        du, dvn = jnp.concatenate(du, axis=1), jnp.concatenate(dvn, axis=1)
        dxh = dvn * ng
        dv = rs * (dxh - jnp.mean(dxh, axis=-1, keepdims=True) - xhat * jnp.mean(dxh * xhat, axis=-1, keepdims=True))
        return ((du * _gelu_grad(u_in.astype(F32)), dv * _gelu_grad(v_in.astype(F32))),
                (_colsum(dvn * xhat), _colsum(dvn), jnp.stack(dws), dbs))

    core = jnp.reshape(ac, (1,)).astype(jnp.int32)

    def dest_layout(dwf, i):
        K, Ns = big[i].shape
        if dwf.ndim == 2:
            dwf = dwf.reshape(K, 4, Ns).transpose(1, 0, 2) if col_sharded[i] else dwf.reshape(4, K, Ns)
        return dwf.reshape(4, 2, K // 2, Ns)

    def pair_sums(idx, g4, sib):
        return [_pair_add(g, s, core, name="rs_pair_add_" + tags[i]) for g, s, i in zip(g4, sib, idx)]

    early = [3, 4, 5, 6, 7]
    g4_early = [dest_layout(d, i) for d, i in zip([dw_bra, dw_brs, dw_out, dw_ffi, dw_ffo], early)]
    (du_in, dv_in), (d_sng, d_snb, d_ws, d_bs), sib_early = _rowwise(
        sgu_bwd_fn, [dsgu, u_in, v_in], [sgu_norm_g, sgu_norm_b, ws3, bs_t], [(SW, BF16), (SW, BF16)],
        [(1, SW), (1, SW), (G, CH, CH), (CH, LANES)], name="sgu_bwd", tm=CH, carry=_PairExchange(g4_early, "halves"))
    pair_early = pair_sums(early, g4_early, sib_early)
    dq_att, dk_att, dv_att, xchg_early = _attn_bwd(q_att, k_att, v_att, attn_o, lse, dattn, heads=H,
                                                   carry=_ChipExchange(pair_early, gather=False))

    def q_post_bwd_fn(dq, t, cos, slo, shi, g):
        outs, dg = [], jnp.zeros((1, HEAD_PAD), F32)
        for hd in range(H):
            sl = slice(hd * HEAD_PAD, (hd + 1) * HEAD_PAD)
            th = t[:, sl]
            r = _rms_stats(th, QK)
            tn = th * r
            dn = _rope_t(dq[:, sl] * scale, cos, slo, shi)
            dg = dg + _colsum(dn * tn)
            outs.append(_rms_bwd(dn * g, tn, r, QK))
        return (jnp.concatenate(outs, axis=1),), (dg,)

    (dq_raw,), (d_gq,) = _rowwise(q_post_bwd_fn, [dq_att, q_raw, cos_t, slo_t, shi_t], [gq_p],
                                  [(H * HEAD_PAD, BF16)], [(1, HEAD_PAD)], name="q_post_bwd")

    def k_post_bwd_fn(dk, dv, t, kvi, cos, slo, shi, g):
        kr = kvi[:, KVL:]
        outs, dg, dkr = [], jnp.zeros((1, HEAD_PAD), F32), jnp.zeros_like(kr)
        for hd in range(H):
            th = jnp.concatenate([t[:, hd * HEAD_PAD:hd * HEAD_PAD + NOPE], kr], axis=1)
            r = _rms_stats(th, QK)
            tn = th * r
            dn = _rope_t(dk[:, hd * HEAD_PAD:(hd + 1) * HEAD_PAD], cos, slo, shi)
            dg = dg + _colsum(dn * tn)
            dt = _rms_bwd(dn * g, tn, r, QK)
            dkr = dkr + dt[:, NOPE:]
            outs += [dt[:, :NOPE], dv[:, hd * VD:(hd + 1) * VD]]
        return (jnp.concatenate(outs, axis=1), dkr), (dg,)

    (dkv_raw, dkrope), (d_gk,) = _rowwise(
        k_post_bwd_fn, [dk_att, dv_att, kv_raw, kvin, cos_k, slo_k, shi_k], [gk_p],
        [(H * HEAD_PAD, BF16), (2 * ROPE_DIM, F32)], [(1, HEAD_PAD)], name="k_post_bwd")

    dqn = _mm([(dq_raw, w_uq_p)], tb=True, name="q_up_dx", outs=(F32,))
    dw_uq_p = _mm([(qn, dq_raw)], ta=True, name="q_up_dw", outs=(BF16,))
    dkvn = _mm([(dkv_raw, w_ukv_f)], tb=True, name="kv_up_dx", outs=(F32,))
    dw_ukv = _mm([(kvn, dkv_raw)], ta=True, name="kv_up_dw", outs=(BF16,), split=w_ukv.shape[-1])

    def q_norm_bwd_fn(dn, t, g):
        r = _rms_stats(t, QL)
        tn = t * r
        return (_rms_bwd(dn * g, tn, r, QL),), (_colsum(dn * tn),)

    (dqc,), (d_qng,) = _rowwise(q_norm_bwd_fn, [dqn, qc], [q_norm_g], [(QL, BF16)], [(1, QL)], name="q_norm_bwd")

    def kv_norm_bwd_fn(dn, dkr, t, g):
        kvc = t[:, :KVL]
        r = _rms_stats(kvc, KVL)
        tn = kvc * r
        return (jnp.concatenate([_rms_bwd(dn * g, tn, r, KVL), dkr], axis=1),), (_colsum(dn * tn),)

    (dkvin,), (d_kvng,) = _rowwise(kv_norm_bwd_fn, [dkvn, dkrope, kvin], [kv_norm_g], [(KVP, BF16)], [(1, KVL)],
                                   name="kv_norm_bwd")
    dkvin_x, dkvin_c = dkvin[:N], dkvin[N:]

    dctx_h = _mm([(dkvin_c, w_kv)], tb=True, name="proj_kv_ctx_dx", outs=(F32,))
    dw_q = _mm([(h, dqc)], ta=True, name="proj_q_dw", outs=(BF16,))
    dw_kv = _mm([(h, dkvin_x), (ctx_h, dkvin_c)], ta=True, name="proj_kv_dw", outs=(BF16,))
    dw_u = _mm([(h, du_in)], ta=True, name="proj_u_dw", outs=(BF16,))
    dw_v = _mm([(h, dv_in)], ta=True, name="proj_v_dw", outs=(BF16,))
    dw_g1 = _mm([(h, dgi1)], ta=True, name="proj_g1_dw", outs=(BF16,))
    dw_g2 = _mm([(h, dgi2)], ta=True, name="proj_g2_dw", outs=(BF16,))

    dw_in_f = jnp.concatenate([dw_q, dw_kv[:, :KVL + ROPE_DIM], dw_u, dw_v, dw_g1, dw_g2], axis=1)
    dw_uq_f = dw_uq_p.reshape(QL, H, HEAD_PAD)[:, :, :QK].reshape(QL, H * QK)
    late = [0, 1, 2]
    g4_late = [dest_layout(d, i) for d, i in zip([dw_in_f, dw_uq_f, dw_ukv], late)]
    pair_late = pair_sums(late, g4_late, _exchange_alone(_PairExchange(g4_late, "halves"), name="rs_pair_late"))
    dh, xchg_late = _mm([(dqc, w_q), (dkvin_x, w_kv), (du_in, w_u), (dv_in, w_v), (dgi1, w_g1), (dgi2, w_g2)],
                        tb=True, name="proj_dx", outs=(F32,), tn=512, tk=1024,
                        carry=_ChipExchange(pair_late, gather=False))

    def norm1_bwd_fn(dhv, t, dres, g, sc):
        r = _rms_stats(t, D)
        tn = t * r
        dxg = dhv * (1.0 + sc)
        return (dres + _rms_bwd(dxg * g, tn, r, D),), (_colsum(dhv), _colsum(dhv * (tn * g)), _colsum(dxg * tn))

    (grad_x,), (dsh1, dsc1, dn1g_x) = _rowwise(norm1_bwd_fn, [dh, x2, dx1], [norm1_g, sc1], [(D, F32)], [(1, D)] * 3,
                                               name="norm1_bwd")
    _, (dsh1c, dsc1c, dn1g_c) = _rowwise(norm1_bwd_fn, [dctx_h, ctx2, jnp.zeros_like(ctx2)], [norm1_g, sc1c],
                                         [(D, F32)], [(1, D)] * 3, name="norm1_ctx_bwd")

    small = [dsh1, dsc1, dg1, dsh2, dsc2, dg2,
             dsh1c, dsc1c, dn1g_x, dn1g_c, d_qng, d_kvng, d_gq, d_gk, d_sng, d_snb, dn2g, loss_cols]
    small_sizes = [a.shape[1] for a in small]
    sm_row = jnp.concatenate(small, axis=1)
    sm_mat = jnp.concatenate([d_ws.reshape(G * CH, CH), d_bs], axis=0)
    row_all, mat_all = _all_gather8([sm_row, sm_mat], name="ag_small", in_vmem=True)
    row_sum = _sum_blocks(row_all, name="sum_small_rows", out_dtype=F32)
    mat_sum = _sum_blocks(mat_all, name="sum_small_mats", out_dtype=F32)
    dmod_rows = row_all[:, 0, :NMOD]
    (_, _, _, _, _, _, t_sh1c, t_sc1c, t_n1x, t_n1c, g_qng, g_kvng, t_gq, t_gk, g_sng, g_snb, g_n2g,
     t_loss) = _split_lanes(row_sum, small_sizes)
    g_ws, t_bs = mat_sum[:G * CH], mat_sum[G * CH:]
    dmodc_row = jnp.concatenate([t_sh1c, t_sc1c, jnp.zeros((1, NMOD - 2 * D), F32)], axis=1)
    dmod16 = jnp.concatenate([dmod_rows, dmodc_row, jnp.zeros((BF16_SUBLANES - 9, NMOD), F32)], axis=0)

    def small_fn(rows, n1x, n1c, lossv):
        return (), (_colsum(rows), n1x + n1c, jnp.sum(lossv, axis=1, keepdims=True))

    _, (g_bmod, g_n1g, loss11) = _rowwise(small_fn, [dmod16], [t_n1x, t_n1c, t_loss], [], [(1, NMOD), (1, D), (1, 1)],
                                          name="small_reduce", tm=16)
    dmod_loc = lax.dynamic_slice_in_dim(dmod16, my_chip * NM, NM, axis=1)
    g_wmod = _mm([(silu_c, dmod_loc)], ta=True, name="mod_dw", outs=(F32,), tn=512)
    dsilu_part = _mm([(dmod_loc, wm)], tb=True, name="mod_dx", outs=(F32,), tk=512)
    part_all = _all_gather8([dsilu_part[8:9]], name="ag_cctx", in_vmem=True)[0]

    def cctx_fn(parts, dsl):
        return (), ((parts[0:1] + parts[2:3] + parts[4:5] + parts[6:7]) * dsl,)

    _, (g_cctx,) = _rowwise(cctx_fn, [part_all[:, 0, :]], [dsilu_c[8:9]], [], [(1, D)], name="cctx_grad", tm=8)

    pair = pair_late + pair_early
    xchg = [lax.dynamic_update_index_in_dim(t4, lax.dynamic_index_in_dim(pr, my_chip, 0, keepdims=False), my_chip, 0)
            for t4, pr in zip(xchg_late + xchg_early, pair)]
    red_half = [_sum_blocks(t4, name="rs_sum_" + t, out_dtype=F32) for t4, t in zip(xchg, tags)]
    mod_upd = _adamw(w_mod[0], g_wmod, m_w_mod[0], v_w_mod[0], name="adamw_w_mod",
                     carry=_PairExchange(red_half, "gather"))
    big_grads = [lax.dynamic_update_index_in_dim(r, mine, ac, 0).reshape(a.shape)
                 for r, mine, a in zip(mod_upd[3], red_half, big)]

    def upd(w, g, m, v, nm):
        shape = w.shape
        w2, g2_, m2, v2 = [t.reshape(-1, shape[-1]) for t in (w, g, m, v)]
        d_, m_, v_ = _adamw(w2, g2_, m2, v2, name="adamw_" + nm)
        return g.reshape(shape), d_.reshape(shape), m_.reshape(shape), v_.reshape(shape)

    g_in, g_uq, g_ukv, g_bra, g_brs, g_out, g_ffi, g_ffo = big_grads
    grads = dict(
        c_ctx=g_cctx.reshape(D), w_mod=g_wmod[None], b_mod=g_bmod, norm1_g=g_n1g, w_in=g_in[None],
        q_norm_g=g_qng, kv_norm_g=g_kvng, w_uq=g_uq[None], w_ukv=g_ukv[None],
        qk_norm_q=t_gq[:, :QK], qk_norm_k=t_gk[:, :QK], sgu_norm_g=g_sng, sgu_norm_b=g_snb,
        w_spatial=g_ws.reshape(w_spatial.shape), b_spatial=t_bs[:, :G].T[None],
        w_br_attn=g_bra[None], w_br_sgu=g_brs[None], w_out=g_out[None], norm2_g=g_n2g,
        w_ffn_in=g_ffi[None], w_ffn_out=g_ffo[None])
    weights = dict(c_ctx=c_ctx, w_mod=w_mod, b_mod=b_mod, norm1_g=norm1_g, w_in=w_in, q_norm_g=q_norm_g,
                   kv_norm_g=kv_norm_g, w_uq=w_uq, w_ukv=w_ukv, qk_norm_q=qk_norm_q, qk_norm_k=qk_norm_k,
                   sgu_norm_g=sgu_norm_g, sgu_norm_b=sgu_norm_b, w_spatial=w_spatial, b_spatial=b_spatial,
                   w_br_attn=w_br_attn, w_br_sgu=w_br_sgu, w_out=w_out, norm2_g=norm2_g, w_ffn_in=w_ffn_in,
                   w_ffn_out=w_ffn_out)
    m_in = dict(c_ctx=m_c_ctx, w_mod=m_w_mod, b_mod=m_b_mod, norm1_g=m_norm1_g, w_in=m_w_in, q_norm_g=m_q_norm_g,
                kv_norm_g=m_kv_norm_g, w_uq=m_w_uq, w_ukv=m_w_ukv, qk_norm_q=m_qk_norm_q, qk_norm_k=m_qk_norm_k,
                sgu_norm_g=m_sgu_norm_g, sgu_norm_b=m_sgu_norm_b, w_spatial=m_w_spatial, b_spatial=m_b_spatial,
                w_br_attn=m_w_br_attn, w_br_sgu=m_w_br_sgu, w_out=m_w_out, norm2_g=m_norm2_g, w_ffn_in=m_w_ffn_in,
                w_ffn_out=m_w_ffn_out)
    v_in_ = dict(c_ctx=v_c_ctx, w_mod=v_w_mod, b_mod=v_b_mod, norm1_g=v_norm1_g, w_in=v_w_in, q_norm_g=v_q_norm_g,
                 kv_norm_g=v_kv_norm_g, w_uq=v_w_uq, w_ukv=v_w_ukv, qk_norm_q=v_qk_norm_q, qk_norm_k=v_qk_norm_k,
                 sgu_norm_g=v_sgu_norm_g, sgu_norm_b=v_sgu_norm_b, w_spatial=v_w_spatial, b_spatial=v_b_spatial,
                 w_br_attn=v_w_br_attn, w_br_sgu=v_w_br_sgu, w_out=v_w_out, norm2_g=v_norm2_g, w_ffn_in=v_w_ffn_in,
                 w_ffn_out=v_w_ffn_out)
    names = list(weights)
    big_names = ("w_mod", "w_in", "w_uq", "w_ukv", "w_br_attn", "w_br_sgu", "w_out", "w_ffn_in", "w_ffn_out")
    out_g, out_d, out_m, out_v = {}, {}, {}, {}
    out_g["w_mod"] = grads["w_mod"]
    out_d["w_mod"], out_m["w_mod"], out_v["w_mod"] = [t[None] for t in mod_upd[:3]]
    for nm in big_names[1:]:
        out_g[nm], out_d[nm], out_m[nm], out_v[nm] = upd(weights[nm], grads[nm], m_in[nm], v_in_[nm], nm)
    row_names = [nm for nm in names if nm not in big_names and nm not in ("w_spatial", "b_spatial")]
    widths = [-(-weights[nm].size // LANES) * LANES for nm in row_names]

    def as_row(d):
        return jnp.concatenate([jnp.pad(d[nm].reshape(1, -1), ((0, 0), (0, wd - d[nm].size)))
                                for nm, wd in zip(row_names, widths)], axis=1)

    def as_mat(d):
        return jnp.concatenate([d["w_spatial"].reshape(G * CH, CH), d["b_spatial"].reshape(G, CH)], axis=0)

    row_res = _adamw(as_row(weights), as_row(grads), as_row(m_in), as_row(v_in_), name="adamw_rows")
    mat_res = _adamw(as_mat(weights), as_mat(grads), as_mat(m_in), as_mat(v_in_), name="adamw_spatial")
    for tgt, row, mat in zip((out_d, out_m, out_v), row_res, mat_res):
        for nm, seg in zip(row_names, _split_lanes(row, widths)):
            tgt[nm] = seg[:, :weights[nm].size].reshape(weights[nm].shape)
        tgt["w_spatial"] = mat[:G * CH].reshape(w_spatial.shape)
        tgt["b_spatial"] = mat[G * CH:].reshape(b_spatial.shape)
    for nm in row_names + ["w_spatial", "b_spatial"]:
        out_g[nm] = grads[nm].reshape(weights[nm].shape)

    loss = loss11.reshape(())
    return (loss, grad_x[None], *[out_g[n] for n in names], *[out_d[n] for n in names],
            *[out_m[n] for n in names], *[out_v[n] for n in names])
```

```python
import math

import jax
import jax.numpy as jnp
from jax import lax
from jax.experimental import pallas as pl
from jax.experimental.pallas import tpu as pltpu

F32, BF16 = jnp.float32, jnp.bfloat16
MESH = pl.DeviceIdType.MESH

LANES = 128
BF16_SUBLANES = 16
VMEM_LIMIT_BYTES = 56 * 1024 * 1024

EPS = 1e-6
ROPE_DIM = 64
ROPE_THETA = 10000.0
GRID_W = 64
HEAD_PAD = 256
ADAM_LR, ADAM_B1, ADAM_B2, ADAM_EPS, ADAM_WD, ADAM_STEP = 0.001, 0.9, 0.999, 1e-08, 0.01, 10


def _tile(dim, pref, align=LANES):
    if dim <= pref:
        return dim
    t = (pref // align) * align
    while t >= align:
        if dim % t == 0:
            return t
        t -= align
    return dim


def _params(sem=None):
    return pltpu.CompilerParams(dimension_semantics=sem, vmem_limit_bytes=VMEM_LIMIT_BYTES)


def _sds(shape, dtype):
    return jax.ShapeDtypeStruct(tuple(shape), dtype)


def _mm(pairs, *, name, ta=False, tb=False, outs=(F32,), tm=1024, tn=1024, tk=2048, extras=(), epi=None,
        split=None, into=None, carry=None):
    dual = len(pairs[0]) == 3
    a0, b0 = pairs[0][0], pairs[0][1]
    M = a0.shape[1] if ta else a0.shape[0]
    N = b0.shape[0] if tb else b0.shape[1]
    tm, tn = _tile(M, tm), _tile(N if split is None else split, tn)
    ks = [(p[0].shape[0] if ta else p[0].shape[1]) for p in pairs]
    tks = [_tile(k, tk) for k in ks]
    nks = [k // t for k, t in zip(ks, tks)]
    offs = [sum(nks[:i]) for i in range(len(pairs))]
    nk_total = sum(nks)
    single = len(pairs) == 1

    def kidx(kk, p):
        return kk if single else jnp.clip(kk - offs[p], 0, nks[p] - 1)

    in_specs, operands = [], []
    for p, pr in enumerate(pairs):
        if ta:
            in_specs.append(pl.BlockSpec((tks[p], tm), lambda i, j, kk, p=p: (kidx(kk, p), i)))
        else:
            in_specs.append(pl.BlockSpec((tm, tks[p]), lambda i, j, kk, p=p: (i, kidx(kk, p))))
        operands.append(pr[0])
        for b in pr[1:]:
            if tb:
                in_specs.append(pl.BlockSpec((tn, tks[p]), lambda i, j, kk, p=p: (j, kidx(kk, p))))
            else:
                in_specs.append(pl.BlockSpec((tks[p], tn), lambda i, j, kk, p=p: (kidx(kk, p), j)))
            operands.append(b)
    for arr, kind in extras:
        if kind == "mn":
            in_specs.append(pl.BlockSpec((tm, tn), lambda i, j, kk: (i, j)))
        else:
            in_specs.append(pl.BlockSpec((1, tn), lambda i, j, kk: (0, j)))
        operands.append(arr)
    n_in = len(operands)
    n_ex = len(extras)
    per = 3 if dual else 2
    dims = (((0 if ta else 1,), (1 if tb else 0,)), ((), ()))

    n_acc = 2 if dual else 1

    def products(ins, p):
        a = ins[per * p][...].astype(BF16)
        return [lax.dot_general(a, ins[per * p + 1 + q][...].astype(BF16), dims, preferred_element_type=F32)
                for q in range(n_acc)]

    def finish(ins, out_refs, acc_vals):
        vals = acc_vals + [r[...] for r in ins[n_in - n_ex:]]
        res = epi(*vals) if epi is not None else (vals[0],)
        for o, r in zip(out_refs, res):
            o[...] = r.astype(o.dtype)

    out_specs = [pl.BlockSpec((tm, tn), lambda i, j, kk: (i, j)) for _ in outs]
    out_shape = [_sds((M, N), d) for d in outs]
    aliases = {}
    n_alias = 0
    if split is not None:
        nps = split // tn
        lead = 0 if into is None else into[1]
        out_specs = [pl.BlockSpec((None, tm, tn), lambda i, j, kk: (j // nps + lead, i, j % nps))]
        out_shape = [_sds((N // split if into is None else into[0].shape[0], M, split), outs[0])]
        if into is not None:
            in_specs.append(pl.BlockSpec(memory_space=pl.ANY))
            operands.append(into[0])
            aliases, n_alias = {n_in: 0}, 1

    grid = (M // tm, N // tn, nk_total)

    def at_step(first):
        ids = [pl.program_id(d) for d in range(3)]
        cond = None
        for d, g in zip(ids, grid):
            t = d == (0 if first else g - 1)
            cond = t if cond is None else cond & t
        return cond

    def body(*refs):
        ins, out_refs, accs, start, wait = _split_refs(refs, n_in + n_alias, len(outs), carry)
        ins = ins[:n_in]
        if carry is not None:
            pl.when(at_step(True))(start)
        if nk_total == 1:
            finish(ins, out_refs, products(ins, 0))
        else:
            kk = pl.program_id(2)

            @pl.when(kk == 0)
            def _():
                for acc, v in zip(accs, products(ins, 0)):
                    acc[...] = v

            for p in range(len(pairs)):
                lo = max(offs[p], 1)

                @pl.when((kk >= lo) & (kk < offs[p] + nks[p]))
                def _(p=p):
                    for acc, v in zip(accs, products(ins, p)):
                        acc[...] += v

            @pl.when(kk == nk_total - 1)
            def _():
                finish(ins, out_refs, [acc[...] for acc in accs])
        if carry is not None:
            pl.when(at_step(False))(wait)

    ex = carry
    res = pl.pallas_call(
        body, name=name, grid=grid, in_specs=in_specs + ([] if ex is None else ex.in_specs),
        out_specs=out_specs + ([] if ex is None else ex.out_specs),
        out_shape=out_shape + ([] if ex is None else ex.out_shape), input_output_aliases=aliases,
        scratch_shapes=[pltpu.VMEM((tm, tn), F32) for _ in range(n_acc if nk_total > 1 else 0)]
        + ([] if ex is None else ex.scratch),
        compiler_params=_params(("arbitrary",) * 3 if ex is not None else ("parallel", "parallel", "arbitrary")),
    )(*operands, *([] if ex is None else ex.xs))
    if ex is not None:
        return (res[0] if len(outs) == 1 else res[:len(outs)]), list(res[len(outs):])
    return res[0] if len(outs) == 1 else res


def _rowwise(fn, rows, vecs, out_rows, out_accs=(), *, name, tm=256, tc=None, carry=None):
    M = rows[0].shape[0]
    tm = _tile(M, tm, BF16_SUBLANES)
    nrow = M // tm
    C = rows[0].shape[1]
    ncol = 1 if tc is None else C // _tile(C, tc)
    tcol = None if tc is None else _tile(C, tc)

    def colwise(shape):
        return tc is not None and len(shape) == 2 and shape[0] == 1 and shape[1] == C

    def vspec(shape):
        if colwise(shape):
            return pl.BlockSpec((1, tcol), lambda j, i: (0, j))
        return pl.BlockSpec(tuple(shape), lambda j, i, n=len(shape): (0,) * n)

    def rspec(width):
        if tc is None:
            return pl.BlockSpec((tm, width), lambda j, i: (i, 0))
        return pl.BlockSpec((tm, tcol), lambda j, i: (i, j))

    in_specs = [rspec(r.shape[1]) for r in rows] + [vspec(v.shape) for v in vecs]
    out_specs = [rspec(c) for c, _ in out_rows] + [vspec(s) for s in out_accs]
    out_shape = [_sds((M, c), d) for c, d in out_rows] + [_sds(s, F32) for s in out_accs]
    n_in, n_or = len(rows) + len(vecs), len(out_rows)

    n_out = n_or + len(out_accs)
    ex = carry

    def body(*refs):
        ins, outs, _, start, wait = _split_refs(refs, n_in, n_out, ex)
        o_rows, o_accs = outs[:n_or], outs[n_or:]
        if ex is not None:
            pl.when((pl.program_id(0) == 0) & (pl.program_id(1) == 0))(start)
        r_out, a_out = fn(*[r[...] for r in ins])
        for o, r in zip(o_rows, r_out):
            o[...] = r.astype(o.dtype)
        i = pl.program_id(1)

        @pl.when(i == 0)
        def _():
            for o, a in zip(o_accs, a_out):
                o[...] = a

        @pl.when(i > 0)
        def _():
            for o, a in zip(o_accs, a_out):
                o[...] += a

        if ex is not None:
            pl.when((pl.program_id(0) == ncol - 1) & (pl.program_id(1) == nrow - 1))(wait)

    res = pl.pallas_call(
        body, name=name, grid=(ncol, nrow), in_specs=in_specs + ([] if ex is None else ex.in_specs),
        out_specs=out_specs + ([] if ex is None else ex.out_specs),
        out_shape=out_shape + ([] if ex is None else ex.out_shape),
        scratch_shapes=[] if ex is None else ex.scratch,
        compiler_params=_params(("arbitrary", "arbitrary") if ex is not None else ("parallel", "arbitrary")),
    )(*rows, *vecs, *([] if ex is None else ex.xs))
    if ex is not None:
        return res[:n_or], res[n_or:n_out], list(res[n_out:])
    return res[:n_or], res[n_or:]


def _colsum(t):
    return jnp.sum(t, axis=0, keepdims=True)


def _gelu(t):
    return 0.5 * t * (1.0 + lax.erf(t * math.sqrt(0.5)))


def _gelu_grad(t):
    return 0.5 * (1.0 + lax.erf(t * math.sqrt(0.5))) + t * jnp.exp(-0.5 * t * t) * (1.0 / math.sqrt(2.0 * math.pi))


def _sigmoid(t):
    return 1.0 / (1.0 + jnp.exp(-t))


def _rms_stats(t, width):
    return lax.rsqrt(jnp.sum(t * t, axis=-1, keepdims=True) * (1.0 / width) + EPS)


def _rms_bwd(dn, tn, r, width):
    return r * (dn - tn * (jnp.sum(dn * tn, axis=-1, keepdims=True) * (1.0 / width)))


def _place():
    return lax.axis_index("x"), lax.axis_index("y"), lax.axis_index("c")


class _ChipExchange:
    def __init__(self, xs, gather):
        self.xs, self.gather, self.n = list(xs), gather, len(xs)
        self.in_specs = [pl.BlockSpec(memory_space=pl.ANY)] * self.n
        self.out_specs = [pl.BlockSpec(memory_space=pl.ANY)] * self.n
        self.out_shape = [_sds((4,) + (x.shape if gather else x.shape[1:]), x.dtype) for x in self.xs]
        self.scratch = [pltpu.SemaphoreType.DMA((self.n, 3)), pltpu.SemaphoreType.DMA((self.n, 3))]

    def bind(self, x_refs, out_refs, send_sems, recv_sems):
        x, y, c = _place()
        p = 2 * x + y
        chips = [(1 - x, y), (x, 1 - y), (1 - x, 1 - y)]

        def copy(w, k, outgoing):
            qx, qy = chips[k]
            there = 2 * qx + qy
            if self.gather:
                src = x_refs[w]
            else:
                src = x_refs[w].at[there if outgoing else p]
            return pltpu.make_async_remote_copy(
                src_ref=src, dst_ref=out_refs[w].at[p if outgoing else there], send_sem=send_sems.at[w, k],
                recv_sem=recv_sems.at[w, k], device_id=(qx, qy, c), device_id_type=MESH)

        def start():
            for w in range(self.n):
                for k in range(3):
                    copy(w, k, True).start()

        def wait():
            for w in range(self.n):
                for k in range(3):
                    copy(w, k, False).wait_recv()
            for w in range(self.n):
                for k in range(3):
                    copy(w, k, True).wait_send()

        return start, wait


class _PairExchange:
    def __init__(self, xs, mode):
        self.xs, self.mode, self.n = list(xs), mode, len(xs)
        self.in_specs = [pl.BlockSpec(memory_space=pl.ANY)] * self.n
        self.out_specs = [pl.BlockSpec(memory_space=pl.ANY)] * self.n
        shape = {"halves": lambda s: (4,) + s[2:], "forward": lambda s: s, "gather": lambda s: (2,) + s}[mode]
        self.out_shape = [_sds(shape(x.shape), x.dtype) for x in self.xs]
        self.scratch = [pltpu.SemaphoreType.DMA((self.n, 3)), pltpu.SemaphoreType.DMA((self.n, 3))]

    def bind(self, x_refs, out_refs, send_sems, recv_sems):
        x, y, c = _place()
        chips = [(1 - x, y), (x, 1 - y), (1 - x, 1 - y)]

        def copy(w, src, dst, k):
            return pltpu.make_async_remote_copy(src_ref=src, dst_ref=dst, send_sem=send_sems.at[w, k],
                                                recv_sem=recv_sems.at[w, k], device_id=(x, y, 1 - c),
                                                device_id_type=MESH)

        def start():
            for w, (xr, orf) in enumerate(zip(x_refs, out_refs)):
                if self.mode == "halves":
                    for q in range(4):
                        copy(w, xr.at[q, 1 - c], orf.at[q], 0).start()
                elif self.mode == "forward":
                    for k, (qx, qy) in enumerate(chips):
                        copy(w, xr.at[2 * qx + qy], orf.at[2 * qx + qy], k).start()
                else:
                    copy(w, xr, orf.at[c], 0).start()

        def wait():
            for w, (xr, orf) in enumerate(zip(x_refs, out_refs)):
                if self.mode == "halves":
                    copy(w, orf, orf, 0).wait()
                elif self.mode == "forward":
                    for k, (qx, qy) in enumerate(chips):
                        copy(w, xr.at[2 * qx + qy], orf.at[2 * qx + qy], k).wait()
                else:
                    cp = copy(w, xr, orf.at[1 - c], 0)
                    cp.wait_recv()
                    cp.wait_send()

        return start, wait


def _split_refs(refs, n_in, n_out, ex):
    ne = 0 if ex is None else ex.n
    ins, xin = refs[:n_in], refs[n_in:n_in + ne]
    outs, xout = refs[n_in + ne:n_in + ne + n_out], refs[n_in + ne + n_out:n_in + 2 * ne + n_out]
    rest = refs[n_in + 2 * ne + n_out:]
    if ex is None:
        return ins, outs, rest, None, None
    start, wait = ex.bind(xin, xout, rest[-2], rest[-1])
    return ins, outs, rest[:-2], start, wait


def _attn_fwd(q, k, v, *, heads, tq=512, carry=None):
    N, M = q.shape[0], k.shape[0]
    tq = _tile(N, tq)
    sub = _tile(tq, 256)
    vd = v.shape[1] // heads
    nq = N // tq

    def body(*refs):
        (q_ref, k_ref, v_ref), (o_ref, lse_ref), _, start, wait = _split_refs(refs, 3, 2, carry)
        if carry is not None:
            pl.when((pl.program_id(0) == 0) & (pl.program_id(1) == 0))(start)
        for sb in range(tq // sub):
            rows = pl.ds(sb * sub, sub)
            s = lax.dot_general(q_ref[rows, :], k_ref[...], (((1,), (1,)), ((), ())), preferred_element_type=F32)
            m = jnp.max(s, axis=-1, keepdims=True)
            p = jnp.exp(s - m)
            l = jnp.sum(p, axis=-1, keepdims=True)
            o = jnp.dot(p.astype(BF16), v_ref[...], preferred_element_type=F32) / l
            o_ref[rows, :] = o.astype(o_ref.dtype)
            lse_ref[rows, :] = jnp.broadcast_to(m + jnp.log(l), (sub, vd))
        if carry is not None:
            pl.when((pl.program_id(0) == heads - 1) & (pl.program_id(1) == nq - 1))(wait)

    ex = carry
    res = pl.pallas_call(
        body, name="attn_fwd", grid=(heads, nq),
        in_specs=[pl.BlockSpec((tq, HEAD_PAD), lambda h, i: (i, h)),
                  pl.BlockSpec((M, HEAD_PAD), lambda h, i: (0, h)),
                  pl.BlockSpec((M, vd), lambda h, i: (0, h))] + ([] if ex is None else ex.in_specs),
        out_specs=[pl.BlockSpec((tq, vd), lambda h, i: (i, h)),
                   pl.BlockSpec((tq, vd), lambda h, i: (i, h))] + ([] if ex is None else ex.out_specs),
        out_shape=[_sds((N, heads * vd), BF16), _sds((N, heads * vd), F32)] + ([] if ex is None else ex.out_shape),
        scratch_shapes=[] if ex is None else ex.scratch,
        compiler_params=_params(("arbitrary", "arbitrary")),
    )(q, k, v, *([] if ex is None else ex.xs))
    return res[0], res[1], list(res[2:])


def _attn_bwd(q, k, v, o, lse, do, *, heads, tq=256, carry=None):
    N, M = q.shape[0], k.shape[0]
    tq = _tile(N, tq)
    vd = v.shape[1] // heads
    nq = N // tq
    nch = next(n for n in (2, 3, 1) if (M // LANES) % n == 0)
    mc = M // nch
    nt = (((1,), (1,)), ((), ()))
    tn = (((0,), (0,)), ((), ()))

    def body(*refs):
        (q_ref, k_ref, v_ref, o_ref, lse_ref, do_ref), (dq_ref, dk_ref, dv_ref), _, start, wait = _split_refs(
            refs, 6, 3, carry)
        if carry is not None:
            pl.when((pl.program_id(0) == 0) & (pl.program_id(1) == 0))(start)
        i = pl.program_id(1)
        qb, dob = q_ref[...], do_ref[...]
        lse = lse_ref[...][:, :1]
        delta = jnp.sum(dob.astype(F32) * o_ref[...].astype(F32), axis=-1, keepdims=True)
        dq, parts = None, []
        for ch in range(nch):
            keys = pl.ds(ch * mc, mc)
            kb = k_ref[keys, :]
            s = lax.dot_general(qb, kb, nt, preferred_element_type=F32)
            p = jnp.exp(s - lse)
            dp = lax.dot_general(dob, v_ref[keys, :], nt, preferred_element_type=F32)
            ds = (p * (dp - delta)).astype(BF16)
            dq_c = jnp.dot(ds, kb, preferred_element_type=F32)
            dq = dq_c if dq is None else dq + dq_c
            parts.append((keys, lax.dot_general(ds, qb, tn, preferred_element_type=F32),
                          lax.dot_general(p.astype(BF16), dob, tn, preferred_element_type=F32)))
        dq_ref[...] = dq

        @pl.when(i == 0)
        def _():
            for keys, dk_part, dv_part in parts:
                dk_ref[keys, :] = dk_part
                dv_ref[keys, :] = dv_part

        @pl.when(i > 0)
        def _():
            for keys, dk_part, dv_part in parts:
                dk_ref[keys, :] += dk_part
                dv_ref[keys, :] += dv_part

        if carry is not None:
            pl.when((pl.program_id(0) == heads - 1) & (pl.program_id(1) == nq - 1))(wait)

    ex = carry
    res = pl.pallas_call(
        body, name="attn_bwd", grid=(heads, nq),
        in_specs=[pl.BlockSpec((tq, HEAD_PAD), lambda h, i: (i, h)),
                  pl.BlockSpec((M, HEAD_PAD), lambda h, i: (0, h)),
                  pl.BlockSpec((M, vd), lambda h, i: (0, h)),
                  pl.BlockSpec((tq, vd), lambda h, i: (i, h)),
                  pl.BlockSpec((tq, vd), lambda h, i: (i, h)),
                  pl.BlockSpec((tq, vd), lambda h, i: (i, h))] + ([] if ex is None else ex.in_specs),
        out_specs=[pl.BlockSpec((tq, HEAD_PAD), lambda h, i: (i, h)),
                   pl.BlockSpec((M, HEAD_PAD), lambda h, i: (0, h)),
                   pl.BlockSpec((M, vd), lambda h, i: (0, h))] + ([] if ex is None else ex.out_specs),
        out_shape=[_sds((N, heads * HEAD_PAD), F32), _sds((M, heads * HEAD_PAD), F32),
                   _sds((M, heads * vd), F32)] + ([] if ex is None else ex.out_shape),
        scratch_shapes=[] if ex is None else ex.scratch,
        compiler_params=_params(("arbitrary", "arbitrary")),
    )(q, k, v, o, lse, do, *([] if ex is None else ex.xs))
    return res[0], res[1], res[2], list(res[3:])


def _comm_call(body, xs, out_shapes, n_sems, name, in_vmem):
    space = pltpu.VMEM if in_vmem else pl.ANY
    n = len(xs)

    def wrapped(*refs):
        body(refs[:n], refs[n:2 * n], *refs[2 * n:])

    return pl.pallas_call(
        wrapped, name=name, out_shape=list(out_shapes),
        in_specs=[pl.BlockSpec(memory_space=space)] * n, out_specs=[pl.BlockSpec(memory_space=space)] * n,
        scratch_shapes=[pltpu.SemaphoreType.DMA((n, n_sems)), pltpu.SemaphoreType.DMA((n, n_sems)),
                        pltpu.SemaphoreType.DMA((n,))],
        compiler_params=pltpu.CompilerParams(vmem_limit_bytes=VMEM_LIMIT_BYTES),
    )(*xs)


def _all_gather8(blks, *, name, in_vmem, others_only=False):
    def body(x_refs, out_refs, send_sems, recv_sems, local_sems):
        x, y, c = _place()
        me, sibling = (x, y, c), (x, y, 1 - c)
        chips = [(1 - x, y), (x, 1 - y), (1 - x, 1 - y)]
        waits = []
        for w, (x_ref, out_ref) in enumerate(zip(x_refs, out_refs)):
            def slot(px, py, pc, out_ref=out_ref):
                return out_ref.at[4 * px + 2 * py + pc]

            def copy(k, block, to, src=None, w=w, slot=slot):
                return pltpu.make_async_remote_copy(
                    src_ref=slot(*block) if src is None else src, dst_ref=slot(*block),
                    send_sem=send_sems.at[w, k], recv_sem=recv_sems.at[w, k], device_id=to, device_id_type=MESH)

            mine = None
            first = []
            if not others_only:
                mine = pltpu.make_async_copy(x_ref, slot(*me), local_sems.at[w])
                mine.start()
                first.append(copy(0, me, sibling, src=x_ref))
            first += [copy(1 + j, me, (*chip, c), src=x_ref) for j, chip in enumerate(chips)]
            for cp in first:
                cp.start()
            waits.append((copy, mine, first))
        for copy, mine, first in waits:
            passed = [copy(4 + j, (*chip, c), sibling) for j, chip in enumerate(chips)]
            for j, chip in enumerate(chips):
                copy(1 + j, (*chip, c), me).wait_recv()
                passed[j].start()
            if not others_only:
                copy(0, sibling, me).wait_recv()
            for j, chip in enumerate(chips):
                copy(4 + j, (*chip, 1 - c), me).wait_recv()
            for cp in first + passed:
                cp.wait_send()
            if mine is not None:
                mine.wait()

    return _comm_call(body, blks, [_sds((8,) + b.shape, b.dtype) for b in blks], 7, name, in_vmem)


def _exchange_alone(ex, *, name):
    def body(x_refs, out_refs, send_sems, recv_sems, local_sems):
        start, wait = ex.bind(x_refs, out_refs, send_sems, recv_sems)
        start()
        wait()

    return list(_comm_call(body, ex.xs, ex.out_shape, 3, name, False))


def _block_rows(rows, row_bytes, target=1 << 21, align=BF16_SUBLANES):
    return _tile(rows, max(align, target // row_bytes // align * align), align)


def _sum_blocks(buf, *, name, out_dtype):
    B, R, C = buf.shape
    tm = _block_rows(R, B * C * buf.dtype.itemsize)

    def body(x_ref, o_ref):
        acc = x_ref[0].astype(F32)
        for b in range(1, B):
            acc = acc + x_ref[b].astype(F32)
        o_ref[...] = acc.astype(o_ref.dtype)

    return pl.pallas_call(
        body, name=name, grid=(R // tm,), in_specs=[pl.BlockSpec((B, tm, C), lambda i: (0, i, 0))],
        out_specs=pl.BlockSpec((tm, C), lambda i: (i, 0)), out_shape=_sds((R, C), out_dtype),
        compiler_params=_params(("parallel",)),
    )(buf)


def _pair_add(mine, theirs, core, *, name):
    _, _, R, C = mine.shape
    tm = _block_rows(R, C * 2)

    def body(core_ref, a_ref, b_ref, o_ref):
        o_ref[...] = (a_ref[...].astype(F32) + b_ref[...].astype(F32)).astype(o_ref.dtype)

    return pl.pallas_call(
        body, name=name, out_shape=_sds(theirs.shape, BF16),
        grid_spec=pltpu.PrefetchScalarGridSpec(
            num_scalar_prefetch=1, grid=(4, R // tm),
            in_specs=[pl.BlockSpec((None, None, tm, C), lambda q, i, core_ref: (q, core_ref[0], i, 0)),
                      pl.BlockSpec((None, tm, C), lambda q, i, core_ref: (q, i, 0))],
            out_specs=pl.BlockSpec((None, tm, C), lambda q, i, core_ref: (q, i, 0))),
        compiler_params=_params(("parallel", "parallel")),
    )(core, mine, theirs)


def _assemble(gathered, own, chip, *, name, transpose):
    _, K, Ns = gathered.shape
    tm = _block_rows(K, Ns * 4)

    def body(chip_ref, g_ref, own_ref, o_ref):
        q = pl.program_id(0)

        @pl.when(q == chip_ref[0])
        def _():
            o_ref[...] = own_ref[...].astype(BF16)

        @pl.when(q != chip_ref[0])
        def _():
            o_ref[...] = g_ref[...]

    if transpose:
        out_spec = pl.BlockSpec((tm, Ns), lambda q, i, ch: (i, q))
        out_shape = _sds((K, 4 * Ns), BF16)
    else:
        out_spec = pl.BlockSpec((None, tm, Ns), lambda q, i, ch: (q, i, 0))
        out_shape = _sds((4, K, Ns), BF16)
    return pl.pallas_call(
        body, name=name, out_shape=out_shape,
        grid_spec=pltpu.PrefetchScalarGridSpec(
            num_scalar_prefetch=1, grid=(4, K // tm),
            in_specs=[pl.BlockSpec((None, tm, Ns), lambda q, i, ch: (jnp.where(q == ch[0], (q + 1) % 4, q), i, 0)),
                      pl.BlockSpec((tm, Ns), lambda q, i, ch: (jnp.where(q == ch[0], i, 0), 0))],
            out_specs=out_spec),
        compiler_params=_params(("arbitrary", "arbitrary")),
    )(chip, gathered, own)


def _assemble_halves(mine, theirs, own, place, *, name, transpose):
    _, K2, Ns = mine.shape
    tm = _block_rows(K2, Ns * 4)
    nb = K2 // tm

    def body(place_ref, m_ref, t_ref, own_ref, o_ref):
        q, hb = pl.program_id(0), pl.program_id(1)
        is_own = q == place_ref[0]
        is_mine = hb == place_ref[1]

        @pl.when(is_own)
        def _():
            o_ref[...] = own_ref[...].astype(BF16)

        @pl.when(jnp.logical_not(is_own) & is_mine)
        def _():
            o_ref[...] = m_ref[...]

        @pl.when(jnp.logical_not(is_own) & jnp.logical_not(is_mine))
        def _():
            o_ref[...] = t_ref[...]

    def other(q, pr):
        return jnp.where(q == pr[0], (q + 1) % 4, q)

    if transpose:
        out_spec = pl.BlockSpec((tm, Ns), lambda q, hb, i, pr: (hb * nb + i, q))
        out_shape = _sds((2 * K2, 4 * Ns), BF16)
    else:
        out_spec = pl.BlockSpec((None, tm, Ns), lambda q, hb, i, pr: (q, hb * nb + i, 0))
        out_shape = _sds((4, 2 * K2, Ns), BF16)
    return pl.pallas_call(
        body, name=name, out_shape=out_shape,
        grid_spec=pltpu.PrefetchScalarGridSpec(
            num_scalar_prefetch=1, grid=(4, 2, nb),
            in_specs=[pl.BlockSpec((None, tm, Ns), lambda q, hb, i, pr: (other(q, pr), jnp.where(hb == pr[1], i, 0), 0)),
                      pl.BlockSpec((None, tm, Ns), lambda q, hb, i, pr: (other(q, pr), jnp.where(hb == pr[1], 0, i), 0)),
                      pl.BlockSpec((tm, Ns), lambda q, hb, i, pr: (jnp.where(q == pr[0], hb * nb + i, 0), 0))],
            out_specs=out_spec),
        compiler_params=_params(("arbitrary",) * 3),
    )(place, mine, theirs, own)


def _split_lanes(row, widths):
    out, off = [], 0
    for wd in widths:
        out.append(row[:, off:off + wd])
        off += wd
    return out


def _adamw(w, g, m, v, *, name, carry=None):
    C = w.shape[1]

    def fn(w, g, m, v):
        m = ADAM_B1 * m + (1.0 - ADAM_B1) * g
        v = ADAM_B2 * v + (1.0 - ADAM_B2) * (g * g)
        m_hat = m / (1.0 - ADAM_B1 ** ADAM_STEP)
        v_hat = v / (1.0 - ADAM_B2 ** ADAM_STEP)
        delta = -ADAM_LR * (m_hat / (jnp.sqrt(v_hat) + ADAM_EPS) + ADAM_WD * w)
        return (delta, m, v), ()

    tm = max(8, min(512, (1 << 20) // (4 * C) // 8 * 8))
    res = _rowwise(fn, [w, g, m, v], [], [(C, F32)] * 3, name=name, tm=tm, carry=carry)
    return tuple(res[0]) + ((res[2],) if carry is not None else ())


def _rope_tables(n):
    rows = n // GRID_W
    row = jnp.repeat(jnp.arange(rows, dtype=F32), GRID_W)
    col = jnp.tile(jnp.arange(GRID_W, dtype=F32), rows)
    nf = ROPE_DIM // 4
    freqs = ROPE_THETA ** (-jnp.arange(nf, dtype=F32) / nf)
    ang_r, ang_c = row[:, None] * freqs[None, :], col[:, None] * freqs[None, :]
    cr, sr, cc, sc = jnp.cos(ang_r), jnp.sin(ang_r), jnp.cos(ang_c), jnp.sin(ang_c)
    nope = HEAD_PAD - 2 * ROPE_DIM
    one, zero, z = jnp.ones((n, nope), F32), jnp.zeros((n, nope), F32), jnp.zeros((n, nf), F32)
    pad = jnp.zeros((n, ROPE_DIM), F32)
    cos = jnp.concatenate([one, cr, cr, cc, cc, pad], axis=1)
    s_lo = jnp.concatenate([zero, -sr, z, -sc, z, pad], axis=1)
    s_hi = jnp.concatenate([zero, z, sr, z, sc, pad], axis=1)
    return cos, s_lo, s_hi


def _rope(n, cos, s_lo, s_hi):
    q = ROPE_DIM // 4
    return n * cos + pltpu.roll(n, HEAD_PAD - q, 1) * s_lo + pltpu.roll(n, q, 1) * s_hi


def _rope_t(d, cos, s_lo, s_hi):
    q = ROPE_DIM // 4
    return d * cos + pltpu.roll(d * s_lo, q, 1) + pltpu.roll(d * s_hi, HEAD_PAD - q, 1)


def kernel(x, c, ctx, c_ctx, w_mod, b_mod, norm1_g, w_in, q_norm_g, kv_norm_g, w_uq, w_ukv, qk_norm_q, qk_norm_k, sgu_norm_g, sgu_norm_b, w_spatial, b_spatial, w_br_attn, w_br_sgu, w_out, norm2_g, w_ffn_in, w_ffn_out, loss_target, m_c_ctx, m_w_mod, m_b_mod, m_norm1_g, m_w_in, m_q_norm_g, m_kv_norm_g, m_w_uq, m_w_ukv, m_qk_norm_q, m_qk_norm_k, m_sgu_norm_g, m_sgu_norm_b, m_w_spatial, m_b_spatial, m_w_br_attn, m_w_br_sgu, m_w_out, m_norm2_g, m_w_ffn_in, m_w_ffn_out, v_c_ctx, v_w_mod, v_b_mod, v_norm1_g, v_w_in, v_q_norm_g, v_kv_norm_g, v_w_uq, v_w_ukv, v_qk_norm_q, v_qk_norm_k, v_sgu_norm_g, v_sgu_norm_b, v_w_spatial, v_b_spatial, v_w_br_attn, v_w_br_sgu, v_w_out, v_norm2_g, v_w_ffn_in, v_w_ffn_out):
    ax, ay, ac = _place()
    my_chip = 2 * ax + ay
    my_dev = 4 * ax + 2 * ay + ac

    N, D = x.shape[1], x.shape[2]
    CT = ctx.shape[1]
    M = N + CT
    QL, KVL, QK = q_norm_g.shape[-1], kv_norm_g.shape[-1], qk_norm_q.shape[-1]
    NOPE = QK - ROPE_DIM
    VD = NOPE
    H = 4 * w_uq.shape[-1] // QK
    SW, G, CH = sgu_norm_g.shape[-1], w_spatial.shape[1], w_spatial.shape[2]
    GD = SW // G
    DFF = 4 * w_ffn_out.shape[1]
    NMOD = 4 * w_mod.shape[-1]
    NM = w_mod.shape[-1]
    KVP = KVL + 2 * ROPE_DIM
    assert NOPE == LANES and GD == LANES and HEAD_PAD == NOPE + 2 * ROPE_DIM and CH == LANES
    scale = QK ** -0.5

    x2, ctx2, tgt2 = x[0], ctx[0], loss_target[0]

    c_all = _all_gather8([c], name="ag_c", in_vmem=True)[0][:, 0, :]
    c_rows = jnp.concatenate([c_all, c_ctx[None, :], jnp.zeros((BF16_SUBLANES - 9, D), F32)], axis=0)

    def silu_fn(t):
        s = _sigmoid(t)
        return (t * s, s * (1.0 + t * (1.0 - s))), ()

    (silu_c, dsilu_c), _ = _rowwise(silu_fn, [c_rows], [], [(D, F32), (D, F32)], name="silu_c", tm=16)
    wm = w_mod[0]
    mod_loc = _mm([(silu_c, wm)], name="mod_fwd", outs=(F32,), tn=512, tk=512,
                  extras=[(lax.dynamic_slice_in_dim(b_mod, my_chip * NM, NM, axis=1), "n")],
                  epi=lambda acc, b: (acc + b,))
    mod_all = _all_gather8([mod_loc], name="ag_mod", in_vmem=True)[0]
    mod_full = jnp.concatenate([mod_all[0], mod_all[2], mod_all[4], mod_all[6]], axis=1)
    mod_me = lax.dynamic_slice_in_dim(mod_full, my_dev, 1, axis=0)
    sh1, sc1, g1, sh2, sc2, g2 = [mod_me[:, i * D:(i + 1) * D] for i in range(6)]
    sh1c, sc1c = mod_full[8:9, :D], mod_full[8:9, D:2 * D]

    big = [w_in[0], w_uq[0], w_ukv[0], w_br_attn[0], w_br_sgu[0], w_out[0], w_ffn_in[0], w_ffn_out[0]]
    col_sharded = [True, True, True, True, True, False, True, False]
    halves = [lax.dynamic_slice_in_dim(a, ac * (a.shape[0] // 2), a.shape[0] // 2, axis=0).astype(BF16) for a in big]
    tags = ["w_in", "w_uq", "w_ukv", "w_br_attn", "w_br_sgu", "w_out", "w_ffn_in", "w_ffn_out"]
    first_group, attn_group, ffn_group = [0, 1, 2], [3, 4, 5, 6], [7]
    chip1 = jnp.reshape(my_chip, (1,)).astype(jnp.int32)
    place2 = jnp.stack([my_chip, ac]).astype(jnp.int32)

    def laid_out(seg, i):
        a = big[i]
        if col_sharded[i] and seg.ndim == 3:
            return seg.transpose(1, 0, 2).reshape(a.shape[0], 4 * a.shape[1])
        return seg if col_sharded[i] else seg.reshape(4 * a.shape[0], a.shape[1])

    def side_by_side(i):
        return col_sharded[i] and big[i].shape[1] % LANES == 0

    def finish_gather(idx, mine4, theirs4):
        return [laid_out(_assemble_halves(m, t, big[i], place2, name="assemble_" + tags[i], transpose=side_by_side(i)), i)
                for i, m, t in zip(idx, mine4, theirs4)]

    gathered = _all_gather8([halves[i] for i in first_group], name="ag_weights", in_vmem=False, others_only=True)
    w_in_f, w_uq_f, w_ukv_f = [
        laid_out(_assemble(seg.reshape((4,) + big[i].shape), big[i], chip1, name="assemble_" + tags[i],
                           transpose=side_by_side(i)), i) for i, seg in zip(first_group, gathered)]
    o_kv, o_u = QL, QL + KVL + ROPE_DIM
    o_v, o_g = o_u + SW, o_u + 2 * SW
    w_q = w_in_f[:, :QL]
    w_kv = jnp.pad(w_in_f[:, o_kv:o_u], ((0, 0), (0, ROPE_DIM)))
    w_u, w_v = w_in_f[:, o_u:o_v], w_in_f[:, o_v:o_g]
    w_g1, w_g2 = w_in_f[:, o_g:o_g + D], w_in_f[:, o_g + D:]
    w_uq_p = jnp.pad(w_uq_f.reshape(QL, H, QK), ((0, 0), (0, 0), (0, HEAD_PAD - QK))).reshape(QL, H * HEAD_PAD)

    cos_t, slo_t, shi_t = _rope_tables(N)
    ones_c = jnp.concatenate([jnp.ones((CT, NOPE + ROPE_DIM), F32), jnp.zeros((CT, ROPE_DIM), F32)], axis=1)
    cos_k = jnp.concatenate([cos_t, ones_c], axis=0)
    slo_k = jnp.concatenate([slo_t, jnp.zeros((CT, HEAD_PAD), F32)], axis=0)
    shi_k = jnp.concatenate([shi_t, jnp.zeros((CT, HEAD_PAD), F32)], axis=0)
    gq_p = jnp.pad(qk_norm_q, ((0, 0), (0, HEAD_PAD - QK)))
    gk_p = jnp.pad(qk_norm_k, ((0, 0), (0, HEAD_PAD - QK)))

    def norm_mod_fn(t, g, sh, sc):
        r = _rms_stats(t, D)
        return (((t * r) * g) * (1.0 + sc) + sh,), ()

    (h,), _ = _rowwise(norm_mod_fn, [x2], [norm1_g, sh1, sc1], [(D, BF16)], name="norm1_x")
    (ctx_h,), _ = _rowwise(norm_mod_fn, [ctx2], [norm1_g, sh1c, sc1c], [(D, BF16)], name="norm1_ctx")

    qc = _mm([(h, w_q)], name="proj_q", outs=(F32,))
    kvin = jnp.concatenate([_mm([(h, w_kv)], name="proj_kv", outs=(F32,)),
                            _mm([(ctx_h, w_kv)], name="proj_kv_ctx", outs=(F32,))], axis=0)
    u_in = _mm([(h, w_u)], name="proj_u", outs=(BF16,))
    v_in = _mm([(h, w_v)], name="proj_v", outs=(BF16,))
    g1_in = _mm([(h, w_g1)], name="proj_g1", outs=(BF16,))
    g2_in = _mm([(h, w_g2)], name="proj_g2", outs=(BF16,))

    def rms_gain_fn(width):
        def fn(t, g):
            return (((t * _rms_stats(t, width)) * g),), ()
        return fn

    (qn,), _ = _rowwise(rms_gain_fn(QL), [qc], [q_norm_g], [(QL, BF16)], name="q_norm")

    def kv_norm_fn(t, g):
        kvc = t[:, :KVL]
        return (((kvc * _rms_stats(kvc, KVL)) * g),), ()

    (kvn,), _ = _rowwise(kv_norm_fn, [kvin], [kv_norm_g], [(KVL, BF16)], name="kv_norm")
    q_raw = _mm([(qn, w_uq_p)], name="q_up", outs=(F32,))
    kv_raw = _mm([(kvn, w_ukv_f)], name="kv_up", outs=(F32,))

    def q_post_fn(t, cos, slo, shi, g):
        outs = []
        for hd in range(H):
            th = t[:, hd * HEAD_PAD:(hd + 1) * HEAD_PAD]
            outs.append(_rope((th * _rms_stats(th, QK)) * g, cos, slo, shi) * scale)
        return (jnp.concatenate(outs, axis=1),), ()

    (q_att,), _ = _rowwise(q_post_fn, [q_raw, cos_t, slo_t, shi_t], [gq_p], [(H * HEAD_PAD, BF16)], name="q_post")

    def k_post_fn(t, kvi, cos, slo, shi, g):
        kr = kvi[:, KVL:]
        ks, vs = [], []
        for hd in range(H):
            th = jnp.concatenate([t[:, hd * HEAD_PAD:hd * HEAD_PAD + NOPE], kr], axis=1)
            ks.append(_rope((th * _rms_stats(th, QK)) * g, cos, slo, shi))
            vs.append(t[:, hd * HEAD_PAD + NOPE:(hd + 1) * HEAD_PAD])
        return (jnp.concatenate(ks, axis=1), jnp.concatenate(vs, axis=1)), ()

    (k_att, v_att), _ = _rowwise(k_post_fn, [kv_raw, kvin, cos_k, slo_k, shi_k], [gk_p],
                                 [(H * HEAD_PAD, BF16), (H * VD, BF16)], name="k_post")
    attn_o, lse, mine4 = _attn_fwd(q_att, k_att, v_att, heads=H,
                                   carry=_ChipExchange([halves[i] for i in attn_group], gather=True))

    ws3 = w_spatial[0]
    bs_t = jnp.pad(b_spatial[0].T, ((0, 0), (0, LANES - G)))

    def sgu_parts(u_in, v_in, ng, nb):
        u, v = _gelu(u_in.astype(F32)), _gelu(v_in.astype(F32))
        mu = jnp.mean(v, axis=-1, keepdims=True)
        vc = v - mu
        rs = lax.rsqrt(jnp.mean(vc * vc, axis=-1, keepdims=True) + EPS)
        xhat = vc * rs
        return u, xhat, rs, (xhat * ng + nb).astype(BF16)

    def sgu_fwd_fn(u_in, v_in, ng, nb, ws, bst):
        u, _, _, vnb = sgu_parts(u_in, v_in, ng, nb)
        outs = []
        for g in range(G):
            sl = slice(g * GD, (g + 1) * GD)
            mixed = jnp.dot(ws[g].astype(BF16), vnb[:, sl], preferred_element_type=F32) + bst[:, g:g + 1]
            outs.append(u[:, sl] * mixed)
        return (jnp.concatenate(outs, axis=1),), ()

    (sgu_o,), _, theirs4 = _rowwise(sgu_fwd_fn, [u_in, v_in], [sgu_norm_g, sgu_norm_b, ws3, bs_t], [(SW, BF16)],
                                    name="sgu_fwd", tm=CH, carry=_PairExchange(mine4, "forward"))
    w_bra, w_brs, w_out_f, w_ffi = finish_gather(attn_group, mine4, theirs4)
    w_fa, w_fb = w_ffi[:, :DFF], w_ffi[:, DFF:]

    a1 = _mm([(attn_o, w_bra)], name="br_attn", outs=(BF16,))
    a2 = _mm([(sgu_o, w_brs)], name="br_sgu", outs=(BF16,))

    def merge_fn(a1, a2, gi1, gi2):
        return ((_sigmoid(gi1.astype(F32)) * a1.astype(F32) + _sigmoid(gi2.astype(F32)) * a2.astype(F32)),), ()

    (merged,), _ = _rowwise(merge_fn, [a1, a2, g1_in, g2_in], [], [(D, BF16)], name="merge", tc=1024)

    def res_gate(acc, res, gate):
        return res + gate * acc, acc

    x1, mo = _mm([(merged, w_out_f)], name="out_proj", outs=(F32, BF16), tn=1024,
                 extras=[(x2, "mn"), (g1, "n")], epi=res_gate)
    (h2,), _ = _rowwise(norm_mod_fn, [x1], [norm2_g, sh2, sc2], [(D, BF16)], name="norm2")

    def swiglu_epi(a, b):
        return a, b, (a * _sigmoid(a)) * b

    (fa, fb, act), mine4 = _mm([(h2, w_fa, w_fb)], name="ffn_in", outs=(BF16, BF16, BF16), tn=512, epi=swiglu_epi,
                               carry=_ChipExchange([halves[i] for i in ffn_group], gather=True))
    (w_ffo,) = finish_gather(ffn_group, mine4, _exchange_alone(_PairExchange(mine4, "forward"), name="ag_forward_ffn"))
    y, f_out = _mm([(act, w_ffo)], name="ffn_out", outs=(F32, BF16), tn=1024,
                   extras=[(x1, "mn"), (g2, "n")], epi=res_gate)

    def loss_fn(y, t, fo, g2v):
        e = y - t
        dy = e * (1.0 / D)
        return (dy, g2v * dy), (_colsum(e * e) * (0.5 / D), _colsum(dy * fo.astype(F32)))

    (dy, df), (loss_cols, dg2) = _rowwise(loss_fn, [y, tgt2, f_out], [g2], [(D, F32), (D, BF16)],
                                          [(1, D), (1, D)], name="loss")

    def swiglu_bwd_epi(dact, a, b):
        a, b = a.astype(F32), b.astype(F32)
        s = _sigmoid(a)
        return dact * b * (s * (1.0 + a * (1.0 - s))), dact * (a * s)

    da, db = _mm([(df, w_ffo)], tb=True, name="ffn_out_dx", outs=(BF16, BF16), tn=512,
                 extras=[(fa, "mn"), (fb, "mn")], epi=swiglu_bwd_epi)
    dw_ffo = _mm([(act, df)], ta=True, name="ffn_out_dw", outs=(BF16,))
    dh2 = _mm([(da, w_fa), (db, w_fb)], tb=True, name="ffn_in_dx", outs=(F32,))
    ns_ffi = w_ffn_in.shape[-1]
    dw_ffi = _mm([(h2, da)], ta=True, name="ffn_in_dw_a", outs=(BF16,), tn=1408, split=ns_ffi,
                 into=(lax.empty((4, D, ns_ffi), BF16), 0))
    dw_ffi = _mm([(h2, db)], ta=True, name="ffn_in_dw_b", outs=(BF16,), tn=1408, split=ns_ffi, into=(dw_ffi, 2))

    def norm2_bwd_fn(dh, t, dyv, mov, g, sc, g1v):
        r = _rms_stats(t, D)
        tn = t * r
        dxg = dh * (1.0 + sc)
        dt = dyv + _rms_bwd(dxg * g, tn, r, D)
        return (dt, g1v * dt), (_colsum(dh), _colsum(dh * (tn * g)), _colsum(dxg * tn), _colsum(dt * mov.astype(F32)))

    (dx1, dmo), (dsh2, dsc2, dn2g, dg1) = _rowwise(
        norm2_bwd_fn, [dh2, x1, dy, mo], [norm2_g, sc2, g1], [(D, F32), (D, BF16)], [(1, D)] * 4, name="norm2_bwd")

    def merge_bwd_epi(dm, a1, a2, gi1, gi2):
        s1, s2 = _sigmoid(gi1.astype(F32)), _sigmoid(gi2.astype(F32))
        a1, a2 = a1.astype(F32), a2.astype(F32)
        return dm * s1, dm * s2, dm * a1 * (s1 * (1.0 - s1)), dm * a2 * (s2 * (1.0 - s2))

    da1, da2, dgi1, dgi2 = _mm([(dmo, w_out_f)], tb=True, name="out_proj_dx", outs=(BF16,) * 4, tn=512,
                               extras=[(a1, "mn"), (a2, "mn"), (g1_in, "mn"), (g2_in, "mn")], epi=merge_bwd_epi)
    dw_out = _mm([(merged, dmo)], ta=True, name="out_proj_dw", outs=(BF16,))
    dattn = _mm([(da1, w_bra)], tb=True, name="br_attn_dx", outs=(BF16,))
    dw_bra = _mm([(attn_o, da1)], ta=True, name="br_attn_dw", outs=(BF16,), split=w_br_attn.shape[-1])
    dsgu = _mm([(da2, w_brs)], tb=True, name="br_sgu_dx", outs=(BF16,))
    dw_brs = _mm([(sgu_o, da2)], ta=True, name="br_sgu_dw", outs=(BF16,), split=w_br_sgu.shape[-1])

    def sgu_bwd_fn(dso, u_in, v_in, ng, nb, ws, bst):
        u, xhat, rs, vnb = sgu_parts(u_in, v_in, ng, nb)
        dso = dso.astype(F32)
        lane = lax.broadcasted_iota(jnp.int32, (CH, LANES), 1)
        du, dvn, dws, dbs = [], [], [], jnp.zeros((CH, LANES), F32)
        for g in range(G):
            sl = slice(g * GD, (g + 1) * GD)
            wg = ws[g].astype(BF16)
            mixed = jnp.dot(wg, vnb[:, sl], preferred_element_type=F32) + bst[:, g:g + 1]
            du.append(dso[:, sl] * mixed)
            dmix = dso[:, sl] * u[:, sl]
            dmb = dmix.astype(BF16)
            dws.append(lax.dot_general(dmb, vnb[:, sl], (((1,), (1,)), ((), ())), preferred_element_type=F32))
            dbs = dbs + jnp.where(lane == g, jnp.sum(dmix, axis=1, keepdims=True), 0.0)
            dvn.append(lax.dot_general(wg, dmb, (((0,), (0,)), ((), ())), preferred_element_type=F32))
        du, dvn = jnp.concatenate(du, axis=1), jnp.concatenate(dvn, axis=1)
        dxh = dvn * ng
        dv = rs * (dxh - jnp.mean(dxh, axis=-1, keepdims=True) - xhat * jnp.mean(dxh * xhat, axis=-1, keepdims=True))
        return ((du * _gelu_grad(u_in.astype(F32)), dv * _gelu_grad(v_in.astype(F32))),
                (_colsum(dvn * xhat), _colsum(dvn), jnp.stack(dws), dbs))

    core = jnp.reshape(ac, (1,)).astype(jnp.int32)

    def dest_layout(dwf, i):
        K, Ns = big[i].shape
        if dwf.ndim == 2:
            dwf = dwf.reshape(K, 4, Ns).transpose(1, 0, 2) if col_sharded[i] else dwf.reshape(4, K, Ns)
        return dwf.reshape(4, 2, K // 2, Ns)

    def pair_sums(idx, g4, sib):
        return [_pair_add(g, s, core, name="rs_pair_add_" + tags[i]) for g, s, i in zip(g4, sib, idx)]

    early = [3, 4, 5, 6, 7]
    g4_early = [dest_layout(d, i) for d, i in zip([dw_bra, dw_brs, dw_out, dw_ffi, dw_ffo], early)]
    (du_in, dv_in), (d_sng, d_snb, d_ws, d_bs), sib_early = _rowwise(
        sgu_bwd_fn, [dsgu, u_in, v_in], [sgu_norm_g, sgu_norm_b, ws3, bs_t], [(SW, BF16), (SW, BF16)],
        [(1, SW), (1, SW), (G, CH, CH), (CH, LANES)], name="sgu_bwd", tm=CH, carry=_PairExchange(g4_early, "halves"))
    pair_early = pair_sums(early, g4_early, sib_early)
    dq_att, dk_att, dv_att, xchg_early = _attn_bwd(q_att, k_att, v_att, attn_o, lse, dattn, heads=H,
                                                   carry=_ChipExchange(pair_early, gather=False))

    def q_post_bwd_fn(dq, t, cos, slo, shi, g):
        outs, dg = [], jnp.zeros((1, HEAD_PAD), F32)
        for hd in range(H):
            sl = slice(hd * HEAD_PAD, (hd + 1) * HEAD_PAD)
            th = t[:, sl]
            r = _rms_stats(th, QK)
            tn = th * r
            dn = _rope_t(dq[:, sl] * scale, cos, slo, shi)
            dg = dg + _colsum(dn * tn)
            outs.append(_rms_bwd(dn * g, tn, r, QK))
        return (jnp.concatenate(outs, axis=1),), (dg,)

    (dq_raw,), (d_gq,) = _rowwise(q_post_bwd_fn, [dq_att, q_raw, cos_t, slo_t, shi_t], [gq_p],
                                  [(H * HEAD_PAD, BF16)], [(1, HEAD_PAD)], name="q_post_bwd")

    def k_post_bwd_fn(dk, dv, t, kvi, cos, slo, shi, g):
        kr = kvi[:, KVL:]
        outs, dg, dkr = [], jnp.zeros((1, HEAD_PAD), F32), jnp.zeros_like(kr)
        for hd in range(H):
            th = jnp.concatenate([t[:, hd * HEAD_PAD:hd * HEAD_PAD + NOPE], kr], axis=1)
            r = _rms_stats(th, QK)
            tn = th * r
            dn = _rope_t(dk[:, hd * HEAD_PAD:(hd + 1) * HEAD_PAD], cos, slo, shi)
            dg = dg + _colsum(dn * tn)
            dt = _rms_bwd(dn * g, tn, r, QK)
            dkr = dkr + dt[:, NOPE:]
            outs += [dt[:, :NOPE], dv[:, hd * VD:(hd + 1) * VD]]
        return (jnp.concatenate(outs, axis=1), dkr), (dg,)

    (dkv_raw, dkrope), (d_gk,) = _rowwise(
        k_post_bwd_fn, [dk_att, dv_att, kv_raw, kvin, cos_k, slo_k, shi_k], [gk_p],
        [(H * HEAD_PAD, BF16), (2 * ROPE_DIM, F32)], [(1, HEAD_PAD)], name="k_post_bwd")

    dqn = _mm([(dq_raw, w_uq_p)], tb=True, name="q_up_dx", outs=(F32,))
    dw_uq_p = _mm([(qn, dq_raw)], ta=True, name="q_up_dw", outs=(BF16,))
    dkvn = _mm([(dkv_raw, w_ukv_f)], tb=True, name="kv_up_dx", outs=(F32,))
    dw_ukv = _mm([(kvn, dkv_raw)], ta=True, name="kv_up_dw", outs=(BF16,))

    def q_norm_bwd_fn(dn, t, g):
        r = _rms_stats(t, QL)
        tn = t * r
        return (_rms_bwd(dn * g, tn, r, QL),), (_colsum(dn * tn),)

    (dqc,), (d_qng,) = _rowwise(q_norm_bwd_fn, [dqn, qc], [q_norm_g], [(QL, BF16)], [(1, QL)], name="q_norm_bwd")

    def kv_norm_bwd_fn(dn, dkr, t, g):
        kvc = t[:, :KVL]
        r = _rms_stats(kvc, KVL)
        tn = kvc * r
        return (jnp.concatenate([_rms_bwd(dn * g, tn, r, KVL), dkr], axis=1),), (_colsum(dn * tn),)

    (dkvin,), (d_kvng,) = _rowwise(kv_norm_bwd_fn, [dkvn, dkrope, kvin], [kv_norm_g], [(KVP, BF16)], [(1, KVL)],
                                   name="kv_norm_bwd")
    dkvin_x, dkvin_c = dkvin[:N], dkvin[N:]

    dctx_h = _mm([(dkvin_c, w_kv)], tb=True, name="proj_kv_ctx_dx", outs=(F32,))
    dw_q = _mm([(h, dqc)], ta=True, name="proj_q_dw", outs=(BF16,))
    dw_kv = _mm([(h, dkvin_x), (ctx_h, dkvin_c)], ta=True, name="proj_kv_dw", outs=(BF16,))
    dw_u = _mm([(h, du_in)], ta=True, name="proj_u_dw", outs=(BF16,))
    dw_v = _mm([(h, dv_in)], ta=True, name="proj_v_dw", outs=(BF16,))
    dw_g1 = _mm([(h, dgi1)], ta=True, name="proj_g1_dw", outs=(BF16,))
    dw_g2 = _mm([(h, dgi2)], ta=True, name="proj_g2_dw", outs=(BF16,))

    dw_in_f = jnp.concatenate([dw_q, dw_kv[:, :KVL + ROPE_DIM], dw_u, dw_v, dw_g1, dw_g2], axis=1)
    dw_uq_f = dw_uq_p.reshape(QL, H, HEAD_PAD)[:, :, :QK].reshape(QL, H * QK)
    late = [0, 1, 2]
    g4_late = [dest_layout(d, i) for d, i in zip([dw_in_f, dw_uq_f, dw_ukv], late)]
    pair_late = pair_sums(late, g4_late, _exchange_alone(_PairExchange(g4_late, "halves"), name="rs_pair_late"))
    dh, xchg_late = _mm([(dqc, w_q), (dkvin_x, w_kv), (du_in, w_u), (dv_in, w_v), (dgi1, w_g1), (dgi2, w_g2)],
                        tb=True, name="proj_dx", outs=(F32,), tn=512, tk=1024,
                        carry=_ChipExchange(pair_late, gather=False))

    def norm1_bwd_fn(dhv, t, dres, g, sc):
        r = _rms_stats(t, D)
        tn = t * r
        dxg = dhv * (1.0 + sc)
        return (dres + _rms_bwd(dxg * g, tn, r, D),), (_colsum(dhv), _colsum(dhv * (tn * g)), _colsum(dxg * tn))

    (grad_x,), (dsh1, dsc1, dn1g_x) = _rowwise(norm1_bwd_fn, [dh, x2, dx1], [norm1_g, sc1], [(D, F32)], [(1, D)] * 3,
                                               name="norm1_bwd")
    _, (dsh1c, dsc1c, dn1g_c) = _rowwise(norm1_bwd_fn, [dctx_h, ctx2, jnp.zeros_like(ctx2)], [norm1_g, sc1c],
                                         [(D, F32)], [(1, D)] * 3, name="norm1_ctx_bwd")

    small = [dsh1, dsc1, dg1, dsh2, dsc2, dg2,
             dsh1c, dsc1c, dn1g_x, dn1g_c, d_qng, d_kvng, d_gq, d_gk, d_sng, d_snb, dn2g, loss_cols]
    small_sizes = [a.shape[1] for a in small]
    sm_row = jnp.concatenate(small, axis=1)
    sm_mat = jnp.concatenate([d_ws.reshape(G * CH, CH), d_bs], axis=0)
    row_all, mat_all = _all_gather8([sm_row, sm_mat], name="ag_small", in_vmem=True)
    row_sum = _sum_blocks(row_all, name="sum_small_rows", out_dtype=F32)
    mat_sum = _sum_blocks(mat_all, name="sum_small_mats", out_dtype=F32)
    dmod_rows = row_all[:, 0, :NMOD]
    (_, _, _, _, _, _, t_sh1c, t_sc1c, t_n1x, t_n1c, g_qng, g_kvng, t_gq, t_gk, g_sng, g_snb, g_n2g,
     t_loss) = _split_lanes(row_sum, small_sizes)
    g_ws, t_bs = mat_sum[:G * CH], mat_sum[G * CH:]
    dmodc_row = jnp.concatenate([t_sh1c, t_sc1c, jnp.zeros((1, NMOD - 2 * D), F32)], axis=1)
    dmod16 = jnp.concatenate([dmod_rows, dmodc_row, jnp.zeros((BF16_SUBLANES - 9, NMOD), F32)], axis=0)

    def small_fn(rows, n1x, n1c, lossv):
        return (), (_colsum(rows), n1x + n1c, jnp.sum(lossv, axis=1, keepdims=True))

    _, (g_bmod, g_n1g, loss11) = _rowwise(small_fn, [dmod16], [t_n1x, t_n1c, t_loss], [], [(1, NMOD), (1, D), (1, 1)],
                                          name="small_reduce", tm=16)
    dmod_loc = lax.dynamic_slice_in_dim(dmod16, my_chip * NM, NM, axis=1)
    g_wmod = _mm([(silu_c, dmod_loc)], ta=True, name="mod_dw", outs=(F32,), tn=512)
    dsilu_part = _mm([(dmod_loc, wm)], tb=True, name="mod_dx", outs=(F32,), tk=512)
    part_all = _all_gather8([dsilu_part[8:9]], name="ag_cctx", in_vmem=True)[0]

    def cctx_fn(parts, dsl):
        return (), ((parts[0:1] + parts[2:3] + parts[4:5] + parts[6:7]) * dsl,)

    _, (g_cctx,) = _rowwise(cctx_fn, [part_all[:, 0, :]], [dsilu_c[8:9]], [], [(1, D)], name="cctx_grad", tm=8)

    pair = pair_late + pair_early
    xchg = [lax.dynamic_update_index_in_dim(t4, lax.dynamic_index_in_dim(pr, my_chip, 0, keepdims=False), my_chip, 0)
            for t4, pr in zip(xchg_late + xchg_early, pair)]
    red_half = [_sum_blocks(t4, name="rs_sum_" + t, out_dtype=F32) for t4, t in zip(xchg, tags)]
    mod_upd = _adamw(w_mod[0], g_wmod, m_w_mod[0], v_w_mod[0], name="adamw_w_mod",
                     carry=_PairExchange(red_half, "gather"))
    big_grads = [lax.dynamic_update_index_in_dim(r, mine, ac, 0).reshape(a.shape)
                 for r, mine, a in zip(mod_upd[3], red_half, big)]

    def upd(w, g, m, v, nm):
        shape = w.shape
        w2, g2_, m2, v2 = [t.reshape(-1, shape[-1]) for t in (w, g, m, v)]
        d_, m_, v_ = _adamw(w2, g2_, m2, v2, name="adamw_" + nm)
        return g.reshape(shape), d_.reshape(shape), m_.reshape(shape), v_.reshape(shape)

    g_in, g_uq, g_ukv, g_bra, g_brs, g_out, g_ffi, g_ffo = big_grads
    grads = dict(
        c_ctx=g_cctx.reshape(D), w_mod=g_wmod[None], b_mod=g_bmod, norm1_g=g_n1g, w_in=g_in[None],
        q_norm_g=g_qng, kv_norm_g=g_kvng, w_uq=g_uq[None], w_ukv=g_ukv[None],
        qk_norm_q=t_gq[:, :QK], qk_norm_k=t_gk[:, :QK], sgu_norm_g=g_sng, sgu_norm_b=g_snb,
        w_spatial=g_ws.reshape(w_spatial.shape), b_spatial=t_bs[:, :G].T[None],
        w_br_attn=g_bra[None], w_br_sgu=g_brs[None], w_out=g_out[None], norm2_g=g_n2g,
        w_ffn_in=g_ffi[None], w_ffn_out=g_ffo[None])
    weights = dict(c_ctx=c_ctx, w_mod=w_mod, b_mod=b_mod, norm1_g=norm1_g, w_in=w_in, q_norm_g=q_norm_g,
                   kv_norm_g=kv_norm_g, w_uq=w_uq, w_ukv=w_ukv, qk_norm_q=qk_norm_q, qk_norm_k=qk_norm_k,
                   sgu_norm_g=sgu_norm_g, sgu_norm_b=sgu_norm_b, w_spatial=w_spatial, b_spatial=b_spatial,
                   w_br_attn=w_br_attn, w_br_sgu=w_br_sgu, w_out=w_out, norm2_g=norm2_g, w_ffn_in=w_ffn_in,
                   w_ffn_out=w_ffn_out)
    m_in = dict(c_ctx=m_c_ctx, w_mod=m_w_mod, b_mod=m_b_mod, norm1_g=m_norm1_g, w_in=m_w_in, q_norm_g=m_q_norm_g,
                kv_norm_g=m_kv_norm_g, w_uq=m_w_uq, w_ukv=m_w_ukv, qk_norm_q=m_qk_norm_q, qk_norm_k=m_qk_norm_k,
                sgu_norm_g=m_sgu_norm_g, sgu_norm_b=m_sgu_norm_b, w_spatial=m_w_spatial, b_spatial=m_b_spatial,
                w_br_attn=m_w_br_attn, w_br_sgu=m_w_br_sgu, w_out=m_w_out, norm2_g=m_norm2_g, w_ffn_in=m_w_ffn_in,
                w_ffn_out=m_w_ffn_out)
    v_in_ = dict(c_ctx=v_c_ctx, w_mod=v_w_mod, b_mod=v_b_mod, norm1_g=v_norm1_g, w_in=v_w_in, q_norm_g=v_q_norm_g,
                 kv_norm_g=v_kv_norm_g, w_uq=v_w_uq, w_ukv=v_w_ukv, qk_norm_q=v_qk_norm_q, qk_norm_k=v_qk_norm_k,
                 sgu_norm_g=v_sgu_norm_g, sgu_norm_b=v_sgu_norm_b, w_spatial=v_w_spatial, b_spatial=v_b_spatial,
                 w_br_attn=v_w_br_attn, w_br_sgu=v_w_br_sgu, w_out=v_w_out, norm2_g=v_norm2_g, w_ffn_in=v_w_ffn_in,
                 w_ffn_out=v_w_ffn_out)
    names = list(weights)
    big_names = ("w_mod", "w_in", "w_uq", "w_ukv", "w_br_attn", "w_br_sgu", "w_out", "w_ffn_in", "w_ffn_out")
    out_g, out_d, out_m, out_v = {}, {}, {}, {}
    out_g["w_mod"] = grads["w_mod"]
    out_d["w_mod"], out_m["w_mod"], out_v["w_mod"] = [t[None] for t in mod_upd[:3]]
    for nm in big_names[1:]:
        out_g[nm], out_d[nm], out_m[nm], out_v[nm] = upd(weights[nm], grads[nm], m_in[nm], v_in_[nm], nm)
    row_names = [nm for nm in names if nm not in big_names and nm not in ("w_spatial", "b_spatial")]
    widths = [-(-weights[nm].size // LANES) * LANES for nm in row_names]

    def as_row(d):
        return jnp.concatenate([jnp.pad(d[nm].reshape(1, -1), ((0, 0), (0, wd - d[nm].size)))
                                for nm, wd in zip(row_names, widths)], axis=1)

    def as_mat(d):
        return jnp.concatenate([d["w_spatial"].reshape(G * CH, CH), d["b_spatial"].reshape(G, CH)], axis=0)

    row_res = _adamw(as_row(weights), as_row(grads), as_row(m_in), as_row(v_in_), name="adamw_rows")
    mat_res = _adamw(as_mat(weights), as_mat(grads), as_mat(m_in), as_mat(v_in_), name="adamw_spatial")
    for tgt, row, mat in zip((out_d, out_m, out_v), row_res, mat_res):
        for nm, seg in zip(row_names, _split_lanes(row, widths)):
            tgt[nm] = seg[:, :weights[nm].size].reshape(weights[nm].shape)
        tgt["w_spatial"] = mat[:G * CH].reshape(w_spatial.shape)
        tgt["b_spatial"] = mat[G * CH:].reshape(b_spatial.shape)
    for nm in row_names + ["w_spatial", "b_spatial"]:
        out_g[nm] = grads[nm].reshape(weights[nm].shape)

    loss = loss11.reshape(())
    return (loss, grad_x[None], *[out_g[n] for n in names], *[out_d[n] for n in names],
            *[out_m[n] for n in names], *[out_v[n] for n in names])
```

```python
import math

import jax
import jax.numpy as jnp
from jax import lax
from jax.experimental import pallas as pl
from jax.experimental.pallas import tpu as pltpu

F32, BF16 = jnp.float32, jnp.bfloat16
MESH = pl.DeviceIdType.MESH

LANES = 128
BF16_SUBLANES = 16
VMEM_LIMIT_BYTES = 56 * 1024 * 1024

EPS = 1e-6
ROPE_DIM = 64
ROPE_THETA = 10000.0
GRID_W = 64
HEAD_PAD = 256
ADAM_LR, ADAM_B1, ADAM_B2, ADAM_EPS, ADAM_WD, ADAM_STEP = 0.001, 0.9, 0.999, 1e-08, 0.01, 10


def _tile(dim, pref, align=LANES):
    if dim <= pref:
        return dim
    t = (pref // align) * align
    while t >= align:
        if dim % t == 0:
            return t
        t -= align
    return dim


def _params(sem=None):
    return pltpu.CompilerParams(dimension_semantics=sem, vmem_limit_bytes=VMEM_LIMIT_BYTES)


def _sds(shape, dtype):
    return jax.ShapeDtypeStruct(tuple(shape), dtype)


def _mm(pairs, *, name, ta=False, tb=False, outs=(F32,), tm=1024, tn=1024, tk=2048, extras=(), epi=None,
        split=None, into=None, carry=None, col_sums=0):
    dual = len(pairs[0]) == 3
    a0, b0 = pairs[0][0], pairs[0][1]
    M = a0.shape[1] if ta else a0.shape[0]
    N = b0.shape[0] if tb else b0.shape[1]
    tm, tn = _tile(M, tm), _tile(N if split is None else split, tn)
    ks = [(p[0].shape[0] if ta else p[0].shape[1]) for p in pairs]
    tks = [_tile(k, tk) for k in ks]
    nks = [k // t for k, t in zip(ks, tks)]
    offs = [sum(nks[:i]) for i in range(len(pairs))]
    nk_total = sum(nks)
    single = len(pairs) == 1

    def kidx(kk, p):
        return kk if single else jnp.clip(kk - offs[p], 0, nks[p] - 1)

    in_specs, operands = [], []
    for p, pr in enumerate(pairs):
        if ta:
            in_specs.append(pl.BlockSpec((tks[p], tm), lambda i, j, kk, p=p: (kidx(kk, p), i)))
        else:
            in_specs.append(pl.BlockSpec((tm, tks[p]), lambda i, j, kk, p=p: (i, kidx(kk, p))))
        operands.append(pr[0])
        for b in pr[1:]:
            if tb:
                in_specs.append(pl.BlockSpec((tn, tks[p]), lambda i, j, kk, p=p: (j, kidx(kk, p))))
            else:
                in_specs.append(pl.BlockSpec((tks[p], tn), lambda i, j, kk, p=p: (kidx(kk, p), j)))
            operands.append(b)
    for arr, kind in extras:
        if kind == "mn":
            in_specs.append(pl.BlockSpec((tm, tn), lambda i, j, kk: (i, j)))
        else:
            in_specs.append(pl.BlockSpec((1, tn), lambda i, j, kk: (0, j)))
        operands.append(arr)
    n_in = len(operands)
    n_ex = len(extras)
    per = 3 if dual else 2
    dims = (((0 if ta else 1,), (1 if tb else 0,)), ((), ()))

    n_acc = 2 if dual else 1

    def products(ins, p):
        a = ins[per * p][...].astype(BF16)
        return [lax.dot_general(a, ins[per * p + 1 + q][...].astype(BF16), dims, preferred_element_type=F32)
                for q in range(n_acc)]

    def finish(ins, out_refs, acc_vals):
        vals = acc_vals + [r[...] for r in ins[n_in - n_ex:]]
        res = epi(*vals) if epi is not None else (vals[0],)
        for o, r in zip(out_refs, res):
            o[...] = jnp.broadcast_to(r, o.shape).astype(o.dtype)

    out_specs = [pl.BlockSpec((tm, tn), lambda i, j, kk: (i, j)) for _ in outs]
    out_specs += [pl.BlockSpec((None, 8, tn), lambda i, j, kk: (i, 0, j)) for _ in range(col_sums)]
    out_shape = [_sds((M, N), d) for d in outs] + [_sds((M // tm, 8, N), F32) for _ in range(col_sums)]
    aliases = {}
    n_alias = 0
    if split is not None:
        nps = split // tn
        lead = 0 if into is None else into[1]
        out_specs = [pl.BlockSpec((None, tm, tn), lambda i, j, kk: (j // nps + lead, i, j % nps))]
        out_shape = [_sds((N // split if into is None else into[0].shape[0], M, split), outs[0])]
        if into is not None:
            in_specs.append(pl.BlockSpec(memory_space=pl.ANY))
            operands.append(into[0])
            aliases, n_alias = {n_in: 0}, 1

    grid = (M // tm, N // tn, nk_total)
    n_out = len(outs) + col_sums

    def at_step(first):
        ids = [pl.program_id(d) for d in range(3)]
        cond = None
        for d, g in zip(ids, grid):
            t = d == (0 if first else g - 1)
            cond = t if cond is None else cond & t
        return cond

    def body(*refs):
        ins, out_refs, accs, start, wait = _split_refs(refs, n_in + n_alias, n_out, carry)
        ins = ins[:n_in]
        if carry is not None:
            pl.when(at_step(True))(start)
        if nk_total == 1:
            finish(ins, out_refs, products(ins, 0))
        else:
            kk = pl.program_id(2)

            @pl.when(kk == 0)
            def _():
                for acc, v in zip(accs, products(ins, 0)):
                    acc[...] = v

            for p in range(len(pairs)):
                lo = max(offs[p], 1)

                @pl.when((kk >= lo) & (kk < offs[p] + nks[p]))
                def _(p=p):
                    for acc, v in zip(accs, products(ins, p)):
                        acc[...] += v

            @pl.when(kk == nk_total - 1)
            def _():
                finish(ins, out_refs, [acc[...] for acc in accs])
        if carry is not None:
            pl.when(at_step(False))(wait)

    ex = carry
    res = pl.pallas_call(
        body, name=name, grid=grid, in_specs=in_specs + ([] if ex is None else ex.in_specs),
        out_specs=out_specs + ([] if ex is None else ex.out_specs),
        out_shape=out_shape + ([] if ex is None else ex.out_shape), input_output_aliases=aliases,
        scratch_shapes=[pltpu.VMEM((tm, tn), F32) for _ in range(n_acc if nk_total > 1 else 0)]
        + ([] if ex is None else ex.scratch),
        compiler_params=_params(("arbitrary",) * 3 if ex is not None else ("parallel", "parallel", "arbitrary")),
    )(*operands, *([] if ex is None else ex.xs))
    if ex is not None:
        return (res[0] if n_out == 1 else res[:n_out]), list(res[n_out:])
    return res[0] if n_out == 1 else res


def _rowwise(fn, rows, vecs, out_rows, out_accs=(), *, name, tm=256, tc=None, carry=None):
    M = rows[0].shape[0]
    tm = _tile(M, tm, BF16_SUBLANES)
    nrow = M // tm
    C = rows[0].shape[1]
    ncol = 1 if tc is None else C // _tile(C, tc)
    tcol = None if tc is None else _tile(C, tc)

    def colwise(shape):
        return tc is not None and len(shape) == 2 and shape[0] == 1 and shape[1] == C

    def vspec(shape):
        if colwise(shape):
            return pl.BlockSpec((1, tcol), lambda j, i: (0, j))
        return pl.BlockSpec(tuple(shape), lambda j, i, n=len(shape): (0,) * n)

    def rspec(width):
        if tc is None:
            return pl.BlockSpec((tm, width), lambda j, i: (i, 0))
        return pl.BlockSpec((tm, tcol), lambda j, i: (i, j))

    in_specs = [rspec(r.shape[1]) for r in rows] + [vspec(v.shape) for v in vecs]
    out_specs = [rspec(c) for c, _ in out_rows] + [vspec(s) for s in out_accs]
    out_shape = [_sds((M, c), d) for c, d in out_rows] + [_sds(s, F32) for s in out_accs]
    n_in, n_or = len(rows) + len(vecs), len(out_rows)

    n_out = n_or + len(out_accs)
    ex = carry

    def body(*refs):
        ins, outs, _, start, wait = _split_refs(refs, n_in, n_out, ex)
        o_rows, o_accs = outs[:n_or], outs[n_or:]
        if ex is not None:
            pl.when((pl.program_id(0) == 0) & (pl.program_id(1) == 0))(start)
        r_out, a_out = fn(*[r[...] for r in ins])
        for o, r in zip(o_rows, r_out):
            o[...] = r.astype(o.dtype)
        i = pl.program_id(1)

        @pl.when(i == 0)
        def _():
            for o, a in zip(o_accs, a_out):
                o[...] = a

        @pl.when(i > 0)
        def _():
            for o, a in zip(o_accs, a_out):
                o[...] += a

        if ex is not None:
            pl.when((pl.program_id(0) == ncol - 1) & (pl.program_id(1) == nrow - 1))(wait)

    res = pl.pallas_call(
        body, name=name, grid=(ncol, nrow), in_specs=in_specs + ([] if ex is None else ex.in_specs),
        out_specs=out_specs + ([] if ex is None else ex.out_specs),
        out_shape=out_shape + ([] if ex is None else ex.out_shape),
        scratch_shapes=[] if ex is None else ex.scratch,
        compiler_params=_params(("arbitrary", "arbitrary") if ex is not None else ("parallel", "arbitrary")),
    )(*rows, *vecs, *([] if ex is None else ex.xs))
    if ex is not None:
        return res[:n_or], res[n_or:n_out], list(res[n_out:])
    return res[:n_or], res[n_or:]


def _colsum(t):
    return jnp.sum(t, axis=0, keepdims=True)


def _gelu(t):
    return 0.5 * t * (1.0 + lax.erf(t * math.sqrt(0.5)))


def _gelu_grad(t):
    return 0.5 * (1.0 + lax.erf(t * math.sqrt(0.5))) + t * jnp.exp(-0.5 * t * t) * (1.0 / math.sqrt(2.0 * math.pi))


def _sigmoid(t):
    return 1.0 / (1.0 + jnp.exp(-t))


def _rms_stats(t, width):
    return lax.rsqrt(jnp.sum(t * t, axis=-1, keepdims=True) * (1.0 / width) + EPS)


def _rms_bwd(dn, tn, r, width):
    return r * (dn - tn * (jnp.sum(dn * tn, axis=-1, keepdims=True) * (1.0 / width)))


def _place():
    return lax.axis_index("x"), lax.axis_index("y"), lax.axis_index("c")


class _ChipExchange:
    def __init__(self, xs, gather):
        self.xs, self.gather, self.n = list(xs), gather, len(xs)
        self.in_specs = [pl.BlockSpec(memory_space=pl.ANY)] * self.n
        self.out_specs = [pl.BlockSpec(memory_space=pl.ANY)] * self.n
        self.out_shape = [_sds((4,) + (x.shape if gather else x.shape[1:]), x.dtype) for x in self.xs]
        self.scratch = [pltpu.SemaphoreType.DMA((self.n, 3)), pltpu.SemaphoreType.DMA((self.n, 3))]

    def bind(self, x_refs, out_refs, send_sems, recv_sems):
        x, y, c = _place()
        p = 2 * x + y
        chips = [(1 - x, y), (x, 1 - y), (1 - x, 1 - y)]

        def copy(w, k, outgoing):
            qx, qy = chips[k]
            there = 2 * qx + qy
            if self.gather:
                src = x_refs[w]
            else:
                src = x_refs[w].at[there if outgoing else p]
            return pltpu.make_async_remote_copy(
                src_ref=src, dst_ref=out_refs[w].at[p if outgoing else there], send_sem=send_sems.at[w, k],
                recv_sem=recv_sems.at[w, k], device_id=(qx, qy, c), device_id_type=MESH)

        def start():
            for w in range(self.n):
                for k in range(3):
                    copy(w, k, True).start()

        def wait():
            for w in range(self.n):
                for k in range(3):
                    copy(w, k, False).wait_recv()
            for w in range(self.n):
                for k in range(3):
                    copy(w, k, True).wait_send()

        return start, wait


class _PairExchange:
    def __init__(self, xs, mode):
        self.xs, self.mode, self.n = list(xs), mode, len(xs)
        self.in_specs = [pl.BlockSpec(memory_space=pl.ANY)] * self.n
        self.out_specs = [pl.BlockSpec(memory_space=pl.ANY)] * self.n
        shape = {"halves": lambda s: (4,) + s[2:], "forward": lambda s: s, "gather": lambda s: (2,) + s}[mode]
        self.out_shape = [_sds(shape(x.shape), x.dtype) for x in self.xs]
        self.scratch = [pltpu.SemaphoreType.DMA((self.n, 3)), pltpu.SemaphoreType.DMA((self.n, 3))]

    def bind(self, x_refs, out_refs, send_sems, recv_sems):
        x, y, c = _place()
        chips = [(1 - x, y), (x, 1 - y), (1 - x, 1 - y)]

        def copy(w, src, dst, k):
            return pltpu.make_async_remote_copy(src_ref=src, dst_ref=dst, send_sem=send_sems.at[w, k],
                                                recv_sem=recv_sems.at[w, k], device_id=(x, y, 1 - c),
                                                device_id_type=MESH)

        def start():
            for w, (xr, orf) in enumerate(zip(x_refs, out_refs)):
                if self.mode == "halves":
                    for q in range(4):
                        copy(w, xr.at[q, 1 - c], orf.at[q], 0).start()
                elif self.mode == "forward":
                    for k, (qx, qy) in enumerate(chips):
                        copy(w, xr.at[2 * qx + qy], orf.at[2 * qx + qy], k).start()
                else:
                    copy(w, xr, orf.at[c], 0).start()

        def wait():
            for w, (xr, orf) in enumerate(zip(x_refs, out_refs)):
                if self.mode == "halves":
                    copy(w, orf, orf, 0).wait()
                elif self.mode == "forward":
                    for k, (qx, qy) in enumerate(chips):
                        copy(w, xr.at[2 * qx + qy], orf.at[2 * qx + qy], k).wait()
                else:
                    cp = copy(w, xr, orf.at[1 - c], 0)
                    cp.wait_recv()
                    cp.wait_send()

        return start, wait


def _split_refs(refs, n_in, n_out, ex):
    ne = 0 if ex is None else ex.n
    ins, xin = refs[:n_in], refs[n_in:n_in + ne]
    outs, xout = refs[n_in + ne:n_in + ne + n_out], refs[n_in + ne + n_out:n_in + 2 * ne + n_out]
    rest = refs[n_in + 2 * ne + n_out:]
    if ex is None:
        return ins, outs, rest, None, None
    start, wait = ex.bind(xin, xout, rest[-2], rest[-1])
    return ins, outs, rest[:-2], start, wait


def _attn_fwd(q, k, v, *, heads, tq=512, carry=None):
    N, M = q.shape[0], k.shape[0]
    tq = _tile(N, tq)
    sub = _tile(tq, 256)
    vd = v.shape[1] // heads
    nq = N // tq

    def body(*refs):
        (q_ref, k_ref, v_ref), (o_ref, lse_ref), _, start, wait = _split_refs(refs, 3, 2, carry)
        if carry is not None:
            pl.when((pl.program_id(0) == 0) & (pl.program_id(1) == 0))(start)
        for sb in range(tq // sub):
            rows = pl.ds(sb * sub, sub)
            s = lax.dot_general(q_ref[rows, :], k_ref[...], (((1,), (1,)), ((), ())), preferred_element_type=F32)
            m = jnp.max(s, axis=-1, keepdims=True)
            p = jnp.exp(s - m)
            l = jnp.sum(p, axis=-1, keepdims=True)
            o = jnp.dot(p.astype(BF16), v_ref[...], preferred_element_type=F32) / l
            o_ref[rows, :] = o.astype(o_ref.dtype)
            lse_ref[rows, :] = jnp.broadcast_to(m + jnp.log(l), (sub, vd))
        if carry is not None:
            pl.when((pl.program_id(0) == heads - 1) & (pl.program_id(1) == nq - 1))(wait)

    ex = carry
    res = pl.pallas_call(
        body, name="attn_fwd", grid=(heads, nq),
        in_specs=[pl.BlockSpec((tq, HEAD_PAD), lambda h, i: (i, h)),
                  pl.BlockSpec((M, HEAD_PAD), lambda h, i: (0, h)),
                  pl.BlockSpec((M, vd), lambda h, i: (0, h))] + ([] if ex is None else ex.in_specs),
        out_specs=[pl.BlockSpec((tq, vd), lambda h, i: (i, h)),
                   pl.BlockSpec((tq, vd), lambda h, i: (i, h))] + ([] if ex is None else ex.out_specs),
        out_shape=[_sds((N, heads * vd), BF16), _sds((N, heads * vd), F32)] + ([] if ex is None else ex.out_shape),
        scratch_shapes=[] if ex is None else ex.scratch,
        compiler_params=_params(("arbitrary", "arbitrary")),
    )(q, k, v, *([] if ex is None else ex.xs))
    return res[0], res[1], list(res[2:])


def _attn_bwd(q, k, v, o, lse, do, *, heads, tq=256, carry=None):
    N, M = q.shape[0], k.shape[0]
    tq = _tile(N, tq)
    vd = v.shape[1] // heads
    nq = N // tq
    nch = next(n for n in (2, 3, 1) if (M // LANES) % n == 0)
    mc = M // nch
    nt = (((1,), (1,)), ((), ()))
    tn = (((0,), (0,)), ((), ()))

    def body(*refs):
        (q_ref, k_ref, v_ref, o_ref, lse_ref, do_ref), (dq_ref, dk_ref, dv_ref), _, start, wait = _split_refs(
            refs, 6, 3, carry)
        if carry is not None:
            pl.when((pl.program_id(0) == 0) & (pl.program_id(1) == 0))(start)
        i = pl.program_id(1)
        qb, dob = q_ref[...], do_ref[...]
        lse = lse_ref[...][:, :1]
        delta = jnp.sum(dob.astype(F32) * o_ref[...].astype(F32), axis=-1, keepdims=True)
        dq, parts = None, []
        for ch in range(nch):
            keys = pl.ds(ch * mc, mc)
            kb = k_ref[keys, :]
            s = lax.dot_general(qb, kb, nt, preferred_element_type=F32)
            p = jnp.exp(s - lse)
            dp = lax.dot_general(dob, v_ref[keys, :], nt, preferred_element_type=F32)
            ds = (p * (dp - delta)).astype(BF16)
            dq_c = jnp.dot(ds, kb, preferred_element_type=F32)
            dq = dq_c if dq is None else dq + dq_c
            parts.append((keys, lax.dot_general(ds, qb, tn, preferred_element_type=F32),
                          lax.dot_general(p.astype(BF16), dob, tn, preferred_element_type=F32)))
        dq_ref[...] = dq

        @pl.when(i == 0)
        def _():
            for keys, dk_part, dv_part in parts:
                dk_ref[keys, :] = dk_part
                dv_ref[keys, :] = dv_part

        @pl.when(i > 0)
        def _():
            for keys, dk_part, dv_part in parts:
                dk_ref[keys, :] += dk_part
                dv_ref[keys, :] += dv_part

        if carry is not None:
            pl.when((pl.program_id(0) == heads - 1) & (pl.program_id(1) == nq - 1))(wait)

    ex = carry
    res = pl.pallas_call(
        body, name="attn_bwd", grid=(heads, nq),
        in_specs=[pl.BlockSpec((tq, HEAD_PAD), lambda h, i: (i, h)),
                  pl.BlockSpec((M, HEAD_PAD), lambda h, i: (0, h)),
                  pl.BlockSpec((M, vd), lambda h, i: (0, h)),
                  pl.BlockSpec((tq, vd), lambda h, i: (i, h)),
                  pl.BlockSpec((tq, vd), lambda h, i: (i, h)),
                  pl.BlockSpec((tq, vd), lambda h, i: (i, h))] + ([] if ex is None else ex.in_specs),
        out_specs=[pl.BlockSpec((tq, HEAD_PAD), lambda h, i: (i, h)),
                   pl.BlockSpec((M, HEAD_PAD), lambda h, i: (0, h)),
                   pl.BlockSpec((M, vd), lambda h, i: (0, h))] + ([] if ex is None else ex.out_specs),
        out_shape=[_sds((N, heads * HEAD_PAD), F32), _sds((M, heads * HEAD_PAD), F32),
                   _sds((M, heads * vd), F32)] + ([] if ex is None else ex.out_shape),
        scratch_shapes=[] if ex is None else ex.scratch,
        compiler_params=_params(("arbitrary", "arbitrary")),
    )(q, k, v, o, lse, do, *([] if ex is None else ex.xs))
    return res[0], res[1], res[2], list(res[3:])


def _comm_call(body, xs, out_shapes, n_sems, name, in_vmem):
    space = pltpu.VMEM if in_vmem else pl.ANY
    n = len(xs)

    def wrapped(*refs):
        body(refs[:n], refs[n:2 * n], *refs[2 * n:])

    return pl.pallas_call(
        wrapped, name=name, out_shape=list(out_shapes),
        in_specs=[pl.BlockSpec(memory_space=space)] * n, out_specs=[pl.BlockSpec(memory_space=space)] * n,
        scratch_shapes=[pltpu.SemaphoreType.DMA((n, n_sems)), pltpu.SemaphoreType.DMA((n, n_sems)),
                        pltpu.SemaphoreType.DMA((n,))],
        compiler_params=pltpu.CompilerParams(vmem_limit_bytes=VMEM_LIMIT_BYTES),
    )(*xs)


def _all_gather8(blks, *, name, in_vmem, others_only=False):
    def body(x_refs, out_refs, send_sems, recv_sems, local_sems):
        x, y, c = _place()
        me, sibling = (x, y, c), (x, y, 1 - c)
        chips = [(1 - x, y), (x, 1 - y), (1 - x, 1 - y)]
        waits = []
        for w, (x_ref, out_ref) in enumerate(zip(x_refs, out_refs)):
            def slot(px, py, pc, out_ref=out_ref):
                return out_ref.at[4 * px + 2 * py + pc]

            def copy(k, block, to, src=None, w=w, slot=slot):
                return pltpu.make_async_remote_copy(
                    src_ref=slot(*block) if src is None else src, dst_ref=slot(*block),
                    send_sem=send_sems.at[w, k], recv_sem=recv_sems.at[w, k], device_id=to, device_id_type=MESH)

            mine = None
            first = []
            if not others_only:
                mine = pltpu.make_async_copy(x_ref, slot(*me), local_sems.at[w])
                mine.start()
                first.append(copy(0, me, sibling, src=x_ref))
            first += [copy(1 + j, me, (*chip, c), src=x_ref) for j, chip in enumerate(chips)]
            for cp in first:
                cp.start()
            waits.append((copy, mine, first))
        for copy, mine, first in waits:
            passed = [copy(4 + j, (*chip, c), sibling) for j, chip in enumerate(chips)]
            for j, chip in enumerate(chips):
                copy(1 + j, (*chip, c), me).wait_recv()
                passed[j].start()
            if not others_only:
                copy(0, sibling, me).wait_recv()
            for j, chip in enumerate(chips):
                copy(4 + j, (*chip, 1 - c), me).wait_recv()
            for cp in first + passed:
                cp.wait_send()
            if mine is not None:
                mine.wait()

    return _comm_call(body, blks, [_sds((8,) + b.shape, b.dtype) for b in blks], 7, name, in_vmem)


def _exchange_alone(ex, *, name):
    def body(x_refs, out_refs, send_sems, recv_sems, local_sems):
        start, wait = ex.bind(x_refs, out_refs, send_sems, recv_sems)
        start()
        wait()

    return list(_comm_call(body, ex.xs, ex.out_shape, 3, name, False))


def _block_rows(rows, row_bytes, target=1 << 21, align=BF16_SUBLANES):
    return _tile(rows, max(align, target // row_bytes // align * align), align)


def _sum_blocks(buf, *, name, out_dtype):
    B, R, C = buf.shape
    tm = _block_rows(R, B * C * buf.dtype.itemsize)

    def body(x_ref, o_ref):
        acc = x_ref[0].astype(F32)
        for b in range(1, B):
            acc = acc + x_ref[b].astype(F32)
        o_ref[...] = acc.astype(o_ref.dtype)

    return pl.pallas_call(
        body, name=name, grid=(R // tm,), in_specs=[pl.BlockSpec((B, tm, C), lambda i: (0, i, 0))],
        out_specs=pl.BlockSpec((tm, C), lambda i: (i, 0)), out_shape=_sds((R, C), out_dtype),
        compiler_params=_params(("parallel",)),
    )(buf)


def _pair_add(mine, theirs, core, *, name):
    _, _, R, C = mine.shape
    tm = _block_rows(R, C * 2)

    def body(core_ref, a_ref, b_ref, o_ref):
        o_ref[...] = (a_ref[...].astype(F32) + b_ref[...].astype(F32)).astype(o_ref.dtype)

    return pl.pallas_call(
        body, name=name, out_shape=_sds(theirs.shape, BF16),
        grid_spec=pltpu.PrefetchScalarGridSpec(
            num_scalar_prefetch=1, grid=(4, R // tm),
            in_specs=[pl.BlockSpec((None, None, tm, C), lambda q, i, core_ref: (q, core_ref[0], i, 0)),
                      pl.BlockSpec((None, tm, C), lambda q, i, core_ref: (q, i, 0))],
            out_specs=pl.BlockSpec((None, tm, C), lambda q, i, core_ref: (q, i, 0))),
        compiler_params=_params(("parallel", "parallel")),
    )(core, mine, theirs)


def _assemble(gathered, own, chip, *, name, transpose):
    _, K, Ns = gathered.shape
    tm = _block_rows(K, Ns * 4)

    def body(chip_ref, g_ref, own_ref, o_ref):
        q = pl.program_id(0)

        @pl.when(q == chip_ref[0])
        def _():
            o_ref[...] = own_ref[...].astype(BF16)

        @pl.when(q != chip_ref[0])
        def _():
            o_ref[...] = g_ref[...]

    if transpose:
        out_spec = pl.BlockSpec((tm, Ns), lambda q, i, ch: (i, q))
        out_shape = _sds((K, 4 * Ns), BF16)
    else:
        out_spec = pl.BlockSpec((None, tm, Ns), lambda q, i, ch: (q, i, 0))
        out_shape = _sds((4, K, Ns), BF16)
    return pl.pallas_call(
        body, name=name, out_shape=out_shape,
        grid_spec=pltpu.PrefetchScalarGridSpec(
            num_scalar_prefetch=1, grid=(4, K // tm),
            in_specs=[pl.BlockSpec((None, tm, Ns), lambda q, i, ch: (jnp.where(q == ch[0], (q + 1) % 4, q), i, 0)),
                      pl.BlockSpec((tm, Ns), lambda q, i, ch: (jnp.where(q == ch[0], i, 0), 0))],
            out_specs=out_spec),
        compiler_params=_params(("arbitrary", "arbitrary")),
    )(chip, gathered, own)


def _assemble_halves(mine, theirs, own, place, *, name, transpose):
    _, K2, Ns = mine.shape
    tm = _block_rows(K2, Ns * 4)
    nb = K2 // tm

    def body(place_ref, m_ref, t_ref, own_ref, o_ref):
        q, hb = pl.program_id(0), pl.program_id(1)
        is_own = q == place_ref[0]
        is_mine = hb == place_ref[1]

        @pl.when(is_own)
        def _():
            o_ref[...] = own_ref[...].astype(BF16)

        @pl.when(jnp.logical_not(is_own) & is_mine)
        def _():
            o_ref[...] = m_ref[...]

        @pl.when(jnp.logical_not(is_own) & jnp.logical_not(is_mine))
        def _():
            o_ref[...] = t_ref[...]

    def other(q, pr):
        return jnp.where(q == pr[0], (q + 1) % 4, q)

    if transpose:
        out_spec = pl.BlockSpec((tm, Ns), lambda q, hb, i, pr: (hb * nb + i, q))
        out_shape = _sds((2 * K2, 4 * Ns), BF16)
    else:
        out_spec = pl.BlockSpec((None, tm, Ns), lambda q, hb, i, pr: (q, hb * nb + i, 0))
        out_shape = _sds((4, 2 * K2, Ns), BF16)
    return pl.pallas_call(
        body, name=name, out_shape=out_shape,
        grid_spec=pltpu.PrefetchScalarGridSpec(
            num_scalar_prefetch=1, grid=(4, 2, nb),
            in_specs=[pl.BlockSpec((None, tm, Ns), lambda q, hb, i, pr: (other(q, pr), jnp.where(hb == pr[1], i, 0), 0)),
                      pl.BlockSpec((None, tm, Ns), lambda q, hb, i, pr: (other(q, pr), jnp.where(hb == pr[1], 0, i), 0)),
                      pl.BlockSpec((tm, Ns), lambda q, hb, i, pr: (jnp.where(q == pr[0], hb * nb + i, 0), 0))],
            out_specs=out_spec),
        compiler_params=_params(("arbitrary",) * 3),
    )(place, mine, theirs, own)


def _split_lanes(row, widths):
    out, off = [], 0
    for wd in widths:
        out.append(row[:, off:off + wd])
        off += wd
    return out


def _adamw(w, g, m, v, *, name, carry=None):
    C = w.shape[1]

    def fn(w, g, m, v):
        m = ADAM_B1 * m + (1.0 - ADAM_B1) * g
        v = ADAM_B2 * v + (1.0 - ADAM_B2) * (g * g)
        m_hat = m / (1.0 - ADAM_B1 ** ADAM_STEP)
        v_hat = v / (1.0 - ADAM_B2 ** ADAM_STEP)
        delta = -ADAM_LR * (m_hat / (jnp.sqrt(v_hat) + ADAM_EPS) + ADAM_WD * w)
        return (delta, m, v), ()

    tm = max(8, min(512, (1 << 20) // (4 * C) // 8 * 8))
    res = _rowwise(fn, [w, g, m, v], [], [(C, F32)] * 3, name=name, tm=tm, carry=carry)
    return tuple(res[0]) + ((res[2],) if carry is not None else ())


def _rope_tables(n):
    rows = n // GRID_W
    row = jnp.repeat(jnp.arange(rows, dtype=F32), GRID_W)
    col = jnp.tile(jnp.arange(GRID_W, dtype=F32), rows)
    nf = ROPE_DIM // 4
    freqs = ROPE_THETA ** (-jnp.arange(nf, dtype=F32) / nf)
    ang_r, ang_c = row[:, None] * freqs[None, :], col[:, None] * freqs[None, :]
    cr, sr, cc, sc = jnp.cos(ang_r), jnp.sin(ang_r), jnp.cos(ang_c), jnp.sin(ang_c)
    nope = HEAD_PAD - 2 * ROPE_DIM
    one, zero, z = jnp.ones((n, nope), F32), jnp.zeros((n, nope), F32), jnp.zeros((n, nf), F32)
    pad = jnp.zeros((n, ROPE_DIM), F32)
    cos = jnp.concatenate([one, cr, cr, cc, cc, pad], axis=1)
    s_lo = jnp.concatenate([zero, -sr, z, -sc, z, pad], axis=1)
    s_hi = jnp.concatenate([zero, z, sr, z, sc, pad], axis=1)
    return cos, s_lo, s_hi


def _rope(n, cos, s_lo, s_hi):
    q = ROPE_DIM // 4
    return n * cos + pltpu.roll(n, HEAD_PAD - q, 1) * s_lo + pltpu.roll(n, q, 1) * s_hi


def _rope_t(d, cos, s_lo, s_hi):
    q = ROPE_DIM // 4
    return d * cos + pltpu.roll(d * s_lo, q, 1) + pltpu.roll(d * s_hi, HEAD_PAD - q, 1)


def kernel(x, c, ctx, c_ctx, w_mod, b_mod, norm1_g, w_in, q_norm_g, kv_norm_g, w_uq, w_ukv, qk_norm_q, qk_norm_k, sgu_norm_g, sgu_norm_b, w_spatial, b_spatial, w_br_attn, w_br_sgu, w_out, norm2_g, w_ffn_in, w_ffn_out, loss_target, m_c_ctx, m_w_mod, m_b_mod, m_norm1_g, m_w_in, m_q_norm_g, m_kv_norm_g, m_w_uq, m_w_ukv, m_qk_norm_q, m_qk_norm_k, m_sgu_norm_g, m_sgu_norm_b, m_w_spatial, m_b_spatial, m_w_br_attn, m_w_br_sgu, m_w_out, m_norm2_g, m_w_ffn_in, m_w_ffn_out, v_c_ctx, v_w_mod, v_b_mod, v_norm1_g, v_w_in, v_q_norm_g, v_kv_norm_g, v_w_uq, v_w_ukv, v_qk_norm_q, v_qk_norm_k, v_sgu_norm_g, v_sgu_norm_b, v_w_spatial, v_b_spatial, v_w_br_attn, v_w_br_sgu, v_w_out, v_norm2_g, v_w_ffn_in, v_w_ffn_out):
    ax, ay, ac = _place()
    my_chip = 2 * ax + ay
    my_dev = 4 * ax + 2 * ay + ac

    N, D = x.shape[1], x.shape[2]
    CT = ctx.shape[1]
    M = N + CT
    QL, KVL, QK = q_norm_g.shape[-1], kv_norm_g.shape[-1], qk_norm_q.shape[-1]
    NOPE = QK - ROPE_DIM
    VD = NOPE
    H = 4 * w_uq.shape[-1] // QK
    SW, G, CH = sgu_norm_g.shape[-1], w_spatial.shape[1], w_spatial.shape[2]
    GD = SW // G
    DFF = 4 * w_ffn_out.shape[1]
    NMOD = 4 * w_mod.shape[-1]
    NM = w_mod.shape[-1]
    KVP = KVL + 2 * ROPE_DIM
    assert NOPE == LANES and GD == LANES and HEAD_PAD == NOPE + 2 * ROPE_DIM and CH == LANES
    scale = QK ** -0.5

    x2, ctx2, tgt2 = x[0], ctx[0], loss_target[0]

    c_all = _all_gather8([c], name="ag_c", in_vmem=True)[0][:, 0, :]
    c_rows = jnp.concatenate([c_all, c_ctx[None, :], jnp.zeros((BF16_SUBLANES - 9, D), F32)], axis=0)

    def silu_fn(t):
        s = _sigmoid(t)
        return (t * s, s * (1.0 + t * (1.0 - s))), ()

    (silu_c, dsilu_c), _ = _rowwise(silu_fn, [c_rows], [], [(D, F32), (D, F32)], name="silu_c", tm=16)
    wm = w_mod[0]
    mod_loc = _mm([(silu_c, wm)], name="mod_fwd", outs=(F32,), tn=512, tk=512,
                  extras=[(lax.dynamic_slice_in_dim(b_mod, my_chip * NM, NM, axis=1), "n")],
                  epi=lambda acc, b: (acc + b,))
    mod_all = _all_gather8([mod_loc], name="ag_mod", in_vmem=True)[0]
    mod_full = jnp.concatenate([mod_all[0], mod_all[2], mod_all[4], mod_all[6]], axis=1)
    mod_me = lax.dynamic_slice_in_dim(mod_full, my_dev, 1, axis=0)
    sh1, sc1, g1, sh2, sc2, g2 = [mod_me[:, i * D:(i + 1) * D] for i in range(6)]
    sh1c, sc1c = mod_full[8:9, :D], mod_full[8:9, D:2 * D]

    big = [w_in[0], w_uq[0], w_ukv[0], w_br_attn[0], w_br_sgu[0], w_out[0], w_ffn_in[0], w_ffn_out[0]]
    col_sharded = [True, True, True, True, True, False, True, False]
    halves = [lax.dynamic_slice_in_dim(a, ac * (a.shape[0] // 2), a.shape[0] // 2, axis=0).astype(BF16) for a in big]
    tags = ["w_in", "w_uq", "w_ukv", "w_br_attn", "w_br_sgu", "w_out", "w_ffn_in", "w_ffn_out"]
    first_group, attn_group, ffn_group = [0, 1, 2], [3, 4, 5, 6], [7]
    chip1 = jnp.reshape(my_chip, (1,)).astype(jnp.int32)
    place2 = jnp.stack([my_chip, ac]).astype(jnp.int32)

    def laid_out(seg, i):
        a = big[i]
        if col_sharded[i] and seg.ndim == 3:
            return seg.transpose(1, 0, 2).reshape(a.shape[0], 4 * a.shape[1])
        return seg if col_sharded[i] else seg.reshape(4 * a.shape[0], a.shape[1])

    def side_by_side(i):
        return col_sharded[i] and big[i].shape[1] % LANES == 0

    def finish_gather(idx, mine4, theirs4):
        return [laid_out(_assemble_halves(m, t, big[i], place2, name="assemble_" + tags[i], transpose=side_by_side(i)), i)
                for i, m, t in zip(idx, mine4, theirs4)]

    gathered = _all_gather8([halves[i] for i in first_group], name="ag_weights", in_vmem=False, others_only=True)
    w_in_f, w_uq_f, w_ukv_f = [
        laid_out(_assemble(seg.reshape((4,) + big[i].shape), big[i], chip1, name="assemble_" + tags[i],
                           transpose=side_by_side(i)), i) for i, seg in zip(first_group, gathered)]
    o_kv, o_u = QL, QL + KVL + ROPE_DIM
    o_v, o_g = o_u + SW, o_u + 2 * SW
    w_q = w_in_f[:, :QL]
    w_kv = jnp.pad(w_in_f[:, o_kv:o_u], ((0, 0), (0, ROPE_DIM)))
    w_u, w_v = w_in_f[:, o_u:o_v], w_in_f[:, o_v:o_g]
    w_g1, w_g2 = w_in_f[:, o_g:o_g + D], w_in_f[:, o_g + D:]
    w_uq_p = jnp.pad(w_uq_f.reshape(QL, H, QK), ((0, 0), (0, 0), (0, HEAD_PAD - QK))).reshape(QL, H * HEAD_PAD)

    cos_t, slo_t, shi_t = _rope_tables(N)
    ones_c = jnp.concatenate([jnp.ones((CT, NOPE + ROPE_DIM), F32), jnp.zeros((CT, ROPE_DIM), F32)], axis=1)
    cos_k = jnp.concatenate([cos_t, ones_c], axis=0)
    slo_k = jnp.concatenate([slo_t, jnp.zeros((CT, HEAD_PAD), F32)], axis=0)
    shi_k = jnp.concatenate([shi_t, jnp.zeros((CT, HEAD_PAD), F32)], axis=0)
    gq_p = jnp.pad(qk_norm_q, ((0, 0), (0, HEAD_PAD - QK)))
    gk_p = jnp.pad(qk_norm_k, ((0, 0), (0, HEAD_PAD - QK)))

    def norm_mod_fn(t, g, sh, sc):
        r = _rms_stats(t, D)
        return (((t * r) * g) * (1.0 + sc) + sh,), ()

    (h,), _ = _rowwise(norm_mod_fn, [x2], [norm1_g, sh1, sc1], [(D, BF16)], name="norm1_x")
    (ctx_h,), _ = _rowwise(norm_mod_fn, [ctx2], [norm1_g, sh1c, sc1c], [(D, BF16)], name="norm1_ctx")

    qc = _mm([(h, w_q)], name="proj_q", outs=(F32,))
    kvin = jnp.concatenate([_mm([(h, w_kv)], name="proj_kv", outs=(F32,)),
                            _mm([(ctx_h, w_kv)], name="proj_kv_ctx", outs=(F32,))], axis=0)
    u_in = _mm([(h, w_u)], name="proj_u", outs=(BF16,))
    v_in = _mm([(h, w_v)], name="proj_v", outs=(BF16,))
    g1_in, (mine_bra,) = _mm([(h, w_g1)], name="proj_g1", outs=(BF16,), carry=_ChipExchange([halves[3]], gather=True))
    g2_in, (mine_brs,) = _mm([(h, w_g2)], name="proj_g2", outs=(BF16,), carry=_ChipExchange([halves[4]], gather=True))

    def rms_gain_fn(width):
        def fn(t, g):
            return (((t * _rms_stats(t, width)) * g),), ()
        return fn

    (qn,), _ = _rowwise(rms_gain_fn(QL), [qc], [q_norm_g], [(QL, BF16)], name="q_norm")

    def kv_norm_fn(t, g):
        kvc = t[:, :KVL]
        return (((kvc * _rms_stats(kvc, KVL)) * g),), ()

    (kvn,), _ = _rowwise(kv_norm_fn, [kvin], [kv_norm_g], [(KVL, BF16)], name="kv_norm")
    q_raw = _mm([(qn, w_uq_p)], name="q_up", outs=(F32,))
    kv_raw = _mm([(kvn, w_ukv_f)], name="kv_up", outs=(F32,))

    def q_post_fn(t, cos, slo, shi, g):
        outs = []
        for hd in range(H):
            th = t[:, hd * HEAD_PAD:(hd + 1) * HEAD_PAD]
            outs.append(_rope((th * _rms_stats(th, QK)) * g, cos, slo, shi) * scale)
        return (jnp.concatenate(outs, axis=1),), ()

    (q_att,), _ = _rowwise(q_post_fn, [q_raw, cos_t, slo_t, shi_t], [gq_p], [(H * HEAD_PAD, BF16)], name="q_post")

    def k_post_fn(t, kvi, cos, slo, shi, g):
        kr = kvi[:, KVL:]
        ks, vs = [], []
        for hd in range(H):
            th = jnp.concatenate([t[:, hd * HEAD_PAD:hd * HEAD_PAD + NOPE], kr], axis=1)
            ks.append(_rope((th * _rms_stats(th, QK)) * g, cos, slo, shi))
            vs.append(t[:, hd * HEAD_PAD + NOPE:(hd + 1) * HEAD_PAD])
        return (jnp.concatenate(ks, axis=1), jnp.concatenate(vs, axis=1)), ()

    (k_att, v_att), _, (mine_out,) = _rowwise(k_post_fn, [kv_raw, kvin, cos_k, slo_k, shi_k], [gk_p],
                                              [(H * HEAD_PAD, BF16), (H * VD, BF16)], name="k_post",
                                              carry=_ChipExchange([halves[5]], gather=True))
    attn_o, lse, (mine_ffi,) = _attn_fwd(q_att, k_att, v_att, heads=H, carry=_ChipExchange([halves[6]], gather=True))
    mine4 = [mine_bra, mine_brs, mine_out, mine_ffi]

    ws3 = w_spatial[0]
    bs_t = jnp.pad(b_spatial[0].T, ((0, 0), (0, LANES - G)))

    def sgu_parts(u_in, v_in, ng, nb):
        u, v = _gelu(u_in.astype(F32)), _gelu(v_in.astype(F32))
        mu = jnp.mean(v, axis=-1, keepdims=True)
        vc = v - mu
        rs = lax.rsqrt(jnp.mean(vc * vc, axis=-1, keepdims=True) + EPS)
        xhat = vc * rs
        return u, xhat, rs, (xhat * ng + nb).astype(BF16)

    def sgu_fwd_fn(u_in, v_in, ng, nb, ws, bst):
        u, _, _, vnb = sgu_parts(u_in, v_in, ng, nb)
        outs = []
        for g in range(G):
            sl = slice(g * GD, (g + 1) * GD)
            mixed = jnp.dot(ws[g].astype(BF16), vnb[:, sl], preferred_element_type=F32) + bst[:, g:g + 1]
            outs.append(u[:, sl] * mixed)
        return (jnp.concatenate(outs, axis=1),), ()

    (sgu_o,), _, theirs4 = _rowwise(sgu_fwd_fn, [u_in, v_in], [sgu_norm_g, sgu_norm_b, ws3, bs_t], [(SW, BF16)],
                                    name="sgu_fwd", tm=CH, carry=_PairExchange(mine4, "forward"))
    w_bra, w_brs, w_out_f, w_ffi = finish_gather(attn_group, mine4, theirs4)
    w_fa, w_fb = w_ffi[:, :DFF], w_ffi[:, DFF:]

    a1 = _mm([(attn_o, w_bra)], name="br_attn", outs=(BF16,))
    def merge_epi(acc, a1v, gi1, gi2):
        return acc, _sigmoid(gi1.astype(F32)) * a1v.astype(F32) + _sigmoid(gi2.astype(F32)) * acc

    a2, merged = _mm([(sgu_o, w_brs)], name="br_sgu", outs=(BF16, BF16),
                     extras=[(a1, "mn"), (g1_in, "mn"), (g2_in, "mn")], epi=merge_epi)

    def res_gate(acc, res, gate):
        return res + gate * acc, acc

    x1, mo = _mm([(merged, w_out_f)], name="out_proj", outs=(F32, BF16), tn=1024,
                 extras=[(x2, "mn"), (g1, "n")], epi=res_gate)
    (h2,), _ = _rowwise(norm_mod_fn, [x1], [norm2_g, sh2, sc2], [(D, BF16)], name="norm2")

    def swiglu_epi(a, b):
        return a, b, (a * _sigmoid(a)) * b

    (fa, fb, act), mine4 = _mm([(h2, w_fa, w_fb)], name="ffn_in", outs=(BF16, BF16, BF16), tn=512, epi=swiglu_epi,
                               carry=_ChipExchange([halves[i] for i in ffn_group], gather=True))
    (w_ffo,) = finish_gather(ffn_group, mine4, _exchange_alone(_PairExchange(mine4, "forward"), name="ag_forward_ffn"))
    def loss_epi(acc, res, t, gate):
        e = (res + gate * acc) - t
        dy = e * (1.0 / D)
        return dy, gate * dy, _colsum(e * e) * (0.5 / D), _colsum(dy * acc)

    dy, df, loss_part, dg2_part = _mm([(act, w_ffo)], name="ffn_out", outs=(F32, BF16), tn=1024, col_sums=2,
                                      extras=[(x1, "mn"), (tgt2, "mn"), (g2, "n")], epi=loss_epi)

    def fold_fn(a, b):
        return (), (_colsum(a), _colsum(b))

    _, (loss_cols, dg2) = _rowwise(fold_fn, [loss_part[:, 0, :], dg2_part[:, 0, :]], [], [], [(1, D), (1, D)],
                                   name="loss_fold", tm=loss_part.shape[0])

    def swiglu_bwd_epi(dact, a, b):
        a, b = a.astype(F32), b.astype(F32)
        s = _sigmoid(a)
        return dact * b * (s * (1.0 + a * (1.0 - s))), dact * (a * s)

    da, db = _mm([(df, w_ffo)], tb=True, name="ffn_out_dx", outs=(BF16, BF16), tn=512,
                 extras=[(fa, "mn"), (fb, "mn")], epi=swiglu_bwd_epi)
    dw_ffo = _mm([(act, df)], ta=True, name="ffn_out_dw", outs=(BF16,))
    dh2 = _mm([(da, w_fa), (db, w_fb)], tb=True, name="ffn_in_dx", outs=(F32,))
    ns_ffi = w_ffn_in.shape[-1]
    dw_ffi = _mm([(h2, da)], ta=True, name="ffn_in_dw_a", outs=(BF16,), tn=1408, split=ns_ffi,
                 into=(lax.empty((4, D, ns_ffi), BF16), 0))
    dw_ffi = _mm([(h2, db)], ta=True, name="ffn_in_dw_b", outs=(BF16,), tn=1408, split=ns_ffi, into=(dw_ffi, 2))

    def norm2_bwd_fn(dh, t, dyv, mov, g, sc, g1v):
        r = _rms_stats(t, D)
        tn = t * r
        dxg = dh * (1.0 + sc)
        dt = dyv + _rms_bwd(dxg * g, tn, r, D)
        return (dt, g1v * dt), (_colsum(dh), _colsum(dh * (tn * g)), _colsum(dxg * tn), _colsum(dt * mov.astype(F32)))

    (dx1, dmo), (dsh2, dsc2, dn2g, dg1) = _rowwise(
        norm2_bwd_fn, [dh2, x1, dy, mo], [norm2_g, sc2, g1], [(D, F32), (D, BF16)], [(1, D)] * 4, name="norm2_bwd")

    def merge_bwd_epi(dm, a1, a2, gi1, gi2):
        s1, s2 = _sigmoid(gi1.astype(F32)), _sigmoid(gi2.astype(F32))
        a1, a2 = a1.astype(F32), a2.astype(F32)
        return dm * s1, dm * s2, dm * a1 * (s1 * (1.0 - s1)), dm * a2 * (s2 * (1.0 - s2))

    da1, da2, dgi1, dgi2 = _mm([(dmo, w_out_f)], tb=True, name="out_proj_dx", outs=(BF16,) * 4, tn=512,
                               extras=[(a1, "mn"), (a2, "mn"), (g1_in, "mn"), (g2_in, "mn")], epi=merge_bwd_epi)
    dw_out = _mm([(merged, dmo)], ta=True, name="out_proj_dw", outs=(BF16,))
    dattn = _mm([(da1, w_bra)], tb=True, name="br_attn_dx", outs=(BF16,))
    dw_bra = _mm([(attn_o, da1)], ta=True, name="br_attn_dw", outs=(BF16,), split=w_br_attn.shape[-1])
    dsgu = _mm([(da2, w_brs)], tb=True, name="br_sgu_dx", outs=(BF16,))
    dw_brs = _mm([(sgu_o, da2)], ta=True, name="br_sgu_dw", outs=(BF16,), split=w_br_sgu.shape[-1])

    def sgu_bwd_fn(dso, u_in, v_in, ng, nb, ws, bst):
        u, xhat, rs, vnb = sgu_parts(u_in, v_in, ng, nb)
        dso = dso.astype(F32)
        lane = lax.broadcasted_iota(jnp.int32, (CH, LANES), 1)
        du, dvn, dws, dbs = [], [], [], jnp.zeros((CH, LANES), F32)
        for g in range(G):
            sl = slice(g * GD, (g + 1) * GD)
            wg = ws[g].astype(BF16)
            mixed = jnp.dot(wg, vnb[:, sl], preferred_element_type=F32) + bst[:, g:g + 1]
            du.append(dso[:, sl] * mixed)
            dmix = dso[:, sl] * u[:, sl]
            dmb = dmix.astype(BF16)
            dws.append(lax.dot_general(dmb, vnb[:, sl], (((1,), (1,)), ((), ())), preferred_element_type=F32))
            dbs = dbs + jnp.where(lane == g, jnp.sum(dmix, axis=1, keepdims=True), 0.0)
            dvn.append(lax.dot_general(wg, dmb, (((0,), (0,)), ((), ())), preferred_element_type=F32))
        du, dvn = jnp.concatenate(du, axis=1), jnp.concatenate(dvn, axis=1)
        dxh = dvn * ng
        dv = rs * (dxh - jnp.mean(dxh, axis=-1, keepdims=True) - xhat * jnp.mean(dxh * xhat, axis=-1, keepdims=True))
        return ((du * _gelu_grad(u_in.astype(F32)), dv * _gelu_grad(v_in.astype(F32))),
                (_colsum(dvn * xhat), _colsum(dvn), jnp.stack(dws), dbs))

    core = jnp.reshape(ac, (1,)).astype(jnp.int32)

    def dest_layout(dwf, i):
        K, Ns = big[i].shape
        if dwf.ndim == 2:
            dwf = dwf.reshape(K, 4, Ns).transpose(1, 0, 2) if col_sharded[i] else dwf.reshape(4, K, Ns)
        return dwf.reshape(4, 2, K // 2, Ns)

    def pair_sums(idx, g4, sib):
        return [_pair_add(g, s, core, name="rs_pair_add_" + tags[i]) for g, s, i in zip(g4, sib, idx)]

    early = [3, 4, 5, 6, 7]
    g4_early = [dest_layout(d, i) for d, i in zip([dw_bra, dw_brs, dw_out, dw_ffi, dw_ffo], early)]
    (du_in, dv_in), (d_sng, d_snb, d_ws, d_bs), sib_early = _rowwise(
        sgu_bwd_fn, [dsgu, u_in, v_in], [sgu_norm_g, sgu_norm_b, ws3, bs_t], [(SW, BF16), (SW, BF16)],
        [(1, SW), (1, SW), (G, CH, CH), (CH, LANES)], name="sgu_bwd", tm=CH, carry=_PairExchange(g4_early, "halves"))
    pair_early = pair_sums(early, g4_early, sib_early)
    dq_att, dk_att, dv_att, xchg_early = _attn_bwd(q_att, k_att, v_att, attn_o, lse, dattn, heads=H,
                                                   carry=_ChipExchange(pair_early, gather=False))

    def q_post_bwd_fn(dq, t, cos, slo, shi, g):
        outs, dg = [], jnp.zeros((1, HEAD_PAD), F32)
        for hd in range(H):
            sl = slice(hd * HEAD_PAD, (hd + 1) * HEAD_PAD)
            th = t[:, sl]
            r = _rms_stats(th, QK)
            tn = th * r
            dn = _rope_t(dq[:, sl] * scale, cos, slo, shi)
            dg = dg + _colsum(dn * tn)
            outs.append(_rms_bwd(dn * g, tn, r, QK))
        return (jnp.concatenate(outs, axis=1),), (dg,)

    (dq_raw,), (d_gq,) = _rowwise(q_post_bwd_fn, [dq_att, q_raw, cos_t, slo_t, shi_t], [gq_p],
                                  [(H * HEAD_PAD, BF16)], [(1, HEAD_PAD)], name="q_post_bwd")

    def k_post_bwd_fn(dk, dv, t, kvi, cos, slo, shi, g):
        kr = kvi[:, KVL:]
        outs, dg, dkr = [], jnp.zeros((1, HEAD_PAD), F32), jnp.zeros_like(kr)
        for hd in range(H):
            th = jnp.concatenate([t[:, hd * HEAD_PAD:hd * HEAD_PAD + NOPE], kr], axis=1)
            r = _rms_stats(th, QK)
            tn = th * r
            dn = _rope_t(dk[:, hd * HEAD_PAD:(hd + 1) * HEAD_PAD], cos, slo, shi)
            dg = dg + _colsum(dn * tn)
            dt = _rms_bwd(dn * g, tn, r, QK)
            dkr = dkr + dt[:, NOPE:]
            outs += [dt[:, :NOPE], dv[:, hd * VD:(hd + 1) * VD]]
        return (jnp.concatenate(outs, axis=1), dkr), (dg,)

    (dkv_raw, dkrope), (d_gk,) = _rowwise(
        k_post_bwd_fn, [dk_att, dv_att, kv_raw, kvin, cos_k, slo_k, shi_k], [gk_p],
        [(H * HEAD_PAD, BF16), (2 * ROPE_DIM, F32)], [(1, HEAD_PAD)], name="k_post_bwd")

    dqn = _mm([(dq_raw, w_uq_p)], tb=True, name="q_up_dx", outs=(F32,))
    dw_uq_p = _mm([(qn, dq_raw)], ta=True, name="q_up_dw", outs=(BF16,))
    dkvn = _mm([(dkv_raw, w_ukv_f)], tb=True, name="kv_up_dx", outs=(F32,))
    dw_ukv = _mm([(kvn, dkv_raw)], ta=True, name="kv_up_dw", outs=(BF16,))

    def q_norm_bwd_fn(dn, t, g):
        r = _rms_stats(t, QL)
        tn = t * r
        return (_rms_bwd(dn * g, tn, r, QL),), (_colsum(dn * tn),)

    (dqc,), (d_qng,) = _rowwise(q_norm_bwd_fn, [dqn, qc], [q_norm_g], [(QL, BF16)], [(1, QL)], name="q_norm_bwd")

    def kv_norm_bwd_fn(dn, dkr, t, g):
        kvc = t[:, :KVL]
        r = _rms_stats(kvc, KVL)
        tn = kvc * r
        return (jnp.concatenate([_rms_bwd(dn * g, tn, r, KVL), dkr], axis=1),), (_colsum(dn * tn),)

    (dkvin,), (d_kvng,) = _rowwise(kv_norm_bwd_fn, [dkvn, dkrope, kvin], [kv_norm_g], [(KVP, BF16)], [(1, KVL)],
                                   name="kv_norm_bwd")
    dkvin_x, dkvin_c = dkvin[:N], dkvin[N:]

    dctx_h = _mm([(dkvin_c, w_kv)], tb=True, name="proj_kv_ctx_dx", outs=(F32,))
    dw_q = _mm([(h, dqc)], ta=True, name="proj_q_dw", outs=(BF16,))
    dw_kv = _mm([(h, dkvin_x), (ctx_h, dkvin_c)], ta=True, name="proj_kv_dw", outs=(BF16,))
    dw_u = _mm([(h, du_in)], ta=True, name="proj_u_dw", outs=(BF16,))
    dw_v = _mm([(h, dv_in)], ta=True, name="proj_v_dw", outs=(BF16,))
    dw_g1 = _mm([(h, dgi1)], ta=True, name="proj_g1_dw", outs=(BF16,))
    dw_g2 = _mm([(h, dgi2)], ta=True, name="proj_g2_dw", outs=(BF16,))

    dw_in_f = jnp.concatenate([dw_q, dw_kv[:, :KVL + ROPE_DIM], dw_u, dw_v, dw_g1, dw_g2], axis=1)
    dw_uq_f = dw_uq_p.reshape(QL, H, HEAD_PAD)[:, :, :QK].reshape(QL, H * QK)
    late = [0, 1, 2]
    g4_late = [dest_layout(d, i) for d, i in zip([dw_in_f, dw_uq_f, dw_ukv], late)]
    pair_late = pair_sums(late, g4_late, _exchange_alone(_PairExchange(g4_late, "halves"), name="rs_pair_late"))
    dh, xchg_late = _mm([(dqc, w_q), (dkvin_x, w_kv), (du_in, w_u), (dv_in, w_v), (dgi1, w_g1), (dgi2, w_g2)],
                        tb=True, name="proj_dx", outs=(F32,), tn=512, tk=1024,
                        carry=_ChipExchange(pair_late, gather=False))

    def norm1_bwd_fn(dhv, t, dres, g, sc):
        r = _rms_stats(t, D)
        tn = t * r
        dxg = dhv * (1.0 + sc)
        return (dres + _rms_bwd(dxg * g, tn, r, D),), (_colsum(dhv), _colsum(dhv * (tn * g)), _colsum(dxg * tn))

    (grad_x,), (dsh1, dsc1, dn1g_x) = _rowwise(norm1_bwd_fn, [dh, x2, dx1], [norm1_g, sc1], [(D, F32)], [(1, D)] * 3,
                                               name="norm1_bwd")
    _, (dsh1c, dsc1c, dn1g_c) = _rowwise(norm1_bwd_fn, [dctx_h, ctx2, jnp.zeros_like(ctx2)], [norm1_g, sc1c],
                                         [(D, F32)], [(1, D)] * 3, name="norm1_ctx_bwd")

    small = [dsh1, dsc1, dg1, dsh2, dsc2, dg2,
             dsh1c, dsc1c, dn1g_x, dn1g_c, d_qng, d_kvng, d_gq, d_gk, d_sng, d_snb, dn2g, loss_cols]
    small_sizes = [a.shape[1] for a in small]
    sm_row = jnp.concatenate(small, axis=1)
    sm_mat = jnp.concatenate([d_ws.reshape(G * CH, CH), d_bs], axis=0)
    row_all, mat_all = _all_gather8([sm_row, sm_mat], name="ag_small", in_vmem=True)
    row_sum = _sum_blocks(row_all, name="sum_small_rows", out_dtype=F32)
    mat_sum = _sum_blocks(mat_all, name="sum_small_mats", out_dtype=F32)
    dmod_rows = row_all[:, 0, :NMOD]
    (_, _, _, _, _, _, t_sh1c, t_sc1c, t_n1x, t_n1c, g_qng, g_kvng, t_gq, t_gk, g_sng, g_snb, g_n2g,
     t_loss) = _split_lanes(row_sum, small_sizes)
    g_ws, t_bs = mat_sum[:G * CH], mat_sum[G * CH:]
    dmodc_row = jnp.concatenate([t_sh1c, t_sc1c, jnp.zeros((1, NMOD - 2 * D), F32)], axis=1)
    dmod16 = jnp.concatenate([dmod_rows, dmodc_row, jnp.zeros((BF16_SUBLANES - 9, NMOD), F32)], axis=0)

    def small_fn(rows, n1x, n1c, lossv):
        return (), (_colsum(rows), n1x + n1c, jnp.sum(lossv, axis=1, keepdims=True))

    _, (g_bmod, g_n1g, loss11) = _rowwise(small_fn, [dmod16], [t_n1x, t_n1c, t_loss], [], [(1, NMOD), (1, D), (1, 1)],
                                          name="small_reduce", tm=16)
    dmod_loc = lax.dynamic_slice_in_dim(dmod16, my_chip * NM, NM, axis=1)
    g_wmod = _mm([(silu_c, dmod_loc)], ta=True, name="mod_dw", outs=(F32,), tn=512)
    dsilu_part = _mm([(dmod_loc, wm)], tb=True, name="mod_dx", outs=(F32,), tk=512)
    part_all = _all_gather8([dsilu_part[8:9]], name="ag_cctx", in_vmem=True)[0]

    def cctx_fn(parts, dsl):
        return (), ((parts[0:1] + parts[2:3] + parts[4:5] + parts[6:7]) * dsl,)

    _, (g_cctx,) = _rowwise(cctx_fn, [part_all[:, 0, :]], [dsilu_c[8:9]], [], [(1, D)], name="cctx_grad", tm=8)

    pair = pair_late + pair_early
    xchg = [lax.dynamic_update_index_in_dim(t4, lax.dynamic_index_in_dim(pr, my_chip, 0, keepdims=False), my_chip, 0)
            for t4, pr in zip(xchg_late + xchg_early, pair)]
    red_half = [_sum_blocks(t4, name="rs_sum_" + t, out_dtype=F32) for t4, t in zip(xchg, tags)]
    mod_upd = _adamw(w_mod[0], g_wmod, m_w_mod[0], v_w_mod[0], name="adamw_w_mod",
                     carry=_PairExchange(red_half, "gather"))
    big_grads = [lax.dynamic_update_index_in_dim(r, mine, ac, 0).reshape(a.shape)
                 for r, mine, a in zip(mod_upd[3], red_half, big)]

    def upd(w, g, m, v, nm):
        shape = w.shape
        w2, g2_, m2, v2 = [t.reshape(-1, shape[-1]) for t in (w, g, m, v)]
        d_, m_, v_ = _adamw(w2, g2_, m2, v2, name="adamw_" + nm)
        return g.reshape(shape), d_.reshape(shape), m_.reshape(shape), v_.reshape(shape)

    g_in, g_uq, g_ukv, g_bra, g_brs, g_out, g_ffi, g_ffo = big_grads
    grads = dict(
        c_ctx=g_cctx.reshape(D), w_mod=g_wmod[None], b_mod=g_bmod, norm1_g=g_n1g, w_in=g_in[None],
        q_norm_g=g_qng, kv_norm_g=g_kvng, w_uq=g_uq[None], w_ukv=g_ukv[None],
        qk_norm_q=t_gq[:, :QK], qk_norm_k=t_gk[:, :QK], sgu_norm_g=g_sng, sgu_norm_b=g_snb,
        w_spatial=g_ws.reshape(w_spatial.shape), b_spatial=t_bs[:, :G].T[None],
        w_br_attn=g_bra[None], w_br_sgu=g_brs[None], w_out=g_out[None], norm2_g=g_n2g,
        w_ffn_in=g_ffi[None], w_ffn_out=g_ffo[None])
    weights = dict(c_ctx=c_ctx, w_mod=w_mod, b_mod=b_mod, norm1_g=norm1_g, w_in=w_in, q_norm_g=q_norm_g,
                   kv_norm_g=kv_norm_g, w_uq=w_uq, w_ukv=w_ukv, qk_norm_q=qk_norm_q, qk_norm_k=qk_norm_k,
                   sgu_norm_g=sgu_norm_g, sgu_norm_b=sgu_norm_b, w_spatial=w_spatial, b_spatial=b_spatial,
                   w_br_attn=w_br_attn, w_br_sgu=w_br_sgu, w_out=w_out, norm2_g=norm2_g, w_ffn_in=w_ffn_in,
                   w_ffn_out=w_ffn_out)
    m_in = dict(c_ctx=m_c_ctx, w_mod=m_w_mod, b_mod=m_b_mod, norm1_g=m_norm1_g, w_in=m_w_in, q_norm_g=m_q_norm_g,
                kv_norm_g=m_kv_norm_g, w_uq=m_w_uq, w_ukv=m_w_ukv, qk_norm_q=m_qk_norm_q, qk_norm_k=m_qk_norm_k,
                sgu_norm_g=m_sgu_norm_g, sgu_norm_b=m_sgu_norm_b, w_spatial=m_w_spatial, b_spatial=m_b_spatial,
                w_br_attn=m_w_br_attn, w_br_sgu=m_w_br_sgu, w_out=m_w_out, norm2_g=m_norm2_g, w_ffn_in=m_w_ffn_in,
                w_ffn_out=m_w_ffn_out)
    v_in_ = dict(c_ctx=v_c_ctx, w_mod=v_w_mod, b_mod=v_b_mod, norm1_g=v_norm1_g, w_in=v_w_in, q_norm_g=v_q_norm_g,
                 kv_norm_g=v_kv_norm_g, w_uq=v_w_uq, w_ukv=v_w_ukv, qk_norm_q=v_qk_norm_q, qk_norm_k=v_qk_norm_k,
                 sgu_norm_g=v_sgu_norm_g, sgu_norm_b=v_sgu_norm_b, w_spatial=v_w_spatial, b_spatial=v_b_spatial,
                 w_br_attn=v_w_br_attn, w_br_sgu=v_w_br_sgu, w_out=v_w_out, norm2_g=v_norm2_g, w_ffn_in=v_w_ffn_in,
                 w_ffn_out=v_w_ffn_out)
    names = list(weights)
    big_names = ("w_mod", "w_in", "w_uq", "w_ukv", "w_br_attn", "w_br_sgu", "w_out", "w_ffn_in", "w_ffn_out")
    out_g, out_d, out_m, out_v = {}, {}, {}, {}
    out_g["w_mod"] = grads["w_mod"]
    out_d["w_mod"], out_m["w_mod"], out_v["w_mod"] = [t[None] for t in mod_upd[:3]]
    for nm in big_names[1:]:
        out_g[nm], out_d[nm], out_m[nm], out_v[nm] = upd(weights[nm], grads[nm], m_in[nm], v_in_[nm], nm)
    row_names = [nm for nm in names if nm not in big_names and nm not in ("w_spatial", "b_spatial")]
    widths = [-(-weights[nm].size // LANES) * LANES for nm in row_names]

    def as_row(d):
        return jnp.concatenate([jnp.pad(d[nm].reshape(1, -1), ((0, 0), (0, wd - d[nm].size)))
                                for nm, wd in zip(row_names, widths)], axis=1)

    def as_mat(d):
        return jnp.concatenate([d["w_spatial"].reshape(G * CH, CH), d["b_spatial"].reshape(G, CH)], axis=0)

    row_res = _adamw(as_row(weights), as_row(grads), as_row(m_in), as_row(v_in_), name="adamw_rows")
    mat_res = _adamw(as_mat(weights), as_mat(grads), as_mat(m_in), as_mat(v_in_), name="adamw_spatial")
    for tgt, row, mat in zip((out_d, out_m, out_v), row_res, mat_res):
        for nm, seg in zip(row_names, _split_lanes(row, widths)):
            tgt[nm] = seg[:, :weights[nm].size].reshape(weights[nm].shape)
        tgt["w_spatial"] = mat[:G * CH].reshape(w_spatial.shape)
        tgt["b_spatial"] = mat[G * CH:].reshape(b_spatial.shape)
    for nm in row_names + ["w_spatial", "b_spatial"]:
        out_g[nm] = grads[nm].reshape(weights[nm].shape)

    loss = loss11.reshape(())
    return (loss, grad_x[None], *[out_g[n] for n in names], *[out_d[n] for n in names],
            *[out_m[n] for n in names], *[out_v[n] for n in names])
```

```python
import math

import jax
import jax.numpy as jnp
from jax import lax
from jax.experimental import pallas as pl
from jax.experimental.pallas import tpu as pltpu

F32, BF16 = jnp.float32, jnp.bfloat16
MESH = pl.DeviceIdType.MESH

LANES = 128
BF16_SUBLANES = 16
VMEM_LIMIT_BYTES = 56 * 1024 * 1024

EPS = 1e-6
ROPE_DIM = 64
ROPE_THETA = 10000.0
GRID_W = 64
HEAD_PAD = 256
ADAM_LR, ADAM_B1, ADAM_B2, ADAM_EPS, ADAM_WD, ADAM_STEP = 0.001, 0.9, 0.999, 1e-08, 0.01, 10


def _tile(dim, pref, align=LANES):
    if dim <= pref:
        return dim
    t = (pref // align) * align
    while t >= align:
        if dim % t == 0:
            return t
        t -= align
    return dim


def _params(sem=None):
    return pltpu.CompilerParams(dimension_semantics=sem, vmem_limit_bytes=VMEM_LIMIT_BYTES)


def _sds(shape, dtype):
    return jax.ShapeDtypeStruct(tuple(shape), dtype)


def _mm(pairs, *, name, ta=False, tb=False, outs=(F32,), tm=1024, tn=1024, tk=2048, extras=(), epi=None,
        split=None, into=None, carry=None, col_sums=0):
    dual = len(pairs[0]) == 3
    a0, b0 = pairs[0][0], pairs[0][1]
    M = a0.shape[1] if ta else a0.shape[0]
    N = b0.shape[0] if tb else b0.shape[1]
    tm, tn = _tile(M, tm), _tile(N if split is None else split, tn)
    ks = [(p[0].shape[0] if ta else p[0].shape[1]) for p in pairs]
    tks = [_tile(k, tk) for k in ks]
    nks = [k // t for k, t in zip(ks, tks)]
    offs = [sum(nks[:i]) for i in range(len(pairs))]
    nk_total = sum(nks)
    single = len(pairs) == 1

    def kidx(kk, p):
        return kk if single else jnp.clip(kk - offs[p], 0, nks[p] - 1)

    in_specs, operands = [], []
    for p, pr in enumerate(pairs):
        if ta:
            in_specs.append(pl.BlockSpec((tks[p], tm), lambda i, j, kk, p=p: (kidx(kk, p), i)))
        else:
            in_specs.append(pl.BlockSpec((tm, tks[p]), lambda i, j, kk, p=p: (i, kidx(kk, p))))
        operands.append(pr[0])
        for b in pr[1:]:
            if tb:
                in_specs.append(pl.BlockSpec((tn, tks[p]), lambda i, j, kk, p=p: (j, kidx(kk, p))))
            else:
                in_specs.append(pl.BlockSpec((tks[p], tn), lambda i, j, kk, p=p: (kidx(kk, p), j)))
            operands.append(b)
    for arr, kind in extras:
        if kind == "mn":
            in_specs.append(pl.BlockSpec((tm, tn), lambda i, j, kk: (i, j)))
        else:
            in_specs.append(pl.BlockSpec((1, tn), lambda i, j, kk: (0, j)))
        operands.append(arr)
    n_in = len(operands)
    n_ex = len(extras)
    per = 3 if dual else 2
    dims = (((0 if ta else 1,), (1 if tb else 0,)), ((), ()))

    n_acc = 2 if dual else 1

    def products(ins, p):
        a = ins[per * p][...].astype(BF16)
        return [lax.dot_general(a, ins[per * p + 1 + q][...].astype(BF16), dims, preferred_element_type=F32)
                for q in range(n_acc)]

    def finish(ins, out_refs, acc_vals):
        vals = acc_vals + [r[...] for r in ins[n_in - n_ex:]]
        res = epi(*vals) if epi is not None else (vals[0],)
        for o, r in zip(out_refs, res):
            o[...] = jnp.broadcast_to(r, o.shape).astype(o.dtype)

    out_specs = [pl.BlockSpec((tm, tn), lambda i, j, kk: (i, j)) for _ in outs]
    out_specs += [pl.BlockSpec((None, 8, tn), lambda i, j, kk: (i, 0, j)) for _ in range(col_sums)]
    out_shape = [_sds((M, N), d) for d in outs] + [_sds((M // tm, 8, N), F32) for _ in range(col_sums)]
    aliases = {}
    n_alias = 0
    if split is not None:
        nps = split // tn
        lead = 0 if into is None else into[1]
        out_specs = [pl.BlockSpec((None, tm, tn), lambda i, j, kk: (j // nps + lead, i, j % nps))]
        out_shape = [_sds((N // split if into is None else into[0].shape[0], M, split), outs[0])]
        if into is not None:
            in_specs.append(pl.BlockSpec(memory_space=pl.ANY))
            operands.append(into[0])
            aliases, n_alias = {n_in: 0}, 1

    grid = (M // tm, N // tn, nk_total)
    n_out = len(outs) + col_sums

    def at_step(first):
        ids = [pl.program_id(d) for d in range(3)]
        cond = None
        for d, g in zip(ids, grid):
            t = d == (0 if first else g - 1)
            cond = t if cond is None else cond & t
        return cond

    def body(*refs):
        ins, out_refs, accs, start, wait = _split_refs(refs, n_in + n_alias, n_out, carry)
        ins = ins[:n_in]
        if carry is not None:
            pl.when(at_step(True))(start)
        if nk_total == 1:
            finish(ins, out_refs, products(ins, 0))
        else:
            kk = pl.program_id(2)

            @pl.when(kk == 0)
            def _():
                for acc, v in zip(accs, products(ins, 0)):
                    acc[...] = v

            for p in range(len(pairs)):
                lo = max(offs[p], 1)

                @pl.when((kk >= lo) & (kk < offs[p] + nks[p]))
                def _(p=p):
                    for acc, v in zip(accs, products(ins, p)):
                        acc[...] += v

            @pl.when(kk == nk_total - 1)
            def _():
                finish(ins, out_refs, [acc[...] for acc in accs])
        if carry is not None:
            pl.when(at_step(False))(wait)

    ex = carry
    res = pl.pallas_call(
        body, name=name, grid=grid, in_specs=in_specs + ([] if ex is None else ex.in_specs),
        out_specs=out_specs + ([] if ex is None else ex.out_specs),
        out_shape=out_shape + ([] if ex is None else ex.out_shape), input_output_aliases=aliases,
        scratch_shapes=[pltpu.VMEM((tm, tn), F32) for _ in range(n_acc if nk_total > 1 else 0)]
        + ([] if ex is None else ex.scratch),
        compiler_params=_params(("arbitrary",) * 3 if ex is not None else ("parallel", "parallel", "arbitrary")),
    )(*operands, *([] if ex is None else ex.xs))
    if ex is not None:
        return (res[0] if n_out == 1 else res[:n_out]), list(res[n_out:])
    return res[0] if n_out == 1 else res


def _rowwise(fn, rows, vecs, out_rows, out_accs=(), *, name, tm=256, tc=None, carry=None):
    M = rows[0].shape[0]
    tm = _tile(M, tm, BF16_SUBLANES)
    nrow = M // tm
    C = rows[0].shape[1]
    ncol = 1 if tc is None else C // _tile(C, tc)
    tcol = None if tc is None else _tile(C, tc)

    def colwise(shape):
        return tc is not None and len(shape) == 2 and shape[0] == 1 and shape[1] == C

    def vspec(shape):
        if colwise(shape):
            return pl.BlockSpec((1, tcol), lambda j, i: (0, j))
        return pl.BlockSpec(tuple(shape), lambda j, i, n=len(shape): (0,) * n)

    def rspec(width):
        if tc is None:
            return pl.BlockSpec((tm, width), lambda j, i: (i, 0))
        return pl.BlockSpec((tm, tcol), lambda j, i: (i, j))

    in_specs = [rspec(r.shape[1]) for r in rows] + [vspec(v.shape) for v in vecs]
    out_specs = [rspec(c) for c, _ in out_rows] + [vspec(s) for s in out_accs]
    out_shape = [_sds((M, c), d) for c, d in out_rows] + [_sds(s, F32) for s in out_accs]
    n_in, n_or = len(rows) + len(vecs), len(out_rows)

    n_out = n_or + len(out_accs)
    ex = carry

    def body(*refs):
        ins, outs, _, start, wait = _split_refs(refs, n_in, n_out, ex)
        o_rows, o_accs = outs[:n_or], outs[n_or:]
        if ex is not None:
            pl.when((pl.program_id(0) == 0) & (pl.program_id(1) == 0))(start)
        r_out, a_out = fn(*[r[...] for r in ins])
        for o, r in zip(o_rows, r_out):
            o[...] = r.astype(o.dtype)
        i = pl.program_id(1)

        @pl.when(i == 0)
        def _():
            for o, a in zip(o_accs, a_out):
                o[...] = a

        @pl.when(i > 0)
        def _():
            for o, a in zip(o_accs, a_out):
                o[...] += a

        if ex is not None:
            pl.when((pl.program_id(0) == ncol - 1) & (pl.program_id(1) == nrow - 1))(wait)

    res = pl.pallas_call(
        body, name=name, grid=(ncol, nrow), in_specs=in_specs + ([] if ex is None else ex.in_specs),
        out_specs=out_specs + ([] if ex is None else ex.out_specs),
        out_shape=out_shape + ([] if ex is None else ex.out_shape),
        scratch_shapes=[] if ex is None else ex.scratch,
        compiler_params=_params(("arbitrary", "arbitrary") if ex is not None else ("parallel", "arbitrary")),
    )(*rows, *vecs, *([] if ex is None else ex.xs))
    if ex is not None:
        return res[:n_or], res[n_or:n_out], list(res[n_out:])
    return res[:n_or], res[n_or:]


def _colsum(t):
    return jnp.sum(t, axis=0, keepdims=True)


def _gelu(t):
    return 0.5 * t * (1.0 + lax.erf(t * math.sqrt(0.5)))


def _gelu_grad(t):
    return 0.5 * (1.0 + lax.erf(t * math.sqrt(0.5))) + t * jnp.exp(-0.5 * t * t) * (1.0 / math.sqrt(2.0 * math.pi))


def _sigmoid(t):
    return 1.0 / (1.0 + jnp.exp(-t))


def _rms_stats(t, width):
    return lax.rsqrt(jnp.sum(t * t, axis=-1, keepdims=True) * (1.0 / width) + EPS)


def _rms_bwd(dn, tn, r, width):
    return r * (dn - tn * (jnp.sum(dn * tn, axis=-1, keepdims=True) * (1.0 / width)))


def _place():
    return lax.axis_index("x"), lax.axis_index("y"), lax.axis_index("c")


class _ChipExchange:
    def __init__(self, xs, gather):
        self.xs, self.gather, self.n = list(xs), gather, len(xs)
        self.in_specs = [pl.BlockSpec(memory_space=pl.ANY)] * self.n
        self.out_specs = [pl.BlockSpec(memory_space=pl.ANY)] * self.n
        self.out_shape = [_sds((4,) + (x.shape if gather else x.shape[1:]), x.dtype) for x in self.xs]
        self.scratch = [pltpu.SemaphoreType.DMA((self.n, 3)), pltpu.SemaphoreType.DMA((self.n, 3))]

    def bind(self, x_refs, out_refs, send_sems, recv_sems):
        x, y, c = _place()
        p = 2 * x + y
        chips = [(1 - x, y), (x, 1 - y), (1 - x, 1 - y)]

        def copy(w, k, outgoing):
            qx, qy = chips[k]
            there = 2 * qx + qy
            if self.gather:
                src = x_refs[w]
            else:
                src = x_refs[w].at[there if outgoing else p]
            return pltpu.make_async_remote_copy(
                src_ref=src, dst_ref=out_refs[w].at[p if outgoing else there], send_sem=send_sems.at[w, k],
                recv_sem=recv_sems.at[w, k], device_id=(qx, qy, c), device_id_type=MESH)

        def start():
            for w in range(self.n):
                for k in range(3):
                    copy(w, k, True).start()

        def wait():
            for w in range(self.n):
                for k in range(3):
                    copy(w, k, False).wait_recv()
            for w in range(self.n):
                for k in range(3):
                    copy(w, k, True).wait_send()

        return start, wait


class _PairExchange:
    def __init__(self, xs, mode):
        self.xs, self.mode, self.n = list(xs), mode, len(xs)
        self.in_specs = [pl.BlockSpec(memory_space=pl.ANY)] * self.n
        self.out_specs = [pl.BlockSpec(memory_space=pl.ANY)] * self.n
        shape = {"halves": lambda s: (4,) + s[2:], "forward": lambda s: s, "gather": lambda s: (2,) + s}[mode]
        self.out_shape = [_sds(shape(x.shape), x.dtype) for x in self.xs]
        self.scratch = [pltpu.SemaphoreType.DMA((self.n, 3)), pltpu.SemaphoreType.DMA((self.n, 3))]

    def bind(self, x_refs, out_refs, send_sems, recv_sems):
        x, y, c = _place()
        chips = [(1 - x, y), (x, 1 - y), (1 - x, 1 - y)]

        def copy(w, src, dst, k):
            return pltpu.make_async_remote_copy(src_ref=src, dst_ref=dst, send_sem=send_sems.at[w, k],
                                                recv_sem=recv_sems.at[w, k], device_id=(x, y, 1 - c),
                                                device_id_type=MESH)

        def start():
            for w, (xr, orf) in enumerate(zip(x_refs, out_refs)):
                if self.mode == "halves":
                    for q in range(4):
                        copy(w, xr.at[q, 1 - c], orf.at[q], 0).start()
                elif self.mode == "forward":
                    for k, (qx, qy) in enumerate(chips):
                        copy(w, xr.at[2 * qx + qy], orf.at[2 * qx + qy], k).start()
                else:
                    copy(w, xr, orf.at[c], 0).start()

        def wait():
            for w, (xr, orf) in enumerate(zip(x_refs, out_refs)):
                if self.mode == "halves":
                    copy(w, orf, orf, 0).wait()
                elif self.mode == "forward":
                    for k, (qx, qy) in enumerate(chips):
                        copy(w, xr.at[2 * qx + qy], orf.at[2 * qx + qy], k).wait()
                else:
                    cp = copy(w, xr, orf.at[1 - c], 0)
                    cp.wait_recv()
                    cp.wait_send()

        return start, wait


def _split_refs(refs, n_in, n_out, ex):
    ne = 0 if ex is None else ex.n
    ins, xin = refs[:n_in], refs[n_in:n_in + ne]
    outs, xout = refs[n_in + ne:n_in + ne + n_out], refs[n_in + ne + n_out:n_in + 2 * ne + n_out]
    rest = refs[n_in + 2 * ne + n_out:]
    if ex is None:
        return ins, outs, rest, None, None
    start, wait = ex.bind(xin, xout, rest[-2], rest[-1])
    return ins, outs, rest[:-2], start, wait


def _attn_fwd(q, k, v, *, heads, tq=512, carry=None):
    N, M = q.shape[0], k.shape[0]
    tq = _tile(N, tq)
    sub = _tile(tq, 256)
    vd = v.shape[1] // heads
    nq = N // tq

    def body(*refs):
        (q_ref, k_ref, v_ref), (o_ref, lse_ref), _, start, wait = _split_refs(refs, 3, 2, carry)
        if carry is not None:
            pl.when((pl.program_id(0) == 0) & (pl.program_id(1) == 0))(start)
        for sb in range(tq // sub):
            rows = pl.ds(sb * sub, sub)
            s = lax.dot_general(q_ref[rows, :], k_ref[...], (((1,), (1,)), ((), ())), preferred_element_type=F32)
            m = jnp.max(s, axis=-1, keepdims=True)
            p = jnp.exp(s - m)
            l = jnp.sum(p, axis=-1, keepdims=True)
            o = jnp.dot(p.astype(BF16), v_ref[...], preferred_element_type=F32) / l
            o_ref[rows, :] = o.astype(o_ref.dtype)
            lse_ref[rows, :] = jnp.broadcast_to(m + jnp.log(l), (sub, vd))
        if carry is not None:
            pl.when((pl.program_id(0) == heads - 1) & (pl.program_id(1) == nq - 1))(wait)

    ex = carry
    res = pl.pallas_call(
        body, name="attn_fwd", grid=(heads, nq),
        in_specs=[pl.BlockSpec((tq, HEAD_PAD), lambda h, i: (i, h)),
                  pl.BlockSpec((M, HEAD_PAD), lambda h, i: (0, h)),
                  pl.BlockSpec((M, vd), lambda h, i: (0, h))] + ([] if ex is None else ex.in_specs),
        out_specs=[pl.BlockSpec((tq, vd), lambda h, i: (i, h)),
                   pl.BlockSpec((tq, vd), lambda h, i: (i, h))] + ([] if ex is None else ex.out_specs),
        out_shape=[_sds((N, heads * vd), BF16), _sds((N, heads * vd), F32)] + ([] if ex is None else ex.out_shape),
        scratch_shapes=[] if ex is None else ex.scratch,
        compiler_params=_params(("arbitrary", "arbitrary")),
    )(q, k, v, *([] if ex is None else ex.xs))
    return res[0], res[1], list(res[2:])


def _attn_bwd(q, k, v, o, lse, do, *, heads, tq=512, carry=None):
    N, M = q.shape[0], k.shape[0]
    tq = _tile(N, tq)
    vd = v.shape[1] // heads
    nq = N // tq
    sub = _tile(tq, 256)
    nt = (((1,), (1,)), ((), ()))
    tn = (((0,), (0,)), ((), ()))

    def body(*refs):
        (q_ref, k_ref, v_ref, o_ref, lse_ref, do_ref), (dq_ref, dk_ref, dv_ref), _, start, wait = _split_refs(
            refs, 6, 3, carry)
        if carry is not None:
            pl.when((pl.program_id(0) == 0) & (pl.program_id(1) == 0))(start)
        i = pl.program_id(1)
        kb, vb = k_ref[...], v_ref[...]
        parts = []
        for sb in range(tq // sub):
            rows = pl.ds(sb * sub, sub)
            qb, dob = q_ref[rows, :], do_ref[rows, :]
            delta = jnp.sum(dob.astype(F32) * o_ref[rows, :].astype(F32), axis=-1, keepdims=True)
            s = lax.dot_general(qb, kb, nt, preferred_element_type=F32)
            p = jnp.exp(s - lse_ref[rows, :][:, :1])
            dp = lax.dot_general(dob, vb, nt, preferred_element_type=F32)
            ds = (p * (dp - delta)).astype(BF16)
            dq_ref[rows, :] = jnp.dot(ds, kb, preferred_element_type=F32)
            parts.append((lax.dot_general(ds, qb, tn, preferred_element_type=F32),
                          lax.dot_general(p.astype(BF16), dob, tn, preferred_element_type=F32)))

        @pl.when(i == 0)
        def _():
            dk_ref[...] = parts[0][0]
            dv_ref[...] = parts[0][1]

        @pl.when(i > 0)
        def _():
            dk_ref[...] += parts[0][0]
            dv_ref[...] += parts[0][1]

        for dk_part, dv_part in parts[1:]:
            dk_ref[...] += dk_part
            dv_ref[...] += dv_part

        if carry is not None:
            pl.when((pl.program_id(0) == heads - 1) & (pl.program_id(1) == nq - 1))(wait)

    ex = carry
    res = pl.pallas_call(
        body, name="attn_bwd", grid=(heads, nq),
        in_specs=[pl.BlockSpec((tq, HEAD_PAD), lambda h, i: (i, h)),
                  pl.BlockSpec((M, HEAD_PAD), lambda h, i: (0, h)),
                  pl.BlockSpec((M, vd), lambda h, i: (0, h)),
                  pl.BlockSpec((tq, vd), lambda h, i: (i, h)),
                  pl.BlockSpec((tq, vd), lambda h, i: (i, h)),
                  pl.BlockSpec((tq, vd), lambda h, i: (i, h))] + ([] if ex is None else ex.in_specs),
        out_specs=[pl.BlockSpec((tq, HEAD_PAD), lambda h, i: (i, h)),
                   pl.BlockSpec((M, HEAD_PAD), lambda h, i: (0, h)),
                   pl.BlockSpec((M, vd), lambda h, i: (0, h))] + ([] if ex is None else ex.out_specs),
        out_shape=[_sds((N, heads * HEAD_PAD), F32), _sds((M, heads * HEAD_PAD), F32),
                   _sds((M, heads * vd), F32)] + ([] if ex is None else ex.out_shape),
        scratch_shapes=[] if ex is None else ex.scratch,
        compiler_params=_params(("arbitrary", "arbitrary")),
    )(q, k, v, o, lse, do, *([] if ex is None else ex.xs))
    return res[0], res[1], res[2], list(res[3:])


def _comm_call(body, xs, out_shapes, n_sems, name, in_vmem):
    space = pltpu.VMEM if in_vmem else pl.ANY
    n = len(xs)

    def wrapped(*refs):
        body(refs[:n], refs[n:2 * n], *refs[2 * n:])

    return pl.pallas_call(
        wrapped, name=name, out_shape=list(out_shapes),
        in_specs=[pl.BlockSpec(memory_space=space)] * n, out_specs=[pl.BlockSpec(memory_space=space)] * n,
        scratch_shapes=[pltpu.SemaphoreType.DMA((n, n_sems)), pltpu.SemaphoreType.DMA((n, n_sems)),
                        pltpu.SemaphoreType.DMA((n,))],
        compiler_params=pltpu.CompilerParams(vmem_limit_bytes=VMEM_LIMIT_BYTES),
    )(*xs)


def _all_gather8(blks, *, name, in_vmem, others_only=False):
    def body(x_refs, out_refs, send_sems, recv_sems, local_sems):
        x, y, c = _place()
        me, sibling = (x, y, c), (x, y, 1 - c)
        chips = [(1 - x, y), (x, 1 - y), (1 - x, 1 - y)]
        waits = []
        for w, (x_ref, out_ref) in enumerate(zip(x_refs, out_refs)):
            def slot(px, py, pc, out_ref=out_ref):
                return out_ref.at[4 * px + 2 * py + pc]

            def copy(k, block, to, src=None, w=w, slot=slot):
                return pltpu.make_async_remote_copy(
                    src_ref=slot(*block) if src is None else src, dst_ref=slot(*block),
                    send_sem=send_sems.at[w, k], recv_sem=recv_sems.at[w, k], device_id=to, device_id_type=MESH)

            mine = None
            first = []
            if not others_only:
                mine = pltpu.make_async_copy(x_ref, slot(*me), local_sems.at[w])
                mine.start()
                first.append(copy(0, me, sibling, src=x_ref))
            first += [copy(1 + j, me, (*chip, c), src=x_ref) for j, chip in enumerate(chips)]
            for cp in first:
                cp.start()
            waits.append((copy, mine, first))
        for copy, mine, first in waits:
            passed = [copy(4 + j, (*chip, c), sibling) for j, chip in enumerate(chips)]
            for j, chip in enumerate(chips):
                copy(1 + j, (*chip, c), me).wait_recv()
                passed[j].start()
            if not others_only:
                copy(0, sibling, me).wait_recv()
            for j, chip in enumerate(chips):
                copy(4 + j, (*chip, 1 - c), me).wait_recv()
            for cp in first + passed:
                cp.wait_send()
            if mine is not None:
                mine.wait()

    return _comm_call(body, blks, [_sds((8,) + b.shape, b.dtype) for b in blks], 7, name, in_vmem)


def _exchange_alone(ex, *, name):
    def body(x_refs, out_refs, send_sems, recv_sems, local_sems):
        start, wait = ex.bind(x_refs, out_refs, send_sems, recv_sems)
        start()
        wait()

    return list(_comm_call(body, ex.xs, ex.out_shape, 3, name, False))


def _block_rows(rows, row_bytes, target=1 << 21, align=BF16_SUBLANES):
    return _tile(rows, max(align, target // row_bytes // align * align), align)


def _sum_blocks(buf, *, name, out_dtype):
    B, R, C = buf.shape
    tm = _block_rows(R, B * C * buf.dtype.itemsize)

    def body(x_ref, o_ref):
        acc = x_ref[0].astype(F32)
        for b in range(1, B):
            acc = acc + x_ref[b].astype(F32)
        o_ref[...] = acc.astype(o_ref.dtype)

    return pl.pallas_call(
        body, name=name, grid=(R // tm,), in_specs=[pl.BlockSpec((B, tm, C), lambda i: (0, i, 0))],
        out_specs=pl.BlockSpec((tm, C), lambda i: (i, 0)), out_shape=_sds((R, C), out_dtype),
        compiler_params=_params(("parallel",)),
    )(buf)


def _pair_add(mine, theirs, core, *, name):
    _, _, R, C = mine.shape
    tm = _block_rows(R, C * 2)

    def body(core_ref, a_ref, b_ref, o_ref):
        o_ref[...] = (a_ref[...].astype(F32) + b_ref[...].astype(F32)).astype(o_ref.dtype)

    return pl.pallas_call(
        body, name=name, out_shape=_sds(theirs.shape, BF16),
        grid_spec=pltpu.PrefetchScalarGridSpec(
            num_scalar_prefetch=1, grid=(4, R // tm),
            in_specs=[pl.BlockSpec((None, None, tm, C), lambda q, i, core_ref: (q, core_ref[0], i, 0)),
                      pl.BlockSpec((None, tm, C), lambda q, i, core_ref: (q, i, 0))],
            out_specs=pl.BlockSpec((None, tm, C), lambda q, i, core_ref: (q, i, 0))),
        compiler_params=_params(("parallel", "parallel")),
    )(core, mine, theirs)


def _assemble(gathered, own, chip, *, name, transpose):
    _, K, Ns = gathered.shape
    tm = _block_rows(K, Ns * 4)

    def body(chip_ref, g_ref, own_ref, o_ref):
        q = pl.program_id(0)

        @pl.when(q == chip_ref[0])
        def _():
            o_ref[...] = own_ref[...].astype(BF16)

        @pl.when(q != chip_ref[0])
        def _():
            o_ref[...] = g_ref[...]

    if transpose:
        out_spec = pl.BlockSpec((tm, Ns), lambda q, i, ch: (i, q))
        out_shape = _sds((K, 4 * Ns), BF16)
    else:
        out_spec = pl.BlockSpec((None, tm, Ns), lambda q, i, ch: (q, i, 0))
        out_shape = _sds((4, K, Ns), BF16)
    return pl.pallas_call(
        body, name=name, out_shape=out_shape,
        grid_spec=pltpu.PrefetchScalarGridSpec(
            num_scalar_prefetch=1, grid=(4, K // tm),
            in_specs=[pl.BlockSpec((None, tm, Ns), lambda q, i, ch: (jnp.where(q == ch[0], (q + 1) % 4, q), i, 0)),
                      pl.BlockSpec((tm, Ns), lambda q, i, ch: (jnp.where(q == ch[0], i, 0), 0))],
            out_specs=out_spec),
        compiler_params=_params(("arbitrary", "arbitrary")),
    )(chip, gathered, own)


def _assemble_halves(mine, theirs, own, place, *, name, transpose):
    _, K2, Ns = mine.shape
    tm = _block_rows(K2, Ns * 4)
    nb = K2 // tm

    def body(place_ref, m_ref, t_ref, own_ref, o_ref):
        q, hb = pl.program_id(0), pl.program_id(1)
        is_own = q == place_ref[0]
        is_mine = hb == place_ref[1]

        @pl.when(is_own)
        def _():
            o_ref[...] = own_ref[...].astype(BF16)

        @pl.when(jnp.logical_not(is_own) & is_mine)
        def _():
            o_ref[...] = m_ref[...]

        @pl.when(jnp.logical_not(is_own) & jnp.logical_not(is_mine))
        def _():
            o_ref[...] = t_ref[...]

    def other(q, pr):
        return jnp.where(q == pr[0], (q + 1) % 4, q)

    if transpose:
        out_spec = pl.BlockSpec((tm, Ns), lambda q, hb, i, pr: (hb * nb + i, q))
        out_shape = _sds((2 * K2, 4 * Ns), BF16)
    else:
        out_spec = pl.BlockSpec((None, tm, Ns), lambda q, hb, i, pr: (q, hb * nb + i, 0))
        out_shape = _sds((4, 2 * K2, Ns), BF16)
    return pl.pallas_call(
        body, name=name, out_shape=out_shape,
        grid_spec=pltpu.PrefetchScalarGridSpec(
            num_scalar_prefetch=1, grid=(4, 2, nb),
            in_specs=[pl.BlockSpec((None, tm, Ns), lambda q, hb, i, pr: (other(q, pr), jnp.where(hb == pr[1], i, 0), 0)),
                      pl.BlockSpec((None, tm, Ns), lambda q, hb, i, pr: (other(q, pr), jnp.where(hb == pr[1], 0, i), 0)),
                      pl.BlockSpec((tm, Ns), lambda q, hb, i, pr: (jnp.where(q == pr[0], hb * nb + i, 0), 0))],
            out_specs=out_spec),
        compiler_params=_params(("arbitrary",) * 3),
    )(place, mine, theirs, own)


def _split_lanes(row, widths):
    out, off = [], 0
    for wd in widths:
        out.append(row[:, off:off + wd])
        off += wd
    return out


def _adamw(w, g, m, v, *, name, carry=None):
    C = w.shape[1]

    def fn(w, g, m, v):
        m = ADAM_B1 * m + (1.0 - ADAM_B1) * g
        v = ADAM_B2 * v + (1.0 - ADAM_B2) * (g * g)
        m_hat = m / (1.0 - ADAM_B1 ** ADAM_STEP)
        v_hat = v / (1.0 - ADAM_B2 ** ADAM_STEP)
        delta = -ADAM_LR * (m_hat / (jnp.sqrt(v_hat) + ADAM_EPS) + ADAM_WD * w)
        return (delta, m, v), ()

    tm = max(8, min(512, (1 << 20) // (4 * C) // 8 * 8))
    res = _rowwise(fn, [w, g, m, v], [], [(C, F32)] * 3, name=name, tm=tm, carry=carry)
    return tuple(res[0]) + ((res[2],) if carry is not None else ())


def _rope_tables(n):
    rows = n // GRID_W
    row = jnp.repeat(jnp.arange(rows, dtype=F32), GRID_W)
    col = jnp.tile(jnp.arange(GRID_W, dtype=F32), rows)
    nf = ROPE_DIM // 4
    freqs = ROPE_THETA ** (-jnp.arange(nf, dtype=F32) / nf)
    ang_r, ang_c = row[:, None] * freqs[None, :], col[:, None] * freqs[None, :]
    cr, sr, cc, sc = jnp.cos(ang_r), jnp.sin(ang_r), jnp.cos(ang_c), jnp.sin(ang_c)
    nope = HEAD_PAD - 2 * ROPE_DIM
    one, zero, z = jnp.ones((n, nope), F32), jnp.zeros((n, nope), F32), jnp.zeros((n, nf), F32)
    pad = jnp.zeros((n, ROPE_DIM), F32)
    cos = jnp.concatenate([one, cr, cr, cc, cc, pad], axis=1)
    s_lo = jnp.concatenate([zero, -sr, z, -sc, z, pad], axis=1)
    s_hi = jnp.concatenate([zero, z, sr, z, sc, pad], axis=1)
    return cos, s_lo, s_hi


def _rope(n, cos, s_lo, s_hi):
    q = ROPE_DIM // 4
    return n * cos + pltpu.roll(n, HEAD_PAD - q, 1) * s_lo + pltpu.roll(n, q, 1) * s_hi


def _rope_t(d, cos, s_lo, s_hi):
    q = ROPE_DIM // 4
    return d * cos + pltpu.roll(d * s_lo, q, 1) + pltpu.roll(d * s_hi, HEAD_PAD - q, 1)


def kernel(x, c, ctx, c_ctx, w_mod, b_mod, norm1_g, w_in, q_norm_g, kv_norm_g, w_uq, w_ukv, qk_norm_q, qk_norm_k, sgu_norm_g, sgu_norm_b, w_spatial, b_spatial, w_br_attn, w_br_sgu, w_out, norm2_g, w_ffn_in, w_ffn_out, loss_target, m_c_ctx, m_w_mod, m_b_mod, m_norm1_g, m_w_in, m_q_norm_g, m_kv_norm_g, m_w_uq, m_w_ukv, m_qk_norm_q, m_qk_norm_k, m_sgu_norm_g, m_sgu_norm_b, m_w_spatial, m_b_spatial, m_w_br_attn, m_w_br_sgu, m_w_out, m_norm2_g, m_w_ffn_in, m_w_ffn_out, v_c_ctx, v_w_mod, v_b_mod, v_norm1_g, v_w_in, v_q_norm_g, v_kv_norm_g, v_w_uq, v_w_ukv, v_qk_norm_q, v_qk_norm_k, v_sgu_norm_g, v_sgu_norm_b, v_w_spatial, v_b_spatial, v_w_br_attn, v_w_br_sgu, v_w_out, v_norm2_g, v_w_ffn_in, v_w_ffn_out):
    ax, ay, ac = _place()
    my_chip = 2 * ax + ay
    my_dev = 4 * ax + 2 * ay + ac

    N, D = x.shape[1], x.shape[2]
    CT = ctx.shape[1]
    M = N + CT
    QL, KVL, QK = q_norm_g.shape[-1], kv_norm_g.shape[-1], qk_norm_q.shape[-1]
    NOPE = QK - ROPE_DIM
    VD = NOPE
    H = 4 * w_uq.shape[-1] // QK
    SW, G, CH = sgu_norm_g.shape[-1], w_spatial.shape[1], w_spatial.shape[2]
    GD = SW // G
    DFF = 4 * w_ffn_out.shape[1]
    NMOD = 4 * w_mod.shape[-1]
    NM = w_mod.shape[-1]
    KVP = KVL + 2 * ROPE_DIM
    assert NOPE == LANES and GD == LANES and HEAD_PAD == NOPE + 2 * ROPE_DIM and CH == LANES
    scale = QK ** -0.5

    x2, ctx2, tgt2 = x[0], ctx[0], loss_target[0]

    c_all = _all_gather8([c], name="ag_c", in_vmem=True)[0][:, 0, :]
    c_rows = jnp.concatenate([c_all, c_ctx[None, :], jnp.zeros((BF16_SUBLANES - 9, D), F32)], axis=0)

    def silu_fn(t):
        s = _sigmoid(t)
        return (t * s, s * (1.0 + t * (1.0 - s))), ()

    (silu_c, dsilu_c), _ = _rowwise(silu_fn, [c_rows], [], [(D, F32), (D, F32)], name="silu_c", tm=16)
    wm = w_mod[0]
    mod_loc = _mm([(silu_c, wm)], name="mod_fwd", outs=(F32,), tn=512, tk=512,
                  extras=[(lax.dynamic_slice_in_dim(b_mod, my_chip * NM, NM, axis=1), "n")],
                  epi=lambda acc, b: (acc + b,))
    mod_all = _all_gather8([mod_loc], name="ag_mod", in_vmem=True)[0]
    mod_full = jnp.concatenate([mod_all[0], mod_all[2], mod_all[4], mod_all[6]], axis=1)
    mod_me = lax.dynamic_slice_in_dim(mod_full, my_dev, 1, axis=0)
    sh1, sc1, g1, sh2, sc2, g2 = [mod_me[:, i * D:(i + 1) * D] for i in range(6)]
    sh1c, sc1c = mod_full[8:9, :D], mod_full[8:9, D:2 * D]

    big = [w_in[0], w_uq[0], w_ukv[0], w_br_attn[0], w_br_sgu[0], w_out[0], w_ffn_in[0], w_ffn_out[0]]
    col_sharded = [True, True, True, True, True, False, True, False]
    halves = [lax.dynamic_slice_in_dim(a, ac * (a.shape[0] // 2), a.shape[0] // 2, axis=0).astype(BF16) for a in big]
    tags = ["w_in", "w_uq", "w_ukv", "w_br_attn", "w_br_sgu", "w_out", "w_ffn_in", "w_ffn_out"]
    first_group, attn_group, ffn_group = [0, 1, 2], [3, 4, 5, 6], [7]
    chip1 = jnp.reshape(my_chip, (1,)).astype(jnp.int32)
    place2 = jnp.stack([my_chip, ac]).astype(jnp.int32)

    def laid_out(seg, i):
        a = big[i]
        if col_sharded[i] and seg.ndim == 3:
            return seg.transpose(1, 0, 2).reshape(a.shape[0], 4 * a.shape[1])
        return seg if col_sharded[i] else seg.reshape(4 * a.shape[0], a.shape[1])

    def side_by_side(i):
        return col_sharded[i] and big[i].shape[1] % LANES == 0

    def finish_gather(idx, mine4, theirs4):
        return [laid_out(_assemble_halves(m, t, big[i], place2, name="assemble_" + tags[i], transpose=side_by_side(i)), i)
                for i, m, t in zip(idx, mine4, theirs4)]

    gathered = _all_gather8([halves[i] for i in first_group], name="ag_weights", in_vmem=False, others_only=True)
    w_in_f, w_uq_f, w_ukv_f = [
        laid_out(_assemble(seg.reshape((4,) + big[i].shape), big[i], chip1, name="assemble_" + tags[i],
                           transpose=side_by_side(i)), i) for i, seg in zip(first_group, gathered)]
    o_kv, o_u = QL, QL + KVL + ROPE_DIM
    o_v, o_g = o_u + SW, o_u + 2 * SW
    w_q = w_in_f[:, :QL]
    w_kv = jnp.pad(w_in_f[:, o_kv:o_u], ((0, 0), (0, ROPE_DIM)))
    w_u, w_v = w_in_f[:, o_u:o_v], w_in_f[:, o_v:o_g]
    w_g1, w_g2 = w_in_f[:, o_g:o_g + D], w_in_f[:, o_g + D:]
    w_uq_p = jnp.pad(w_uq_f.reshape(QL, H, QK), ((0, 0), (0, 0), (0, HEAD_PAD - QK))).reshape(QL, H * HEAD_PAD)

    cos_t, slo_t, shi_t = _rope_tables(N)
    ones_c = jnp.concatenate([jnp.ones((CT, NOPE + ROPE_DIM), F32), jnp.zeros((CT, ROPE_DIM), F32)], axis=1)
    cos_k = jnp.concatenate([cos_t, ones_c], axis=0)
    slo_k = jnp.concatenate([slo_t, jnp.zeros((CT, HEAD_PAD), F32)], axis=0)
    shi_k = jnp.concatenate([shi_t, jnp.zeros((CT, HEAD_PAD), F32)], axis=0)
    gq_p = jnp.pad(qk_norm_q, ((0, 0), (0, HEAD_PAD - QK)))
    gk_p = jnp.pad(qk_norm_k, ((0, 0), (0, HEAD_PAD - QK)))

    def norm_mod_fn(t, g, sh, sc):
        r = _rms_stats(t, D)
        return (((t * r) * g) * (1.0 + sc) + sh,), ()

    (h,), _ = _rowwise(norm_mod_fn, [x2], [norm1_g, sh1, sc1], [(D, BF16)], name="norm1_x")
    (ctx_h,), _ = _rowwise(norm_mod_fn, [ctx2], [norm1_g, sh1c, sc1c], [(D, BF16)], name="norm1_ctx")

    qc = _mm([(h, w_q)], name="proj_q", outs=(F32,))
    kvin = jnp.concatenate([_mm([(h, w_kv)], name="proj_kv", outs=(F32,)),
                            _mm([(ctx_h, w_kv)], name="proj_kv_ctx", outs=(F32,))], axis=0)
    u_in = _mm([(h, w_u)], name="proj_u", outs=(BF16,))
    v_in = _mm([(h, w_v)], name="proj_v", outs=(BF16,))
    g1_in, (mine_bra,) = _mm([(h, w_g1)], name="proj_g1", outs=(BF16,), carry=_ChipExchange([halves[3]], gather=True))
    g2_in, (mine_brs,) = _mm([(h, w_g2)], name="proj_g2", outs=(BF16,), carry=_ChipExchange([halves[4]], gather=True))

    def rms_gain_fn(width):
        def fn(t, g):
            return (((t * _rms_stats(t, width)) * g),), ()
        return fn

    (qn,), _ = _rowwise(rms_gain_fn(QL), [qc], [q_norm_g], [(QL, BF16)], name="q_norm")

    def kv_norm_fn(t, g):
        kvc = t[:, :KVL]
        return (((kvc * _rms_stats(kvc, KVL)) * g),), ()

    (kvn,), _ = _rowwise(kv_norm_fn, [kvin], [kv_norm_g], [(KVL, BF16)], name="kv_norm")
    q_raw = _mm([(qn, w_uq_p)], name="q_up", outs=(F32,))
    kv_raw = _mm([(kvn, w_ukv_f)], name="kv_up", outs=(F32,))

    def q_post_fn(t, cos, slo, shi, g):
        outs = []
        for hd in range(H):
            th = t[:, hd * HEAD_PAD:(hd + 1) * HEAD_PAD]
            outs.append(_rope((th * _rms_stats(th, QK)) * g, cos, slo, shi) * scale)
        return (jnp.concatenate(outs, axis=1),), ()

    (q_att,), _ = _rowwise(q_post_fn, [q_raw, cos_t, slo_t, shi_t], [gq_p], [(H * HEAD_PAD, BF16)], name="q_post")

    def k_post_fn(t, kvi, cos, slo, shi, g):
        kr = kvi[:, KVL:]
        ks, vs = [], []
        for hd in range(H):
            th = jnp.concatenate([t[:, hd * HEAD_PAD:hd * HEAD_PAD + NOPE], kr], axis=1)
            ks.append(_rope((th * _rms_stats(th, QK)) * g, cos, slo, shi))
            vs.append(t[:, hd * HEAD_PAD + NOPE:(hd + 1) * HEAD_PAD])
        return (jnp.concatenate(ks, axis=1), jnp.concatenate(vs, axis=1)), ()

    (k_att, v_att), _, (mine_out,) = _rowwise(k_post_fn, [kv_raw, kvin, cos_k, slo_k, shi_k], [gk_p],
                                              [(H * HEAD_PAD, BF16), (H * VD, BF16)], name="k_post",
                                              carry=_ChipExchange([halves[5]], gather=True))
    attn_o, lse, (mine_ffi,) = _attn_fwd(q_att, k_att, v_att, heads=H, carry=_ChipExchange([halves[6]], gather=True))
    mine4 = [mine_bra, mine_brs, mine_out, mine_ffi]

    ws3 = w_spatial[0]
    bs_t = jnp.pad(b_spatial[0].T, ((0, 0), (0, LANES - G)))

    def sgu_parts(u_in, v_in, ng, nb):
        u, v = _gelu(u_in.astype(F32)), _gelu(v_in.astype(F32))
        mu = jnp.mean(v, axis=-1, keepdims=True)
        vc = v - mu
        rs = lax.rsqrt(jnp.mean(vc * vc, axis=-1, keepdims=True) + EPS)
        xhat = vc * rs
        return u, xhat, rs, (xhat * ng + nb).astype(BF16)

    def sgu_fwd_fn(u_in, v_in, ng, nb, ws, bst):
        u, _, _, vnb = sgu_parts(u_in, v_in, ng, nb)
        outs = []
        for g in range(G):
            sl = slice(g * GD, (g + 1) * GD)
            mixed = jnp.dot(ws[g].astype(BF16), vnb[:, sl], preferred_element_type=F32) + bst[:, g:g + 1]
            outs.append(u[:, sl] * mixed)
        return (jnp.concatenate(outs, axis=1),), ()

    (sgu_o,), _, theirs4 = _rowwise(sgu_fwd_fn, [u_in, v_in], [sgu_norm_g, sgu_norm_b, ws3, bs_t], [(SW, BF16)],
                                    name="sgu_fwd", tm=CH, carry=_PairExchange(mine4, "forward"))
    w_bra, w_brs, w_out_f, w_ffi = finish_gather(attn_group, mine4, theirs4)
    w_fa, w_fb = w_ffi[:, :DFF], w_ffi[:, DFF:]

    a1 = _mm([(attn_o, w_bra)], name="br_attn", outs=(BF16,))
    def merge_epi(acc, a1v, gi1, gi2):
        return acc, _sigmoid(gi1.astype(F32)) * a1v.astype(F32) + _sigmoid(gi2.astype(F32)) * acc

    a2, merged = _mm([(sgu_o, w_brs)], name="br_sgu", outs=(BF16, BF16),
                     extras=[(a1, "mn"), (g1_in, "mn"), (g2_in, "mn")], epi=merge_epi)

    def res_gate(acc, res, gate):
        return res + gate * acc, acc

    x1, mo = _mm([(merged, w_out_f)], name="out_proj", outs=(F32, BF16), tn=1024,
                 extras=[(x2, "mn"), (g1, "n")], epi=res_gate)
    (h2,), _ = _rowwise(norm_mod_fn, [x1], [norm2_g, sh2, sc2], [(D, BF16)], name="norm2")

    def swiglu_epi(a, b):
        return a, b, (a * _sigmoid(a)) * b

    (fa, fb, act), mine4 = _mm([(h2, w_fa, w_fb)], name="ffn_in", outs=(BF16, BF16, BF16), tn=512, epi=swiglu_epi,
                               carry=_ChipExchange([halves[i] for i in ffn_group], gather=True))
    (w_ffo,) = finish_gather(ffn_group, mine4, _exchange_alone(_PairExchange(mine4, "forward"), name="ag_forward_ffn"))
    def loss_epi(acc, res, t, gate):
        e = (res + gate * acc) - t
        dy = e * (1.0 / D)
        return dy, gate * dy, _colsum(e * e) * (0.5 / D), _colsum(dy * acc)

    dy, df, loss_part, dg2_part = _mm([(act, w_ffo)], name="ffn_out", outs=(F32, BF16), tn=1024, col_sums=2,
                                      extras=[(x1, "mn"), (tgt2, "mn"), (g2, "n")], epi=loss_epi)

    def fold_fn(a, b):
        return (), (_colsum(a), _colsum(b))

    _, (loss_cols, dg2) = _rowwise(fold_fn, [loss_part[:, 0, :], dg2_part[:, 0, :]], [], [], [(1, D), (1, D)],
                                   name="loss_fold", tm=loss_part.shape[0])

    def swiglu_bwd_epi(dact, a, b):
        a, b = a.astype(F32), b.astype(F32)
        s = _sigmoid(a)
        return dact * b * (s * (1.0 + a * (1.0 - s))), dact * (a * s)

    da, db = _mm([(df, w_ffo)], tb=True, name="ffn_out_dx", outs=(BF16, BF16), tn=512,
                 extras=[(fa, "mn"), (fb, "mn")], epi=swiglu_bwd_epi)
    dw_ffo = _mm([(act, df)], ta=True, name="ffn_out_dw", outs=(BF16,))
    dh2 = _mm([(da, w_fa), (db, w_fb)], tb=True, name="ffn_in_dx", outs=(F32,))
    ns_ffi = w_ffn_in.shape[-1]
    dw_ffi = _mm([(h2, da)], ta=True, name="ffn_in_dw_a", outs=(BF16,), tn=1408, split=ns_ffi,
                 into=(lax.empty((4, D, ns_ffi), BF16), 0))
    dw_ffi = _mm([(h2, db)], ta=True, name="ffn_in_dw_b", outs=(BF16,), tn=1408, split=ns_ffi, into=(dw_ffi, 2))

    def norm2_bwd_fn(dh, t, dyv, mov, g, sc, g1v):
        r = _rms_stats(t, D)
        tn = t * r
        dxg = dh * (1.0 + sc)
        dt = dyv + _rms_bwd(dxg * g, tn, r, D)
        return (dt, g1v * dt), (_colsum(dh), _colsum(dh * (tn * g)), _colsum(dxg * tn), _colsum(dt * mov.astype(F32)))

    (dx1, dmo), (dsh2, dsc2, dn2g, dg1) = _rowwise(
        norm2_bwd_fn, [dh2, x1, dy, mo], [norm2_g, sc2, g1], [(D, F32), (D, BF16)], [(1, D)] * 4, name="norm2_bwd")

    def merge_bwd_epi(dm, a1, a2, gi1, gi2):
        s1, s2 = _sigmoid(gi1.astype(F32)), _sigmoid(gi2.astype(F32))
        a1, a2 = a1.astype(F32), a2.astype(F32)
        return dm * s1, dm * s2, dm * a1 * (s1 * (1.0 - s1)), dm * a2 * (s2 * (1.0 - s2))

    da1, da2, dgi1, dgi2 = _mm([(dmo, w_out_f)], tb=True, name="out_proj_dx", outs=(BF16,) * 4, tn=512,
                               extras=[(a1, "mn"), (a2, "mn"), (g1_in, "mn"), (g2_in, "mn")], epi=merge_bwd_epi)
    dw_out = _mm([(merged, dmo)], ta=True, name="out_proj_dw", outs=(BF16,))
    dattn = _mm([(da1, w_bra)], tb=True, name="br_attn_dx", outs=(BF16,))
    dw_bra = _mm([(attn_o, da1)], ta=True, name="br_attn_dw", outs=(BF16,), split=w_br_attn.shape[-1])
    dsgu = _mm([(da2, w_brs)], tb=True, name="br_sgu_dx", outs=(BF16,))
    dw_brs = _mm([(sgu_o, da2)], ta=True, name="br_sgu_dw", outs=(BF16,), split=w_br_sgu.shape[-1])

    def sgu_bwd_fn(dso, u_in, v_in, ng, nb, ws, bst):
        u, xhat, rs, vnb = sgu_parts(u_in, v_in, ng, nb)
        dso = dso.astype(F32)
        lane = lax.broadcasted_iota(jnp.int32, (CH, LANES), 1)
        du, dvn, dws, dbs = [], [], [], jnp.zeros((CH, LANES), F32)
        for g in range(G):
            sl = slice(g * GD, (g + 1) * GD)
            wg = ws[g].astype(BF16)
            mixed = jnp.dot(wg, vnb[:, sl], preferred_element_type=F32) + bst[:, g:g + 1]
            du.append(dso[:, sl] * mixed)
            dmix = dso[:, sl] * u[:, sl]
            dmb = dmix.astype(BF16)
            dws.append(lax.dot_general(dmb, vnb[:, sl], (((1,), (1,)), ((), ())), preferred_element_type=F32))
            dbs = dbs + jnp.where(lane == g, jnp.sum(dmix, axis=1, keepdims=True), 0.0)
            dvn.append(lax.dot_general(wg, dmb, (((0,), (0,)), ((), ())), preferred_element_type=F32))
        du, dvn = jnp.concatenate(du, axis=1), jnp.concatenate(dvn, axis=1)
        dxh = dvn * ng
        dv = rs * (dxh - jnp.mean(dxh, axis=-1, keepdims=True) - xhat * jnp.mean(dxh * xhat, axis=-1, keepdims=True))
        return ((du * _gelu_grad(u_in.astype(F32)), dv * _gelu_grad(v_in.astype(F32))),
                (_colsum(dvn * xhat), _colsum(dvn), jnp.stack(dws), dbs))

    core = jnp.reshape(ac, (1,)).astype(jnp.int32)

    def dest_layout(dwf, i):
        K, Ns = big[i].shape
        if dwf.ndim == 2:
            dwf = dwf.reshape(K, 4, Ns).transpose(1, 0, 2) if col_sharded[i] else dwf.reshape(4, K, Ns)
        return dwf.reshape(4, 2, K // 2, Ns)

    def pair_sums(idx, g4, sib):
        return [_pair_add(g, s, core, name="rs_pair_add_" + tags[i]) for g, s, i in zip(g4, sib, idx)]

    early = [3, 4, 5, 6, 7]
    g4_early = [dest_layout(d, i) for d, i in zip([dw_bra, dw_brs, dw_out, dw_ffi, dw_ffo], early)]
    (du_in, dv_in), (d_sng, d_snb, d_ws, d_bs), sib_early = _rowwise(
        sgu_bwd_fn, [dsgu, u_in, v_in], [sgu_norm_g, sgu_norm_b, ws3, bs_t], [(SW, BF16), (SW, BF16)],
        [(1, SW), (1, SW), (G, CH, CH), (CH, LANES)], name="sgu_bwd", tm=CH, carry=_PairExchange(g4_early, "halves"))
    pair_early = pair_sums(early, g4_early, sib_early)
    dq_att, dk_att, dv_att, xchg_early = _attn_bwd(q_att, k_att, v_att, attn_o, lse, dattn, heads=H,
                                                   carry=_ChipExchange(pair_early, gather=False))

    def q_post_bwd_fn(dq, t, cos, slo, shi, g):
        outs, dg = [], jnp.zeros((1, HEAD_PAD), F32)
        for hd in range(H):
            sl = slice(hd * HEAD_PAD, (hd + 1) * HEAD_PAD)
            th = t[:, sl]
            r = _rms_stats(th, QK)
            tn = th * r
            dn = _rope_t(dq[:, sl] * scale, cos, slo, shi)
            dg = dg + _colsum(dn * tn)
            outs.append(_rms_bwd(dn * g, tn, r, QK))
        return (jnp.concatenate(outs, axis=1),), (dg,)

    (dq_raw,), (d_gq,) = _rowwise(q_post_bwd_fn, [dq_att, q_raw, cos_t, slo_t, shi_t], [gq_p],
                                  [(H * HEAD_PAD, BF16)], [(1, HEAD_PAD)], name="q_post_bwd")

    def k_post_bwd_fn(dk, dv, t, kvi, cos, slo, shi, g):
        kr = kvi[:, KVL:]
        outs, dg, dkr = [], jnp.zeros((1, HEAD_PAD), F32), jnp.zeros_like(kr)
        for hd in range(H):
            th = jnp.concatenate([t[:, hd * HEAD_PAD:hd * HEAD_PAD + NOPE], kr], axis=1)
            r = _rms_stats(th, QK)
            tn = th * r
            dn = _rope_t(dk[:, hd * HEAD_PAD:(hd + 1) * HEAD_PAD], cos, slo, shi)
            dg = dg + _colsum(dn * tn)
            dt = _rms_bwd(dn * g, tn, r, QK)
            dkr = dkr + dt[:, NOPE:]
            outs += [dt[:, :NOPE], dv[:, hd * VD:(hd + 1) * VD]]
        return (jnp.concatenate(outs, axis=1), dkr), (dg,)

    def reduced_halves(idx, xchg, pair):
        filled = [lax.dynamic_update_index_in_dim(t4, lax.dynamic_index_in_dim(pr, my_chip, 0, keepdims=False),
                                                  my_chip, 0) for t4, pr in zip(xchg, pair)]
        return [_sum_blocks(t4, name="rs_sum_" + tags[i], out_dtype=F32) for t4, i in zip(filled, idx)]

    red_early = reduced_halves(early, xchg_early, pair_early)
    (dkv_raw, dkrope), (d_gk,), other_early = _rowwise(
        k_post_bwd_fn, [dk_att, dv_att, kv_raw, kvin, cos_k, slo_k, shi_k], [gk_p],
        [(H * HEAD_PAD, BF16), (2 * ROPE_DIM, F32)], [(1, HEAD_PAD)], name="k_post_bwd",
        carry=_PairExchange(red_early, "gather"))

    dqn = _mm([(dq_raw, w_uq_p)], tb=True, name="q_up_dx", outs=(F32,))
    dw_uq_p = _mm([(qn, dq_raw)], ta=True, name="q_up_dw", outs=(BF16,))
    dkvn = _mm([(dkv_raw, w_ukv_f)], tb=True, name="kv_up_dx", outs=(F32,))
    dw_ukv = _mm([(kvn, dkv_raw)], ta=True, name="kv_up_dw", outs=(BF16,))

    def q_norm_bwd_fn(dn, t, g):
        r = _rms_stats(t, QL)
        tn = t * r
        return (_rms_bwd(dn * g, tn, r, QL),), (_colsum(dn * tn),)

    (dqc,), (d_qng,) = _rowwise(q_norm_bwd_fn, [dqn, qc], [q_norm_g], [(QL, BF16)], [(1, QL)], name="q_norm_bwd")

    def kv_norm_bwd_fn(dn, dkr, t, g):
        kvc = t[:, :KVL]
        r = _rms_stats(kvc, KVL)
        tn = kvc * r
        return (jnp.concatenate([_rms_bwd(dn * g, tn, r, KVL), dkr], axis=1),), (_colsum(dn * tn),)

    (dkvin,), (d_kvng,) = _rowwise(kv_norm_bwd_fn, [dkvn, dkrope, kvin], [kv_norm_g], [(KVP, BF16)], [(1, KVL)],
                                   name="kv_norm_bwd")
    dkvin_x, dkvin_c = dkvin[:N], dkvin[N:]

    dctx_h = _mm([(dkvin_c, w_kv)], tb=True, name="proj_kv_ctx_dx", outs=(F32,))
    dw_q = _mm([(h, dqc)], ta=True, name="proj_q_dw", outs=(BF16,))
    dw_kv = _mm([(h, dkvin_x), (ctx_h, dkvin_c)], ta=True, name="proj_kv_dw", outs=(BF16,))
    dw_u = _mm([(h, du_in)], ta=True, name="proj_u_dw", outs=(BF16,))
    dw_v = _mm([(h, dv_in)], ta=True, name="proj_v_dw", outs=(BF16,))
    dw_g1 = _mm([(h, dgi1)], ta=True, name="proj_g1_dw", outs=(BF16,))
    dw_g2 = _mm([(h, dgi2)], ta=True, name="proj_g2_dw", outs=(BF16,))

    dw_in_f = jnp.concatenate([dw_q, dw_kv[:, :KVL + ROPE_DIM], dw_u, dw_v, dw_g1, dw_g2], axis=1)
    dw_uq_f = dw_uq_p.reshape(QL, H, HEAD_PAD)[:, :, :QK].reshape(QL, H * QK)
    late = [0, 1, 2]
    g4_late = [dest_layout(d, i) for d, i in zip([dw_in_f, dw_uq_f, dw_ukv], late)]
    pair_late = pair_sums(late, g4_late, _exchange_alone(_PairExchange(g4_late, "halves"), name="rs_pair_late"))
    dh, xchg_late = _mm([(dqc, w_q), (dkvin_x, w_kv), (du_in, w_u), (dv_in, w_v), (dgi1, w_g1), (dgi2, w_g2)],
                        tb=True, name="proj_dx", outs=(F32,), tn=1024, tk=512,
                        carry=_ChipExchange(pair_late, gather=False))

    def norm1_bwd_fn(dhv, t, dres, g, sc):
        r = _rms_stats(t, D)
        tn = t * r
        dxg = dhv * (1.0 + sc)
        return (dres + _rms_bwd(dxg * g, tn, r, D),), (_colsum(dhv), _colsum(dhv * (tn * g)), _colsum(dxg * tn))

    (grad_x,), (dsh1, dsc1, dn1g_x) = _rowwise(norm1_bwd_fn, [dh, x2, dx1], [norm1_g, sc1], [(D, F32)], [(1, D)] * 3,
                                               name="norm1_bwd")
    _, (dsh1c, dsc1c, dn1g_c) = _rowwise(norm1_bwd_fn, [dctx_h, ctx2, jnp.zeros_like(ctx2)], [norm1_g, sc1c],
                                         [(D, F32)], [(1, D)] * 3, name="norm1_ctx_bwd")

    small = [dsh1, dsc1, dg1, dsh2, dsc2, dg2,
             dsh1c, dsc1c, dn1g_x, dn1g_c, d_qng, d_kvng, d_gq, d_gk, d_sng, d_snb, dn2g, loss_cols]
    small_sizes = [a.shape[1] for a in small]
    sm_row = jnp.concatenate(small, axis=1)
    sm_mat = jnp.concatenate([d_ws.reshape(G * CH, CH), d_bs], axis=0)
    row_all, mat_all = _all_gather8([sm_row, sm_mat], name="ag_small", in_vmem=True)
    row_sum = _sum_blocks(row_all, name="sum_small_rows", out_dtype=F32)
    mat_sum = _sum_blocks(mat_all, name="sum_small_mats", out_dtype=F32)
    dmod_rows = row_all[:, 0, :NMOD]
    (_, _, _, _, _, _, t_sh1c, t_sc1c, t_n1x, t_n1c, g_qng, g_kvng, t_gq, t_gk, g_sng, g_snb, g_n2g,
     t_loss) = _split_lanes(row_sum, small_sizes)
    g_ws, t_bs = mat_sum[:G * CH], mat_sum[G * CH:]
    dmodc_row = jnp.concatenate([t_sh1c, t_sc1c, jnp.zeros((1, NMOD - 2 * D), F32)], axis=1)
    dmod16 = jnp.concatenate([dmod_rows, dmodc_row, jnp.zeros((BF16_SUBLANES - 9, NMOD), F32)], axis=0)

    def small_fn(rows, n1x, n1c, lossv):
        return (), (_colsum(rows), n1x + n1c, jnp.sum(lossv, axis=1, keepdims=True))

    _, (g_bmod, g_n1g, loss11) = _rowwise(small_fn, [dmod16], [t_n1x, t_n1c, t_loss], [], [(1, NMOD), (1, D), (1, 1)],
                                          name="small_reduce", tm=16)
    dmod_loc = lax.dynamic_slice_in_dim(dmod16, my_chip * NM, NM, axis=1)
    g_wmod = _mm([(silu_c, dmod_loc)], ta=True, name="mod_dw", outs=(F32,), tn=512)
    dsilu_part = _mm([(dmod_loc, wm)], tb=True, name="mod_dx", outs=(F32,), tk=512)
    part_all = _all_gather8([dsilu_part[8:9]], name="ag_cctx", in_vmem=True)[0]

    def cctx_fn(parts, dsl):
        return (), ((parts[0:1] + parts[2:3] + parts[4:5] + parts[6:7]) * dsl,)

    _, (g_cctx,) = _rowwise(cctx_fn, [part_all[:, 0, :]], [dsilu_c[8:9]], [], [(1, D)], name="cctx_grad", tm=8)

    red_late = reduced_halves(late, xchg_late, pair_late)
    other_late = _exchange_alone(_PairExchange(red_late, "gather"), name="rs_halves_late")
    big_grads = [lax.dynamic_update_index_in_dim(r, mine, ac, 0).reshape(a.shape)
                 for r, mine, a in zip(other_late + other_early, red_late + red_early, big)]
    mod_upd = _adamw(w_mod[0], g_wmod, m_w_mod[0], v_w_mod[0], name="adamw_w_mod")

    def upd(w, g, m, v, nm):
        shape = w.shape
        w2, g2_, m2, v2 = [t.reshape(-1, shape[-1]) for t in (w, g, m, v)]
        d_, m_, v_ = _adamw(w2, g2_, m2, v2, name="adamw_" + nm)
        return g.reshape(shape), d_.reshape(shape), m_.reshape(shape), v_.reshape(shape)

    g_in, g_uq, g_ukv, g_bra, g_brs, g_out, g_ffi, g_ffo = big_grads
    grads = dict(
        c_ctx=g_cctx.reshape(D), w_mod=g_wmod[None], b_mod=g_bmod, norm1_g=g_n1g, w_in=g_in[None],
        q_norm_g=g_qng, kv_norm_g=g_kvng, w_uq=g_uq[None], w_ukv=g_ukv[None],
        qk_norm_q=t_gq[:, :QK], qk_norm_k=t_gk[:, :QK], sgu_norm_g=g_sng, sgu_norm_b=g_snb,
        w_spatial=g_ws.reshape(w_spatial.shape), b_spatial=t_bs[:, :G].T[None],
        w_br_attn=g_bra[None], w_br_sgu=g_brs[None], w_out=g_out[None], norm2_g=g_n2g,
        w_ffn_in=g_ffi[None], w_ffn_out=g_ffo[None])
    weights = dict(c_ctx=c_ctx, w_mod=w_mod, b_mod=b_mod, norm1_g=norm1_g, w_in=w_in, q_norm_g=q_norm_g,
                   kv_norm_g=kv_norm_g, w_uq=w_uq, w_ukv=w_ukv, qk_norm_q=qk_norm_q, qk_norm_k=qk_norm_k,
                   sgu_norm_g=sgu_norm_g, sgu_norm_b=sgu_norm_b, w_spatial=w_spatial, b_spatial=b_spatial,
                   w_br_attn=w_br_attn, w_br_sgu=w_br_sgu, w_out=w_out, norm2_g=norm2_g, w_ffn_in=w_ffn_in,
                   w_ffn_out=w_ffn_out)
    m_in = dict(c_ctx=m_c_ctx, w_mod=m_w_mod, b_mod=m_b_mod, norm1_g=m_norm1_g, w_in=m_w_in, q_norm_g=m_q_norm_g,
                kv_norm_g=m_kv_norm_g, w_uq=m_w_uq, w_ukv=m_w_ukv, qk_norm_q=m_qk_norm_q, qk_norm_k=m_qk_norm_k,
                sgu_norm_g=m_sgu_norm_g, sgu_norm_b=m_sgu_norm_b, w_spatial=m_w_spatial, b_spatial=m_b_spatial,
                w_br_attn=m_w_br_attn, w_br_sgu=m_w_br_sgu, w_out=m_w_out, norm2_g=m_norm2_g, w_ffn_in=m_w_ffn_in,
                w_ffn_out=m_w_ffn_out)
    v_in_ = dict(c_ctx=v_c_ctx, w_mod=v_w_mod, b_mod=v_b_mod, norm1_g=v_norm1_g, w_in=v_w_in, q_norm_g=v_q_norm_g,
                 kv_norm_g=v_kv_norm_g, w_uq=v_w_uq, w_ukv=v_w_ukv, qk_norm_q=v_qk_norm_q, qk_norm_k=v_qk_norm_k,
                 sgu_norm_g=v_sgu_norm_g, sgu_norm_b=v_sgu_norm_b, w_spatial=v_w_spatial, b_spatial=v_b_spatial,
                 w_br_attn=v_w_br_attn, w_br_sgu=v_w_br_sgu, w_out=v_w_out, norm2_g=v_norm2_g, w_ffn_in=v_w_ffn_in,
                 w_ffn_out=v_w_ffn_out)
    names = list(weights)
    big_names = ("w_mod", "w_in", "w_uq", "w_ukv", "w_br_attn", "w_br_sgu", "w_out", "w_ffn_in", "w_ffn_out")
    out_g, out_d, out_m, out_v = {}, {}, {}, {}
    out_g["w_mod"] = grads["w_mod"]
    out_d["w_mod"], out_m["w_mod"], out_v["w_mod"] = [t[None] for t in mod_upd[:3]]
    for nm in big_names[1:]:
        out_g[nm], out_d[nm], out_m[nm], out_v[nm] = upd(weights[nm], grads[nm], m_in[nm], v_in_[nm], nm)
    row_names = [nm for nm in names if nm not in big_names and nm not in ("w_spatial", "b_spatial")]
    widths = [-(-weights[nm].size // LANES) * LANES for nm in row_names]

    def as_row(d):
        return jnp.concatenate([jnp.pad(d[nm].reshape(1, -1), ((0, 0), (0, wd - d[nm].size)))
                                for nm, wd in zip(row_names, widths)], axis=1)

    def as_mat(d):
        return jnp.concatenate([d["w_spatial"].reshape(G * CH, CH), d["b_spatial"].reshape(G, CH)], axis=0)

    row_res = _adamw(as_row(weights), as_row(grads), as_row(m_in), as_row(v_in_), name="adamw_rows")
    mat_res = _adamw(as_mat(weights), as_mat(grads), as_mat(m_in), as_mat(v_in_), name="adamw_spatial")
    for tgt, row, mat in zip((out_d, out_m, out_v), row_res, mat_res):
        for nm, seg in zip(row_names, _split_lanes(row, widths)):
            tgt[nm] = seg[:, :weights[nm].size].reshape(weights[nm].shape)
        tgt["w_spatial"] = mat[:G * CH].reshape(w_spatial.shape)
        tgt["b_spatial"] = mat[G * CH:].reshape(b_spatial.shape)
    for nm in row_names + ["w_spatial", "b_spatial"]:
        out_g[nm] = grads[nm].reshape(weights[nm].shape)

    loss = loss11.reshape(())
    return (loss, grad_x[None], *[out_g[n] for n in names], *[out_d[n] for n in names],
            *[out_m[n] for n in names], *[out_v[n] for n in names])
```

```python
import math

import jax
import jax.numpy as jnp
from jax import lax
from jax.experimental import pallas as pl
from jax.experimental.pallas import tpu as pltpu

F32, BF16 = jnp.float32, jnp.bfloat16
MESH = pl.DeviceIdType.MESH

LANES = 128
F32_SUBLANES = 8
BF16_SUBLANES = 16
MXU_DIM = 256
VMEM_LIMIT_BYTES = 56 * 1024 * 1024

EPS = 1e-6
ROPE_DIM = 64
ROPE_THETA = 10000.0
GRID_W = 64
HEAD_PAD = 256
ADAM_LR, ADAM_B1, ADAM_B2, ADAM_EPS, ADAM_WD, ADAM_STEP = 0.001, 0.9, 0.999, 1e-08, 0.01, 10


def _tile(dim, pref, align=LANES):
    if dim <= pref:
        return dim
    t = (pref // align) * align
    while t >= align:
        if dim % t == 0:
            return t
        t -= align
    return dim


def _params(sem=None):
    return pltpu.CompilerParams(dimension_semantics=sem, vmem_limit_bytes=VMEM_LIMIT_BYTES)


def _sds(shape, dtype):
    return jax.ShapeDtypeStruct(tuple(shape), dtype)


def _mm(pairs, *, name, ta=False, tb=False, outs=(F32,), tm=1024, tn=1024, tk=2048, extras=(), epi=None,
        split=None, into=None, carry=None, col_sums=0):
    dual = len(pairs[0]) == 3
    a0, b0 = pairs[0][0], pairs[0][1]
    M = a0.shape[1] if ta else a0.shape[0]
    N = b0.shape[0] if tb else b0.shape[1]
    tm, tn = _tile(M, tm), _tile(N if split is None else split, tn)
    ks = [(p[0].shape[0] if ta else p[0].shape[1]) for p in pairs]
    tks = [_tile(k, tk) for k in ks]
    nks = [k // t for k, t in zip(ks, tks)]
    offs = [sum(nks[:i]) for i in range(len(pairs))]
    nk_total = sum(nks)
    single = len(pairs) == 1

    def kidx(kk, p):
        return kk if single else jnp.clip(kk - offs[p], 0, nks[p] - 1)

    in_specs, operands = [], []
    for p, pr in enumerate(pairs):
        if ta:
            in_specs.append(pl.BlockSpec((tks[p], tm), lambda i, j, kk, p=p: (kidx(kk, p), i)))
        else:
            in_specs.append(pl.BlockSpec((tm, tks[p]), lambda i, j, kk, p=p: (i, kidx(kk, p))))
        operands.append(pr[0])
        for b in pr[1:]:
            if tb:
                in_specs.append(pl.BlockSpec((tn, tks[p]), lambda i, j, kk, p=p: (j, kidx(kk, p))))
            else:
                in_specs.append(pl.BlockSpec((tks[p], tn), lambda i, j, kk, p=p: (kidx(kk, p), j)))
            operands.append(b)
    for arr, kind in extras:
        if kind == "mn":
            in_specs.append(pl.BlockSpec((tm, tn), lambda i, j, kk: (i, j)))
        else:
            in_specs.append(pl.BlockSpec((1, tn), lambda i, j, kk: (0, j)))
        operands.append(arr)
    n_in = len(operands)
    n_ex = len(extras)
    per = 3 if dual else 2
    dims = (((0 if ta else 1,), (1 if tb else 0,)), ((), ()))

    n_acc = 2 if dual else 1

    def products(ins, p):
        a = ins[per * p][...].astype(BF16)
        return [lax.dot_general(a, ins[per * p + 1 + q][...].astype(BF16), dims, preferred_element_type=F32)
                for q in range(n_acc)]

    def finish(ins, out_refs, acc_vals):
        vals = acc_vals + [r[...] for r in ins[n_in - n_ex:]]
        res = epi(*vals) if epi is not None else (vals[0],)
        for o, r in zip(out_refs, res):
            o[...] = jnp.broadcast_to(r, o.shape).astype(o.dtype)

    out_specs = [pl.BlockSpec((tm, tn), lambda i, j, kk: (i, j)) for _ in outs]
    out_specs += [pl.BlockSpec((None, F32_SUBLANES, tn), lambda i, j, kk: (i, 0, j)) for _ in range(col_sums)]
    out_shape = [_sds((M, N), d) for d in outs] + [_sds((M // tm, F32_SUBLANES, N), F32) for _ in range(col_sums)]
    aliases = {}
    n_alias = 0
    if split is not None:
        nps = split // tn
        lead = 0 if into is None else into[1]
        out_specs = [pl.BlockSpec((None, tm, tn), lambda i, j, kk: (j // nps + lead, i, j % nps))]
        out_shape = [_sds((N // split if into is None else into[0].shape[0], M, split), outs[0])]
        if into is not None:
            in_specs.append(pl.BlockSpec(memory_space=pl.ANY))
            operands.append(into[0])
            aliases, n_alias = {n_in: 0}, 1

    grid = (M // tm, N // tn, nk_total)
    n_out = len(outs) + col_sums

    def at_step(first):
        ids = [pl.program_id(d) for d in range(3)]
        cond = None
        for d, g in zip(ids, grid):
            t = d == (0 if first else g - 1)
            cond = t if cond is None else cond & t
        return cond

    def body(*refs):
        ins, out_refs, accs, start, wait = _split_refs(refs, n_in + n_alias, n_out, carry)
        ins = ins[:n_in]
        if carry is not None:
            pl.when(at_step(True))(start)
        if nk_total == 1:
            finish(ins, out_refs, products(ins, 0))
        else:
            kk = pl.program_id(2)

            @pl.when(kk == 0)
            def _():
                for acc, v in zip(accs, products(ins, 0)):
                    acc[...] = v

            for p in range(len(pairs)):
                lo = max(offs[p], 1)

                @pl.when((kk >= lo) & (kk < offs[p] + nks[p]))
                def _(p=p):
                    for acc, v in zip(accs, products(ins, p)):
                        acc[...] += v

            @pl.when(kk == nk_total - 1)
            def _():
                finish(ins, out_refs, [acc[...] for acc in accs])
        if carry is not None:
            pl.when(at_step(False))(wait)

    ex = carry
    res = pl.pallas_call(
        body, name=name, grid=grid, in_specs=in_specs + ([] if ex is None else ex.in_specs),
        out_specs=out_specs + ([] if ex is None else ex.out_specs),
        out_shape=out_shape + ([] if ex is None else ex.out_shape), input_output_aliases=aliases,
        scratch_shapes=[pltpu.VMEM((tm, tn), F32) for _ in range(n_acc if nk_total > 1 else 0)]
        + ([] if ex is None else ex.scratch),
        compiler_params=_params(("arbitrary",) * 3 if ex is not None else ("parallel", "parallel", "arbitrary")),
    )(*operands, *([] if ex is None else ex.xs))
    if ex is not None:
        return (res[0] if n_out == 1 else res[:n_out]), list(res[n_out:])
    return res[0] if n_out == 1 else res


def _rowwise(fn, rows, vecs, out_rows, out_accs=(), *, name, tm=256, tc=None, carry=None):
    M = rows[0].shape[0]
    tm = _tile(M, tm, BF16_SUBLANES)
    nrow = M // tm
    C = rows[0].shape[1]
    ncol = 1 if tc is None else C // _tile(C, tc)
    tcol = None if tc is None else _tile(C, tc)

    def colwise(shape):
        return tc is not None and len(shape) == 2 and shape[0] == 1 and shape[1] == C

    def vspec(shape):
        if colwise(shape):
            return pl.BlockSpec((1, tcol), lambda j, i: (0, j))
        return pl.BlockSpec(tuple(shape), lambda j, i, n=len(shape): (0,) * n)

    def rspec(width):
        if tc is None:
            return pl.BlockSpec((tm, width), lambda j, i: (i, 0))
        return pl.BlockSpec((tm, tcol), lambda j, i: (i, j))

    in_specs = [rspec(r.shape[1]) for r in rows] + [vspec(v.shape) for v in vecs]
    out_specs = [rspec(c) for c, _ in out_rows] + [vspec(s) for s in out_accs]
    out_shape = [_sds((M, c), d) for c, d in out_rows] + [_sds(s, F32) for s in out_accs]
    n_in, n_or = len(rows) + len(vecs), len(out_rows)

    n_out = n_or + len(out_accs)
    ex = carry

    def body(*refs):
        ins, outs, _, start, wait = _split_refs(refs, n_in, n_out, ex)
        o_rows, o_accs = outs[:n_or], outs[n_or:]
        if ex is not None:
            pl.when((pl.program_id(0) == 0) & (pl.program_id(1) == 0))(start)
        r_out, a_out = fn(*[r[...] for r in ins])
        for o, r in zip(o_rows, r_out):
            o[...] = r.astype(o.dtype)
        i = pl.program_id(1)

        @pl.when(i == 0)
        def _():
            for o, a in zip(o_accs, a_out):
                o[...] = a

        @pl.when(i > 0)
        def _():
            for o, a in zip(o_accs, a_out):
                o[...] += a

        if ex is not None:
            pl.when((pl.program_id(0) == ncol - 1) & (pl.program_id(1) == nrow - 1))(wait)

    res = pl.pallas_call(
        body, name=name, grid=(ncol, nrow), in_specs=in_specs + ([] if ex is None else ex.in_specs),
        out_specs=out_specs + ([] if ex is None else ex.out_specs),
        out_shape=out_shape + ([] if ex is None else ex.out_shape),
        scratch_shapes=[] if ex is None else ex.scratch,
        compiler_params=_params(("arbitrary", "arbitrary") if ex is not None else ("parallel", "arbitrary")),
    )(*rows, *vecs, *([] if ex is None else ex.xs))
    if ex is not None:
        return res[:n_or], res[n_or:n_out], list(res[n_out:])
    return res[:n_or], res[n_or:]


def _colsum(t):
    return jnp.sum(t, axis=0, keepdims=True)


def _gelu(t):
    return 0.5 * t * (1.0 + lax.erf(t * math.sqrt(0.5)))


def _gelu_grad(t):
    return 0.5 * (1.0 + lax.erf(t * math.sqrt(0.5))) + t * jnp.exp(-0.5 * t * t) * (1.0 / math.sqrt(2.0 * math.pi))


def _sigmoid(t):
    return 1.0 / (1.0 + jnp.exp(-t))


def _rms_stats(t, width):
    return lax.rsqrt(jnp.sum(t * t, axis=-1, keepdims=True) * (1.0 / width) + EPS)


def _rms_bwd(dn, tn, r, width):
    return r * (dn - tn * (jnp.sum(dn * tn, axis=-1, keepdims=True) * (1.0 / width)))


def _place():
    return lax.axis_index("x"), lax.axis_index("y"), lax.axis_index("c")


class _ChipExchange:
    def __init__(self, xs, gather):
        self.xs, self.gather, self.n = list(xs), gather, len(xs)
        self.in_specs = [pl.BlockSpec(memory_space=pl.ANY)] * self.n
        self.out_specs = [pl.BlockSpec(memory_space=pl.ANY)] * self.n
        self.out_shape = [_sds((4,) + (x.shape if gather else x.shape[1:]), x.dtype) for x in self.xs]
        self.scratch = [pltpu.SemaphoreType.DMA((self.n, 3)), pltpu.SemaphoreType.DMA((self.n, 3))]

    def bind(self, x_refs, out_refs, send_sems, recv_sems):
        x, y, c = _place()
        p = 2 * x + y
        chips = [(1 - x, y), (x, 1 - y), (1 - x, 1 - y)]

        def copy(w, k, outgoing):
            qx, qy = chips[k]
            there = 2 * qx + qy
            if self.gather:
                src = x_refs[w]
            else:
                src = x_refs[w].at[there if outgoing else p]
            return pltpu.make_async_remote_copy(
                src_ref=src, dst_ref=out_refs[w].at[p if outgoing else there], send_sem=send_sems.at[w, k],
                recv_sem=recv_sems.at[w, k], device_id=(qx, qy, c), device_id_type=MESH)

        def start():
            for w in range(self.n):
                for k in range(3):
                    copy(w, k, True).start()

        def wait():
            for w in range(self.n):
                for k in range(3):
                    copy(w, k, False).wait_recv()
            for w in range(self.n):
                for k in range(3):
                    copy(w, k, True).wait_send()

        return start, wait


class _PairExchange:
    def __init__(self, xs, mode):
        self.xs, self.mode, self.n = list(xs), mode, len(xs)
        self.in_specs = [pl.BlockSpec(memory_space=pl.ANY)] * self.n
        self.out_specs = [pl.BlockSpec(memory_space=pl.ANY)] * self.n
        shape = {"halves": lambda s: (4,) + s[2:], "forward": lambda s: s, "gather": lambda s: (2,) + s}[mode]
        self.out_shape = [_sds(shape(x.shape), x.dtype) for x in self.xs]
        self.scratch = [pltpu.SemaphoreType.DMA((self.n, 3)), pltpu.SemaphoreType.DMA((self.n, 3))]

    def bind(self, x_refs, out_refs, send_sems, recv_sems):
        x, y, c = _place()
        chips = [(1 - x, y), (x, 1 - y), (1 - x, 1 - y)]

        def copy(w, src, dst, k):
            return pltpu.make_async_remote_copy(src_ref=src, dst_ref=dst, send_sem=send_sems.at[w, k],
                                                recv_sem=recv_sems.at[w, k], device_id=(x, y, 1 - c),
                                                device_id_type=MESH)

        def start():
            for w, (xr, orf) in enumerate(zip(x_refs, out_refs)):
                if self.mode == "halves":
                    for q in range(4):
                        copy(w, xr.at[q, 1 - c], orf.at[q], 0).start()
                elif self.mode == "forward":
                    for k, (qx, qy) in enumerate(chips):
                        copy(w, xr.at[2 * qx + qy], orf.at[2 * qx + qy], k).start()
                else:
                    copy(w, xr, orf.at[c], 0).start()

        def wait():
            for w, (xr, orf) in enumerate(zip(x_refs, out_refs)):
                if self.mode == "halves":
                    copy(w, orf, orf, 0).wait()
                elif self.mode == "forward":
                    for k, (qx, qy) in enumerate(chips):
                        copy(w, xr.at[2 * qx + qy], orf.at[2 * qx + qy], k).wait()
                else:
                    cp = copy(w, xr, orf.at[1 - c], 0)
                    cp.wait_recv()
                    cp.wait_send()

        return start, wait


def _split_refs(refs, n_in, n_out, ex):
    ne = 0 if ex is None else ex.n
    ins, xin = refs[:n_in], refs[n_in:n_in + ne]
    outs, xout = refs[n_in + ne:n_in + ne + n_out], refs[n_in + ne + n_out:n_in + 2 * ne + n_out]
    rest = refs[n_in + 2 * ne + n_out:]
    if ex is None:
        return ins, outs, rest, None, None
    start, wait = ex.bind(xin, xout, rest[-2], rest[-1])
    return ins, outs, rest[:-2], start, wait


def _attn_fwd(q, k, v, *, heads, tq=512, carry=None):
    N, M = q.shape[0], k.shape[0]
    tq = _tile(N, tq)
    sub = _tile(tq, MXU_DIM)
    vd = v.shape[1] // heads
    nq = N // tq

    def body(*refs):
        (q_ref, k_ref, v_ref), (o_ref, lse_ref), _, start, wait = _split_refs(refs, 3, 2, carry)
        if carry is not None:
            pl.when((pl.program_id(0) == 0) & (pl.program_id(1) == 0))(start)
        for sb in range(tq // sub):
            rows = pl.ds(sb * sub, sub)
            s = lax.dot_general(q_ref[rows, :], k_ref[...], (((1,), (1,)), ((), ())), preferred_element_type=F32)
            m = jnp.max(s, axis=-1, keepdims=True)
            p = jnp.exp(s - m)
            l = jnp.sum(p, axis=-1, keepdims=True)
            o = jnp.dot(p.astype(BF16), v_ref[...], preferred_element_type=F32) / l
            o_ref[rows, :] = o.astype(o_ref.dtype)
            lse_ref[rows, :] = jnp.broadcast_to(m + jnp.log(l), (sub, vd))
        if carry is not None:
            pl.when((pl.program_id(0) == heads - 1) & (pl.program_id(1) == nq - 1))(wait)

    ex = carry
    res = pl.pallas_call(
        body, name="attn_fwd", grid=(heads, nq),
        in_specs=[pl.BlockSpec((tq, HEAD_PAD), lambda h, i: (i, h)),
                  pl.BlockSpec((M, HEAD_PAD), lambda h, i: (0, h)),
                  pl.BlockSpec((M, vd), lambda h, i: (0, h))] + ([] if ex is None else ex.in_specs),
        out_specs=[pl.BlockSpec((tq, vd), lambda h, i: (i, h)),
                   pl.BlockSpec((tq, vd), lambda h, i: (i, h))] + ([] if ex is None else ex.out_specs),
        out_shape=[_sds((N, heads * vd), BF16), _sds((N, heads * vd), F32)] + ([] if ex is None else ex.out_shape),
        scratch_shapes=[] if ex is None else ex.scratch,
        compiler_params=_params(("arbitrary", "arbitrary")),
    )(q, k, v, *([] if ex is None else ex.xs))
    return res[0], res[1], list(res[2:])


def _attn_bwd(q, k, v, o, lse, do, *, heads, tq=512, carry=None):
    N, M = q.shape[0], k.shape[0]
    tq = _tile(N, tq)
    vd = v.shape[1] // heads
    nq = N // tq
    sub = _tile(tq, MXU_DIM)
    nt = (((1,), (1,)), ((), ()))
    tn = (((0,), (0,)), ((), ()))

    def body(*refs):
        (q_ref, k_ref, v_ref, o_ref, lse_ref, do_ref), (dq_ref, dk_ref, dv_ref), _, start, wait = _split_refs(
            refs, 6, 3, carry)
        if carry is not None:
            pl.when((pl.program_id(0) == 0) & (pl.program_id(1) == 0))(start)
        i = pl.program_id(1)
        kb, vb = k_ref[...], v_ref[...]
        parts = []
        for sb in range(tq // sub):
            rows = pl.ds(sb * sub, sub)
            qb, dob = q_ref[rows, :], do_ref[rows, :]
            delta = jnp.sum(dob.astype(F32) * o_ref[rows, :].astype(F32), axis=-1, keepdims=True)
            s = lax.dot_general(qb, kb, nt, preferred_element_type=F32)
            p = jnp.exp(s - lse_ref[rows, :][:, :1])
            dp = lax.dot_general(dob, vb, nt, preferred_element_type=F32)
            ds = (p * (dp - delta)).astype(BF16)
            dq_ref[rows, :] = jnp.dot(ds, kb, preferred_element_type=F32)
            parts.append((lax.dot_general(ds, qb, tn, preferred_element_type=F32),
                          lax.dot_general(p.astype(BF16), dob, tn, preferred_element_type=F32)))

        @pl.when(i == 0)
        def _():
            dk_ref[...] = parts[0][0]
            dv_ref[...] = parts[0][1]

        @pl.when(i > 0)
        def _():
            dk_ref[...] += parts[0][0]
            dv_ref[...] += parts[0][1]

        for dk_part, dv_part in parts[1:]:
            dk_ref[...] += dk_part
            dv_ref[...] += dv_part

        if carry is not None:
            pl.when((pl.program_id(0) == heads - 1) & (pl.program_id(1) == nq - 1))(wait)

    ex = carry
    res = pl.pallas_call(
        body, name="attn_bwd", grid=(heads, nq),
        in_specs=[pl.BlockSpec((tq, HEAD_PAD), lambda h, i: (i, h)),
                  pl.BlockSpec((M, HEAD_PAD), lambda h, i: (0, h)),
                  pl.BlockSpec((M, vd), lambda h, i: (0, h)),
                  pl.BlockSpec((tq, vd), lambda h, i: (i, h)),
                  pl.BlockSpec((tq, vd), lambda h, i: (i, h)),
                  pl.BlockSpec((tq, vd), lambda h, i: (i, h))] + ([] if ex is None else ex.in_specs),
        out_specs=[pl.BlockSpec((tq, HEAD_PAD), lambda h, i: (i, h)),
                   pl.BlockSpec((M, HEAD_PAD), lambda h, i: (0, h)),
                   pl.BlockSpec((M, vd), lambda h, i: (0, h))] + ([] if ex is None else ex.out_specs),
        out_shape=[_sds((N, heads * HEAD_PAD), F32), _sds((M, heads * HEAD_PAD), F32),
                   _sds((M, heads * vd), F32)] + ([] if ex is None else ex.out_shape),
        scratch_shapes=[] if ex is None else ex.scratch,
        compiler_params=_params(("arbitrary", "arbitrary")),
    )(q, k, v, o, lse, do, *([] if ex is None else ex.xs))
    return res[0], res[1], res[2], list(res[3:])


def _comm_call(body, xs, out_shapes, n_sems, name, in_vmem):
    space = pltpu.VMEM if in_vmem else pl.ANY
    n = len(xs)

    def wrapped(*refs):
        body(refs[:n], refs[n:2 * n], *refs[2 * n:])

    return pl.pallas_call(
        wrapped, name=name, out_shape=list(out_shapes),
        in_specs=[pl.BlockSpec(memory_space=space)] * n, out_specs=[pl.BlockSpec(memory_space=space)] * n,
        scratch_shapes=[pltpu.SemaphoreType.DMA((n, n_sems)), pltpu.SemaphoreType.DMA((n, n_sems)),
                        pltpu.SemaphoreType.DMA((n,))],
        compiler_params=pltpu.CompilerParams(vmem_limit_bytes=VMEM_LIMIT_BYTES),
    )(*xs)


def _all_gather8(blks, *, name, in_vmem):
    def body(x_refs, out_refs, send_sems, recv_sems, local_sems):
        x, y, c = _place()
        me, sibling = (x, y, c), (x, y, 1 - c)
        chips = [(1 - x, y), (x, 1 - y), (1 - x, 1 - y)]
        waits = []
        for w, (x_ref, out_ref) in enumerate(zip(x_refs, out_refs)):
            def slot(px, py, pc, out_ref=out_ref):
                return out_ref.at[4 * px + 2 * py + pc]

            def copy(k, block, to, src=None, w=w, slot=slot):
                return pltpu.make_async_remote_copy(
                    src_ref=slot(*block) if src is None else src, dst_ref=slot(*block),
                    send_sem=send_sems.at[w, k], recv_sem=recv_sems.at[w, k], device_id=to, device_id_type=MESH)

            mine = pltpu.make_async_copy(x_ref, slot(*me), local_sems.at[w])
            mine.start()
            first = [copy(0, me, sibling, src=x_ref)]
            first += [copy(1 + j, me, (*chip, c), src=x_ref) for j, chip in enumerate(chips)]
            for cp in first:
                cp.start()
            waits.append((copy, mine, first))
        for copy, mine, first in waits:
            passed = [copy(4 + j, (*chip, c), sibling) for j, chip in enumerate(chips)]
            for j, chip in enumerate(chips):
                copy(1 + j, (*chip, c), me).wait_recv()
                passed[j].start()
            copy(0, sibling, me).wait_recv()
            for j, chip in enumerate(chips):
                copy(4 + j, (*chip, 1 - c), me).wait_recv()
            for cp in first + passed:
                cp.wait_send()
            mine.wait()

    return _comm_call(body, blks, [_sds((8,) + b.shape, b.dtype) for b in blks], 7, name, in_vmem)


def _gather_others(blks, *, name):
    def body(x_refs, out_refs, send_sems, recv_sems, local_sems):
        x, y, c = _place()
        own, xn, yn, dg = (x, y), (1 - x, y), (x, 1 - y), (1 - x, 1 - y)

        def slot(w, chip, core):
            return out_refs[w].at[4 * chip[0] + 2 * chip[1] + core]

        def cp(w, k, src, dst, chip, core):
            return pltpu.make_async_remote_copy(src_ref=src, dst_ref=dst, send_sem=send_sems.at[w, k],
                                                recv_sem=recv_sems.at[w, k], device_id=(*chip, core),
                                                device_id_type=MESH)

        def halves(w):
            h = x_refs[w].shape[0] // 2
            return pl.ds(0, h), pl.ds(h, h)

        sends = []
        for w, x_ref in enumerate(x_refs):
            sends += [cp(w, 0, x_ref, slot(w, own, c), xn, c), cp(w, 1, x_ref, slot(w, own, c), yn, c)]
        for s in sends:
            s.start()
        for w, x_ref in enumerate(x_refs):
            lo, hi = halves(w)
            cp(w, 1, x_ref, slot(w, yn, c), yn, c).wait_recv()
            passed = [cp(w, 2, slot(w, yn, c).at[lo], slot(w, yn, c).at[lo], xn, c),
                      cp(w, 4, slot(w, yn, c), slot(w, yn, c), own, 1 - c)]
            cp(w, 0, x_ref, slot(w, xn, c), xn, c).wait_recv()
            passed += [cp(w, 3, slot(w, xn, c).at[hi], slot(w, xn, c).at[hi], yn, c),
                       cp(w, 5, slot(w, xn, c), slot(w, xn, c), own, 1 - c)]
            for s in passed:
                s.start()
            sends += passed
        for w in range(len(x_refs)):
            lo, hi = halves(w)
            cp(w, 2, slot(w, dg, c).at[lo], slot(w, dg, c).at[lo], xn, c).wait_recv()
            cp(w, 3, slot(w, dg, c).at[hi], slot(w, dg, c).at[hi], yn, c).wait_recv()
            passed = [cp(w, 6, slot(w, dg, c), slot(w, dg, c), own, 1 - c)]
            passed[0].start()
            sends += passed
        for w in range(len(x_refs)):
            cp(w, 4, slot(w, yn, c), slot(w, yn, 1 - c), own, 1 - c).wait_recv()
            cp(w, 5, slot(w, xn, c), slot(w, xn, 1 - c), own, 1 - c).wait_recv()
            cp(w, 6, slot(w, dg, c), slot(w, dg, 1 - c), own, 1 - c).wait_recv()
        for s in sends:
            s.wait_send()

    return list(_comm_call(body, blks, [_sds((8,) + b.shape, b.dtype) for b in blks], 7, name, False))


def _exchange_alone(ex, *, name):
    def body(x_refs, out_refs, send_sems, recv_sems, local_sems):
        start, wait = ex.bind(x_refs, out_refs, send_sems, recv_sems)
        start()
        wait()

    return list(_comm_call(body, ex.xs, ex.out_shape, 3, name, False))


def _block_rows(rows, row_bytes, target=1 << 21, align=BF16_SUBLANES):
    return _tile(rows, max(align, target // row_bytes // align * align), align)


def _sum_blocks(buf, *, name, out_dtype):
    B, R, C = buf.shape
    tm = _block_rows(R, B * C * buf.dtype.itemsize)

    def body(x_ref, o_ref):
        acc = x_ref[0].astype(F32)
        for b in range(1, B):
            acc = acc + x_ref[b].astype(F32)
        o_ref[...] = acc.astype(o_ref.dtype)

    return pl.pallas_call(
        body, name=name, grid=(R // tm,), in_specs=[pl.BlockSpec((B, tm, C), lambda i: (0, i, 0))],
        out_specs=pl.BlockSpec((tm, C), lambda i: (i, 0)), out_shape=_sds((R, C), out_dtype),
        compiler_params=_params(("parallel",)),
    )(buf)


def _pair_add(mine, theirs, core, *, name):
    _, _, R, C = mine.shape
    tm = _block_rows(R, C * 2)

    def body(core_ref, a_ref, b_ref, o_ref):
        o_ref[...] = (a_ref[...].astype(F32) + b_ref[...].astype(F32)).astype(o_ref.dtype)

    return pl.pallas_call(
        body, name=name, out_shape=_sds(theirs.shape, BF16),
        grid_spec=pltpu.PrefetchScalarGridSpec(
            num_scalar_prefetch=1, grid=(4, R // tm),
            in_specs=[pl.BlockSpec((None, None, tm, C), lambda q, i, core_ref: (q, core_ref[0], i, 0)),
                      pl.BlockSpec((None, tm, C), lambda q, i, core_ref: (q, i, 0))],
            out_specs=pl.BlockSpec((None, tm, C), lambda q, i, core_ref: (q, i, 0))),
        compiler_params=_params(("parallel", "parallel")),
    )(core, mine, theirs)


def _assemble(gathered, own, chip, *, name, transpose):
    _, K, Ns = gathered.shape
    tm = _block_rows(K, Ns * 4)

    def body(chip_ref, g_ref, own_ref, o_ref):
        q = pl.program_id(0)

        @pl.when(q == chip_ref[0])
        def _():
            o_ref[...] = own_ref[...].astype(BF16)

        @pl.when(q != chip_ref[0])
        def _():
            o_ref[...] = g_ref[...]

    if transpose:
        out_spec = pl.BlockSpec((tm, Ns), lambda q, i, ch: (i, q))
        out_shape = _sds((K, 4 * Ns), BF16)
    else:
        out_spec = pl.BlockSpec((None, tm, Ns), lambda q, i, ch: (q, i, 0))
        out_shape = _sds((4, K, Ns), BF16)
    return pl.pallas_call(
        body, name=name, out_shape=out_shape,
        grid_spec=pltpu.PrefetchScalarGridSpec(
            num_scalar_prefetch=1, grid=(4, K // tm),
            in_specs=[pl.BlockSpec((None, tm, Ns), lambda q, i, ch: (jnp.where(q == ch[0], (q + 1) % 4, q), i, 0)),
                      pl.BlockSpec((tm, Ns), lambda q, i, ch: (jnp.where(q == ch[0], i, 0), 0))],
            out_specs=out_spec),
        compiler_params=_params(("arbitrary", "arbitrary")),
    )(chip, gathered, own)


def _assemble_halves(mine, theirs, own, place, *, name, transpose):
    _, K2, Ns = mine.shape
    tm = _block_rows(K2, Ns * 4)
    nb = K2 // tm

    def body(place_ref, m_ref, t_ref, own_ref, o_ref):
        q, hb = pl.program_id(0), pl.program_id(1)
        is_own = q == place_ref[0]
        is_mine = hb == place_ref[1]

        @pl.when(is_own)
        def _():
            o_ref[...] = own_ref[...].astype(BF16)

        @pl.when(jnp.logical_not(is_own) & is_mine)
        def _():
            o_ref[...] = m_ref[...]

        @pl.when(jnp.logical_not(is_own) & jnp.logical_not(is_mine))
        def _():
            o_ref[...] = t_ref[...]

    def other(q, pr):
        return jnp.where(q == pr[0], (q + 1) % 4, q)

    if transpose:
        out_spec = pl.BlockSpec((tm, Ns), lambda q, hb, i, pr: (hb * nb + i, q))
        out_shape = _sds((2 * K2, 4 * Ns), BF16)
    else:
        out_spec = pl.BlockSpec((None, tm, Ns), lambda q, hb, i, pr: (q, hb * nb + i, 0))
        out_shape = _sds((4, 2 * K2, Ns), BF16)
    return pl.pallas_call(
        body, name=name, out_shape=out_shape,
        grid_spec=pltpu.PrefetchScalarGridSpec(
            num_scalar_prefetch=1, grid=(4, 2, nb),
            in_specs=[pl.BlockSpec((None, tm, Ns), lambda q, hb, i, pr: (other(q, pr), jnp.where(hb == pr[1], i, 0), 0)),
                      pl.BlockSpec((None, tm, Ns), lambda q, hb, i, pr: (other(q, pr), jnp.where(hb == pr[1], 0, i), 0)),
                      pl.BlockSpec((tm, Ns), lambda q, hb, i, pr: (jnp.where(q == pr[0], hb * nb + i, 0), 0))],
            out_specs=out_spec),
        compiler_params=_params(("arbitrary",) * 3),
    )(place, mine, theirs, own)


def _split_lanes(row, widths):
    out, off = [], 0
    for wd in widths:
        out.append(row[:, off:off + wd])
        off += wd
    return out


def _adamw(w, g, m, v, *, name, carry=None):
    C = w.shape[1]

    def fn(w, g, m, v):
        m = ADAM_B1 * m + (1.0 - ADAM_B1) * g
        v = ADAM_B2 * v + (1.0 - ADAM_B2) * (g * g)
        m_hat = m / (1.0 - ADAM_B1 ** ADAM_STEP)
        v_hat = v / (1.0 - ADAM_B2 ** ADAM_STEP)
        delta = -ADAM_LR * (m_hat / (jnp.sqrt(v_hat) + ADAM_EPS) + ADAM_WD * w)
        return (delta, m, v), ()

    tm = max(F32_SUBLANES, min(512, (1 << 20) // (4 * C) // F32_SUBLANES * F32_SUBLANES))
    res = _rowwise(fn, [w, g, m, v], [], [(C, F32)] * 3, name=name, tm=tm, carry=carry)
    return tuple(res[0]) + ((res[2],) if carry is not None else ())


def _rope_tables(n):
    rows = n // GRID_W
    row = jnp.repeat(jnp.arange(rows, dtype=F32), GRID_W)
    col = jnp.tile(jnp.arange(GRID_W, dtype=F32), rows)
    nf = ROPE_DIM // 4
    freqs = ROPE_THETA ** (-jnp.arange(nf, dtype=F32) / nf)
    ang_r, ang_c = row[:, None] * freqs[None, :], col[:, None] * freqs[None, :]
    cr, sr, cc, sc = jnp.cos(ang_r), jnp.sin(ang_r), jnp.cos(ang_c), jnp.sin(ang_c)
    nope = HEAD_PAD - 2 * ROPE_DIM
    one, zero, z = jnp.ones((n, nope), F32), jnp.zeros((n, nope), F32), jnp.zeros((n, nf), F32)
    pad = jnp.zeros((n, ROPE_DIM), F32)
    cos = jnp.concatenate([one, cr, cr, cc, cc, pad], axis=1)
    s_lo = jnp.concatenate([zero, -sr, z, -sc, z, pad], axis=1)
    s_hi = jnp.concatenate([zero, z, sr, z, sc, pad], axis=1)
    return cos, s_lo, s_hi


def _rope(n, cos, s_lo, s_hi):
    q = ROPE_DIM // 4
    return n * cos + pltpu.roll(n, HEAD_PAD - q, 1) * s_lo + pltpu.roll(n, q, 1) * s_hi


def _rope_t(d, cos, s_lo, s_hi):
    q = ROPE_DIM // 4
    return d * cos + pltpu.roll(d * s_lo, q, 1) + pltpu.roll(d * s_hi, HEAD_PAD - q, 1)


def kernel(x, c, ctx, c_ctx, w_mod, b_mod, norm1_g, w_in, q_norm_g, kv_norm_g, w_uq, w_ukv, qk_norm_q, qk_norm_k, sgu_norm_g, sgu_norm_b, w_spatial, b_spatial, w_br_attn, w_br_sgu, w_out, norm2_g, w_ffn_in, w_ffn_out, loss_target, m_c_ctx, m_w_mod, m_b_mod, m_norm1_g, m_w_in, m_q_norm_g, m_kv_norm_g, m_w_uq, m_w_ukv, m_qk_norm_q, m_qk_norm_k, m_sgu_norm_g, m_sgu_norm_b, m_w_spatial, m_b_spatial, m_w_br_attn, m_w_br_sgu, m_w_out, m_norm2_g, m_w_ffn_in, m_w_ffn_out, v_c_ctx, v_w_mod, v_b_mod, v_norm1_g, v_w_in, v_q_norm_g, v_kv_norm_g, v_w_uq, v_w_ukv, v_qk_norm_q, v_qk_norm_k, v_sgu_norm_g, v_sgu_norm_b, v_w_spatial, v_b_spatial, v_w_br_attn, v_w_br_sgu, v_w_out, v_norm2_g, v_w_ffn_in, v_w_ffn_out):
    ax, ay, ac = _place()
    my_chip = 2 * ax + ay
    my_dev = 4 * ax + 2 * ay + ac

    N, D = x.shape[1], x.shape[2]
    CT = ctx.shape[1]
    M = N + CT
    QL, KVL, QK = q_norm_g.shape[-1], kv_norm_g.shape[-1], qk_norm_q.shape[-1]
    NOPE = QK - ROPE_DIM
    VD = NOPE
    H = 4 * w_uq.shape[-1] // QK
    SW, G, CH = sgu_norm_g.shape[-1], w_spatial.shape[1], w_spatial.shape[2]
    GD = SW // G
    DFF = 4 * w_ffn_out.shape[1]
    NMOD = 4 * w_mod.shape[-1]
    NM = w_mod.shape[-1]
    KVP = KVL + 2 * ROPE_DIM
    assert NOPE == LANES and GD == LANES and HEAD_PAD == NOPE + 2 * ROPE_DIM and CH == LANES
    scale = QK ** -0.5

    x2, ctx2, tgt2 = x[0], ctx[0], loss_target[0]

    c_all = _all_gather8([c], name="ag_c", in_vmem=True)[0][:, 0, :]
    c_rows = jnp.concatenate([c_all, c_ctx[None, :], jnp.zeros((BF16_SUBLANES - 9, D), F32)], axis=0)

    def silu_fn(t):
        s = _sigmoid(t)
        return (t * s, s * (1.0 + t * (1.0 - s))), ()

    (silu_c, dsilu_c), _ = _rowwise(silu_fn, [c_rows], [], [(D, F32), (D, F32)], name="silu_c", tm=16)
    wm = w_mod[0]
    mod_loc = _mm([(silu_c, wm)], name="mod_fwd", outs=(F32,), tn=512, tk=512,
                  extras=[(lax.dynamic_slice_in_dim(b_mod, my_chip * NM, NM, axis=1), "n")],
                  epi=lambda acc, b: (acc + b,))
    mod_all = _all_gather8([mod_loc], name="ag_mod", in_vmem=True)[0]
    mod_full = jnp.concatenate([mod_all[0], mod_all[2], mod_all[4], mod_all[6]], axis=1)
    mod_me = lax.dynamic_slice_in_dim(mod_full, my_dev, 1, axis=0)
    sh1, sc1, g1, sh2, sc2, g2 = [mod_me[:, i * D:(i + 1) * D] for i in range(6)]
    sh1c, sc1c = mod_full[8:9, :D], mod_full[8:9, D:2 * D]

    big = [w_in[0], w_uq[0], w_ukv[0], w_br_attn[0], w_br_sgu[0], w_out[0], w_ffn_in[0], w_ffn_out[0]]
    col_sharded = [True, True, True, True, True, False, True, False]
    halves = [lax.dynamic_slice_in_dim(a, ac * (a.shape[0] // 2), a.shape[0] // 2, axis=0).astype(BF16) for a in big]
    tags = ["w_in", "w_uq", "w_ukv", "w_br_attn", "w_br_sgu", "w_out", "w_ffn_in", "w_ffn_out"]
    first_group, attn_group, ffn_group = [0, 1, 2], [3, 4, 5, 6], [7]
    chip1 = jnp.reshape(my_chip, (1,)).astype(jnp.int32)
    place2 = jnp.stack([my_chip, ac]).astype(jnp.int32)

    def laid_out(seg, i):
        a = big[i]
        if col_sharded[i] and seg.ndim == 3:
            return seg.transpose(1, 0, 2).reshape(a.shape[0], 4 * a.shape[1])
        return seg if col_sharded[i] else seg.reshape(4 * a.shape[0], a.shape[1])

    def side_by_side(i):
        return col_sharded[i] and big[i].shape[1] % LANES == 0

    def finish_gather(idx, mine4, theirs4):
        return [laid_out(_assemble_halves(m, t, big[i], place2, name="assemble_" + tags[i], transpose=side_by_side(i)), i)
                for i, m, t in zip(idx, mine4, theirs4)]

    gathered = _gather_others([halves[i] for i in first_group], name="ag_weights")
    w_in_f, w_uq_f, w_ukv_f = [
        laid_out(_assemble(seg.reshape((4,) + big[i].shape), big[i], chip1, name="assemble_" + tags[i],
                           transpose=side_by_side(i)), i) for i, seg in zip(first_group, gathered)]
    o_kv, o_u = QL, QL + KVL + ROPE_DIM
    o_v, o_g = o_u + SW, o_u + 2 * SW
    w_q = w_in_f[:, :QL]
    w_kv = jnp.pad(w_in_f[:, o_kv:o_u], ((0, 0), (0, ROPE_DIM)))
    w_u, w_v = w_in_f[:, o_u:o_v], w_in_f[:, o_v:o_g]
    w_g1, w_g2 = w_in_f[:, o_g:o_g + D], w_in_f[:, o_g + D:]
    w_uq_p = jnp.pad(w_uq_f.reshape(QL, H, QK), ((0, 0), (0, 0), (0, HEAD_PAD - QK))).reshape(QL, H * HEAD_PAD)

    cos_t, slo_t, shi_t = _rope_tables(N)
    ones_c = jnp.concatenate([jnp.ones((CT, NOPE + ROPE_DIM), F32), jnp.zeros((CT, ROPE_DIM), F32)], axis=1)
    cos_k = jnp.concatenate([cos_t, ones_c], axis=0)
    slo_k = jnp.concatenate([slo_t, jnp.zeros((CT, HEAD_PAD), F32)], axis=0)
    shi_k = jnp.concatenate([shi_t, jnp.zeros((CT, HEAD_PAD), F32)], axis=0)
    gq_p = jnp.pad(qk_norm_q, ((0, 0), (0, HEAD_PAD - QK)))
    gk_p = jnp.pad(qk_norm_k, ((0, 0), (0, HEAD_PAD - QK)))

    def norm_mod_fn(t, g, sh, sc):
        r = _rms_stats(t, D)
        return (((t * r) * g) * (1.0 + sc) + sh,), ()

    (h,), _ = _rowwise(norm_mod_fn, [x2], [norm1_g, sh1, sc1], [(D, BF16)], name="norm1_x")
    (ctx_h,), _ = _rowwise(norm_mod_fn, [ctx2], [norm1_g, sh1c, sc1c], [(D, BF16)], name="norm1_ctx")

    qc = _mm([(h, w_q)], name="proj_q", outs=(F32,))
    kvin = jnp.concatenate([_mm([(h, w_kv)], name="proj_kv", outs=(F32,)),
                            _mm([(ctx_h, w_kv)], name="proj_kv_ctx", outs=(F32,))], axis=0)
    u_in = _mm([(h, w_u)], name="proj_u", outs=(BF16,))
    v_in = _mm([(h, w_v)], name="proj_v", outs=(BF16,))
    g1_in, (mine_bra,) = _mm([(h, w_g1)], name="proj_g1", outs=(BF16,), carry=_ChipExchange([halves[3]], gather=True))
    g2_in, (mine_brs,) = _mm([(h, w_g2)], name="proj_g2", outs=(BF16,), carry=_ChipExchange([halves[4]], gather=True))

    def rms_gain_fn(width):
        def fn(t, g):
            return (((t * _rms_stats(t, width)) * g),), ()
        return fn

    (qn,), _ = _rowwise(rms_gain_fn(QL), [qc], [q_norm_g], [(QL, BF16)], name="q_norm")

    def kv_norm_fn(t, g):
        kvc = t[:, :KVL]
        return (((kvc * _rms_stats(kvc, KVL)) * g),), ()

    (kvn,), _ = _rowwise(kv_norm_fn, [kvin], [kv_norm_g], [(KVL, BF16)], name="kv_norm")
    q_raw = _mm([(qn, w_uq_p)], name="q_up", outs=(F32,))
    kv_raw = _mm([(kvn, w_ukv_f)], name="kv_up", outs=(F32,))

    def q_post_fn(t, cos, slo, shi, g):
        outs = []
        for hd in range(H):
            th = t[:, hd * HEAD_PAD:(hd + 1) * HEAD_PAD]
            outs.append(_rope((th * _rms_stats(th, QK)) * g, cos, slo, shi) * scale)
        return (jnp.concatenate(outs, axis=1),), ()

    (q_att,), _ = _rowwise(q_post_fn, [q_raw, cos_t, slo_t, shi_t], [gq_p], [(H * HEAD_PAD, BF16)], name="q_post")

    def k_post_fn(t, kvi, cos, slo, shi, g):
        kr = kvi[:, KVL:]
        ks, vs = [], []
        for hd in range(H):
            th = jnp.concatenate([t[:, hd * HEAD_PAD:hd * HEAD_PAD + NOPE], kr], axis=1)
            ks.append(_rope((th * _rms_stats(th, QK)) * g, cos, slo, shi))
            vs.append(t[:, hd * HEAD_PAD + NOPE:(hd + 1) * HEAD_PAD])
        return (jnp.concatenate(ks, axis=1), jnp.concatenate(vs, axis=1)), ()

    (k_att, v_att), _, (mine_out,) = _rowwise(k_post_fn, [kv_raw, kvin, cos_k, slo_k, shi_k], [gk_p],
                                              [(H * HEAD_PAD, BF16), (H * VD, BF16)], name="k_post",
                                              carry=_ChipExchange([halves[5]], gather=True))
    attn_o, lse, (mine_ffi,) = _attn_fwd(q_att, k_att, v_att, heads=H, carry=_ChipExchange([halves[6]], gather=True))
    mine4 = [mine_bra, mine_brs, mine_out, mine_ffi]

    ws3 = w_spatial[0]
    bs_t = jnp.pad(b_spatial[0].T, ((0, 0), (0, LANES - G)))

    def sgu_parts(u_in, v_in, ng, nb):
        u, v = _gelu(u_in.astype(F32)), _gelu(v_in.astype(F32))
        mu = jnp.mean(v, axis=-1, keepdims=True)
        vc = v - mu
        rs = lax.rsqrt(jnp.mean(vc * vc, axis=-1, keepdims=True) + EPS)
        xhat = vc * rs
        return u, xhat, rs, (xhat * ng + nb).astype(BF16)

    def sgu_fwd_fn(u_in, v_in, ng, nb, ws, bst):
        u, _, _, vnb = sgu_parts(u_in, v_in, ng, nb)
        outs = []
        for g in range(G):
            sl = slice(g * GD, (g + 1) * GD)
            mixed = jnp.dot(ws[g].astype(BF16), vnb[:, sl], preferred_element_type=F32) + bst[:, g:g + 1]
            outs.append(u[:, sl] * mixed)
        return (jnp.concatenate(outs, axis=1),), ()

    (sgu_o,), _, theirs4 = _rowwise(sgu_fwd_fn, [u_in, v_in], [sgu_norm_g, sgu_norm_b, ws3, bs_t], [(SW, BF16)],
                                    name="sgu_fwd", tm=CH, carry=_PairExchange(mine4, "forward"))
    w_bra, w_brs, w_out_f, w_ffi = finish_gather(attn_group, mine4, theirs4)
    w_fa, w_fb = w_ffi[:, :DFF], w_ffi[:, DFF:]

    a1 = _mm([(attn_o, w_bra)], name="br_attn", outs=(BF16,))
    def merge_epi(acc, a1v, gi1, gi2):
        return acc, _sigmoid(gi1.astype(F32)) * a1v.astype(F32) + _sigmoid(gi2.astype(F32)) * acc

    a2, merged = _mm([(sgu_o, w_brs)], name="br_sgu", outs=(BF16, BF16),
                     extras=[(a1, "mn"), (g1_in, "mn"), (g2_in, "mn")], epi=merge_epi)

    def res_gate(acc, res, gate):
        return res + gate * acc, acc

    x1, mo = _mm([(merged, w_out_f)], name="out_proj", outs=(F32, BF16), tn=1024,
                 extras=[(x2, "mn"), (g1, "n")], epi=res_gate)
    (h2,), _ = _rowwise(norm_mod_fn, [x1], [norm2_g, sh2, sc2], [(D, BF16)], name="norm2")

    def swiglu_epi(a, b):
        return a, b, (a * _sigmoid(a)) * b

    (fa, fb, act), mine4 = _mm([(h2, w_fa, w_fb)], name="ffn_in", outs=(BF16, BF16, BF16), tn=512, epi=swiglu_epi,
                               carry=_ChipExchange([halves[i] for i in ffn_group], gather=True))
    (w_ffo,) = finish_gather(ffn_group, mine4, _exchange_alone(_PairExchange(mine4, "forward"), name="ag_forward_ffn"))
    def loss_epi(acc, res, t, gate):
        e = (res + gate * acc) - t
        dy = e * (1.0 / D)
        return dy, gate * dy, _colsum(e * e) * (0.5 / D), _colsum(dy * acc)

    dy, df, loss_part, dg2_part = _mm([(act, w_ffo)], name="ffn_out", outs=(F32, BF16), tn=1024, col_sums=2,
                                      extras=[(x1, "mn"), (tgt2, "mn"), (g2, "n")], epi=loss_epi)

    def fold_fn(a, b):
        return (), (_colsum(a), _colsum(b))

    _, (loss_cols, dg2) = _rowwise(fold_fn, [loss_part[:, 0, :], dg2_part[:, 0, :]], [], [], [(1, D), (1, D)],
                                   name="loss_fold", tm=loss_part.shape[0])

    def swiglu_bwd_epi(dact, a, b):
        a, b = a.astype(F32), b.astype(F32)
        s = _sigmoid(a)
        return dact * b * (s * (1.0 + a * (1.0 - s))), dact * (a * s)

    da, db = _mm([(df, w_ffo)], tb=True, name="ffn_out_dx", outs=(BF16, BF16), tn=512,
                 extras=[(fa, "mn"), (fb, "mn")], epi=swiglu_bwd_epi)
    dw_ffo = _mm([(act, df)], ta=True, name="ffn_out_dw", outs=(BF16,))
    dh2 = _mm([(da, w_fa), (db, w_fb)], tb=True, name="ffn_in_dx", outs=(F32,))
    ns_ffi = w_ffn_in.shape[-1]
    dw_ffi = _mm([(h2, da)], ta=True, name="ffn_in_dw_a", outs=(BF16,), tn=1408, split=ns_ffi,
                 into=(lax.empty((4, D, ns_ffi), BF16), 0))
    dw_ffi = _mm([(h2, db)], ta=True, name="ffn_in_dw_b", outs=(BF16,), tn=1408, split=ns_ffi, into=(dw_ffi, 2))

    def norm2_bwd_fn(dh, t, dyv, mov, g, sc, g1v):
        r = _rms_stats(t, D)
        tn = t * r
        dxg = dh * (1.0 + sc)
        dt = dyv + _rms_bwd(dxg * g, tn, r, D)
        return (dt, g1v * dt), (_colsum(dh), _colsum(dh * (tn * g)), _colsum(dxg * tn), _colsum(dt * mov.astype(F32)))

    (dx1, dmo), (dsh2, dsc2, dn2g, dg1) = _rowwise(
        norm2_bwd_fn, [dh2, x1, dy, mo], [norm2_g, sc2, g1], [(D, F32), (D, BF16)], [(1, D)] * 4, name="norm2_bwd")

    def merge_bwd_epi(dm, a1, a2, gi1, gi2):
        s1, s2 = _sigmoid(gi1.astype(F32)), _sigmoid(gi2.astype(F32))
        a1, a2 = a1.astype(F32), a2.astype(F32)
        return dm * s1, dm * s2, dm * a1 * (s1 * (1.0 - s1)), dm * a2 * (s2 * (1.0 - s2))

    da1, da2, dgi1, dgi2 = _mm([(dmo, w_out_f)], tb=True, name="out_proj_dx", outs=(BF16,) * 4, tn=512,
                               extras=[(a1, "mn"), (a2, "mn"), (g1_in, "mn"), (g2_in, "mn")], epi=merge_bwd_epi)
    dw_out = _mm([(merged, dmo)], ta=True, name="out_proj_dw", outs=(BF16,))
    dattn = _mm([(da1, w_bra)], tb=True, name="br_attn_dx", outs=(BF16,))
    dw_bra = _mm([(attn_o, da1)], ta=True, name="br_attn_dw", outs=(BF16,), split=w_br_attn.shape[-1])
    dsgu = _mm([(da2, w_brs)], tb=True, name="br_sgu_dx", outs=(BF16,))
    dw_brs = _mm([(sgu_o, da2)], ta=True, name="br_sgu_dw", outs=(BF16,), split=w_br_sgu.shape[-1])

    def sgu_bwd_fn(dso, u_in, v_in, ng, nb, ws, bst):
        u, xhat, rs, vnb = sgu_parts(u_in, v_in, ng, nb)
        dso = dso.astype(F32)
        lane = lax.broadcasted_iota(jnp.int32, (CH, LANES), 1)
        du, dvn, dws, dbs = [], [], [], jnp.zeros((CH, LANES), F32)
        for g in range(G):
            sl = slice(g * GD, (g + 1) * GD)
            wg = ws[g].astype(BF16)
            mixed = jnp.dot(wg, vnb[:, sl], preferred_element_type=F32) + bst[:, g:g + 1]
            du.append(dso[:, sl] * mixed)
            dmix = dso[:, sl] * u[:, sl]
            dmb = dmix.astype(BF16)
            dws.append(lax.dot_general(dmb, vnb[:, sl], (((1,), (1,)), ((), ())), preferred_element_type=F32))
            dbs = dbs + jnp.where(lane == g, jnp.sum(dmix, axis=1, keepdims=True), 0.0)
            dvn.append(lax.dot_general(wg, dmb, (((0,), (0,)), ((), ())), preferred_element_type=F32))
        du, dvn = jnp.concatenate(du, axis=1), jnp.concatenate(dvn, axis=1)
        dxh = dvn * ng
        dv = rs * (dxh - jnp.mean(dxh, axis=-1, keepdims=True) - xhat * jnp.mean(dxh * xhat, axis=-1, keepdims=True))
        return ((du * _gelu_grad(u_in.astype(F32)), dv * _gelu_grad(v_in.astype(F32))),
                (_colsum(dvn * xhat), _colsum(dvn), jnp.stack(dws), dbs))

    core = jnp.reshape(ac, (1,)).astype(jnp.int32)

    def dest_layout(dwf, i):
        K, Ns = big[i].shape
        if dwf.ndim == 2:
            dwf = dwf.reshape(K, 4, Ns).transpose(1, 0, 2) if col_sharded[i] else dwf.reshape(4, K, Ns)
        return dwf.reshape(4, 2, K // 2, Ns)

    def pair_sums(idx, g4, sib):
        return [_pair_add(g, s, core, name="rs_pair_add_" + tags[i]) for g, s, i in zip(g4, sib, idx)]

    early = [3, 4, 5, 6, 7]
    g4_early = [dest_layout(d, i) for d, i in zip([dw_bra, dw_brs, dw_out, dw_ffi, dw_ffo], early)]
    (du_in, dv_in), (d_sng, d_snb, d_ws, d_bs), sib_early = _rowwise(
        sgu_bwd_fn, [dsgu, u_in, v_in], [sgu_norm_g, sgu_norm_b, ws3, bs_t], [(SW, BF16), (SW, BF16)],
        [(1, SW), (1, SW), (G, CH, CH), (CH, LANES)], name="sgu_bwd", tm=CH, carry=_PairExchange(g4_early, "halves"))
    pair_early = pair_sums(early, g4_early, sib_early)
    dq_att, dk_att, dv_att, xchg_early = _attn_bwd(q_att, k_att, v_att, attn_o, lse, dattn, heads=H,
                                                   carry=_ChipExchange(pair_early, gather=False))

    def q_post_bwd_fn(dq, t, cos, slo, shi, g):
        outs, dg = [], jnp.zeros((1, HEAD_PAD), F32)
        for hd in range(H):
            sl = slice(hd * HEAD_PAD, (hd + 1) * HEAD_PAD)
            th = t[:, sl]
            r = _rms_stats(th, QK)
            tn = th * r
            dn = _rope_t(dq[:, sl] * scale, cos, slo, shi)
            dg = dg + _colsum(dn * tn)
            outs.append(_rms_bwd(dn * g, tn, r, QK))
        return (jnp.concatenate(outs, axis=1),), (dg,)

    (dq_raw,), (d_gq,) = _rowwise(q_post_bwd_fn, [dq_att, q_raw, cos_t, slo_t, shi_t], [gq_p],
                                  [(H * HEAD_PAD, BF16)], [(1, HEAD_PAD)], name="q_post_bwd")

    def k_post_bwd_fn(dk, dv, t, kvi, cos, slo, shi, g):
        kr = kvi[:, KVL:]
        outs, dg, dkr = [], jnp.zeros((1, HEAD_PAD), F32), jnp.zeros_like(kr)
        for hd in range(H):
            th = jnp.concatenate([t[:, hd * HEAD_PAD:hd * HEAD_PAD + NOPE], kr], axis=1)
            r = _rms_stats(th, QK)
            tn = th * r
            dn = _rope_t(dk[:, hd * HEAD_PAD:(hd + 1) * HEAD_PAD], cos, slo, shi)
            dg = dg + _colsum(dn * tn)
            dt = _rms_bwd(dn * g, tn, r, QK)
            dkr = dkr + dt[:, NOPE:]
            outs += [dt[:, :NOPE], dv[:, hd * VD:(hd + 1) * VD]]
        return (jnp.concatenate(outs, axis=1), dkr), (dg,)

    def reduced_halves(idx, xchg, pair):
        filled = [lax.dynamic_update_index_in_dim(t4, lax.dynamic_index_in_dim(pr, my_chip, 0, keepdims=False),
                                                  my_chip, 0) for t4, pr in zip(xchg, pair)]
        return [_sum_blocks(t4, name="rs_sum_" + tags[i], out_dtype=F32) for t4, i in zip(filled, idx)]

    red_early = reduced_halves(early, xchg_early, pair_early)
    (dkv_raw, dkrope), (d_gk,), other_early = _rowwise(
        k_post_bwd_fn, [dk_att, dv_att, kv_raw, kvin, cos_k, slo_k, shi_k], [gk_p],
        [(H * HEAD_PAD, BF16), (2 * ROPE_DIM, F32)], [(1, HEAD_PAD)], name="k_post_bwd",
        carry=_PairExchange(red_early, "gather"))

    dqn = _mm([(dq_raw, w_uq_p)], tb=True, name="q_up_dx", outs=(F32,))
    dw_uq_p = _mm([(qn, dq_raw)], ta=True, name="q_up_dw", outs=(BF16,))
    dkvn = _mm([(dkv_raw, w_ukv_f)], tb=True, name="kv_up_dx", outs=(F32,))
    dw_ukv = _mm([(kvn, dkv_raw)], ta=True, name="kv_up_dw", outs=(BF16,))

    def q_norm_bwd_fn(dn, t, g):
        r = _rms_stats(t, QL)
        tn = t * r
        return (_rms_bwd(dn * g, tn, r, QL),), (_colsum(dn * tn),)

    (dqc,), (d_qng,) = _rowwise(q_norm_bwd_fn, [dqn, qc], [q_norm_g], [(QL, BF16)], [(1, QL)], name="q_norm_bwd")

    def kv_norm_bwd_fn(dn, dkr, t, g):
        kvc = t[:, :KVL]
        r = _rms_stats(kvc, KVL)
        tn = kvc * r
        return (jnp.concatenate([_rms_bwd(dn * g, tn, r, KVL), dkr], axis=1),), (_colsum(dn * tn),)

    (dkvin,), (d_kvng,) = _rowwise(kv_norm_bwd_fn, [dkvn, dkrope, kvin], [kv_norm_g], [(KVP, BF16)], [(1, KVL)],
                                   name="kv_norm_bwd")
    dkvin_x, dkvin_c = dkvin[:N], dkvin[N:]

    dctx_h = _mm([(dkvin_c, w_kv)], tb=True, name="proj_kv_ctx_dx", outs=(F32,))
    dw_q = _mm([(h, dqc)], ta=True, name="proj_q_dw", outs=(BF16,))
    dw_kv = _mm([(h, dkvin_x), (ctx_h, dkvin_c)], ta=True, name="proj_kv_dw", outs=(BF16,))
    dw_u = _mm([(h, du_in)], ta=True, name="proj_u_dw", outs=(BF16,))
    dw_v = _mm([(h, dv_in)], ta=True, name="proj_v_dw", outs=(BF16,))
    dw_g1 = _mm([(h, dgi1)], ta=True, name="proj_g1_dw", outs=(BF16,))
    dw_g2 = _mm([(h, dgi2)], ta=True, name="proj_g2_dw", outs=(BF16,))

    dw_in_f = jnp.concatenate([dw_q, dw_kv[:, :KVL + ROPE_DIM], dw_u, dw_v, dw_g1, dw_g2], axis=1)
    dw_uq_f = dw_uq_p.reshape(QL, H, HEAD_PAD)[:, :, :QK].reshape(QL, H * QK)
    late = [0, 1, 2]
    g4_late = [dest_layout(d, i) for d, i in zip([dw_in_f, dw_uq_f, dw_ukv], late)]
    pair_late = pair_sums(late, g4_late, _exchange_alone(_PairExchange(g4_late, "halves"), name="rs_pair_late"))
    dh, xchg_late = _mm([(dqc, w_q), (dkvin_x, w_kv), (du_in, w_u), (dv_in, w_v), (dgi1, w_g1), (dgi2, w_g2)],
                        tb=True, name="proj_dx", outs=(F32,), tn=1024, tk=512,
                        carry=_ChipExchange(pair_late, gather=False))

    def norm1_bwd_fn(dhv, t, dres, g, sc):
        r = _rms_stats(t, D)
        tn = t * r
        dxg = dhv * (1.0 + sc)
        return (dres + _rms_bwd(dxg * g, tn, r, D),), (_colsum(dhv), _colsum(dhv * (tn * g)), _colsum(dxg * tn))

    (grad_x,), (dsh1, dsc1, dn1g_x) = _rowwise(norm1_bwd_fn, [dh, x2, dx1], [norm1_g, sc1], [(D, F32)], [(1, D)] * 3,
                                               name="norm1_bwd")
    _, (dsh1c, dsc1c, dn1g_c) = _rowwise(norm1_bwd_fn, [dctx_h, ctx2, jnp.zeros_like(ctx2)], [norm1_g, sc1c],
                                         [(D, F32)], [(1, D)] * 3, name="norm1_ctx_bwd")

    small = [dsh1, dsc1, dg1, dsh2, dsc2, dg2,
             dsh1c, dsc1c, dn1g_x, dn1g_c, d_qng, d_kvng, d_gq, d_gk, d_sng, d_snb, dn2g, loss_cols]
    small_sizes = [a.shape[1] for a in small]
    sm_row = jnp.concatenate(small, axis=1)
    sm_mat = jnp.concatenate([d_ws.reshape(G * CH, CH), d_bs], axis=0)
    row_all, mat_all = _all_gather8([sm_row, sm_mat], name="ag_small", in_vmem=True)
    row_sum = _sum_blocks(row_all, name="sum_small_rows", out_dtype=F32)
    mat_sum = _sum_blocks(mat_all, name="sum_small_mats", out_dtype=F32)
    dmod_rows = row_all[:, 0, :NMOD]
    (_, _, _, _, _, _, t_sh1c, t_sc1c, t_n1x, t_n1c, g_qng, g_kvng, t_gq, t_gk, g_sng, g_snb, g_n2g,
     t_loss) = _split_lanes(row_sum, small_sizes)
    g_ws, t_bs = mat_sum[:G * CH], mat_sum[G * CH:]
    dmodc_row = jnp.concatenate([t_sh1c, t_sc1c, jnp.zeros((1, NMOD - 2 * D), F32)], axis=1)
    dmod16 = jnp.concatenate([dmod_rows, dmodc_row, jnp.zeros((BF16_SUBLANES - 9, NMOD), F32)], axis=0)

    def small_fn(rows, n1x, n1c, lossv):
        return (), (_colsum(rows), n1x + n1c, jnp.sum(lossv, axis=1, keepdims=True))

    _, (g_bmod, g_n1g, loss11) = _rowwise(small_fn, [dmod16], [t_n1x, t_n1c, t_loss], [], [(1, NMOD), (1, D), (1, 1)],
                                          name="small_reduce", tm=16)
    dmod_loc = lax.dynamic_slice_in_dim(dmod16, my_chip * NM, NM, axis=1)
    g_wmod = _mm([(silu_c, dmod_loc)], ta=True, name="mod_dw", outs=(F32,), tn=512)
    dsilu_part = _mm([(dmod_loc, wm)], tb=True, name="mod_dx", outs=(F32,), tk=512)
    part_all = _all_gather8([dsilu_part[8:9]], name="ag_cctx", in_vmem=True)[0]

    def cctx_fn(parts, dsl):
        return (), ((parts[0:1] + parts[2:3] + parts[4:5] + parts[6:7]) * dsl,)

    _, (g_cctx,) = _rowwise(cctx_fn, [part_all[:, 0, :]], [dsilu_c[8:9]], [], [(1, D)], name="cctx_grad", tm=8)

    red_late = reduced_halves(late, xchg_late, pair_late)
    other_late = _exchange_alone(_PairExchange(red_late, "gather"), name="rs_halves_late")
    big_grads = [lax.dynamic_update_index_in_dim(r, mine, ac, 0).reshape(a.shape)
                 for r, mine, a in zip(other_late + other_early, red_late + red_early, big)]
    mod_upd = _adamw(w_mod[0], g_wmod, m_w_mod[0], v_w_mod[0], name="adamw_w_mod")

    def upd(w, g, m, v, nm):
        shape = w.shape
        w2, g2_, m2, v2 = [t.reshape(-1, shape[-1]) for t in (w, g, m, v)]
        d_, m_, v_ = _adamw(w2, g2_, m2, v2, name="adamw_" + nm)
        return g.reshape(shape), d_.reshape(shape), m_.reshape(shape), v_.reshape(shape)

    g_in, g_uq, g_ukv, g_bra, g_brs, g_out, g_ffi, g_ffo = big_grads
    grads = dict(
        c_ctx=g_cctx.reshape(D), w_mod=g_wmod[None], b_mod=g_bmod, norm1_g=g_n1g, w_in=g_in[None],
        q_norm_g=g_qng, kv_norm_g=g_kvng, w_uq=g_uq[None], w_ukv=g_ukv[None],
        qk_norm_q=t_gq[:, :QK], qk_norm_k=t_gk[:, :QK], sgu_norm_g=g_sng, sgu_norm_b=g_snb,
        w_spatial=g_ws.reshape(w_spatial.shape), b_spatial=t_bs[:, :G].T[None],
        w_br_attn=g_bra[None], w_br_sgu=g_brs[None], w_out=g_out[None], norm2_g=g_n2g,
        w_ffn_in=g_ffi[None], w_ffn_out=g_ffo[None])
    weights = dict(c_ctx=c_ctx, w_mod=w_mod, b_mod=b_mod, norm1_g=norm1_g, w_in=w_in, q_norm_g=q_norm_g,
                   kv_norm_g=kv_norm_g, w_uq=w_uq, w_ukv=w_ukv, qk_norm_q=qk_norm_q, qk_norm_k=qk_norm_k,
                   sgu_norm_g=sgu_norm_g, sgu_norm_b=sgu_norm_b, w_spatial=w_spatial, b_spatial=b_spatial,
                   w_br_attn=w_br_attn, w_br_sgu=w_br_sgu, w_out=w_out, norm2_g=norm2_g, w_ffn_in=w_ffn_in,
                   w_ffn_out=w_ffn_out)
    m_in = dict(c_ctx=m_c_ctx, w_mod=m_w_mod, b_mod=m_b_mod, norm1_g=m_norm1_g, w_in=m_w_in, q_norm_g=m_q_norm_g,
                kv_norm_g=m_kv_norm_g, w_uq=m_w_uq, w_ukv=m_w_ukv, qk_norm_q=m_qk_norm_q, qk_norm_k=m_qk_norm_k,
                sgu_norm_g=m_sgu_norm_g, sgu_norm_b=m_sgu_norm_b, w_spatial=m_w_spatial, b_spatial=m_b_spatial,
                w_br_attn=m_w_br_attn, w_br_sgu=m_w_br_sgu, w_out=m_w_out, norm2_g=m_norm2_g, w_ffn_in=m_w_ffn_in,
                w_ffn_out=m_w_ffn_out)
    v_in_ = dict(c_ctx=v_c_ctx, w_mod=v_w_mod, b_mod=v_b_mod, norm1_g=v_norm1_g, w_in=v_w_in, q_norm_g=v_q_norm_g,
                 kv_norm_g=v_kv_norm_g, w_uq=v_w_uq, w_ukv=v_w_ukv, qk_norm_q=v_qk_norm_q, qk_norm_k=v_qk_norm_k,
                 sgu_norm_g=v_sgu_norm_g, sgu_norm_b=v_sgu_norm_b, w_spatial=v_w_spatial, b_spatial=v_b_spatial,
                 w_br_attn=v_w_br_attn, w_br_sgu=v_w_br_sgu, w_out=v_w_out, norm2_g=v_norm2_g, w_ffn_in=v_w_ffn_in,
                 w_ffn_out=v_w_ffn_out)
    names = list(weights)
    big_names = ("w_mod", "w_in", "w_uq", "w_ukv", "w_br_attn", "w_br_sgu", "w_out", "w_ffn_in", "w_ffn_out")
    out_g, out_d, out_m, out_v = {}, {}, {}, {}
    out_g["w_mod"] = grads["w_mod"]
    out_d["w_mod"], out_m["w_mod"], out_v["w_mod"] = [t[None] for t in mod_upd[:3]]
    for nm in big_names[1:]:
        out_g[nm], out_d[nm], out_m[nm], out_v[nm] = upd(weights[nm], grads[nm], m_in[nm], v_in_[nm], nm)
    row_names = [nm for nm in names if nm not in big_names and nm not in ("w_spatial", "b_spatial")]
    widths = [-(-weights[nm].size // LANES) * LANES for nm in row_names]

    def as_row(d):
        return jnp.concatenate([jnp.pad(d[nm].reshape(1, -1), ((0, 0), (0, wd - d[nm].size)))
                                for nm, wd in zip(row_names, widths)], axis=1)

    def as_mat(d):
        return jnp.concatenate([d["w_spatial"].reshape(G * CH, CH), d["b_spatial"].reshape(G, CH)], axis=0)

    row_res = _adamw(as_row(weights), as_row(grads), as_row(m_in), as_row(v_in_), name="adamw_rows")
    mat_res = _adamw(as_mat(weights), as_mat(grads), as_mat(m_in), as_mat(v_in_), name="adamw_spatial")
    for tgt, row, mat in zip((out_d, out_m, out_v), row_res, mat_res):
        for nm, seg in zip(row_names, _split_lanes(row, widths)):
            tgt[nm] = seg[:, :weights[nm].size].reshape(weights[nm].shape)
        tgt["w_spatial"] = mat[:G * CH].reshape(w_spatial.shape)
        tgt["b_spatial"] = mat[G * CH:].reshape(b_spatial.shape)
    for nm in row_names + ["w_spatial", "b_spatial"]:
        out_g[nm] = grads[nm].reshape(weights[nm].shape)

    loss = loss11.reshape(())
    return (loss, grad_x[None], *[out_g[n] for n in names], *[out_d[n] for n in names],
            *[out_m[n] for n in names], *[out_v[n] for n in names])
```

```python
import math

import jax
import jax.numpy as jnp
from jax import lax
from jax.experimental import pallas as pl
from jax.experimental.pallas import tpu as pltpu

F32, BF16 = jnp.float32, jnp.bfloat16
MESH = pl.DeviceIdType.MESH

LANES = 128
F32_SUBLANES = 8
BF16_SUBLANES = 16
MXU_DIM = 256
VMEM_LIMIT_BYTES = 56 * 1024 * 1024

EPS = 1e-6
ROPE_DIM = 64
ROPE_THETA = 10000.0
GRID_W = 64
HEAD_PAD = 256
ADAM_LR, ADAM_B1, ADAM_B2, ADAM_EPS, ADAM_WD, ADAM_STEP = 0.001, 0.9, 0.999, 1e-08, 0.01, 10


def _tile(dim, pref, align=LANES):
    if dim <= pref:
        return dim
    t = (pref // align) * align
    while t >= align:
        if dim % t == 0:
            return t
        t -= align
    return dim


def _params(sem=None):
    return pltpu.CompilerParams(dimension_semantics=sem, vmem_limit_bytes=VMEM_LIMIT_BYTES)


def _sds(shape, dtype):
    return jax.ShapeDtypeStruct(tuple(shape), dtype)


def _mm(pairs, *, name, ta=False, tb=False, outs=(F32,), tm=1024, tn=1024, tk=2048, extras=(), epi=None,
        split=None, into=None, carry=None, col_sums=0):
    dual = len(pairs[0]) == 3
    a0, b0 = pairs[0][0], pairs[0][1]
    M = a0.shape[1] if ta else a0.shape[0]
    N = b0.shape[0] if tb else b0.shape[1]
    tm, tn = _tile(M, tm), _tile(N if split is None else split, tn)
    ks = [(p[0].shape[0] if ta else p[0].shape[1]) for p in pairs]
    tks = [_tile(k, tk) for k in ks]
    nks = [k // t for k, t in zip(ks, tks)]
    offs = [sum(nks[:i]) for i in range(len(pairs))]
    nk_total = sum(nks)
    single = len(pairs) == 1

    def kidx(kk, p):
        return kk if single else jnp.clip(kk - offs[p], 0, nks[p] - 1)

    in_specs, operands = [], []
    for p, pr in enumerate(pairs):
        if ta:
            in_specs.append(pl.BlockSpec((tks[p], tm), lambda i, j, kk, p=p: (kidx(kk, p), i)))
        else:
            in_specs.append(pl.BlockSpec((tm, tks[p]), lambda i, j, kk, p=p: (i, kidx(kk, p))))
        operands.append(pr[0])
        for b in pr[1:]:
            if tb:
                in_specs.append(pl.BlockSpec((tn, tks[p]), lambda i, j, kk, p=p: (j, kidx(kk, p))))
            else:
                in_specs.append(pl.BlockSpec((tks[p], tn), lambda i, j, kk, p=p: (kidx(kk, p), j)))
            operands.append(b)
    for arr, kind in extras:
        if kind == "mn":
            in_specs.append(pl.BlockSpec((tm, tn), lambda i, j, kk: (i, j)))
        else:
            in_specs.append(pl.BlockSpec((1, tn), lambda i, j, kk: (0, j)))
        operands.append(arr)
    n_in = len(operands)
    n_ex = len(extras)
    per = 3 if dual else 2
    dims = (((0 if ta else 1,), (1 if tb else 0,)), ((), ()))

    n_acc = 2 if dual else 1

    def products(ins, p):
        a = ins[per * p][...].astype(BF16)
        return [lax.dot_general(a, ins[per * p + 1 + q][...].astype(BF16), dims, preferred_element_type=F32)
                for q in range(n_acc)]

    def finish(ins, out_refs, acc_vals):
        vals = acc_vals + [r[...] for r in ins[n_in - n_ex:]]
        res = epi(*vals) if epi is not None else (vals[0],)
        for o, r in zip(out_refs, res):
            o[...] = jnp.broadcast_to(r, o.shape).astype(o.dtype)

    row_halves = epi is not None and not ta and col_sums == 0 and split is None and tm >= 2 * MXU_DIM

    def one_step(ins, out_refs):
        if not row_halves:
            finish(ins, out_refs, products(ins, 0))
            return
        for sb in range(2):
            rows = pl.ds(sb * (tm // 2), tm // 2)
            a = ins[0][rows, :].astype(BF16)
            vals = [lax.dot_general(a, ins[1 + q][...].astype(BF16), dims, preferred_element_type=F32)
                    for q in range(n_acc)]
            vals += [r[rows, :] if kind == "mn" else r[...] for r, (_, kind) in zip(ins[n_in - n_ex:], extras)]
            for o, r in zip(out_refs, epi(*vals)):
                o[rows, :] = r.astype(o.dtype)

    out_specs = [pl.BlockSpec((tm, tn), lambda i, j, kk: (i, j)) for _ in outs]
    out_specs += [pl.BlockSpec((None, F32_SUBLANES, tn), lambda i, j, kk: (i, 0, j)) for _ in range(col_sums)]
    out_shape = [_sds((M, N), d) for d in outs] + [_sds((M // tm, F32_SUBLANES, N), F32) for _ in range(col_sums)]
    aliases = {}
    n_alias = 0
    if split is not None:
        nps = split // tn
        lead = 0 if into is None else into[1]
        out_specs = [pl.BlockSpec((None, tm, tn), lambda i, j, kk: (j // nps + lead, i, j % nps))]
        out_shape = [_sds((N // split if into is None else into[0].shape[0], M, split), outs[0])]
        if into is not None:
            in_specs.append(pl.BlockSpec(memory_space=pl.ANY))
            operands.append(into[0])
            aliases, n_alias = {n_in: 0}, 1

    grid = (M // tm, N // tn, nk_total)
    n_out = len(outs) + col_sums

    def at_step(first):
        ids = [pl.program_id(d) for d in range(3)]
        cond = None
        for d, g in zip(ids, grid):
            t = d == (0 if first else g - 1)
            cond = t if cond is None else cond & t
        return cond

    def body(*refs):
        ins, out_refs, accs, start, wait = _split_refs(refs, n_in + n_alias, n_out, carry)
        ins = ins[:n_in]
        if carry is not None:
            pl.when(at_step(True))(start)
        if nk_total == 1:
            one_step(ins, out_refs)
        else:
            kk = pl.program_id(2)

            @pl.when(kk == 0)
            def _():
                for acc, v in zip(accs, products(ins, 0)):
                    acc[...] = v

            for p in range(len(pairs)):
                lo = max(offs[p], 1)

                @pl.when((kk >= lo) & (kk < offs[p] + nks[p]))
                def _(p=p):
                    for acc, v in zip(accs, products(ins, p)):
                        acc[...] += v

            @pl.when(kk == nk_total - 1)
            def _():
                finish(ins, out_refs, [acc[...] for acc in accs])
        if carry is not None:
            pl.when(at_step(False))(wait)

    ex = carry
    res = pl.pallas_call(
        body, name=name, grid=grid, in_specs=in_specs + ([] if ex is None else ex.in_specs),
        out_specs=out_specs + ([] if ex is None else ex.out_specs),
        out_shape=out_shape + ([] if ex is None else ex.out_shape), input_output_aliases=aliases,
        scratch_shapes=[pltpu.VMEM((tm, tn), F32) for _ in range(n_acc if nk_total > 1 else 0)]
        + ([] if ex is None else ex.scratch),
        compiler_params=_params(("arbitrary",) * 3 if ex is not None else ("parallel", "parallel", "arbitrary")),
    )(*operands, *([] if ex is None else ex.xs))
    if ex is not None:
        return (res[0] if n_out == 1 else res[:n_out]), list(res[n_out:])
    return res[0] if n_out == 1 else res


def _rowwise(fn, rows, vecs, out_rows, out_accs=(), *, name, tm=256, tc=None, carry=None):
    M = rows[0].shape[0]
    tm = _tile(M, tm, BF16_SUBLANES)
    nrow = M // tm
    C = rows[0].shape[1]
    ncol = 1 if tc is None else C // _tile(C, tc)
    tcol = None if tc is None else _tile(C, tc)

    def colwise(shape):
        return tc is not None and len(shape) == 2 and shape[0] == 1 and shape[1] == C

    def vspec(shape):
        if colwise(shape):
            return pl.BlockSpec((1, tcol), lambda j, i: (0, j))
        return pl.BlockSpec(tuple(shape), lambda j, i, n=len(shape): (0,) * n)

    def rspec(width):
        if tc is None:
            return pl.BlockSpec((tm, width), lambda j, i: (i, 0))
        return pl.BlockSpec((tm, tcol), lambda j, i: (i, j))

    in_specs = [rspec(r.shape[1]) for r in rows] + [vspec(v.shape) for v in vecs]
    out_specs = [rspec(c) for c, _ in out_rows] + [vspec(s) for s in out_accs]
    out_shape = [_sds((M, c), d) for c, d in out_rows] + [_sds(s, F32) for s in out_accs]
    n_in, n_or = len(rows) + len(vecs), len(out_rows)

    n_out = n_or + len(out_accs)
    ex = carry

    def body(*refs):
        ins, outs, _, start, wait = _split_refs(refs, n_in, n_out, ex)
        o_rows, o_accs = outs[:n_or], outs[n_or:]
        if ex is not None:
            pl.when((pl.program_id(0) == 0) & (pl.program_id(1) == 0))(start)
        r_out, a_out = fn(*[r[...] for r in ins])
        for o, r in zip(o_rows, r_out):
            o[...] = r.astype(o.dtype)
        i = pl.program_id(1)

        @pl.when(i == 0)
        def _():
            for o, a in zip(o_accs, a_out):
                o[...] = a

        @pl.when(i > 0)
        def _():
            for o, a in zip(o_accs, a_out):
                o[...] += a

        if ex is not None:
            pl.when((pl.program_id(0) == ncol - 1) & (pl.program_id(1) == nrow - 1))(wait)

    res = pl.pallas_call(
        body, name=name, grid=(ncol, nrow), in_specs=in_specs + ([] if ex is None else ex.in_specs),
        out_specs=out_specs + ([] if ex is None else ex.out_specs),
        out_shape=out_shape + ([] if ex is None else ex.out_shape),
        scratch_shapes=[] if ex is None else ex.scratch,
        compiler_params=_params(("arbitrary", "arbitrary") if ex is not None else ("parallel", "arbitrary")),
    )(*rows, *vecs, *([] if ex is None else ex.xs))
    if ex is not None:
        return res[:n_or], res[n_or:n_out], list(res[n_out:])
    return res[:n_or], res[n_or:]


def _colsum(t):
    return jnp.sum(t, axis=0, keepdims=True)


def _gelu(t):
    return 0.5 * t * (1.0 + lax.erf(t * math.sqrt(0.5)))


def _gelu_grad(t):
    return 0.5 * (1.0 + lax.erf(t * math.sqrt(0.5))) + t * jnp.exp(-0.5 * t * t) * (1.0 / math.sqrt(2.0 * math.pi))


def _sigmoid(t):
    return 1.0 / (1.0 + jnp.exp(-t))


def _rms_stats(t, width):
    return lax.rsqrt(jnp.sum(t * t, axis=-1, keepdims=True) * (1.0 / width) + EPS)


def _rms_bwd(dn, tn, r, width):
    return r * (dn - tn * (jnp.sum(dn * tn, axis=-1, keepdims=True) * (1.0 / width)))


def _place():
    return lax.axis_index("x"), lax.axis_index("y"), lax.axis_index("c")


class _ChipExchange:
    def __init__(self, xs, gather):
        self.xs, self.gather, self.n = list(xs), gather, len(xs)
        self.in_specs = [pl.BlockSpec(memory_space=pl.ANY)] * self.n
        self.out_specs = [pl.BlockSpec(memory_space=pl.ANY)] * self.n
        self.out_shape = [_sds((4,) + (x.shape if gather else x.shape[1:]), x.dtype) for x in self.xs]
        self.scratch = [pltpu.SemaphoreType.DMA((self.n, 3)), pltpu.SemaphoreType.DMA((self.n, 3))]

    def bind(self, x_refs, out_refs, send_sems, recv_sems):
        x, y, c = _place()
        p = 2 * x + y
        chips = [(1 - x, y), (x, 1 - y), (1 - x, 1 - y)]

        def copy(w, k, outgoing):
            qx, qy = chips[k]
            there = 2 * qx + qy
            if self.gather:
                src = x_refs[w]
            else:
                src = x_refs[w].at[there if outgoing else p]
            return pltpu.make_async_remote_copy(
                src_ref=src, dst_ref=out_refs[w].at[p if outgoing else there], send_sem=send_sems.at[w, k],
                recv_sem=recv_sems.at[w, k], device_id=(qx, qy, c), device_id_type=MESH)

        def start():
            for w in range(self.n):
                for k in range(3):
                    copy(w, k, True).start()

        def wait():
            for w in range(self.n):
                for k in range(3):
                    copy(w, k, False).wait_recv()
            for w in range(self.n):
                for k in range(3):
                    copy(w, k, True).wait_send()

        return start, wait


class _RelayGather:
    def __init__(self, xs):
        self.xs, self.n = list(xs), len(xs)
        self.in_specs = [pl.BlockSpec(memory_space=pl.ANY)] * self.n
        self.out_specs = [pl.BlockSpec(memory_space=pl.ANY)] * self.n
        self.out_shape = [_sds((4,) + x.shape, x.dtype) for x in self.xs]
        self.scratch = [pltpu.SemaphoreType.DMA((self.n, 4)), pltpu.SemaphoreType.DMA((self.n, 4))]
        self.relay = None

    def bind(self, x_refs, out_refs, send_sems, recv_sems):
        x, y, c = _place()
        own, xn, yn, dg = (x, y), (1 - x, y), (x, 1 - y), (1 - x, 1 - y)

        def slot(w, chip):
            return out_refs[w].at[2 * chip[0] + chip[1]]

        def cp(w, k, src, dst, chip):
            return pltpu.make_async_remote_copy(src_ref=src, dst_ref=dst, send_sem=send_sems.at[w, k],
                                                recv_sem=recv_sems.at[w, k], device_id=(*chip, c), device_id_type=MESH)

        def halves(w):
            h = x_refs[w].shape[0] // 2
            return pl.ds(0, h), pl.ds(h, h)

        def start():
            for w in range(self.n):
                cp(w, 0, x_refs[w], slot(w, own), xn).start()
                cp(w, 1, x_refs[w], slot(w, own), yn).start()

        def relay():
            for w in range(self.n):
                lo, hi = halves(w)
                cp(w, 1, x_refs[w], slot(w, yn), yn).wait_recv()
                cp(w, 2, slot(w, yn).at[lo], slot(w, yn).at[lo], xn).start()
                cp(w, 0, x_refs[w], slot(w, xn), xn).wait_recv()
                cp(w, 3, slot(w, xn).at[hi], slot(w, xn).at[hi], yn).start()

        def wait():
            for w in range(self.n):
                lo, hi = halves(w)
                cp(w, 2, slot(w, dg).at[lo], slot(w, dg).at[lo], xn).wait_recv()
                cp(w, 3, slot(w, dg).at[hi], slot(w, dg).at[hi], yn).wait_recv()
                cp(w, 0, x_refs[w], slot(w, own), xn).wait_send()
                cp(w, 1, x_refs[w], slot(w, own), yn).wait_send()
                cp(w, 2, slot(w, yn).at[lo], slot(w, yn).at[lo], xn).wait_send()
                cp(w, 3, slot(w, xn).at[hi], slot(w, xn).at[hi], yn).wait_send()

        self.relay = relay
        return start, wait


class _PairExchange:
    def __init__(self, xs, mode):
        self.xs, self.mode, self.n = list(xs), mode, len(xs)
        self.in_specs = [pl.BlockSpec(memory_space=pl.ANY)] * self.n
        self.out_specs = [pl.BlockSpec(memory_space=pl.ANY)] * self.n
        shape = {"halves": lambda s: (4,) + s[2:], "forward": lambda s: s, "gather": lambda s: (2,) + s}[mode]
        self.out_shape = [_sds(shape(x.shape), x.dtype) for x in self.xs]
        self.scratch = [pltpu.SemaphoreType.DMA((self.n, 3)), pltpu.SemaphoreType.DMA((self.n, 3))]

    def bind(self, x_refs, out_refs, send_sems, recv_sems):
        x, y, c = _place()
        chips = [(1 - x, y), (x, 1 - y), (1 - x, 1 - y)]

        def copy(w, src, dst, k):
            return pltpu.make_async_remote_copy(src_ref=src, dst_ref=dst, send_sem=send_sems.at[w, k],
                                                recv_sem=recv_sems.at[w, k], device_id=(x, y, 1 - c),
                                                device_id_type=MESH)

        def start():
            for w, (xr, orf) in enumerate(zip(x_refs, out_refs)):
                if self.mode == "halves":
                    for q in range(4):
                        copy(w, xr.at[q, 1 - c], orf.at[q], 0).start()
                elif self.mode == "forward":
                    for k, (qx, qy) in enumerate(chips):
                        copy(w, xr.at[2 * qx + qy], orf.at[2 * qx + qy], k).start()
                else:
                    copy(w, xr, orf.at[c], 0).start()

        def wait():
            for w, (xr, orf) in enumerate(zip(x_refs, out_refs)):
                if self.mode == "halves":
                    copy(w, orf, orf, 0).wait()
                elif self.mode == "forward":
                    for k, (qx, qy) in enumerate(chips):
                        copy(w, xr.at[2 * qx + qy], orf.at[2 * qx + qy], k).wait()
                else:
                    cp = copy(w, xr, orf.at[1 - c], 0)
                    cp.wait_recv()
                    cp.wait_send()

        return start, wait


def _split_refs(refs, n_in, n_out, ex):
    ne = 0 if ex is None else ex.n
    ins, xin = refs[:n_in], refs[n_in:n_in + ne]
    outs, xout = refs[n_in + ne:n_in + ne + n_out], refs[n_in + ne + n_out:n_in + 2 * ne + n_out]
    rest = refs[n_in + 2 * ne + n_out:]
    if ex is None:
        return ins, outs, rest, None, None
    start, wait = ex.bind(xin, xout, rest[-2], rest[-1])
    return ins, outs, rest[:-2], start, wait


def _attn_fwd(q, k, v, *, heads, tq=512, carry=None):
    N, M = q.shape[0], k.shape[0]
    tq = _tile(N, tq)
    sub = _tile(tq, MXU_DIM)
    vd = v.shape[1] // heads
    nq = N // tq

    def body(*refs):
        (q_ref, k_ref, v_ref), (o_ref, lse_ref), _, start, wait = _split_refs(refs, 3, 2, carry)
        if carry is not None:
            pl.when((pl.program_id(0) == 0) & (pl.program_id(1) == 0))(start)
            if getattr(carry, "relay", None) is not None:
                pl.when((pl.program_id(0) == (5 * heads) // 8) & (pl.program_id(1) == 0))(carry.relay)
        for sb in range(tq // sub):
            rows = pl.ds(sb * sub, sub)
            s = lax.dot_general(q_ref[rows, :], k_ref[...], (((1,), (1,)), ((), ())), preferred_element_type=F32)
            m = jnp.max(s, axis=-1, keepdims=True)
            p = jnp.exp(s - m)
            l = jnp.sum(p, axis=-1, keepdims=True)
            o = jnp.dot(p.astype(BF16), v_ref[...], preferred_element_type=F32) / l
            o_ref[rows, :] = o.astype(o_ref.dtype)
            lse_ref[rows, :] = jnp.broadcast_to(m + jnp.log(l), (sub, vd))
        if carry is not None:
            pl.when((pl.program_id(0) == heads - 1) & (pl.program_id(1) == nq - 1))(wait)

    ex = carry
    res = pl.pallas_call(
        body, name="attn_fwd", grid=(heads, nq),
        in_specs=[pl.BlockSpec((tq, HEAD_PAD), lambda h, i: (i, h)),
                  pl.BlockSpec((M, HEAD_PAD), lambda h, i: (0, h)),
                  pl.BlockSpec((M, vd), lambda h, i: (0, h))] + ([] if ex is None else ex.in_specs),
        out_specs=[pl.BlockSpec((tq, vd), lambda h, i: (i, h)),
                   pl.BlockSpec((tq, vd), lambda h, i: (i, h))] + ([] if ex is None else ex.out_specs),
        out_shape=[_sds((N, heads * vd), BF16), _sds((N, heads * vd), F32)] + ([] if ex is None else ex.out_shape),
        scratch_shapes=[] if ex is None else ex.scratch,
        compiler_params=_params(("arbitrary", "arbitrary")),
    )(q, k, v, *([] if ex is None else ex.xs))
    return res[0], res[1], list(res[2:])


def _attn_bwd(q, k, v, o, lse, do, *, heads, tq=512, carry=None):
    N, M = q.shape[0], k.shape[0]
    tq = _tile(N, tq)
    vd = v.shape[1] // heads
    nq = N // tq
    sub = _tile(tq, MXU_DIM)
    nt = (((1,), (1,)), ((), ()))
    tn = (((0,), (0,)), ((), ()))

    def body(*refs):
        (q_ref, k_ref, v_ref, o_ref, lse_ref, do_ref), (dq_ref, dk_ref, dv_ref), _, start, wait = _split_refs(
            refs, 6, 3, carry)
        if carry is not None:
            pl.when((pl.program_id(0) == 0) & (pl.program_id(1) == 0))(start)
        i = pl.program_id(1)
        kb, vb = k_ref[...], v_ref[...]
        parts = []
        for sb in range(tq // sub):
            rows = pl.ds(sb * sub, sub)
            qb, dob = q_ref[rows, :], do_ref[rows, :]
            delta = jnp.sum(dob.astype(F32) * o_ref[rows, :].astype(F32), axis=-1, keepdims=True)
            s = lax.dot_general(qb, kb, nt, preferred_element_type=F32)
            p = jnp.exp(s - lse_ref[rows, :][:, :1])
            dp = lax.dot_general(dob, vb, nt, preferred_element_type=F32)
            ds = (p * (dp - delta)).astype(BF16)
            dq_ref[rows, :] = jnp.dot(ds, kb, preferred_element_type=F32)
            parts.append((lax.dot_general(ds, qb, tn, preferred_element_type=F32),
                          lax.dot_general(p.astype(BF16), dob, tn, preferred_element_type=F32)))

        @pl.when(i == 0)
        def _():
            dk_ref[...] = parts[0][0]
            dv_ref[...] = parts[0][1]

        @pl.when(i > 0)
        def _():
            dk_ref[...] += parts[0][0]
            dv_ref[...] += parts[0][1]

        for dk_part, dv_part in parts[1:]:
            dk_ref[...] += dk_part
            dv_ref[...] += dv_part

        if carry is not None:
            pl.when((pl.program_id(0) == heads - 1) & (pl.program_id(1) == nq - 1))(wait)

    ex = carry
    res = pl.pallas_call(
        body, name="attn_bwd", grid=(heads, nq),
        in_specs=[pl.BlockSpec((tq, HEAD_PAD), lambda h, i: (i, h)),
                  pl.BlockSpec((M, HEAD_PAD), lambda h, i: (0, h)),
                  pl.BlockSpec((M, vd), lambda h, i: (0, h)),
                  pl.BlockSpec((tq, vd), lambda h, i: (i, h)),
                  pl.BlockSpec((tq, vd), lambda h, i: (i, h)),
                  pl.BlockSpec((tq, vd), lambda h, i: (i, h))] + ([] if ex is None else ex.in_specs),
        out_specs=[pl.BlockSpec((tq, HEAD_PAD), lambda h, i: (i, h)),
                   pl.BlockSpec((M, HEAD_PAD), lambda h, i: (0, h)),
                   pl.BlockSpec((M, vd), lambda h, i: (0, h))] + ([] if ex is None else ex.out_specs),
        out_shape=[_sds((N, heads * HEAD_PAD), F32), _sds((M, heads * HEAD_PAD), F32),
                   _sds((M, heads * vd), F32)] + ([] if ex is None else ex.out_shape),
        scratch_shapes=[] if ex is None else ex.scratch,
        compiler_params=_params(("arbitrary", "arbitrary")),
    )(q, k, v, o, lse, do, *([] if ex is None else ex.xs))
    return res[0], res[1], res[2], list(res[3:])


def _comm_call(body, xs, out_shapes, n_sems, name, in_vmem):
    space = pltpu.VMEM if in_vmem else pl.ANY
    n = len(xs)

    def wrapped(*refs):
        body(refs[:n], refs[n:2 * n], *refs[2 * n:])

    return pl.pallas_call(
        wrapped, name=name, out_shape=list(out_shapes),
        in_specs=[pl.BlockSpec(memory_space=space)] * n, out_specs=[pl.BlockSpec(memory_space=space)] * n,
        scratch_shapes=[pltpu.SemaphoreType.DMA((n, n_sems)), pltpu.SemaphoreType.DMA((n, n_sems)),
                        pltpu.SemaphoreType.DMA((n,))],
        compiler_params=pltpu.CompilerParams(vmem_limit_bytes=VMEM_LIMIT_BYTES),
    )(*xs)


def _all_gather8(blks, *, name, in_vmem):
    def body(x_refs, out_refs, send_sems, recv_sems, local_sems):
        x, y, c = _place()
        me, sibling = (x, y, c), (x, y, 1 - c)
        chips = [(1 - x, y), (x, 1 - y), (1 - x, 1 - y)]
        waits = []
        for w, (x_ref, out_ref) in enumerate(zip(x_refs, out_refs)):
            def slot(px, py, pc, out_ref=out_ref):
                return out_ref.at[4 * px + 2 * py + pc]

            def copy(k, block, to, src=None, w=w, slot=slot):
                return pltpu.make_async_remote_copy(
                    src_ref=slot(*block) if src is None else src, dst_ref=slot(*block),
                    send_sem=send_sems.at[w, k], recv_sem=recv_sems.at[w, k], device_id=to, device_id_type=MESH)

            mine = pltpu.make_async_copy(x_ref, slot(*me), local_sems.at[w])
            mine.start()
            first = [copy(0, me, sibling, src=x_ref)]
            first += [copy(1 + j, me, (*chip, c), src=x_ref) for j, chip in enumerate(chips)]
            for cp in first:
                cp.start()
            waits.append((copy, mine, first))
        for copy, mine, first in waits:
            passed = [copy(4 + j, (*chip, c), sibling) for j, chip in enumerate(chips)]
            for j, chip in enumerate(chips):
                copy(1 + j, (*chip, c), me).wait_recv()
                passed[j].start()
            copy(0, sibling, me).wait_recv()
            for j, chip in enumerate(chips):
                copy(4 + j, (*chip, 1 - c), me).wait_recv()
            for cp in first + passed:
                cp.wait_send()
            mine.wait()

    return _comm_call(body, blks, [_sds((8,) + b.shape, b.dtype) for b in blks], 7, name, in_vmem)


def _gather_others(blks, *, name):
    def body(x_refs, out_refs, send_sems, recv_sems, local_sems):
        x, y, c = _place()
        own, xn, yn, dg = (x, y), (1 - x, y), (x, 1 - y), (1 - x, 1 - y)

        def slot(w, chip, core):
            return out_refs[w].at[4 * chip[0] + 2 * chip[1] + core]

        def cp(w, k, src, dst, chip, core):
            return pltpu.make_async_remote_copy(src_ref=src, dst_ref=dst, send_sem=send_sems.at[w, k],
                                                recv_sem=recv_sems.at[w, k], device_id=(*chip, core),
                                                device_id_type=MESH)

        def halves(w):
            h = x_refs[w].shape[0] // 2
            return pl.ds(0, h), pl.ds(h, h)

        sends = []
        for w, x_ref in enumerate(x_refs):
            sends += [cp(w, 0, x_ref, slot(w, own, c), xn, c), cp(w, 1, x_ref, slot(w, own, c), yn, c)]
        for s in sends:
            s.start()
        for w, x_ref in enumerate(x_refs):
            lo, hi = halves(w)
            cp(w, 1, x_ref, slot(w, yn, c), yn, c).wait_recv()
            passed = [cp(w, 2, slot(w, yn, c).at[lo], slot(w, yn, c).at[lo], xn, c),
                      cp(w, 4, slot(w, yn, c), slot(w, yn, c), own, 1 - c)]
            cp(w, 0, x_ref, slot(w, xn, c), xn, c).wait_recv()
            passed += [cp(w, 3, slot(w, xn, c).at[hi], slot(w, xn, c).at[hi], yn, c),
                       cp(w, 5, slot(w, xn, c), slot(w, xn, c), own, 1 - c)]
            for s in passed:
                s.start()
            sends += passed
        for w in range(len(x_refs)):
            lo, hi = halves(w)
            cp(w, 2, slot(w, dg, c).at[lo], slot(w, dg, c).at[lo], xn, c).wait_recv()
            cp(w, 3, slot(w, dg, c).at[hi], slot(w, dg, c).at[hi], yn, c).wait_recv()
            passed = [cp(w, 6, slot(w, dg, c), slot(w, dg, c), own, 1 - c)]
            passed[0].start()
            sends += passed
        for w in range(len(x_refs)):
            cp(w, 4, slot(w, yn, c), slot(w, yn, 1 - c), own, 1 - c).wait_recv()
            cp(w, 5, slot(w, xn, c), slot(w, xn, 1 - c), own, 1 - c).wait_recv()
            cp(w, 6, slot(w, dg, c), slot(w, dg, 1 - c), own, 1 - c).wait_recv()
        for s in sends:
            s.wait_send()

    return list(_comm_call(body, blks, [_sds((8,) + b.shape, b.dtype) for b in blks], 7, name, False))


def _exchange_alone(ex, *, name):
    def body(x_refs, out_refs, send_sems, recv_sems, local_sems):
        start, wait = ex.bind(x_refs, out_refs, send_sems, recv_sems)
        start()
        wait()

    return list(_comm_call(body, ex.xs, ex.out_shape, 3, name, False))


def _block_rows(rows, row_bytes, target=1 << 21, align=BF16_SUBLANES):
    return _tile(rows, max(align, target // row_bytes // align * align), align)


def _sum_blocks(buf, *, name, out_dtype):
    B, R, C = buf.shape
    tm = _block_rows(R, B * C * buf.dtype.itemsize)

    def body(x_ref, o_ref):
        acc = x_ref[0].astype(F32)
        for b in range(1, B):
            acc = acc + x_ref[b].astype(F32)
        o_ref[...] = acc.astype(o_ref.dtype)

    return pl.pallas_call(
        body, name=name, grid=(R // tm,), in_specs=[pl.BlockSpec((B, tm, C), lambda i: (0, i, 0))],
        out_specs=pl.BlockSpec((tm, C), lambda i: (i, 0)), out_shape=_sds((R, C), out_dtype),
        compiler_params=_params(("parallel",)),
    )(buf)


def _pair_add(mine, theirs, core, *, name):
    _, _, R, C = mine.shape
    tm = _block_rows(R, C * 2)

    def body(core_ref, a_ref, b_ref, o_ref):
        o_ref[...] = (a_ref[...].astype(F32) + b_ref[...].astype(F32)).astype(o_ref.dtype)

    return pl.pallas_call(
        body, name=name, out_shape=_sds(theirs.shape, BF16),
        grid_spec=pltpu.PrefetchScalarGridSpec(
            num_scalar_prefetch=1, grid=(4, R // tm),
            in_specs=[pl.BlockSpec((None, None, tm, C), lambda q, i, core_ref: (q, core_ref[0], i, 0)),
                      pl.BlockSpec((None, tm, C), lambda q, i, core_ref: (q, i, 0))],
            out_specs=pl.BlockSpec((None, tm, C), lambda q, i, core_ref: (q, i, 0))),
        compiler_params=_params(("parallel", "parallel")),
    )(core, mine, theirs)


def _assemble(gathered, own, chip, *, name, transpose):
    _, K, Ns = gathered.shape
    tm = _block_rows(K, Ns * 4)

    def body(chip_ref, g_ref, own_ref, o_ref):
        q = pl.program_id(0)

        @pl.when(q == chip_ref[0])
        def _():
            o_ref[...] = own_ref[...].astype(BF16)

        @pl.when(q != chip_ref[0])
        def _():
            o_ref[...] = g_ref[...]

    if transpose:
        out_spec = pl.BlockSpec((tm, Ns), lambda q, i, ch: (i, q))
        out_shape = _sds((K, 4 * Ns), BF16)
    else:
        out_spec = pl.BlockSpec((None, tm, Ns), lambda q, i, ch: (q, i, 0))
        out_shape = _sds((4, K, Ns), BF16)
    return pl.pallas_call(
        body, name=name, out_shape=out_shape,
        grid_spec=pltpu.PrefetchScalarGridSpec(
            num_scalar_prefetch=1, grid=(4, K // tm),
            in_specs=[pl.BlockSpec((None, tm, Ns), lambda q, i, ch: (jnp.where(q == ch[0], (q + 1) % 4, q), i, 0)),
                      pl.BlockSpec((tm, Ns), lambda q, i, ch: (jnp.where(q == ch[0], i, 0), 0))],
            out_specs=out_spec),
        compiler_params=_params(("arbitrary", "arbitrary")),
    )(chip, gathered, own)


def _assemble_halves(mine, theirs, own, place, *, name, transpose):
    _, K2, Ns = mine.shape
    tm = _block_rows(K2, Ns * 4)
    nb = K2 // tm

    def body(place_ref, m_ref, t_ref, own_ref, o_ref):
        q, hb = pl.program_id(0), pl.program_id(1)
        is_own = q == place_ref[0]
        is_mine = hb == place_ref[1]

        @pl.when(is_own)
        def _():
            o_ref[...] = own_ref[...].astype(BF16)

        @pl.when(jnp.logical_not(is_own) & is_mine)
        def _():
            o_ref[...] = m_ref[...]

        @pl.when(jnp.logical_not(is_own) & jnp.logical_not(is_mine))
        def _():
            o_ref[...] = t_ref[...]

    def other(q, pr):
        return jnp.where(q == pr[0], (q + 1) % 4, q)

    if transpose:
        out_spec = pl.BlockSpec((tm, Ns), lambda q, hb, i, pr: (hb * nb + i, q))
        out_shape = _sds((2 * K2, 4 * Ns), BF16)
    else:
        out_spec = pl.BlockSpec((None, tm, Ns), lambda q, hb, i, pr: (q, hb * nb + i, 0))
        out_shape = _sds((4, 2 * K2, Ns), BF16)
    return pl.pallas_call(
        body, name=name, out_shape=out_shape,
        grid_spec=pltpu.PrefetchScalarGridSpec(
            num_scalar_prefetch=1, grid=(4, 2, nb),
            in_specs=[pl.BlockSpec((None, tm, Ns), lambda q, hb, i, pr: (other(q, pr), jnp.where(hb == pr[1], i, 0), 0)),
                      pl.BlockSpec((None, tm, Ns), lambda q, hb, i, pr: (other(q, pr), jnp.where(hb == pr[1], 0, i), 0)),
                      pl.BlockSpec((tm, Ns), lambda q, hb, i, pr: (jnp.where(q == pr[0], hb * nb + i, 0), 0))],
            out_specs=out_spec),
        compiler_params=_params(("arbitrary",) * 3),
    )(place, mine, theirs, own)


def _split_lanes(row, widths):
    out, off = [], 0
    for wd in widths:
        out.append(row[:, off:off + wd])
        off += wd
    return out


def _adamw(w, g, m, v, *, name, carry=None):
    C = w.shape[1]

    def fn(w, g, m, v):
        m = ADAM_B1 * m + (1.0 - ADAM_B1) * g
        v = ADAM_B2 * v + (1.0 - ADAM_B2) * (g * g)
        m_hat = m / (1.0 - ADAM_B1 ** ADAM_STEP)
        v_hat = v / (1.0 - ADAM_B2 ** ADAM_STEP)
        delta = -ADAM_LR * (m_hat / (jnp.sqrt(v_hat) + ADAM_EPS) + ADAM_WD * w)
        return (delta, m, v), ()

    tm = max(F32_SUBLANES, min(512, (1 << 20) // (4 * C) // F32_SUBLANES * F32_SUBLANES))
    res = _rowwise(fn, [w, g, m, v], [], [(C, F32)] * 3, name=name, tm=tm, carry=carry)
    return tuple(res[0]) + ((res[2],) if carry is not None else ())


def _rope_tables(n):
    rows = n // GRID_W
    row = jnp.repeat(jnp.arange(rows, dtype=F32), GRID_W)
    col = jnp.tile(jnp.arange(GRID_W, dtype=F32), rows)
    nf = ROPE_DIM // 4
    freqs = ROPE_THETA ** (-jnp.arange(nf, dtype=F32) / nf)
    ang_r, ang_c = row[:, None] * freqs[None, :], col[:, None] * freqs[None, :]
    cr, sr, cc, sc = jnp.cos(ang_r), jnp.sin(ang_r), jnp.cos(ang_c), jnp.sin(ang_c)
    nope = HEAD_PAD - 2 * ROPE_DIM
    one, zero, z = jnp.ones((n, nope), F32), jnp.zeros((n, nope), F32), jnp.zeros((n, nf), F32)
    pad = jnp.zeros((n, ROPE_DIM), F32)
    cos = jnp.concatenate([one, cr, cr, cc, cc, pad], axis=1)
    s_lo = jnp.concatenate([zero, -sr, z, -sc, z, pad], axis=1)
    s_hi = jnp.concatenate([zero, z, sr, z, sc, pad], axis=1)
    return cos, s_lo, s_hi


def _rope(n, cos, s_lo, s_hi):
    q = ROPE_DIM // 4
    return n * cos + pltpu.roll(n, HEAD_PAD - q, 1) * s_lo + pltpu.roll(n, q, 1) * s_hi


def _rope_t(d, cos, s_lo, s_hi):
    q = ROPE_DIM // 4
    return d * cos + pltpu.roll(d * s_lo, q, 1) + pltpu.roll(d * s_hi, HEAD_PAD - q, 1)


def kernel(x, c, ctx, c_ctx, w_mod, b_mod, norm1_g, w_in, q_norm_g, kv_norm_g, w_uq, w_ukv, qk_norm_q, qk_norm_k, sgu_norm_g, sgu_norm_b, w_spatial, b_spatial, w_br_attn, w_br_sgu, w_out, norm2_g, w_ffn_in, w_ffn_out, loss_target, m_c_ctx, m_w_mod, m_b_mod, m_norm1_g, m_w_in, m_q_norm_g, m_kv_norm_g, m_w_uq, m_w_ukv, m_qk_norm_q, m_qk_norm_k, m_sgu_norm_g, m_sgu_norm_b, m_w_spatial, m_b_spatial, m_w_br_attn, m_w_br_sgu, m_w_out, m_norm2_g, m_w_ffn_in, m_w_ffn_out, v_c_ctx, v_w_mod, v_b_mod, v_norm1_g, v_w_in, v_q_norm_g, v_kv_norm_g, v_w_uq, v_w_ukv, v_qk_norm_q, v_qk_norm_k, v_sgu_norm_g, v_sgu_norm_b, v_w_spatial, v_b_spatial, v_w_br_attn, v_w_br_sgu, v_w_out, v_norm2_g, v_w_ffn_in, v_w_ffn_out):
    ax, ay, ac = _place()
    my_chip = 2 * ax + ay
    my_dev = 4 * ax + 2 * ay + ac

    N, D = x.shape[1], x.shape[2]
    CT = ctx.shape[1]
    M = N + CT
    QL, KVL, QK = q_norm_g.shape[-1], kv_norm_g.shape[-1], qk_norm_q.shape[-1]
    NOPE = QK - ROPE_DIM
    VD = NOPE
    H = 4 * w_uq.shape[-1] // QK
    SW, G, CH = sgu_norm_g.shape[-1], w_spatial.shape[1], w_spatial.shape[2]
    GD = SW // G
    DFF = 4 * w_ffn_out.shape[1]
    NMOD = 4 * w_mod.shape[-1]
    NM = w_mod.shape[-1]
    KVP = KVL + 2 * ROPE_DIM
    assert NOPE == LANES and GD == LANES and HEAD_PAD == NOPE + 2 * ROPE_DIM and CH == LANES
    scale = QK ** -0.5

    x2, ctx2, tgt2 = x[0], ctx[0], loss_target[0]

    c_all = _all_gather8([c], name="ag_c", in_vmem=True)[0][:, 0, :]
    c_rows = jnp.concatenate([c_all, c_ctx[None, :], jnp.zeros((BF16_SUBLANES - 9, D), F32)], axis=0)

    def silu_fn(t):
        s = _sigmoid(t)
        return (t * s, s * (1.0 + t * (1.0 - s))), ()

    (silu_c, dsilu_c), _ = _rowwise(silu_fn, [c_rows], [], [(D, F32), (D, F32)], name="silu_c", tm=16)
    wm = w_mod[0]
    mod_loc = _mm([(silu_c, wm)], name="mod_fwd", outs=(F32,), tn=512, tk=512,
                  extras=[(lax.dynamic_slice_in_dim(b_mod, my_chip * NM, NM, axis=1), "n")],
                  epi=lambda acc, b: (acc + b,))
    mod_all = _all_gather8([mod_loc], name="ag_mod", in_vmem=True)[0]
    mod_full = jnp.concatenate([mod_all[0], mod_all[2], mod_all[4], mod_all[6]], axis=1)
    mod_me = lax.dynamic_slice_in_dim(mod_full, my_dev, 1, axis=0)
    sh1, sc1, g1, sh2, sc2, g2 = [mod_me[:, i * D:(i + 1) * D] for i in range(6)]
    sh1c, sc1c = mod_full[8:9, :D], mod_full[8:9, D:2 * D]

    big = [w_in[0], w_uq[0], w_ukv[0], w_br_attn[0], w_br_sgu[0], w_out[0], w_ffn_in[0], w_ffn_out[0]]
    col_sharded = [True, True, True, True, True, False, True, False]
    halves = [lax.dynamic_slice_in_dim(a, ac * (a.shape[0] // 2), a.shape[0] // 2, axis=0).astype(BF16) for a in big]
    tags = ["w_in", "w_uq", "w_ukv", "w_br_attn", "w_br_sgu", "w_out", "w_ffn_in", "w_ffn_out"]
    first_group, attn_group, ffn_group = [0, 1, 2], [3, 4, 5, 6], [7]
    chip1 = jnp.reshape(my_chip, (1,)).astype(jnp.int32)
    place2 = jnp.stack([my_chip, ac]).astype(jnp.int32)

    def laid_out(seg, i):
        a = big[i]
        if col_sharded[i] and seg.ndim == 3:
            return seg.transpose(1, 0, 2).reshape(a.shape[0], 4 * a.shape[1])
        return seg if col_sharded[i] else seg.reshape(4 * a.shape[0], a.shape[1])

    def side_by_side(i):
        return col_sharded[i] and big[i].shape[1] % LANES == 0

    def finish_gather(idx, mine4, theirs4):
        return [laid_out(_assemble_halves(m, t, big[i], place2, name="assemble_" + tags[i], transpose=side_by_side(i)), i)
                for i, m, t in zip(idx, mine4, theirs4)]

    gathered = _gather_others([halves[i] for i in first_group], name="ag_weights")
    w_in_f, w_uq_f, w_ukv_f = [
        laid_out(_assemble(seg.reshape((4,) + big[i].shape), big[i], chip1, name="assemble_" + tags[i],
                           transpose=side_by_side(i)), i) for i, seg in zip(first_group, gathered)]
    o_kv, o_u = QL, QL + KVL + ROPE_DIM
    o_v, o_g = o_u + SW, o_u + 2 * SW
    w_q = w_in_f[:, :QL]
    w_kv = jnp.pad(w_in_f[:, o_kv:o_u], ((0, 0), (0, ROPE_DIM)))
    w_u, w_v = w_in_f[:, o_u:o_v], w_in_f[:, o_v:o_g]
    w_g1, w_g2 = w_in_f[:, o_g:o_g + D], w_in_f[:, o_g + D:]
    w_uq_p = jnp.pad(w_uq_f.reshape(QL, H, QK), ((0, 0), (0, 0), (0, HEAD_PAD - QK))).reshape(QL, H * HEAD_PAD)

    cos_t, slo_t, shi_t = _rope_tables(N)
    ones_c = jnp.concatenate([jnp.ones((CT, NOPE + ROPE_DIM), F32), jnp.zeros((CT, ROPE_DIM), F32)], axis=1)
    cos_k = jnp.concatenate([cos_t, ones_c], axis=0)
    slo_k = jnp.concatenate([slo_t, jnp.zeros((CT, HEAD_PAD), F32)], axis=0)
    shi_k = jnp.concatenate([shi_t, jnp.zeros((CT, HEAD_PAD), F32)], axis=0)
    gq_p = jnp.pad(qk_norm_q, ((0, 0), (0, HEAD_PAD - QK)))
    gk_p = jnp.pad(qk_norm_k, ((0, 0), (0, HEAD_PAD - QK)))

    def norm_mod_fn(t, g, sh, sc):
        r = _rms_stats(t, D)
        return (((t * r) * g) * (1.0 + sc) + sh,), ()

    (h,), _ = _rowwise(norm_mod_fn, [x2], [norm1_g, sh1, sc1], [(D, BF16)], name="norm1_x")
    (ctx_h,), _ = _rowwise(norm_mod_fn, [ctx2], [norm1_g, sh1c, sc1c], [(D, BF16)], name="norm1_ctx")

    qc = _mm([(h, w_q)], name="proj_q", outs=(F32,))
    kvin = jnp.concatenate([_mm([(h, w_kv)], name="proj_kv", outs=(F32,)),
                            _mm([(ctx_h, w_kv)], name="proj_kv_ctx", outs=(F32,))], axis=0)
    u_in = _mm([(h, w_u)], name="proj_u", outs=(BF16,))
    v_in = _mm([(h, w_v)], name="proj_v", outs=(BF16,))
    g1_in, (mine_bra,) = _mm([(h, w_g1)], name="proj_g1", outs=(BF16,), carry=_ChipExchange([halves[3]], gather=True))
    g2_in, (mine_brs,) = _mm([(h, w_g2)], name="proj_g2", outs=(BF16,), carry=_ChipExchange([halves[4]], gather=True))

    def rms_gain_fn(width):
        def fn(t, g):
            return (((t * _rms_stats(t, width)) * g),), ()
        return fn

    (qn,), _ = _rowwise(rms_gain_fn(QL), [qc], [q_norm_g], [(QL, BF16)], name="q_norm")

    def kv_norm_fn(t, g):
        kvc = t[:, :KVL]
        return (((kvc * _rms_stats(kvc, KVL)) * g),), ()

    (kvn,), _ = _rowwise(kv_norm_fn, [kvin], [kv_norm_g], [(KVL, BF16)], name="kv_norm")
    q_raw = _mm([(qn, w_uq_p)], name="q_up", outs=(F32,))
    kv_raw = _mm([(kvn, w_ukv_f)], name="kv_up", outs=(F32,))

    def q_post_fn(t, cos, slo, shi, g):
        outs = []
        for hd in range(H):
            th = t[:, hd * HEAD_PAD:(hd + 1) * HEAD_PAD]
            outs.append(_rope((th * _rms_stats(th, QK)) * g, cos, slo, shi) * scale)
        return (jnp.concatenate(outs, axis=1),), ()

    (q_att,), _ = _rowwise(q_post_fn, [q_raw, cos_t, slo_t, shi_t], [gq_p], [(H * HEAD_PAD, BF16)], name="q_post")

    def k_post_fn(t, kvi, cos, slo, shi, g):
        kr = kvi[:, KVL:]
        ks, vs = [], []
        for hd in range(H):
            th = jnp.concatenate([t[:, hd * HEAD_PAD:hd * HEAD_PAD + NOPE], kr], axis=1)
            ks.append(_rope((th * _rms_stats(th, QK)) * g, cos, slo, shi))
            vs.append(t[:, hd * HEAD_PAD + NOPE:(hd + 1) * HEAD_PAD])
        return (jnp.concatenate(ks, axis=1), jnp.concatenate(vs, axis=1)), ()

    (k_att, v_att), _, (mine_out,) = _rowwise(k_post_fn, [kv_raw, kvin, cos_k, slo_k, shi_k], [gk_p],
                                              [(H * HEAD_PAD, BF16), (H * VD, BF16)], name="k_post",
                                              carry=_ChipExchange([halves[5]], gather=True))
    attn_o, lse, (mine_ffi,) = _attn_fwd(q_att, k_att, v_att, heads=H, carry=_RelayGather([halves[6]]))
    mine4 = [mine_bra, mine_brs, mine_out, mine_ffi]

    ws3 = w_spatial[0]
    bs_t = jnp.pad(b_spatial[0].T, ((0, 0), (0, LANES - G)))

    def sgu_parts(u_in, v_in, ng, nb):
        u, v = _gelu(u_in.astype(F32)), _gelu(v_in.astype(F32))
        mu = jnp.mean(v, axis=-1, keepdims=True)
        vc = v - mu
        rs = lax.rsqrt(jnp.mean(vc * vc, axis=-1, keepdims=True) + EPS)
        xhat = vc * rs
        return u, xhat, rs, (xhat * ng + nb).astype(BF16)

    def sgu_fwd_fn(u_in, v_in, ng, nb, ws, bst):
        u, _, _, vnb = sgu_parts(u_in, v_in, ng, nb)
        outs = []
        for g in range(G):
            sl = slice(g * GD, (g + 1) * GD)
            mixed = jnp.dot(ws[g].astype(BF16), vnb[:, sl], preferred_element_type=F32) + bst[:, g:g + 1]
            outs.append(u[:, sl] * mixed)
        return (jnp.concatenate(outs, axis=1),), ()

    (sgu_o,), _, theirs4 = _rowwise(sgu_fwd_fn, [u_in, v_in], [sgu_norm_g, sgu_norm_b, ws3, bs_t], [(SW, BF16)],
                                    name="sgu_fwd", tm=CH, carry=_PairExchange(mine4, "forward"))
    w_bra, w_brs, w_out_f, w_ffi = finish_gather(attn_group, mine4, theirs4)
    w_fa, w_fb = w_ffi[:, :DFF], w_ffi[:, DFF:]

    a1 = _mm([(attn_o, w_bra)], name="br_attn", outs=(BF16,))
    def merge_epi(acc, a1v, gi1, gi2):
        return acc, _sigmoid(gi1.astype(F32)) * a1v.astype(F32) + _sigmoid(gi2.astype(F32)) * acc

    a2, merged = _mm([(sgu_o, w_brs)], name="br_sgu", outs=(BF16, BF16),
                     extras=[(a1, "mn"), (g1_in, "mn"), (g2_in, "mn")], epi=merge_epi)

    def res_gate(acc, res, gate):
        return res + gate * acc, acc

    x1, mo = _mm([(merged, w_out_f)], name="out_proj", outs=(F32, BF16), tn=1024,
                 extras=[(x2, "mn"), (g1, "n")], epi=res_gate)
    (h2,), _ = _rowwise(norm_mod_fn, [x1], [norm2_g, sh2, sc2], [(D, BF16)], name="norm2")

    def swiglu_epi(a, b):
        return a, b, (a * _sigmoid(a)) * b

    (fa, fb, act), mine4 = _mm([(h2, w_fa, w_fb)], name="ffn_in", outs=(BF16, BF16, BF16), tn=512, epi=swiglu_epi,
                               carry=_ChipExchange([halves[i] for i in ffn_group], gather=True))
    (w_ffo,) = finish_gather(ffn_group, mine4, _exchange_alone(_PairExchange(mine4, "forward"), name="ag_forward_ffn"))
    def loss_epi(acc, res, t, gate):
        e = (res + gate * acc) - t
        dy = e * (1.0 / D)
        return dy, gate * dy, _colsum(e * e) * (0.5 / D), _colsum(dy * acc)

    dy, df, loss_part, dg2_part = _mm([(act, w_ffo)], name="ffn_out", outs=(F32, BF16), tn=1024, col_sums=2,
                                      extras=[(x1, "mn"), (tgt2, "mn"), (g2, "n")], epi=loss_epi)

    def fold_fn(a, b):
        return (), (_colsum(a), _colsum(b))

    _, (loss_cols, dg2) = _rowwise(fold_fn, [loss_part[:, 0, :], dg2_part[:, 0, :]], [], [], [(1, D), (1, D)],
                                   name="loss_fold", tm=loss_part.shape[0])

    def swiglu_bwd_epi(dact, a, b):
        a, b = a.astype(F32), b.astype(F32)
        s = _sigmoid(a)
        return dact * b * (s * (1.0 + a * (1.0 - s))), dact * (a * s)

    da, db = _mm([(df, w_ffo)], tb=True, name="ffn_out_dx", outs=(BF16, BF16), tn=512,
                 extras=[(fa, "mn"), (fb, "mn")], epi=swiglu_bwd_epi)
    dw_ffo = _mm([(act, df)], ta=True, name="ffn_out_dw", outs=(BF16,))
    dh2 = _mm([(da, w_fa), (db, w_fb)], tb=True, name="ffn_in_dx", outs=(F32,))
    ns_ffi = w_ffn_in.shape[-1]
    dw_ffi = _mm([(h2, da)], ta=True, name="ffn_in_dw_a", outs=(BF16,), tn=1408, split=ns_ffi,
                 into=(lax.empty((4, D, ns_ffi), BF16), 0))
    dw_ffi = _mm([(h2, db)], ta=True, name="ffn_in_dw_b", outs=(BF16,), tn=1408, split=ns_ffi, into=(dw_ffi, 2))

    def norm2_bwd_fn(dh, t, dyv, mov, g, sc, g1v):
        r = _rms_stats(t, D)
        tn = t * r
        dxg = dh * (1.0 + sc)
        dt = dyv + _rms_bwd(dxg * g, tn, r, D)
        return (dt, g1v * dt), (_colsum(dh), _colsum(dh * (tn * g)), _colsum(dxg * tn), _colsum(dt * mov.astype(F32)))

    (dx1, dmo), (dsh2, dsc2, dn2g, dg1) = _rowwise(
        norm2_bwd_fn, [dh2, x1, dy, mo], [norm2_g, sc2, g1], [(D, F32), (D, BF16)], [(1, D)] * 4, name="norm2_bwd")

    def merge_bwd_epi(dm, a1, a2, gi1, gi2):
        s1, s2 = _sigmoid(gi1.astype(F32)), _sigmoid(gi2.astype(F32))
        a1, a2 = a1.astype(F32), a2.astype(F32)
        return dm * s1, dm * s2, dm * a1 * (s1 * (1.0 - s1)), dm * a2 * (s2 * (1.0 - s2))

    da1, da2, dgi1, dgi2 = _mm([(dmo, w_out_f)], tb=True, name="out_proj_dx", outs=(BF16,) * 4, tn=512,
                               extras=[(a1, "mn"), (a2, "mn"), (g1_in, "mn"), (g2_in, "mn")], epi=merge_bwd_epi)
    dw_out = _mm([(merged, dmo)], ta=True, name="out_proj_dw", outs=(BF16,))
    dattn = _mm([(da1, w_bra)], tb=True, name="br_attn_dx", outs=(BF16,))
    dw_bra = _mm([(attn_o, da1)], ta=True, name="br_attn_dw", outs=(BF16,), split=w_br_attn.shape[-1])
    dsgu = _mm([(da2, w_brs)], tb=True, name="br_sgu_dx", outs=(BF16,))
    dw_brs = _mm([(sgu_o, da2)], ta=True, name="br_sgu_dw", outs=(BF16,), split=w_br_sgu.shape[-1])

    def sgu_bwd_fn(dso, u_in, v_in, ng, nb, ws, bst):
        u, xhat, rs, vnb = sgu_parts(u_in, v_in, ng, nb)
        dso = dso.astype(F32)
        lane = lax.broadcasted_iota(jnp.int32, (CH, LANES), 1)
        du, dvn, dws, dbs = [], [], [], jnp.zeros((CH, LANES), F32)
        for g in range(G):
            sl = slice(g * GD, (g + 1) * GD)
            wg = ws[g].astype(BF16)
            mixed = jnp.dot(wg, vnb[:, sl], preferred_element_type=F32) + bst[:, g:g + 1]
            du.append(dso[:, sl] * mixed)
            dmix = dso[:, sl] * u[:, sl]
            dmb = dmix.astype(BF16)
            dws.append(lax.dot_general(dmb, vnb[:, sl], (((1,), (1,)), ((), ())), preferred_element_type=F32))
            dbs = dbs + jnp.where(lane == g, jnp.sum(dmix, axis=1, keepdims=True), 0.0)
            dvn.append(lax.dot_general(wg, dmb, (((0,), (0,)), ((), ())), preferred_element_type=F32))
        du, dvn = jnp.concatenate(du, axis=1), jnp.concatenate(dvn, axis=1)
        dxh = dvn * ng
        dv = rs * (dxh - jnp.mean(dxh, axis=-1, keepdims=True) - xhat * jnp.mean(dxh * xhat, axis=-1, keepdims=True))
        return ((du * _gelu_grad(u_in.astype(F32)), dv * _gelu_grad(v_in.astype(F32))),
                (_colsum(dvn * xhat), _colsum(dvn), jnp.stack(dws), dbs))

    core = jnp.reshape(ac, (1,)).astype(jnp.int32)

    def dest_layout(dwf, i):
        K, Ns = big[i].shape
        if dwf.ndim == 2:
            dwf = dwf.reshape(K, 4, Ns).transpose(1, 0, 2) if col_sharded[i] else dwf.reshape(4, K, Ns)
        return dwf.reshape(4, 2, K // 2, Ns)

    def pair_sums(idx, g4, sib):
        return [_pair_add(g, s, core, name="rs_pair_add_" + tags[i]) for g, s, i in zip(g4, sib, idx)]

    early = [3, 4, 5, 6, 7]
    g4_early = [dest_layout(d, i) for d, i in zip([dw_bra, dw_brs, dw_out, dw_ffi, dw_ffo], early)]
    (du_in, dv_in), (d_sng, d_snb, d_ws, d_bs), sib_early = _rowwise(
        sgu_bwd_fn, [dsgu, u_in, v_in], [sgu_norm_g, sgu_norm_b, ws3, bs_t], [(SW, BF16), (SW, BF16)],
        [(1, SW), (1, SW), (G, CH, CH), (CH, LANES)], name="sgu_bwd", tm=CH, carry=_PairExchange(g4_early, "halves"))
    pair_early = pair_sums(early, g4_early, sib_early)
    dq_att, dk_att, dv_att, xchg_early = _attn_bwd(q_att, k_att, v_att, attn_o, lse, dattn, heads=H,
                                                   carry=_ChipExchange(pair_early, gather=False))

    def q_post_bwd_fn(dq, t, cos, slo, shi, g):
        outs, dg = [], jnp.zeros((1, HEAD_PAD), F32)
        for hd in range(H):
            sl = slice(hd * HEAD_PAD, (hd + 1) * HEAD_PAD)
            th = t[:, sl]
            r = _rms_stats(th, QK)
            tn = th * r
            dn = _rope_t(dq[:, sl] * scale, cos, slo, shi)
            dg = dg + _colsum(dn * tn)
            outs.append(_rms_bwd(dn * g, tn, r, QK))
        return (jnp.concatenate(outs, axis=1),), (dg,)

    (dq_raw,), (d_gq,) = _rowwise(q_post_bwd_fn, [dq_att, q_raw, cos_t, slo_t, shi_t], [gq_p],
                                  [(H * HEAD_PAD, BF16)], [(1, HEAD_PAD)], name="q_post_bwd")

    def k_post_bwd_fn(dk, dv, t, kvi, cos, slo, shi, g):
        kr = kvi[:, KVL:]
        outs, dg, dkr = [], jnp.zeros((1, HEAD_PAD), F32), jnp.zeros_like(kr)
        for hd in range(H):
            th = jnp.concatenate([t[:, hd * HEAD_PAD:hd * HEAD_PAD + NOPE], kr], axis=1)
            r = _rms_stats(th, QK)
            tn = th * r
            dn = _rope_t(dk[:, hd * HEAD_PAD:(hd + 1) * HEAD_PAD], cos, slo, shi)
            dg = dg + _colsum(dn * tn)
            dt = _rms_bwd(dn * g, tn, r, QK)
            dkr = dkr + dt[:, NOPE:]
            outs += [dt[:, :NOPE], dv[:, hd * VD:(hd + 1) * VD]]
        return (jnp.concatenate(outs, axis=1), dkr), (dg,)

    def reduced_halves(idx, xchg, pair):
        filled = [lax.dynamic_update_index_in_dim(t4, lax.dynamic_index_in_dim(pr, my_chip, 0, keepdims=False),
                                                  my_chip, 0) for t4, pr in zip(xchg, pair)]
        return [_sum_blocks(t4, name="rs_sum_" + tags[i], out_dtype=F32) for t4, i in zip(filled, idx)]

    red_early = reduced_halves(early, xchg_early, pair_early)
    (dkv_raw, dkrope), (d_gk,), other_early = _rowwise(
        k_post_bwd_fn, [dk_att, dv_att, kv_raw, kvin, cos_k, slo_k, shi_k], [gk_p],
        [(H * HEAD_PAD, BF16), (2 * ROPE_DIM, F32)], [(1, HEAD_PAD)], name="k_post_bwd",
        carry=_PairExchange(red_early, "gather"))

    dqn = _mm([(dq_raw, w_uq_p)], tb=True, name="q_up_dx", outs=(F32,))
    dw_uq_p = _mm([(qn, dq_raw)], ta=True, name="q_up_dw", outs=(BF16,))
    dkvn = _mm([(dkv_raw, w_ukv_f)], tb=True, name="kv_up_dx", outs=(F32,))
    dw_ukv = _mm([(kvn, dkv_raw)], ta=True, name="kv_up_dw", outs=(BF16,))

    def q_norm_bwd_fn(dn, t, g):
        r = _rms_stats(t, QL)
        tn = t * r
        return (_rms_bwd(dn * g, tn, r, QL),), (_colsum(dn * tn),)

    (dqc,), (d_qng,) = _rowwise(q_norm_bwd_fn, [dqn, qc], [q_norm_g], [(QL, BF16)], [(1, QL)], name="q_norm_bwd")

    def kv_norm_bwd_fn(dn, dkr, t, g):
        kvc = t[:, :KVL]
        r = _rms_stats(kvc, KVL)
        tn = kvc * r
        return (jnp.concatenate([_rms_bwd(dn * g, tn, r, KVL), dkr], axis=1),), (_colsum(dn * tn),)

    (dkvin,), (d_kvng,) = _rowwise(kv_norm_bwd_fn, [dkvn, dkrope, kvin], [kv_norm_g], [(KVP, BF16)], [(1, KVL)],
                                   name="kv_norm_bwd")
    dkvin_x, dkvin_c = dkvin[:N], dkvin[N:]

    dctx_h = _mm([(dkvin_c, w_kv)], tb=True, name="proj_kv_ctx_dx", outs=(F32,))
    dw_q = _mm([(h, dqc)], ta=True, name="proj_q_dw", outs=(BF16,))
    dw_kv = _mm([(h, dkvin_x), (ctx_h, dkvin_c)], ta=True, name="proj_kv_dw", outs=(BF16,))
    dw_u = _mm([(h, du_in)], ta=True, name="proj_u_dw", outs=(BF16,))
    dw_v = _mm([(h, dv_in)], ta=True, name="proj_v_dw", outs=(BF16,))
    dw_g1 = _mm([(h, dgi1)], ta=True, name="proj_g1_dw", outs=(BF16,))
    dw_g2 = _mm([(h, dgi2)], ta=True, name="proj_g2_dw", outs=(BF16,))

    dw_in_f = jnp.concatenate([dw_q, dw_kv[:, :KVL + ROPE_DIM], dw_u, dw_v, dw_g1, dw_g2], axis=1)
    dw_uq_f = dw_uq_p.reshape(QL, H, HEAD_PAD)[:, :, :QK].reshape(QL, H * QK)
    late = [0, 1, 2]
    g4_late = [dest_layout(d, i) for d, i in zip([dw_in_f, dw_uq_f, dw_ukv], late)]
    pair_late = pair_sums(late, g4_late, _exchange_alone(_PairExchange(g4_late, "halves"), name="rs_pair_late"))
    dh, xchg_late = _mm([(dqc, w_q), (dkvin_x, w_kv), (du_in, w_u), (dv_in, w_v), (dgi1, w_g1), (dgi2, w_g2)],
                        tb=True, name="proj_dx", outs=(F32,), tn=1024, tk=512,
                        carry=_ChipExchange(pair_late, gather=False))

    def norm1_bwd_fn(dhv, t, dres, g, sc):
        r = _rms_stats(t, D)
        tn = t * r
        dxg = dhv * (1.0 + sc)
        return (dres + _rms_bwd(dxg * g, tn, r, D),), (_colsum(dhv), _colsum(dhv * (tn * g)), _colsum(dxg * tn))

    (grad_x,), (dsh1, dsc1, dn1g_x) = _rowwise(norm1_bwd_fn, [dh, x2, dx1], [norm1_g, sc1], [(D, F32)], [(1, D)] * 3,
                                               name="norm1_bwd")
    _, (dsh1c, dsc1c, dn1g_c) = _rowwise(norm1_bwd_fn, [dctx_h, ctx2, jnp.zeros_like(ctx2)], [norm1_g, sc1c],
                                         [(D, F32)], [(1, D)] * 3, name="norm1_ctx_bwd")

    small = [dsh1, dsc1, dg1, dsh2, dsc2, dg2,
             dsh1c, dsc1c, dn1g_x, dn1g_c, d_qng, d_kvng, d_gq, d_gk, d_sng, d_snb, dn2g, loss_cols]
    small_sizes = [a.shape[1] for a in small]
    sm_row = jnp.concatenate(small, axis=1)
    sm_mat = jnp.concatenate([d_ws.reshape(G * CH, CH), d_bs], axis=0)
    row_all, mat_all = _all_gather8([sm_row, sm_mat], name="ag_small", in_vmem=True)
    row_sum = _sum_blocks(row_all, name="sum_small_rows", out_dtype=F32)
    mat_sum = _sum_blocks(mat_all, name="sum_small_mats", out_dtype=F32)
    dmod_rows = row_all[:, 0, :NMOD]
    (_, _, _, _, _, _, t_sh1c, t_sc1c, t_n1x, t_n1c, g_qng, g_kvng, t_gq, t_gk, g_sng, g_snb, g_n2g,
     t_loss) = _split_lanes(row_sum, small_sizes)
    g_ws, t_bs = mat_sum[:G * CH], mat_sum[G * CH:]
    dmodc_row = jnp.concatenate([t_sh1c, t_sc1c, jnp.zeros((1, NMOD - 2 * D), F32)], axis=1)
    dmod16 = jnp.concatenate([dmod_rows, dmodc_row, jnp.zeros((BF16_SUBLANES - 9, NMOD), F32)], axis=0)

    def small_fn(rows, n1x, n1c, lossv):
        return (), (_colsum(rows), n1x + n1c, jnp.sum(lossv, axis=1, keepdims=True))

    _, (g_bmod, g_n1g, loss11) = _rowwise(small_fn, [dmod16], [t_n1x, t_n1c, t_loss], [], [(1, NMOD), (1, D), (1, 1)],
                                          name="small_reduce", tm=16)
    dmod_loc = lax.dynamic_slice_in_dim(dmod16, my_chip * NM, NM, axis=1)
    g_wmod = _mm([(silu_c, dmod_loc)], ta=True, name="mod_dw", outs=(F32,), tn=512)
    dsilu_part = _mm([(dmod_loc, wm)], tb=True, name="mod_dx", outs=(F32,), tk=512)
    part_all = _all_gather8([dsilu_part[8:9]], name="ag_cctx", in_vmem=True)[0]

    def cctx_fn(parts, dsl):
        return (), ((parts[0:1] + parts[2:3] + parts[4:5] + parts[6:7]) * dsl,)

    _, (g_cctx,) = _rowwise(cctx_fn, [part_all[:, 0, :]], [dsilu_c[8:9]], [], [(1, D)], name="cctx_grad", tm=8)

    red_late = reduced_halves(late, xchg_late, pair_late)
    other_late = _exchange_alone(_PairExchange(red_late, "gather"), name="rs_halves_late")
    big_grads = [lax.dynamic_update_index_in_dim(r, mine, ac, 0).reshape(a.shape)
                 for r, mine, a in zip(other_late + other_early, red_late + red_early, big)]
    mod_upd = _adamw(w_mod[0], g_wmod, m_w_mod[0], v_w_mod[0], name="adamw_w_mod")

    def upd(w, g, m, v, nm):
        shape = w.shape
        w2, g2_, m2, v2 = [t.reshape(-1, shape[-1]) for t in (w, g, m, v)]
        d_, m_, v_ = _adamw(w2, g2_, m2, v2, name="adamw_" + nm)
        return g.reshape(shape), d_.reshape(shape), m_.reshape(shape), v_.reshape(shape)

    g_in, g_uq, g_ukv, g_bra, g_brs, g_out, g_ffi, g_ffo = big_grads
    grads = dict(
        c_ctx=g_cctx.reshape(D), w_mod=g_wmod[None], b_mod=g_bmod, norm1_g=g_n1g, w_in=g_in[None],
        q_norm_g=g_qng, kv_norm_g=g_kvng, w_uq=g_uq[None], w_ukv=g_ukv[None],
        qk_norm_q=t_gq[:, :QK], qk_norm_k=t_gk[:, :QK], sgu_norm_g=g_sng, sgu_norm_b=g_snb,
        w_spatial=g_ws.reshape(w_spatial.shape), b_spatial=t_bs[:, :G].T[None],
        w_br_attn=g_bra[None], w_br_sgu=g_brs[None], w_out=g_out[None], norm2_g=g_n2g,
        w_ffn_in=g_ffi[None], w_ffn_out=g_ffo[None])
    weights = dict(c_ctx=c_ctx, w_mod=w_mod, b_mod=b_mod, norm1_g=norm1_g, w_in=w_in, q_norm_g=q_norm_g,
                   kv_norm_g=kv_norm_g, w_uq=w_uq, w_ukv=w_ukv, qk_norm_q=qk_norm_q, qk_norm_k=qk_norm_k,
                   sgu_norm_g=sgu_norm_g, sgu_norm_b=sgu_norm_b, w_spatial=w_spatial, b_spatial=b_spatial,
                   w_br_attn=w_br_attn, w_br_sgu=w_br_sgu, w_out=w_out, norm2_g=norm2_g, w_ffn_in=w_ffn_in,
                   w_ffn_out=w_ffn_out)
    m_in = dict(c_ctx=m_c_ctx, w_mod=m_w_mod, b_mod=m_b_mod, norm1_g=m_norm1_g, w_in=m_w_in, q_norm_g=m_q_norm_g,
                kv_norm_g=m_kv_norm_g, w_uq=m_w_uq, w_ukv=m_w_ukv, qk_norm_q=m_qk_norm_q, qk_norm_k=m_qk_norm_k,
                sgu_norm_g=m_sgu_norm_g, sgu_norm_b=m_sgu_norm_b, w_spatial=m_w_spatial, b_spatial=m_b_spatial,
                w_br_attn=m_w_br_attn, w_br_sgu=m_w_br_sgu, w_out=m_w_out, norm2_g=m_norm2_g, w_ffn_in=m_w_ffn_in,
                w_ffn_out=m_w_ffn_out)
    v_in_ = dict(c_ctx=v_c_ctx, w_mod=v_w_mod, b_mod=v_b_mod, norm1_g=v_norm1_g, w_in=v_w_in, q_norm_g=v_q_norm_g,
                 kv_norm_g=v_kv_norm_g, w_uq=v_w_uq, w_ukv=v_w_ukv, qk_norm_q=v_qk_norm_q, qk_norm_k=v_qk_norm_k,
                 sgu_norm_g=v_sgu_norm_g, sgu_norm_b=v_sgu_norm_b, w_spatial=v_w_spatial, b_spatial=v_b_spatial,
                 w_br_attn=v_w_br_attn, w_br_sgu=v_w_br_sgu, w_out=v_w_out, norm2_g=v_norm2_g, w_ffn_in=v_w_ffn_in,
                 w_ffn_out=v_w_ffn_out)
    names = list(weights)
    big_names = ("w_mod", "w_in", "w_uq", "w_ukv", "w_br_attn", "w_br_sgu", "w_out", "w_ffn_in", "w_ffn_out")
    out_g, out_d, out_m, out_v = {}, {}, {}, {}
    out_g["w_mod"] = grads["w_mod"]
    out_d["w_mod"], out_m["w_mod"], out_v["w_mod"] = [t[None] for t in mod_upd[:3]]
    for nm in big_names[1:]:
        out_g[nm], out_d[nm], out_m[nm], out_v[nm] = upd(weights[nm], grads[nm], m_in[nm], v_in_[nm], nm)
    row_names = [nm for nm in names if nm not in big_names and nm not in ("w_spatial", "b_spatial")]
    widths = [-(-weights[nm].size // LANES) * LANES for nm in row_names]

    def as_row(d):
        return jnp.concatenate([jnp.pad(d[nm].reshape(1, -1), ((0, 0), (0, wd - d[nm].size)))
                                for nm, wd in zip(row_names, widths)], axis=1)

    def as_mat(d):
        return jnp.concatenate([d["w_spatial"].reshape(G * CH, CH), d["b_spatial"].reshape(G, CH)], axis=0)

    row_res = _adamw(as_row(weights), as_row(grads), as_row(m_in), as_row(v_in_), name="adamw_rows")
    mat_res = _adamw(as_mat(weights), as_mat(grads), as_mat(m_in), as_mat(v_in_), name="adamw_spatial")
    for tgt, row, mat in zip((out_d, out_m, out_v), row_res, mat_res):
        for nm, seg in zip(row_names, _split_lanes(row, widths)):
            tgt[nm] = seg[:, :weights[nm].size].reshape(weights[nm].shape)
        tgt["w_spatial"] = mat[:G * CH].reshape(w_spatial.shape)
        tgt["b_spatial"] = mat[G * CH:].reshape(b_spatial.shape)
    for nm in row_names + ["w_spatial", "b_spatial"]:
        out_g[nm] = grads[nm].reshape(weights[nm].shape)

    loss = loss11.reshape(())
    return (loss, grad_x[None], *[out_g[n] for n in names], *[out_d[n] for n in names],
            *[out_m[n] for n in names], *[out_v[n] for n in names])
```

```python
import math

import jax
import jax.numpy as jnp
from jax import lax
from jax.experimental import pallas as pl
from jax.experimental.pallas import tpu as pltpu

F32, BF16 = jnp.float32, jnp.bfloat16
MESH = pl.DeviceIdType.MESH

LANES = 128
F32_SUBLANES = 8
BF16_SUBLANES = 16
MXU_DIM = 256
VMEM_LIMIT_BYTES = 56 * 1024 * 1024

EPS = 1e-6
ROPE_DIM = 64
ROPE_THETA = 10000.0
GRID_W = 64
HEAD_PAD = 256
ADAM_LR, ADAM_B1, ADAM_B2, ADAM_EPS, ADAM_WD, ADAM_STEP = 0.001, 0.9, 0.999, 1e-08, 0.01, 10


def _tile(dim, pref, align=LANES):
    if dim <= pref:
        return dim
    t = (pref // align) * align
    while t >= align:
        if dim % t == 0:
            return t
        t -= align
    return dim


def _params(sem=None):
    return pltpu.CompilerParams(dimension_semantics=sem, vmem_limit_bytes=VMEM_LIMIT_BYTES)


def _sds(shape, dtype):
    return jax.ShapeDtypeStruct(tuple(shape), dtype)


def _mm(pairs, *, name, ta=False, tb=False, outs=(F32,), tm=1024, tn=1024, tk=2048, extras=(), epi=None,
        split=None, into=None, carry=None, col_sums=0):
    dual = len(pairs[0]) == 3
    a0, b0 = pairs[0][0], pairs[0][1]
    M = a0.shape[1] if ta else a0.shape[0]
    N = b0.shape[0] if tb else b0.shape[1]
    tm, tn = _tile(M, tm), _tile(N if split is None else split, tn)
    ks = [(p[0].shape[0] if ta else p[0].shape[1]) for p in pairs]
    tks = [_tile(k, tk) for k in ks]
    nks = [k // t for k, t in zip(ks, tks)]
    offs = [sum(nks[:i]) for i in range(len(pairs))]
    nk_total = sum(nks)
    single = len(pairs) == 1

    def kidx(kk, p):
        return kk if single else jnp.clip(kk - offs[p], 0, nks[p] - 1)

    in_specs, operands = [], []
    for p, pr in enumerate(pairs):
        if ta:
            in_specs.append(pl.BlockSpec((tks[p], tm), lambda i, j, kk, p=p: (kidx(kk, p), i)))
        else:
            in_specs.append(pl.BlockSpec((tm, tks[p]), lambda i, j, kk, p=p: (i, kidx(kk, p))))
        operands.append(pr[0])
        for b in pr[1:]:
            if tb:
                in_specs.append(pl.BlockSpec((tn, tks[p]), lambda i, j, kk, p=p: (j, kidx(kk, p))))
            else:
                in_specs.append(pl.BlockSpec((tks[p], tn), lambda i, j, kk, p=p: (kidx(kk, p), j)))
            operands.append(b)
    for arr, kind in extras:
        if kind == "mn":
            in_specs.append(pl.BlockSpec((tm, tn), lambda i, j, kk: (i, j)))
        else:
            in_specs.append(pl.BlockSpec((1, tn), lambda i, j, kk: (0, j)))
        operands.append(arr)
    n_in = len(operands)
    n_ex = len(extras)
    per = 3 if dual else 2
    dims = (((0 if ta else 1,), (1 if tb else 0,)), ((), ()))

    n_acc = 2 if dual else 1

    def products(ins, p):
        a = ins[per * p][...].astype(BF16)
        return [lax.dot_general(a, ins[per * p + 1 + q][...].astype(BF16), dims, preferred_element_type=F32)
                for q in range(n_acc)]

    def finish(ins, out_refs, acc_vals):
        vals = acc_vals + [r[...] for r in ins[n_in - n_ex:]]
        res = epi(*vals) if epi is not None else (vals[0],)
        for o, r in zip(out_refs, res):
            o[...] = jnp.broadcast_to(r, o.shape).astype(o.dtype)

    out_specs = [pl.BlockSpec((tm, tn), lambda i, j, kk: (i, j)) for _ in outs]
    out_specs += [pl.BlockSpec((None, F32_SUBLANES, tn), lambda i, j, kk: (i, 0, j)) for _ in range(col_sums)]
    out_shape = [_sds((M, N), d) for d in outs] + [_sds((M // tm, F32_SUBLANES, N), F32) for _ in range(col_sums)]
    aliases = {}
    n_alias = 0
    if split is not None:
        nps = split // tn
        lead = 0 if into is None else into[1]
        out_specs = [pl.BlockSpec((None, tm, tn), lambda i, j, kk: (j // nps + lead, i, j % nps))]
        out_shape = [_sds((N // split if into is None else into[0].shape[0], M, split), outs[0])]
        if into is not None:
            in_specs.append(pl.BlockSpec(memory_space=pl.ANY))
            operands.append(into[0])
            aliases, n_alias = {n_in: 0}, 1

    grid = (M // tm, N // tn, nk_total)
    n_out = len(outs) + col_sums

    def at_step(first):
        ids = [pl.program_id(d) for d in range(3)]
        cond = None
        for d, g in zip(ids, grid):
            t = d == (0 if first else g - 1)
            cond = t if cond is None else cond & t
        return cond

    def body(*refs):
        ins, out_refs, accs, start, wait = _split_refs(refs, n_in + n_alias, n_out, carry)
        ins = ins[:n_in]
        if carry is not None:
            pl.when(at_step(True))(start)
        if nk_total == 1:
            finish(ins, out_refs, products(ins, 0))
        else:
            kk = pl.program_id(2)

            @pl.when(kk == 0)
            def _():
                for acc, v in zip(accs, products(ins, 0)):
                    acc[...] = v

            for p in range(len(pairs)):
                lo = max(offs[p], 1)

                @pl.when((kk >= lo) & (kk < offs[p] + nks[p]))
                def _(p=p):
                    for acc, v in zip(accs, products(ins, p)):
                        acc[...] += v

            @pl.when(kk == nk_total - 1)
            def _():
                finish(ins, out_refs, [acc[...] for acc in accs])
        if carry is not None:
            pl.when(at_step(False))(wait)

    ex = carry
    res = pl.pallas_call(
        body, name=name, grid=grid, in_specs=in_specs + ([] if ex is None else ex.in_specs),
        out_specs=out_specs + ([] if ex is None else ex.out_specs),
        out_shape=out_shape + ([] if ex is None else ex.out_shape), input_output_aliases=aliases,
        scratch_shapes=[pltpu.VMEM((tm, tn), F32) for _ in range(n_acc if nk_total > 1 else 0)]
        + ([] if ex is None else ex.scratch),
        compiler_params=_params(("arbitrary",) * 3 if ex is not None else ("parallel", "parallel", "arbitrary")),
    )(*operands, *([] if ex is None else ex.xs))
    if ex is not None:
        return (res[0] if n_out == 1 else res[:n_out]), list(res[n_out:])
    return res[0] if n_out == 1 else res


def _rowwise(fn, rows, vecs, out_rows, out_accs=(), *, name, tm=256, tc=None, carry=None):
    M = rows[0].shape[0]
    tm = _tile(M, tm, BF16_SUBLANES)
    nrow = M // tm
    C = rows[0].shape[1]
    ncol = 1 if tc is None else C // _tile(C, tc)
    tcol = None if tc is None else _tile(C, tc)

    def colwise(shape):
        return tc is not None and len(shape) == 2 and shape[0] == 1 and shape[1] == C

    def vspec(shape):
        if colwise(shape):
            return pl.BlockSpec((1, tcol), lambda j, i: (0, j))
        return pl.BlockSpec(tuple(shape), lambda j, i, n=len(shape): (0,) * n)

    def rspec(width):
        if tc is None:
            return pl.BlockSpec((tm, width), lambda j, i: (i, 0))
        return pl.BlockSpec((tm, tcol), lambda j, i: (i, j))

    in_specs = [rspec(r.shape[1]) for r in rows] + [vspec(v.shape) for v in vecs]
    out_specs = [rspec(c) for c, _ in out_rows] + [vspec(s) for s in out_accs]
    out_shape = [_sds((M, c), d) for c, d in out_rows] + [_sds(s, F32) for s in out_accs]
    n_in, n_or = len(rows) + len(vecs), len(out_rows)

    n_out = n_or + len(out_accs)
    ex = carry

    def body(*refs):
        ins, outs, _, start, wait = _split_refs(refs, n_in, n_out, ex)
        o_rows, o_accs = outs[:n_or], outs[n_or:]
        if ex is not None:
            pl.when((pl.program_id(0) == 0) & (pl.program_id(1) == 0))(start)
        r_out, a_out = fn(*[r[...] for r in ins])
        for o, r in zip(o_rows, r_out):
            o[...] = r.astype(o.dtype)
        i = pl.program_id(1)

        @pl.when(i == 0)
        def _():
            for o, a in zip(o_accs, a_out):
                o[...] = a

        @pl.when(i > 0)
        def _():
            for o, a in zip(o_accs, a_out):
                o[...] += a

        if ex is not None:
            pl.when((pl.program_id(0) == ncol - 1) & (pl.program_id(1) == nrow - 1))(wait)

    res = pl.pallas_call(
        body, name=name, grid=(ncol, nrow), in_specs=in_specs + ([] if ex is None else ex.in_specs),
        out_specs=out_specs + ([] if ex is None else ex.out_specs),
        out_shape=out_shape + ([] if ex is None else ex.out_shape),
        scratch_shapes=[] if ex is None else ex.scratch,
        compiler_params=_params(("arbitrary", "arbitrary") if ex is not None else ("parallel", "arbitrary")),
    )(*rows, *vecs, *([] if ex is None else ex.xs))
    if ex is not None:
        return res[:n_or], res[n_or:n_out], list(res[n_out:])
    return res[:n_or], res[n_or:]


def _colsum(t):
    return jnp.sum(t, axis=0, keepdims=True)


def _gelu(t):
    return 0.5 * t * (1.0 + lax.erf(t * math.sqrt(0.5)))


def _gelu_grad(t):
    return 0.5 * (1.0 + lax.erf(t * math.sqrt(0.5))) + t * jnp.exp(-0.5 * t * t) * (1.0 / math.sqrt(2.0 * math.pi))


def _sigmoid(t):
    return 1.0 / (1.0 + jnp.exp(-t))


def _rms_stats(t, width):
    return lax.rsqrt(jnp.sum(t * t, axis=-1, keepdims=True) * (1.0 / width) + EPS)


def _rms_bwd(dn, tn, r, width):
    return r * (dn - tn * (jnp.sum(dn * tn, axis=-1, keepdims=True) * (1.0 / width)))


def _place():
    return lax.axis_index("x"), lax.axis_index("y"), lax.axis_index("c")


class _ChipExchange:
    def __init__(self, xs, gather):
        self.xs, self.gather, self.n = list(xs), gather, len(xs)
        self.in_specs = [pl.BlockSpec(memory_space=pl.ANY)] * self.n
        self.out_specs = [pl.BlockSpec(memory_space=pl.ANY)] * self.n
        self.out_shape = [_sds((4,) + (x.shape if gather else x.shape[1:]), x.dtype) for x in self.xs]
        self.scratch = [pltpu.SemaphoreType.DMA((self.n, 3)), pltpu.SemaphoreType.DMA((self.n, 3))]

    def bind(self, x_refs, out_refs, send_sems, recv_sems):
        x, y, c = _place()
        p = 2 * x + y
        chips = [(1 - x, y), (x, 1 - y), (1 - x, 1 - y)]

        def copy(w, k, outgoing):
            qx, qy = chips[k]
            there = 2 * qx + qy
            if self.gather:
                src = x_refs[w]
            else:
                src = x_refs[w].at[there if outgoing else p]
            return pltpu.make_async_remote_copy(
                src_ref=src, dst_ref=out_refs[w].at[p if outgoing else there], send_sem=send_sems.at[w, k],
                recv_sem=recv_sems.at[w, k], device_id=(qx, qy, c), device_id_type=MESH)

        def start():
            for w in range(self.n):
                for k in range(3):
                    copy(w, k, True).start()

        def wait():
            for w in range(self.n):
                for k in range(3):
                    copy(w, k, False).wait_recv()
            for w in range(self.n):
                for k in range(3):
                    copy(w, k, True).wait_send()

        return start, wait


class _PairExchange:
    def __init__(self, xs, mode):
        self.xs, self.mode, self.n = list(xs), mode, len(xs)
        self.in_specs = [pl.BlockSpec(memory_space=pl.ANY)] * self.n
        self.out_specs = [pl.BlockSpec(memory_space=pl.ANY)] * self.n
        shape = {"halves": lambda s: (4,) + s[2:], "forward": lambda s: s, "gather": lambda s: (2,) + s}[mode]
        self.out_shape = [_sds(shape(x.shape), x.dtype) for x in self.xs]
        self.scratch = [pltpu.SemaphoreType.DMA((self.n, 3)), pltpu.SemaphoreType.DMA((self.n, 3))]

    def bind(self, x_refs, out_refs, send_sems, recv_sems):
        x, y, c = _place()
        chips = [(1 - x, y), (x, 1 - y), (1 - x, 1 - y)]

        def copy(w, src, dst, k):
            return pltpu.make_async_remote_copy(src_ref=src, dst_ref=dst, send_sem=send_sems.at[w, k],
                                                recv_sem=recv_sems.at[w, k], device_id=(x, y, 1 - c),
                                                device_id_type=MESH)

        def start():
            for w, (xr, orf) in enumerate(zip(x_refs, out_refs)):
                if self.mode == "halves":
                    for q in range(4):
                        copy(w, xr.at[q, 1 - c], orf.at[q], 0).start()
                elif self.mode == "forward":
                    for k, (qx, qy) in enumerate(chips):
                        copy(w, xr.at[2 * qx + qy], orf.at[2 * qx + qy], k).start()
                else:
                    copy(w, xr, orf.at[c], 0).start()

        def wait():
            for w, (xr, orf) in enumerate(zip(x_refs, out_refs)):
                if self.mode == "halves":
                    copy(w, orf, orf, 0).wait()
                elif self.mode == "forward":
                    for k, (qx, qy) in enumerate(chips):
                        copy(w, xr.at[2 * qx + qy], orf.at[2 * qx + qy], k).wait()
                else:
                    cp = copy(w, xr, orf.at[1 - c], 0)
                    cp.wait_recv()
                    cp.wait_send()

        return start, wait


def _split_refs(refs, n_in, n_out, ex):
    ne = 0 if ex is None else ex.n
    ins, xin = refs[:n_in], refs[n_in:n_in + ne]
    outs, xout = refs[n_in + ne:n_in + ne + n_out], refs[n_in + ne + n_out:n_in + 2 * ne + n_out]
    rest = refs[n_in + 2 * ne + n_out:]
    if ex is None:
        return ins, outs, rest, None, None
    start, wait = ex.bind(xin, xout, rest[-2], rest[-1])
    return ins, outs, rest[:-2], start, wait


def _attn_fwd(q, k, v, *, heads, tq=512, carry=None):
    N, M = q.shape[0], k.shape[0]
    tq = _tile(N, tq)
    sub = _tile(tq, MXU_DIM)
    vd = v.shape[1] // heads
    nq = N // tq

    def body(*refs):
        (q_ref, k_ref, v_ref), (o_ref, lse_ref), _, start, wait = _split_refs(refs, 3, 2, carry)
        if carry is not None:
            pl.when((pl.program_id(0) == 0) & (pl.program_id(1) == 0))(start)
        for sb in range(tq // sub):
            rows = pl.ds(sb * sub, sub)
            s = lax.dot_general(q_ref[rows, :], k_ref[...], (((1,), (1,)), ((), ())), preferred_element_type=F32)
            m = jnp.max(s, axis=-1, keepdims=True)
            p = jnp.exp(s - m)
            l = jnp.sum(p, axis=-1, keepdims=True)
            o = jnp.dot(p.astype(BF16), v_ref[...], preferred_element_type=F32) / l
            o_ref[rows, :] = o.astype(o_ref.dtype)
            lse_ref[rows, :] = jnp.broadcast_to(m + jnp.log(l), (sub, vd))
        if carry is not None:
            pl.when((pl.program_id(0) == heads - 1) & (pl.program_id(1) == nq - 1))(wait)

    ex = carry
    res = pl.pallas_call(
        body, name="attn_fwd", grid=(heads, nq),
        in_specs=[pl.BlockSpec((tq, HEAD_PAD), lambda h, i: (i, h)),
                  pl.BlockSpec((M, HEAD_PAD), lambda h, i: (0, h)),
                  pl.BlockSpec((M, vd), lambda h, i: (0, h))] + ([] if ex is None else ex.in_specs),
        out_specs=[pl.BlockSpec((tq, vd), lambda h, i: (i, h)),
                   pl.BlockSpec((tq, vd), lambda h, i: (i, h))] + ([] if ex is None else ex.out_specs),
        out_shape=[_sds((N, heads * vd), BF16), _sds((N, heads * vd), F32)] + ([] if ex is None else ex.out_shape),
        scratch_shapes=[] if ex is None else ex.scratch,
        compiler_params=_params(("arbitrary", "arbitrary")),
    )(q, k, v, *([] if ex is None else ex.xs))
    return res[0], res[1], list(res[2:])


def _attn_bwd(q, k, v, o, lse, do, *, heads, tq=512, carry=None):
    N, M = q.shape[0], k.shape[0]
    tq = _tile(N, tq)
    vd = v.shape[1] // heads
    nq = N // tq
    sub = _tile(tq, MXU_DIM)
    nt = (((1,), (1,)), ((), ()))
    tn = (((0,), (0,)), ((), ()))

    def body(*refs):
        (q_ref, k_ref, v_ref, o_ref, lse_ref, do_ref), (dq_ref, dk_ref, dv_ref), _, start, wait = _split_refs(
            refs, 6, 3, carry)
        if carry is not None:
            pl.when((pl.program_id(0) == 0) & (pl.program_id(1) == 0))(start)
        i = pl.program_id(1)
        kb, vb = k_ref[...], v_ref[...]
        parts = []
        for sb in range(tq // sub):
            rows = pl.ds(sb * sub, sub)
            qb, dob = q_ref[rows, :], do_ref[rows, :]
            delta = jnp.sum(dob.astype(F32) * o_ref[rows, :].astype(F32), axis=-1, keepdims=True)
            s = lax.dot_general(qb, kb, nt, preferred_element_type=F32)
            p = jnp.exp(s - lse_ref[rows, :][:, :1])
            dp = lax.dot_general(dob, vb, nt, preferred_element_type=F32)
            ds = (p * (dp - delta)).astype(BF16)
            dq_ref[rows, :] = jnp.dot(ds, kb, preferred_element_type=F32)
            parts.append((lax.dot_general(ds, qb, tn, preferred_element_type=F32),
                          lax.dot_general(p.astype(BF16), dob, tn, preferred_element_type=F32)))

        @pl.when(i == 0)
        def _():
            dk_ref[...] = parts[0][0]
            dv_ref[...] = parts[0][1]

        @pl.when(i > 0)
        def _():
            dk_ref[...] += parts[0][0]
            dv_ref[...] += parts[0][1]

        for dk_part, dv_part in parts[1:]:
            dk_ref[...] += dk_part
            dv_ref[...] += dv_part

        if carry is not None:
            pl.when((pl.program_id(0) == heads - 1) & (pl.program_id(1) == nq - 1))(wait)

    ex = carry
    res = pl.pallas_call(
        body, name="attn_bwd", grid=(heads, nq),
        in_specs=[pl.BlockSpec((tq, HEAD_PAD), lambda h, i: (i, h)),
                  pl.BlockSpec((M, HEAD_PAD), lambda h, i: (0, h)),
                  pl.BlockSpec((M, vd), lambda h, i: (0, h)),
                  pl.BlockSpec((tq, vd), lambda h, i: (i, h)),
                  pl.BlockSpec((tq, vd), lambda h, i: (i, h)),
                  pl.BlockSpec((tq, vd), lambda h, i: (i, h))] + ([] if ex is None else ex.in_specs),
        out_specs=[pl.BlockSpec((tq, HEAD_PAD), lambda h, i: (i, h)),
                   pl.BlockSpec((M, HEAD_PAD), lambda h, i: (0, h)),
                   pl.BlockSpec((M, vd), lambda h, i: (0, h))] + ([] if ex is None else ex.out_specs),
        out_shape=[_sds((N, heads * HEAD_PAD), F32), _sds((M, heads * HEAD_PAD), F32),
                   _sds((M, heads * vd), F32)] + ([] if ex is None else ex.out_shape),
        scratch_shapes=[] if ex is None else ex.scratch,
        compiler_params=_params(("arbitrary", "arbitrary")),
    )(q, k, v, o, lse, do, *([] if ex is None else ex.xs))
    return res[0], res[1], res[2], list(res[3:])


def _comm_call(body, xs, out_shapes, n_sems, name, in_vmem):
    space = pltpu.VMEM if in_vmem else pl.ANY
    n = len(xs)

    def wrapped(*refs):
        body(refs[:n], refs[n:2 * n], *refs[2 * n:])

    return pl.pallas_call(
        wrapped, name=name, out_shape=list(out_shapes),
        in_specs=[pl.BlockSpec(memory_space=space)] * n, out_specs=[pl.BlockSpec(memory_space=space)] * n,
        scratch_shapes=[pltpu.SemaphoreType.DMA((n, n_sems)), pltpu.SemaphoreType.DMA((n, n_sems)),
                        pltpu.SemaphoreType.DMA((n,))],
        compiler_params=pltpu.CompilerParams(vmem_limit_bytes=VMEM_LIMIT_BYTES),
    )(*xs)


def _all_gather8(blks, *, name, in_vmem):
    def body(x_refs, out_refs, send_sems, recv_sems, local_sems):
        x, y, c = _place()
        me, sibling = (x, y, c), (x, y, 1 - c)
        chips = [(1 - x, y), (x, 1 - y), (1 - x, 1 - y)]
        waits = []
        for w, (x_ref, out_ref) in enumerate(zip(x_refs, out_refs)):
            def slot(px, py, pc, out_ref=out_ref):
                return out_ref.at[4 * px + 2 * py + pc]

            def copy(k, block, to, src=None, w=w, slot=slot):
                return pltpu.make_async_remote_copy(
                    src_ref=slot(*block) if src is None else src, dst_ref=slot(*block),
                    send_sem=send_sems.at[w, k], recv_sem=recv_sems.at[w, k], device_id=to, device_id_type=MESH)

            mine = pltpu.make_async_copy(x_ref, slot(*me), local_sems.at[w])
            mine.start()
            first = [copy(0, me, sibling, src=x_ref)]
            first += [copy(1 + j, me, (*chip, c), src=x_ref) for j, chip in enumerate(chips)]
            for cp in first:
                cp.start()
            waits.append((copy, mine, first))
        for copy, mine, first in waits:
            passed = [copy(4 + j, (*chip, c), sibling) for j, chip in enumerate(chips)]
            for j, chip in enumerate(chips):
                copy(1 + j, (*chip, c), me).wait_recv()
                passed[j].start()
            copy(0, sibling, me).wait_recv()
            for j, chip in enumerate(chips):
                copy(4 + j, (*chip, 1 - c), me).wait_recv()
            for cp in first + passed:
                cp.wait_send()
            mine.wait()

    return _comm_call(body, blks, [_sds((8,) + b.shape, b.dtype) for b in blks], 7, name, in_vmem)


def _gather_others(blks, *, name):
    def body(x_refs, out_refs, send_sems, recv_sems, local_sems):
        x, y, c = _place()
        own, xn, yn, dg = (x, y), (1 - x, y), (x, 1 - y), (1 - x, 1 - y)

        def slot(w, chip, core):
            return out_refs[w].at[4 * chip[0] + 2 * chip[1] + core]

        def cp(w, k, src, dst, chip, core):
            return pltpu.make_async_remote_copy(src_ref=src, dst_ref=dst, send_sem=send_sems.at[w, k],
                                                recv_sem=recv_sems.at[w, k], device_id=(*chip, core),
                                                device_id_type=MESH)

        def halves(w):
            h = x_refs[w].shape[0] // 2
            return pl.ds(0, h), pl.ds(h, h)

        sends = []
        for w, x_ref in enumerate(x_refs):
            sends += [cp(w, 0, x_ref, slot(w, own, c), xn, c), cp(w, 1, x_ref, slot(w, own, c), yn, c)]
        for s in sends:
            s.start()
        for w, x_ref in enumerate(x_refs):
            lo, hi = halves(w)
            cp(w, 1, x_ref, slot(w, yn, c), yn, c).wait_recv()
            passed = [cp(w, 2, slot(w, yn, c).at[lo], slot(w, yn, c).at[lo], xn, c),
                      cp(w, 4, slot(w, yn, c), slot(w, yn, c), own, 1 - c)]
            cp(w, 0, x_ref, slot(w, xn, c), xn, c).wait_recv()
            passed += [cp(w, 3, slot(w, xn, c).at[hi], slot(w, xn, c).at[hi], yn, c),
                       cp(w, 5, slot(w, xn, c), slot(w, xn, c), own, 1 - c)]
            for s in passed:
                s.start()
            sends += passed
        for w in range(len(x_refs)):
            lo, hi = halves(w)
            cp(w, 2, slot(w, dg, c).at[lo], slot(w, dg, c).at[lo], xn, c).wait_recv()
            cp(w, 3, slot(w, dg, c).at[hi], slot(w, dg, c).at[hi], yn, c).wait_recv()
            passed = [cp(w, 6, slot(w, dg, c), slot(w, dg, c), own, 1 - c)]
            passed[0].start()
            sends += passed
        for w in range(len(x_refs)):
            cp(w, 4, slot(w, yn, c), slot(w, yn, 1 - c), own, 1 - c).wait_recv()
            cp(w, 5, slot(w, xn, c), slot(w, xn, 1 - c), own, 1 - c).wait_recv()
            cp(w, 6, slot(w, dg, c), slot(w, dg, 1 - c), own, 1 - c).wait_recv()
        for s in sends:
            s.wait_send()

    return list(_comm_call(body, blks, [_sds((8,) + b.shape, b.dtype) for b in blks], 7, name, False))


def _exchange_alone(ex, *, name):
    def body(x_refs, out_refs, send_sems, recv_sems, local_sems):
        start, wait = ex.bind(x_refs, out_refs, send_sems, recv_sems)
        start()
        wait()

    return list(_comm_call(body, ex.xs, ex.out_shape, 3, name, False))


def _block_rows(rows, row_bytes, target=1 << 21, align=BF16_SUBLANES):
    return _tile(rows, max(align, target // row_bytes // align * align), align)


def _sum_blocks(buf, *, name, out_dtype):
    B, R, C = buf.shape
    tm = _block_rows(R, B * C * buf.dtype.itemsize)

    def body(x_ref, o_ref):
        acc = x_ref[0].astype(F32)
        for b in range(1, B):
            acc = acc + x_ref[b].astype(F32)
        o_ref[...] = acc.astype(o_ref.dtype)

    return pl.pallas_call(
        body, name=name, grid=(R // tm,), in_specs=[pl.BlockSpec((B, tm, C), lambda i: (0, i, 0))],
        out_specs=pl.BlockSpec((tm, C), lambda i: (i, 0)), out_shape=_sds((R, C), out_dtype),
        compiler_params=_params(("parallel",)),
    )(buf)


def _pair_add(mine, theirs, core, *, name):
    _, _, R, C = mine.shape
    tm = _block_rows(R, C * 2)

    def body(core_ref, a_ref, b_ref, o_ref):
        o_ref[...] = (a_ref[...].astype(F32) + b_ref[...].astype(F32)).astype(o_ref.dtype)

    return pl.pallas_call(
        body, name=name, out_shape=_sds(theirs.shape, BF16),
        grid_spec=pltpu.PrefetchScalarGridSpec(
            num_scalar_prefetch=1, grid=(4, R // tm),
            in_specs=[pl.BlockSpec((None, None, tm, C), lambda q, i, core_ref: (q, core_ref[0], i, 0)),
                      pl.BlockSpec((None, tm, C), lambda q, i, core_ref: (q, i, 0))],
            out_specs=pl.BlockSpec((None, tm, C), lambda q, i, core_ref: (q, i, 0))),
        compiler_params=_params(("parallel", "parallel")),
    )(core, mine, theirs)


def _assemble(gathered, own, chip, *, name, transpose):
    _, K, Ns = gathered.shape
    tm = _block_rows(K, Ns * 4)

    def body(chip_ref, g_ref, own_ref, o_ref):
        q = pl.program_id(0)

        @pl.when(q == chip_ref[0])
        def _():
            o_ref[...] = own_ref[...].astype(BF16)

        @pl.when(q != chip_ref[0])
        def _():
            o_ref[...] = g_ref[...]

    if transpose:
        out_spec = pl.BlockSpec((tm, Ns), lambda q, i, ch: (i, q))
        out_shape = _sds((K, 4 * Ns), BF16)
    else:
        out_spec = pl.BlockSpec((None, tm, Ns), lambda q, i, ch: (q, i, 0))
        out_shape = _sds((4, K, Ns), BF16)
    return pl.pallas_call(
        body, name=name, out_shape=out_shape,
        grid_spec=pltpu.PrefetchScalarGridSpec(
            num_scalar_prefetch=1, grid=(4, K // tm),
            in_specs=[pl.BlockSpec((None, tm, Ns), lambda q, i, ch: (jnp.where(q == ch[0], (q + 1) % 4, q), i, 0)),
                      pl.BlockSpec((tm, Ns), lambda q, i, ch: (jnp.where(q == ch[0], i, 0), 0))],
            out_specs=out_spec),
        compiler_params=_params(("arbitrary", "arbitrary")),
    )(chip, gathered, own)


def _assemble_halves(mine, theirs, own, place, *, name, transpose):
    _, K2, Ns = mine.shape
    tm = _block_rows(K2, Ns * 4)
    nb = K2 // tm

    def body(place_ref, m_ref, t_ref, own_ref, o_ref):
        q, hb = pl.program_id(0), pl.program_id(1)
        is_own = q == place_ref[0]
        is_mine = hb == place_ref[1]

        @pl.when(is_own)
        def _():
            o_ref[...] = own_ref[...].astype(BF16)

        @pl.when(jnp.logical_not(is_own) & is_mine)
        def _():
            o_ref[...] = m_ref[...]

        @pl.when(jnp.logical_not(is_own) & jnp.logical_not(is_mine))
        def _():
            o_ref[...] = t_ref[...]

    def other(q, pr):
        return jnp.where(q == pr[0], (q + 1) % 4, q)

    if transpose:
        out_spec = pl.BlockSpec((tm, Ns), lambda q, hb, i, pr: (hb * nb + i, q))
        out_shape = _sds((2 * K2, 4 * Ns), BF16)
    else:
        out_spec = pl.BlockSpec((None, tm, Ns), lambda q, hb, i, pr: (q, hb * nb + i, 0))
        out_shape = _sds((4, 2 * K2, Ns), BF16)
    return pl.pallas_call(
        body, name=name, out_shape=out_shape,
        grid_spec=pltpu.PrefetchScalarGridSpec(
            num_scalar_prefetch=1, grid=(4, 2, nb),
            in_specs=[pl.BlockSpec((None, tm, Ns), lambda q, hb, i, pr: (other(q, pr), jnp.where(hb == pr[1], i, 0), 0)),
                      pl.BlockSpec((None, tm, Ns), lambda q, hb, i, pr: (other(q, pr), jnp.where(hb == pr[1], 0, i), 0)),
                      pl.BlockSpec((tm, Ns), lambda q, hb, i, pr: (jnp.where(q == pr[0], hb * nb + i, 0), 0))],
            out_specs=out_spec),
        compiler_params=_params(("arbitrary",) * 3),
    )(place, mine, theirs, own)


def _split_lanes(row, widths):
    out, off = [], 0
    for wd in widths:
        out.append(row[:, off:off + wd])
        off += wd
    return out


def _adamw_math(w, g, m, v):
    m = ADAM_B1 * m + (1.0 - ADAM_B1) * g
    v = ADAM_B2 * v + (1.0 - ADAM_B2) * (g * g)
    m_hat = m / (1.0 - ADAM_B1 ** ADAM_STEP)
    v_hat = v / (1.0 - ADAM_B2 ** ADAM_STEP)
    delta = -ADAM_LR * (m_hat / (jnp.sqrt(v_hat) + ADAM_EPS) + ADAM_WD * w)
    return delta, m, v


def _adamw_halves(w, mine, other, m, v, core, *, name):
    K, Ns = w.shape
    tm = _block_rows(K // 2, Ns * 4, target=1 << 20, align=F32_SUBLANES)
    nb = (K // 2) // tm

    def body(core_ref, w_ref, mine_ref, other_ref, m_ref, v_ref, g_out, d_out, m_out, v_out):
        g = jnp.where(pl.program_id(0) // nb == core_ref[0], mine_ref[...], other_ref[...])
        g_out[...] = g
        d_out[...], m_out[...], v_out[...] = _adamw_math(w_ref[...], g, m_ref[...], v_ref[...])

    row = pl.BlockSpec((tm, Ns), lambda i, cr: (i, 0))
    return pl.pallas_call(
        body, name=name, out_shape=[_sds((K, Ns), F32)] * 4,
        grid_spec=pltpu.PrefetchScalarGridSpec(
            num_scalar_prefetch=1, grid=(K // tm,),
            in_specs=[row,
                      pl.BlockSpec((tm, Ns), lambda i, cr: (jnp.where(i // nb == cr[0], i % nb, 0), 0)),
                      pl.BlockSpec((None, tm, Ns), lambda i, cr: (1 - cr[0], jnp.where(i // nb == cr[0], 0, i % nb), 0)),
                      row, row],
            out_specs=[row, row, row, row]),
        compiler_params=_params(("arbitrary",)),
    )(core, w, mine, other, m, v)


def _adamw(w, g, m, v, *, name, carry=None):
    C = w.shape[1]

    def fn(w, g, m, v):
        return _adamw_math(w, g, m, v), ()

    tm = max(F32_SUBLANES, min(512, (1 << 20) // (4 * C) // F32_SUBLANES * F32_SUBLANES))
    res = _rowwise(fn, [w, g, m, v], [], [(C, F32)] * 3, name=name, tm=tm, carry=carry)
    return tuple(res[0]) + ((res[2],) if carry is not None else ())


def _rope_tables(n):
    rows = n // GRID_W
    row = jnp.repeat(jnp.arange(rows, dtype=F32), GRID_W)
    col = jnp.tile(jnp.arange(GRID_W, dtype=F32), rows)
    nf = ROPE_DIM // 4
    freqs = ROPE_THETA ** (-jnp.arange(nf, dtype=F32) / nf)
    ang_r, ang_c = row[:, None] * freqs[None, :], col[:, None] * freqs[None, :]
    cr, sr, cc, sc = jnp.cos(ang_r), jnp.sin(ang_r), jnp.cos(ang_c), jnp.sin(ang_c)
    nope = HEAD_PAD - 2 * ROPE_DIM
    one, zero, z = jnp.ones((n, nope), F32), jnp.zeros((n, nope), F32), jnp.zeros((n, nf), F32)
    pad = jnp.zeros((n, ROPE_DIM), F32)
    cos = jnp.concatenate([one, cr, cr, cc, cc, pad], axis=1)
    s_lo = jnp.concatenate([zero, -sr, z, -sc, z, pad], axis=1)
    s_hi = jnp.concatenate([zero, z, sr, z, sc, pad], axis=1)
    return cos, s_lo, s_hi


def _rope(n, cos, s_lo, s_hi):
    q = ROPE_DIM // 4
    return n * cos + pltpu.roll(n, HEAD_PAD - q, 1) * s_lo + pltpu.roll(n, q, 1) * s_hi


def _rope_t(d, cos, s_lo, s_hi):
    q = ROPE_DIM // 4
    return d * cos + pltpu.roll(d * s_lo, q, 1) + pltpu.roll(d * s_hi, HEAD_PAD - q, 1)


def kernel(x, c, ctx, c_ctx, w_mod, b_mod, norm1_g, w_in, q_norm_g, kv_norm_g, w_uq, w_ukv, qk_norm_q, qk_norm_k, sgu_norm_g, sgu_norm_b, w_spatial, b_spatial, w_br_attn, w_br_sgu, w_out, norm2_g, w_ffn_in, w_ffn_out, loss_target, m_c_ctx, m_w_mod, m_b_mod, m_norm1_g, m_w_in, m_q_norm_g, m_kv_norm_g, m_w_uq, m_w_ukv, m_qk_norm_q, m_qk_norm_k, m_sgu_norm_g, m_sgu_norm_b, m_w_spatial, m_b_spatial, m_w_br_attn, m_w_br_sgu, m_w_out, m_norm2_g, m_w_ffn_in, m_w_ffn_out, v_c_ctx, v_w_mod, v_b_mod, v_norm1_g, v_w_in, v_q_norm_g, v_kv_norm_g, v_w_uq, v_w_ukv, v_qk_norm_q, v_qk_norm_k, v_sgu_norm_g, v_sgu_norm_b, v_w_spatial, v_b_spatial, v_w_br_attn, v_w_br_sgu, v_w_out, v_norm2_g, v_w_ffn_in, v_w_ffn_out):
    ax, ay, ac = _place()
    my_chip = 2 * ax + ay
    my_dev = 4 * ax + 2 * ay + ac

    N, D = x.shape[1], x.shape[2]
    CT = ctx.shape[1]
    M = N + CT
    QL, KVL, QK = q_norm_g.shape[-1], kv_norm_g.shape[-1], qk_norm_q.shape[-1]
    NOPE = QK - ROPE_DIM
    VD = NOPE
    H = 4 * w_uq.shape[-1] // QK
    SW, G, CH = sgu_norm_g.shape[-1], w_spatial.shape[1], w_spatial.shape[2]
    GD = SW // G
    DFF = 4 * w_ffn_out.shape[1]
    NMOD = 4 * w_mod.shape[-1]
    NM = w_mod.shape[-1]
    KVP = KVL + 2 * ROPE_DIM
    assert NOPE == LANES and GD == LANES and HEAD_PAD == NOPE + 2 * ROPE_DIM and CH == LANES
    scale = QK ** -0.5

    x2, ctx2, tgt2 = x[0], ctx[0], loss_target[0]

    c_all = _all_gather8([c], name="ag_c", in_vmem=True)[0][:, 0, :]
    c_rows = jnp.concatenate([c_all, c_ctx[None, :], jnp.zeros((BF16_SUBLANES - 9, D), F32)], axis=0)

    def silu_fn(t):
        s = _sigmoid(t)
        return (t * s, s * (1.0 + t * (1.0 - s))), ()

    (silu_c, dsilu_c), _ = _rowwise(silu_fn, [c_rows], [], [(D, F32), (D, F32)], name="silu_c", tm=16)
    wm = w_mod[0]
    mod_loc = _mm([(silu_c, wm)], name="mod_fwd", outs=(F32,), tn=512, tk=512,
                  extras=[(lax.dynamic_slice_in_dim(b_mod, my_chip * NM, NM, axis=1), "n")],
                  epi=lambda acc, b: (acc + b,))
    mod_all = _all_gather8([mod_loc], name="ag_mod", in_vmem=True)[0]
    mod_full = jnp.concatenate([mod_all[0], mod_all[2], mod_all[4], mod_all[6]], axis=1)
    mod_me = lax.dynamic_slice_in_dim(mod_full, my_dev, 1, axis=0)
    sh1, sc1, g1, sh2, sc2, g2 = [mod_me[:, i * D:(i + 1) * D] for i in range(6)]
    sh1c, sc1c = mod_full[8:9, :D], mod_full[8:9, D:2 * D]

    big = [w_in[0], w_uq[0], w_ukv[0], w_br_attn[0], w_br_sgu[0], w_out[0], w_ffn_in[0], w_ffn_out[0]]
    col_sharded = [True, True, True, True, True, False, True, False]
    halves = [lax.dynamic_slice_in_dim(a, ac * (a.shape[0] // 2), a.shape[0] // 2, axis=0).astype(BF16) for a in big]
    tags = ["w_in", "w_uq", "w_ukv", "w_br_attn", "w_br_sgu", "w_out", "w_ffn_in", "w_ffn_out"]
    first_group, attn_group, ffn_group = [0, 1, 2], [3, 4, 5, 6], [7]
    chip1 = jnp.reshape(my_chip, (1,)).astype(jnp.int32)
    place2 = jnp.stack([my_chip, ac]).astype(jnp.int32)

    def laid_out(seg, i):
        a = big[i]
        if col_sharded[i] and seg.ndim == 3:
            return seg.transpose(1, 0, 2).reshape(a.shape[0], 4 * a.shape[1])
        return seg if col_sharded[i] else seg.reshape(4 * a.shape[0], a.shape[1])

    def side_by_side(i):
        return col_sharded[i] and big[i].shape[1] % LANES == 0

    def finish_gather(idx, mine4, theirs4):
        return [laid_out(_assemble_halves(m, t, big[i], place2, name="assemble_" + tags[i], transpose=side_by_side(i)), i)
                for i, m, t in zip(idx, mine4, theirs4)]

    gathered = _gather_others([halves[i] for i in first_group], name="ag_weights")
    w_in_f, w_uq_f, w_ukv_f = [
        laid_out(_assemble(seg.reshape((4,) + big[i].shape), big[i], chip1, name="assemble_" + tags[i],
                           transpose=side_by_side(i)), i) for i, seg in zip(first_group, gathered)]
    o_kv, o_u = QL, QL + KVL + ROPE_DIM
    o_v, o_g = o_u + SW, o_u + 2 * SW
    w_q = w_in_f[:, :QL]
    w_kv = jnp.pad(w_in_f[:, o_kv:o_u], ((0, 0), (0, ROPE_DIM)))
    w_u, w_v = w_in_f[:, o_u:o_v], w_in_f[:, o_v:o_g]
    w_g1, w_g2 = w_in_f[:, o_g:o_g + D], w_in_f[:, o_g + D:]
    w_uq_p = jnp.pad(w_uq_f.reshape(QL, H, QK), ((0, 0), (0, 0), (0, HEAD_PAD - QK))).reshape(QL, H * HEAD_PAD)

    cos_t, slo_t, shi_t = _rope_tables(N)
    ones_c = jnp.concatenate([jnp.ones((CT, NOPE + ROPE_DIM), F32), jnp.zeros((CT, ROPE_DIM), F32)], axis=1)
    cos_k = jnp.concatenate([cos_t, ones_c], axis=0)
    slo_k = jnp.concatenate([slo_t, jnp.zeros((CT, HEAD_PAD), F32)], axis=0)
    shi_k = jnp.concatenate([shi_t, jnp.zeros((CT, HEAD_PAD), F32)], axis=0)
    gq_p = jnp.pad(qk_norm_q, ((0, 0), (0, HEAD_PAD - QK)))
    gk_p = jnp.pad(qk_norm_k, ((0, 0), (0, HEAD_PAD - QK)))

    def norm_mod_fn(t, g, sh, sc):
        r = _rms_stats(t, D)
        return (((t * r) * g) * (1.0 + sc) + sh,), ()

    (h,), _ = _rowwise(norm_mod_fn, [x2], [norm1_g, sh1, sc1], [(D, BF16)], name="norm1_x")
    (ctx_h,), _ = _rowwise(norm_mod_fn, [ctx2], [norm1_g, sh1c, sc1c], [(D, BF16)], name="norm1_ctx")

    qc = _mm([(h, w_q)], name="proj_q", outs=(F32,))
    kvin = jnp.concatenate([_mm([(h, w_kv)], name="proj_kv", outs=(F32,)),
                            _mm([(ctx_h, w_kv)], name="proj_kv_ctx", outs=(F32,))], axis=0)
    u_in = _mm([(h, w_u)], name="proj_u", outs=(BF16,))
    v_in = _mm([(h, w_v)], name="proj_v", outs=(BF16,))
    g1_in, (mine_bra,) = _mm([(h, w_g1)], name="proj_g1", outs=(BF16,), carry=_ChipExchange([halves[3]], gather=True))
    g2_in, (mine_brs,) = _mm([(h, w_g2)], name="proj_g2", outs=(BF16,), carry=_ChipExchange([halves[4]], gather=True))

    def rms_gain_fn(width):
        def fn(t, g):
            return (((t * _rms_stats(t, width)) * g),), ()
        return fn

    (qn,), _ = _rowwise(rms_gain_fn(QL), [qc], [q_norm_g], [(QL, BF16)], name="q_norm")

    def kv_norm_fn(t, g):
        kvc = t[:, :KVL]
        return (((kvc * _rms_stats(kvc, KVL)) * g),), ()

    (kvn,), _ = _rowwise(kv_norm_fn, [kvin], [kv_norm_g], [(KVL, BF16)], name="kv_norm")
    q_raw = _mm([(qn, w_uq_p)], name="q_up", outs=(F32,))
    kv_raw = _mm([(kvn, w_ukv_f)], name="kv_up", outs=(F32,))

    def q_post_fn(t, cos, slo, shi, g):
        outs = []
        for hd in range(H):
            th = t[:, hd * HEAD_PAD:(hd + 1) * HEAD_PAD]
            outs.append(_rope((th * _rms_stats(th, QK)) * g, cos, slo, shi) * scale)
        return (jnp.concatenate(outs, axis=1),), ()

    (q_att,), _ = _rowwise(q_post_fn, [q_raw, cos_t, slo_t, shi_t], [gq_p], [(H * HEAD_PAD, BF16)], name="q_post")

    def k_post_fn(t, kvi, cos, slo, shi, g):
        kr = kvi[:, KVL:]
        ks, vs = [], []
        for hd in range(H):
            th = jnp.concatenate([t[:, hd * HEAD_PAD:hd * HEAD_PAD + NOPE], kr], axis=1)
            ks.append(_rope((th * _rms_stats(th, QK)) * g, cos, slo, shi))
            vs.append(t[:, hd * HEAD_PAD + NOPE:(hd + 1) * HEAD_PAD])
        return (jnp.concatenate(ks, axis=1), jnp.concatenate(vs, axis=1)), ()

    (k_att, v_att), _, (mine_out,) = _rowwise(k_post_fn, [kv_raw, kvin, cos_k, slo_k, shi_k], [gk_p],
                                              [(H * HEAD_PAD, BF16), (H * VD, BF16)], name="k_post",
                                              carry=_ChipExchange([halves[5]], gather=True))
    attn_o, lse, (mine_ffi,) = _attn_fwd(q_att, k_att, v_att, heads=H, carry=_ChipExchange([halves[6]], gather=True))
    mine4 = [mine_bra, mine_brs, mine_out, mine_ffi]

    ws3 = w_spatial[0]
    bs_t = jnp.pad(b_spatial[0].T, ((0, 0), (0, LANES - G)))

    def sgu_parts(u_in, v_in, ng, nb):
        u, v = _gelu(u_in.astype(F32)), _gelu(v_in.astype(F32))
        mu = jnp.mean(v, axis=-1, keepdims=True)
        vc = v - mu
        rs = lax.rsqrt(jnp.mean(vc * vc, axis=-1, keepdims=True) + EPS)
        xhat = vc * rs
        return u, xhat, rs, (xhat * ng + nb).astype(BF16)

    def sgu_fwd_fn(u_in, v_in, ng, nb, ws, bst):
        u, _, _, vnb = sgu_parts(u_in, v_in, ng, nb)
        outs = []
        for g in range(G):
            sl = slice(g * GD, (g + 1) * GD)
            mixed = jnp.dot(ws[g].astype(BF16), vnb[:, sl], preferred_element_type=F32) + bst[:, g:g + 1]
            outs.append(u[:, sl] * mixed)
        return (jnp.concatenate(outs, axis=1),), ()

    (sgu_o,), _, theirs4 = _rowwise(sgu_fwd_fn, [u_in, v_in], [sgu_norm_g, sgu_norm_b, ws3, bs_t], [(SW, BF16)],
                                    name="sgu_fwd", tm=CH, carry=_PairExchange(mine4, "forward"))
    w_bra, w_brs, w_out_f, w_ffi = finish_gather(attn_group, mine4, theirs4)
    w_fa, w_fb = w_ffi[:, :DFF], w_ffi[:, DFF:]

    a1 = _mm([(attn_o, w_bra)], name="br_attn", outs=(BF16,))
    def merge_epi(acc, a1v, gi1, gi2):
        return acc, _sigmoid(gi1.astype(F32)) * a1v.astype(F32) + _sigmoid(gi2.astype(F32)) * acc

    a2, merged = _mm([(sgu_o, w_brs)], name="br_sgu", outs=(BF16, BF16),
                     extras=[(a1, "mn"), (g1_in, "mn"), (g2_in, "mn")], epi=merge_epi)

    def res_gate(acc, res, gate):
        return res + gate * acc, acc

    x1, mo = _mm([(merged, w_out_f)], name="out_proj", outs=(F32, BF16), tn=1024,
                 extras=[(x2, "mn"), (g1, "n")], epi=res_gate)
    (h2,), _ = _rowwise(norm_mod_fn, [x1], [norm2_g, sh2, sc2], [(D, BF16)], name="norm2")

    def swiglu_epi(a, b):
        return a, b, (a * _sigmoid(a)) * b

    (fa, fb, act), mine4 = _mm([(h2, w_fa, w_fb)], name="ffn_in", outs=(BF16, BF16, BF16), tn=512, epi=swiglu_epi,
                               carry=_ChipExchange([halves[i] for i in ffn_group], gather=True))
    (w_ffo,) = finish_gather(ffn_group, mine4, _exchange_alone(_PairExchange(mine4, "forward"), name="ag_forward_ffn"))
    def loss_epi(acc, res, t, gate):
        e = (res + gate * acc) - t
        dy = e * (1.0 / D)
        return dy, gate * dy, _colsum(e * e) * (0.5 / D), _colsum(dy * acc)

    dy, df, loss_part, dg2_part = _mm([(act, w_ffo)], name="ffn_out", outs=(F32, BF16), tn=1024, col_sums=2,
                                      extras=[(x1, "mn"), (tgt2, "mn"), (g2, "n")], epi=loss_epi)

    def fold_fn(a, b):
        return (), (_colsum(a), _colsum(b))

    _, (loss_cols, dg2) = _rowwise(fold_fn, [loss_part[:, 0, :], dg2_part[:, 0, :]], [], [], [(1, D), (1, D)],
                                   name="loss_fold", tm=loss_part.shape[0])

    def swiglu_bwd_epi(dact, a, b):
        a, b = a.astype(F32), b.astype(F32)
        s = _sigmoid(a)
        return dact * b * (s * (1.0 + a * (1.0 - s))), dact * (a * s)

    da, db = _mm([(df, w_ffo)], tb=True, name="ffn_out_dx", outs=(BF16, BF16), tn=512,
                 extras=[(fa, "mn"), (fb, "mn")], epi=swiglu_bwd_epi)
    dw_ffo = _mm([(act, df)], ta=True, name="ffn_out_dw", outs=(BF16,))
    dh2 = _mm([(da, w_fa), (db, w_fb)], tb=True, name="ffn_in_dx", outs=(F32,))
    ns_ffi = w_ffn_in.shape[-1]
    dw_ffi = _mm([(h2, da)], ta=True, name="ffn_in_dw_a", outs=(BF16,), tn=1408, split=ns_ffi,
                 into=(lax.empty((4, D, ns_ffi), BF16), 0))
    dw_ffi = _mm([(h2, db)], ta=True, name="ffn_in_dw_b", outs=(BF16,), tn=1408, split=ns_ffi, into=(dw_ffi, 2))

    def norm2_bwd_fn(dh, t, dyv, mov, g, sc, g1v):
        r = _rms_stats(t, D)
        tn = t * r
        dxg = dh * (1.0 + sc)
        dt = dyv + _rms_bwd(dxg * g, tn, r, D)
        return (dt, g1v * dt), (_colsum(dh), _colsum(dh * (tn * g)), _colsum(dxg * tn), _colsum(dt * mov.astype(F32)))

    (dx1, dmo), (dsh2, dsc2, dn2g, dg1) = _rowwise(
        norm2_bwd_fn, [dh2, x1, dy, mo], [norm2_g, sc2, g1], [(D, F32), (D, BF16)], [(1, D)] * 4, name="norm2_bwd")

    def merge_bwd_epi(dm, a1, a2, gi1, gi2):
        s1, s2 = _sigmoid(gi1.astype(F32)), _sigmoid(gi2.astype(F32))
        a1, a2 = a1.astype(F32), a2.astype(F32)
        return dm * s1, dm * s2, dm * a1 * (s1 * (1.0 - s1)), dm * a2 * (s2 * (1.0 - s2))

    da1, da2, dgi1, dgi2 = _mm([(dmo, w_out_f)], tb=True, name="out_proj_dx", outs=(BF16,) * 4, tn=512,
                               extras=[(a1, "mn"), (a2, "mn"), (g1_in, "mn"), (g2_in, "mn")], epi=merge_bwd_epi)
    dw_out = _mm([(merged, dmo)], ta=True, name="out_proj_dw", outs=(BF16,))
    dattn = _mm([(da1, w_bra)], tb=True, name="br_attn_dx", outs=(BF16,))
    dw_bra = _mm([(attn_o, da1)], ta=True, name="br_attn_dw", outs=(BF16,), split=w_br_attn.shape[-1])
    dsgu = _mm([(da2, w_brs)], tb=True, name="br_sgu_dx", outs=(BF16,))
    dw_brs = _mm([(sgu_o, da2)], ta=True, name="br_sgu_dw", outs=(BF16,), split=w_br_sgu.shape[-1])

    def sgu_bwd_fn(dso, u_in, v_in, ng, nb, ws, bst):
        u, xhat, rs, vnb = sgu_parts(u_in, v_in, ng, nb)
        dso = dso.astype(F32)
        lane = lax.broadcasted_iota(jnp.int32, (CH, LANES), 1)
        du, dvn, dws, dbs = [], [], [], jnp.zeros((CH, LANES), F32)
        for g in range(G):
            sl = slice(g * GD, (g + 1) * GD)
            wg = ws[g].astype(BF16)
            mixed = jnp.dot(wg, vnb[:, sl], preferred_element_type=F32) + bst[:, g:g + 1]
            du.append(dso[:, sl] * mixed)
            dmix = dso[:, sl] * u[:, sl]
            dmb = dmix.astype(BF16)
            dws.append(lax.dot_general(dmb, vnb[:, sl], (((1,), (1,)), ((), ())), preferred_element_type=F32))
            dbs = dbs + jnp.where(lane == g, jnp.sum(dmix, axis=1, keepdims=True), 0.0)
            dvn.append(lax.dot_general(wg, dmb, (((0,), (0,)), ((), ())), preferred_element_type=F32))
        du, dvn = jnp.concatenate(du, axis=1), jnp.concatenate(dvn, axis=1)
        dxh = dvn * ng
        dv = rs * (dxh - jnp.mean(dxh, axis=-1, keepdims=True) - xhat * jnp.mean(dxh * xhat, axis=-1, keepdims=True))
        return ((du * _gelu_grad(u_in.astype(F32)), dv * _gelu_grad(v_in.astype(F32))),
                (_colsum(dvn * xhat), _colsum(dvn), jnp.stack(dws), dbs))

    core = jnp.reshape(ac, (1,)).astype(jnp.int32)

    def dest_layout(dwf, i):
        K, Ns = big[i].shape
        if dwf.ndim == 2:
            dwf = dwf.reshape(K, 4, Ns).transpose(1, 0, 2) if col_sharded[i] else dwf.reshape(4, K, Ns)
        return dwf.reshape(4, 2, K // 2, Ns)

    def pair_sums(idx, g4, sib):
        return [_pair_add(g, s, core, name="rs_pair_add_" + tags[i]) for g, s, i in zip(g4, sib, idx)]

    early = [3, 4, 5, 6, 7]
    g4_early = [dest_layout(d, i) for d, i in zip([dw_bra, dw_brs, dw_out, dw_ffi, dw_ffo], early)]
    (du_in, dv_in), (d_sng, d_snb, d_ws, d_bs), sib_early = _rowwise(
        sgu_bwd_fn, [dsgu, u_in, v_in], [sgu_norm_g, sgu_norm_b, ws3, bs_t], [(SW, BF16), (SW, BF16)],
        [(1, SW), (1, SW), (G, CH, CH), (CH, LANES)], name="sgu_bwd", tm=CH, carry=_PairExchange(g4_early, "halves"))
    pair_early = pair_sums(early, g4_early, sib_early)
    dq_att, dk_att, dv_att, xchg_early = _attn_bwd(q_att, k_att, v_att, attn_o, lse, dattn, heads=H,
                                                   carry=_ChipExchange(pair_early, gather=False))

    def q_post_bwd_fn(dq, t, cos, slo, shi, g):
        outs, dg = [], jnp.zeros((1, HEAD_PAD), F32)
        for hd in range(H):
            sl = slice(hd * HEAD_PAD, (hd + 1) * HEAD_PAD)
            th = t[:, sl]
            r = _rms_stats(th, QK)
            tn = th * r
            dn = _rope_t(dq[:, sl] * scale, cos, slo, shi)
            dg = dg + _colsum(dn * tn)
            outs.append(_rms_bwd(dn * g, tn, r, QK))
        return (jnp.concatenate(outs, axis=1),), (dg,)

    (dq_raw,), (d_gq,) = _rowwise(q_post_bwd_fn, [dq_att, q_raw, cos_t, slo_t, shi_t], [gq_p],
                                  [(H * HEAD_PAD, BF16)], [(1, HEAD_PAD)], name="q_post_bwd")

    def k_post_bwd_fn(dk, dv, t, kvi, cos, slo, shi, g):
        kr = kvi[:, KVL:]
        outs, dg, dkr = [], jnp.zeros((1, HEAD_PAD), F32), jnp.zeros_like(kr)
        for hd in range(H):
            th = jnp.concatenate([t[:, hd * HEAD_PAD:hd * HEAD_PAD + NOPE], kr], axis=1)
            r = _rms_stats(th, QK)
            tn = th * r
            dn = _rope_t(dk[:, hd * HEAD_PAD:(hd + 1) * HEAD_PAD], cos, slo, shi)
            dg = dg + _colsum(dn * tn)
            dt = _rms_bwd(dn * g, tn, r, QK)
            dkr = dkr + dt[:, NOPE:]
            outs += [dt[:, :NOPE], dv[:, hd * VD:(hd + 1) * VD]]
        return (jnp.concatenate(outs, axis=1), dkr), (dg,)

    def reduced_halves(idx, xchg, pair):
        filled = [lax.dynamic_update_index_in_dim(t4, lax.dynamic_index_in_dim(pr, my_chip, 0, keepdims=False),
                                                  my_chip, 0) for t4, pr in zip(xchg, pair)]
        return [_sum_blocks(t4, name="rs_sum_" + tags[i], out_dtype=F32) for t4, i in zip(filled, idx)]

    red_early = reduced_halves(early, xchg_early, pair_early)
    (dkv_raw, dkrope), (d_gk,), other_early = _rowwise(
        k_post_bwd_fn, [dk_att, dv_att, kv_raw, kvin, cos_k, slo_k, shi_k], [gk_p],
        [(H * HEAD_PAD, BF16), (2 * ROPE_DIM, F32)], [(1, HEAD_PAD)], name="k_post_bwd",
        carry=_PairExchange(red_early, "gather"))

    dqn = _mm([(dq_raw, w_uq_p)], tb=True, name="q_up_dx", outs=(F32,))
    dw_uq_p = _mm([(qn, dq_raw)], ta=True, name="q_up_dw", outs=(BF16,))
    dkvn = _mm([(dkv_raw, w_ukv_f)], tb=True, name="kv_up_dx", outs=(F32,))
    dw_ukv = _mm([(kvn, dkv_raw)], ta=True, name="kv_up_dw", outs=(BF16,))

    def q_norm_bwd_fn(dn, t, g):
        r = _rms_stats(t, QL)
        tn = t * r
        return (_rms_bwd(dn * g, tn, r, QL),), (_colsum(dn * tn),)

    (dqc,), (d_qng,) = _rowwise(q_norm_bwd_fn, [dqn, qc], [q_norm_g], [(QL, BF16)], [(1, QL)], name="q_norm_bwd")

    def kv_norm_bwd_fn(dn, dkr, t, g):
        kvc = t[:, :KVL]
        r = _rms_stats(kvc, KVL)
        tn = kvc * r
        return (jnp.concatenate([_rms_bwd(dn * g, tn, r, KVL), dkr], axis=1),), (_colsum(dn * tn),)

    (dkvin,), (d_kvng,) = _rowwise(kv_norm_bwd_fn, [dkvn, dkrope, kvin], [kv_norm_g], [(KVP, BF16)], [(1, KVL)],
                                   name="kv_norm_bwd")
    dkvin_x, dkvin_c = dkvin[:N], dkvin[N:]

    dctx_h = _mm([(dkvin_c, w_kv)], tb=True, name="proj_kv_ctx_dx", outs=(F32,))
    dw_q = _mm([(h, dqc)], ta=True, name="proj_q_dw", outs=(BF16,))
    dw_kv = _mm([(h, dkvin_x), (ctx_h, dkvin_c)], ta=True, name="proj_kv_dw", outs=(BF16,))
    dw_u = _mm([(h, du_in)], ta=True, name="proj_u_dw", outs=(BF16,))
    dw_v = _mm([(h, dv_in)], ta=True, name="proj_v_dw", outs=(BF16,))
    dw_g1 = _mm([(h, dgi1)], ta=True, name="proj_g1_dw", outs=(BF16,))
    dw_g2 = _mm([(h, dgi2)], ta=True, name="proj_g2_dw", outs=(BF16,))

    dw_in_f = jnp.concatenate([dw_q, dw_kv[:, :KVL + ROPE_DIM], dw_u, dw_v, dw_g1, dw_g2], axis=1)
    dw_uq_f = dw_uq_p.reshape(QL, H, HEAD_PAD)[:, :, :QK].reshape(QL, H * QK)
    late = [0, 1, 2]
    g4_late = [dest_layout(d, i) for d, i in zip([dw_in_f, dw_uq_f, dw_ukv], late)]
    pair_late = pair_sums(late, g4_late, _exchange_alone(_PairExchange(g4_late, "halves"), name="rs_pair_late"))
    dh, xchg_late = _mm([(dqc, w_q), (dkvin_x, w_kv), (du_in, w_u), (dv_in, w_v), (dgi1, w_g1), (dgi2, w_g2)],
                        tb=True, name="proj_dx", outs=(F32,), tn=1024, tk=512,
                        carry=_ChipExchange(pair_late, gather=False))

    def norm1_bwd_fn(dhv, t, dres, g, sc):
        r = _rms_stats(t, D)
        tn = t * r
        dxg = dhv * (1.0 + sc)
        return (dres + _rms_bwd(dxg * g, tn, r, D),), (_colsum(dhv), _colsum(dhv * (tn * g)), _colsum(dxg * tn))

    (grad_x,), (dsh1, dsc1, dn1g_x) = _rowwise(norm1_bwd_fn, [dh, x2, dx1], [norm1_g, sc1], [(D, F32)], [(1, D)] * 3,
                                               name="norm1_bwd")
    _, (dsh1c, dsc1c, dn1g_c) = _rowwise(norm1_bwd_fn, [dctx_h, ctx2, jnp.zeros_like(ctx2)], [norm1_g, sc1c],
                                         [(D, F32)], [(1, D)] * 3, name="norm1_ctx_bwd")

    small = [dsh1, dsc1, dg1, dsh2, dsc2, dg2,
             dsh1c, dsc1c, dn1g_x, dn1g_c, d_qng, d_kvng, d_gq, d_gk, d_sng, d_snb, dn2g, loss_cols]
    small_sizes = [a.shape[1] for a in small]
    sm_row = jnp.concatenate(small, axis=1)
    sm_mat = jnp.concatenate([d_ws.reshape(G * CH, CH), d_bs], axis=0)
    row_all, mat_all = _all_gather8([sm_row, sm_mat], name="ag_small", in_vmem=True)
    row_sum = _sum_blocks(row_all, name="sum_small_rows", out_dtype=F32)
    mat_sum = _sum_blocks(mat_all, name="sum_small_mats", out_dtype=F32)
    dmod_rows = row_all[:, 0, :NMOD]
    (_, _, _, _, _, _, t_sh1c, t_sc1c, t_n1x, t_n1c, g_qng, g_kvng, t_gq, t_gk, g_sng, g_snb, g_n2g,
     t_loss) = _split_lanes(row_sum, small_sizes)
    g_ws, t_bs = mat_sum[:G * CH], mat_sum[G * CH:]
    dmodc_row = jnp.concatenate([t_sh1c, t_sc1c, jnp.zeros((1, NMOD - 2 * D), F32)], axis=1)
    dmod16 = jnp.concatenate([dmod_rows, dmodc_row, jnp.zeros((BF16_SUBLANES - 9, NMOD), F32)], axis=0)

    def small_fn(rows, n1x, n1c, lossv):
        return (), (_colsum(rows), n1x + n1c, jnp.sum(lossv, axis=1, keepdims=True))

    _, (g_bmod, g_n1g, loss11) = _rowwise(small_fn, [dmod16], [t_n1x, t_n1c, t_loss], [], [(1, NMOD), (1, D), (1, 1)],
                                          name="small_reduce", tm=16)
    dmod_loc = lax.dynamic_slice_in_dim(dmod16, my_chip * NM, NM, axis=1)
    g_wmod = _mm([(silu_c, dmod_loc)], ta=True, name="mod_dw", outs=(F32,), tn=512)
    dsilu_part = _mm([(dmod_loc, wm)], tb=True, name="mod_dx", outs=(F32,), tk=512)
    part_all = _all_gather8([dsilu_part[8:9]], name="ag_cctx", in_vmem=True)[0]

    def cctx_fn(parts, dsl):
        return (), ((parts[0:1] + parts[2:3] + parts[4:5] + parts[6:7]) * dsl,)

    _, (g_cctx,) = _rowwise(cctx_fn, [part_all[:, 0, :]], [dsilu_c[8:9]], [], [(1, D)], name="cctx_grad", tm=8)

    red_late = reduced_halves(late, xchg_late, pair_late)
    other_late = _exchange_alone(_PairExchange(red_late, "gather"), name="rs_halves_late")
    grad_halves = dict(zip(tags, zip(red_late + red_early, other_late + other_early)))

    mod_upd = _adamw(w_mod[0], g_wmod, m_w_mod[0], v_w_mod[0], name="adamw_w_mod")
    grads = dict(
        c_ctx=g_cctx.reshape(D), w_mod=g_wmod[None], b_mod=g_bmod, norm1_g=g_n1g,
        q_norm_g=g_qng, kv_norm_g=g_kvng, qk_norm_q=t_gq[:, :QK], qk_norm_k=t_gk[:, :QK], sgu_norm_g=g_sng,
        sgu_norm_b=g_snb, w_spatial=g_ws.reshape(w_spatial.shape), b_spatial=t_bs[:, :G].T[None], norm2_g=g_n2g)
    weights = dict(c_ctx=c_ctx, w_mod=w_mod, b_mod=b_mod, norm1_g=norm1_g, w_in=w_in, q_norm_g=q_norm_g,
                   kv_norm_g=kv_norm_g, w_uq=w_uq, w_ukv=w_ukv, qk_norm_q=qk_norm_q, qk_norm_k=qk_norm_k,
                   sgu_norm_g=sgu_norm_g, sgu_norm_b=sgu_norm_b, w_spatial=w_spatial, b_spatial=b_spatial,
                   w_br_attn=w_br_attn, w_br_sgu=w_br_sgu, w_out=w_out, norm2_g=norm2_g, w_ffn_in=w_ffn_in,
                   w_ffn_out=w_ffn_out)
    m_in = dict(c_ctx=m_c_ctx, w_mod=m_w_mod, b_mod=m_b_mod, norm1_g=m_norm1_g, w_in=m_w_in, q_norm_g=m_q_norm_g,
                kv_norm_g=m_kv_norm_g, w_uq=m_w_uq, w_ukv=m_w_ukv, qk_norm_q=m_qk_norm_q, qk_norm_k=m_qk_norm_k,
                sgu_norm_g=m_sgu_norm_g, sgu_norm_b=m_sgu_norm_b, w_spatial=m_w_spatial, b_spatial=m_b_spatial,
                w_br_attn=m_w_br_attn, w_br_sgu=m_w_br_sgu, w_out=m_w_out, norm2_g=m_norm2_g, w_ffn_in=m_w_ffn_in,
                w_ffn_out=m_w_ffn_out)
    v_in_ = dict(c_ctx=v_c_ctx, w_mod=v_w_mod, b_mod=v_b_mod, norm1_g=v_norm1_g, w_in=v_w_in, q_norm_g=v_q_norm_g,
                 kv_norm_g=v_kv_norm_g, w_uq=v_w_uq, w_ukv=v_w_ukv, qk_norm_q=v_qk_norm_q, qk_norm_k=v_qk_norm_k,
                 sgu_norm_g=v_sgu_norm_g, sgu_norm_b=v_sgu_norm_b, w_spatial=v_w_spatial, b_spatial=v_b_spatial,
                 w_br_attn=v_w_br_attn, w_br_sgu=v_w_br_sgu, w_out=v_w_out, norm2_g=v_norm2_g, w_ffn_in=v_w_ffn_in,
                 w_ffn_out=v_w_ffn_out)
    names = list(weights)
    big_names = ("w_mod", "w_in", "w_uq", "w_ukv", "w_br_attn", "w_br_sgu", "w_out", "w_ffn_in", "w_ffn_out")
    out_g, out_d, out_m, out_v = {}, {}, {}, {}
    out_g["w_mod"] = grads["w_mod"]
    out_d["w_mod"], out_m["w_mod"], out_v["w_mod"] = [t[None] for t in mod_upd[:3]]
    for nm in big_names[1:]:
        res = _adamw_halves(weights[nm][0], *grad_halves[nm], m_in[nm][0], v_in_[nm][0], core, name="adamw_" + nm)
        out_g[nm], out_d[nm], out_m[nm], out_v[nm] = [t[None] for t in res]
    row_names = [nm for nm in names if nm not in big_names and nm not in ("w_spatial", "b_spatial")]
    widths = [-(-weights[nm].size // LANES) * LANES for nm in row_names]

    def as_row(d):
        return jnp.concatenate([jnp.pad(d[nm].reshape(1, -1), ((0, 0), (0, wd - d[nm].size)))
                                for nm, wd in zip(row_names, widths)], axis=1)

    def as_mat(d):
        return jnp.concatenate([d["w_spatial"].reshape(G * CH, CH), d["b_spatial"].reshape(G, CH)], axis=0)

    row_res = _adamw(as_row(weights), as_row(grads), as_row(m_in), as_row(v_in_), name="adamw_rows")
    mat_res = _adamw(as_mat(weights), as_mat(grads), as_mat(m_in), as_mat(v_in_), name="adamw_spatial")
    for tgt, row, mat in zip((out_d, out_m, out_v), row_res, mat_res):
        for nm, seg in zip(row_names, _split_lanes(row, widths)):
            tgt[nm] = seg[:, :weights[nm].size].reshape(weights[nm].shape)
        tgt["w_spatial"] = mat[:G * CH].reshape(w_spatial.shape)
        tgt["b_spatial"] = mat[G * CH:].reshape(b_spatial.shape)
    for nm in row_names + ["w_spatial", "b_spatial"]:
        out_g[nm] = grads[nm].reshape(weights[nm].shape)

    loss = loss11.reshape(())
    return (loss, grad_x[None], *[out_g[n] for n in names], *[out_d[n] for n in names],
            *[out_m[n] for n in names], *[out_v[n] for n in names])
```

```python
import math

import jax
import jax.numpy as jnp
from jax import lax
from jax.experimental import pallas as pl
from jax.experimental.pallas import tpu as pltpu

F32, BF16 = jnp.float32, jnp.bfloat16
MESH = pl.DeviceIdType.MESH

LANES = 128
F32_SUBLANES = 8
BF16_SUBLANES = 16
MXU_DIM = 256
VMEM_LIMIT_BYTES = 56 * 1024 * 1024

EPS = 1e-6
ROPE_DIM = 64
ROPE_THETA = 10000.0
GRID_W = 64
HEAD_PAD = 256
ADAM_LR, ADAM_B1, ADAM_B2, ADAM_EPS, ADAM_WD, ADAM_STEP = 0.001, 0.9, 0.999, 1e-08, 0.01, 10


def _tile(dim, pref, align=LANES):
    if dim <= pref:
        return dim
    t = (pref // align) * align
    while t >= align:
        if dim % t == 0:
            return t
        t -= align
    return dim


def _params(sem=None):
    return pltpu.CompilerParams(dimension_semantics=sem, vmem_limit_bytes=VMEM_LIMIT_BYTES)


def _sds(shape, dtype):
    return jax.ShapeDtypeStruct(tuple(shape), dtype)


def _mm(pairs, *, name, ta=False, tb=False, outs=(F32,), tm=1024, tn=1024, tk=2048, extras=(), epi=None,
        split=None, into=None, carry=None, col_sums=0):
    dual = len(pairs[0]) == 3
    a0, b0 = pairs[0][0], pairs[0][1]
    M = a0.shape[1] if ta else a0.shape[0]
    N = b0.shape[0] if tb else b0.shape[1]
    tm, tn = _tile(M, tm), _tile(N if split is None else split, tn)
    ks = [(p[0].shape[0] if ta else p[0].shape[1]) for p in pairs]
    tks = [_tile(k, tk) for k in ks]
    nks = [k // t for k, t in zip(ks, tks)]
    offs = [sum(nks[:i]) for i in range(len(pairs))]
    nk_total = sum(nks)
    single = len(pairs) == 1

    def kidx(kk, p):
        return kk if single else jnp.clip(kk - offs[p], 0, nks[p] - 1)

    in_specs, operands = [], []
    for p, pr in enumerate(pairs):
        if ta:
            in_specs.append(pl.BlockSpec((tks[p], tm), lambda i, j, kk, p=p: (kidx(kk, p), i)))
        else:
            in_specs.append(pl.BlockSpec((tm, tks[p]), lambda i, j, kk, p=p: (i, kidx(kk, p))))
        operands.append(pr[0])
        for b in pr[1:]:
            if tb:
                in_specs.append(pl.BlockSpec((tn, tks[p]), lambda i, j, kk, p=p: (j, kidx(kk, p))))
            else:
                in_specs.append(pl.BlockSpec((tks[p], tn), lambda i, j, kk, p=p: (kidx(kk, p), j)))
            operands.append(b)
    for arr, kind in extras:
        if kind == "mn":
            in_specs.append(pl.BlockSpec((tm, tn), lambda i, j, kk: (i, j)))
        else:
            in_specs.append(pl.BlockSpec((1, tn), lambda i, j, kk: (0, j)))
        operands.append(arr)
    n_in = len(operands)
    n_ex = len(extras)
    per = 3 if dual else 2
    dims = (((0 if ta else 1,), (1 if tb else 0,)), ((), ()))

    n_acc = 2 if dual else 1

    def products(ins, p):
        a = ins[per * p][...].astype(BF16)
        return [lax.dot_general(a, ins[per * p + 1 + q][...].astype(BF16), dims, preferred_element_type=F32)
                for q in range(n_acc)]

    def finish(ins, out_refs, acc_vals):
        vals = acc_vals + [r[...] for r in ins[n_in - n_ex:]]
        res = epi(*vals) if epi is not None else (vals[0],)
        for o, r in zip(out_refs, res):
            o[...] = jnp.broadcast_to(r, o.shape).astype(o.dtype)

    out_specs = [pl.BlockSpec((tm, tn), lambda i, j, kk: (i, j)) for _ in outs]
    out_specs += [pl.BlockSpec((None, F32_SUBLANES, tn), lambda i, j, kk: (i, 0, j)) for _ in range(col_sums)]
    out_shape = [_sds((M, N), d) for d in outs] + [_sds((M // tm, F32_SUBLANES, N), F32) for _ in range(col_sums)]
    aliases = {}
    n_alias = 0
    if split is not None:
        nps = split // tn
        lead = 0 if into is None else into[1]
        out_specs = [pl.BlockSpec((None, tm, tn), lambda i, j, kk: (j // nps + lead, i, j % nps))]
        out_shape = [_sds((N // split if into is None else into[0].shape[0], M, split), outs[0])]
        if into is not None:
            in_specs.append(pl.BlockSpec(memory_space=pl.ANY))
            operands.append(into[0])
            aliases, n_alias = {n_in: 0}, 1

    grid = (M // tm, N // tn, nk_total)
    n_out = len(outs) + col_sums

    def at_step(first):
        ids = [pl.program_id(d) for d in range(3)]
        cond = None
        for d, g in zip(ids, grid):
            t = d == (0 if first else g - 1)
            cond = t if cond is None else cond & t
        return cond

    def body(*refs):
        ins, out_refs, accs, start, wait = _split_refs(refs, n_in + n_alias, n_out, carry)
        ins = ins[:n_in]
        if carry is not None:
            pl.when(at_step(True))(start)
        if nk_total == 1:
            finish(ins, out_refs, products(ins, 0))
        else:
            kk = pl.program_id(2)

            @pl.when(kk == 0)
            def _():
                for acc, v in zip(accs, products(ins, 0)):
                    acc[...] = v

            for p in range(len(pairs)):
                lo = max(offs[p], 1)

                @pl.when((kk >= lo) & (kk < offs[p] + nks[p]))
                def _(p=p):
                    for acc, v in zip(accs, products(ins, p)):
                        acc[...] += v

            @pl.when(kk == nk_total - 1)
            def _():
                finish(ins, out_refs, [acc[...] for acc in accs])
        if carry is not None:
            pl.when(at_step(False))(wait)

    ex = carry
    res = pl.pallas_call(
        body, name=name, grid=grid, in_specs=in_specs + ([] if ex is None else ex.in_specs),
        out_specs=out_specs + ([] if ex is None else ex.out_specs),
        out_shape=out_shape + ([] if ex is None else ex.out_shape), input_output_aliases=aliases,
        scratch_shapes=[pltpu.VMEM((tm, tn), F32) for _ in range(n_acc if nk_total > 1 else 0)]
        + ([] if ex is None else ex.scratch),
        compiler_params=_params(("arbitrary",) * 3 if ex is not None else ("parallel", "parallel", "arbitrary")),
    )(*operands, *([] if ex is None else ex.xs))
    if ex is not None:
        return (res[0] if n_out == 1 else res[:n_out]), list(res[n_out:])
    return res[0] if n_out == 1 else res


def _rowwise(fn, rows, vecs, out_rows, out_accs=(), *, name, tm=256, tc=None, carry=None):
    M = rows[0].shape[0]
    tm = _tile(M, tm, BF16_SUBLANES)
    nrow = M // tm
    C = rows[0].shape[1]
    ncol = 1 if tc is None else C // _tile(C, tc)
    tcol = None if tc is None else _tile(C, tc)

    def colwise(shape):
        return tc is not None and len(shape) == 2 and shape[0] == 1 and shape[1] == C

    def vspec(shape):
        if colwise(shape):
            return pl.BlockSpec((1, tcol), lambda j, i: (0, j))
        return pl.BlockSpec(tuple(shape), lambda j, i, n=len(shape): (0,) * n)

    def rspec(width):
        if tc is None:
            return pl.BlockSpec((tm, width), lambda j, i: (i, 0))
        return pl.BlockSpec((tm, tcol), lambda j, i: (i, j))

    in_specs = [rspec(r.shape[1]) for r in rows] + [vspec(v.shape) for v in vecs]
    out_specs = [rspec(c) for c, _ in out_rows] + [vspec(s) for s in out_accs]
    out_shape = [_sds((M, c), d) for c, d in out_rows] + [_sds(s, F32) for s in out_accs]
    n_in, n_or = len(rows) + len(vecs), len(out_rows)

    n_out = n_or + len(out_accs)
    ex = carry

    def body(*refs):
        ins, outs, _, start, wait = _split_refs(refs, n_in, n_out, ex)
        o_rows, o_accs = outs[:n_or], outs[n_or:]
        if ex is not None:
            pl.when((pl.program_id(0) == 0) & (pl.program_id(1) == 0))(start)
        r_out, a_out = fn(*[r[...] for r in ins])
        for o, r in zip(o_rows, r_out):
            o[...] = r.astype(o.dtype)
        i = pl.program_id(1)

        @pl.when(i == 0)
        def _():
            for o, a in zip(o_accs, a_out):
                o[...] = a

        @pl.when(i > 0)
        def _():
            for o, a in zip(o_accs, a_out):
                o[...] += a

        if ex is not None:
            pl.when((pl.program_id(0) == ncol - 1) & (pl.program_id(1) == nrow - 1))(wait)

    res = pl.pallas_call(
        body, name=name, grid=(ncol, nrow), in_specs=in_specs + ([] if ex is None else ex.in_specs),
        out_specs=out_specs + ([] if ex is None else ex.out_specs),
        out_shape=out_shape + ([] if ex is None else ex.out_shape),
        scratch_shapes=[] if ex is None else ex.scratch,
        compiler_params=_params(("arbitrary", "arbitrary") if ex is not None else ("parallel", "arbitrary")),
    )(*rows, *vecs, *([] if ex is None else ex.xs))
    if ex is not None:
        return res[:n_or], res[n_or:n_out], list(res[n_out:])
    return res[:n_or], res[n_or:]


def _colsum(t):
    return jnp.sum(t, axis=0, keepdims=True)


def _gelu(t):
    return 0.5 * t * (1.0 + lax.erf(t * math.sqrt(0.5)))


def _gelu_grad(t):
    return 0.5 * (1.0 + lax.erf(t * math.sqrt(0.5))) + t * jnp.exp(-0.5 * t * t) * (1.0 / math.sqrt(2.0 * math.pi))


def _sigmoid(t):
    return 1.0 / (1.0 + jnp.exp(-t))


def _rms_stats(t, width):
    return lax.rsqrt(jnp.sum(t * t, axis=-1, keepdims=True) * (1.0 / width) + EPS)


def _rms_bwd(dn, tn, r, width):
    return r * (dn - tn * (jnp.sum(dn * tn, axis=-1, keepdims=True) * (1.0 / width)))


def _place():
    return lax.axis_index("x"), lax.axis_index("y"), lax.axis_index("c")


class _ChipExchange:
    def __init__(self, xs, gather):
        self.xs, self.gather, self.n = list(xs), gather, len(xs)
        self.in_specs = [pl.BlockSpec(memory_space=pl.ANY)] * self.n
        self.out_specs = [pl.BlockSpec(memory_space=pl.ANY)] * self.n
        self.out_shape = [_sds((4,) + (x.shape if gather else x.shape[1:]), x.dtype) for x in self.xs]
        self.scratch = [pltpu.SemaphoreType.DMA((self.n, 3)), pltpu.SemaphoreType.DMA((self.n, 3))]

    def bind(self, x_refs, out_refs, send_sems, recv_sems):
        x, y, c = _place()
        p = 2 * x + y
        chips = [(1 - x, y), (x, 1 - y), (1 - x, 1 - y)]

        def copy(w, k, outgoing):
            qx, qy = chips[k]
            there = 2 * qx + qy
            if self.gather:
                src = x_refs[w]
            else:
                src = x_refs[w].at[there if outgoing else p]
            return pltpu.make_async_remote_copy(
                src_ref=src, dst_ref=out_refs[w].at[p if outgoing else there], send_sem=send_sems.at[w, k],
                recv_sem=recv_sems.at[w, k], device_id=(qx, qy, c), device_id_type=MESH)

        def start():
            for w in range(self.n):
                for k in range(3):
                    copy(w, k, True).start()

        def wait():
            for w in range(self.n):
                for k in range(3):
                    copy(w, k, False).wait_recv()
            for w in range(self.n):
                for k in range(3):
                    copy(w, k, True).wait_send()

        return start, wait


class _PairExchange:
    def __init__(self, xs, mode):
        self.xs, self.mode, self.n = list(xs), mode, len(xs)
        self.in_specs = [pl.BlockSpec(memory_space=pl.ANY)] * self.n
        self.out_specs = [pl.BlockSpec(memory_space=pl.ANY)] * self.n
        shape = {"halves": lambda s: (4,) + s[2:], "forward": lambda s: s, "gather": lambda s: (2,) + s}[mode]
        self.out_shape = [_sds(shape(x.shape), x.dtype) for x in self.xs]
        self.scratch = [pltpu.SemaphoreType.DMA((self.n, 3)), pltpu.SemaphoreType.DMA((self.n, 3))]

    def bind(self, x_refs, out_refs, send_sems, recv_sems):
        x, y, c = _place()
        chips = [(1 - x, y), (x, 1 - y), (1 - x, 1 - y)]

        def copy(w, src, dst, k):
            return pltpu.make_async_remote_copy(src_ref=src, dst_ref=dst, send_sem=send_sems.at[w, k],
                                                recv_sem=recv_sems.at[w, k], device_id=(x, y, 1 - c),
                                                device_id_type=MESH)

        def start():
            for w, (xr, orf) in enumerate(zip(x_refs, out_refs)):
                if self.mode == "halves":
                    for q in range(4):
                        copy(w, xr.at[q, 1 - c], orf.at[q], 0).start()
                elif self.mode == "forward":
                    for k, (qx, qy) in enumerate(chips):
                        copy(w, xr.at[2 * qx + qy], orf.at[2 * qx + qy], k).start()
                else:
                    copy(w, xr, orf.at[c], 0).start()

        def wait():
            for w, (xr, orf) in enumerate(zip(x_refs, out_refs)):
                if self.mode == "halves":
                    copy(w, orf, orf, 0).wait()
                elif self.mode == "forward":
                    for k, (qx, qy) in enumerate(chips):
                        copy(w, xr.at[2 * qx + qy], orf.at[2 * qx + qy], k).wait()
                else:
                    cp = copy(w, xr, orf.at[1 - c], 0)
                    cp.wait_recv()
                    cp.wait_send()

        return start, wait


def _split_refs(refs, n_in, n_out, ex):
    ne = 0 if ex is None else ex.n
    ins, xin = refs[:n_in], refs[n_in:n_in + ne]
    outs, xout = refs[n_in + ne:n_in + ne + n_out], refs[n_in + ne + n_out:n_in + 2 * ne + n_out]
    rest = refs[n_in + 2 * ne + n_out:]
    if ex is None:
        return ins, outs, rest, None, None
    start, wait = ex.bind(xin, xout, rest[-2], rest[-1])
    return ins, outs, rest[:-2], start, wait


def _attn_fwd(q, k, v, *, heads, tq=512, carry=None):
    N, M = q.shape[0], k.shape[0]
    tq = _tile(N, tq)
    sub = _tile(tq, MXU_DIM)
    vd = v.shape[1] // heads
    nq = N // tq

    def body(*refs):
        (q_ref, k_ref, v_ref), (o_ref, lse_ref), _, start, wait = _split_refs(refs, 3, 2, carry)
        if carry is not None:
            pl.when((pl.program_id(0) == 0) & (pl.program_id(1) == 0))(start)
        for sb in range(tq // sub):
            rows = pl.ds(sb * sub, sub)
            s = lax.dot_general(q_ref[rows, :], k_ref[...], (((1,), (1,)), ((), ())), preferred_element_type=F32)
            m = jnp.max(s, axis=-1, keepdims=True)
            p = jnp.exp(s - m)
            l = jnp.sum(p, axis=-1, keepdims=True)
            o = jnp.dot(p.astype(BF16), v_ref[...], preferred_element_type=F32) / l
            o_ref[rows, :] = o.astype(o_ref.dtype)
            lse_ref[rows, :] = jnp.broadcast_to(m + jnp.log(l), (sub, vd))
        if carry is not None:
            pl.when((pl.program_id(0) == heads - 1) & (pl.program_id(1) == nq - 1))(wait)

    ex = carry
    res = pl.pallas_call(
        body, name="attn_fwd", grid=(heads, nq),
        in_specs=[pl.BlockSpec((tq, HEAD_PAD), lambda h, i: (i, h)),
                  pl.BlockSpec((M, HEAD_PAD), lambda h, i: (0, h)),
                  pl.BlockSpec((M, vd), lambda h, i: (0, h))] + ([] if ex is None else ex.in_specs),
        out_specs=[pl.BlockSpec((tq, vd), lambda h, i: (i, h)),
                   pl.BlockSpec((tq, vd), lambda h, i: (i, h))] + ([] if ex is None else ex.out_specs),
        out_shape=[_sds((N, heads * vd), BF16), _sds((N, heads * vd), F32)] + ([] if ex is None else ex.out_shape),
        scratch_shapes=[] if ex is None else ex.scratch,
        compiler_params=_params(("arbitrary", "arbitrary")),
    )(q, k, v, *([] if ex is None else ex.xs))
    return res[0], res[1], list(res[2:])


def _attn_bwd(q, k, v, o, lse, do, *, heads, tq=512, carry=None):
    N, M = q.shape[0], k.shape[0]
    tq = _tile(N, tq)
    vd = v.shape[1] // heads
    nq = N // tq
    sub = _tile(tq, MXU_DIM)
    nt = (((1,), (1,)), ((), ()))
    tn = (((0,), (0,)), ((), ()))

    def body(*refs):
        (q_ref, k_ref, v_ref, o_ref, lse_ref, do_ref), (dq_ref, dk_ref, dv_ref), _, start, wait = _split_refs(
            refs, 6, 3, carry)
        if carry is not None:
            pl.when((pl.program_id(0) == 0) & (pl.program_id(1) == 0))(start)
        i = pl.program_id(1)
        kb, vb = k_ref[...], v_ref[...]
        parts = []
        for sb in range(tq // sub):
            rows = pl.ds(sb * sub, sub)
            qb, dob = q_ref[rows, :], do_ref[rows, :]
            delta = jnp.sum(dob.astype(F32) * o_ref[rows, :].astype(F32), axis=-1, keepdims=True)
            s = lax.dot_general(qb, kb, nt, preferred_element_type=F32)
            p = jnp.exp(s - lse_ref[rows, :][:, :1])
            dp = lax.dot_general(dob, vb, nt, preferred_element_type=F32)
            ds = (p * (dp - delta)).astype(BF16)
            dq_ref[rows, :] = jnp.dot(ds, kb, preferred_element_type=F32)
            parts.append((lax.dot_general(ds, qb, tn, preferred_element_type=F32),
                          lax.dot_general(p.astype(BF16), dob, tn, preferred_element_type=F32)))

        @pl.when(i == 0)
        def _():
            dk_ref[...] = parts[0][0]
            dv_ref[...] = parts[0][1]

        @pl.when(i > 0)
        def _():
            dk_ref[...] += parts[0][0]
            dv_ref[...] += parts[0][1]

        for dk_part, dv_part in parts[1:]:
            dk_ref[...] += dk_part
            dv_ref[...] += dv_part

        if carry is not None:
            pl.when((pl.program_id(0) == heads - 1) & (pl.program_id(1) == nq - 1))(wait)

    ex = carry
    res = pl.pallas_call(
        body, name="attn_bwd", grid=(heads, nq),
        in_specs=[pl.BlockSpec((tq, HEAD_PAD), lambda h, i: (i, h)),
                  pl.BlockSpec((M, HEAD_PAD), lambda h, i: (0, h)),
                  pl.BlockSpec((M, vd), lambda h, i: (0, h)),
                  pl.BlockSpec((tq, vd), lambda h, i: (i, h)),
                  pl.BlockSpec((tq, vd), lambda h, i: (i, h)),
                  pl.BlockSpec((tq, vd), lambda h, i: (i, h))] + ([] if ex is None else ex.in_specs),
        out_specs=[pl.BlockSpec((tq, HEAD_PAD), lambda h, i: (i, h)),
                   pl.BlockSpec((M, HEAD_PAD), lambda h, i: (0, h)),
                   pl.BlockSpec((M, vd), lambda h, i: (0, h))] + ([] if ex is None else ex.out_specs),
        out_shape=[_sds((N, heads * HEAD_PAD), F32), _sds((M, heads * HEAD_PAD), F32),
                   _sds((M, heads * vd), F32)] + ([] if ex is None else ex.out_shape),
        scratch_shapes=[] if ex is None else ex.scratch,
        compiler_params=_params(("arbitrary", "arbitrary")),
    )(q, k, v, o, lse, do, *([] if ex is None else ex.xs))
    return res[0], res[1], res[2], list(res[3:])


def _comm_call(body, xs, out_shapes, n_sems, name, in_vmem):
    space = pltpu.VMEM if in_vmem else pl.ANY
    n = len(xs)

    def wrapped(*refs):
        body(refs[:n], refs[n:2 * n], *refs[2 * n:])

    return pl.pallas_call(
        wrapped, name=name, out_shape=list(out_shapes),
        in_specs=[pl.BlockSpec(memory_space=space)] * n, out_specs=[pl.BlockSpec(memory_space=space)] * n,
        scratch_shapes=[pltpu.SemaphoreType.DMA((n, n_sems)), pltpu.SemaphoreType.DMA((n, n_sems)),
                        pltpu.SemaphoreType.DMA((n,))],
        compiler_params=pltpu.CompilerParams(vmem_limit_bytes=VMEM_LIMIT_BYTES),
    )(*xs)


def _all_gather8(blks, *, name, in_vmem):
    def body(x_refs, out_refs, send_sems, recv_sems, local_sems):
        x, y, c = _place()
        me, sibling = (x, y, c), (x, y, 1 - c)
        chips = [(1 - x, y), (x, 1 - y), (1 - x, 1 - y)]
        waits = []
        for w, (x_ref, out_ref) in enumerate(zip(x_refs, out_refs)):
            def slot(px, py, pc, out_ref=out_ref):
                return out_ref.at[4 * px + 2 * py + pc]

            def copy(k, block, to, src=None, w=w, slot=slot):
                return pltpu.make_async_remote_copy(
                    src_ref=slot(*block) if src is None else src, dst_ref=slot(*block),
                    send_sem=send_sems.at[w, k], recv_sem=recv_sems.at[w, k], device_id=to, device_id_type=MESH)

            mine = pltpu.make_async_copy(x_ref, slot(*me), local_sems.at[w])
            mine.start()
            first = [copy(0, me, sibling, src=x_ref)]
            first += [copy(1 + j, me, (*chip, c), src=x_ref) for j, chip in enumerate(chips)]
            for cp in first:
                cp.start()
            waits.append((copy, mine, first))
        for copy, mine, first in waits:
            passed = [copy(4 + j, (*chip, c), sibling) for j, chip in enumerate(chips)]
            for j, chip in enumerate(chips):
                copy(1 + j, (*chip, c), me).wait_recv()
                passed[j].start()
            copy(0, sibling, me).wait_recv()
            for j, chip in enumerate(chips):
                copy(4 + j, (*chip, 1 - c), me).wait_recv()
            for cp in first + passed:
                cp.wait_send()
            mine.wait()

    return _comm_call(body, blks, [_sds((8,) + b.shape, b.dtype) for b in blks], 7, name, in_vmem)


def _gather_others(blks, *, name):
    def body(x_refs, out_refs, send_sems, recv_sems, local_sems):
        x, y, c = _place()
        own, xn, yn, dg = (x, y), (1 - x, y), (x, 1 - y), (1 - x, 1 - y)

        def slot(w, chip, core):
            return out_refs[w].at[4 * chip[0] + 2 * chip[1] + core]

        def cp(w, k, src, dst, chip, core):
            return pltpu.make_async_remote_copy(src_ref=src, dst_ref=dst, send_sem=send_sems.at[w, k],
                                                recv_sem=recv_sems.at[w, k], device_id=(*chip, core),
                                                device_id_type=MESH)

        def halves(w):
            h = x_refs[w].shape[0] // 2
            return pl.ds(0, h), pl.ds(h, h)

        sends = []
        for w, x_ref in enumerate(x_refs):
            sends += [cp(w, 0, x_ref, slot(w, own, c), xn, c), cp(w, 1, x_ref, slot(w, own, c), yn, c)]
        for s in sends:
            s.start()
        for w, x_ref in enumerate(x_refs):
            lo, hi = halves(w)
            cp(w, 1, x_ref, slot(w, yn, c), yn, c).wait_recv()
            passed = [cp(w, 2, slot(w, yn, c).at[lo], slot(w, yn, c).at[lo], xn, c),
                      cp(w, 4, slot(w, yn, c), slot(w, yn, c), own, 1 - c)]
            cp(w, 0, x_ref, slot(w, xn, c), xn, c).wait_recv()
            passed += [cp(w, 3, slot(w, xn, c).at[hi], slot(w, xn, c).at[hi], yn, c),
                       cp(w, 5, slot(w, xn, c), slot(w, xn, c), own, 1 - c)]
            for s in passed:
                s.start()
            sends += passed
        for w in range(len(x_refs)):
            lo, hi = halves(w)
            cp(w, 2, slot(w, dg, c).at[lo], slot(w, dg, c).at[lo], xn, c).wait_recv()
            cp(w, 3, slot(w, dg, c).at[hi], slot(w, dg, c).at[hi], yn, c).wait_recv()
            passed = [cp(w, 6, slot(w, dg, c), slot(w, dg, c), own, 1 - c)]
            passed[0].start()
            sends += passed
        for w in range(len(x_refs)):
            cp(w, 4, slot(w, yn, c), slot(w, yn, 1 - c), own, 1 - c).wait_recv()
            cp(w, 5, slot(w, xn, c), slot(w, xn, 1 - c), own, 1 - c).wait_recv()
            cp(w, 6, slot(w, dg, c), slot(w, dg, 1 - c), own, 1 - c).wait_recv()
        for s in sends:
            s.wait_send()

    return list(_comm_call(body, blks, [_sds((8,) + b.shape, b.dtype) for b in blks], 7, name, False))


def _exchange_alone(ex, *, name):
    def body(x_refs, out_refs, send_sems, recv_sems, local_sems):
        start, wait = ex.bind(x_refs, out_refs, send_sems, recv_sems)
        start()
        wait()

    return list(_comm_call(body, ex.xs, ex.out_shape, 3, name, False))


def _block_rows(rows, row_bytes, target=1 << 21, align=BF16_SUBLANES):
    return _tile(rows, max(align, target // row_bytes // align * align), align)


def _sum_blocks(buf, *, name, out_dtype):
    B, R, C = buf.shape
    tm = _block_rows(R, B * C * buf.dtype.itemsize)

    def body(x_ref, o_ref):
        acc = x_ref[0].astype(F32)
        for b in range(1, B):
            acc = acc + x_ref[b].astype(F32)
        o_ref[...] = acc.astype(o_ref.dtype)

    return pl.pallas_call(
        body, name=name, grid=(R // tm,), in_specs=[pl.BlockSpec((B, tm, C), lambda i: (0, i, 0))],
        out_specs=pl.BlockSpec((tm, C), lambda i: (i, 0)), out_shape=_sds((R, C), out_dtype),
        compiler_params=_params(("parallel",)),
    )(buf)


def _sum_chips(received, sent, chip, *, name):
    _, R, C = received.shape
    tm = _block_rows(R, 5 * C * received.dtype.itemsize)

    def body(chip_ref, r0, r1, r2, r3, own_ref, o_ref):
        acc = None
        for q, r in enumerate((r0, r1, r2, r3)):
            term = jnp.where(q == chip_ref[0], own_ref[...], r[...]).astype(F32)
            acc = term if acc is None else acc + term
        o_ref[...] = acc

    def slot(q):
        return pl.BlockSpec((None, tm, C), lambda i, ch, q=q: (jnp.where(q == ch[0], (q + 1) % 4, q), i, 0))

    return pl.pallas_call(
        body, name=name, out_shape=_sds((R, C), F32),
        grid_spec=pltpu.PrefetchScalarGridSpec(
            num_scalar_prefetch=1, grid=(R // tm,),
            in_specs=[slot(0), slot(1), slot(2), slot(3), pl.BlockSpec((None, tm, C), lambda i, ch: (ch[0], i, 0))],
            out_specs=pl.BlockSpec((tm, C), lambda i, ch: (i, 0))),
        compiler_params=_params(("arbitrary",)),
    )(chip, received, received, received, received, sent)


def _pair_add(mine, theirs, core, *, name):
    _, _, R, C = mine.shape
    tm = _block_rows(R, C * 2)

    def body(core_ref, a_ref, b_ref, o_ref):
        o_ref[...] = (a_ref[...].astype(F32) + b_ref[...].astype(F32)).astype(o_ref.dtype)

    return pl.pallas_call(
        body, name=name, out_shape=_sds(theirs.shape, BF16),
        grid_spec=pltpu.PrefetchScalarGridSpec(
            num_scalar_prefetch=1, grid=(4, R // tm),
            in_specs=[pl.BlockSpec((None, None, tm, C), lambda q, i, core_ref: (q, core_ref[0], i, 0)),
                      pl.BlockSpec((None, tm, C), lambda q, i, core_ref: (q, i, 0))],
            out_specs=pl.BlockSpec((None, tm, C), lambda q, i, core_ref: (q, i, 0))),
        compiler_params=_params(("parallel", "parallel")),
    )(core, mine, theirs)


def _assemble(gathered, own, chip, *, name, transpose):
    _, K, Ns = gathered.shape
    tm = _block_rows(K, Ns * 4)

    def body(chip_ref, g_ref, own_ref, o_ref):
        q = pl.program_id(0)

        @pl.when(q == chip_ref[0])
        def _():
            o_ref[...] = own_ref[...].astype(BF16)

        @pl.when(q != chip_ref[0])
        def _():
            o_ref[...] = g_ref[...]

    if transpose:
        out_spec = pl.BlockSpec((tm, Ns), lambda q, i, ch: (i, q))
        out_shape = _sds((K, 4 * Ns), BF16)
    else:
        out_spec = pl.BlockSpec((None, tm, Ns), lambda q, i, ch: (q, i, 0))
        out_shape = _sds((4, K, Ns), BF16)
    return pl.pallas_call(
        body, name=name, out_shape=out_shape,
        grid_spec=pltpu.PrefetchScalarGridSpec(
            num_scalar_prefetch=1, grid=(4, K // tm),
            in_specs=[pl.BlockSpec((None, tm, Ns), lambda q, i, ch: (jnp.where(q == ch[0], (q + 1) % 4, q), i, 0)),
                      pl.BlockSpec((tm, Ns), lambda q, i, ch: (jnp.where(q == ch[0], i, 0), 0))],
            out_specs=out_spec),
        compiler_params=_params(("arbitrary", "arbitrary")),
    )(chip, gathered, own)


def _assemble_halves(mine, theirs, own, place, *, name, transpose):
    _, K2, Ns = mine.shape
    tm = _block_rows(K2, Ns * 4, target=1 << 22)
    nb = K2 // tm

    def body(place_ref, m_ref, t_ref, own_ref, o_ref):
        q, hb = pl.program_id(0), pl.program_id(1)
        is_own = q == place_ref[0]
        is_mine = hb == place_ref[1]

        @pl.when(is_own)
        def _():
            o_ref[...] = own_ref[...].astype(BF16)

        @pl.when(jnp.logical_not(is_own) & is_mine)
        def _():
            o_ref[...] = m_ref[...]

        @pl.when(jnp.logical_not(is_own) & jnp.logical_not(is_mine))
        def _():
            o_ref[...] = t_ref[...]

    def other(q, pr):
        return jnp.where(q == pr[0], (q + 1) % 4, q)

    if transpose:
        out_spec = pl.BlockSpec((tm, Ns), lambda q, hb, i, pr: (hb * nb + i, q))
        out_shape = _sds((2 * K2, 4 * Ns), BF16)
    else:
        out_spec = pl.BlockSpec((None, tm, Ns), lambda q, hb, i, pr: (q, hb * nb + i, 0))
        out_shape = _sds((4, 2 * K2, Ns), BF16)
    return pl.pallas_call(
        body, name=name, out_shape=out_shape,
        grid_spec=pltpu.PrefetchScalarGridSpec(
            num_scalar_prefetch=1, grid=(4, 2, nb),
            in_specs=[pl.BlockSpec((None, tm, Ns), lambda q, hb, i, pr: (other(q, pr), jnp.where(hb == pr[1], i, 0), 0)),
                      pl.BlockSpec((None, tm, Ns), lambda q, hb, i, pr: (other(q, pr), jnp.where(hb == pr[1], 0, i), 0)),
                      pl.BlockSpec((tm, Ns), lambda q, hb, i, pr: (jnp.where(q == pr[0], hb * nb + i, 0), 0))],
            out_specs=out_spec),
        compiler_params=_params(("arbitrary",) * 3),
    )(place, mine, theirs, own)


def _split_lanes(row, widths):
    out, off = [], 0
    for wd in widths:
        out.append(row[:, off:off + wd])
        off += wd
    return out


def _adamw_math(w, g, m, v):
    m = ADAM_B1 * m + (1.0 - ADAM_B1) * g
    v = ADAM_B2 * v + (1.0 - ADAM_B2) * (g * g)
    m_hat = m / (1.0 - ADAM_B1 ** ADAM_STEP)
    v_hat = v / (1.0 - ADAM_B2 ** ADAM_STEP)
    delta = -ADAM_LR * (m_hat / (jnp.sqrt(v_hat) + ADAM_EPS) + ADAM_WD * w)
    return delta, m, v


def _adamw_halves(w, mine, other, m, v, core, *, name):
    K, Ns = w.shape
    tm = _block_rows(K // 2, Ns * 4, target=1 << 20, align=F32_SUBLANES)
    nb = (K // 2) // tm

    def body(core_ref, w_ref, mine_ref, other_ref, m_ref, v_ref, g_out, d_out, m_out, v_out):
        g = jnp.where(pl.program_id(0) // nb == core_ref[0], mine_ref[...], other_ref[...])
        g_out[...] = g
        d_out[...], m_out[...], v_out[...] = _adamw_math(w_ref[...], g, m_ref[...], v_ref[...])

    row = pl.BlockSpec((tm, Ns), lambda i, cr: (i, 0))
    return pl.pallas_call(
        body, name=name, out_shape=[_sds((K, Ns), F32)] * 4,
        grid_spec=pltpu.PrefetchScalarGridSpec(
            num_scalar_prefetch=1, grid=(K // tm,),
            in_specs=[row,
                      pl.BlockSpec((tm, Ns), lambda i, cr: (jnp.where(i // nb == cr[0], i % nb, 0), 0)),
                      pl.BlockSpec((None, tm, Ns), lambda i, cr: (1 - cr[0], jnp.where(i // nb == cr[0], 0, i % nb), 0)),
                      row, row],
            out_specs=[row, row, row, row]),
        compiler_params=_params(("arbitrary",)),
    )(core, w, mine, other, m, v)


def _adamw(w, g, m, v, *, name, carry=None):
    C = w.shape[1]

    def fn(w, g, m, v):
        return _adamw_math(w, g, m, v), ()

    tm = max(F32_SUBLANES, min(512, (1 << 20) // (4 * C) // F32_SUBLANES * F32_SUBLANES))
    res = _rowwise(fn, [w, g, m, v], [], [(C, F32)] * 3, name=name, tm=tm, carry=carry)
    return tuple(res[0]) + ((res[2],) if carry is not None else ())


def _rope_tables(n):
    rows = n // GRID_W
    row = jnp.repeat(jnp.arange(rows, dtype=F32), GRID_W)
    col = jnp.tile(jnp.arange(GRID_W, dtype=F32), rows)
    nf = ROPE_DIM // 4
    freqs = ROPE_THETA ** (-jnp.arange(nf, dtype=F32) / nf)
    ang_r, ang_c = row[:, None] * freqs[None, :], col[:, None] * freqs[None, :]
    cr, sr, cc, sc = jnp.cos(ang_r), jnp.sin(ang_r), jnp.cos(ang_c), jnp.sin(ang_c)
    nope = HEAD_PAD - 2 * ROPE_DIM
    one, zero, z = jnp.ones((n, nope), F32), jnp.zeros((n, nope), F32), jnp.zeros((n, nf), F32)
    pad = jnp.zeros((n, ROPE_DIM), F32)
    cos = jnp.concatenate([one, cr, cr, cc, cc, pad], axis=1)
    s_lo = jnp.concatenate([zero, -sr, z, -sc, z, pad], axis=1)
    s_hi = jnp.concatenate([zero, z, sr, z, sc, pad], axis=1)
    return cos, s_lo, s_hi


def _rope(n, cos, s_lo, s_hi):
    q = ROPE_DIM // 4
    return n * cos + pltpu.roll(n, HEAD_PAD - q, 1) * s_lo + pltpu.roll(n, q, 1) * s_hi


def _rope_t(d, cos, s_lo, s_hi):
    q = ROPE_DIM // 4
    return d * cos + pltpu.roll(d * s_lo, q, 1) + pltpu.roll(d * s_hi, HEAD_PAD - q, 1)


def kernel(x, c, ctx, c_ctx, w_mod, b_mod, norm1_g, w_in, q_norm_g, kv_norm_g, w_uq, w_ukv, qk_norm_q, qk_norm_k, sgu_norm_g, sgu_norm_b, w_spatial, b_spatial, w_br_attn, w_br_sgu, w_out, norm2_g, w_ffn_in, w_ffn_out, loss_target, m_c_ctx, m_w_mod, m_b_mod, m_norm1_g, m_w_in, m_q_norm_g, m_kv_norm_g, m_w_uq, m_w_ukv, m_qk_norm_q, m_qk_norm_k, m_sgu_norm_g, m_sgu_norm_b, m_w_spatial, m_b_spatial, m_w_br_attn, m_w_br_sgu, m_w_out, m_norm2_g, m_w_ffn_in, m_w_ffn_out, v_c_ctx, v_w_mod, v_b_mod, v_norm1_g, v_w_in, v_q_norm_g, v_kv_norm_g, v_w_uq, v_w_ukv, v_qk_norm_q, v_qk_norm_k, v_sgu_norm_g, v_sgu_norm_b, v_w_spatial, v_b_spatial, v_w_br_attn, v_w_br_sgu, v_w_out, v_norm2_g, v_w_ffn_in, v_w_ffn_out):
    ax, ay, ac = _place()
    my_chip = 2 * ax + ay
    my_dev = 4 * ax + 2 * ay + ac

    N, D = x.shape[1], x.shape[2]
    CT = ctx.shape[1]
    M = N + CT
    QL, KVL, QK = q_norm_g.shape[-1], kv_norm_g.shape[-1], qk_norm_q.shape[-1]
    NOPE = QK - ROPE_DIM
    VD = NOPE
    H = 4 * w_uq.shape[-1] // QK
    SW, G, CH = sgu_norm_g.shape[-1], w_spatial.shape[1], w_spatial.shape[2]
    GD = SW // G
    DFF = 4 * w_ffn_out.shape[1]
    NMOD = 4 * w_mod.shape[-1]
    NM = w_mod.shape[-1]
    KVP = KVL + 2 * ROPE_DIM
    assert NOPE == LANES and GD == LANES and HEAD_PAD == NOPE + 2 * ROPE_DIM and CH == LANES
    scale = QK ** -0.5

    x2, ctx2, tgt2 = x[0], ctx[0], loss_target[0]

    c_all = _all_gather8([c], name="ag_c", in_vmem=True)[0][:, 0, :]
    c_rows = jnp.concatenate([c_all, c_ctx[None, :], jnp.zeros((BF16_SUBLANES - 9, D), F32)], axis=0)

    def silu_fn(t):
        s = _sigmoid(t)
        return (t * s, s * (1.0 + t * (1.0 - s))), ()

    (silu_c, dsilu_c), _ = _rowwise(silu_fn, [c_rows], [], [(D, F32), (D, F32)], name="silu_c", tm=16)
    wm = w_mod[0]
    mod_loc = _mm([(silu_c, wm)], name="mod_fwd", outs=(F32,), tn=512, tk=512,
                  extras=[(lax.dynamic_slice_in_dim(b_mod, my_chip * NM, NM, axis=1), "n")],
                  epi=lambda acc, b: (acc + b,))
    mod_all = _all_gather8([mod_loc], name="ag_mod", in_vmem=True)[0]
    mod_full = jnp.concatenate([mod_all[0], mod_all[2], mod_all[4], mod_all[6]], axis=1)
    mod_me = lax.dynamic_slice_in_dim(mod_full, my_dev, 1, axis=0)
    sh1, sc1, g1, sh2, sc2, g2 = [mod_me[:, i * D:(i + 1) * D] for i in range(6)]
    sh1c, sc1c = mod_full[8:9, :D], mod_full[8:9, D:2 * D]

    big = [w_in[0], w_uq[0], w_ukv[0], w_br_attn[0], w_br_sgu[0], w_out[0], w_ffn_in[0], w_ffn_out[0]]
    col_sharded = [True, True, True, True, True, False, True, False]
    halves = [lax.dynamic_slice_in_dim(a, ac * (a.shape[0] // 2), a.shape[0] // 2, axis=0).astype(BF16) for a in big]
    tags = ["w_in", "w_uq", "w_ukv", "w_br_attn", "w_br_sgu", "w_out", "w_ffn_in", "w_ffn_out"]
    first_group, attn_group, ffn_group = [0, 1, 2], [3, 4, 5, 6], [7]
    chip1 = jnp.reshape(my_chip, (1,)).astype(jnp.int32)
    place2 = jnp.stack([my_chip, ac]).astype(jnp.int32)

    def laid_out(seg, i):
        a = big[i]
        if col_sharded[i] and seg.ndim == 3:
            return seg.transpose(1, 0, 2).reshape(a.shape[0], 4 * a.shape[1])
        return seg if col_sharded[i] else seg.reshape(4 * a.shape[0], a.shape[1])

    def side_by_side(i):
        return col_sharded[i] and big[i].shape[1] % LANES == 0

    def finish_gather(idx, mine4, theirs4):
        return [laid_out(_assemble_halves(m, t, big[i], place2, name="assemble_" + tags[i], transpose=side_by_side(i)), i)
                for i, m, t in zip(idx, mine4, theirs4)]

    gathered = _gather_others([halves[i] for i in first_group], name="ag_weights")
    w_in_f, w_uq_f, w_ukv_f = [
        laid_out(_assemble(seg.reshape((4,) + big[i].shape), big[i], chip1, name="assemble_" + tags[i],
                           transpose=side_by_side(i)), i) for i, seg in zip(first_group, gathered)]
    o_kv, o_u = QL, QL + KVL + ROPE_DIM
    o_v, o_g = o_u + SW, o_u + 2 * SW
    w_q = w_in_f[:, :QL]
    w_kv = jnp.pad(w_in_f[:, o_kv:o_u], ((0, 0), (0, ROPE_DIM)))
    w_u, w_v = w_in_f[:, o_u:o_v], w_in_f[:, o_v:o_g]
    w_g1, w_g2 = w_in_f[:, o_g:o_g + D], w_in_f[:, o_g + D:]
    w_uq_p = jnp.pad(w_uq_f.reshape(QL, H, QK), ((0, 0), (0, 0), (0, HEAD_PAD - QK))).reshape(QL, H * HEAD_PAD)

    cos_t, slo_t, shi_t = _rope_tables(N)
    ones_c = jnp.concatenate([jnp.ones((CT, NOPE + ROPE_DIM), F32), jnp.zeros((CT, ROPE_DIM), F32)], axis=1)
    cos_k = jnp.concatenate([cos_t, ones_c], axis=0)
    slo_k = jnp.concatenate([slo_t, jnp.zeros((CT, HEAD_PAD), F32)], axis=0)
    shi_k = jnp.concatenate([shi_t, jnp.zeros((CT, HEAD_PAD), F32)], axis=0)
    gq_p = jnp.pad(qk_norm_q, ((0, 0), (0, HEAD_PAD - QK)))
    gk_p = jnp.pad(qk_norm_k, ((0, 0), (0, HEAD_PAD - QK)))

    def norm_mod_fn(t, g, sh, sc):
        r = _rms_stats(t, D)
        return (((t * r) * g) * (1.0 + sc) + sh,), ()

    (h,), _ = _rowwise(norm_mod_fn, [x2], [norm1_g, sh1, sc1], [(D, BF16)], name="norm1_x")
    (ctx_h,), _ = _rowwise(norm_mod_fn, [ctx2], [norm1_g, sh1c, sc1c], [(D, BF16)], name="norm1_ctx")

    qc = _mm([(h, w_q)], name="proj_q", outs=(F32,))
    kvin = jnp.concatenate([_mm([(h, w_kv)], name="proj_kv", outs=(F32,)),
                            _mm([(ctx_h, w_kv)], name="proj_kv_ctx", outs=(F32,))], axis=0)
    u_in = _mm([(h, w_u)], name="proj_u", outs=(BF16,))
    v_in = _mm([(h, w_v)], name="proj_v", outs=(BF16,))
    g1_in, (mine_bra,) = _mm([(h, w_g1)], name="proj_g1", outs=(BF16,), carry=_ChipExchange([halves[3]], gather=True))
    g2_in, (mine_brs,) = _mm([(h, w_g2)], name="proj_g2", outs=(BF16,), carry=_ChipExchange([halves[4]], gather=True))

    def rms_gain_fn(width):
        def fn(t, g):
            return (((t * _rms_stats(t, width)) * g),), ()
        return fn

    (qn,), _ = _rowwise(rms_gain_fn(QL), [qc], [q_norm_g], [(QL, BF16)], name="q_norm")

    def kv_norm_fn(t, g):
        kvc = t[:, :KVL]
        return (((kvc * _rms_stats(kvc, KVL)) * g),), ()

    (kvn,), _ = _rowwise(kv_norm_fn, [kvin], [kv_norm_g], [(KVL, BF16)], name="kv_norm")
    q_raw = _mm([(qn, w_uq_p)], name="q_up", outs=(F32,))
    kv_raw = _mm([(kvn, w_ukv_f)], name="kv_up", outs=(F32,))

    def q_post_fn(t, cos, slo, shi, g):
        outs = []
        for hd in range(H):
            th = t[:, hd * HEAD_PAD:(hd + 1) * HEAD_PAD]
            outs.append(_rope((th * _rms_stats(th, QK)) * g, cos, slo, shi) * scale)
        return (jnp.concatenate(outs, axis=1),), ()

    (q_att,), _ = _rowwise(q_post_fn, [q_raw, cos_t, slo_t, shi_t], [gq_p], [(H * HEAD_PAD, BF16)], name="q_post")

    def k_post_fn(t, kvi, cos, slo, shi, g):
        kr = kvi[:, KVL:]
        ks, vs = [], []
        for hd in range(H):
            th = jnp.concatenate([t[:, hd * HEAD_PAD:hd * HEAD_PAD + NOPE], kr], axis=1)
            ks.append(_rope((th * _rms_stats(th, QK)) * g, cos, slo, shi))
            vs.append(t[:, hd * HEAD_PAD + NOPE:(hd + 1) * HEAD_PAD])
        return (jnp.concatenate(ks, axis=1), jnp.concatenate(vs, axis=1)), ()

    (k_att, v_att), _, (mine_out,) = _rowwise(k_post_fn, [kv_raw, kvin, cos_k, slo_k, shi_k], [gk_p],
                                              [(H * HEAD_PAD, BF16), (H * VD, BF16)], name="k_post",
                                              carry=_ChipExchange([halves[5]], gather=True))
    attn_o, lse, (mine_ffi,) = _attn_fwd(q_att, k_att, v_att, heads=H, carry=_ChipExchange([halves[6]], gather=True))
    mine4 = [mine_bra, mine_brs, mine_out, mine_ffi]

    ws3 = w_spatial[0]
    bs_t = jnp.pad(b_spatial[0].T, ((0, 0), (0, LANES - G)))

    def sgu_parts(u_in, v_in, ng, nb):
        u, v = _gelu(u_in.astype(F32)), _gelu(v_in.astype(F32))
        mu = jnp.mean(v, axis=-1, keepdims=True)
        vc = v - mu
        rs = lax.rsqrt(jnp.mean(vc * vc, axis=-1, keepdims=True) + EPS)
        xhat = vc * rs
        return u, xhat, rs, (xhat * ng + nb).astype(BF16)

    def sgu_fwd_fn(u_in, v_in, ng, nb, ws, bst):
        u, _, _, vnb = sgu_parts(u_in, v_in, ng, nb)
        outs = []
        for g in range(G):
            sl = slice(g * GD, (g + 1) * GD)
            mixed = jnp.dot(ws[g].astype(BF16), vnb[:, sl], preferred_element_type=F32) + bst[:, g:g + 1]
            outs.append(u[:, sl] * mixed)
        return (jnp.concatenate(outs, axis=1),), ()

    (sgu_o,), _, theirs4 = _rowwise(sgu_fwd_fn, [u_in, v_in], [sgu_norm_g, sgu_norm_b, ws3, bs_t], [(SW, BF16)],
                                    name="sgu_fwd", tm=CH, carry=_PairExchange(mine4, "forward"))
    w_bra, w_brs, w_out_f, w_ffi = finish_gather(attn_group, mine4, theirs4)
    w_fa, w_fb = w_ffi[:, :DFF], w_ffi[:, DFF:]

    a1 = _mm([(attn_o, w_bra)], name="br_attn", outs=(BF16,))
    def merge_epi(acc, a1v, gi1, gi2):
        return acc, _sigmoid(gi1.astype(F32)) * a1v.astype(F32) + _sigmoid(gi2.astype(F32)) * acc

    a2, merged = _mm([(sgu_o, w_brs)], name="br_sgu", outs=(BF16, BF16),
                     extras=[(a1, "mn"), (g1_in, "mn"), (g2_in, "mn")], epi=merge_epi)

    def res_gate(acc, res, gate):
        return res + gate * acc, acc

    x1, mo = _mm([(merged, w_out_f)], name="out_proj", outs=(F32, BF16), tn=1024,
                 extras=[(x2, "mn"), (g1, "n")], epi=res_gate)
    (h2,), _ = _rowwise(norm_mod_fn, [x1], [norm2_g, sh2, sc2], [(D, BF16)], name="norm2")

    def swiglu_epi(a, b):
        return a, b, (a * _sigmoid(a)) * b

    (fa, fb, act), mine4 = _mm([(h2, w_fa, w_fb)], name="ffn_in", outs=(BF16, BF16, BF16), tn=512, epi=swiglu_epi,
                               carry=_ChipExchange([halves[i] for i in ffn_group], gather=True))
    (w_ffo,) = finish_gather(ffn_group, mine4, _exchange_alone(_PairExchange(mine4, "forward"), name="ag_forward_ffn"))
    def loss_epi(acc, res, t, gate):
        e = (res + gate * acc) - t
        dy = e * (1.0 / D)
        return dy, gate * dy, _colsum(e * e) * (0.5 / D), _colsum(dy * acc)

    dy, df, loss_part, dg2_part = _mm([(act, w_ffo)], name="ffn_out", outs=(F32, BF16), tn=1024, col_sums=2,
                                      extras=[(x1, "mn"), (tgt2, "mn"), (g2, "n")], epi=loss_epi)

    def fold_fn(a, b):
        return (), (_colsum(a), _colsum(b))

    _, (loss_cols, dg2) = _rowwise(fold_fn, [loss_part[:, 0, :], dg2_part[:, 0, :]], [], [], [(1, D), (1, D)],
                                   name="loss_fold", tm=loss_part.shape[0])

    def swiglu_bwd_epi(dact, a, b):
        a, b = a.astype(F32), b.astype(F32)
        s = _sigmoid(a)
        return dact * b * (s * (1.0 + a * (1.0 - s))), dact * (a * s)

    da, db = _mm([(df, w_ffo)], tb=True, name="ffn_out_dx", outs=(BF16, BF16), tn=512,
                 extras=[(fa, "mn"), (fb, "mn")], epi=swiglu_bwd_epi)
    dw_ffo = _mm([(act, df)], ta=True, name="ffn_out_dw", outs=(BF16,))
    dh2 = _mm([(da, w_fa), (db, w_fb)], tb=True, name="ffn_in_dx", outs=(F32,))
    ns_ffi = w_ffn_in.shape[-1]
    dw_ffi = _mm([(h2, da)], ta=True, name="ffn_in_dw_a", outs=(BF16,), tn=1408, split=ns_ffi,
                 into=(lax.empty((4, D, ns_ffi), BF16), 0))
    dw_ffi = _mm([(h2, db)], ta=True, name="ffn_in_dw_b", outs=(BF16,), tn=1408, split=ns_ffi, into=(dw_ffi, 2))

    def norm2_bwd_fn(dh, t, dyv, mov, g, sc, g1v):
        r = _rms_stats(t, D)
        tn = t * r
        dxg = dh * (1.0 + sc)
        dt = dyv + _rms_bwd(dxg * g, tn, r, D)
        return (dt, g1v * dt), (_colsum(dh), _colsum(dh * (tn * g)), _colsum(dxg * tn), _colsum(dt * mov.astype(F32)))

    (dx1, dmo), (dsh2, dsc2, dn2g, dg1) = _rowwise(
        norm2_bwd_fn, [dh2, x1, dy, mo], [norm2_g, sc2, g1], [(D, F32), (D, BF16)], [(1, D)] * 4, name="norm2_bwd")

    def merge_bwd_epi(dm, a1, a2, gi1, gi2):
        s1, s2 = _sigmoid(gi1.astype(F32)), _sigmoid(gi2.astype(F32))
        a1, a2 = a1.astype(F32), a2.astype(F32)
        return dm * s1, dm * s2, dm * a1 * (s1 * (1.0 - s1)), dm * a2 * (s2 * (1.0 - s2))

    da1, da2, dgi1, dgi2 = _mm([(dmo, w_out_f)], tb=True, name="out_proj_dx", outs=(BF16,) * 4, tn=512,
                               extras=[(a1, "mn"), (a2, "mn"), (g1_in, "mn"), (g2_in, "mn")], epi=merge_bwd_epi)
    dw_out = _mm([(merged, dmo)], ta=True, name="out_proj_dw", outs=(BF16,))
    dattn = _mm([(da1, w_bra)], tb=True, name="br_attn_dx", outs=(BF16,))
    dw_bra = _mm([(attn_o, da1)], ta=True, name="br_attn_dw", outs=(BF16,), split=w_br_attn.shape[-1])
    dsgu = _mm([(da2, w_brs)], tb=True, name="br_sgu_dx", outs=(BF16,))
    dw_brs = _mm([(sgu_o, da2)], ta=True, name="br_sgu_dw", outs=(BF16,), split=w_br_sgu.shape[-1])

    def sgu_bwd_fn(dso, u_in, v_in, ng, nb, ws, bst):
        u, xhat, rs, vnb = sgu_parts(u_in, v_in, ng, nb)
        dso = dso.astype(F32)
        lane = lax.broadcasted_iota(jnp.int32, (CH, LANES), 1)
        du, dvn, dws, dbs = [], [], [], jnp.zeros((CH, LANES), F32)
        for g in range(G):
            sl = slice(g * GD, (g + 1) * GD)
            wg = ws[g].astype(BF16)
            mixed = jnp.dot(wg, vnb[:, sl], preferred_element_type=F32) + bst[:, g:g + 1]
            du.append(dso[:, sl] * mixed)
            dmix = dso[:, sl] * u[:, sl]
            dmb = dmix.astype(BF16)
            dws.append(lax.dot_general(dmb, vnb[:, sl], (((1,), (1,)), ((), ())), preferred_element_type=F32))
            dbs = dbs + jnp.where(lane == g, jnp.sum(dmix, axis=1, keepdims=True), 0.0)
            dvn.append(lax.dot_general(wg, dmb, (((0,), (0,)), ((), ())), preferred_element_type=F32))
        du, dvn = jnp.concatenate(du, axis=1), jnp.concatenate(dvn, axis=1)
        dxh = dvn * ng
        dv = rs * (dxh - jnp.mean(dxh, axis=-1, keepdims=True) - xhat * jnp.mean(dxh * xhat, axis=-1, keepdims=True))
        return ((du * _gelu_grad(u_in.astype(F32)), dv * _gelu_grad(v_in.astype(F32))),
                (_colsum(dvn * xhat), _colsum(dvn), jnp.stack(dws), dbs))

    core = jnp.reshape(ac, (1,)).astype(jnp.int32)

    def dest_layout(dwf, i):
        K, Ns = big[i].shape
        if dwf.ndim == 2:
            dwf = dwf.reshape(K, 4, Ns).transpose(1, 0, 2) if col_sharded[i] else dwf.reshape(4, K, Ns)
        return dwf.reshape(4, 2, K // 2, Ns)

    def pair_sums(idx, g4, sib):
        return [_pair_add(g, s, core, name="rs_pair_add_" + tags[i]) for g, s, i in zip(g4, sib, idx)]

    early = [3, 4, 5, 6, 7]
    g4_early = [dest_layout(d, i) for d, i in zip([dw_bra, dw_brs, dw_out, dw_ffi, dw_ffo], early)]
    (du_in, dv_in), (d_sng, d_snb, d_ws, d_bs), sib_early = _rowwise(
        sgu_bwd_fn, [dsgu, u_in, v_in], [sgu_norm_g, sgu_norm_b, ws3, bs_t], [(SW, BF16), (SW, BF16)],
        [(1, SW), (1, SW), (G, CH, CH), (CH, LANES)], name="sgu_bwd", tm=CH, carry=_PairExchange(g4_early, "halves"))
    pair_early = pair_sums(early, g4_early, sib_early)
    dq_att, dk_att, dv_att, xchg_early = _attn_bwd(q_att, k_att, v_att, attn_o, lse, dattn, heads=H,
                                                   carry=_ChipExchange(pair_early, gather=False))

    def q_post_bwd_fn(dq, t, cos, slo, shi, g):
        outs, dg = [], jnp.zeros((1, HEAD_PAD), F32)
        for hd in range(H):
            sl = slice(hd * HEAD_PAD, (hd + 1) * HEAD_PAD)
            th = t[:, sl]
            r = _rms_stats(th, QK)
            tn = th * r
            dn = _rope_t(dq[:, sl] * scale, cos, slo, shi)
            dg = dg + _colsum(dn * tn)
            outs.append(_rms_bwd(dn * g, tn, r, QK))
        return (jnp.concatenate(outs, axis=1),), (dg,)

    (dq_raw,), (d_gq,) = _rowwise(q_post_bwd_fn, [dq_att, q_raw, cos_t, slo_t, shi_t], [gq_p],
                                  [(H * HEAD_PAD, BF16)], [(1, HEAD_PAD)], name="q_post_bwd")

    def k_post_bwd_fn(dk, dv, t, kvi, cos, slo, shi, g):
        kr = kvi[:, KVL:]
        outs, dg, dkr = [], jnp.zeros((1, HEAD_PAD), F32), jnp.zeros_like(kr)
        for hd in range(H):
            th = jnp.concatenate([t[:, hd * HEAD_PAD:hd * HEAD_PAD + NOPE], kr], axis=1)
            r = _rms_stats(th, QK)
            tn = th * r
            dn = _rope_t(dk[:, hd * HEAD_PAD:(hd + 1) * HEAD_PAD], cos, slo, shi)
            dg = dg + _colsum(dn * tn)
            dt = _rms_bwd(dn * g, tn, r, QK)
            dkr = dkr + dt[:, NOPE:]
            outs += [dt[:, :NOPE], dv[:, hd * VD:(hd + 1) * VD]]
        return (jnp.concatenate(outs, axis=1), dkr), (dg,)

    def reduced_halves(idx, xchg, pair):
        return [_sum_chips(t4, pr, chip1, name="rs_sum_" + tags[i]) for t4, pr, i in zip(xchg, pair, idx)]

    red_early = reduced_halves(early, xchg_early, pair_early)
    (dkv_raw, dkrope), (d_gk,), other_early = _rowwise(
        k_post_bwd_fn, [dk_att, dv_att, kv_raw, kvin, cos_k, slo_k, shi_k], [gk_p],
        [(H * HEAD_PAD, BF16), (2 * ROPE_DIM, F32)], [(1, HEAD_PAD)], name="k_post_bwd",
        carry=_PairExchange(red_early, "gather"))

    dqn = _mm([(dq_raw, w_uq_p)], tb=True, name="q_up_dx", outs=(F32,))
    dw_uq_p = _mm([(qn, dq_raw)], ta=True, name="q_up_dw", outs=(BF16,))
    dkvn = _mm([(dkv_raw, w_ukv_f)], tb=True, name="kv_up_dx", outs=(F32,))
    dw_ukv = _mm([(kvn, dkv_raw)], ta=True, name="kv_up_dw", outs=(BF16,))

    def q_norm_bwd_fn(dn, t, g):
        r = _rms_stats(t, QL)
        tn = t * r
        return (_rms_bwd(dn * g, tn, r, QL),), (_colsum(dn * tn),)

    (dqc,), (d_qng,) = _rowwise(q_norm_bwd_fn, [dqn, qc], [q_norm_g], [(QL, BF16)], [(1, QL)], name="q_norm_bwd")

    def kv_norm_bwd_fn(dn, dkr, t, g):
        kvc = t[:, :KVL]
        r = _rms_stats(kvc, KVL)
        tn = kvc * r
        return (jnp.concatenate([_rms_bwd(dn * g, tn, r, KVL), dkr], axis=1),), (_colsum(dn * tn),)

    (dkvin,), (d_kvng,) = _rowwise(kv_norm_bwd_fn, [dkvn, dkrope, kvin], [kv_norm_g], [(KVP, BF16)], [(1, KVL)],
                                   name="kv_norm_bwd")
    dkvin_x, dkvin_c = dkvin[:N], dkvin[N:]

    dctx_h = _mm([(dkvin_c, w_kv)], tb=True, name="proj_kv_ctx_dx", outs=(F32,))
    dw_q = _mm([(h, dqc)], ta=True, name="proj_q_dw", outs=(BF16,))
    dw_kv = _mm([(h, dkvin_x), (ctx_h, dkvin_c)], ta=True, name="proj_kv_dw", outs=(BF16,))
    dw_u = _mm([(h, du_in)], ta=True, name="proj_u_dw", outs=(BF16,))
    dw_v = _mm([(h, dv_in)], ta=True, name="proj_v_dw", outs=(BF16,))
    dw_g1 = _mm([(h, dgi1)], ta=True, name="proj_g1_dw", outs=(BF16,))
    dw_g2 = _mm([(h, dgi2)], ta=True, name="proj_g2_dw", outs=(BF16,))

    dw_in_f = jnp.concatenate([dw_q, dw_kv[:, :KVL + ROPE_DIM], dw_u, dw_v, dw_g1, dw_g2], axis=1)
    dw_uq_f = dw_uq_p.reshape(QL, H, HEAD_PAD)[:, :, :QK].reshape(QL, H * QK)
    late = [0, 1, 2]
    g4_late = [dest_layout(d, i) for d, i in zip([dw_in_f, dw_uq_f, dw_ukv], late)]
    pair_late = pair_sums(late, g4_late, _exchange_alone(_PairExchange(g4_late, "halves"), name="rs_pair_late"))
    dh, xchg_late = _mm([(dqc, w_q), (dkvin_x, w_kv), (du_in, w_u), (dv_in, w_v), (dgi1, w_g1), (dgi2, w_g2)],
                        tb=True, name="proj_dx", outs=(F32,), tn=1024, tk=512,
                        carry=_ChipExchange(pair_late, gather=False))

    def norm1_bwd_fn(dhv, t, dres, g, sc):
        r = _rms_stats(t, D)
        tn = t * r
        dxg = dhv * (1.0 + sc)
        return (dres + _rms_bwd(dxg * g, tn, r, D),), (_colsum(dhv), _colsum(dhv * (tn * g)), _colsum(dxg * tn))

    (grad_x,), (dsh1, dsc1, dn1g_x) = _rowwise(norm1_bwd_fn, [dh, x2, dx1], [norm1_g, sc1], [(D, F32)], [(1, D)] * 3,
                                               name="norm1_bwd")
    _, (dsh1c, dsc1c, dn1g_c) = _rowwise(norm1_bwd_fn, [dctx_h, ctx2, jnp.zeros_like(ctx2)], [norm1_g, sc1c],
                                         [(D, F32)], [(1, D)] * 3, name="norm1_ctx_bwd")

    small = [dsh1, dsc1, dg1, dsh2, dsc2, dg2,
             dsh1c, dsc1c, dn1g_x, dn1g_c, d_qng, d_kvng, d_gq, d_gk, d_sng, d_snb, dn2g, loss_cols]
    small_sizes = [a.shape[1] for a in small]
    sm_row = jnp.concatenate(small, axis=1)
    sm_mat = jnp.concatenate([d_ws.reshape(G * CH, CH), d_bs], axis=0)
    row_all, mat_all = _all_gather8([sm_row, sm_mat], name="ag_small", in_vmem=True)
    row_sum = _sum_blocks(row_all, name="sum_small_rows", out_dtype=F32)
    mat_sum = _sum_blocks(mat_all, name="sum_small_mats", out_dtype=F32)
    dmod_rows = row_all[:, 0, :NMOD]
    (_, _, _, _, _, _, t_sh1c, t_sc1c, t_n1x, t_n1c, g_qng, g_kvng, t_gq, t_gk, g_sng, g_snb, g_n2g,
     t_loss) = _split_lanes(row_sum, small_sizes)
    g_ws, t_bs = mat_sum[:G * CH], mat_sum[G * CH:]
    dmodc_row = jnp.concatenate([t_sh1c, t_sc1c, jnp.zeros((1, NMOD - 2 * D), F32)], axis=1)
    dmod16 = jnp.concatenate([dmod_rows, dmodc_row, jnp.zeros((BF16_SUBLANES - 9, NMOD), F32)], axis=0)

    def small_fn(rows, n1x, n1c, lossv):
        return (), (_colsum(rows), n1x + n1c, jnp.sum(lossv, axis=1, keepdims=True))

    _, (g_bmod, g_n1g, loss11) = _rowwise(small_fn, [dmod16], [t_n1x, t_n1c, t_loss], [], [(1, NMOD), (1, D), (1, 1)],
                                          name="small_reduce", tm=16)
    dmod_loc = lax.dynamic_slice_in_dim(dmod16, my_chip * NM, NM, axis=1)
    g_wmod = _mm([(silu_c, dmod_loc)], ta=True, name="mod_dw", outs=(F32,), tn=512)
    dsilu_part = _mm([(dmod_loc, wm)], tb=True, name="mod_dx", outs=(F32,), tk=512)
    part_all = _all_gather8([dsilu_part[8:9]], name="ag_cctx", in_vmem=True)[0]

    def cctx_fn(parts, dsl):
        return (), ((parts[0:1] + parts[2:3] + parts[4:5] + parts[6:7]) * dsl,)

    _, (g_cctx,) = _rowwise(cctx_fn, [part_all[:, 0, :]], [dsilu_c[8:9]], [], [(1, D)], name="cctx_grad", tm=8)

    red_late = reduced_halves(late, xchg_late, pair_late)
    other_late = _exchange_alone(_PairExchange(red_late, "gather"), name="rs_halves_late")
    grad_halves = dict(zip(tags, zip(red_late + red_early, other_late + other_early)))

    mod_upd = _adamw(w_mod[0], g_wmod, m_w_mod[0], v_w_mod[0], name="adamw_w_mod")
    grads = dict(
        c_ctx=g_cctx.reshape(D), w_mod=g_wmod[None], b_mod=g_bmod, norm1_g=g_n1g,
        q_norm_g=g_qng, kv_norm_g=g_kvng, qk_norm_q=t_gq[:, :QK], qk_norm_k=t_gk[:, :QK], sgu_norm_g=g_sng,
        sgu_norm_b=g_snb, w_spatial=g_ws.reshape(w_spatial.shape), b_spatial=t_bs[:, :G].T[None], norm2_g=g_n2g)
    weights = dict(c_ctx=c_ctx, w_mod=w_mod, b_mod=b_mod, norm1_g=norm1_g, w_in=w_in, q_norm_g=q_norm_g,
                   kv_norm_g=kv_norm_g, w_uq=w_uq, w_ukv=w_ukv, qk_norm_q=qk_norm_q, qk_norm_k=qk_norm_k,
                   sgu_norm_g=sgu_norm_g, sgu_norm_b=sgu_norm_b, w_spatial=w_spatial, b_spatial=b_spatial,
                   w_br_attn=w_br_attn, w_br_sgu=w_br_sgu, w_out=w_out, norm2_g=norm2_g, w_ffn_in=w_ffn_in,
                   w_ffn_out=w_ffn_out)
    m_in = dict(c_ctx=m_c_ctx, w_mod=m_w_mod, b_mod=m_b_mod, norm1_g=m_norm1_g, w_in=m_w_in, q_norm_g=m_q_norm_g,
                kv_norm_g=m_kv_norm_g, w_uq=m_w_uq, w_ukv=m_w_ukv, qk_norm_q=m_qk_norm_q, qk_norm_k=m_qk_norm_k,
                sgu_norm_g=m_sgu_norm_g, sgu_norm_b=m_sgu_norm_b, w_spatial=m_w_spatial, b_spatial=m_b_spatial,
                w_br_attn=m_w_br_attn, w_br_sgu=m_w_br_sgu, w_out=m_w_out, norm2_g=m_norm2_g, w_ffn_in=m_w_ffn_in,
                w_ffn_out=m_w_ffn_out)
    v_in_ = dict(c_ctx=v_c_ctx, w_mod=v_w_mod, b_mod=v_b_mod, norm1_g=v_norm1_g, w_in=v_w_in, q_norm_g=v_q_norm_g,
                 kv_norm_g=v_kv_norm_g, w_uq=v_w_uq, w_ukv=v_w_ukv, qk_norm_q=v_qk_norm_q, qk_norm_k=v_qk_norm_k,
                 sgu_norm_g=v_sgu_norm_g, sgu_norm_b=v_sgu_norm_b, w_spatial=v_w_spatial, b_spatial=v_b_spatial,
                 w_br_attn=v_w_br_attn, w_br_sgu=v_w_br_sgu, w_out=v_w_out, norm2_g=v_norm2_g, w_ffn_in=v_w_ffn_in,
                 w_ffn_out=v_w_ffn_out)
    names = list(weights)
    big_names = ("w_mod", "w_in", "w_uq", "w_ukv", "w_br_attn", "w_br_sgu", "w_out", "w_ffn_in", "w_ffn_out")
    out_g, out_d, out_m, out_v = {}, {}, {}, {}
    out_g["w_mod"] = grads["w_mod"]
    out_d["w_mod"], out_m["w_mod"], out_v["w_mod"] = [t[None] for t in mod_upd[:3]]
    for nm in big_names[1:]:
        res = _adamw_halves(weights[nm][0], *grad_halves[nm], m_in[nm][0], v_in_[nm][0], core, name="adamw_" + nm)
        out_g[nm], out_d[nm], out_m[nm], out_v[nm] = [t[None] for t in res]
    row_names = [nm for nm in names if nm not in big_names and nm not in ("w_spatial", "b_spatial")]
    widths = [-(-weights[nm].size // LANES) * LANES for nm in row_names]

    def as_row(d):
        return jnp.concatenate([jnp.pad(d[nm].reshape(1, -1), ((0, 0), (0, wd - d[nm].size)))
                                for nm, wd in zip(row_names, widths)], axis=1)

    def as_mat(d):
        return jnp.concatenate([d["w_spatial"].reshape(G * CH, CH), d["b_spatial"].reshape(G, CH)], axis=0)

    row_res = _adamw(as_row(weights), as_row(grads), as_row(m_in), as_row(v_in_), name="adamw_rows")
    mat_res = _adamw(as_mat(weights), as_mat(grads), as_mat(m_in), as_mat(v_in_), name="adamw_spatial")
    for tgt, row, mat in zip((out_d, out_m, out_v), row_res, mat_res):
        for nm, seg in zip(row_names, _split_lanes(row, widths)):
            tgt[nm] = seg[:, :weights[nm].size].reshape(weights[nm].shape)
        tgt["w_spatial"] = mat[:G * CH].reshape(w_spatial.shape)
        tgt["b_spatial"] = mat[G * CH:].reshape(b_spatial.shape)
    for nm in row_names + ["w_spatial", "b_spatial"]:
        out_g[nm] = grads[nm].reshape(weights[nm].shape)

    loss = loss11.reshape(())
    return (loss, grad_x[None], *[out_g[n] for n in names], *[out_d[n] for n in names],
            *[out_m[n] for n in names], *[out_v[n] for n in names])
```

```python
import math

import jax
import jax.numpy as jnp
from jax import lax
from jax.experimental import pallas as pl
from jax.experimental.pallas import tpu as pltpu

F32, BF16 = jnp.float32, jnp.bfloat16
MESH = pl.DeviceIdType.MESH

LANES = 128
F32_SUBLANES = 8
BF16_SUBLANES = 16
MXU_DIM = 256
VMEM_LIMIT_BYTES = 56 * 1024 * 1024

EPS = 1e-6
ROPE_DIM = 64
ROPE_THETA = 10000.0
GRID_W = 64
HEAD_PAD = 256
ADAM_LR, ADAM_B1, ADAM_B2, ADAM_EPS, ADAM_WD, ADAM_STEP = 0.001, 0.9, 0.999, 1e-08, 0.01, 10


def _tile(dim, pref, align=LANES):
    if dim <= pref:
        return dim
    t = (pref // align) * align
    while t >= align:
        if dim % t == 0:
            return t
        t -= align
    return dim


def _params(sem=None):
    return pltpu.CompilerParams(dimension_semantics=sem, vmem_limit_bytes=VMEM_LIMIT_BYTES)


def _sds(shape, dtype):
    return jax.ShapeDtypeStruct(tuple(shape), dtype)


def _mm(pairs, *, name, ta=False, tb=False, outs=(F32,), tm=1024, tn=1024, tk=2048, extras=(), epi=None,
        split=None, into=None, carry=None, col_sums=0):
    dual = len(pairs[0]) == 3
    a0, b0 = pairs[0][0], pairs[0][1]
    M = a0.shape[1] if ta else a0.shape[0]
    N = b0.shape[0] if tb else b0.shape[1]
    tm, tn = _tile(M, tm), _tile(N if split is None else split, tn)
    ks = [(p[0].shape[0] if ta else p[0].shape[1]) for p in pairs]
    tks = [_tile(k, tk) for k in ks]
    nks = [k // t for k, t in zip(ks, tks)]
    offs = [sum(nks[:i]) for i in range(len(pairs))]
    nk_total = sum(nks)
    single = len(pairs) == 1

    def kidx(kk, p):
        return kk if single else jnp.clip(kk - offs[p], 0, nks[p] - 1)

    in_specs, operands = [], []
    for p, pr in enumerate(pairs):
        if ta:
            in_specs.append(pl.BlockSpec((tks[p], tm), lambda i, j, kk, p=p: (kidx(kk, p), i)))
        else:
            in_specs.append(pl.BlockSpec((tm, tks[p]), lambda i, j, kk, p=p: (i, kidx(kk, p))))
        operands.append(pr[0])
        for b in pr[1:]:
            if tb:
                in_specs.append(pl.BlockSpec((tn, tks[p]), lambda i, j, kk, p=p: (j, kidx(kk, p))))
            else:
                in_specs.append(pl.BlockSpec((tks[p], tn), lambda i, j, kk, p=p: (kidx(kk, p), j)))
            operands.append(b)
    for arr, kind in extras:
        if kind == "mn":
            in_specs.append(pl.BlockSpec((tm, tn), lambda i, j, kk: (i, j)))
        else:
            in_specs.append(pl.BlockSpec((1, tn), lambda i, j, kk: (0, j)))
        operands.append(arr)
    n_in = len(operands)
    n_ex = len(extras)
    per = 3 if dual else 2
    dims = (((0 if ta else 1,), (1 if tb else 0,)), ((), ()))

    n_acc = 2 if dual else 1

    def products(ins, p):
        a = ins[per * p][...].astype(BF16)
        return [lax.dot_general(a, ins[per * p + 1 + q][...].astype(BF16), dims, preferred_element_type=F32)
                for q in range(n_acc)]

    def finish(ins, out_refs, acc_vals):
        vals = acc_vals + [r[...] for r in ins[n_in - n_ex:]]
        res = epi(*vals) if epi is not None else (vals[0],)
        for o, r in zip(out_refs, res):
            o[...] = jnp.broadcast_to(r, o.shape).astype(o.dtype)

    out_specs = [pl.BlockSpec((tm, tn), lambda i, j, kk: (i, j)) for _ in outs]
    out_specs += [pl.BlockSpec((None, F32_SUBLANES, tn), lambda i, j, kk: (i, 0, j)) for _ in range(col_sums)]
    out_shape = [_sds((M, N), d) for d in outs] + [_sds((M // tm, F32_SUBLANES, N), F32) for _ in range(col_sums)]
    aliases = {}
    n_alias = 0
    if split is not None:
        nps = split // tn
        lead = 0 if into is None else into[1]
        out_specs = [pl.BlockSpec((None, tm, tn), lambda i, j, kk: (j // nps + lead, i, j % nps))]
        out_shape = [_sds((N // split if into is None else into[0].shape[0], M, split), outs[0])]
        if into is not None:
            in_specs.append(pl.BlockSpec(memory_space=pl.ANY))
            operands.append(into[0])
            aliases, n_alias = {n_in: 0}, 1

    grid = (M // tm, N // tn, nk_total)
    n_out = len(outs) + col_sums

    def at_step(first):
        ids = [pl.program_id(d) for d in range(3)]
        cond = None
        for d, g in zip(ids, grid):
            t = d == (0 if first else g - 1)
            cond = t if cond is None else cond & t
        return cond

    def body(*refs):
        ins, out_refs, accs, start, wait = _split_refs(refs, n_in + n_alias, n_out, carry)
        ins = ins[:n_in]
        if carry is not None:
            pl.when(at_step(True))(start)
        if nk_total == 1:
            finish(ins, out_refs, products(ins, 0))
        else:
            kk = pl.program_id(2)

            @pl.when(kk == 0)
            def _():
                for acc, v in zip(accs, products(ins, 0)):
                    acc[...] = v

            for p in range(len(pairs)):
                lo = max(offs[p], 1)

                @pl.when((kk >= lo) & (kk < offs[p] + nks[p]))
                def _(p=p):
                    for acc, v in zip(accs, products(ins, p)):
                        acc[...] += v

            @pl.when(kk == nk_total - 1)
            def _():
                finish(ins, out_refs, [acc[...] for acc in accs])
        if carry is not None:
            pl.when(at_step(False))(wait)

    ex = carry
    res = pl.pallas_call(
        body, name=name, grid=grid, in_specs=in_specs + ([] if ex is None else ex.in_specs),
        out_specs=out_specs + ([] if ex is None else ex.out_specs),
        out_shape=out_shape + ([] if ex is None else ex.out_shape), input_output_aliases=aliases,
        scratch_shapes=[pltpu.VMEM((tm, tn), F32) for _ in range(n_acc if nk_total > 1 else 0)]
        + ([] if ex is None else ex.scratch),
        compiler_params=_params(("arbitrary",) * 3 if ex is not None else ("parallel", "parallel", "arbitrary")),
    )(*operands, *([] if ex is None else ex.xs))
    if ex is not None:
        return (res[0] if n_out == 1 else res[:n_out]), list(res[n_out:])
    return res[0] if n_out == 1 else res


def _rowwise(fn, rows, vecs, out_rows, out_accs=(), *, name, tm=256, tc=None, carry=None):
    M = rows[0].shape[0]
    tm = _tile(M, tm, BF16_SUBLANES)
    nrow = M // tm
    C = rows[0].shape[1]
    ncol = 1 if tc is None else C // _tile(C, tc)
    tcol = None if tc is None else _tile(C, tc)

    def colwise(shape):
        return tc is not None and len(shape) == 2 and shape[0] == 1 and shape[1] == C

    def vspec(shape):
        if colwise(shape):
            return pl.BlockSpec((1, tcol), lambda j, i: (0, j))
        return pl.BlockSpec(tuple(shape), lambda j, i, n=len(shape): (0,) * n)

    def rspec(width):
        if tc is None:
            return pl.BlockSpec((tm, width), lambda j, i: (i, 0))
        return pl.BlockSpec((tm, tcol), lambda j, i: (i, j))

    in_specs = [rspec(r.shape[1]) for r in rows] + [vspec(v.shape) for v in vecs]
    out_specs = [rspec(c) for c, _ in out_rows] + [vspec(s) for s in out_accs]
    out_shape = [_sds((M, c), d) for c, d in out_rows] + [_sds(s, F32) for s in out_accs]
    n_in, n_or = len(rows) + len(vecs), len(out_rows)

    n_out = n_or + len(out_accs)
    ex = carry

    def body(*refs):
        ins, outs, _, start, wait = _split_refs(refs, n_in, n_out, ex)
        o_rows, o_accs = outs[:n_or], outs[n_or:]
        if ex is not None:
            pl.when((pl.program_id(0) == 0) & (pl.program_id(1) == 0))(start)
        r_out, a_out = fn(*[r[...] for r in ins])
        for o, r in zip(o_rows, r_out):
            o[...] = r.astype(o.dtype)
        i = pl.program_id(1)

        @pl.when(i == 0)
        def _():
            for o, a in zip(o_accs, a_out):
                o[...] = a

        @pl.when(i > 0)
        def _():
            for o, a in zip(o_accs, a_out):
                o[...] += a

        if ex is not None:
            pl.when((pl.program_id(0) == ncol - 1) & (pl.program_id(1) == nrow - 1))(wait)

    res = pl.pallas_call(
        body, name=name, grid=(ncol, nrow), in_specs=in_specs + ([] if ex is None else ex.in_specs),
        out_specs=out_specs + ([] if ex is None else ex.out_specs),
        out_shape=out_shape + ([] if ex is None else ex.out_shape),
        scratch_shapes=[] if ex is None else ex.scratch,
        compiler_params=_params(("arbitrary", "arbitrary") if ex is not None else ("parallel", "arbitrary")),
    )(*rows, *vecs, *([] if ex is None else ex.xs))
    if ex is not None:
        return res[:n_or], res[n_or:n_out], list(res[n_out:])
    return res[:n_or], res[n_or:]


def _colsum(t):
    return jnp.sum(t, axis=0, keepdims=True)


def _gelu(t):
    return 0.5 * t * (1.0 + lax.erf(t * math.sqrt(0.5)))


def _gelu_grad(t):
    return 0.5 * (1.0 + lax.erf(t * math.sqrt(0.5))) + t * jnp.exp(-0.5 * t * t) * (1.0 / math.sqrt(2.0 * math.pi))


def _sigmoid(t):
    return 1.0 / (1.0 + jnp.exp(-t))


def _rms_stats(t, width):
    return lax.rsqrt(jnp.sum(t * t, axis=-1, keepdims=True) * (1.0 / width) + EPS)


def _rms_bwd(dn, tn, r, width):
    return r * (dn - tn * (jnp.sum(dn * tn, axis=-1, keepdims=True) * (1.0 / width)))


def _place():
    return lax.axis_index("x"), lax.axis_index("y"), lax.axis_index("c")


class _ChipExchange:
    def __init__(self, xs, gather):
        self.xs, self.gather, self.n = list(xs), gather, len(xs)
        self.in_specs = [pl.BlockSpec(memory_space=pl.ANY)] * self.n
        self.out_specs = [pl.BlockSpec(memory_space=pl.ANY)] * self.n
        self.out_shape = [_sds((4,) + (x.shape if gather else x.shape[1:]), x.dtype) for x in self.xs]
        self.scratch = [pltpu.SemaphoreType.DMA((self.n, 3)), pltpu.SemaphoreType.DMA((self.n, 3))]

    def bind(self, x_refs, out_refs, send_sems, recv_sems):
        x, y, c = _place()
        p = 2 * x + y
        chips = [(1 - x, y), (x, 1 - y), (1 - x, 1 - y)]

        def copy(w, k, outgoing):
            qx, qy = chips[k]
            there = 2 * qx + qy
            if self.gather:
                src = x_refs[w]
            else:
                src = x_refs[w].at[there if outgoing else p]
            return pltpu.make_async_remote_copy(
                src_ref=src, dst_ref=out_refs[w].at[p if outgoing else there], send_sem=send_sems.at[w, k],
                recv_sem=recv_sems.at[w, k], device_id=(qx, qy, c), device_id_type=MESH)

        def start():
            for w in range(self.n):
                for k in range(3):
                    copy(w, k, True).start()

        def wait():
            for w in range(self.n):
                for k in range(3):
                    copy(w, k, False).wait_recv()
            for w in range(self.n):
                for k in range(3):
                    copy(w, k, True).wait_send()

        return start, wait


class _PairExchange:
    def __init__(self, xs, mode):
        self.xs, self.mode, self.n = list(xs), mode, len(xs)
        self.in_specs = [pl.BlockSpec(memory_space=pl.ANY)] * self.n
        self.out_specs = [pl.BlockSpec(memory_space=pl.ANY)] * self.n
        shape = {"halves": lambda s: (4,) + s[2:], "forward": lambda s: s, "gather": lambda s: (2,) + s}[mode]
        self.out_shape = [_sds(shape(x.shape), x.dtype) for x in self.xs]
        self.scratch = [pltpu.SemaphoreType.DMA((self.n, 3)), pltpu.SemaphoreType.DMA((self.n, 3))]

    def bind(self, x_refs, out_refs, send_sems, recv_sems):
        x, y, c = _place()
        chips = [(1 - x, y), (x, 1 - y), (1 - x, 1 - y)]

        def copy(w, src, dst, k):
            return pltpu.make_async_remote_copy(src_ref=src, dst_ref=dst, send_sem=send_sems.at[w, k],
                                                recv_sem=recv_sems.at[w, k], device_id=(x, y, 1 - c),
                                                device_id_type=MESH)

        def start():
            for w, (xr, orf) in enumerate(zip(x_refs, out_refs)):
                if self.mode == "halves":
                    for q in range(4):
                        copy(w, xr.at[q, 1 - c], orf.at[q], 0).start()
                elif self.mode == "forward":
                    for k, (qx, qy) in enumerate(chips):
                        copy(w, xr.at[2 * qx + qy], orf.at[2 * qx + qy], k).start()
                else:
                    copy(w, xr, orf.at[c], 0).start()

        def wait():
            for w, (xr, orf) in enumerate(zip(x_refs, out_refs)):
                if self.mode == "halves":
                    copy(w, orf, orf, 0).wait()
                elif self.mode == "forward":
                    for k, (qx, qy) in enumerate(chips):
                        copy(w, xr.at[2 * qx + qy], orf.at[2 * qx + qy], k).wait()
                else:
                    cp = copy(w, xr, orf.at[1 - c], 0)
                    cp.wait_recv()
                    cp.wait_send()

        return start, wait


def _split_refs(refs, n_in, n_out, ex):
    ne = 0 if ex is None else ex.n
    ins, xin = refs[:n_in], refs[n_in:n_in + ne]
    outs, xout = refs[n_in + ne:n_in + ne + n_out], refs[n_in + ne + n_out:n_in + 2 * ne + n_out]
    rest = refs[n_in + 2 * ne + n_out:]
    if ex is None:
        return ins, outs, rest, None, None
    start, wait = ex.bind(xin, xout, rest[-2], rest[-1])
    return ins, outs, rest[:-2], start, wait


def _attn_fwd(q, k, v, *, heads, tq=512, carry=None):
    N, M = q.shape[0], k.shape[0]
    tq = _tile(N, tq)
    sub = _tile(tq, MXU_DIM)
    vd = v.shape[1] // heads
    nq = N // tq

    def body(*refs):
        (q_ref, k_ref, v_ref), (o_ref, lse_ref), _, start, wait = _split_refs(refs, 3, 2, carry)
        if carry is not None:
            pl.when((pl.program_id(0) == 0) & (pl.program_id(1) == 0))(start)
        for sb in range(tq // sub):
            rows = pl.ds(sb * sub, sub)
            s = lax.dot_general(q_ref[rows, :], k_ref[...], (((1,), (1,)), ((), ())), preferred_element_type=F32)
            m = jnp.max(s, axis=-1, keepdims=True)
            p = jnp.exp(s - m)
            l = jnp.sum(p, axis=-1, keepdims=True)
            o = jnp.dot(p.astype(BF16), v_ref[...], preferred_element_type=F32) / l
            o_ref[rows, :] = o.astype(o_ref.dtype)
            lse_ref[rows, :] = jnp.broadcast_to(m + jnp.log(l), (sub, vd))
        if carry is not None:
            pl.when((pl.program_id(0) == heads - 1) & (pl.program_id(1) == nq - 1))(wait)

    ex = carry
    res = pl.pallas_call(
        body, name="attn_fwd", grid=(heads, nq),
        in_specs=[pl.BlockSpec((tq, HEAD_PAD), lambda h, i: (i, h)),
                  pl.BlockSpec((M, HEAD_PAD), lambda h, i: (0, h)),
                  pl.BlockSpec((M, vd), lambda h, i: (0, h))] + ([] if ex is None else ex.in_specs),
        out_specs=[pl.BlockSpec((tq, vd), lambda h, i: (i, h)),
                   pl.BlockSpec((tq, vd), lambda h, i: (i, h))] + ([] if ex is None else ex.out_specs),
        out_shape=[_sds((N, heads * vd), BF16), _sds((N, heads * vd), F32)] + ([] if ex is None else ex.out_shape),
        scratch_shapes=[] if ex is None else ex.scratch,
        compiler_params=_params(("arbitrary", "arbitrary")),
    )(q, k, v, *([] if ex is None else ex.xs))
    return res[0], res[1], list(res[2:])


def _attn_bwd(q, k, v, o, lse, do, *, heads, tq=512, carry=None):
    N, M = q.shape[0], k.shape[0]
    tq = _tile(N, tq)
    vd = v.shape[1] // heads
    nq = N // tq
    sub = _tile(tq, MXU_DIM)
    nt = (((1,), (1,)), ((), ()))
    tn = (((0,), (0,)), ((), ()))

    def body(*refs):
        (q_ref, k_ref, v_ref, o_ref, lse_ref, do_ref), (dq_ref, dk_ref, dv_ref), _, start, wait = _split_refs(
            refs, 6, 3, carry)
        if carry is not None:
            pl.when((pl.program_id(0) == 0) & (pl.program_id(1) == 0))(start)
        i = pl.program_id(1)
        kb, vb = k_ref[...], v_ref[...]
        parts = []
        for sb in range(tq // sub):
            rows = pl.ds(sb * sub, sub)
            qb, dob = q_ref[rows, :], do_ref[rows, :]
            delta = jnp.sum(dob.astype(F32) * o_ref[rows, :].astype(F32), axis=-1, keepdims=True)
            s = lax.dot_general(qb, kb, nt, preferred_element_type=F32)
            p = jnp.exp(s - lse_ref[rows, :][:, :1])
            dp = lax.dot_general(dob, vb, nt, preferred_element_type=F32)
            ds = (p * (dp - delta)).astype(BF16)
            dq_ref[rows, :] = jnp.dot(ds, kb, preferred_element_type=F32)
            parts.append((lax.dot_general(ds, qb, tn, preferred_element_type=F32),
                          lax.dot_general(p.astype(BF16), dob, tn, preferred_element_type=F32)))

        dk_step, dv_step = parts[0]
        for dk_part, dv_part in parts[1:]:
            dk_step, dv_step = dk_step + dk_part, dv_step + dv_part

        @pl.when(i == 0)
        def _():
            dk_ref[...] = dk_step
            dv_ref[...] = dv_step

        @pl.when(i > 0)
        def _():
            dk_ref[...] += dk_step
            dv_ref[...] += dv_step

        if carry is not None:
            pl.when((pl.program_id(0) == heads - 1) & (pl.program_id(1) == nq - 1))(wait)

    ex = carry
    res = pl.pallas_call(
        body, name="attn_bwd", grid=(heads, nq),
        in_specs=[pl.BlockSpec((tq, HEAD_PAD), lambda h, i: (i, h)),
                  pl.BlockSpec((M, HEAD_PAD), lambda h, i: (0, h)),
                  pl.BlockSpec((M, vd), lambda h, i: (0, h)),
                  pl.BlockSpec((tq, vd), lambda h, i: (i, h)),
                  pl.BlockSpec((tq, vd), lambda h, i: (i, h)),
                  pl.BlockSpec((tq, vd), lambda h, i: (i, h))] + ([] if ex is None else ex.in_specs),
        out_specs=[pl.BlockSpec((tq, HEAD_PAD), lambda h, i: (i, h)),
                   pl.BlockSpec((M, HEAD_PAD), lambda h, i: (0, h)),
                   pl.BlockSpec((M, vd), lambda h, i: (0, h))] + ([] if ex is None else ex.out_specs),
        out_shape=[_sds((N, heads * HEAD_PAD), F32), _sds((M, heads * HEAD_PAD), F32),
                   _sds((M, heads * vd), F32)] + ([] if ex is None else ex.out_shape),
        scratch_shapes=[] if ex is None else ex.scratch,
        compiler_params=_params(("arbitrary", "arbitrary")),
    )(q, k, v, o, lse, do, *([] if ex is None else ex.xs))
    return res[0], res[1], res[2], list(res[3:])


def _comm_call(body, xs, out_shapes, n_sems, name, in_vmem):
    space = pltpu.VMEM if in_vmem else pl.ANY
    n = len(xs)

    def wrapped(*refs):
        body(refs[:n], refs[n:2 * n], *refs[2 * n:])

    return pl.pallas_call(
        wrapped, name=name, out_shape=list(out_shapes),
        in_specs=[pl.BlockSpec(memory_space=space)] * n, out_specs=[pl.BlockSpec(memory_space=space)] * n,
        scratch_shapes=[pltpu.SemaphoreType.DMA((n, n_sems)), pltpu.SemaphoreType.DMA((n, n_sems)),
                        pltpu.SemaphoreType.DMA((n,))],
        compiler_params=pltpu.CompilerParams(vmem_limit_bytes=VMEM_LIMIT_BYTES),
    )(*xs)


def _all_gather8(blks, *, name, in_vmem):
    def body(x_refs, out_refs, send_sems, recv_sems, local_sems):
        x, y, c = _place()
        me, sibling = (x, y, c), (x, y, 1 - c)
        chips = [(1 - x, y), (x, 1 - y), (1 - x, 1 - y)]
        waits = []
        for w, (x_ref, out_ref) in enumerate(zip(x_refs, out_refs)):
            def slot(px, py, pc, out_ref=out_ref):
                return out_ref.at[4 * px + 2 * py + pc]

            def copy(k, block, to, src=None, w=w, slot=slot):
                return pltpu.make_async_remote_copy(
                    src_ref=slot(*block) if src is None else src, dst_ref=slot(*block),
                    send_sem=send_sems.at[w, k], recv_sem=recv_sems.at[w, k], device_id=to, device_id_type=MESH)

            mine = pltpu.make_async_copy(x_ref, slot(*me), local_sems.at[w])
            mine.start()
            first = [copy(0, me, sibling, src=x_ref)]
            first += [copy(1 + j, me, (*chip, c), src=x_ref) for j, chip in enumerate(chips)]
            for cp in first:
                cp.start()
            waits.append((copy, mine, first))
        for copy, mine, first in waits:
            passed = [copy(4 + j, (*chip, c), sibling) for j, chip in enumerate(chips)]
            for j, chip in enumerate(chips):
                copy(1 + j, (*chip, c), me).wait_recv()
                passed[j].start()
            copy(0, sibling, me).wait_recv()
            for j, chip in enumerate(chips):
                copy(4 + j, (*chip, 1 - c), me).wait_recv()
            for cp in first + passed:
                cp.wait_send()
            mine.wait()

    return _comm_call(body, blks, [_sds((8,) + b.shape, b.dtype) for b in blks], 7, name, in_vmem)


def _gather_others(blks, *, name):
    def body(x_refs, out_refs, send_sems, recv_sems, local_sems):
        x, y, c = _place()
        own, xn, yn, dg = (x, y), (1 - x, y), (x, 1 - y), (1 - x, 1 - y)

        def slot(w, chip, core):
            return out_refs[w].at[4 * chip[0] + 2 * chip[1] + core]

        def cp(w, k, src, dst, chip, core):
            return pltpu.make_async_remote_copy(src_ref=src, dst_ref=dst, send_sem=send_sems.at[w, k],
                                                recv_sem=recv_sems.at[w, k], device_id=(*chip, core),
                                                device_id_type=MESH)

        def halves(w):
            h = x_refs[w].shape[0] // 2
            return pl.ds(0, h), pl.ds(h, h)

        sends = []
        for w, x_ref in enumerate(x_refs):
            sends += [cp(w, 0, x_ref, slot(w, own, c), xn, c), cp(w, 1, x_ref, slot(w, own, c), yn, c)]
        for s in sends:
            s.start()
        for w, x_ref in enumerate(x_refs):
            lo, hi = halves(w)
            cp(w, 1, x_ref, slot(w, yn, c), yn, c).wait_recv()
            passed = [cp(w, 2, slot(w, yn, c).at[lo], slot(w, yn, c).at[lo], xn, c),
                      cp(w, 4, slot(w, yn, c), slot(w, yn, c), own, 1 - c)]
            cp(w, 0, x_ref, slot(w, xn, c), xn, c).wait_recv()
            passed += [cp(w, 3, slot(w, xn, c).at[hi], slot(w, xn, c).at[hi], yn, c),
                       cp(w, 5, slot(w, xn, c), slot(w, xn, c), own, 1 - c)]
            for s in passed:
                s.start()
            sends += passed
        for w in range(len(x_refs)):
            lo, hi = halves(w)
            cp(w, 2, slot(w, dg, c).at[lo], slot(w, dg, c).at[lo], xn, c).wait_recv()
            cp(w, 3, slot(w, dg, c).at[hi], slot(w, dg, c).at[hi], yn, c).wait_recv()
            passed = [cp(w, 6, slot(w, dg, c), slot(w, dg, c), own, 1 - c)]
            passed[0].start()
            sends += passed
        for w in range(len(x_refs)):
            cp(w, 4, slot(w, yn, c), slot(w, yn, 1 - c), own, 1 - c).wait_recv()
            cp(w, 5, slot(w, xn, c), slot(w, xn, 1 - c), own, 1 - c).wait_recv()
            cp(w, 6, slot(w, dg, c), slot(w, dg, 1 - c), own, 1 - c).wait_recv()
        for s in sends:
            s.wait_send()

    return list(_comm_call(body, blks, [_sds((8,) + b.shape, b.dtype) for b in blks], 7, name, False))


def _exchange_alone(ex, *, name):
    def body(x_refs, out_refs, send_sems, recv_sems, local_sems):
        start, wait = ex.bind(x_refs, out_refs, send_sems, recv_sems)
        start()
        wait()

    return list(_comm_call(body, ex.xs, ex.out_shape, 3, name, False))


def _block_rows(rows, row_bytes, target=1 << 21, align=BF16_SUBLANES):
    return _tile(rows, max(align, target // row_bytes // align * align), align)


def _sum_blocks(buf, *, name, out_dtype):
    B, R, C = buf.shape
    tm = _block_rows(R, B * C * buf.dtype.itemsize)

    def body(x_ref, o_ref):
        acc = x_ref[0].astype(F32)
        for b in range(1, B):
            acc = acc + x_ref[b].astype(F32)
        o_ref[...] = acc.astype(o_ref.dtype)

    return pl.pallas_call(
        body, name=name, grid=(R // tm,), in_specs=[pl.BlockSpec((B, tm, C), lambda i: (0, i, 0))],
        out_specs=pl.BlockSpec((tm, C), lambda i: (i, 0)), out_shape=_sds((R, C), out_dtype),
        compiler_params=_params(("parallel",)),
    )(buf)


def _sum_chips(received, sent, chip, *, name):
    _, R, C = received.shape
    tm = _block_rows(R, 5 * C * received.dtype.itemsize)

    def body(chip_ref, r0, r1, r2, r3, own_ref, o_ref):
        acc = None
        for q, r in enumerate((r0, r1, r2, r3)):
            term = jnp.where(q == chip_ref[0], own_ref[...], r[...]).astype(F32)
            acc = term if acc is None else acc + term
        o_ref[...] = acc

    def slot(q):
        return pl.BlockSpec((None, tm, C), lambda i, ch, q=q: (jnp.where(q == ch[0], (q + 1) % 4, q), i, 0))

    return pl.pallas_call(
        body, name=name, out_shape=_sds((R, C), F32),
        grid_spec=pltpu.PrefetchScalarGridSpec(
            num_scalar_prefetch=1, grid=(R // tm,),
            in_specs=[slot(0), slot(1), slot(2), slot(3), pl.BlockSpec((None, tm, C), lambda i, ch: (ch[0], i, 0))],
            out_specs=pl.BlockSpec((tm, C), lambda i, ch: (i, 0))),
        compiler_params=_params(("arbitrary",)),
    )(chip, received, received, received, received, sent)


def _pair_add(mine, theirs, core, *, name):
    _, _, R, C = mine.shape
    tm = _block_rows(R, C * 2)

    def body(core_ref, a_ref, b_ref, o_ref):
        o_ref[...] = (a_ref[...].astype(F32) + b_ref[...].astype(F32)).astype(o_ref.dtype)

    return pl.pallas_call(
        body, name=name, out_shape=_sds(theirs.shape, BF16),
        grid_spec=pltpu.PrefetchScalarGridSpec(
            num_scalar_prefetch=1, grid=(4, R // tm),
            in_specs=[pl.BlockSpec((None, None, tm, C), lambda q, i, core_ref: (q, core_ref[0], i, 0)),
                      pl.BlockSpec((None, tm, C), lambda q, i, core_ref: (q, i, 0))],
            out_specs=pl.BlockSpec((None, tm, C), lambda q, i, core_ref: (q, i, 0))),
        compiler_params=_params(("parallel", "parallel")),
    )(core, mine, theirs)


def _assemble(gathered, own, chip, *, name, transpose):
    _, K, Ns = gathered.shape
    tm = _block_rows(K, Ns * 4)

    def body(chip_ref, g_ref, own_ref, o_ref):
        q = pl.program_id(0)

        @pl.when(q == chip_ref[0])
        def _():
            o_ref[...] = own_ref[...].astype(BF16)

        @pl.when(q != chip_ref[0])
        def _():
            o_ref[...] = g_ref[...]

    if transpose:
        out_spec = pl.BlockSpec((tm, Ns), lambda q, i, ch: (i, q))
        out_shape = _sds((K, 4 * Ns), BF16)
    else:
        out_spec = pl.BlockSpec((None, tm, Ns), lambda q, i, ch: (q, i, 0))
        out_shape = _sds((4, K, Ns), BF16)
    return pl.pallas_call(
        body, name=name, out_shape=out_shape,
        grid_spec=pltpu.PrefetchScalarGridSpec(
            num_scalar_prefetch=1, grid=(4, K // tm),
            in_specs=[pl.BlockSpec((None, tm, Ns), lambda q, i, ch: (jnp.where(q == ch[0], (q + 1) % 4, q), i, 0)),
                      pl.BlockSpec((tm, Ns), lambda q, i, ch: (jnp.where(q == ch[0], i, 0), 0))],
            out_specs=out_spec),
        compiler_params=_params(("arbitrary", "arbitrary")),
    )(chip, gathered, own)


def _assemble_halves(mine, theirs, own, place, *, name, transpose):
    _, K2, Ns = mine.shape
    tm = _block_rows(K2, Ns * 4, target=1 << 22)
    nb = K2 // tm

    def body(place_ref, m_ref, t_ref, own_ref, o_ref):
        q, hb = pl.program_id(0), pl.program_id(1)
        is_own = q == place_ref[0]
        is_mine = hb == place_ref[1]

        @pl.when(is_own)
        def _():
            o_ref[...] = own_ref[...].astype(BF16)

        @pl.when(jnp.logical_not(is_own) & is_mine)
        def _():
            o_ref[...] = m_ref[...]

        @pl.when(jnp.logical_not(is_own) & jnp.logical_not(is_mine))
        def _():
            o_ref[...] = t_ref[...]

    def other(q, pr):
        return jnp.where(q == pr[0], (q + 1) % 4, q)

    if transpose:
        out_spec = pl.BlockSpec((tm, Ns), lambda q, hb, i, pr: (hb * nb + i, q))
        out_shape = _sds((2 * K2, 4 * Ns), BF16)
    else:
        out_spec = pl.BlockSpec((None, tm, Ns), lambda q, hb, i, pr: (q, hb * nb + i, 0))
        out_shape = _sds((4, 2 * K2, Ns), BF16)
    return pl.pallas_call(
        body, name=name, out_shape=out_shape,
        grid_spec=pltpu.PrefetchScalarGridSpec(
            num_scalar_prefetch=1, grid=(4, 2, nb),
            in_specs=[pl.BlockSpec((None, tm, Ns), lambda q, hb, i, pr: (other(q, pr), jnp.where(hb == pr[1], i, 0), 0)),
                      pl.BlockSpec((None, tm, Ns), lambda q, hb, i, pr: (other(q, pr), jnp.where(hb == pr[1], 0, i), 0)),
                      pl.BlockSpec((tm, Ns), lambda q, hb, i, pr: (jnp.where(q == pr[0], hb * nb + i, 0), 0))],
            out_specs=out_spec),
        compiler_params=_params(("arbitrary",) * 3),
    )(place, mine, theirs, own)


def _split_lanes(row, widths):
    out, off = [], 0
    for wd in widths:
        out.append(row[:, off:off + wd])
        off += wd
    return out


def _adamw_math(w, g, m, v):
    m = ADAM_B1 * m + (1.0 - ADAM_B1) * g
    v = ADAM_B2 * v + (1.0 - ADAM_B2) * (g * g)
    m_hat = m / (1.0 - ADAM_B1 ** ADAM_STEP)
    v_hat = v / (1.0 - ADAM_B2 ** ADAM_STEP)
    delta = -ADAM_LR * (m_hat / (jnp.sqrt(v_hat) + ADAM_EPS) + ADAM_WD * w)
    return delta, m, v


def _adamw_halves(w, mine, other, m, v, core, *, name):
    K, Ns = w.shape
    tm = _block_rows(K // 2, Ns * 4, target=1 << 20, align=F32_SUBLANES)
    nb = (K // 2) // tm

    def body(core_ref, w_ref, mine_ref, other_ref, m_ref, v_ref, g_out, d_out, m_out, v_out):
        g = jnp.where(pl.program_id(0) // nb == core_ref[0], mine_ref[...], other_ref[...])
        g_out[...] = g
        d_out[...], m_out[...], v_out[...] = _adamw_math(w_ref[...], g, m_ref[...], v_ref[...])

    row = pl.BlockSpec((tm, Ns), lambda i, cr: (i, 0))
    return pl.pallas_call(
        body, name=name, out_shape=[_sds((K, Ns), F32)] * 4,
        grid_spec=pltpu.PrefetchScalarGridSpec(
            num_scalar_prefetch=1, grid=(K // tm,),
            in_specs=[row,
                      pl.BlockSpec((tm, Ns), lambda i, cr: (jnp.where(i // nb == cr[0], i % nb, 0), 0)),
                      pl.BlockSpec((None, tm, Ns), lambda i, cr: (1 - cr[0], jnp.where(i // nb == cr[0], 0, i % nb), 0)),
                      row, row],
            out_specs=[row, row, row, row]),
        compiler_params=_params(("arbitrary",)),
    )(core, w, mine, other, m, v)


def _adamw(w, g, m, v, *, name, carry=None):
    C = w.shape[1]

    def fn(w, g, m, v):
        return _adamw_math(w, g, m, v), ()

    tm = max(F32_SUBLANES, min(512, (1 << 20) // (4 * C) // F32_SUBLANES * F32_SUBLANES))
    res = _rowwise(fn, [w, g, m, v], [], [(C, F32)] * 3, name=name, tm=tm, carry=carry)
    return tuple(res[0]) + ((res[2],) if carry is not None else ())


def _rope_tables(n):
    rows = n // GRID_W
    row = jnp.repeat(jnp.arange(rows, dtype=F32), GRID_W)
    col = jnp.tile(jnp.arange(GRID_W, dtype=F32), rows)
    nf = ROPE_DIM // 4
    freqs = ROPE_THETA ** (-jnp.arange(nf, dtype=F32) / nf)
    ang_r, ang_c = row[:, None] * freqs[None, :], col[:, None] * freqs[None, :]
    cr, sr, cc, sc = jnp.cos(ang_r), jnp.sin(ang_r), jnp.cos(ang_c), jnp.sin(ang_c)
    nope = HEAD_PAD - 2 * ROPE_DIM
    one, zero, z = jnp.ones((n, nope), F32), jnp.zeros((n, nope), F32), jnp.zeros((n, nf), F32)
    pad = jnp.zeros((n, ROPE_DIM), F32)
    cos = jnp.concatenate([one, cr, cr, cc, cc, pad], axis=1)
    s_lo = jnp.concatenate([zero, -sr, z, -sc, z, pad], axis=1)
    s_hi = jnp.concatenate([zero, z, sr, z, sc, pad], axis=1)
    return cos, s_lo, s_hi


def _rope(n, cos, s_lo, s_hi):
    q = ROPE_DIM // 4
    return n * cos + pltpu.roll(n, HEAD_PAD - q, 1) * s_lo + pltpu.roll(n, q, 1) * s_hi


def _rope_t(d, cos, s_lo, s_hi):
    q = ROPE_DIM // 4
    return d * cos + pltpu.roll(d * s_lo, q, 1) + pltpu.roll(d * s_hi, HEAD_PAD - q, 1)


def kernel(x, c, ctx, c_ctx, w_mod, b_mod, norm1_g, w_in, q_norm_g, kv_norm_g, w_uq, w_ukv, qk_norm_q, qk_norm_k, sgu_norm_g, sgu_norm_b, w_spatial, b_spatial, w_br_attn, w_br_sgu, w_out, norm2_g, w_ffn_in, w_ffn_out, loss_target, m_c_ctx, m_w_mod, m_b_mod, m_norm1_g, m_w_in, m_q_norm_g, m_kv_norm_g, m_w_uq, m_w_ukv, m_qk_norm_q, m_qk_norm_k, m_sgu_norm_g, m_sgu_norm_b, m_w_spatial, m_b_spatial, m_w_br_attn, m_w_br_sgu, m_w_out, m_norm2_g, m_w_ffn_in, m_w_ffn_out, v_c_ctx, v_w_mod, v_b_mod, v_norm1_g, v_w_in, v_q_norm_g, v_kv_norm_g, v_w_uq, v_w_ukv, v_qk_norm_q, v_qk_norm_k, v_sgu_norm_g, v_sgu_norm_b, v_w_spatial, v_b_spatial, v_w_br_attn, v_w_br_sgu, v_w_out, v_norm2_g, v_w_ffn_in, v_w_ffn_out):
    ax, ay, ac = _place()
    my_chip = 2 * ax + ay
    my_dev = 4 * ax + 2 * ay + ac

    N, D = x.shape[1], x.shape[2]
    CT = ctx.shape[1]
    M = N + CT
    QL, KVL, QK = q_norm_g.shape[-1], kv_norm_g.shape[-1], qk_norm_q.shape[-1]
    NOPE = QK - ROPE_DIM
    VD = NOPE
    H = 4 * w_uq.shape[-1] // QK
    SW, G, CH = sgu_norm_g.shape[-1], w_spatial.shape[1], w_spatial.shape[2]
    GD = SW // G
    DFF = 4 * w_ffn_out.shape[1]
    NMOD = 4 * w_mod.shape[-1]
    NM = w_mod.shape[-1]
    KVP = KVL + 2 * ROPE_DIM
    assert NOPE == LANES and GD == LANES and HEAD_PAD == NOPE + 2 * ROPE_DIM and CH == LANES
    scale = QK ** -0.5

    x2, ctx2, tgt2 = x[0], ctx[0], loss_target[0]

    c_all = _all_gather8([c], name="ag_c", in_vmem=True)[0][:, 0, :]
    c_rows = jnp.concatenate([c_all, c_ctx[None, :], jnp.zeros((BF16_SUBLANES - 9, D), F32)], axis=0)

    def silu_fn(t):
        s = _sigmoid(t)
        return (t * s, s * (1.0 + t * (1.0 - s))), ()

    (silu_c, dsilu_c), _ = _rowwise(silu_fn, [c_rows], [], [(D, F32), (D, F32)], name="silu_c", tm=16)
    wm = w_mod[0]
    mod_loc = _mm([(silu_c, wm)], name="mod_fwd", outs=(F32,), tn=512, tk=512,
                  extras=[(lax.dynamic_slice_in_dim(b_mod, my_chip * NM, NM, axis=1), "n")],
                  epi=lambda acc, b: (acc + b,))
    mod_all = _all_gather8([mod_loc], name="ag_mod", in_vmem=True)[0]
    mod_full = jnp.concatenate([mod_all[0], mod_all[2], mod_all[4], mod_all[6]], axis=1)
    mod_me = lax.dynamic_slice_in_dim(mod_full, my_dev, 1, axis=0)
    sh1, sc1, g1, sh2, sc2, g2 = [mod_me[:, i * D:(i + 1) * D] for i in range(6)]
    sh1c, sc1c = mod_full[8:9, :D], mod_full[8:9, D:2 * D]

    big = [w_in[0], w_uq[0], w_ukv[0], w_br_attn[0], w_br_sgu[0], w_out[0], w_ffn_in[0], w_ffn_out[0]]
    col_sharded = [True, True, True, True, True, False, True, False]
    halves = [lax.dynamic_slice_in_dim(a, ac * (a.shape[0] // 2), a.shape[0] // 2, axis=0).astype(BF16) for a in big]
    tags = ["w_in", "w_uq", "w_ukv", "w_br_attn", "w_br_sgu", "w_out", "w_ffn_in", "w_ffn_out"]
    first_group, attn_group, ffn_group = [0, 1, 2], [3, 4, 5, 6], [7]
    chip1 = jnp.reshape(my_chip, (1,)).astype(jnp.int32)
    place2 = jnp.stack([my_chip, ac]).astype(jnp.int32)

    def laid_out(seg, i):
        a = big[i]
        if col_sharded[i] and seg.ndim == 3:
            return seg.transpose(1, 0, 2).reshape(a.shape[0], 4 * a.shape[1])
        return seg if col_sharded[i] else seg.reshape(4 * a.shape[0], a.shape[1])

    def side_by_side(i):
        return col_sharded[i] and big[i].shape[1] % LANES == 0

    def finish_gather(idx, mine4, theirs4):
        return [laid_out(_assemble_halves(m, t, big[i], place2, name="assemble_" + tags[i], transpose=side_by_side(i)), i)
                for i, m, t in zip(idx, mine4, theirs4)]

    gathered = _gather_others([halves[i] for i in first_group], name="ag_weights")
    w_in_f, w_uq_f, w_ukv_f = [
        laid_out(_assemble(seg.reshape((4,) + big[i].shape), big[i], chip1, name="assemble_" + tags[i],
                           transpose=side_by_side(i)), i) for i, seg in zip(first_group, gathered)]
    o_kv, o_u = QL, QL + KVL + ROPE_DIM
    o_v, o_g = o_u + SW, o_u + 2 * SW
    w_q = w_in_f[:, :QL]
    w_kv = jnp.pad(w_in_f[:, o_kv:o_u], ((0, 0), (0, ROPE_DIM)))
    w_u, w_v = w_in_f[:, o_u:o_v], w_in_f[:, o_v:o_g]
    w_g1, w_g2 = w_in_f[:, o_g:o_g + D], w_in_f[:, o_g + D:]
    w_uq_p = jnp.pad(w_uq_f.reshape(QL, H, QK), ((0, 0), (0, 0), (0, HEAD_PAD - QK))).reshape(QL, H * HEAD_PAD)

    cos_t, slo_t, shi_t = _rope_tables(N)
    ones_c = jnp.concatenate([jnp.ones((CT, NOPE + ROPE_DIM), F32), jnp.zeros((CT, ROPE_DIM), F32)], axis=1)
    cos_k = jnp.concatenate([cos_t, ones_c], axis=0)
    slo_k = jnp.concatenate([slo_t, jnp.zeros((CT, HEAD_PAD), F32)], axis=0)
    shi_k = jnp.concatenate([shi_t, jnp.zeros((CT, HEAD_PAD), F32)], axis=0)
    gq_p = jnp.pad(qk_norm_q, ((0, 0), (0, HEAD_PAD - QK)))
    gk_p = jnp.pad(qk_norm_k, ((0, 0), (0, HEAD_PAD - QK)))

    def norm_mod_fn(t, g, sh, sc):
        r = _rms_stats(t, D)
        return (((t * r) * g) * (1.0 + sc) + sh,), ()

    (h,), _ = _rowwise(norm_mod_fn, [x2], [norm1_g, sh1, sc1], [(D, BF16)], name="norm1_x")
    (ctx_h,), _ = _rowwise(norm_mod_fn, [ctx2], [norm1_g, sh1c, sc1c], [(D, BF16)], name="norm1_ctx")

    qc = _mm([(h, w_q)], name="proj_q", outs=(F32,))
    kvin = jnp.concatenate([_mm([(h, w_kv)], name="proj_kv", outs=(F32,)),
                            _mm([(ctx_h, w_kv)], name="proj_kv_ctx", outs=(F32,))], axis=0)
    u_in = _mm([(h, w_u)], name="proj_u", outs=(BF16,))
    v_in = _mm([(h, w_v)], name="proj_v", outs=(BF16,))
    g1_in, (mine_bra,) = _mm([(h, w_g1)], name="proj_g1", outs=(BF16,), carry=_ChipExchange([halves[3]], gather=True))
    g2_in, (mine_brs,) = _mm([(h, w_g2)], name="proj_g2", outs=(BF16,), carry=_ChipExchange([halves[4]], gather=True))

    def rms_gain_fn(width):
        def fn(t, g):
            return (((t * _rms_stats(t, width)) * g),), ()
        return fn

    (qn,), _ = _rowwise(rms_gain_fn(QL), [qc], [q_norm_g], [(QL, BF16)], name="q_norm")

    def kv_norm_fn(t, g):
        kvc = t[:, :KVL]
        return (((kvc * _rms_stats(kvc, KVL)) * g),), ()

    (kvn,), _ = _rowwise(kv_norm_fn, [kvin], [kv_norm_g], [(KVL, BF16)], name="kv_norm")
    q_raw = _mm([(qn, w_uq_p)], name="q_up", outs=(F32,))
    kv_raw = _mm([(kvn, w_ukv_f)], name="kv_up", outs=(F32,))

    def q_post_fn(t, cos, slo, shi, g):
        outs = []
        for hd in range(H):
            th = t[:, hd * HEAD_PAD:(hd + 1) * HEAD_PAD]
            outs.append(_rope((th * _rms_stats(th, QK)) * g, cos, slo, shi) * scale)
        return (jnp.concatenate(outs, axis=1),), ()

    (q_att,), _ = _rowwise(q_post_fn, [q_raw, cos_t, slo_t, shi_t], [gq_p], [(H * HEAD_PAD, BF16)], name="q_post")

    def k_post_fn(t, kvi, cos, slo, shi, g):
        kr = kvi[:, KVL:]
        ks, vs = [], []
        for hd in range(H):
            th = jnp.concatenate([t[:, hd * HEAD_PAD:hd * HEAD_PAD + NOPE], kr], axis=1)
            ks.append(_rope((th * _rms_stats(th, QK)) * g, cos, slo, shi))
            vs.append(t[:, hd * HEAD_PAD + NOPE:(hd + 1) * HEAD_PAD])
        return (jnp.concatenate(ks, axis=1), jnp.concatenate(vs, axis=1)), ()

    (k_att, v_att), _, (mine_out,) = _rowwise(k_post_fn, [kv_raw, kvin, cos_k, slo_k, shi_k], [gk_p],
                                              [(H * HEAD_PAD, BF16), (H * VD, BF16)], name="k_post",
                                              carry=_ChipExchange([halves[5]], gather=True))
    attn_o, lse, (mine_ffi,) = _attn_fwd(q_att, k_att, v_att, heads=H, carry=_ChipExchange([halves[6]], gather=True))
    mine4 = [mine_bra, mine_brs, mine_out, mine_ffi]

    ws3 = w_spatial[0]
    bs_t = jnp.pad(b_spatial[0].T, ((0, 0), (0, LANES - G)))

    def sgu_parts(u_in, v_in, ng, nb):
        u, v = _gelu(u_in.astype(F32)), _gelu(v_in.astype(F32))
        mu = jnp.mean(v, axis=-1, keepdims=True)
        vc = v - mu
        rs = lax.rsqrt(jnp.mean(vc * vc, axis=-1, keepdims=True) + EPS)
        xhat = vc * rs
        return u, xhat, rs, (xhat * ng + nb).astype(BF16)

    def sgu_fwd_fn(u_in, v_in, ng, nb, ws, bst):
        u, _, _, vnb = sgu_parts(u_in, v_in, ng, nb)
        outs = []
        for g in range(G):
            sl = slice(g * GD, (g + 1) * GD)
            mixed = jnp.dot(ws[g].astype(BF16), vnb[:, sl], preferred_element_type=F32) + bst[:, g:g + 1]
            outs.append(u[:, sl] * mixed)
        return (jnp.concatenate(outs, axis=1),), ()

    (sgu_o,), _, theirs4 = _rowwise(sgu_fwd_fn, [u_in, v_in], [sgu_norm_g, sgu_norm_b, ws3, bs_t], [(SW, BF16)],
                                    name="sgu_fwd", tm=CH, carry=_PairExchange(mine4, "forward"))
    w_bra, w_brs, w_out_f, w_ffi = finish_gather(attn_group, mine4, theirs4)
    w_fa, w_fb = w_ffi[:, :DFF], w_ffi[:, DFF:]

    a1 = _mm([(attn_o, w_bra)], name="br_attn", outs=(BF16,))
    def merge_epi(acc, a1v, gi1, gi2):
        return acc, _sigmoid(gi1.astype(F32)) * a1v.astype(F32) + _sigmoid(gi2.astype(F32)) * acc

    a2, merged = _mm([(sgu_o, w_brs)], name="br_sgu", outs=(BF16, BF16),
                     extras=[(a1, "mn"), (g1_in, "mn"), (g2_in, "mn")], epi=merge_epi)

    def res_gate(acc, res, gate):
        return res + gate * acc, acc

    x1, mo = _mm([(merged, w_out_f)], name="out_proj", outs=(F32, BF16), tn=1024,
                 extras=[(x2, "mn"), (g1, "n")], epi=res_gate)
    (h2,), _ = _rowwise(norm_mod_fn, [x1], [norm2_g, sh2, sc2], [(D, BF16)], name="norm2")

    def swiglu_epi(a, b):
        return a, b, (a * _sigmoid(a)) * b

    (fa, fb, act), mine4 = _mm([(h2, w_fa, w_fb)], name="ffn_in", outs=(BF16, BF16, BF16), tn=512, epi=swiglu_epi,
                               carry=_ChipExchange([halves[i] for i in ffn_group], gather=True))
    (w_ffo,) = finish_gather(ffn_group, mine4, _exchange_alone(_PairExchange(mine4, "forward"), name="ag_forward_ffn"))
    def loss_epi(acc, res, t, gate):
        e = (res + gate * acc) - t
        dy = e * (1.0 / D)
        return dy, gate * dy, _colsum(e * e) * (0.5 / D), _colsum(dy * acc)

    dy, df, loss_part, dg2_part = _mm([(act, w_ffo)], name="ffn_out", outs=(F32, BF16), tn=1024, col_sums=2,
                                      extras=[(x1, "mn"), (tgt2, "mn"), (g2, "n")], epi=loss_epi)

    def fold_fn(a, b):
        return (), (_colsum(a), _colsum(b))

    _, (loss_cols, dg2) = _rowwise(fold_fn, [loss_part[:, 0, :], dg2_part[:, 0, :]], [], [], [(1, D), (1, D)],
                                   name="loss_fold", tm=loss_part.shape[0])

    def swiglu_bwd_epi(dact, a, b):
        a, b = a.astype(F32), b.astype(F32)
        s = _sigmoid(a)
        return dact * b * (s * (1.0 + a * (1.0 - s))), dact * (a * s)

    da, db = _mm([(df, w_ffo)], tb=True, name="ffn_out_dx", outs=(BF16, BF16), tn=512,
                 extras=[(fa, "mn"), (fb, "mn")], epi=swiglu_bwd_epi)
    dw_ffo = _mm([(act, df)], ta=True, name="ffn_out_dw", outs=(BF16,))
    dh2 = _mm([(da, w_fa), (db, w_fb)], tb=True, name="ffn_in_dx", outs=(F32,))
    ns_ffi = w_ffn_in.shape[-1]
    dw_ffi = _mm([(h2, da)], ta=True, name="ffn_in_dw_a", outs=(BF16,), tn=1408, split=ns_ffi,
                 into=(lax.empty((4, D, ns_ffi), BF16), 0))
    dw_ffi = _mm([(h2, db)], ta=True, name="ffn_in_dw_b", outs=(BF16,), tn=1408, split=ns_ffi, into=(dw_ffi, 2))

    def norm2_bwd_fn(dh, t, dyv, mov, g, sc, g1v):
        r = _rms_stats(t, D)
        tn = t * r
        dxg = dh * (1.0 + sc)
        dt = dyv + _rms_bwd(dxg * g, tn, r, D)
        return (dt, g1v * dt), (_colsum(dh), _colsum(dh * (tn * g)), _colsum(dxg * tn), _colsum(dt * mov.astype(F32)))

    (dx1, dmo), (dsh2, dsc2, dn2g, dg1) = _rowwise(
        norm2_bwd_fn, [dh2, x1, dy, mo], [norm2_g, sc2, g1], [(D, F32), (D, BF16)], [(1, D)] * 4, name="norm2_bwd")

    def merge_bwd_epi(dm, a1, a2, gi1, gi2):
        s1, s2 = _sigmoid(gi1.astype(F32)), _sigmoid(gi2.astype(F32))
        a1, a2 = a1.astype(F32), a2.astype(F32)
        return dm * s1, dm * s2, dm * a1 * (s1 * (1.0 - s1)), dm * a2 * (s2 * (1.0 - s2))

    da1, da2, dgi1, dgi2 = _mm([(dmo, w_out_f)], tb=True, name="out_proj_dx", outs=(BF16,) * 4, tn=512,
                               extras=[(a1, "mn"), (a2, "mn"), (g1_in, "mn"), (g2_in, "mn")], epi=merge_bwd_epi)
    dw_out = _mm([(merged, dmo)], ta=True, name="out_proj_dw", outs=(BF16,))
    dattn = _mm([(da1, w_bra)], tb=True, name="br_attn_dx", outs=(BF16,))
    dw_bra = _mm([(attn_o, da1)], ta=True, name="br_attn_dw", outs=(BF16,), split=w_br_attn.shape[-1])
    dsgu = _mm([(da2, w_brs)], tb=True, name="br_sgu_dx", outs=(BF16,))
    dw_brs = _mm([(sgu_o, da2)], ta=True, name="br_sgu_dw", outs=(BF16,), split=w_br_sgu.shape[-1])

    def sgu_bwd_fn(dso, u_in, v_in, ng, nb, ws, bst):
        u, xhat, rs, vnb = sgu_parts(u_in, v_in, ng, nb)
        dso = dso.astype(F32)
        lane = lax.broadcasted_iota(jnp.int32, (CH, LANES), 1)
        du, dvn, dws, dbs = [], [], [], jnp.zeros((CH, LANES), F32)
        for g in range(G):
            sl = slice(g * GD, (g + 1) * GD)
            wg = ws[g].astype(BF16)
            mixed = jnp.dot(wg, vnb[:, sl], preferred_element_type=F32) + bst[:, g:g + 1]
            du.append(dso[:, sl] * mixed)
            dmix = dso[:, sl] * u[:, sl]
            dmb = dmix.astype(BF16)
            dws.append(lax.dot_general(dmb, vnb[:, sl], (((1,), (1,)), ((), ())), preferred_element_type=F32))
            dbs = dbs + jnp.where(lane == g, jnp.sum(dmix, axis=1, keepdims=True), 0.0)
            dvn.append(lax.dot_general(wg, dmb, (((0,), (0,)), ((), ())), preferred_element_type=F32))
        du, dvn = jnp.concatenate(du, axis=1), jnp.concatenate(dvn, axis=1)
        dxh = dvn * ng
        dv = rs * (dxh - jnp.mean(dxh, axis=-1, keepdims=True) - xhat * jnp.mean(dxh * xhat, axis=-1, keepdims=True))
        return ((du * _gelu_grad(u_in.astype(F32)), dv * _gelu_grad(v_in.astype(F32))),
                (_colsum(dvn * xhat), _colsum(dvn), jnp.stack(dws), dbs))

    core = jnp.reshape(ac, (1,)).astype(jnp.int32)

    def dest_layout(dwf, i):
        K, Ns = big[i].shape
        if dwf.ndim == 2:
            dwf = dwf.reshape(K, 4, Ns).transpose(1, 0, 2) if col_sharded[i] else dwf.reshape(4, K, Ns)
        return dwf.reshape(4, 2, K // 2, Ns)

    def pair_sums(idx, g4, sib):
        return [_pair_add(g, s, core, name="rs_pair_add_" + tags[i]) for g, s, i in zip(g4, sib, idx)]

    early = [3, 4, 5, 6, 7]
    g4_early = [dest_layout(d, i) for d, i in zip([dw_bra, dw_brs, dw_out, dw_ffi, dw_ffo], early)]
    (du_in, dv_in), (d_sng, d_snb, d_ws, d_bs), sib_early = _rowwise(
        sgu_bwd_fn, [dsgu, u_in, v_in], [sgu_norm_g, sgu_norm_b, ws3, bs_t], [(SW, BF16), (SW, BF16)],
        [(1, SW), (1, SW), (G, CH, CH), (CH, LANES)], name="sgu_bwd", tm=CH, carry=_PairExchange(g4_early, "halves"))
    pair_early = pair_sums(early, g4_early, sib_early)
    dq_att, dk_att, dv_att, xchg_early = _attn_bwd(q_att, k_att, v_att, attn_o, lse, dattn, heads=H,
                                                   carry=_ChipExchange(pair_early, gather=False))

    def q_post_bwd_fn(dq, t, cos, slo, shi, g):
        outs, dg = [], jnp.zeros((1, HEAD_PAD), F32)
        for hd in range(H):
            sl = slice(hd * HEAD_PAD, (hd + 1) * HEAD_PAD)
            th = t[:, sl]
            r = _rms_stats(th, QK)
            tn = th * r
            dn = _rope_t(dq[:, sl] * scale, cos, slo, shi)
            dg = dg + _colsum(dn * tn)
            outs.append(_rms_bwd(dn * g, tn, r, QK))
        return (jnp.concatenate(outs, axis=1),), (dg,)

    (dq_raw,), (d_gq,) = _rowwise(q_post_bwd_fn, [dq_att, q_raw, cos_t, slo_t, shi_t], [gq_p],
                                  [(H * HEAD_PAD, BF16)], [(1, HEAD_PAD)], name="q_post_bwd")

    def k_post_bwd_fn(dk, dv, t, kvi, cos, slo, shi, g):
        kr = kvi[:, KVL:]
        outs, dg, dkr = [], jnp.zeros((1, HEAD_PAD), F32), jnp.zeros_like(kr)
        for hd in range(H):
            th = jnp.concatenate([t[:, hd * HEAD_PAD:hd * HEAD_PAD + NOPE], kr], axis=1)
            r = _rms_stats(th, QK)
            tn = th * r
            dn = _rope_t(dk[:, hd * HEAD_PAD:(hd + 1) * HEAD_PAD], cos, slo, shi)
            dg = dg + _colsum(dn * tn)
            dt = _rms_bwd(dn * g, tn, r, QK)
            dkr = dkr + dt[:, NOPE:]
            outs += [dt[:, :NOPE], dv[:, hd * VD:(hd + 1) * VD]]
        return (jnp.concatenate(outs, axis=1), dkr), (dg,)

    def reduced_halves(idx, xchg, pair):
        return [_sum_chips(t4, pr, chip1, name="rs_sum_" + tags[i]) for t4, pr, i in zip(xchg, pair, idx)]

    red_early = reduced_halves(early, xchg_early, pair_early)
    (dkv_raw, dkrope), (d_gk,), other_early = _rowwise(
        k_post_bwd_fn, [dk_att, dv_att, kv_raw, kvin, cos_k, slo_k, shi_k], [gk_p],
        [(H * HEAD_PAD, BF16), (2 * ROPE_DIM, F32)], [(1, HEAD_PAD)], name="k_post_bwd",
        carry=_PairExchange(red_early, "gather"))

    dqn = _mm([(dq_raw, w_uq_p)], tb=True, name="q_up_dx", outs=(F32,))
    dw_uq_p = _mm([(qn, dq_raw)], ta=True, name="q_up_dw", outs=(BF16,))
    dkvn = _mm([(dkv_raw, w_ukv_f)], tb=True, name="kv_up_dx", outs=(F32,))
    dw_ukv = _mm([(kvn, dkv_raw)], ta=True, name="kv_up_dw", outs=(BF16,))

    def q_norm_bwd_fn(dn, t, g):
        r = _rms_stats(t, QL)
        tn = t * r
        return (_rms_bwd(dn * g, tn, r, QL),), (_colsum(dn * tn),)

    (dqc,), (d_qng,) = _rowwise(q_norm_bwd_fn, [dqn, qc], [q_norm_g], [(QL, BF16)], [(1, QL)], name="q_norm_bwd")

    def kv_norm_bwd_fn(dn, dkr, t, g):
        kvc = t[:, :KVL]
        r = _rms_stats(kvc, KVL)
        tn = kvc * r
        return (jnp.concatenate([_rms_bwd(dn * g, tn, r, KVL), dkr], axis=1),), (_colsum(dn * tn),)

    (dkvin,), (d_kvng,) = _rowwise(kv_norm_bwd_fn, [dkvn, dkrope, kvin], [kv_norm_g], [(KVP, BF16)], [(1, KVL)],
                                   name="kv_norm_bwd")
    dkvin_x, dkvin_c = dkvin[:N], dkvin[N:]

    dctx_h = _mm([(dkvin_c, w_kv)], tb=True, name="proj_kv_ctx_dx", outs=(F32,))
    dw_q = _mm([(h, dqc)], ta=True, name="proj_q_dw", outs=(BF16,))
    dw_kv = _mm([(h, dkvin_x), (ctx_h, dkvin_c)], ta=True, name="proj_kv_dw", outs=(BF16,))
    dw_u = _mm([(h, du_in)], ta=True, name="proj_u_dw", outs=(BF16,))
    dw_v = _mm([(h, dv_in)], ta=True, name="proj_v_dw", outs=(BF16,))
    dw_g1 = _mm([(h, dgi1)], ta=True, name="proj_g1_dw", outs=(BF16,))
    dw_g2 = _mm([(h, dgi2)], ta=True, name="proj_g2_dw", outs=(BF16,))

    dw_in_f = jnp.concatenate([dw_q, dw_kv[:, :KVL + ROPE_DIM], dw_u, dw_v, dw_g1, dw_g2], axis=1)
    dw_uq_f = dw_uq_p.reshape(QL, H, HEAD_PAD)[:, :, :QK].reshape(QL, H * QK)
    late = [0, 1, 2]
    g4_late = [dest_layout(d, i) for d, i in zip([dw_in_f, dw_uq_f, dw_ukv], late)]
    pair_late = pair_sums(late, g4_late, _exchange_alone(_PairExchange(g4_late, "halves"), name="rs_pair_late"))
    dh, xchg_late = _mm([(dqc, w_q), (dkvin_x, w_kv), (du_in, w_u), (dv_in, w_v), (dgi1, w_g1), (dgi2, w_g2)],
                        tb=True, name="proj_dx", outs=(F32,), tn=1024, tk=512,
                        carry=_ChipExchange(pair_late, gather=False))

    def norm1_bwd_fn(dhv, t, dres, g, sc):
        r = _rms_stats(t, D)
        tn = t * r
        dxg = dhv * (1.0 + sc)
        return (dres + _rms_bwd(dxg * g, tn, r, D),), (_colsum(dhv), _colsum(dhv * (tn * g)), _colsum(dxg * tn))

    (grad_x,), (dsh1, dsc1, dn1g_x) = _rowwise(norm1_bwd_fn, [dh, x2, dx1], [norm1_g, sc1], [(D, F32)], [(1, D)] * 3,
                                               name="norm1_bwd")
    _, (dsh1c, dsc1c, dn1g_c) = _rowwise(norm1_bwd_fn, [dctx_h, ctx2, jnp.zeros_like(ctx2)], [norm1_g, sc1c],
                                         [(D, F32)], [(1, D)] * 3, name="norm1_ctx_bwd")

    small = [dsh1, dsc1, dg1, dsh2, dsc2, dg2,
             dsh1c, dsc1c, dn1g_x, dn1g_c, d_qng, d_kvng, d_gq, d_gk, d_sng, d_snb, dn2g, loss_cols]
    small_sizes = [a.shape[1] for a in small]
    sm_row = jnp.concatenate(small, axis=1)
    sm_mat = jnp.concatenate([d_ws.reshape(G * CH, CH), d_bs], axis=0)
    row_all, mat_all = _all_gather8([sm_row, sm_mat], name="ag_small", in_vmem=True)
    row_sum = _sum_blocks(row_all, name="sum_small_rows", out_dtype=F32)
    mat_sum = _sum_blocks(mat_all, name="sum_small_mats", out_dtype=F32)
    dmod_rows = row_all[:, 0, :NMOD]
    (_, _, _, _, _, _, t_sh1c, t_sc1c, t_n1x, t_n1c, g_qng, g_kvng, t_gq, t_gk, g_sng, g_snb, g_n2g,
     t_loss) = _split_lanes(row_sum, small_sizes)
    g_ws, t_bs = mat_sum[:G * CH], mat_sum[G * CH:]
    dmodc_row = jnp.concatenate([t_sh1c, t_sc1c, jnp.zeros((1, NMOD - 2 * D), F32)], axis=1)
    dmod16 = jnp.concatenate([dmod_rows, dmodc_row, jnp.zeros((BF16_SUBLANES - 9, NMOD), F32)], axis=0)

    def small_fn(rows, n1x, n1c, lossv):
        return (), (_colsum(rows), n1x + n1c, jnp.sum(lossv, axis=1, keepdims=True))

    _, (g_bmod, g_n1g, loss11) = _rowwise(small_fn, [dmod16], [t_n1x, t_n1c, t_loss], [], [(1, NMOD), (1, D), (1, 1)],
                                          name="small_reduce", tm=16)
    dmod_loc = lax.dynamic_slice_in_dim(dmod16, my_chip * NM, NM, axis=1)
    g_wmod = _mm([(silu_c, dmod_loc)], ta=True, name="mod_dw", outs=(F32,), tn=512)
    dsilu_part = _mm([(dmod_loc, wm)], tb=True, name="mod_dx", outs=(F32,), tk=512)
    part_all = _all_gather8([dsilu_part[8:9]], name="ag_cctx", in_vmem=True)[0]

    def cctx_fn(parts, dsl):
        return (), ((parts[0:1] + parts[2:3] + parts[4:5] + parts[6:7]) * dsl,)

    _, (g_cctx,) = _rowwise(cctx_fn, [part_all[:, 0, :]], [dsilu_c[8:9]], [], [(1, D)], name="cctx_grad", tm=8)

    red_late = reduced_halves(late, xchg_late, pair_late)
    other_late = _exchange_alone(_PairExchange(red_late, "gather"), name="rs_halves_late")
    grad_halves = dict(zip(tags, zip(red_late + red_early, other_late + other_early)))

    mod_upd = _adamw(w_mod[0], g_wmod, m_w_mod[0], v_w_mod[0], name="adamw_w_mod")
    grads = dict(
        c_ctx=g_cctx.reshape(D), w_mod=g_wmod[None], b_mod=g_bmod, norm1_g=g_n1g,
        q_norm_g=g_qng, kv_norm_g=g_kvng, qk_norm_q=t_gq[:, :QK], qk_norm_k=t_gk[:, :QK], sgu_norm_g=g_sng,
        sgu_norm_b=g_snb, w_spatial=g_ws.reshape(w_spatial.shape), b_spatial=t_bs[:, :G].T[None], norm2_g=g_n2g)
    weights = dict(c_ctx=c_ctx, w_mod=w_mod, b_mod=b_mod, norm1_g=norm1_g, w_in=w_in, q_norm_g=q_norm_g,
                   kv_norm_g=kv_norm_g, w_uq=w_uq, w_ukv=w_ukv, qk_norm_q=qk_norm_q, qk_norm_k=qk_norm_k,
                   sgu_norm_g=sgu_norm_g, sgu_norm_b=sgu_norm_b, w_spatial=w_spatial, b_spatial=b_spatial,
                   w_br_attn=w_br_attn, w_br_sgu=w_br_sgu, w_out=w_out, norm2_g=norm2_g, w_ffn_in=w_ffn_in,
                   w_ffn_out=w_ffn_out)
    m_in = dict(c_ctx=m_c_ctx, w_mod=m_w_mod, b_mod=m_b_mod, norm1_g=m_norm1_g, w_in=m_w_in, q_norm_g=m_q_norm_g,
                kv_norm_g=m_kv_norm_g, w_uq=m_w_uq, w_ukv=m_w_ukv, qk_norm_q=m_qk_norm_q, qk_norm_k=m_qk_norm_k,
                sgu_norm_g=m_sgu_norm_g, sgu_norm_b=m_sgu_norm_b, w_spatial=m_w_spatial, b_spatial=m_b_spatial,
                w_br_attn=m_w_br_attn, w_br_sgu=m_w_br_sgu, w_out=m_w_out, norm2_g=m_norm2_g, w_ffn_in=m_w_ffn_in,
                w_ffn_out=m_w_ffn_out)
    v_in_ = dict(c_ctx=v_c_ctx, w_mod=v_w_mod, b_mod=v_b_mod, norm1_g=v_norm1_g, w_in=v_w_in, q_norm_g=v_q_norm_g,
                 kv_norm_g=v_kv_norm_g, w_uq=v_w_uq, w_ukv=v_w_ukv, qk_norm_q=v_qk_norm_q, qk_norm_k=v_qk_norm_k,
                 sgu_norm_g=v_sgu_norm_g, sgu_norm_b=v_sgu_norm_b, w_spatial=v_w_spatial, b_spatial=v_b_spatial,
                 w_br_attn=v_w_br_attn, w_br_sgu=v_w_br_sgu, w_out=v_w_out, norm2_g=v_norm2_g, w_ffn_in=v_w_ffn_in,
                 w_ffn_out=v_w_ffn_out)
    names = list(weights)
    big_names = ("w_mod", "w_in", "w_uq", "w_ukv", "w_br_attn", "w_br_sgu", "w_out", "w_ffn_in", "w_ffn_out")
    out_g, out_d, out_m, out_v = {}, {}, {}, {}
    out_g["w_mod"] = grads["w_mod"]
    out_d["w_mod"], out_m["w_mod"], out_v["w_mod"] = [t[None] for t in mod_upd[:3]]
    for nm in big_names[1:]:
        res = _adamw_halves(weights[nm][0], *grad_halves[nm], m_in[nm][0], v_in_[nm][0], core, name="adamw_" + nm)
        out_g[nm], out_d[nm], out_m[nm], out_v[nm] = [t[None] for t in res]
    row_names = [nm for nm in names if nm not in big_names and nm not in ("w_spatial", "b_spatial")]
    widths = [-(-weights[nm].size // LANES) * LANES for nm in row_names]

    def as_row(d):
        return jnp.concatenate([jnp.pad(d[nm].reshape(1, -1), ((0, 0), (0, wd - d[nm].size)))
                                for nm, wd in zip(row_names, widths)], axis=1)

    def as_mat(d):
        return jnp.concatenate([d["w_spatial"].reshape(G * CH, CH), d["b_spatial"].reshape(G, CH)], axis=0)

    row_res = _adamw(as_row(weights), as_row(grads), as_row(m_in), as_row(v_in_), name="adamw_rows")
    mat_res = _adamw(as_mat(weights), as_mat(grads), as_mat(m_in), as_mat(v_in_), name="adamw_spatial")
    for tgt, row, mat in zip((out_d, out_m, out_v), row_res, mat_res):
        for nm, seg in zip(row_names, _split_lanes(row, widths)):
            tgt[nm] = seg[:, :weights[nm].size].reshape(weights[nm].shape)
        tgt["w_spatial"] = mat[:G * CH].reshape(w_spatial.shape)
        tgt["b_spatial"] = mat[G * CH:].reshape(b_spatial.shape)
    for nm in row_names + ["w_spatial", "b_spatial"]:
        out_g[nm] = grads[nm].reshape(weights[nm].shape)

    loss = loss11.reshape(())
    return (loss, grad_x[None], *[out_g[n] for n in names], *[out_d[n] for n in names],
            *[out_m[n] for n in names], *[out_v[n] for n in names])
```

```python
import math

import jax
import jax.numpy as jnp
from jax import lax
from jax.experimental import pallas as pl
from jax.experimental.pallas import tpu as pltpu

F32, BF16 = jnp.float32, jnp.bfloat16
MESH = pl.DeviceIdType.MESH

LANES = 128
F32_SUBLANES = 8
BF16_SUBLANES = 16
MXU_DIM = 256
VMEM_LIMIT_BYTES = 56 * 1024 * 1024

EPS = 1e-6
ROPE_DIM = 64
ROPE_THETA = 10000.0
GRID_W = 64
HEAD_PAD = 256
ADAM_LR, ADAM_B1, ADAM_B2, ADAM_EPS, ADAM_WD, ADAM_STEP = 0.001, 0.9, 0.999, 1e-08, 0.01, 10


def _tile(dim, pref, align=LANES):
    if dim <= pref:
        return dim
    t = (pref // align) * align
    while t >= align:
        if dim % t == 0:
            return t
        t -= align
    return dim


def _params(sem=None):
    return pltpu.CompilerParams(dimension_semantics=sem, vmem_limit_bytes=VMEM_LIMIT_BYTES)


def _sds(shape, dtype):
    return jax.ShapeDtypeStruct(tuple(shape), dtype)


def _mm(pairs, *, name, ta=False, tb=False, outs=(F32,), tm=1024, tn=1024, tk=2048, extras=(), epi=None,
        split=None, into=None, carry=None, col_sums=0):
    dual = len(pairs[0]) == 3
    a0, b0 = pairs[0][0], pairs[0][1]
    M = a0.shape[1] if ta else a0.shape[0]
    N = b0.shape[0] if tb else b0.shape[1]
    tm, tn = _tile(M, tm), _tile(N if split is None else split, tn)
    ks = [(p[0].shape[0] if ta else p[0].shape[1]) for p in pairs]
    tks = [_tile(k, tk) for k in ks]
    nks = [k // t for k, t in zip(ks, tks)]
    offs = [sum(nks[:i]) for i in range(len(pairs))]
    nk_total = sum(nks)
    single = len(pairs) == 1

    def kidx(kk, p):
        return kk if single else jnp.clip(kk - offs[p], 0, nks[p] - 1)

    in_specs, operands = [], []
    for p, pr in enumerate(pairs):
        if ta:
            in_specs.append(pl.BlockSpec((tks[p], tm), lambda i, j, kk, p=p: (kidx(kk, p), i)))
        else:
            in_specs.append(pl.BlockSpec((tm, tks[p]), lambda i, j, kk, p=p: (i, kidx(kk, p))))
        operands.append(pr[0])
        for b in pr[1:]:
            if tb:
                in_specs.append(pl.BlockSpec((tn, tks[p]), lambda i, j, kk, p=p: (j, kidx(kk, p))))
            else:
                in_specs.append(pl.BlockSpec((tks[p], tn), lambda i, j, kk, p=p: (kidx(kk, p), j)))
            operands.append(b)
    for arr, kind in extras:
        if kind == "mn":
            in_specs.append(pl.BlockSpec((tm, tn), lambda i, j, kk: (i, j)))
        else:
            in_specs.append(pl.BlockSpec((1, tn), lambda i, j, kk: (0, j)))
        operands.append(arr)
    n_in = len(operands)
    n_ex = len(extras)
    per = 3 if dual else 2
    dims = (((0 if ta else 1,), (1 if tb else 0,)), ((), ()))

    n_acc = 2 if dual else 1

    def products(ins, p):
        a = ins[per * p][...].astype(BF16)
        return [lax.dot_general(a, ins[per * p + 1 + q][...].astype(BF16), dims, preferred_element_type=F32)
                for q in range(n_acc)]

    def finish(ins, out_refs, acc_vals):
        vals = acc_vals + [r[...] for r in ins[n_in - n_ex:]]
        res = epi(*vals) if epi is not None else (vals[0],)
        for o, r in zip(out_refs, res):
            o[...] = jnp.broadcast_to(r, o.shape).astype(o.dtype)

    out_specs = [pl.BlockSpec((tm, tn), lambda i, j, kk: (i, j)) for _ in outs]
    out_specs += [pl.BlockSpec((None, F32_SUBLANES, tn), lambda i, j, kk: (i, 0, j)) for _ in range(col_sums)]
    out_shape = [_sds((M, N), d) for d in outs] + [_sds((M // tm, F32_SUBLANES, N), F32) for _ in range(col_sums)]
    aliases = {}
    n_alias = 0
    if split is not None:
        nps = split // tn
        lead = 0 if into is None else into[1]
        out_specs = [pl.BlockSpec((None, tm, tn), lambda i, j, kk: (j // nps + lead, i, j % nps))]
        out_shape = [_sds((N // split if into is None else into[0].shape[0], M, split), outs[0])]
        if into is not None:
            in_specs.append(pl.BlockSpec(memory_space=pl.ANY))
            operands.append(into[0])
            aliases, n_alias = {n_in: 0}, 1

    grid = (M // tm, N // tn, nk_total)
    n_out = len(outs) + col_sums

    def at_step(first):
        ids = [pl.program_id(d) for d in range(3)]
        cond = None
        for d, g in zip(ids, grid):
            t = d == (0 if first else g - 1)
            cond = t if cond is None else cond & t
        return cond

    def body(*refs):
        ins, out_refs, accs, start, wait = _split_refs(refs, n_in + n_alias, n_out, carry)
        ins = ins[:n_in]
        if carry is not None:
            pl.when(at_step(True))(start)
        if nk_total == 1:
            finish(ins, out_refs, products(ins, 0))
        else:
            kk = pl.program_id(2)

            @pl.when(kk == 0)
            def _():
                for acc, v in zip(accs, products(ins, 0)):
                    acc[...] = v

            for p in range(len(pairs)):
                lo = max(offs[p], 1)

                @pl.when((kk >= lo) & (kk < offs[p] + nks[p]))
                def _(p=p):
                    for acc, v in zip(accs, products(ins, p)):
                        acc[...] += v

            @pl.when(kk == nk_total - 1)
            def _():
                finish(ins, out_refs, [acc[...] for acc in accs])
        if carry is not None:
            pl.when(at_step(False))(wait)

    ex = carry
    res = pl.pallas_call(
        body, name=name, grid=grid, in_specs=in_specs + ([] if ex is None else ex.in_specs),
        out_specs=out_specs + ([] if ex is None else ex.out_specs),
        out_shape=out_shape + ([] if ex is None else ex.out_shape), input_output_aliases=aliases,
        scratch_shapes=[pltpu.VMEM((tm, tn), F32) for _ in range(n_acc if nk_total > 1 else 0)]
        + ([] if ex is None else ex.scratch),
        compiler_params=_params(("arbitrary",) * 3 if ex is not None else ("parallel", "parallel", "arbitrary")),
    )(*operands, *([] if ex is None else ex.xs))
    if ex is not None:
        return (res[0] if n_out == 1 else res[:n_out]), list(res[n_out:])
    return res[0] if n_out == 1 else res


def _rowwise(fn, rows, vecs, out_rows, out_accs=(), *, name, tm=256, tc=None, carry=None):
    M = rows[0].shape[0]
    tm = _tile(M, tm, BF16_SUBLANES)
    nrow = M // tm
    C = rows[0].shape[1]
    ncol = 1 if tc is None else C // _tile(C, tc)
    tcol = None if tc is None else _tile(C, tc)

    def colwise(shape):
        return tc is not None and len(shape) == 2 and shape[0] == 1 and shape[1] == C

    def vspec(shape):
        if colwise(shape):
            return pl.BlockSpec((1, tcol), lambda j, i: (0, j))
        return pl.BlockSpec(tuple(shape), lambda j, i, n=len(shape): (0,) * n)

    def rspec(width):
        if tc is None:
            return pl.BlockSpec((tm, width), lambda j, i: (i, 0))
        return pl.BlockSpec((tm, tcol), lambda j, i: (i, j))

    in_specs = [rspec(r.shape[1]) for r in rows] + [vspec(v.shape) for v in vecs]
    out_specs = [rspec(c) for c, _ in out_rows] + [vspec(s) for s in out_accs]
    out_shape = [_sds((M, c), d) for c, d in out_rows] + [_sds(s, F32) for s in out_accs]
    n_in, n_or = len(rows) + len(vecs), len(out_rows)

    n_out = n_or + len(out_accs)
    ex = carry

    def body(*refs):
        ins, outs, _, start, wait = _split_refs(refs, n_in, n_out, ex)
        o_rows, o_accs = outs[:n_or], outs[n_or:]
        if ex is not None:
            pl.when((pl.program_id(0) == 0) & (pl.program_id(1) == 0))(start)
        r_out, a_out = fn(*[r[...] for r in ins])
        for o, r in zip(o_rows, r_out):
            o[...] = r.astype(o.dtype)
        i = pl.program_id(1)

        @pl.when(i == 0)
        def _():
            for o, a in zip(o_accs, a_out):
                o[...] = a

        @pl.when(i > 0)
        def _():
            for o, a in zip(o_accs, a_out):
                o[...] += a

        if ex is not None:
            pl.when((pl.program_id(0) == ncol - 1) & (pl.program_id(1) == nrow - 1))(wait)

    res = pl.pallas_call(
        body, name=name, grid=(ncol, nrow), in_specs=in_specs + ([] if ex is None else ex.in_specs),
        out_specs=out_specs + ([] if ex is None else ex.out_specs),
        out_shape=out_shape + ([] if ex is None else ex.out_shape),
        scratch_shapes=[] if ex is None else ex.scratch,
        compiler_params=_params(("arbitrary", "arbitrary") if ex is not None else ("parallel", "arbitrary")),
    )(*rows, *vecs, *([] if ex is None else ex.xs))
    if ex is not None:
        return res[:n_or], res[n_or:n_out], list(res[n_out:])
    return res[:n_or], res[n_or:]


def _colsum(t):
    return jnp.sum(t, axis=0, keepdims=True)


def _gelu(t):
    return 0.5 * t * (1.0 + lax.erf(t * math.sqrt(0.5)))


def _gelu_grad(t):
    return 0.5 * (1.0 + lax.erf(t * math.sqrt(0.5))) + t * jnp.exp(-0.5 * t * t) * (1.0 / math.sqrt(2.0 * math.pi))


def _sigmoid(t):
    return 1.0 / (1.0 + jnp.exp(-t))


def _rms_stats(t, width):
    return lax.rsqrt(jnp.sum(t * t, axis=-1, keepdims=True) * (1.0 / width) + EPS)


def _rms_bwd(dn, tn, r, width):
    return r * (dn - tn * (jnp.sum(dn * tn, axis=-1, keepdims=True) * (1.0 / width)))


def _place():
    return lax.axis_index("x"), lax.axis_index("y"), lax.axis_index("c")


class _ChipExchange:
    def __init__(self, xs, gather):
        self.xs, self.gather, self.n = list(xs), gather, len(xs)
        self.in_specs = [pl.BlockSpec(memory_space=pl.ANY)] * self.n
        self.out_specs = [pl.BlockSpec(memory_space=pl.ANY)] * self.n
        self.out_shape = [_sds((4,) + (x.shape if gather else x.shape[1:]), x.dtype) for x in self.xs]
        self.scratch = [pltpu.SemaphoreType.DMA((self.n, 3)), pltpu.SemaphoreType.DMA((self.n, 3))]

    def bind(self, x_refs, out_refs, send_sems, recv_sems):
        x, y, c = _place()
        p = 2 * x + y
        chips = [(1 - x, y), (x, 1 - y), (1 - x, 1 - y)]

        def copy(w, k, outgoing):
            qx, qy = chips[k]
            there = 2 * qx + qy
            if self.gather:
                src = x_refs[w]
            else:
                src = x_refs[w].at[there if outgoing else p]
            return pltpu.make_async_remote_copy(
                src_ref=src, dst_ref=out_refs[w].at[p if outgoing else there], send_sem=send_sems.at[w, k],
                recv_sem=recv_sems.at[w, k], device_id=(qx, qy, c), device_id_type=MESH)

        def start():
            for w in range(self.n):
                for k in range(3):
                    copy(w, k, True).start()

        def wait():
            for w in range(self.n):
                for k in range(3):
                    copy(w, k, False).wait_recv()
            for w in range(self.n):
                for k in range(3):
                    copy(w, k, True).wait_send()

        return start, wait


class _PairExchange:
    def __init__(self, xs, mode):
        self.xs, self.mode, self.n = list(xs), mode, len(xs)
        self.in_specs = [pl.BlockSpec(memory_space=pl.ANY)] * self.n
        self.out_specs = [pl.BlockSpec(memory_space=pl.ANY)] * self.n
        shape = {"halves": lambda s: (4,) + s[2:], "forward": lambda s: s, "gather": lambda s: (2,) + s}[mode]
        self.out_shape = [_sds(shape(x.shape), x.dtype) for x in self.xs]
        self.scratch = [pltpu.SemaphoreType.DMA((self.n, 3)), pltpu.SemaphoreType.DMA((self.n, 3))]

    def bind(self, x_refs, out_refs, send_sems, recv_sems):
        x, y, c = _place()
        chips = [(1 - x, y), (x, 1 - y), (1 - x, 1 - y)]

        def copy(w, src, dst, k):
            return pltpu.make_async_remote_copy(src_ref=src, dst_ref=dst, send_sem=send_sems.at[w, k],
                                                recv_sem=recv_sems.at[w, k], device_id=(x, y, 1 - c),
                                                device_id_type=MESH)

        def start():
            for w, (xr, orf) in enumerate(zip(x_refs, out_refs)):
                if self.mode == "halves":
                    for q in range(4):
                        copy(w, xr.at[q, 1 - c], orf.at[q], 0).start()
                elif self.mode == "forward":
                    for k, (qx, qy) in enumerate(chips):
                        copy(w, xr.at[2 * qx + qy], orf.at[2 * qx + qy], k).start()
                else:
                    copy(w, xr, orf.at[c], 0).start()

        def wait():
            for w, (xr, orf) in enumerate(zip(x_refs, out_refs)):
                if self.mode == "halves":
                    copy(w, orf, orf, 0).wait()
                elif self.mode == "forward":
                    for k, (qx, qy) in enumerate(chips):
                        copy(w, xr.at[2 * qx + qy], orf.at[2 * qx + qy], k).wait()
                else:
                    cp = copy(w, xr, orf.at[1 - c], 0)
                    cp.wait_recv()
                    cp.wait_send()

        return start, wait


def _split_refs(refs, n_in, n_out, ex):
    ne = 0 if ex is None else ex.n
    ins, xin = refs[:n_in], refs[n_in:n_in + ne]
    outs, xout = refs[n_in + ne:n_in + ne + n_out], refs[n_in + ne + n_out:n_in + 2 * ne + n_out]
    rest = refs[n_in + 2 * ne + n_out:]
    if ex is None:
        return ins, outs, rest, None, None
    start, wait = ex.bind(xin, xout, rest[-2], rest[-1])
    return ins, outs, rest[:-2], start, wait


def _attn_fwd(q, k, v, *, heads, tq=512, carry=None):
    N, M = q.shape[0], k.shape[0]
    tq = _tile(N, tq)
    sub = _tile(tq, MXU_DIM)
    vd = v.shape[1] // heads
    nq = N // tq

    def body(*refs):
        (q_ref, k_ref, v_ref), (o_ref, lse_ref), _, start, wait = _split_refs(refs, 3, 2, carry)
        if carry is not None:
            pl.when((pl.program_id(0) == 0) & (pl.program_id(1) == 0))(start)
        for sb in range(tq // sub):
            rows = pl.ds(sb * sub, sub)
            s = lax.dot_general(q_ref[rows, :], k_ref[...], (((1,), (1,)), ((), ())), preferred_element_type=F32)
            m = jnp.max(s, axis=-1, keepdims=True)
            p = jnp.exp(s - m)
            l = jnp.sum(p, axis=-1, keepdims=True)
            o = jnp.dot(p.astype(BF16), v_ref[...], preferred_element_type=F32) / l
            o_ref[rows, :] = o.astype(o_ref.dtype)
            lse_ref[rows, :] = jnp.broadcast_to(m + jnp.log(l), (sub, vd))
        if carry is not None:
            pl.when((pl.program_id(0) == heads - 1) & (pl.program_id(1) == nq - 1))(wait)

    ex = carry
    res = pl.pallas_call(
        body, name="attn_fwd", grid=(heads, nq),
        in_specs=[pl.BlockSpec((tq, HEAD_PAD), lambda h, i: (i, h)),
                  pl.BlockSpec((M, HEAD_PAD), lambda h, i: (0, h)),
                  pl.BlockSpec((M, vd), lambda h, i: (0, h))] + ([] if ex is None else ex.in_specs),
        out_specs=[pl.BlockSpec((tq, vd), lambda h, i: (i, h)),
                   pl.BlockSpec((tq, vd), lambda h, i: (i, h))] + ([] if ex is None else ex.out_specs),
        out_shape=[_sds((N, heads * vd), BF16), _sds((N, heads * vd), F32)] + ([] if ex is None else ex.out_shape),
        scratch_shapes=[] if ex is None else ex.scratch,
        compiler_params=_params(("arbitrary", "arbitrary")),
    )(q, k, v, *([] if ex is None else ex.xs))
    return res[0], res[1], list(res[2:])


def _attn_bwd(q, k, v, o, lse, do, *, heads, tq=512, carry=None):
    N, M = q.shape[0], k.shape[0]
    tq = _tile(N, tq)
    vd = v.shape[1] // heads
    nq = N // tq
    sub = _tile(tq, MXU_DIM)
    nt = (((1,), (1,)), ((), ()))
    tn = (((0,), (0,)), ((), ()))

    def body(*refs):
        (q_ref, k_ref, v_ref, o_ref, lse_ref, do_ref), (dq_ref, dk_ref, dv_ref), _, start, wait = _split_refs(
            refs, 6, 3, carry)
        if carry is not None:
            pl.when((pl.program_id(0) == 0) & (pl.program_id(1) == 0))(start)
        i = pl.program_id(1)
        kb, vb = k_ref[...], v_ref[...]
        parts = []
        for sb in range(tq // sub):
            rows = pl.ds(sb * sub, sub)
            qb, dob = q_ref[rows, :], do_ref[rows, :]
            delta = jnp.sum(dob.astype(F32) * o_ref[rows, :].astype(F32), axis=-1, keepdims=True)
            s = lax.dot_general(qb, kb, nt, preferred_element_type=F32)
            p = jnp.exp(s - lse_ref[rows, :][:, :1])
            dp = lax.dot_general(dob, vb, nt, preferred_element_type=F32)
            ds = (p * (dp - delta)).astype(BF16)
            dq_ref[rows, :] = jnp.dot(ds, kb, preferred_element_type=F32)
            parts.append((lax.dot_general(ds, qb, tn, preferred_element_type=F32),
                          lax.dot_general(p.astype(BF16), dob, tn, preferred_element_type=F32)))

        dk_step, dv_step = parts[0]
        for dk_part, dv_part in parts[1:]:
            dk_step, dv_step = dk_step + dk_part, dv_step + dv_part

        @pl.when(i == 0)
        def _():
            dk_ref[...] = dk_step
            dv_ref[...] = dv_step

        @pl.when(i > 0)
        def _():
            dk_ref[...] += dk_step
            dv_ref[...] += dv_step

        if carry is not None:
            pl.when((pl.program_id(0) == heads - 1) & (pl.program_id(1) == nq - 1))(wait)

    ex = carry
    res = pl.pallas_call(
        body, name="attn_bwd", grid=(heads, nq),
        in_specs=[pl.BlockSpec((tq, HEAD_PAD), lambda h, i: (i, h)),
                  pl.BlockSpec((M, HEAD_PAD), lambda h, i: (0, h)),
                  pl.BlockSpec((M, vd), lambda h, i: (0, h)),
                  pl.BlockSpec((tq, vd), lambda h, i: (i, h)),
                  pl.BlockSpec((tq, vd), lambda h, i: (i, h)),
                  pl.BlockSpec((tq, vd), lambda h, i: (i, h))] + ([] if ex is None else ex.in_specs),
        out_specs=[pl.BlockSpec((tq, HEAD_PAD), lambda h, i: (i, h)),
                   pl.BlockSpec((M, HEAD_PAD), lambda h, i: (0, h)),
                   pl.BlockSpec((M, vd), lambda h, i: (0, h))] + ([] if ex is None else ex.out_specs),
        out_shape=[_sds((N, heads * HEAD_PAD), F32), _sds((M, heads * HEAD_PAD), F32),
                   _sds((M, heads * vd), F32)] + ([] if ex is None else ex.out_shape),
        scratch_shapes=[] if ex is None else ex.scratch,
        compiler_params=_params(("arbitrary", "arbitrary")),
    )(q, k, v, o, lse, do, *([] if ex is None else ex.xs))
    return res[0], res[1], res[2], list(res[3:])


def _comm_call(body, xs, out_shapes, n_sems, name, in_vmem):
    space = pltpu.VMEM if in_vmem else pl.ANY
    n = len(xs)

    def wrapped(*refs):
        body(refs[:n], refs[n:2 * n], *refs[2 * n:])

    return pl.pallas_call(
        wrapped, name=name, out_shape=list(out_shapes),
        in_specs=[pl.BlockSpec(memory_space=space)] * n, out_specs=[pl.BlockSpec(memory_space=space)] * n,
        scratch_shapes=[pltpu.SemaphoreType.DMA((n, n_sems)), pltpu.SemaphoreType.DMA((n, n_sems)),
                        pltpu.SemaphoreType.DMA((n,))],
        compiler_params=pltpu.CompilerParams(vmem_limit_bytes=VMEM_LIMIT_BYTES),
    )(*xs)


def _all_gather8(blks, *, name, in_vmem):
    def body(x_refs, out_refs, send_sems, recv_sems, local_sems):
        x, y, c = _place()
        me, sibling = (x, y, c), (x, y, 1 - c)
        chips = [(1 - x, y), (x, 1 - y), (1 - x, 1 - y)]
        waits = []
        for w, (x_ref, out_ref) in enumerate(zip(x_refs, out_refs)):
            def slot(px, py, pc, out_ref=out_ref):
                return out_ref.at[4 * px + 2 * py + pc]

            def copy(k, block, to, src=None, w=w, slot=slot):
                return pltpu.make_async_remote_copy(
                    src_ref=slot(*block) if src is None else src, dst_ref=slot(*block),
                    send_sem=send_sems.at[w, k], recv_sem=recv_sems.at[w, k], device_id=to, device_id_type=MESH)

            mine = pltpu.make_async_copy(x_ref, slot(*me), local_sems.at[w])
            mine.start()
            first = [copy(0, me, sibling, src=x_ref)]
            first += [copy(1 + j, me, (*chip, c), src=x_ref) for j, chip in enumerate(chips)]
            for cp in first:
                cp.start()
            waits.append((copy, mine, first))
        for copy, mine, first in waits:
            passed = [copy(4 + j, (*chip, c), sibling) for j, chip in enumerate(chips)]
            for j, chip in enumerate(chips):
                copy(1 + j, (*chip, c), me).wait_recv()
                passed[j].start()
            copy(0, sibling, me).wait_recv()
            for j, chip in enumerate(chips):
                copy(4 + j, (*chip, 1 - c), me).wait_recv()
            for cp in first + passed:
                cp.wait_send()
            mine.wait()

    return _comm_call(body, blks, [_sds((8,) + b.shape, b.dtype) for b in blks], 7, name, in_vmem)


def _gather_others(blks, *, name):
    def body(x_refs, out_refs, send_sems, recv_sems, local_sems):
        x, y, c = _place()
        own, xn, yn, dg = (x, y), (1 - x, y), (x, 1 - y), (1 - x, 1 - y)

        def slot(w, chip, core):
            return out_refs[w].at[4 * chip[0] + 2 * chip[1] + core]

        def cp(w, k, src, dst, chip, core):
            return pltpu.make_async_remote_copy(src_ref=src, dst_ref=dst, send_sem=send_sems.at[w, k],
                                                recv_sem=recv_sems.at[w, k], device_id=(*chip, core),
                                                device_id_type=MESH)

        def halves(w):
            h = x_refs[w].shape[0] // 2
            return pl.ds(0, h), pl.ds(h, h)

        sends = []
        for w, x_ref in enumerate(x_refs):
            sends += [cp(w, 0, x_ref, slot(w, own, c), xn, c), cp(w, 1, x_ref, slot(w, own, c), yn, c)]
        for s in sends:
            s.start()
        for w, x_ref in enumerate(x_refs):
            lo, hi = halves(w)
            cp(w, 1, x_ref, slot(w, yn, c), yn, c).wait_recv()
            passed = [cp(w, 2, slot(w, yn, c).at[lo], slot(w, yn, c).at[lo], xn, c),
                      cp(w, 4, slot(w, yn, c), slot(w, yn, c), own, 1 - c)]
            cp(w, 0, x_ref, slot(w, xn, c), xn, c).wait_recv()
            passed += [cp(w, 3, slot(w, xn, c).at[hi], slot(w, xn, c).at[hi], yn, c),
                       cp(w, 5, slot(w, xn, c), slot(w, xn, c), own, 1 - c)]
            for s in passed:
                s.start()
            sends += passed
        for w in range(len(x_refs)):
            lo, hi = halves(w)
            cp(w, 2, slot(w, dg, c).at[lo], slot(w, dg, c).at[lo], xn, c).wait_recv()
            cp(w, 3, slot(w, dg, c).at[hi], slot(w, dg, c).at[hi], yn, c).wait_recv()
            passed = [cp(w, 6, slot(w, dg, c), slot(w, dg, c), own, 1 - c)]
            passed[0].start()
            sends += passed
        for w in range(len(x_refs)):
            cp(w, 4, slot(w, yn, c), slot(w, yn, 1 - c), own, 1 - c).wait_recv()
            cp(w, 5, slot(w, xn, c), slot(w, xn, 1 - c), own, 1 - c).wait_recv()
            cp(w, 6, slot(w, dg, c), slot(w, dg, 1 - c), own, 1 - c).wait_recv()
        for s in sends:
            s.wait_send()

    return list(_comm_call(body, blks, [_sds((8,) + b.shape, b.dtype) for b in blks], 7, name, False))


def _exchange_alone(ex, *, name):
    def body(x_refs, out_refs, send_sems, recv_sems, local_sems):
        start, wait = ex.bind(x_refs, out_refs, send_sems, recv_sems)
        start()
        wait()

    return list(_comm_call(body, ex.xs, ex.out_shape, 3, name, False))


def _block_rows(rows, row_bytes, target=1 << 21, align=BF16_SUBLANES):
    return _tile(rows, max(align, target // row_bytes // align * align), align)


def _sum_blocks(buf, *, name, out_dtype):
    B, R, C = buf.shape
    tm = _block_rows(R, B * C * buf.dtype.itemsize)

    def body(x_ref, o_ref):
        acc = x_ref[0].astype(F32)
        for b in range(1, B):
            acc = acc + x_ref[b].astype(F32)
        o_ref[...] = acc.astype(o_ref.dtype)

    return pl.pallas_call(
        body, name=name, grid=(R // tm,), in_specs=[pl.BlockSpec((B, tm, C), lambda i: (0, i, 0))],
        out_specs=pl.BlockSpec((tm, C), lambda i: (i, 0)), out_shape=_sds((R, C), out_dtype),
        compiler_params=_params(("parallel",)),
    )(buf)


def _sum_chips(received, sent, chip, *, name):
    _, R, C = received.shape
    tm = _block_rows(R, 5 * C * received.dtype.itemsize)

    def body(chip_ref, r0, r1, r2, r3, own_ref, o_ref):
        acc = None
        for q, r in enumerate((r0, r1, r2, r3)):
            term = jnp.where(q == chip_ref[0], own_ref[...], r[...]).astype(F32)
            acc = term if acc is None else acc + term
        o_ref[...] = acc

    def slot(q):
        return pl.BlockSpec((None, tm, C), lambda i, ch, q=q: (jnp.where(q == ch[0], (q + 1) % 4, q), i, 0))

    return pl.pallas_call(
        body, name=name, out_shape=_sds((R, C), F32),
        grid_spec=pltpu.PrefetchScalarGridSpec(
            num_scalar_prefetch=1, grid=(R // tm,),
            in_specs=[slot(0), slot(1), slot(2), slot(3), pl.BlockSpec((None, tm, C), lambda i, ch: (ch[0], i, 0))],
            out_specs=pl.BlockSpec((tm, C), lambda i, ch: (i, 0))),
        compiler_params=_params(("arbitrary",)),
    )(chip, received, received, received, received, sent)


def _pair_add(mine, theirs, core, *, name):
    _, _, R, C = mine.shape
    tm = _block_rows(R, C * 2)

    def body(core_ref, a_ref, b_ref, o_ref):
        o_ref[...] = (a_ref[...].astype(F32) + b_ref[...].astype(F32)).astype(o_ref.dtype)

    return pl.pallas_call(
        body, name=name, out_shape=_sds(theirs.shape, BF16),
        grid_spec=pltpu.PrefetchScalarGridSpec(
            num_scalar_prefetch=1, grid=(4, R // tm),
            in_specs=[pl.BlockSpec((None, None, tm, C), lambda q, i, core_ref: (q, core_ref[0], i, 0)),
                      pl.BlockSpec((None, tm, C), lambda q, i, core_ref: (q, i, 0))],
            out_specs=pl.BlockSpec((None, tm, C), lambda q, i, core_ref: (q, i, 0))),
        compiler_params=_params(("parallel", "parallel")),
    )(core, mine, theirs)


def _assemble(gathered, own, chip, *, name, transpose):
    _, K, Ns = gathered.shape
    tm = _block_rows(K, Ns * 4)

    def body(chip_ref, g_ref, own_ref, o_ref):
        q = pl.program_id(0)

        @pl.when(q == chip_ref[0])
        def _():
            o_ref[...] = own_ref[...].astype(BF16)

        @pl.when(q != chip_ref[0])
        def _():
            o_ref[...] = g_ref[...]

    if transpose:
        out_spec = pl.BlockSpec((tm, Ns), lambda q, i, ch: (i, q))
        out_shape = _sds((K, 4 * Ns), BF16)
    else:
        out_spec = pl.BlockSpec((None, tm, Ns), lambda q, i, ch: (q, i, 0))
        out_shape = _sds((4, K, Ns), BF16)
    return pl.pallas_call(
        body, name=name, out_shape=out_shape,
        grid_spec=pltpu.PrefetchScalarGridSpec(
            num_scalar_prefetch=1, grid=(4, K // tm),
            in_specs=[pl.BlockSpec((None, tm, Ns), lambda q, i, ch: (jnp.where(q == ch[0], (q + 1) % 4, q), i, 0)),
                      pl.BlockSpec((tm, Ns), lambda q, i, ch: (jnp.where(q == ch[0], i, 0), 0))],
            out_specs=out_spec),
        compiler_params=_params(("arbitrary", "arbitrary")),
    )(chip, gathered, own)


def _assemble_halves(mine, theirs, own, place, *, name, transpose):
    _, K2, Ns = mine.shape
    tm = _block_rows(K2, Ns * 4, target=1 << 22)
    nb = K2 // tm

    def body(place_ref, m_ref, t_ref, own_ref, o_ref):
        q, hb = pl.program_id(0), pl.program_id(1)
        is_own = q == place_ref[0]
        is_mine = hb == place_ref[1]

        @pl.when(is_own)
        def _():
            o_ref[...] = own_ref[...].astype(BF16)

        @pl.when(jnp.logical_not(is_own) & is_mine)
        def _():
            o_ref[...] = m_ref[...]

        @pl.when(jnp.logical_not(is_own) & jnp.logical_not(is_mine))
        def _():
            o_ref[...] = t_ref[...]

    def other(q, pr):
        return jnp.where(q == pr[0], (q + 1) % 4, q)

    if transpose:
        out_spec = pl.BlockSpec((tm, Ns), lambda q, hb, i, pr: (hb * nb + i, q))
        out_shape = _sds((2 * K2, 4 * Ns), BF16)
    else:
        out_spec = pl.BlockSpec((None, tm, Ns), lambda q, hb, i, pr: (q, hb * nb + i, 0))
        out_shape = _sds((4, 2 * K2, Ns), BF16)
    return pl.pallas_call(
        body, name=name, out_shape=out_shape,
        grid_spec=pltpu.PrefetchScalarGridSpec(
            num_scalar_prefetch=1, grid=(4, 2, nb),
            in_specs=[pl.BlockSpec((None, tm, Ns), lambda q, hb, i, pr: (other(q, pr), jnp.where(hb == pr[1], i, 0), 0)),
                      pl.BlockSpec((None, tm, Ns), lambda q, hb, i, pr: (other(q, pr), jnp.where(hb == pr[1], 0, i), 0)),
                      pl.BlockSpec((tm, Ns), lambda q, hb, i, pr: (jnp.where(q == pr[0], hb * nb + i, 0), 0))],
            out_specs=out_spec),
        compiler_params=_params(("arbitrary",) * 3),
    )(place, mine, theirs, own)


def _split_lanes(row, widths):
    out, off = [], 0
    for wd in widths:
        out.append(row[:, off:off + wd])
        off += wd
    return out


def _adamw_math(w, g, m, v):
    m = ADAM_B1 * m + (1.0 - ADAM_B1) * g
    v = ADAM_B2 * v + (1.0 - ADAM_B2) * (g * g)
    m_hat = m / (1.0 - ADAM_B1 ** ADAM_STEP)
    v_hat = v / (1.0 - ADAM_B2 ** ADAM_STEP)
    delta = -ADAM_LR * (m_hat / (jnp.sqrt(v_hat) + ADAM_EPS) + ADAM_WD * w)
    return delta, m, v


def _adamw_halves(w, mine, other, m, v, core, *, name):
    K, Ns = w.shape
    tm = _block_rows(K // 2, Ns * 4, target=1 << 20, align=F32_SUBLANES)
    nb = (K // 2) // tm

    def body(core_ref, w_ref, mine_ref, other_ref, m_ref, v_ref, g_out, d_out, m_out, v_out):
        g = jnp.where(pl.program_id(0) // nb == core_ref[0], mine_ref[...], other_ref[...])
        g_out[...] = g
        d_out[...], m_out[...], v_out[...] = _adamw_math(w_ref[...], g, m_ref[...], v_ref[...])

    row = pl.BlockSpec((tm, Ns), lambda i, cr: (i, 0))
    return pl.pallas_call(
        body, name=name, out_shape=[_sds((K, Ns), F32)] * 4,
        grid_spec=pltpu.PrefetchScalarGridSpec(
            num_scalar_prefetch=1, grid=(K // tm,),
            in_specs=[row,
                      pl.BlockSpec((tm, Ns), lambda i, cr: (jnp.where(i // nb == cr[0], i % nb, 0), 0)),
                      pl.BlockSpec((None, tm, Ns), lambda i, cr: (1 - cr[0], jnp.where(i // nb == cr[0], 0, i % nb), 0)),
                      row, row],
            out_specs=[row, row, row, row]),
        compiler_params=_params(("arbitrary",)),
    )(core, w, mine, other, m, v)


def _adamw(w, g, m, v, *, name, carry=None):
    C = w.shape[1]

    def fn(w, g, m, v):
        return _adamw_math(w, g, m, v), ()

    tm = max(F32_SUBLANES, min(512, (1 << 20) // (4 * C) // F32_SUBLANES * F32_SUBLANES))
    res = _rowwise(fn, [w, g, m, v], [], [(C, F32)] * 3, name=name, tm=tm, carry=carry)
    return tuple(res[0]) + ((res[2],) if carry is not None else ())


def _rope_tables(n):
    rows = n // GRID_W
    row = jnp.repeat(jnp.arange(rows, dtype=F32), GRID_W)
    col = jnp.tile(jnp.arange(GRID_W, dtype=F32), rows)
    nf = ROPE_DIM // 4
    freqs = ROPE_THETA ** (-jnp.arange(nf, dtype=F32) / nf)
    ang_r, ang_c = row[:, None] * freqs[None, :], col[:, None] * freqs[None, :]
    cr, sr, cc, sc = jnp.cos(ang_r), jnp.sin(ang_r), jnp.cos(ang_c), jnp.sin(ang_c)
    nope = HEAD_PAD - 2 * ROPE_DIM
    one, zero, z = jnp.ones((n, nope), F32), jnp.zeros((n, nope), F32), jnp.zeros((n, nf), F32)
    pad = jnp.zeros((n, ROPE_DIM), F32)
    cos = jnp.concatenate([one, cr, cr, cc, cc, pad], axis=1)
    s_lo = jnp.concatenate([zero, -sr, z, -sc, z, pad], axis=1)
    s_hi = jnp.concatenate([zero, z, sr, z, sc, pad], axis=1)
    return cos, s_lo, s_hi


def _rope(n, cos, s_lo, s_hi):
    q = ROPE_DIM // 4
    return n * cos + pltpu.roll(n, HEAD_PAD - q, 1) * s_lo + pltpu.roll(n, q, 1) * s_hi


def _rope_t(d, cos, s_lo, s_hi):
    q = ROPE_DIM // 4
    return d * cos + pltpu.roll(d * s_lo, q, 1) + pltpu.roll(d * s_hi, HEAD_PAD - q, 1)


def kernel(x, c, ctx, c_ctx, w_mod, b_mod, norm1_g, w_in, q_norm_g, kv_norm_g, w_uq, w_ukv, qk_norm_q, qk_norm_k, sgu_norm_g, sgu_norm_b, w_spatial, b_spatial, w_br_attn, w_br_sgu, w_out, norm2_g, w_ffn_in, w_ffn_out, loss_target, m_c_ctx, m_w_mod, m_b_mod, m_norm1_g, m_w_in, m_q_norm_g, m_kv_norm_g, m_w_uq, m_w_ukv, m_qk_norm_q, m_qk_norm_k, m_sgu_norm_g, m_sgu_norm_b, m_w_spatial, m_b_spatial, m_w_br_attn, m_w_br_sgu, m_w_out, m_norm2_g, m_w_ffn_in, m_w_ffn_out, v_c_ctx, v_w_mod, v_b_mod, v_norm1_g, v_w_in, v_q_norm_g, v_kv_norm_g, v_w_uq, v_w_ukv, v_qk_norm_q, v_qk_norm_k, v_sgu_norm_g, v_sgu_norm_b, v_w_spatial, v_b_spatial, v_w_br_attn, v_w_br_sgu, v_w_out, v_norm2_g, v_w_ffn_in, v_w_ffn_out):
    ax, ay, ac = _place()
    my_chip = 2 * ax + ay
    my_dev = 4 * ax + 2 * ay + ac

    N, D = x.shape[1], x.shape[2]
    CT = ctx.shape[1]
    M = N + CT
    QL, KVL, QK = q_norm_g.shape[-1], kv_norm_g.shape[-1], qk_norm_q.shape[-1]
    NOPE = QK - ROPE_DIM
    VD = NOPE
    H = 4 * w_uq.shape[-1] // QK
    SW, G, CH = sgu_norm_g.shape[-1], w_spatial.shape[1], w_spatial.shape[2]
    GD = SW // G
    DFF = 4 * w_ffn_out.shape[1]
    NMOD = 4 * w_mod.shape[-1]
    NM = w_mod.shape[-1]
    KVP = KVL + 2 * ROPE_DIM
    assert NOPE == LANES and GD == LANES and HEAD_PAD == NOPE + 2 * ROPE_DIM and CH == LANES
    scale = QK ** -0.5

    x2, ctx2, tgt2 = x[0], ctx[0], loss_target[0]

    c_all = _all_gather8([c], name="ag_c", in_vmem=True)[0][:, 0, :]
    c_rows = jnp.concatenate([c_all, c_ctx[None, :], jnp.zeros((BF16_SUBLANES - 9, D), F32)], axis=0)

    def silu_fn(t):
        s = _sigmoid(t)
        return (t * s, s * (1.0 + t * (1.0 - s))), ()

    (silu_c, dsilu_c), _ = _rowwise(silu_fn, [c_rows], [], [(D, F32), (D, F32)], name="silu_c", tm=16)
    wm = w_mod[0]
    mod_loc = _mm([(silu_c, wm)], name="mod_fwd", outs=(F32,), tn=512, tk=512,
                  extras=[(lax.dynamic_slice_in_dim(b_mod, my_chip * NM, NM, axis=1), "n")],
                  epi=lambda acc, b: (acc + b,))
    mod_all = _all_gather8([mod_loc], name="ag_mod", in_vmem=True)[0]
    mod_full = jnp.concatenate([mod_all[0], mod_all[2], mod_all[4], mod_all[6]], axis=1)
    mod_me = lax.dynamic_slice_in_dim(mod_full, my_dev, 1, axis=0)
    sh1, sc1, g1, sh2, sc2, g2 = [mod_me[:, i * D:(i + 1) * D] for i in range(6)]
    sh1c, sc1c = mod_full[8:9, :D], mod_full[8:9, D:2 * D]

    big = [w_in[0], w_uq[0], w_ukv[0], w_br_attn[0], w_br_sgu[0], w_out[0], w_ffn_in[0], w_ffn_out[0]]
    col_sharded = [True, True, True, True, True, False, True, False]
    halves = [lax.dynamic_slice_in_dim(a, ac * (a.shape[0] // 2), a.shape[0] // 2, axis=0).astype(BF16) for a in big]
    tags = ["w_in", "w_uq", "w_ukv", "w_br_attn", "w_br_sgu", "w_out", "w_ffn_in", "w_ffn_out"]
    first_group, attn_group, ffn_group = [0, 1, 2], [3, 4, 5, 6], [7]
    chip1 = jnp.reshape(my_chip, (1,)).astype(jnp.int32)
    place2 = jnp.stack([my_chip, ac]).astype(jnp.int32)

    def laid_out(seg, i):
        a = big[i]
        if col_sharded[i] and seg.ndim == 3:
            return seg.transpose(1, 0, 2).reshape(a.shape[0], 4 * a.shape[1])
        return seg if col_sharded[i] else seg.reshape(4 * a.shape[0], a.shape[1])

    def side_by_side(i):
        return col_sharded[i] and big[i].shape[1] % LANES == 0

    def finish_gather(idx, mine4, theirs4):
        return [laid_out(_assemble_halves(m, t, big[i], place2, name="assemble_" + tags[i], transpose=side_by_side(i)), i)
                for i, m, t in zip(idx, mine4, theirs4)]

    gathered = _gather_others([halves[i] for i in first_group], name="ag_weights")
    w_in_f, w_uq_f, w_ukv_f = [
        laid_out(_assemble(seg.reshape((4,) + big[i].shape), big[i], chip1, name="assemble_" + tags[i],
                           transpose=side_by_side(i)), i) for i, seg in zip(first_group, gathered)]
    o_kv, o_u = QL, QL + KVL + ROPE_DIM
    o_v, o_g = o_u + SW, o_u + 2 * SW
    w_q = w_in_f[:, :QL]
    w_kv = jnp.pad(w_in_f[:, o_kv:o_u], ((0, 0), (0, ROPE_DIM)))
    w_u, w_v = w_in_f[:, o_u:o_v], w_in_f[:, o_v:o_g]
    w_g1, w_g2 = w_in_f[:, o_g:o_g + D], w_in_f[:, o_g + D:]
    w_uq_p = jnp.pad(w_uq_f.reshape(QL, H, QK), ((0, 0), (0, 0), (0, HEAD_PAD - QK))).reshape(QL, H * HEAD_PAD)

    cos_t, slo_t, shi_t = _rope_tables(N)
    ones_c = jnp.concatenate([jnp.ones((CT, NOPE + ROPE_DIM), F32), jnp.zeros((CT, ROPE_DIM), F32)], axis=1)
    cos_k = jnp.concatenate([cos_t, ones_c], axis=0)
    slo_k = jnp.concatenate([slo_t, jnp.zeros((CT, HEAD_PAD), F32)], axis=0)
    shi_k = jnp.concatenate([shi_t, jnp.zeros((CT, HEAD_PAD), F32)], axis=0)
    gq_p = jnp.pad(qk_norm_q, ((0, 0), (0, HEAD_PAD - QK)))
    gk_p = jnp.pad(qk_norm_k, ((0, 0), (0, HEAD_PAD - QK)))

    def norm_mod_fn(t, g, sh, sc):
        r = _rms_stats(t, D)
        return (((t * r) * g) * (1.0 + sc) + sh,), ()

    (h,), _ = _rowwise(norm_mod_fn, [x2], [norm1_g, sh1, sc1], [(D, BF16)], name="norm1_x")
    (ctx_h,), _ = _rowwise(norm_mod_fn, [ctx2], [norm1_g, sh1c, sc1c], [(D, BF16)], name="norm1_ctx")

    def q_norm_epi(acc, g):
        return acc, (acc * _rms_stats(acc, QL)) * g

    qc, qn = _mm([(h, w_q)], name="proj_q", outs=(F32, BF16), tn=QL, extras=[(q_norm_g, "n")], epi=q_norm_epi)
    kvin = jnp.concatenate([_mm([(h, w_kv)], name="proj_kv", outs=(F32,)),
                            _mm([(ctx_h, w_kv)], name="proj_kv_ctx", outs=(F32,))], axis=0)
    u_in = _mm([(h, w_u)], name="proj_u", outs=(BF16,))
    v_in = _mm([(h, w_v)], name="proj_v", outs=(BF16,))
    g1_in, (mine_bra,) = _mm([(h, w_g1)], name="proj_g1", outs=(BF16,), carry=_ChipExchange([halves[3]], gather=True))
    g2_in, (mine_brs,) = _mm([(h, w_g2)], name="proj_g2", outs=(BF16,), carry=_ChipExchange([halves[4]], gather=True))

    def kv_norm_fn(t, g):
        kvc = t[:, :KVL]
        return (((kvc * _rms_stats(kvc, KVL)) * g),), ()

    (kvn,), _ = _rowwise(kv_norm_fn, [kvin], [kv_norm_g], [(KVL, BF16)], name="kv_norm")
    q_raw = _mm([(qn, w_uq_p)], name="q_up", outs=(F32,))
    kv_rows = _tile(M, 2304)
    kv_raw = _mm([(kvn, w_ukv_f)], name="kv_up", outs=(F32,), tm=kv_rows)

    def q_post_fn(t, cos, slo, shi, g):
        outs = []
        for hd in range(H):
            th = t[:, hd * HEAD_PAD:(hd + 1) * HEAD_PAD]
            outs.append(_rope((th * _rms_stats(th, QK)) * g, cos, slo, shi) * scale)
        return (jnp.concatenate(outs, axis=1),), ()

    (q_att,), _ = _rowwise(q_post_fn, [q_raw, cos_t, slo_t, shi_t], [gq_p], [(H * HEAD_PAD, BF16)], name="q_post")

    def k_post_fn(t, kvi, cos, slo, shi, g):
        kr = kvi[:, KVL:]
        ks, vs = [], []
        for hd in range(H):
            th = jnp.concatenate([t[:, hd * HEAD_PAD:hd * HEAD_PAD + NOPE], kr], axis=1)
            ks.append(_rope((th * _rms_stats(th, QK)) * g, cos, slo, shi))
            vs.append(t[:, hd * HEAD_PAD + NOPE:(hd + 1) * HEAD_PAD])
        return (jnp.concatenate(ks, axis=1), jnp.concatenate(vs, axis=1)), ()

    (k_att, v_att), _, (mine_out,) = _rowwise(k_post_fn, [kv_raw, kvin, cos_k, slo_k, shi_k], [gk_p],
                                              [(H * HEAD_PAD, BF16), (H * VD, BF16)], name="k_post",
                                              carry=_ChipExchange([halves[5]], gather=True))
    attn_o, lse, (mine_ffi,) = _attn_fwd(q_att, k_att, v_att, heads=H, carry=_ChipExchange([halves[6]], gather=True))
    mine4 = [mine_bra, mine_brs, mine_out, mine_ffi]

    ws3 = w_spatial[0]
    bs_t = jnp.pad(b_spatial[0].T, ((0, 0), (0, LANES - G)))

    def sgu_parts(u_in, v_in, ng, nb):
        u, v = _gelu(u_in.astype(F32)), _gelu(v_in.astype(F32))
        mu = jnp.mean(v, axis=-1, keepdims=True)
        vc = v - mu
        rs = lax.rsqrt(jnp.mean(vc * vc, axis=-1, keepdims=True) + EPS)
        xhat = vc * rs
        return u, xhat, rs, (xhat * ng + nb).astype(BF16)

    def sgu_fwd_fn(u_in, v_in, ng, nb, ws, bst):
        u, _, _, vnb = sgu_parts(u_in, v_in, ng, nb)
        outs = []
        for g in range(G):
            sl = slice(g * GD, (g + 1) * GD)
            mixed = jnp.dot(ws[g].astype(BF16), vnb[:, sl], preferred_element_type=F32) + bst[:, g:g + 1]
            outs.append(u[:, sl] * mixed)
        return (jnp.concatenate(outs, axis=1),), ()

    (sgu_o,), _, theirs4 = _rowwise(sgu_fwd_fn, [u_in, v_in], [sgu_norm_g, sgu_norm_b, ws3, bs_t], [(SW, BF16)],
                                    name="sgu_fwd", tm=CH, carry=_PairExchange(mine4, "forward"))
    w_bra, w_brs, w_out_f, w_ffi = finish_gather(attn_group, mine4, theirs4)
    w_fa, w_fb = w_ffi[:, :DFF], w_ffi[:, DFF:]

    a1 = _mm([(attn_o, w_bra)], name="br_attn", outs=(BF16,))
    def merge_epi(acc, a1v, gi1, gi2):
        return acc, _sigmoid(gi1.astype(F32)) * a1v.astype(F32) + _sigmoid(gi2.astype(F32)) * acc

    a2, merged = _mm([(sgu_o, w_brs)], name="br_sgu", outs=(BF16, BF16),
                     extras=[(a1, "mn"), (g1_in, "mn"), (g2_in, "mn")], epi=merge_epi)

    def res_gate(acc, res, gate):
        return res + gate * acc, acc

    x1, mo = _mm([(merged, w_out_f)], name="out_proj", outs=(F32, BF16), tn=1024,
                 extras=[(x2, "mn"), (g1, "n")], epi=res_gate)
    (h2,), _ = _rowwise(norm_mod_fn, [x1], [norm2_g, sh2, sc2], [(D, BF16)], name="norm2")

    def swiglu_epi(a, b):
        return a, b, (a * _sigmoid(a)) * b

    (fa, fb, act), mine4 = _mm([(h2, w_fa, w_fb)], name="ffn_in", outs=(BF16, BF16, BF16), tn=512, epi=swiglu_epi,
                               carry=_ChipExchange([halves[i] for i in ffn_group], gather=True))
    (w_ffo,) = finish_gather(ffn_group, mine4, _exchange_alone(_PairExchange(mine4, "forward"), name="ag_forward_ffn"))
    def loss_epi(acc, res, t, gate):
        e = (res + gate * acc) - t
        dy = e * (1.0 / D)
        return dy, gate * dy, _colsum(e * e) * (0.5 / D), _colsum(dy * acc)

    dy, df, loss_part, dg2_part = _mm([(act, w_ffo)], name="ffn_out", outs=(F32, BF16), tn=1024, col_sums=2,
                                      extras=[(x1, "mn"), (tgt2, "mn"), (g2, "n")], epi=loss_epi)

    def fold_fn(a, b):
        return (), (_colsum(a), _colsum(b))

    _, (loss_cols, dg2) = _rowwise(fold_fn, [loss_part[:, 0, :], dg2_part[:, 0, :]], [], [], [(1, D), (1, D)],
                                   name="loss_fold", tm=loss_part.shape[0])

    def swiglu_bwd_epi(dact, a, b):
        a, b = a.astype(F32), b.astype(F32)
        s = _sigmoid(a)
        return dact * b * (s * (1.0 + a * (1.0 - s))), dact * (a * s)

    da, db = _mm([(df, w_ffo)], tb=True, name="ffn_out_dx", outs=(BF16, BF16), tn=512,
                 extras=[(fa, "mn"), (fb, "mn")], epi=swiglu_bwd_epi)
    dw_ffo = _mm([(act, df)], ta=True, name="ffn_out_dw", outs=(BF16,))
    dh2 = _mm([(da, w_fa), (db, w_fb)], tb=True, name="ffn_in_dx", outs=(F32,))
    ns_ffi = w_ffn_in.shape[-1]
    dw_ffi = _mm([(h2, da)], ta=True, name="ffn_in_dw_a", outs=(BF16,), tn=1408, split=ns_ffi,
                 into=(lax.empty((4, D, ns_ffi), BF16), 0))
    dw_ffi = _mm([(h2, db)], ta=True, name="ffn_in_dw_b", outs=(BF16,), tn=1408, split=ns_ffi, into=(dw_ffi, 2))

    def norm2_bwd_fn(dh, t, dyv, mov, g, sc, g1v):
        r = _rms_stats(t, D)
        tn = t * r
        dxg = dh * (1.0 + sc)
        dt = dyv + _rms_bwd(dxg * g, tn, r, D)
        return (dt, g1v * dt), (_colsum(dh), _colsum(dh * (tn * g)), _colsum(dxg * tn), _colsum(dt * mov.astype(F32)))

    (dx1, dmo), (dsh2, dsc2, dn2g, dg1) = _rowwise(
        norm2_bwd_fn, [dh2, x1, dy, mo], [norm2_g, sc2, g1], [(D, F32), (D, BF16)], [(1, D)] * 4, name="norm2_bwd")

    def merge_bwd_epi(dm, a1, a2, gi1, gi2):
        s1, s2 = _sigmoid(gi1.astype(F32)), _sigmoid(gi2.astype(F32))
        a1, a2 = a1.astype(F32), a2.astype(F32)
        return dm * s1, dm * s2, dm * a1 * (s1 * (1.0 - s1)), dm * a2 * (s2 * (1.0 - s2))

    da1, da2, dgi1, dgi2 = _mm([(dmo, w_out_f)], tb=True, name="out_proj_dx", outs=(BF16,) * 4, tn=512,
                               extras=[(a1, "mn"), (a2, "mn"), (g1_in, "mn"), (g2_in, "mn")], epi=merge_bwd_epi)
    dw_out = _mm([(merged, dmo)], ta=True, name="out_proj_dw", outs=(BF16,))
    dattn = _mm([(da1, w_bra)], tb=True, name="br_attn_dx", outs=(BF16,))
    dw_bra = _mm([(attn_o, da1)], ta=True, name="br_attn_dw", outs=(BF16,), split=w_br_attn.shape[-1])
    dsgu = _mm([(da2, w_brs)], tb=True, name="br_sgu_dx", outs=(BF16,))
    dw_brs = _mm([(sgu_o, da2)], ta=True, name="br_sgu_dw", outs=(BF16,), split=w_br_sgu.shape[-1])

    def sgu_bwd_fn(dso, u_in, v_in, ng, nb, ws, bst):
        u, xhat, rs, vnb = sgu_parts(u_in, v_in, ng, nb)
        dso = dso.astype(F32)
        lane = lax.broadcasted_iota(jnp.int32, (CH, LANES), 1)
        du, dvn, dws, dbs = [], [], [], jnp.zeros((CH, LANES), F32)
        for g in range(G):
            sl = slice(g * GD, (g + 1) * GD)
            wg = ws[g].astype(BF16)
            mixed = jnp.dot(wg, vnb[:, sl], preferred_element_type=F32) + bst[:, g:g + 1]
            du.append(dso[:, sl] * mixed)
            dmix = dso[:, sl] * u[:, sl]
            dmb = dmix.astype(BF16)
            dws.append(lax.dot_general(dmb, vnb[:, sl], (((1,), (1,)), ((), ())), preferred_element_type=F32))
            dbs = dbs + jnp.where(lane == g, jnp.sum(dmix, axis=1, keepdims=True), 0.0)
            dvn.append(lax.dot_general(wg, dmb, (((0,), (0,)), ((), ())), preferred_element_type=F32))
        du, dvn = jnp.concatenate(du, axis=1), jnp.concatenate(dvn, axis=1)
        dxh = dvn * ng
        dv = rs * (dxh - jnp.mean(dxh, axis=-1, keepdims=True) - xhat * jnp.mean(dxh * xhat, axis=-1, keepdims=True))
        return ((du * _gelu_grad(u_in.astype(F32)), dv * _gelu_grad(v_in.astype(F32))),
                (_colsum(dvn * xhat), _colsum(dvn), jnp.stack(dws), dbs))

    core = jnp.reshape(ac, (1,)).astype(jnp.int32)

    def dest_layout(dwf, i):
        K, Ns = big[i].shape
        if dwf.ndim == 2:
            dwf = dwf.reshape(K, 4, Ns).transpose(1, 0, 2) if col_sharded[i] else dwf.reshape(4, K, Ns)
        return dwf.reshape(4, 2, K // 2, Ns)

    def pair_sums(idx, g4, sib):
        return [_pair_add(g, s, core, name="rs_pair_add_" + tags[i]) for g, s, i in zip(g4, sib, idx)]

    early = [3, 4, 5, 6, 7]
    g4_early = [dest_layout(d, i) for d, i in zip([dw_bra, dw_brs, dw_out, dw_ffi, dw_ffo], early)]
    (du_in, dv_in), (d_sng, d_snb, d_ws, d_bs), sib_early = _rowwise(
        sgu_bwd_fn, [dsgu, u_in, v_in], [sgu_norm_g, sgu_norm_b, ws3, bs_t], [(SW, BF16), (SW, BF16)],
        [(1, SW), (1, SW), (G, CH, CH), (CH, LANES)], name="sgu_bwd", tm=CH, carry=_PairExchange(g4_early, "halves"))
    pair_early = pair_sums(early, g4_early, sib_early)
    dq_att, dk_att, dv_att, xchg_early = _attn_bwd(q_att, k_att, v_att, attn_o, lse, dattn, heads=H,
                                                   carry=_ChipExchange(pair_early, gather=False))

    def q_post_bwd_fn(dq, t, cos, slo, shi, g):
        outs, dg = [], jnp.zeros((1, HEAD_PAD), F32)
        for hd in range(H):
            sl = slice(hd * HEAD_PAD, (hd + 1) * HEAD_PAD)
            th = t[:, sl]
            r = _rms_stats(th, QK)
            tn = th * r
            dn = _rope_t(dq[:, sl] * scale, cos, slo, shi)
            dg = dg + _colsum(dn * tn)
            outs.append(_rms_bwd(dn * g, tn, r, QK))
        return (jnp.concatenate(outs, axis=1),), (dg,)

    (dq_raw,), (d_gq,) = _rowwise(q_post_bwd_fn, [dq_att, q_raw, cos_t, slo_t, shi_t], [gq_p],
                                  [(H * HEAD_PAD, BF16)], [(1, HEAD_PAD)], name="q_post_bwd")

    def k_post_bwd_fn(dk, dv, t, kvi, cos, slo, shi, g):
        kr = kvi[:, KVL:]
        outs, dg, dkr = [], jnp.zeros((1, HEAD_PAD), F32), jnp.zeros_like(kr)
        for hd in range(H):
            th = jnp.concatenate([t[:, hd * HEAD_PAD:hd * HEAD_PAD + NOPE], kr], axis=1)
            r = _rms_stats(th, QK)
            tn = th * r
            dn = _rope_t(dk[:, hd * HEAD_PAD:(hd + 1) * HEAD_PAD], cos, slo, shi)
            dg = dg + _colsum(dn * tn)
            dt = _rms_bwd(dn * g, tn, r, QK)
            dkr = dkr + dt[:, NOPE:]
            outs += [dt[:, :NOPE], dv[:, hd * VD:(hd + 1) * VD]]
        return (jnp.concatenate(outs, axis=1), dkr), (dg,)

    def reduced_halves(idx, xchg, pair):
        return [_sum_chips(t4, pr, chip1, name="rs_sum_" + tags[i]) for t4, pr, i in zip(xchg, pair, idx)]

    red_early = reduced_halves(early, xchg_early, pair_early)
    (dkv_raw, dkrope), (d_gk,), other_early = _rowwise(
        k_post_bwd_fn, [dk_att, dv_att, kv_raw, kvin, cos_k, slo_k, shi_k], [gk_p],
        [(H * HEAD_PAD, BF16), (2 * ROPE_DIM, F32)], [(1, HEAD_PAD)], name="k_post_bwd",
        carry=_PairExchange(red_early, "gather"))

    def q_norm_bwd_epi(dn, t, g):
        r = _rms_stats(t, QL)
        tn = t * r
        return _rms_bwd(dn * g, tn, r, QL), _colsum(dn * tn)

    dqc, d_qng_part = _mm([(dq_raw, w_uq_p)], tb=True, name="q_up_dx", outs=(BF16,), tn=QL, col_sums=1,
                          extras=[(qc, "mn"), (q_norm_g, "n")], epi=q_norm_bwd_epi)
    _, (d_qng,) = _rowwise(lambda a: ((), (_colsum(a),)), [d_qng_part[:, 0, :]], [], [], [(1, QL)],
                           name="q_norm_fold", tm=d_qng_part.shape[0])
    dw_uq_p = _mm([(qn, dq_raw)], ta=True, name="q_up_dw", outs=(BF16,))
    dkvn = _mm([(dkv_raw, w_ukv_f)], tb=True, name="kv_up_dx", outs=(F32,), tm=kv_rows)
    dw_ukv = _mm([(kvn, dkv_raw)], ta=True, name="kv_up_dw", outs=(BF16,), tk=kv_rows)

    def kv_norm_bwd_fn(dn, dkr, t, g):
        kvc = t[:, :KVL]
        r = _rms_stats(kvc, KVL)
        tn = kvc * r
        return (jnp.concatenate([_rms_bwd(dn * g, tn, r, KVL), dkr], axis=1),), (_colsum(dn * tn),)

    (dkvin,), (d_kvng,) = _rowwise(kv_norm_bwd_fn, [dkvn, dkrope, kvin], [kv_norm_g], [(KVP, BF16)], [(1, KVL)],
                                   name="kv_norm_bwd")
    dkvin_x, dkvin_c = dkvin[:N], dkvin[N:]

    dctx_h = _mm([(dkvin_c, w_kv)], tb=True, name="proj_kv_ctx_dx", outs=(F32,))
    dw_q = _mm([(h, dqc)], ta=True, name="proj_q_dw", outs=(BF16,))
    dw_kv = _mm([(h, dkvin_x), (ctx_h, dkvin_c)], ta=True, name="proj_kv_dw", outs=(BF16,))
    dw_u = _mm([(h, du_in)], ta=True, name="proj_u_dw", outs=(BF16,))
    dw_v = _mm([(h, dv_in)], ta=True, name="proj_v_dw", outs=(BF16,))
    dw_g1 = _mm([(h, dgi1)], ta=True, name="proj_g1_dw", outs=(BF16,))
    dw_g2 = _mm([(h, dgi2)], ta=True, name="proj_g2_dw", outs=(BF16,))

    dw_in_f = jnp.concatenate([dw_q, dw_kv[:, :KVL + ROPE_DIM], dw_u, dw_v, dw_g1, dw_g2], axis=1)
    dw_uq_f = dw_uq_p.reshape(QL, H, HEAD_PAD)[:, :, :QK].reshape(QL, H * QK)
    late = [0, 1, 2]
    g4_late = [dest_layout(d, i) for d, i in zip([dw_in_f, dw_uq_f, dw_ukv], late)]
    pair_late = pair_sums(late, g4_late, _exchange_alone(_PairExchange(g4_late, "halves"), name="rs_pair_late"))
    dh, xchg_late = _mm([(dqc, w_q), (dkvin_x, w_kv), (du_in, w_u), (dv_in, w_v), (dgi1, w_g1), (dgi2, w_g2)],
                        tb=True, name="proj_dx", outs=(F32,), tn=1024, tk=512,
                        carry=_ChipExchange(pair_late, gather=False))

    def norm1_bwd_fn(dhv, t, dres, g, sc):
        r = _rms_stats(t, D)
        tn = t * r
        dxg = dhv * (1.0 + sc)
        return (dres + _rms_bwd(dxg * g, tn, r, D),), (_colsum(dhv), _colsum(dhv * (tn * g)), _colsum(dxg * tn))

    (grad_x,), (dsh1, dsc1, dn1g_x) = _rowwise(norm1_bwd_fn, [dh, x2, dx1], [norm1_g, sc1], [(D, F32)], [(1, D)] * 3,
                                               name="norm1_bwd")
    _, (dsh1c, dsc1c, dn1g_c) = _rowwise(norm1_bwd_fn, [dctx_h, ctx2, jnp.zeros_like(ctx2)], [norm1_g, sc1c],
                                         [(D, F32)], [(1, D)] * 3, name="norm1_ctx_bwd")

    small = [dsh1, dsc1, dg1, dsh2, dsc2, dg2,
             dsh1c, dsc1c, dn1g_x, dn1g_c, d_qng, d_kvng, d_gq, d_gk, d_sng, d_snb, dn2g, loss_cols]
    small_sizes = [a.shape[1] for a in small]
    sm_row = jnp.concatenate(small, axis=1)
    sm_mat = jnp.concatenate([d_ws.reshape(G * CH, CH), d_bs], axis=0)
    row_all, mat_all = _all_gather8([sm_row, sm_mat], name="ag_small", in_vmem=True)
    row_sum = _sum_blocks(row_all, name="sum_small_rows", out_dtype=F32)
    mat_sum = _sum_blocks(mat_all, name="sum_small_mats", out_dtype=F32)
    dmod_rows = row_all[:, 0, :NMOD]
    (_, _, _, _, _, _, t_sh1c, t_sc1c, t_n1x, t_n1c, g_qng, g_kvng, t_gq, t_gk, g_sng, g_snb, g_n2g,
     t_loss) = _split_lanes(row_sum, small_sizes)
    g_ws, t_bs = mat_sum[:G * CH], mat_sum[G * CH:]
    dmodc_row = jnp.concatenate([t_sh1c, t_sc1c, jnp.zeros((1, NMOD - 2 * D), F32)], axis=1)
    dmod16 = jnp.concatenate([dmod_rows, dmodc_row, jnp.zeros((BF16_SUBLANES - 9, NMOD), F32)], axis=0)

    def small_fn(rows, n1x, n1c, lossv):
        return (), (_colsum(rows), n1x + n1c, jnp.sum(lossv, axis=1, keepdims=True))

    _, (g_bmod, g_n1g, loss11) = _rowwise(small_fn, [dmod16], [t_n1x, t_n1c, t_loss], [], [(1, NMOD), (1, D), (1, 1)],
                                          name="small_reduce", tm=16)
    dmod_loc = lax.dynamic_slice_in_dim(dmod16, my_chip * NM, NM, axis=1)
    g_wmod = _mm([(silu_c, dmod_loc)], ta=True, name="mod_dw", outs=(F32,), tn=512)
    dsilu_part = _mm([(dmod_loc, wm)], tb=True, name="mod_dx", outs=(F32,), tk=512)
    part_all = _all_gather8([dsilu_part[8:9]], name="ag_cctx", in_vmem=True)[0]

    def cctx_fn(parts, dsl):
        return (), ((parts[0:1] + parts[2:3] + parts[4:5] + parts[6:7]) * dsl,)

    _, (g_cctx,) = _rowwise(cctx_fn, [part_all[:, 0, :]], [dsilu_c[8:9]], [], [(1, D)], name="cctx_grad", tm=8)

    red_late = reduced_halves(late, xchg_late, pair_late)
    other_late = _exchange_alone(_PairExchange(red_late, "gather"), name="rs_halves_late")
    grad_halves = dict(zip(tags, zip(red_late + red_early, other_late + other_early)))

    mod_upd = _adamw(w_mod[0], g_wmod, m_w_mod[0], v_w_mod[0], name="adamw_w_mod")
    grads = dict(
        c_ctx=g_cctx.reshape(D), w_mod=g_wmod[None], b_mod=g_bmod, norm1_g=g_n1g,
        q_norm_g=g_qng, kv_norm_g=g_kvng, qk_norm_q=t_gq[:, :QK], qk_norm_k=t_gk[:, :QK], sgu_norm_g=g_sng,
        sgu_norm_b=g_snb, w_spatial=g_ws.reshape(w_spatial.shape), b_spatial=t_bs[:, :G].T[None], norm2_g=g_n2g)
    weights = dict(c_ctx=c_ctx, w_mod=w_mod, b_mod=b_mod, norm1_g=norm1_g, w_in=w_in, q_norm_g=q_norm_g,
                   kv_norm_g=kv_norm_g, w_uq=w_uq, w_ukv=w_ukv, qk_norm_q=qk_norm_q, qk_norm_k=qk_norm_k,
                   sgu_norm_g=sgu_norm_g, sgu_norm_b=sgu_norm_b, w_spatial=w_spatial, b_spatial=b_spatial,
                   w_br_attn=w_br_attn, w_br_sgu=w_br_sgu, w_out=w_out, norm2_g=norm2_g, w_ffn_in=w_ffn_in,
                   w_ffn_out=w_ffn_out)
    m_in = dict(c_ctx=m_c_ctx, w_mod=m_w_mod, b_mod=m_b_mod, norm1_g=m_norm1_g, w_in=m_w_in, q_norm_g=m_q_norm_g,
                kv_norm_g=m_kv_norm_g, w_uq=m_w_uq, w_ukv=m_w_ukv, qk_norm_q=m_qk_norm_q, qk_norm_k=m_qk_norm_k,
                sgu_norm_g=m_sgu_norm_g, sgu_norm_b=m_sgu_norm_b, w_spatial=m_w_spatial, b_spatial=m_b_spatial,
                w_br_attn=m_w_br_attn, w_br_sgu=m_w_br_sgu, w_out=m_w_out, norm2_g=m_norm2_g, w_ffn_in=m_w_ffn_in,
                w_ffn_out=m_w_ffn_out)
    v_in_ = dict(c_ctx=v_c_ctx, w_mod=v_w_mod, b_mod=v_b_mod, norm1_g=v_norm1_g, w_in=v_w_in, q_norm_g=v_q_norm_g,
                 kv_norm_g=v_kv_norm_g, w_uq=v_w_uq, w_ukv=v_w_ukv, qk_norm_q=v_qk_norm_q, qk_norm_k=v_qk_norm_k,
                 sgu_norm_g=v_sgu_norm_g, sgu_norm_b=v_sgu_norm_b, w_spatial=v_w_spatial, b_spatial=v_b_spatial,
                 w_br_attn=v_w_br_attn, w_br_sgu=v_w_br_sgu, w_out=v_w_out, norm2_g=v_norm2_g, w_ffn_in=v_w_ffn_in,
                 w_ffn_out=v_w_ffn_out)
    names = list(weights)
    big_names = ("w_mod", "w_in", "w_uq", "w_ukv", "w_br_attn", "w_br_sgu", "w_out", "w_ffn_in", "w_ffn_out")
    out_g, out_d, out_m, out_v = {}, {}, {}, {}
    out_g["w_mod"] = grads["w_mod"]
    out_d["w_mod"], out_m["w_mod"], out_v["w_mod"] = [t[None] for t in mod_upd[:3]]
    for nm in big_names[1:]:
        res = _adamw_halves(weights[nm][0], *grad_halves[nm], m_in[nm][0], v_in_[nm][0], core, name="adamw_" + nm)
        out_g[nm], out_d[nm], out_m[nm], out_v[nm] = [t[None] for t in res]
    row_names = [nm for nm in names if nm not in big_names and nm not in ("w_spatial", "b_spatial")]
    widths = [-(-weights[nm].size // LANES) * LANES for nm in row_names]

    def as_row(d):
        return jnp.concatenate([jnp.pad(d[nm].reshape(1, -1), ((0, 0), (0, wd - d[nm].size)))
                                for nm, wd in zip(row_names, widths)], axis=1)

    def as_mat(d):
        return jnp.concatenate([d["w_spatial"].reshape(G * CH, CH), d["b_spatial"].reshape(G, CH)], axis=0)

    row_res = _adamw(as_row(weights), as_row(grads), as_row(m_in), as_row(v_in_), name="adamw_rows")
    mat_res = _adamw(as_mat(weights), as_mat(grads), as_mat(m_in), as_mat(v_in_), name="adamw_spatial")
    for tgt, row, mat in zip((out_d, out_m, out_v), row_res, mat_res):
        for nm, seg in zip(row_names, _split_lanes(row, widths)):
            tgt[nm] = seg[:, :weights[nm].size].reshape(weights[nm].shape)
        tgt["w_spatial"] = mat[:G * CH].reshape(w_spatial.shape)
        tgt["b_spatial"] = mat[G * CH:].reshape(b_spatial.shape)
    for nm in row_names + ["w_spatial", "b_spatial"]:
        out_g[nm] = grads[nm].reshape(weights[nm].shape)

    loss = loss11.reshape(())
    return (loss, grad_x[None], *[out_g[n] for n in names], *[out_d[n] for n in names],
            *[out_m[n] for n in names], *[out_v[n] for n in names])
```

```python
import math

import jax
import jax.numpy as jnp
from jax import lax
from jax.experimental import pallas as pl
from jax.experimental.pallas import tpu as pltpu

F32, BF16 = jnp.float32, jnp.bfloat16
MESH = pl.DeviceIdType.MESH

LANES = 128
F32_SUBLANES = 8
BF16_SUBLANES = 16
MXU_DIM = 256
VMEM_LIMIT_BYTES = 56 * 1024 * 1024

EPS = 1e-6
ROPE_DIM = 64
ROPE_THETA = 10000.0
GRID_W = 64
HEAD_PAD = 256
ADAM_LR, ADAM_B1, ADAM_B2, ADAM_EPS, ADAM_WD, ADAM_STEP = 0.001, 0.9, 0.999, 1e-08, 0.01, 10


def _tile(dim, pref, align=LANES):
    if dim <= pref:
        return dim
    t = (pref // align) * align
    while t >= align:
        if dim % t == 0:
            return t
        t -= align
    return dim


def _params(sem=None):
    return pltpu.CompilerParams(dimension_semantics=sem, vmem_limit_bytes=VMEM_LIMIT_BYTES)


def _sds(shape, dtype):
    return jax.ShapeDtypeStruct(tuple(shape), dtype)


def _mm(pairs, *, name, ta=False, tb=False, outs=(F32,), tm=1024, tn=1024, tk=2048, extras=(), epi=None,
        split=None, into=None, carry=None, col_sums=0):
    dual = len(pairs[0]) == 3
    a0, b0 = pairs[0][0], pairs[0][1]
    M = a0.shape[1] if ta else a0.shape[0]
    N = b0.shape[0] if tb else b0.shape[1]
    tm, tn = _tile(M, tm), _tile(N if split is None else split, tn)
    ks = [(p[0].shape[0] if ta else p[0].shape[1]) for p in pairs]
    tks = [_tile(k, tk) for k in ks]
    nks = [k // t for k, t in zip(ks, tks)]
    offs = [sum(nks[:i]) for i in range(len(pairs))]
    nk_total = sum(nks)
    single = len(pairs) == 1

    def kidx(kk, p):
        return kk if single else jnp.clip(kk - offs[p], 0, nks[p] - 1)

    in_specs, operands = [], []
    for p, pr in enumerate(pairs):
        if ta:
            in_specs.append(pl.BlockSpec((tks[p], tm), lambda i, j, kk, p=p: (kidx(kk, p), i)))
        else:
            in_specs.append(pl.BlockSpec((tm, tks[p]), lambda i, j, kk, p=p: (i, kidx(kk, p))))
        operands.append(pr[0])
        for b in pr[1:]:
            if tb:
                in_specs.append(pl.BlockSpec((tn, tks[p]), lambda i, j, kk, p=p: (j, kidx(kk, p))))
            else:
                in_specs.append(pl.BlockSpec((tks[p], tn), lambda i, j, kk, p=p: (kidx(kk, p), j)))
            operands.append(b)
    for arr, kind in extras:
        if kind == "mn":
            in_specs.append(pl.BlockSpec((tm, tn), lambda i, j, kk: (i, j)))
        else:
            in_specs.append(pl.BlockSpec((1, tn), lambda i, j, kk: (0, j)))
        operands.append(arr)
    n_in = len(operands)
    n_ex = len(extras)
    per = 3 if dual else 2
    dims = (((0 if ta else 1,), (1 if tb else 0,)), ((), ()))

    n_acc = 2 if dual else 1

    def products(ins, p):
        a = ins[per * p][...].astype(BF16)
        return [lax.dot_general(a, ins[per * p + 1 + q][...].astype(BF16), dims, preferred_element_type=F32)
                for q in range(n_acc)]

    def finish(ins, out_refs, acc_vals):
        vals = acc_vals + [r[...] for r in ins[n_in - n_ex:]]
        res = epi(*vals) if epi is not None else (vals[0],)
        for o, r in zip(out_refs, res):
            o[...] = jnp.broadcast_to(r, o.shape).astype(o.dtype)

    out_specs = [pl.BlockSpec((tm, tn), lambda i, j, kk: (i, j)) for _ in outs]
    out_specs += [pl.BlockSpec((None, F32_SUBLANES, tn), lambda i, j, kk: (i, 0, j)) for _ in range(col_sums)]
    out_shape = [_sds((M, N), d) for d in outs] + [_sds((M // tm, F32_SUBLANES, N), F32) for _ in range(col_sums)]
    aliases = {}
    n_alias = 0
    if split is not None:
        nps = split // tn
        lead = 0 if into is None else into[1]
        out_specs = [pl.BlockSpec((None, tm, tn), lambda i, j, kk: (j // nps + lead, i, j % nps))]
        out_shape = [_sds((N // split if into is None else into[0].shape[0], M, split), outs[0])]
        if into is not None:
            in_specs.append(pl.BlockSpec(memory_space=pl.ANY))
            operands.append(into[0])
            aliases, n_alias = {n_in: 0}, 1

    grid = (M // tm, N // tn, nk_total)
    n_out = len(outs) + col_sums

    def at_step(first):
        ids = [pl.program_id(d) for d in range(3)]
        cond = None
        for d, g in zip(ids, grid):
            t = d == (0 if first else g - 1)
            cond = t if cond is None else cond & t
        return cond

    def body(*refs):
        ins, out_refs, accs, start, wait = _split_refs(refs, n_in + n_alias, n_out, carry)
        ins = ins[:n_in]
        if carry is not None:
            pl.when(at_step(True))(start)
        if nk_total == 1:
            finish(ins, out_refs, products(ins, 0))
        else:
            kk = pl.program_id(2)

            @pl.when(kk == 0)
            def _():
                for acc, v in zip(accs, products(ins, 0)):
                    acc[...] = v

            for p in range(len(pairs)):
                lo = max(offs[p], 1)

                @pl.when((kk >= lo) & (kk < offs[p] + nks[p]))
                def _(p=p):
                    for acc, v in zip(accs, products(ins, p)):
                        acc[...] += v

            @pl.when(kk == nk_total - 1)
            def _():
                finish(ins, out_refs, [acc[...] for acc in accs])
        if carry is not None:
            pl.when(at_step(False))(wait)

    ex = carry
    res = pl.pallas_call(
        body, name=name, grid=grid, in_specs=in_specs + ([] if ex is None else ex.in_specs),
        out_specs=out_specs + ([] if ex is None else ex.out_specs),
        out_shape=out_shape + ([] if ex is None else ex.out_shape), input_output_aliases=aliases,
        scratch_shapes=[pltpu.VMEM((tm, tn), F32) for _ in range(n_acc if nk_total > 1 else 0)]
        + ([] if ex is None else ex.scratch),
        compiler_params=_params(("arbitrary",) * 3 if ex is not None else ("parallel", "parallel", "arbitrary")),
    )(*operands, *([] if ex is None else ex.xs))
    if ex is not None:
        return (res[0] if n_out == 1 else res[:n_out]), list(res[n_out:])
    return res[0] if n_out == 1 else res


def _rowwise(fn, rows, vecs, out_rows, out_accs=(), *, name, tm=256, tc=None, carry=None):
    M = rows[0].shape[0]
    tm = _tile(M, tm, BF16_SUBLANES)
    nrow = M // tm
    C = rows[0].shape[1]
    ncol = 1 if tc is None else C // _tile(C, tc)
    tcol = None if tc is None else _tile(C, tc)

    def colwise(shape):
        return tc is not None and len(shape) == 2 and shape[0] == 1 and shape[1] == C

    def vspec(shape):
        if colwise(shape):
            return pl.BlockSpec((1, tcol), lambda j, i: (0, j))
        return pl.BlockSpec(tuple(shape), lambda j, i, n=len(shape): (0,) * n)

    def rspec(width):
        if tc is None:
            return pl.BlockSpec((tm, width), lambda j, i: (i, 0))
        return pl.BlockSpec((tm, tcol), lambda j, i: (i, j))

    in_specs = [rspec(r.shape[1]) for r in rows] + [vspec(v.shape) for v in vecs]
    out_specs = [rspec(c) for c, _ in out_rows] + [vspec(s) for s in out_accs]
    out_shape = [_sds((M, c), d) for c, d in out_rows] + [_sds(s, F32) for s in out_accs]
    n_in, n_or = len(rows) + len(vecs), len(out_rows)

    n_out = n_or + len(out_accs)
    ex = carry

    def body(*refs):
        ins, outs, _, start, wait = _split_refs(refs, n_in, n_out, ex)
        o_rows, o_accs = outs[:n_or], outs[n_or:]
        if ex is not None:
            pl.when((pl.program_id(0) == 0) & (pl.program_id(1) == 0))(start)
        r_out, a_out = fn(*[r[...] for r in ins])
        for o, r in zip(o_rows, r_out):
            o[...] = r.astype(o.dtype)
        i = pl.program_id(1)

        @pl.when(i == 0)
        def _():
            for o, a in zip(o_accs, a_out):
                o[...] = a

        @pl.when(i > 0)
        def _():
            for o, a in zip(o_accs, a_out):
                o[...] += a

        if ex is not None:
            pl.when((pl.program_id(0) == ncol - 1) & (pl.program_id(1) == nrow - 1))(wait)

    res = pl.pallas_call(
        body, name=name, grid=(ncol, nrow), in_specs=in_specs + ([] if ex is None else ex.in_specs),
        out_specs=out_specs + ([] if ex is None else ex.out_specs),
        out_shape=out_shape + ([] if ex is None else ex.out_shape),
        scratch_shapes=[] if ex is None else ex.scratch,
        compiler_params=_params(("arbitrary", "arbitrary") if ex is not None else ("parallel", "arbitrary")),
    )(*rows, *vecs, *([] if ex is None else ex.xs))
    if ex is not None:
        return res[:n_or], res[n_or:n_out], list(res[n_out:])
    return res[:n_or], res[n_or:]


def _colsum(t):
    return jnp.sum(t, axis=0, keepdims=True)


def _gelu(t):
    return 0.5 * t * (1.0 + lax.erf(t * math.sqrt(0.5)))


def _gelu_grad(t):
    return 0.5 * (1.0 + lax.erf(t * math.sqrt(0.5))) + t * jnp.exp(-0.5 * t * t) * (1.0 / math.sqrt(2.0 * math.pi))


def _sigmoid(t):
    return 1.0 / (1.0 + jnp.exp(-t))


def _rms_stats(t, width):
    return lax.rsqrt(jnp.sum(t * t, axis=-1, keepdims=True) * (1.0 / width) + EPS)


def _rms_bwd(dn, tn, r, width):
    return r * (dn - tn * (jnp.sum(dn * tn, axis=-1, keepdims=True) * (1.0 / width)))


def _place():
    return lax.axis_index("x"), lax.axis_index("y"), lax.axis_index("c")


class _ChipExchange:
    def __init__(self, xs, gather):
        self.xs, self.gather, self.n = list(xs), gather, len(xs)
        self.in_specs = [pl.BlockSpec(memory_space=pl.ANY)] * self.n
        self.out_specs = [pl.BlockSpec(memory_space=pl.ANY)] * self.n
        self.out_shape = [_sds((4,) + (x.shape if gather else x.shape[1:]), x.dtype) for x in self.xs]
        self.scratch = [pltpu.SemaphoreType.DMA((self.n, 3)), pltpu.SemaphoreType.DMA((self.n, 3))]

    def bind(self, x_refs, out_refs, send_sems, recv_sems):
        x, y, c = _place()
        p = 2 * x + y
        chips = [(1 - x, y), (x, 1 - y), (1 - x, 1 - y)]

        def copy(w, k, outgoing):
            qx, qy = chips[k]
            there = 2 * qx + qy
            if self.gather:
                src = x_refs[w]
            else:
                src = x_refs[w].at[there if outgoing else p]
            return pltpu.make_async_remote_copy(
                src_ref=src, dst_ref=out_refs[w].at[p if outgoing else there], send_sem=send_sems.at[w, k],
                recv_sem=recv_sems.at[w, k], device_id=(qx, qy, c), device_id_type=MESH)

        def start():
            for w in range(self.n):
                for k in range(3):
                    copy(w, k, True).start()

        def wait():
            for w in range(self.n):
                for k in range(3):
                    copy(w, k, False).wait_recv()
            for w in range(self.n):
                for k in range(3):
                    copy(w, k, True).wait_send()

        return start, wait


class _PairExchange:
    def __init__(self, xs, mode):
        self.xs, self.mode, self.n = list(xs), mode, len(xs)
        self.in_specs = [pl.BlockSpec(memory_space=pl.ANY)] * self.n
        self.out_specs = [pl.BlockSpec(memory_space=pl.ANY)] * self.n
        shape = {"halves": lambda s: (4,) + s[2:], "forward": lambda s: s, "gather": lambda s: (2,) + s}[mode]
        self.out_shape = [_sds(shape(x.shape), x.dtype) for x in self.xs]
        self.scratch = [pltpu.SemaphoreType.DMA((self.n, 3)), pltpu.SemaphoreType.DMA((self.n, 3))]

    def bind(self, x_refs, out_refs, send_sems, recv_sems):
        x, y, c = _place()
        chips = [(1 - x, y), (x, 1 - y), (1 - x, 1 - y)]

        def copy(w, src, dst, k):
            return pltpu.make_async_remote_copy(src_ref=src, dst_ref=dst, send_sem=send_sems.at[w, k],
                                                recv_sem=recv_sems.at[w, k], device_id=(x, y, 1 - c),
                                                device_id_type=MESH)

        def start():
            for w, (xr, orf) in enumerate(zip(x_refs, out_refs)):
                if self.mode == "halves":
                    for q in range(4):
                        copy(w, xr.at[q, 1 - c], orf.at[q], 0).start()
                elif self.mode == "forward":
                    for k, (qx, qy) in enumerate(chips):
                        copy(w, xr.at[2 * qx + qy], orf.at[2 * qx + qy], k).start()
                else:
                    copy(w, xr, orf.at[c], 0).start()

        def wait():
            for w, (xr, orf) in enumerate(zip(x_refs, out_refs)):
                if self.mode == "halves":
                    copy(w, orf, orf, 0).wait()
                elif self.mode == "forward":
                    for k, (qx, qy) in enumerate(chips):
                        copy(w, xr.at[2 * qx + qy], orf.at[2 * qx + qy], k).wait()
                else:
                    cp = copy(w, xr, orf.at[1 - c], 0)
                    cp.wait_recv()
                    cp.wait_send()

        return start, wait


def _split_refs(refs, n_in, n_out, ex):
    ne = 0 if ex is None else ex.n
    ins, xin = refs[:n_in], refs[n_in:n_in + ne]
    outs, xout = refs[n_in + ne:n_in + ne + n_out], refs[n_in + ne + n_out:n_in + 2 * ne + n_out]
    rest = refs[n_in + 2 * ne + n_out:]
    if ex is None:
        return ins, outs, rest, None, None
    start, wait = ex.bind(xin, xout, rest[-2], rest[-1])
    return ins, outs, rest[:-2], start, wait


def _attn_fwd(q, k, v, *, heads, tq=512, carry=None):
    N, M = q.shape[0], k.shape[0]
    tq = _tile(N, tq)
    sub = _tile(tq, MXU_DIM)
    vd = v.shape[1] // heads
    nq = N // tq

    def body(*refs):
        (q_ref, k_ref, v_ref), (o_ref, lse_ref), _, start, wait = _split_refs(refs, 3, 2, carry)
        if carry is not None:
            pl.when((pl.program_id(0) == 0) & (pl.program_id(1) == 0))(start)
        for sb in range(tq // sub):
            rows = pl.ds(sb * sub, sub)
            s = lax.dot_general(q_ref[rows, :], k_ref[...], (((1,), (1,)), ((), ())), preferred_element_type=F32)
            m = jnp.max(s, axis=-1, keepdims=True)
            p = jnp.exp(s - m)
            l = jnp.sum(p, axis=-1, keepdims=True)
            o = jnp.dot(p.astype(BF16), v_ref[...], preferred_element_type=F32) / l
            o_ref[rows, :] = o.astype(o_ref.dtype)
            lse_ref[rows, :] = jnp.broadcast_to(m + jnp.log(l), (sub, vd))
        if carry is not None:
            pl.when((pl.program_id(0) == heads - 1) & (pl.program_id(1) == nq - 1))(wait)

    ex = carry
    res = pl.pallas_call(
        body, name="attn_fwd", grid=(heads, nq),
        in_specs=[pl.BlockSpec((tq, HEAD_PAD), lambda h, i: (i, h)),
                  pl.BlockSpec((M, HEAD_PAD), lambda h, i: (0, h)),
                  pl.BlockSpec((M, vd), lambda h, i: (0, h))] + ([] if ex is None else ex.in_specs),
        out_specs=[pl.BlockSpec((tq, vd), lambda h, i: (i, h)),
                   pl.BlockSpec((tq, vd), lambda h, i: (i, h))] + ([] if ex is None else ex.out_specs),
        out_shape=[_sds((N, heads * vd), BF16), _sds((N, heads * vd), F32)] + ([] if ex is None else ex.out_shape),
        scratch_shapes=[] if ex is None else ex.scratch,
        compiler_params=_params(("arbitrary", "arbitrary")),
    )(q, k, v, *([] if ex is None else ex.xs))
    return res[0], res[1], list(res[2:])


def _attn_bwd(q, k, v, o, lse, do, *, heads, tq=512, carry=None):
    N, M = q.shape[0], k.shape[0]
    tq = _tile(N, tq)
    vd = v.shape[1] // heads
    nq = N // tq
    sub = _tile(tq, MXU_DIM)
    nt = (((1,), (1,)), ((), ()))
    tn = (((0,), (0,)), ((), ()))

    def body(*refs):
        (q_ref, k_ref, v_ref, o_ref, lse_ref, do_ref), (dq_ref, dk_ref, dv_ref), _, start, wait = _split_refs(
            refs, 6, 3, carry)
        if carry is not None:
            pl.when((pl.program_id(0) == 0) & (pl.program_id(1) == 0))(start)
        i = pl.program_id(1)
        kb, vb = k_ref[...], v_ref[...]
        parts = []
        for sb in range(tq // sub):
            rows = pl.ds(sb * sub, sub)
            qb, dob = q_ref[rows, :], do_ref[rows, :]
            delta = jnp.sum(dob.astype(F32) * o_ref[rows, :].astype(F32), axis=-1, keepdims=True)
            s = lax.dot_general(qb, kb, nt, preferred_element_type=F32)
            p = jnp.exp(s - lse_ref[rows, :][:, :1])
            dp = lax.dot_general(dob, vb, nt, preferred_element_type=F32)
            ds = (p * (dp - delta)).astype(BF16)
            dq_ref[rows, :] = jnp.dot(ds, kb, preferred_element_type=F32)
            parts.append((lax.dot_general(ds, qb, tn, preferred_element_type=F32),
                          lax.dot_general(p.astype(BF16), dob, tn, preferred_element_type=F32)))

        dk_step, dv_step = parts[0]
        for dk_part, dv_part in parts[1:]:
            dk_step, dv_step = dk_step + dk_part, dv_step + dv_part

        @pl.when(i == 0)
        def _():
            dk_ref[...] = dk_step
            dv_ref[...] = dv_step

        @pl.when(i > 0)
        def _():
            dk_ref[...] += dk_step
            dv_ref[...] += dv_step

        if carry is not None:
            pl.when((pl.program_id(0) == heads - 1) & (pl.program_id(1) == nq - 1))(wait)

    ex = carry
    res = pl.pallas_call(
        body, name="attn_bwd", grid=(heads, nq),
        in_specs=[pl.BlockSpec((tq, HEAD_PAD), lambda h, i: (i, h)),
                  pl.BlockSpec((M, HEAD_PAD), lambda h, i: (0, h)),
                  pl.BlockSpec((M, vd), lambda h, i: (0, h)),
                  pl.BlockSpec((tq, vd), lambda h, i: (i, h)),
                  pl.BlockSpec((tq, vd), lambda h, i: (i, h)),
                  pl.BlockSpec((tq, vd), lambda h, i: (i, h))] + ([] if ex is None else ex.in_specs),
        out_specs=[pl.BlockSpec((tq, HEAD_PAD), lambda h, i: (i, h)),
                   pl.BlockSpec((M, HEAD_PAD), lambda h, i: (0, h)),
                   pl.BlockSpec((M, vd), lambda h, i: (0, h))] + ([] if ex is None else ex.out_specs),
        out_shape=[_sds((N, heads * HEAD_PAD), F32), _sds((M, heads * HEAD_PAD), F32),
                   _sds((M, heads * vd), F32)] + ([] if ex is None else ex.out_shape),
        scratch_shapes=[] if ex is None else ex.scratch,
        compiler_params=_params(("arbitrary", "arbitrary")),
    )(q, k, v, o, lse, do, *([] if ex is None else ex.xs))
    return res[0], res[1], res[2], list(res[3:])


def _comm_call(body, xs, out_shapes, n_sems, name, in_vmem):
    space = pltpu.VMEM if in_vmem else pl.ANY
    n = len(xs)

    def wrapped(*refs):
        body(refs[:n], refs[n:2 * n], *refs[2 * n:])

    return pl.pallas_call(
        wrapped, name=name, out_shape=list(out_shapes),
        in_specs=[pl.BlockSpec(memory_space=space)] * n, out_specs=[pl.BlockSpec(memory_space=space)] * n,
        scratch_shapes=[pltpu.SemaphoreType.DMA((n, n_sems)), pltpu.SemaphoreType.DMA((n, n_sems)),
                        pltpu.SemaphoreType.DMA((n,))],
        compiler_params=pltpu.CompilerParams(vmem_limit_bytes=VMEM_LIMIT_BYTES),
    )(*xs)


def _all_gather8(blks, *, name, in_vmem):
    def body(x_refs, out_refs, send_sems, recv_sems, local_sems):
        x, y, c = _place()
        me, sibling = (x, y, c), (x, y, 1 - c)
        chips = [(1 - x, y), (x, 1 - y), (1 - x, 1 - y)]
        waits = []
        for w, (x_ref, out_ref) in enumerate(zip(x_refs, out_refs)):
            def slot(px, py, pc, out_ref=out_ref):
                return out_ref.at[4 * px + 2 * py + pc]

            def copy(k, block, to, src=None, w=w, slot=slot):
                return pltpu.make_async_remote_copy(
                    src_ref=slot(*block) if src is None else src, dst_ref=slot(*block),
                    send_sem=send_sems.at[w, k], recv_sem=recv_sems.at[w, k], device_id=to, device_id_type=MESH)

            mine = pltpu.make_async_copy(x_ref, slot(*me), local_sems.at[w])
            mine.start()
            first = [copy(0, me, sibling, src=x_ref)]
            first += [copy(1 + j, me, (*chip, c), src=x_ref) for j, chip in enumerate(chips)]
            for cp in first:
                cp.start()
            waits.append((copy, mine, first))
        for copy, mine, first in waits:
            passed = [copy(4 + j, (*chip, c), sibling) for j, chip in enumerate(chips)]
            for j, chip in enumerate(chips):
                copy(1 + j, (*chip, c), me).wait_recv()
                passed[j].start()
            copy(0, sibling, me).wait_recv()
            for j, chip in enumerate(chips):
                copy(4 + j, (*chip, 1 - c), me).wait_recv()
            for cp in first + passed:
                cp.wait_send()
            mine.wait()

    return _comm_call(body, blks, [_sds((8,) + b.shape, b.dtype) for b in blks], 7, name, in_vmem)


def _gather_others(blks, *, name):
    def body(x_refs, out_refs, send_sems, recv_sems, local_sems):
        x, y, c = _place()
        own, xn, yn, dg = (x, y), (1 - x, y), (x, 1 - y), (1 - x, 1 - y)

        def slot(w, chip, core):
            return out_refs[w].at[4 * chip[0] + 2 * chip[1] + core]

        def cp(w, k, src, dst, chip, core):
            return pltpu.make_async_remote_copy(src_ref=src, dst_ref=dst, send_sem=send_sems.at[w, k],
                                                recv_sem=recv_sems.at[w, k], device_id=(*chip, core),
                                                device_id_type=MESH)

        def halves(w):
            h = x_refs[w].shape[0] // 2
            return pl.ds(0, h), pl.ds(h, h)

        sends = []
        for w, x_ref in enumerate(x_refs):
            sends += [cp(w, 0, x_ref, slot(w, own, c), xn, c), cp(w, 1, x_ref, slot(w, own, c), yn, c)]
        for s in sends:
            s.start()
        for w, x_ref in enumerate(x_refs):
            lo, hi = halves(w)
            cp(w, 1, x_ref, slot(w, yn, c), yn, c).wait_recv()
            passed = [cp(w, 2, slot(w, yn, c).at[lo], slot(w, yn, c).at[lo], xn, c),
                      cp(w, 4, slot(w, yn, c), slot(w, yn, c), own, 1 - c)]
            cp(w, 0, x_ref, slot(w, xn, c), xn, c).wait_recv()
            passed += [cp(w, 3, slot(w, xn, c).at[hi], slot(w, xn, c).at[hi], yn, c),
                       cp(w, 5, slot(w, xn, c), slot(w, xn, c), own, 1 - c)]
            for s in passed:
                s.start()
            sends += passed
        for w in range(len(x_refs)):
            lo, hi = halves(w)
            cp(w, 2, slot(w, dg, c).at[lo], slot(w, dg, c).at[lo], xn, c).wait_recv()
            cp(w, 3, slot(w, dg, c).at[hi], slot(w, dg, c).at[hi], yn, c).wait_recv()
            passed = [cp(w, 6, slot(w, dg, c), slot(w, dg, c), own, 1 - c)]
            passed[0].start()
            sends += passed
        for w in range(len(x_refs)):
            cp(w, 4, slot(w, yn, c), slot(w, yn, 1 - c), own, 1 - c).wait_recv()
            cp(w, 5, slot(w, xn, c), slot(w, xn, 1 - c), own, 1 - c).wait_recv()
            cp(w, 6, slot(w, dg, c), slot(w, dg, 1 - c), own, 1 - c).wait_recv()
        for s in sends:
            s.wait_send()

    return list(_comm_call(body, blks, [_sds((8,) + b.shape, b.dtype) for b in blks], 7, name, False))


def _exchange_alone(ex, *, name):
    def body(x_refs, out_refs, send_sems, recv_sems, local_sems):
        start, wait = ex.bind(x_refs, out_refs, send_sems, recv_sems)
        start()
        wait()

    return list(_comm_call(body, ex.xs, ex.out_shape, 3, name, False))


def _block_rows(rows, row_bytes, target=1 << 21, align=BF16_SUBLANES):
    return _tile(rows, max(align, target // row_bytes // align * align), align)


def _sum_blocks(buf, *, name, out_dtype):
    B, R, C = buf.shape
    tm = _block_rows(R, B * C * buf.dtype.itemsize)

    def body(x_ref, o_ref):
        acc = x_ref[0].astype(F32)
        for b in range(1, B):
            acc = acc + x_ref[b].astype(F32)
        o_ref[...] = acc.astype(o_ref.dtype)

    return pl.pallas_call(
        body, name=name, grid=(R // tm,), in_specs=[pl.BlockSpec((B, tm, C), lambda i: (0, i, 0))],
        out_specs=pl.BlockSpec((tm, C), lambda i: (i, 0)), out_shape=_sds((R, C), out_dtype),
        compiler_params=_params(("parallel",)),
    )(buf)


def _sum_chips(received, sent, chip, *, name):
    _, R, C = received.shape
    tm = _block_rows(R, 5 * C * received.dtype.itemsize)

    def body(chip_ref, r0, r1, r2, r3, own_ref, o_ref):
        acc = None
        for q, r in enumerate((r0, r1, r2, r3)):
            term = jnp.where(q == chip_ref[0], own_ref[...], r[...]).astype(F32)
            acc = term if acc is None else acc + term
        o_ref[...] = acc

    def slot(q):
        return pl.BlockSpec((None, tm, C), lambda i, ch, q=q: (jnp.where(q == ch[0], (q + 1) % 4, q), i, 0))

    return pl.pallas_call(
        body, name=name, out_shape=_sds((R, C), F32),
        grid_spec=pltpu.PrefetchScalarGridSpec(
            num_scalar_prefetch=1, grid=(R // tm,),
            in_specs=[slot(0), slot(1), slot(2), slot(3), pl.BlockSpec((None, tm, C), lambda i, ch: (ch[0], i, 0))],
            out_specs=pl.BlockSpec((tm, C), lambda i, ch: (i, 0))),
        compiler_params=_params(("arbitrary",)),
    )(chip, received, received, received, received, sent)


def _pair_add(mine, theirs, core, *, name):
    _, _, R, C = mine.shape
    tm = _block_rows(R, C * 2)

    def body(core_ref, a_ref, b_ref, o_ref):
        o_ref[...] = (a_ref[...].astype(F32) + b_ref[...].astype(F32)).astype(o_ref.dtype)

    return pl.pallas_call(
        body, name=name, out_shape=_sds(theirs.shape, BF16),
        grid_spec=pltpu.PrefetchScalarGridSpec(
            num_scalar_prefetch=1, grid=(4, R // tm),
            in_specs=[pl.BlockSpec((None, None, tm, C), lambda q, i, core_ref: (q, core_ref[0], i, 0)),
                      pl.BlockSpec((None, tm, C), lambda q, i, core_ref: (q, i, 0))],
            out_specs=pl.BlockSpec((None, tm, C), lambda q, i, core_ref: (q, i, 0))),
        compiler_params=_params(("parallel", "parallel")),
    )(core, mine, theirs)


def _assemble(gathered, own, chip, *, name, transpose):
    _, K, Ns = gathered.shape
    tm = _block_rows(K, Ns * 4)

    def body(chip_ref, g_ref, own_ref, o_ref):
        q = pl.program_id(0)

        @pl.when(q == chip_ref[0])
        def _():
            o_ref[...] = own_ref[...].astype(BF16)

        @pl.when(q != chip_ref[0])
        def _():
            o_ref[...] = g_ref[...]

    if transpose:
        out_spec = pl.BlockSpec((tm, Ns), lambda q, i, ch: (i, q))
        out_shape = _sds((K, 4 * Ns), BF16)
    else:
        out_spec = pl.BlockSpec((None, tm, Ns), lambda q, i, ch: (q, i, 0))
        out_shape = _sds((4, K, Ns), BF16)
    return pl.pallas_call(
        body, name=name, out_shape=out_shape,
        grid_spec=pltpu.PrefetchScalarGridSpec(
            num_scalar_prefetch=1, grid=(4, K // tm),
            in_specs=[pl.BlockSpec((None, tm, Ns), lambda q, i, ch: (jnp.where(q == ch[0], (q + 1) % 4, q), i, 0)),
                      pl.BlockSpec((tm, Ns), lambda q, i, ch: (jnp.where(q == ch[0], i, 0), 0))],
            out_specs=out_spec),
        compiler_params=_params(("arbitrary", "arbitrary")),
    )(chip, gathered, own)


def _assemble_halves(mine, theirs, own, place, *, name, transpose):
    _, K2, Ns = mine.shape
    tm = _block_rows(K2, Ns * 4, target=1 << 22)
    nb = K2 // tm

    def body(place_ref, m_ref, t_ref, own_ref, o_ref):
        q, hb = pl.program_id(0), pl.program_id(1)
        is_own = q == place_ref[0]
        is_mine = hb == place_ref[1]

        @pl.when(is_own)
        def _():
            o_ref[...] = own_ref[...].astype(BF16)

        @pl.when(jnp.logical_not(is_own) & is_mine)
        def _():
            o_ref[...] = m_ref[...]

        @pl.when(jnp.logical_not(is_own) & jnp.logical_not(is_mine))
        def _():
            o_ref[...] = t_ref[...]

    def other(q, pr):
        return jnp.where(q == pr[0], (q + 1) % 4, q)

    if transpose:
        out_spec = pl.BlockSpec((tm, Ns), lambda q, hb, i, pr: (hb * nb + i, q))
        out_shape = _sds((2 * K2, 4 * Ns), BF16)
    else:
        out_spec = pl.BlockSpec((None, tm, Ns), lambda q, hb, i, pr: (q, hb * nb + i, 0))
        out_shape = _sds((4, 2 * K2, Ns), BF16)
    return pl.pallas_call(
        body, name=name, out_shape=out_shape,
        grid_spec=pltpu.PrefetchScalarGridSpec(
            num_scalar_prefetch=1, grid=(4, 2, nb),
            in_specs=[pl.BlockSpec((None, tm, Ns), lambda q, hb, i, pr: (other(q, pr), jnp.where(hb == pr[1], i, 0), 0)),
                      pl.BlockSpec((None, tm, Ns), lambda q, hb, i, pr: (other(q, pr), jnp.where(hb == pr[1], 0, i), 0)),
                      pl.BlockSpec((tm, Ns), lambda q, hb, i, pr: (jnp.where(q == pr[0], hb * nb + i, 0), 0))],
            out_specs=out_spec),
        compiler_params=_params(("arbitrary",) * 3),
    )(place, mine, theirs, own)


def _split_lanes(row, widths):
    out, off = [], 0
    for wd in widths:
        out.append(row[:, off:off + wd])
        off += wd
    return out


def _adamw_math(w, g, m, v):
    m = ADAM_B1 * m + (1.0 - ADAM_B1) * g
    v = ADAM_B2 * v + (1.0 - ADAM_B2) * (g * g)
    m_hat = m / (1.0 - ADAM_B1 ** ADAM_STEP)
    v_hat = v / (1.0 - ADAM_B2 ** ADAM_STEP)
    delta = -ADAM_LR * (m_hat / (jnp.sqrt(v_hat) + ADAM_EPS) + ADAM_WD * w)
    return delta, m, v


def _adamw_halves(w, mine, other, m, v, core, *, name):
    K, Ns = w.shape
    tm = _block_rows(K // 2, Ns * 4, target=1 << 20, align=F32_SUBLANES)
    nb = (K // 2) // tm

    def body(core_ref, w_ref, mine_ref, other_ref, m_ref, v_ref, g_out, d_out, m_out, v_out):
        g = jnp.where(pl.program_id(0) // nb == core_ref[0], mine_ref[...], other_ref[...])
        g_out[...] = g
        d_out[...], m_out[...], v_out[...] = _adamw_math(w_ref[...], g, m_ref[...], v_ref[...])

    row = pl.BlockSpec((tm, Ns), lambda i, cr: (i, 0))
    return pl.pallas_call(
        body, name=name, out_shape=[_sds((K, Ns), F32)] * 4,
        grid_spec=pltpu.PrefetchScalarGridSpec(
            num_scalar_prefetch=1, grid=(K // tm,),
            in_specs=[row,
                      pl.BlockSpec((tm, Ns), lambda i, cr: (jnp.where(i // nb == cr[0], i % nb, 0), 0)),
                      pl.BlockSpec((None, tm, Ns), lambda i, cr: (1 - cr[0], jnp.where(i // nb == cr[0], 0, i % nb), 0)),
                      row, row],
            out_specs=[row, row, row, row]),
        compiler_params=_params(("arbitrary",)),
    )(core, w, mine, other, m, v)


def _adamw(w, g, m, v, *, name, carry=None):
    C = w.shape[1]

    def fn(w, g, m, v):
        return _adamw_math(w, g, m, v), ()

    tm = max(F32_SUBLANES, min(512, (1 << 20) // (4 * C) // F32_SUBLANES * F32_SUBLANES))
    res = _rowwise(fn, [w, g, m, v], [], [(C, F32)] * 3, name=name, tm=tm, carry=carry)
    return tuple(res[0]) + ((res[2],) if carry is not None else ())


def _rope_tables(n):
    rows = n // GRID_W
    row = jnp.repeat(jnp.arange(rows, dtype=F32), GRID_W)
    col = jnp.tile(jnp.arange(GRID_W, dtype=F32), rows)
    nf = ROPE_DIM // 4
    freqs = ROPE_THETA ** (-jnp.arange(nf, dtype=F32) / nf)
    ang_r, ang_c = row[:, None] * freqs[None, :], col[:, None] * freqs[None, :]
    cr, sr, cc, sc = jnp.cos(ang_r), jnp.sin(ang_r), jnp.cos(ang_c), jnp.sin(ang_c)
    nope = HEAD_PAD - 2 * ROPE_DIM
    one, zero, z = jnp.ones((n, nope), F32), jnp.zeros((n, nope), F32), jnp.zeros((n, nf), F32)
    pad = jnp.zeros((n, ROPE_DIM), F32)
    cos = jnp.concatenate([one, cr, cr, cc, cc, pad], axis=1)
    s_lo = jnp.concatenate([zero, -sr, z, -sc, z, pad], axis=1)
    s_hi = jnp.concatenate([zero, z, sr, z, sc, pad], axis=1)
    return cos, s_lo, s_hi


def _rope(n, cos, s_lo, s_hi):
    q = ROPE_DIM // 4
    return n * cos + pltpu.roll(n, HEAD_PAD - q, 1) * s_lo + pltpu.roll(n, q, 1) * s_hi


def _rope_t(d, cos, s_lo, s_hi):
    q = ROPE_DIM // 4
    return d * cos + pltpu.roll(d * s_lo, q, 1) + pltpu.roll(d * s_hi, HEAD_PAD - q, 1)


def kernel(x, c, ctx, c_ctx, w_mod, b_mod, norm1_g, w_in, q_norm_g, kv_norm_g, w_uq, w_ukv, qk_norm_q, qk_norm_k, sgu_norm_g, sgu_norm_b, w_spatial, b_spatial, w_br_attn, w_br_sgu, w_out, norm2_g, w_ffn_in, w_ffn_out, loss_target, m_c_ctx, m_w_mod, m_b_mod, m_norm1_g, m_w_in, m_q_norm_g, m_kv_norm_g, m_w_uq, m_w_ukv, m_qk_norm_q, m_qk_norm_k, m_sgu_norm_g, m_sgu_norm_b, m_w_spatial, m_b_spatial, m_w_br_attn, m_w_br_sgu, m_w_out, m_norm2_g, m_w_ffn_in, m_w_ffn_out, v_c_ctx, v_w_mod, v_b_mod, v_norm1_g, v_w_in, v_q_norm_g, v_kv_norm_g, v_w_uq, v_w_ukv, v_qk_norm_q, v_qk_norm_k, v_sgu_norm_g, v_sgu_norm_b, v_w_spatial, v_b_spatial, v_w_br_attn, v_w_br_sgu, v_w_out, v_norm2_g, v_w_ffn_in, v_w_ffn_out):
    ax, ay, ac = _place()
    my_chip = 2 * ax + ay
    my_dev = 4 * ax + 2 * ay + ac

    N, D = x.shape[1], x.shape[2]
    CT = ctx.shape[1]
    M = N + CT
    QL, KVL, QK = q_norm_g.shape[-1], kv_norm_g.shape[-1], qk_norm_q.shape[-1]
    NOPE = QK - ROPE_DIM
    VD = NOPE
    H = 4 * w_uq.shape[-1] // QK
    SW, G, CH = sgu_norm_g.shape[-1], w_spatial.shape[1], w_spatial.shape[2]
    GD = SW // G
    DFF = 4 * w_ffn_out.shape[1]
    NMOD = 4 * w_mod.shape[-1]
    NM = w_mod.shape[-1]
    KVP = KVL + 2 * ROPE_DIM
    assert NOPE == LANES and GD == LANES and HEAD_PAD == NOPE + 2 * ROPE_DIM and CH == LANES
    scale = QK ** -0.5

    x2, ctx2, tgt2 = x[0], ctx[0], loss_target[0]

    c_all = _all_gather8([c], name="ag_c", in_vmem=True)[0][:, 0, :]
    c_rows = jnp.concatenate([c_all, c_ctx[None, :], jnp.zeros((BF16_SUBLANES - 9, D), F32)], axis=0)

    def silu_fn(t):
        s = _sigmoid(t)
        return (t * s, s * (1.0 + t * (1.0 - s))), ()

    (silu_c, dsilu_c), _ = _rowwise(silu_fn, [c_rows], [], [(D, F32), (D, F32)], name="silu_c", tm=16)
    wm = w_mod[0]
    mod_loc = _mm([(silu_c, wm)], name="mod_fwd", outs=(F32,),
                  extras=[(lax.dynamic_slice_in_dim(b_mod, my_chip * NM, NM, axis=1), "n")],
                  epi=lambda acc, b: (acc + b,))
    mod_all = _all_gather8([mod_loc], name="ag_mod", in_vmem=True)[0]
    mod_full = jnp.concatenate([mod_all[0], mod_all[2], mod_all[4], mod_all[6]], axis=1)
    mod_me = lax.dynamic_slice_in_dim(mod_full, my_dev, 1, axis=0)
    sh1, sc1, g1, sh2, sc2, g2 = [mod_me[:, i * D:(i + 1) * D] for i in range(6)]
    sh1c, sc1c = mod_full[8:9, :D], mod_full[8:9, D:2 * D]

    big = [w_in[0], w_uq[0], w_ukv[0], w_br_attn[0], w_br_sgu[0], w_out[0], w_ffn_in[0], w_ffn_out[0]]
    col_sharded = [True, True, True, True, True, False, True, False]
    halves = [lax.dynamic_slice_in_dim(a, ac * (a.shape[0] // 2), a.shape[0] // 2, axis=0).astype(BF16) for a in big]
    tags = ["w_in", "w_uq", "w_ukv", "w_br_attn", "w_br_sgu", "w_out", "w_ffn_in", "w_ffn_out"]
    first_group, attn_group, ffn_group = [0, 1, 2], [3, 4, 5, 6], [7]
    chip1 = jnp.reshape(my_chip, (1,)).astype(jnp.int32)
    place2 = jnp.stack([my_chip, ac]).astype(jnp.int32)

    def laid_out(seg, i):
        a = big[i]
        if col_sharded[i] and seg.ndim == 3:
            return seg.transpose(1, 0, 2).reshape(a.shape[0], 4 * a.shape[1])
        return seg if col_sharded[i] else seg.reshape(4 * a.shape[0], a.shape[1])

    def side_by_side(i):
        return col_sharded[i] and big[i].shape[1] % LANES == 0

    def finish_gather(idx, mine4, theirs4):
        return [laid_out(_assemble_halves(m, t, big[i], place2, name="assemble_" + tags[i], transpose=side_by_side(i)), i)
                for i, m, t in zip(idx, mine4, theirs4)]

    gathered = _gather_others([halves[i] for i in first_group], name="ag_weights")
    w_in_f, w_uq_f, w_ukv_f = [
        laid_out(_assemble(seg.reshape((4,) + big[i].shape), big[i], chip1, name="assemble_" + tags[i],
                           transpose=side_by_side(i)), i) for i, seg in zip(first_group, gathered)]
    o_kv, o_u = QL, QL + KVL + ROPE_DIM
    o_v, o_g = o_u + SW, o_u + 2 * SW
    w_q = w_in_f[:, :QL]
    w_kv = jnp.pad(w_in_f[:, o_kv:o_u], ((0, 0), (0, ROPE_DIM)))
    w_u, w_v = w_in_f[:, o_u:o_v], w_in_f[:, o_v:o_g]
    w_g1, w_g2 = w_in_f[:, o_g:o_g + D], w_in_f[:, o_g + D:]
    w_uq_p = jnp.pad(w_uq_f.reshape(QL, H, QK), ((0, 0), (0, 0), (0, HEAD_PAD - QK))).reshape(QL, H * HEAD_PAD)

    cos_t, slo_t, shi_t = _rope_tables(N)
    ones_c = jnp.concatenate([jnp.ones((CT, NOPE + ROPE_DIM), F32), jnp.zeros((CT, ROPE_DIM), F32)], axis=1)
    cos_k = jnp.concatenate([cos_t, ones_c], axis=0)
    slo_k = jnp.concatenate([slo_t, jnp.zeros((CT, HEAD_PAD), F32)], axis=0)
    shi_k = jnp.concatenate([shi_t, jnp.zeros((CT, HEAD_PAD), F32)], axis=0)
    gq_p = jnp.pad(qk_norm_q, ((0, 0), (0, HEAD_PAD - QK)))
    gk_p = jnp.pad(qk_norm_k, ((0, 0), (0, HEAD_PAD - QK)))

    def norm_mod_fn(t, g, sh, sc):
        r = _rms_stats(t, D)
        return (((t * r) * g) * (1.0 + sc) + sh,), ()

    (h,), _ = _rowwise(norm_mod_fn, [x2], [norm1_g, sh1, sc1], [(D, BF16)], name="norm1_x")
    (ctx_h,), _ = _rowwise(norm_mod_fn, [ctx2], [norm1_g, sh1c, sc1c], [(D, BF16)], name="norm1_ctx")

    def q_norm_epi(acc, g):
        return acc, (acc * _rms_stats(acc, QL)) * g

    qc, qn = _mm([(h, w_q)], name="proj_q", outs=(F32, BF16), tn=QL, extras=[(q_norm_g, "n")], epi=q_norm_epi)
    kvin = jnp.concatenate([_mm([(h, w_kv)], name="proj_kv", outs=(F32,)),
                            _mm([(ctx_h, w_kv)], name="proj_kv_ctx", outs=(F32,))], axis=0)
    def both(a, b):
        return a, b

    u_in, v_in = _mm([(h, w_u, w_v)], name="proj_uv", outs=(BF16, BF16), epi=both)
    (g1_in, g2_in), (mine_bra, mine_brs) = _mm([(h, w_g1, w_g2)], name="proj_gates", outs=(BF16, BF16), epi=both,
                                               carry=_ChipExchange([halves[3], halves[4]], gather=True))

    def kv_norm_fn(t, g):
        kvc = t[:, :KVL]
        return (((kvc * _rms_stats(kvc, KVL)) * g),), ()

    (kvn,), _ = _rowwise(kv_norm_fn, [kvin], [kv_norm_g], [(KVL, BF16)], name="kv_norm")
    q_raw = _mm([(qn, w_uq_p)], name="q_up", outs=(F32,))
    kv_rows = _tile(M, 2304)
    kv_raw = _mm([(kvn, w_ukv_f)], name="kv_up", outs=(F32,), tm=kv_rows)

    def q_post_fn(t, cos, slo, shi, g):
        outs = []
        for hd in range(H):
            th = t[:, hd * HEAD_PAD:(hd + 1) * HEAD_PAD]
            outs.append(_rope((th * _rms_stats(th, QK)) * g, cos, slo, shi) * scale)
        return (jnp.concatenate(outs, axis=1),), ()

    (q_att,), _ = _rowwise(q_post_fn, [q_raw, cos_t, slo_t, shi_t], [gq_p], [(H * HEAD_PAD, BF16)], name="q_post")

    def k_post_fn(t, kvi, cos, slo, shi, g):
        kr = kvi[:, KVL:]
        ks, vs = [], []
        for hd in range(H):
            th = jnp.concatenate([t[:, hd * HEAD_PAD:hd * HEAD_PAD + NOPE], kr], axis=1)
            ks.append(_rope((th * _rms_stats(th, QK)) * g, cos, slo, shi))
            vs.append(t[:, hd * HEAD_PAD + NOPE:(hd + 1) * HEAD_PAD])
        return (jnp.concatenate(ks, axis=1), jnp.concatenate(vs, axis=1)), ()

    (k_att, v_att), _, (mine_out,) = _rowwise(k_post_fn, [kv_raw, kvin, cos_k, slo_k, shi_k], [gk_p],
                                              [(H * HEAD_PAD, BF16), (H * VD, BF16)], name="k_post",
                                              carry=_ChipExchange([halves[5]], gather=True))
    attn_o, lse, (mine_ffi,) = _attn_fwd(q_att, k_att, v_att, heads=H, carry=_ChipExchange([halves[6]], gather=True))
    mine4 = [mine_bra, mine_brs, mine_out, mine_ffi]

    ws3 = w_spatial[0]
    bs_t = jnp.pad(b_spatial[0].T, ((0, 0), (0, LANES - G)))

    def sgu_parts(u_in, v_in, ng, nb):
        u, v = _gelu(u_in.astype(F32)), _gelu(v_in.astype(F32))
        mu = jnp.mean(v, axis=-1, keepdims=True)
        vc = v - mu
        rs = lax.rsqrt(jnp.mean(vc * vc, axis=-1, keepdims=True) + EPS)
        xhat = vc * rs
        return u, xhat, rs, (xhat * ng + nb).astype(BF16)

    def sgu_fwd_fn(u_in, v_in, ng, nb, ws, bst):
        u, _, _, vnb = sgu_parts(u_in, v_in, ng, nb)
        outs = []
        for g in range(G):
            sl = slice(g * GD, (g + 1) * GD)
            mixed = jnp.dot(ws[g].astype(BF16), vnb[:, sl], preferred_element_type=F32) + bst[:, g:g + 1]
            outs.append(u[:, sl] * mixed)
        return (jnp.concatenate(outs, axis=1),), ()

    (sgu_o,), _, theirs4 = _rowwise(sgu_fwd_fn, [u_in, v_in], [sgu_norm_g, sgu_norm_b, ws3, bs_t], [(SW, BF16)],
                                    name="sgu_fwd", tm=CH, carry=_PairExchange(mine4, "forward"))
    w_bra, w_brs, w_out_f, w_ffi = finish_gather(attn_group, mine4, theirs4)
    w_fa, w_fb = w_ffi[:, :DFF], w_ffi[:, DFF:]

    a1 = _mm([(attn_o, w_bra)], name="br_attn", outs=(BF16,))
    def merge_epi(acc, a1v, gi1, gi2):
        return acc, _sigmoid(gi1.astype(F32)) * a1v.astype(F32) + _sigmoid(gi2.astype(F32)) * acc

    a2, merged = _mm([(sgu_o, w_brs)], name="br_sgu", outs=(BF16, BF16),
                     extras=[(a1, "mn"), (g1_in, "mn"), (g2_in, "mn")], epi=merge_epi)

    def res_gate(acc, res, gate):
        return res + gate * acc, acc

    x1, mo = _mm([(merged, w_out_f)], name="out_proj", outs=(F32, BF16), tn=1024,
                 extras=[(x2, "mn"), (g1, "n")], epi=res_gate)
    (h2,), _ = _rowwise(norm_mod_fn, [x1], [norm2_g, sh2, sc2], [(D, BF16)], name="norm2")

    def swiglu_epi(a, b):
        return a, b, (a * _sigmoid(a)) * b

    (fa, fb, act), mine4 = _mm([(h2, w_fa, w_fb)], name="ffn_in", outs=(BF16, BF16, BF16), tn=512, epi=swiglu_epi,
                               carry=_ChipExchange([halves[i] for i in ffn_group], gather=True))
    (w_ffo,) = finish_gather(ffn_group, mine4, _exchange_alone(_PairExchange(mine4, "forward"), name="ag_forward_ffn"))
    def loss_epi(acc, res, t, gate):
        e = (res + gate * acc) - t
        dy = e * (1.0 / D)
        return dy, gate * dy, _colsum(e * e) * (0.5 / D), _colsum(dy * acc)

    dy, df, loss_part, dg2_part = _mm([(act, w_ffo)], name="ffn_out", outs=(F32, BF16), tn=1024, col_sums=2,
                                      extras=[(x1, "mn"), (tgt2, "mn"), (g2, "n")], epi=loss_epi)

    def fold_fn(a, b):
        return (), (_colsum(a), _colsum(b))

    _, (loss_cols, dg2) = _rowwise(fold_fn, [loss_part[:, 0, :], dg2_part[:, 0, :]], [], [], [(1, D), (1, D)],
                                   name="loss_fold", tm=loss_part.shape[0])

    def swiglu_bwd_epi(dact, a, b):
        a, b = a.astype(F32), b.astype(F32)
        s = _sigmoid(a)
        return dact * b * (s * (1.0 + a * (1.0 - s))), dact * (a * s)

    da, db = _mm([(df, w_ffo)], tb=True, name="ffn_out_dx", outs=(BF16, BF16), tn=512,
                 extras=[(fa, "mn"), (fb, "mn")], epi=swiglu_bwd_epi)
    dw_ffo = _mm([(act, df)], ta=True, name="ffn_out_dw", outs=(BF16,), tm=1408)
    dh2 = _mm([(da, w_fa), (db, w_fb)], tb=True, name="ffn_in_dx", outs=(F32,))
    ns_ffi = w_ffn_in.shape[-1]
    dw_ffi = _mm([(h2, da)], ta=True, name="ffn_in_dw_a", outs=(BF16,), tn=1408, split=ns_ffi,
                 into=(lax.empty((4, D, ns_ffi), BF16), 0))
    dw_ffi = _mm([(h2, db)], ta=True, name="ffn_in_dw_b", outs=(BF16,), tn=1408, split=ns_ffi, into=(dw_ffi, 2))

    def norm2_bwd_fn(dh, t, dyv, mov, g, sc, g1v):
        r = _rms_stats(t, D)
        tn = t * r
        dxg = dh * (1.0 + sc)
        dt = dyv + _rms_bwd(dxg * g, tn, r, D)
        return (dt, g1v * dt), (_colsum(dh), _colsum(dh * (tn * g)), _colsum(dxg * tn), _colsum(dt * mov.astype(F32)))

    (dx1, dmo), (dsh2, dsc2, dn2g, dg1) = _rowwise(
        norm2_bwd_fn, [dh2, x1, dy, mo], [norm2_g, sc2, g1], [(D, F32), (D, BF16)], [(1, D)] * 4, name="norm2_bwd")

    def merge_bwd_epi(dm, a1, a2, gi1, gi2):
        s1, s2 = _sigmoid(gi1.astype(F32)), _sigmoid(gi2.astype(F32))
        a1, a2 = a1.astype(F32), a2.astype(F32)
        return dm * s1, dm * s2, dm * a1 * (s1 * (1.0 - s1)), dm * a2 * (s2 * (1.0 - s2))

    da1, da2, dgi1, dgi2 = _mm([(dmo, w_out_f)], tb=True, name="out_proj_dx", outs=(BF16,) * 4, tn=512,
                               extras=[(a1, "mn"), (a2, "mn"), (g1_in, "mn"), (g2_in, "mn")], epi=merge_bwd_epi)
    dw_out = _mm([(merged, dmo)], ta=True, name="out_proj_dw", outs=(BF16,))
    dattn = _mm([(da1, w_bra)], tb=True, name="br_attn_dx", outs=(BF16,))
    dw_bra = _mm([(attn_o, da1)], ta=True, name="br_attn_dw", outs=(BF16,), split=w_br_attn.shape[-1])
    dsgu = _mm([(da2, w_brs)], tb=True, name="br_sgu_dx", outs=(BF16,))
    dw_brs = _mm([(sgu_o, da2)], ta=True, name="br_sgu_dw", outs=(BF16,), split=w_br_sgu.shape[-1])

    def sgu_bwd_fn(dso, u_in, v_in, ng, nb, ws, bst):
        u, xhat, rs, vnb = sgu_parts(u_in, v_in, ng, nb)
        dso = dso.astype(F32)
        lane = lax.broadcasted_iota(jnp.int32, (CH, LANES), 1)
        du, dvn, dws, dbs = [], [], [], jnp.zeros((CH, LANES), F32)
        for g in range(G):
            sl = slice(g * GD, (g + 1) * GD)
            wg = ws[g].astype(BF16)
            mixed = jnp.dot(wg, vnb[:, sl], preferred_element_type=F32) + bst[:, g:g + 1]
            du.append(dso[:, sl] * mixed)
            dmix = dso[:, sl] * u[:, sl]
            dmb = dmix.astype(BF16)
            dws.append(lax.dot_general(dmb, vnb[:, sl], (((1,), (1,)), ((), ())), preferred_element_type=F32))
            dbs = dbs + jnp.where(lane == g, jnp.sum(dmix, axis=1, keepdims=True), 0.0)
            dvn.append(lax.dot_general(wg, dmb, (((0,), (0,)), ((), ())), preferred_element_type=F32))
        du, dvn = jnp.concatenate(du, axis=1), jnp.concatenate(dvn, axis=1)
        dxh = dvn * ng
        dv = rs * (dxh - jnp.mean(dxh, axis=-1, keepdims=True) - xhat * jnp.mean(dxh * xhat, axis=-1, keepdims=True))
        return ((du * _gelu_grad(u_in.astype(F32)), dv * _gelu_grad(v_in.astype(F32))),
                (_colsum(dvn * xhat), _colsum(dvn), jnp.stack(dws), dbs))

    core = jnp.reshape(ac, (1,)).astype(jnp.int32)

    def dest_layout(dwf, i):
        K, Ns = big[i].shape
        if dwf.ndim == 2:
            dwf = dwf.reshape(K, 4, Ns).transpose(1, 0, 2) if col_sharded[i] else dwf.reshape(4, K, Ns)
        return dwf.reshape(4, 2, K // 2, Ns)

    def pair_sums(idx, g4, sib):
        return [_pair_add(g, s, core, name="rs_pair_add_" + tags[i]) for g, s, i in zip(g4, sib, idx)]

    early = [3, 4, 5, 6, 7]
    g4_early = [dest_layout(d, i) for d, i in zip([dw_bra, dw_brs, dw_out, dw_ffi, dw_ffo], early)]
    (du_in, dv_in), (d_sng, d_snb, d_ws, d_bs), sib_early = _rowwise(
        sgu_bwd_fn, [dsgu, u_in, v_in], [sgu_norm_g, sgu_norm_b, ws3, bs_t], [(SW, BF16), (SW, BF16)],
        [(1, SW), (1, SW), (G, CH, CH), (CH, LANES)], name="sgu_bwd", tm=CH, carry=_PairExchange(g4_early, "halves"))
    pair_early = pair_sums(early, g4_early, sib_early)
    dq_att, dk_att, dv_att, xchg_early = _attn_bwd(q_att, k_att, v_att, attn_o, lse, dattn, heads=H,
                                                   carry=_ChipExchange(pair_early, gather=False))

    def q_post_bwd_fn(dq, t, cos, slo, shi, g):
        outs, dg = [], jnp.zeros((1, HEAD_PAD), F32)
        for hd in range(H):
            sl = slice(hd * HEAD_PAD, (hd + 1) * HEAD_PAD)
            th = t[:, sl]
            r = _rms_stats(th, QK)
            tn = th * r
            dn = _rope_t(dq[:, sl] * scale, cos, slo, shi)
            dg = dg + _colsum(dn * tn)
            outs.append(_rms_bwd(dn * g, tn, r, QK))
        return (jnp.concatenate(outs, axis=1),), (dg,)

    (dq_raw,), (d_gq,) = _rowwise(q_post_bwd_fn, [dq_att, q_raw, cos_t, slo_t, shi_t], [gq_p],
                                  [(H * HEAD_PAD, BF16)], [(1, HEAD_PAD)], name="q_post_bwd")

    def k_post_bwd_fn(dk, dv, t, kvi, cos, slo, shi, g):
        kr = kvi[:, KVL:]
        outs, dg, dkr = [], jnp.zeros((1, HEAD_PAD), F32), jnp.zeros_like(kr)
        for hd in range(H):
            th = jnp.concatenate([t[:, hd * HEAD_PAD:hd * HEAD_PAD + NOPE], kr], axis=1)
            r = _rms_stats(th, QK)
            tn = th * r
            dn = _rope_t(dk[:, hd * HEAD_PAD:(hd + 1) * HEAD_PAD], cos, slo, shi)
            dg = dg + _colsum(dn * tn)
            dt = _rms_bwd(dn * g, tn, r, QK)
            dkr = dkr + dt[:, NOPE:]
            outs += [dt[:, :NOPE], dv[:, hd * VD:(hd + 1) * VD]]
        return (jnp.concatenate(outs, axis=1), dkr), (dg,)

    def reduced_halves(idx, xchg, pair):
        return [_sum_chips(t4, pr, chip1, name="rs_sum_" + tags[i]) for t4, pr, i in zip(xchg, pair, idx)]

    red_early = reduced_halves(early, xchg_early, pair_early)
    (dkv_raw, dkrope), (d_gk,), other_early = _rowwise(
        k_post_bwd_fn, [dk_att, dv_att, kv_raw, kvin, cos_k, slo_k, shi_k], [gk_p],
        [(H * HEAD_PAD, BF16), (2 * ROPE_DIM, F32)], [(1, HEAD_PAD)], name="k_post_bwd",
        carry=_PairExchange(red_early, "gather"))

    def q_norm_bwd_epi(dn, t, g):
        r = _rms_stats(t, QL)
        tn = t * r
        return _rms_bwd(dn * g, tn, r, QL), _colsum(dn * tn)

    dqc, d_qng_part = _mm([(dq_raw, w_uq_p)], tb=True, name="q_up_dx", outs=(BF16,), tn=QL, col_sums=1,
                          extras=[(qc, "mn"), (q_norm_g, "n")], epi=q_norm_bwd_epi)
    _, (d_qng,) = _rowwise(lambda a: ((), (_colsum(a),)), [d_qng_part[:, 0, :]], [], [], [(1, QL)],
                           name="q_norm_fold", tm=d_qng_part.shape[0])
    dw_uq_p = _mm([(qn, dq_raw)], ta=True, name="q_up_dw", outs=(BF16,))
    dkvn = _mm([(dkv_raw, w_ukv_f)], tb=True, name="kv_up_dx", outs=(F32,), tm=kv_rows)
    dw_ukv = _mm([(kvn, dkv_raw)], ta=True, name="kv_up_dw", outs=(BF16,), tk=kv_rows)

    def kv_norm_bwd_fn(dn, dkr, t, g):
        kvc = t[:, :KVL]
        r = _rms_stats(kvc, KVL)
        tn = kvc * r
        return (jnp.concatenate([_rms_bwd(dn * g, tn, r, KVL), dkr], axis=1),), (_colsum(dn * tn),)

    (dkvin,), (d_kvng,) = _rowwise(kv_norm_bwd_fn, [dkvn, dkrope, kvin], [kv_norm_g], [(KVP, BF16)], [(1, KVL)],
                                   name="kv_norm_bwd")
    dkvin_x, dkvin_c = dkvin[:N], dkvin[N:]

    dctx_h = _mm([(dkvin_c, w_kv)], tb=True, name="proj_kv_ctx_dx", outs=(F32,))
    dw_q = _mm([(h, dqc)], ta=True, name="proj_q_dw", outs=(BF16,))
    dw_kv = _mm([(h, dkvin_x), (ctx_h, dkvin_c)], ta=True, name="proj_kv_dw", outs=(BF16,))
    dw_u = _mm([(h, du_in)], ta=True, name="proj_u_dw", outs=(BF16,))
    dw_v = _mm([(h, dv_in)], ta=True, name="proj_v_dw", outs=(BF16,))
    dw_g1 = _mm([(h, dgi1)], ta=True, name="proj_g1_dw", outs=(BF16,))
    dw_g2 = _mm([(h, dgi2)], ta=True, name="proj_g2_dw", outs=(BF16,))

    dw_in_f = jnp.concatenate([dw_q, dw_kv[:, :KVL + ROPE_DIM], dw_u, dw_v, dw_g1, dw_g2], axis=1)
    dw_uq_f = dw_uq_p.reshape(QL, H, HEAD_PAD)[:, :, :QK].reshape(QL, H * QK)
    late = [0, 1, 2]
    g4_late = [dest_layout(d, i) for d, i in zip([dw_in_f, dw_uq_f, dw_ukv], late)]
    pair_late = pair_sums(late, g4_late, _exchange_alone(_PairExchange(g4_late, "halves"), name="rs_pair_late"))
    dh, xchg_late = _mm([(dqc, w_q), (dkvin_x, w_kv), (du_in, w_u), (dv_in, w_v), (dgi1, w_g1), (dgi2, w_g2)],
                        tb=True, name="proj_dx", outs=(F32,), tn=1024, tk=512,
                        carry=_ChipExchange(pair_late, gather=False))

    def norm1_bwd_fn(dhv, t, dres, g, sc):
        r = _rms_stats(t, D)
        tn = t * r
        dxg = dhv * (1.0 + sc)
        return (dres + _rms_bwd(dxg * g, tn, r, D),), (_colsum(dhv), _colsum(dhv * (tn * g)), _colsum(dxg * tn))

    (grad_x,), (dsh1, dsc1, dn1g_x) = _rowwise(norm1_bwd_fn, [dh, x2, dx1], [norm1_g, sc1], [(D, F32)], [(1, D)] * 3,
                                               name="norm1_bwd")
    _, (dsh1c, dsc1c, dn1g_c) = _rowwise(norm1_bwd_fn, [dctx_h, ctx2, jnp.zeros_like(ctx2)], [norm1_g, sc1c],
                                         [(D, F32)], [(1, D)] * 3, name="norm1_ctx_bwd")

    small = [dsh1, dsc1, dg1, dsh2, dsc2, dg2,
             dsh1c, dsc1c, dn1g_x, dn1g_c, d_qng, d_kvng, d_gq, d_gk, d_sng, d_snb, dn2g, loss_cols]
    small_sizes = [a.shape[1] for a in small]
    sm_row = jnp.concatenate(small, axis=1)
    sm_mat = jnp.concatenate([d_ws.reshape(G * CH, CH), d_bs], axis=0)
    row_all, mat_all = _all_gather8([sm_row, sm_mat], name="ag_small", in_vmem=True)
    row_sum = _sum_blocks(row_all, name="sum_small_rows", out_dtype=F32)
    mat_sum = _sum_blocks(mat_all, name="sum_small_mats", out_dtype=F32)
    dmod_rows = row_all[:, 0, :NMOD]
    (_, _, _, _, _, _, t_sh1c, t_sc1c, t_n1x, t_n1c, g_qng, g_kvng, t_gq, t_gk, g_sng, g_snb, g_n2g,
     t_loss) = _split_lanes(row_sum, small_sizes)
    g_ws, t_bs = mat_sum[:G * CH], mat_sum[G * CH:]
    dmodc_row = jnp.concatenate([t_sh1c, t_sc1c, jnp.zeros((1, NMOD - 2 * D), F32)], axis=1)
    dmod16 = jnp.concatenate([dmod_rows, dmodc_row, jnp.zeros((BF16_SUBLANES - 9, NMOD), F32)], axis=0)

    def small_fn(rows, n1x, n1c, lossv):
        return (), (_colsum(rows), n1x + n1c, jnp.sum(lossv, axis=1, keepdims=True))

    _, (g_bmod, g_n1g, loss11) = _rowwise(small_fn, [dmod16], [t_n1x, t_n1c, t_loss], [], [(1, NMOD), (1, D), (1, 1)],
                                          name="small_reduce", tm=16)
    dmod_loc = lax.dynamic_slice_in_dim(dmod16, my_chip * NM, NM, axis=1)
    g_wmod = _mm([(silu_c, dmod_loc)], ta=True, name="mod_dw", outs=(F32,))
    dsilu_part = _mm([(dmod_loc, wm)], tb=True, name="mod_dx", outs=(F32,))
    part_all = _all_gather8([dsilu_part[8:9]], name="ag_cctx", in_vmem=True)[0]

    def cctx_fn(parts, dsl):
        return (), ((parts[0:1] + parts[2:3] + parts[4:5] + parts[6:7]) * dsl,)

    _, (g_cctx,) = _rowwise(cctx_fn, [part_all[:, 0, :]], [dsilu_c[8:9]], [], [(1, D)], name="cctx_grad", tm=8)

    red_late = reduced_halves(late, xchg_late, pair_late)
    other_late = _exchange_alone(_PairExchange(red_late, "gather"), name="rs_halves_late")
    grad_halves = dict(zip(tags, zip(red_late + red_early, other_late + other_early)))

    mod_upd = _adamw(w_mod[0], g_wmod, m_w_mod[0], v_w_mod[0], name="adamw_w_mod")
    grads = dict(
        c_ctx=g_cctx.reshape(D), w_mod=g_wmod[None], b_mod=g_bmod, norm1_g=g_n1g,
        q_norm_g=g_qng, kv_norm_g=g_kvng, qk_norm_q=t_gq[:, :QK], qk_norm_k=t_gk[:, :QK], sgu_norm_g=g_sng,
        sgu_norm_b=g_snb, w_spatial=g_ws.reshape(w_spatial.shape), b_spatial=t_bs[:, :G].T[None], norm2_g=g_n2g)
    weights = dict(c_ctx=c_ctx, w_mod=w_mod, b_mod=b_mod, norm1_g=norm1_g, w_in=w_in, q_norm_g=q_norm_g,
                   kv_norm_g=kv_norm_g, w_uq=w_uq, w_ukv=w_ukv, qk_norm_q=qk_norm_q, qk_norm_k=qk_norm_k,
                   sgu_norm_g=sgu_norm_g, sgu_norm_b=sgu_norm_b, w_spatial=w_spatial, b_spatial=b_spatial,
                   w_br_attn=w_br_attn, w_br_sgu=w_br_sgu, w_out=w_out, norm2_g=norm2_g, w_ffn_in=w_ffn_in,
                   w_ffn_out=w_ffn_out)
    m_in = dict(c_ctx=m_c_ctx, w_mod=m_w_mod, b_mod=m_b_mod, norm1_g=m_norm1_g, w_in=m_w_in, q_norm_g=m_q_norm_g,
                kv_norm_g=m_kv_norm_g, w_uq=m_w_uq, w_ukv=m_w_ukv, qk_norm_q=m_qk_norm_q, qk_norm_k=m_qk_norm_k,
                sgu_norm_g=m_sgu_norm_g, sgu_norm_b=m_sgu_norm_b, w_spatial=m_w_spatial, b_spatial=m_b_spatial,
                w_br_attn=m_w_br_attn, w_br_sgu=m_w_br_sgu, w_out=m_w_out, norm2_g=m_norm2_g, w_ffn_in=m_w_ffn_in,
                w_ffn_out=m_w_ffn_out)
    v_in_ = dict(c_ctx=v_c_ctx, w_mod=v_w_mod, b_mod=v_b_mod, norm1_g=v_norm1_g, w_in=v_w_in, q_norm_g=v_q_norm_g,
                 kv_norm_g=v_kv_norm_g, w_uq=v_w_uq, w_ukv=v_w_ukv, qk_norm_q=v_qk_norm_q, qk_norm_k=v_qk_norm_k,
                 sgu_norm_g=v_sgu_norm_g, sgu_norm_b=v_sgu_norm_b, w_spatial=v_w_spatial, b_spatial=v_b_spatial,
                 w_br_attn=v_w_br_attn, w_br_sgu=v_w_br_sgu, w_out=v_w_out, norm2_g=v_norm2_g, w_ffn_in=v_w_ffn_in,
                 w_ffn_out=v_w_ffn_out)
    names = list(weights)
    big_names = ("w_mod", "w_in", "w_uq", "w_ukv", "w_br_attn", "w_br_sgu", "w_out", "w_ffn_in", "w_ffn_out")
    out_g, out_d, out_m, out_v = {}, {}, {}, {}
    out_g["w_mod"] = grads["w_mod"]
    out_d["w_mod"], out_m["w_mod"], out_v["w_mod"] = [t[None] for t in mod_upd[:3]]
    for nm in big_names[1:]:
        res = _adamw_halves(weights[nm][0], *grad_halves[nm], m_in[nm][0], v_in_[nm][0], core, name="adamw_" + nm)
        out_g[nm], out_d[nm], out_m[nm], out_v[nm] = [t[None] for t in res]
    row_names = [nm for nm in names if nm not in big_names and nm not in ("w_spatial", "b_spatial")]
    widths = [-(-weights[nm].size // LANES) * LANES for nm in row_names]

    def as_row(d):
        return jnp.concatenate([jnp.pad(d[nm].reshape(1, -1), ((0, 0), (0, wd - d[nm].size)))
                                for nm, wd in zip(row_names, widths)], axis=1)

    def as_mat(d):
        return jnp.concatenate([d["w_spatial"].reshape(G * CH, CH), d["b_spatial"].reshape(G, CH)], axis=0)

    row_res = _adamw(as_row(weights), as_row(grads), as_row(m_in), as_row(v_in_), name="adamw_rows")
    mat_res = _adamw(as_mat(weights), as_mat(grads), as_mat(m_in), as_mat(v_in_), name="adamw_spatial")
    for tgt, row, mat in zip((out_d, out_m, out_v), row_res, mat_res):
        for nm, seg in zip(row_names, _split_lanes(row, widths)):
            tgt[nm] = seg[:, :weights[nm].size].reshape(weights[nm].shape)
        tgt["w_spatial"] = mat[:G * CH].reshape(w_spatial.shape)
        tgt["b_spatial"] = mat[G * CH:].reshape(b_spatial.shape)
    for nm in row_names + ["w_spatial", "b_spatial"]:
        out_g[nm] = grads[nm].reshape(weights[nm].shape)

    loss = loss11.reshape(())
    return (loss, grad_x[None], *[out_g[n] for n in names], *[out_d[n] for n in names],
            *[out_m[n] for n in names], *[out_v[n] for n in names])
```

```python
import math

import jax
import jax.numpy as jnp
from jax import lax
from jax.experimental import pallas as pl
from jax.experimental.pallas import tpu as pltpu

F32, BF16 = jnp.float32, jnp.bfloat16
MESH = pl.DeviceIdType.MESH

LANES = 128
F32_SUBLANES = 8
BF16_SUBLANES = 16
MXU_DIM = 256
VMEM_LIMIT_BYTES = 56 * 1024 * 1024

EPS = 1e-6
ROPE_DIM = 64
ROPE_THETA = 10000.0
GRID_W = 64
HEAD_PAD = 256
ADAM_LR, ADAM_B1, ADAM_B2, ADAM_EPS, ADAM_WD, ADAM_STEP = 0.001, 0.9, 0.999, 1e-08, 0.01, 10


def _tile(dim, pref, align=LANES):
    if dim <= pref:
        return dim
    t = (pref // align) * align
    while t >= align:
        if dim % t == 0:
            return t
        t -= align
    return dim


def _params(sem=None):
    return pltpu.CompilerParams(dimension_semantics=sem, vmem_limit_bytes=VMEM_LIMIT_BYTES)


def _sds(shape, dtype):
    return jax.ShapeDtypeStruct(tuple(shape), dtype)


def _mm(pairs, *, name, ta=False, tb=False, outs=(F32,), tm=1024, tn=1024, tk=2048, extras=(), epi=None,
        split=None, into=None, carry=None, col_sums=0):
    dual = len(pairs[0]) == 3
    a0, b0 = pairs[0][0], pairs[0][1]
    M = a0.shape[1] if ta else a0.shape[0]
    N = b0.shape[0] if tb else b0.shape[1]
    tm, tn = _tile(M, tm), _tile(N if split is None else split, tn)
    ks = [(p[0].shape[0] if ta else p[0].shape[1]) for p in pairs]
    tks = [_tile(k, tk) for k in ks]
    nks = [k // t for k, t in zip(ks, tks)]
    offs = [sum(nks[:i]) for i in range(len(pairs))]
    nk_total = sum(nks)
    single = len(pairs) == 1

    def kidx(kk, p):
        return kk if single else jnp.clip(kk - offs[p], 0, nks[p] - 1)

    in_specs, operands = [], []
    for p, pr in enumerate(pairs):
        if ta:
            in_specs.append(pl.BlockSpec((tks[p], tm), lambda i, j, kk, p=p: (kidx(kk, p), i)))
        else:
            in_specs.append(pl.BlockSpec((tm, tks[p]), lambda i, j, kk, p=p: (i, kidx(kk, p))))
        operands.append(pr[0])
        for b in pr[1:]:
            if tb:
                in_specs.append(pl.BlockSpec((tn, tks[p]), lambda i, j, kk, p=p: (j, kidx(kk, p))))
            else:
                in_specs.append(pl.BlockSpec((tks[p], tn), lambda i, j, kk, p=p: (kidx(kk, p), j)))
            operands.append(b)
    for arr, kind in extras:
        if kind == "mn":
            in_specs.append(pl.BlockSpec((tm, tn), lambda i, j, kk: (i, j)))
        else:
            in_specs.append(pl.BlockSpec((1, tn), lambda i, j, kk: (0, j)))
        operands.append(arr)
    n_in = len(operands)
    n_ex = len(extras)
    per = 3 if dual else 2
    dims = (((0 if ta else 1,), (1 if tb else 0,)), ((), ()))

    n_acc = 2 if dual else 1

    def products(ins, p):
        a = ins[per * p][...].astype(BF16)
        return [lax.dot_general(a, ins[per * p + 1 + q][...].astype(BF16), dims, preferred_element_type=F32)
                for q in range(n_acc)]

    def finish(ins, out_refs, acc_vals):
        vals = acc_vals + [r[...] for r in ins[n_in - n_ex:]]
        res = epi(*vals) if epi is not None else (vals[0],)
        for o, r in zip(out_refs, res):
            o[...] = jnp.broadcast_to(r, o.shape).astype(o.dtype)

    out_specs = [pl.BlockSpec((tm, tn), lambda i, j, kk: (i, j)) for _ in outs]
    out_specs += [pl.BlockSpec((None, F32_SUBLANES, tn), lambda i, j, kk: (i, 0, j)) for _ in range(col_sums)]
    out_shape = [_sds((M, N), d) for d in outs] + [_sds((M // tm, F32_SUBLANES, N), F32) for _ in range(col_sums)]
    aliases = {}
    n_alias = 0
    if split is not None:
        nps = split // tn
        lead = 0 if into is None else into[1]
        out_specs = [pl.BlockSpec((None, tm, tn), lambda i, j, kk: (j // nps + lead, i, j % nps))]
        out_shape = [_sds((N // split if into is None else into[0].shape[0], M, split), outs[0])]
        if into is not None:
            in_specs.append(pl.BlockSpec(memory_space=pl.ANY))
            operands.append(into[0])
            aliases, n_alias = {n_in: 0}, 1

    grid = (M // tm, N // tn, nk_total)
    n_out = len(outs) + col_sums

    def at_step(first):
        ids = [pl.program_id(d) for d in range(3)]
        cond = None
        for d, g in zip(ids, grid):
            t = d == (0 if first else g - 1)
            cond = t if cond is None else cond & t
        return cond

    def body(*refs):
        ins, out_refs, accs, start, wait = _split_refs(refs, n_in + n_alias, n_out, carry)
        ins = ins[:n_in]
        if carry is not None:
            pl.when(at_step(True))(start)
        if nk_total == 1:
            finish(ins, out_refs, products(ins, 0))
        else:
            kk = pl.program_id(2)

            @pl.when(kk == 0)
            def _():
                for acc, v in zip(accs, products(ins, 0)):
                    acc[...] = v

            for p in range(len(pairs)):
                lo = max(offs[p], 1)

                @pl.when((kk >= lo) & (kk < offs[p] + nks[p]))
                def _(p=p):
                    for acc, v in zip(accs, products(ins, p)):
                        acc[...] += v

            @pl.when(kk == nk_total - 1)
            def _():
                finish(ins, out_refs, [acc[...] for acc in accs])
        if carry is not None:
            pl.when(at_step(False))(wait)

    ex = carry
    res = pl.pallas_call(
        body, name=name, grid=grid, in_specs=in_specs + ([] if ex is None else ex.in_specs),
        out_specs=out_specs + ([] if ex is None else ex.out_specs),
        out_shape=out_shape + ([] if ex is None else ex.out_shape), input_output_aliases=aliases,
        scratch_shapes=[pltpu.VMEM((tm, tn), F32) for _ in range(n_acc if nk_total > 1 else 0)]
        + ([] if ex is None else ex.scratch),
        compiler_params=_params(("arbitrary",) * 3 if ex is not None else ("parallel", "parallel", "arbitrary")),
    )(*operands, *([] if ex is None else ex.xs))
    if ex is not None:
        return (res[0] if n_out == 1 else res[:n_out]), list(res[n_out:])
    return res[0] if n_out == 1 else res


def _rowwise(fn, rows, vecs, out_rows, out_accs=(), *, name, tm=256, tc=None, carry=None):
    M = rows[0].shape[0]
    tm = _tile(M, tm, BF16_SUBLANES)
    nrow = M // tm
    C = rows[0].shape[1]
    ncol = 1 if tc is None else C // _tile(C, tc)
    tcol = None if tc is None else _tile(C, tc)

    def colwise(shape):
        return tc is not None and len(shape) == 2 and shape[0] == 1 and shape[1] == C

    def vspec(shape):
        if colwise(shape):
            return pl.BlockSpec((1, tcol), lambda j, i: (0, j))
        return pl.BlockSpec(tuple(shape), lambda j, i, n=len(shape): (0,) * n)

    def rspec(width):
        if tc is None:
            return pl.BlockSpec((tm, width), lambda j, i: (i, 0))
        return pl.BlockSpec((tm, tcol), lambda j, i: (i, j))

    in_specs = [rspec(r.shape[1]) for r in rows] + [vspec(v.shape) for v in vecs]
    out_specs = [rspec(c) for c, _ in out_rows] + [vspec(s) for s in out_accs]
    out_shape = [_sds((M, c), d) for c, d in out_rows] + [_sds(s, F32) for s in out_accs]
    n_in, n_or = len(rows) + len(vecs), len(out_rows)

    n_out = n_or + len(out_accs)
    ex = carry

    def body(*refs):
        ins, outs, _, start, wait = _split_refs(refs, n_in, n_out, ex)
        o_rows, o_accs = outs[:n_or], outs[n_or:]
        if ex is not None:
            pl.when((pl.program_id(0) == 0) & (pl.program_id(1) == 0))(start)
        r_out, a_out = fn(*[r[...] for r in ins])
        for o, r in zip(o_rows, r_out):
            o[...] = r.astype(o.dtype)
        i = pl.program_id(1)

        @pl.when(i == 0)
        def _():
            for o, a in zip(o_accs, a_out):
                o[...] = a

        @pl.when(i > 0)
        def _():
            for o, a in zip(o_accs, a_out):
                o[...] += a

        if ex is not None:
            pl.when((pl.program_id(0) == ncol - 1) & (pl.program_id(1) == nrow - 1))(wait)

    res = pl.pallas_call(
        body, name=name, grid=(ncol, nrow), in_specs=in_specs + ([] if ex is None else ex.in_specs),
        out_specs=out_specs + ([] if ex is None else ex.out_specs),
        out_shape=out_shape + ([] if ex is None else ex.out_shape),
        scratch_shapes=[] if ex is None else ex.scratch,
        compiler_params=_params(("arbitrary", "arbitrary") if ex is not None else ("parallel", "arbitrary")),
    )(*rows, *vecs, *([] if ex is None else ex.xs))
    if ex is not None:
        return res[:n_or], res[n_or:n_out], list(res[n_out:])
    return res[:n_or], res[n_or:]


def _colsum(t):
    return jnp.sum(t, axis=0, keepdims=True)


def _gelu(t):
    return 0.5 * t * (1.0 + lax.erf(t * math.sqrt(0.5)))


def _gelu_grad(t):
    return 0.5 * (1.0 + lax.erf(t * math.sqrt(0.5))) + t * jnp.exp(-0.5 * t * t) * (1.0 / math.sqrt(2.0 * math.pi))


def _sigmoid(t):
    return 1.0 / (1.0 + jnp.exp(-t))


def _rms_stats(t, width):
    return lax.rsqrt(jnp.sum(t * t, axis=-1, keepdims=True) * (1.0 / width) + EPS)


def _rms_bwd(dn, tn, r, width):
    return r * (dn - tn * (jnp.sum(dn * tn, axis=-1, keepdims=True) * (1.0 / width)))


def _place():
    return lax.axis_index("x"), lax.axis_index("y"), lax.axis_index("c")


class _ChipExchange:
    def __init__(self, xs, gather):
        self.xs, self.gather, self.n = list(xs), gather, len(xs)
        self.in_specs = [pl.BlockSpec(memory_space=pl.ANY)] * self.n
        self.out_specs = [pl.BlockSpec(memory_space=pl.ANY)] * self.n
        self.out_shape = [_sds((4,) + (x.shape if gather else x.shape[1:]), x.dtype) for x in self.xs]
        self.scratch = [pltpu.SemaphoreType.DMA((self.n, 3)), pltpu.SemaphoreType.DMA((self.n, 3))]

    def bind(self, x_refs, out_refs, send_sems, recv_sems):
        x, y, c = _place()
        p = 2 * x + y
        chips = [(1 - x, y), (x, 1 - y), (1 - x, 1 - y)]

        def copy(w, k, outgoing):
            qx, qy = chips[k]
            there = 2 * qx + qy
            if self.gather:
                src = x_refs[w]
            else:
                src = x_refs[w].at[there if outgoing else p]
            return pltpu.make_async_remote_copy(
                src_ref=src, dst_ref=out_refs[w].at[p if outgoing else there], send_sem=send_sems.at[w, k],
                recv_sem=recv_sems.at[w, k], device_id=(qx, qy, c), device_id_type=MESH)

        def start():
            for w in range(self.n):
                for k in range(3):
                    copy(w, k, True).start()

        def wait():
            for w in range(self.n):
                for k in range(3):
                    copy(w, k, False).wait_recv()
            for w in range(self.n):
                for k in range(3):
                    copy(w, k, True).wait_send()

        return start, wait


class _PairExchange:
    def __init__(self, xs, mode):
        self.xs, self.mode, self.n = list(xs), mode, len(xs)
        self.in_specs = [pl.BlockSpec(memory_space=pl.ANY)] * self.n
        self.out_specs = [pl.BlockSpec(memory_space=pl.ANY)] * self.n
        shape = {"halves": lambda s: (4,) + s[2:], "forward": lambda s: s, "gather": lambda s: (2,) + s}[mode]
        self.out_shape = [_sds(shape(x.shape), x.dtype) for x in self.xs]
        self.scratch = [pltpu.SemaphoreType.DMA((self.n, 3)), pltpu.SemaphoreType.DMA((self.n, 3))]

    def bind(self, x_refs, out_refs, send_sems, recv_sems):
        x, y, c = _place()
        chips = [(1 - x, y), (x, 1 - y), (1 - x, 1 - y)]

        def copy(w, src, dst, k):
            return pltpu.make_async_remote_copy(src_ref=src, dst_ref=dst, send_sem=send_sems.at[w, k],
                                                recv_sem=recv_sems.at[w, k], device_id=(x, y, 1 - c),
                                                device_id_type=MESH)

        def start():
            for w, (xr, orf) in enumerate(zip(x_refs, out_refs)):
                if self.mode == "halves":
                    for q in range(4):
                        copy(w, xr.at[q, 1 - c], orf.at[q], 0).start()
                elif self.mode == "forward":
                    for k, (qx, qy) in enumerate(chips):
                        copy(w, xr.at[2 * qx + qy], orf.at[2 * qx + qy], k).start()
                else:
                    copy(w, xr, orf.at[c], 0).start()

        def wait():
            for w, (xr, orf) in enumerate(zip(x_refs, out_refs)):
                if self.mode == "halves":
                    copy(w, orf, orf, 0).wait()
                elif self.mode == "forward":
                    for k, (qx, qy) in enumerate(chips):
                        copy(w, xr.at[2 * qx + qy], orf.at[2 * qx + qy], k).wait()
                else:
                    cp = copy(w, xr, orf.at[1 - c], 0)
                    cp.wait_recv()
                    cp.wait_send()

        return start, wait


def _split_refs(refs, n_in, n_out, ex):
    ne = 0 if ex is None else ex.n
    ins, xin = refs[:n_in], refs[n_in:n_in + ne]
    outs, xout = refs[n_in + ne:n_in + ne + n_out], refs[n_in + ne + n_out:n_in + 2 * ne + n_out]
    rest = refs[n_in + 2 * ne + n_out:]
    if ex is None:
        return ins, outs, rest, None, None
    start, wait = ex.bind(xin, xout, rest[-2], rest[-1])
    return ins, outs, rest[:-2], start, wait


def _attn_fwd(q, k, v, *, heads, tq=512, carry=None):
    N, M = q.shape[0], k.shape[0]
    tq = _tile(N, tq)
    sub = _tile(tq, MXU_DIM)
    vd = v.shape[1] // heads
    nq = N // tq

    def body(*refs):
        (q_ref, k_ref, v_ref), (o_ref, lse_ref), _, start, wait = _split_refs(refs, 3, 2, carry)
        if carry is not None:
            pl.when((pl.program_id(0) == 0) & (pl.program_id(1) == 0))(start)
        for sb in range(tq // sub):
            rows = pl.ds(sb * sub, sub)
            s = lax.dot_general(q_ref[rows, :], k_ref[...], (((1,), (1,)), ((), ())), preferred_element_type=F32)
            m = jnp.max(s, axis=-1, keepdims=True)
            p = jnp.exp(s - m)
            l = jnp.sum(p, axis=-1, keepdims=True)
            o = jnp.dot(p.astype(BF16), v_ref[...], preferred_element_type=F32) / l
            o_ref[rows, :] = o.astype(o_ref.dtype)
            lse_ref[rows, :] = jnp.broadcast_to(m + jnp.log(l), (sub, vd))
        if carry is not None:
            pl.when((pl.program_id(0) == heads - 1) & (pl.program_id(1) == nq - 1))(wait)

    ex = carry
    res = pl.pallas_call(
        body, name="attn_fwd", grid=(heads, nq),
        in_specs=[pl.BlockSpec((tq, HEAD_PAD), lambda h, i: (i, h)),
                  pl.BlockSpec((M, HEAD_PAD), lambda h, i: (0, h)),
                  pl.BlockSpec((M, vd), lambda h, i: (0, h))] + ([] if ex is None else ex.in_specs),
        out_specs=[pl.BlockSpec((tq, vd), lambda h, i: (i, h)),
                   pl.BlockSpec((tq, vd), lambda h, i: (i, h))] + ([] if ex is None else ex.out_specs),
        out_shape=[_sds((N, heads * vd), BF16), _sds((N, heads * vd), F32)] + ([] if ex is None else ex.out_shape),
        scratch_shapes=[] if ex is None else ex.scratch,
        compiler_params=_params(("arbitrary", "arbitrary")),
    )(q, k, v, *([] if ex is None else ex.xs))
    return res[0], res[1], list(res[2:])


def _attn_bwd(q, k, v, o, lse, do, *, heads, tq=512, carry=None):
    N, M = q.shape[0], k.shape[0]
    tq = _tile(N, tq)
    vd = v.shape[1] // heads
    nq = N // tq
    sub = _tile(tq, MXU_DIM)
    nt = (((1,), (1,)), ((), ()))
    tn = (((0,), (0,)), ((), ()))

    def body(*refs):
        (q_ref, k_ref, v_ref, o_ref, lse_ref, do_ref), (dq_ref, dk_ref, dv_ref), _, start, wait = _split_refs(
            refs, 6, 3, carry)
        if carry is not None:
            pl.when((pl.program_id(0) == 0) & (pl.program_id(1) == 0))(start)
        i = pl.program_id(1)
        kb, vb = k_ref[...], v_ref[...]
        parts = []
        for sb in range(tq // sub):
            rows = pl.ds(sb * sub, sub)
            qb, dob = q_ref[rows, :], do_ref[rows, :]
            delta = jnp.sum(dob.astype(F32) * o_ref[rows, :].astype(F32), axis=-1, keepdims=True)
            s = lax.dot_general(qb, kb, nt, preferred_element_type=F32)
            p = jnp.exp(s - lse_ref[rows, :][:, :1])
            dp = lax.dot_general(dob, vb, nt, preferred_element_type=F32)
            ds = (p * (dp - delta)).astype(BF16)
            dq_ref[rows, :] = jnp.dot(ds, kb, preferred_element_type=F32)
            parts.append((lax.dot_general(ds, qb, tn, preferred_element_type=F32),
                          lax.dot_general(p.astype(BF16), dob, tn, preferred_element_type=F32)))

        dk_step, dv_step = parts[0]
        for dk_part, dv_part in parts[1:]:
            dk_step, dv_step = dk_step + dk_part, dv_step + dv_part

        @pl.when(i == 0)
        def _():
            dk_ref[...] = dk_step
            dv_ref[...] = dv_step

        @pl.when(i > 0)
        def _():
            dk_ref[...] += dk_step
            dv_ref[...] += dv_step

        if carry is not None:
            pl.when((pl.program_id(0) == heads - 1) & (pl.program_id(1) == nq - 1))(wait)

    ex = carry
    res = pl.pallas_call(
        body, name="attn_bwd", grid=(heads, nq),
        in_specs=[pl.BlockSpec((tq, HEAD_PAD), lambda h, i: (i, h)),
                  pl.BlockSpec((M, HEAD_PAD), lambda h, i: (0, h)),
                  pl.BlockSpec((M, vd), lambda h, i: (0, h)),
                  pl.BlockSpec((tq, vd), lambda h, i: (i, h)),
                  pl.BlockSpec((tq, vd), lambda h, i: (i, h)),
                  pl.BlockSpec((tq, vd), lambda h, i: (i, h))] + ([] if ex is None else ex.in_specs),
        out_specs=[pl.BlockSpec((tq, HEAD_PAD), lambda h, i: (i, h)),
                   pl.BlockSpec((M, HEAD_PAD), lambda h, i: (0, h)),
                   pl.BlockSpec((M, vd), lambda h, i: (0, h))] + ([] if ex is None else ex.out_specs),
        out_shape=[_sds((N, heads * HEAD_PAD), F32), _sds((M, heads * HEAD_PAD), F32),
                   _sds((M, heads * vd), F32)] + ([] if ex is None else ex.out_shape),
        scratch_shapes=[] if ex is None else ex.scratch,
        compiler_params=_params(("arbitrary", "arbitrary")),
    )(q, k, v, o, lse, do, *([] if ex is None else ex.xs))
    return res[0], res[1], res[2], list(res[3:])


def _comm_call(body, xs, out_shapes, n_sems, name, in_vmem):
    space = pltpu.VMEM if in_vmem else pl.ANY
    n = len(xs)

    def wrapped(*refs):
        body(refs[:n], refs[n:2 * n], *refs[2 * n:])

    return pl.pallas_call(
        wrapped, name=name, out_shape=list(out_shapes),
        in_specs=[pl.BlockSpec(memory_space=space)] * n, out_specs=[pl.BlockSpec(memory_space=space)] * n,
        scratch_shapes=[pltpu.SemaphoreType.DMA((n, n_sems)), pltpu.SemaphoreType.DMA((n, n_sems)),
                        pltpu.SemaphoreType.DMA((n,))],
        compiler_params=pltpu.CompilerParams(vmem_limit_bytes=VMEM_LIMIT_BYTES),
    )(*xs)


def _all_gather8(blks, *, name, in_vmem):
    def body(x_refs, out_refs, send_sems, recv_sems, local_sems):
        x, y, c = _place()
        me, sibling = (x, y, c), (x, y, 1 - c)
        chips = [(1 - x, y), (x, 1 - y), (1 - x, 1 - y)]
        waits = []
        for w, (x_ref, out_ref) in enumerate(zip(x_refs, out_refs)):
            def slot(px, py, pc, out_ref=out_ref):
                return out_ref.at[4 * px + 2 * py + pc]

            def copy(k, block, to, src=None, w=w, slot=slot):
                return pltpu.make_async_remote_copy(
                    src_ref=slot(*block) if src is None else src, dst_ref=slot(*block),
                    send_sem=send_sems.at[w, k], recv_sem=recv_sems.at[w, k], device_id=to, device_id_type=MESH)

            mine = pltpu.make_async_copy(x_ref, slot(*me), local_sems.at[w])
            mine.start()
            first = [copy(0, me, sibling, src=x_ref)]
            first += [copy(1 + j, me, (*chip, c), src=x_ref) for j, chip in enumerate(chips)]
            for cp in first:
                cp.start()
            waits.append((copy, mine, first))
        for copy, mine, first in waits:
            passed = [copy(4 + j, (*chip, c), sibling) for j, chip in enumerate(chips)]
            for j, chip in enumerate(chips):
                copy(1 + j, (*chip, c), me).wait_recv()
                passed[j].start()
            copy(0, sibling, me).wait_recv()
            for j, chip in enumerate(chips):
                copy(4 + j, (*chip, 1 - c), me).wait_recv()
            for cp in first + passed:
                cp.wait_send()
            mine.wait()

    return _comm_call(body, blks, [_sds((8,) + b.shape, b.dtype) for b in blks], 7, name, in_vmem)


def _gather_others(blks, *, name):
    def body(x_refs, out_refs, send_sems, recv_sems, local_sems):
        x, y, c = _place()
        own, xn, yn, dg = (x, y), (1 - x, y), (x, 1 - y), (1 - x, 1 - y)

        def slot(w, chip, core):
            return out_refs[w].at[4 * chip[0] + 2 * chip[1] + core]

        def cp(w, k, src, dst, chip, core):
            return pltpu.make_async_remote_copy(src_ref=src, dst_ref=dst, send_sem=send_sems.at[w, k],
                                                recv_sem=recv_sems.at[w, k], device_id=(*chip, core),
                                                device_id_type=MESH)

        def halves(w):
            h = x_refs[w].shape[0] // 2
            return pl.ds(0, h), pl.ds(h, h)

        sends = []
        for w, x_ref in enumerate(x_refs):
            sends += [cp(w, 0, x_ref, slot(w, own, c), xn, c), cp(w, 1, x_ref, slot(w, own, c), yn, c)]
        for s in sends:
            s.start()
        for w, x_ref in enumerate(x_refs):
            lo, hi = halves(w)
            cp(w, 1, x_ref, slot(w, yn, c), yn, c).wait_recv()
            passed = [cp(w, 2, slot(w, yn, c).at[lo], slot(w, yn, c).at[lo], xn, c),
                      cp(w, 4, slot(w, yn, c), slot(w, yn, c), own, 1 - c)]
            cp(w, 0, x_ref, slot(w, xn, c), xn, c).wait_recv()
            passed += [cp(w, 3, slot(w, xn, c).at[hi], slot(w, xn, c).at[hi], yn, c),
                       cp(w, 5, slot(w, xn, c), slot(w, xn, c), own, 1 - c)]
            for s in passed:
                s.start()
            sends += passed
        for w in range(len(x_refs)):
            lo, hi = halves(w)
            cp(w, 2, slot(w, dg, c).at[lo], slot(w, dg, c).at[lo], xn, c).wait_recv()
            cp(w, 3, slot(w, dg, c).at[hi], slot(w, dg, c).at[hi], yn, c).wait_recv()
            passed = [cp(w, 6, slot(w, dg, c), slot(w, dg, c), own, 1 - c)]
            passed[0].start()
            sends += passed
        for w in range(len(x_refs)):
            cp(w, 4, slot(w, yn, c), slot(w, yn, 1 - c), own, 1 - c).wait_recv()
            cp(w, 5, slot(w, xn, c), slot(w, xn, 1 - c), own, 1 - c).wait_recv()
            cp(w, 6, slot(w, dg, c), slot(w, dg, 1 - c), own, 1 - c).wait_recv()
        for s in sends:
            s.wait_send()

    return list(_comm_call(body, blks, [_sds((8,) + b.shape, b.dtype) for b in blks], 7, name, False))


def _exchange_alone(ex, *, name):
    def body(x_refs, out_refs, send_sems, recv_sems, local_sems):
        start, wait = ex.bind(x_refs, out_refs, send_sems, recv_sems)
        start()
        wait()

    return list(_comm_call(body, ex.xs, ex.out_shape, 3, name, False))


def _block_rows(rows, row_bytes, target=1 << 21, align=BF16_SUBLANES):
    return _tile(rows, max(align, target // row_bytes // align * align), align)


def _sum_blocks(buf, *, name, out_dtype):
    B, R, C = buf.shape
    tm = _block_rows(R, B * C * buf.dtype.itemsize)

    def body(x_ref, o_ref):
        acc = x_ref[0].astype(F32)
        for b in range(1, B):
            acc = acc + x_ref[b].astype(F32)
        o_ref[...] = acc.astype(o_ref.dtype)

    return pl.pallas_call(
        body, name=name, grid=(R // tm,), in_specs=[pl.BlockSpec((B, tm, C), lambda i: (0, i, 0))],
        out_specs=pl.BlockSpec((tm, C), lambda i: (i, 0)), out_shape=_sds((R, C), out_dtype),
        compiler_params=_params(("parallel",)),
    )(buf)


def _sum_chips(received, sent, chip, *, name):
    _, R, C = received.shape
    tm = _block_rows(R, 5 * C * received.dtype.itemsize)

    def body(chip_ref, r0, r1, r2, r3, own_ref, o_ref):
        acc = None
        for q, r in enumerate((r0, r1, r2, r3)):
            term = jnp.where(q == chip_ref[0], own_ref[...], r[...]).astype(F32)
            acc = term if acc is None else acc + term
        o_ref[...] = acc

    def slot(q):
        return pl.BlockSpec((None, tm, C), lambda i, ch, q=q: (jnp.where(q == ch[0], (q + 1) % 4, q), i, 0))

    return pl.pallas_call(
        body, name=name, out_shape=_sds((R, C), F32),
        grid_spec=pltpu.PrefetchScalarGridSpec(
            num_scalar_prefetch=1, grid=(R // tm,),
            in_specs=[slot(0), slot(1), slot(2), slot(3), pl.BlockSpec((None, tm, C), lambda i, ch: (ch[0], i, 0))],
            out_specs=pl.BlockSpec((tm, C), lambda i, ch: (i, 0))),
        compiler_params=_params(("arbitrary",)),
    )(chip, received, received, received, received, sent)


def _pair_add(mine, theirs, core, *, name):
    _, _, R, C = mine.shape
    tm = _block_rows(R, C * 2)

    def body(core_ref, a_ref, b_ref, o_ref):
        o_ref[...] = (a_ref[...].astype(F32) + b_ref[...].astype(F32)).astype(o_ref.dtype)

    return pl.pallas_call(
        body, name=name, out_shape=_sds(theirs.shape, BF16),
        grid_spec=pltpu.PrefetchScalarGridSpec(
            num_scalar_prefetch=1, grid=(4, R // tm),
            in_specs=[pl.BlockSpec((None, None, tm, C), lambda q, i, core_ref: (q, core_ref[0], i, 0)),
                      pl.BlockSpec((None, tm, C), lambda q, i, core_ref: (q, i, 0))],
            out_specs=pl.BlockSpec((None, tm, C), lambda q, i, core_ref: (q, i, 0))),
        compiler_params=_params(("parallel", "parallel")),
    )(core, mine, theirs)


def _assemble(gathered, own, chip, *, name, transpose):
    _, K, Ns = gathered.shape
    tm = _block_rows(K, Ns * 4)

    def body(chip_ref, g_ref, own_ref, o_ref):
        q = pl.program_id(0)

        @pl.when(q == chip_ref[0])
        def _():
            o_ref[...] = own_ref[...].astype(BF16)

        @pl.when(q != chip_ref[0])
        def _():
            o_ref[...] = g_ref[...]

    if transpose:
        out_spec = pl.BlockSpec((tm, Ns), lambda q, i, ch: (i, q))
        out_shape = _sds((K, 4 * Ns), BF16)
    else:
        out_spec = pl.BlockSpec((None, tm, Ns), lambda q, i, ch: (q, i, 0))
        out_shape = _sds((4, K, Ns), BF16)
    return pl.pallas_call(
        body, name=name, out_shape=out_shape,
        grid_spec=pltpu.PrefetchScalarGridSpec(
            num_scalar_prefetch=1, grid=(4, K // tm),
            in_specs=[pl.BlockSpec((None, tm, Ns), lambda q, i, ch: (jnp.where(q == ch[0], (q + 1) % 4, q), i, 0)),
                      pl.BlockSpec((tm, Ns), lambda q, i, ch: (jnp.where(q == ch[0], i, 0), 0))],
            out_specs=out_spec),
        compiler_params=_params(("arbitrary", "arbitrary")),
    )(chip, gathered, own)


def _assemble_halves(mine, theirs, own, place, *, name, transpose):
    _, K2, Ns = mine.shape
    tm = _block_rows(K2, Ns * 4, target=1 << 22)
    nb = K2 // tm

    def body(place_ref, m_ref, t_ref, own_ref, o_ref):
        q, hb = pl.program_id(0), pl.program_id(1)
        is_own = q == place_ref[0]
        is_mine = hb == place_ref[1]

        @pl.when(is_own)
        def _():
            o_ref[...] = own_ref[...].astype(BF16)

        @pl.when(jnp.logical_not(is_own) & is_mine)
        def _():
            o_ref[...] = m_ref[...]

        @pl.when(jnp.logical_not(is_own) & jnp.logical_not(is_mine))
        def _():
            o_ref[...] = t_ref[...]

    def other(q, pr):
        return jnp.where(q == pr[0], (q + 1) % 4, q)

    if transpose:
        out_spec = pl.BlockSpec((tm, Ns), lambda q, hb, i, pr: (hb * nb + i, q))
        out_shape = _sds((2 * K2, 4 * Ns), BF16)
    else:
        out_spec = pl.BlockSpec((None, tm, Ns), lambda q, hb, i, pr: (q, hb * nb + i, 0))
        out_shape = _sds((4, 2 * K2, Ns), BF16)
    return pl.pallas_call(
        body, name=name, out_shape=out_shape,
        grid_spec=pltpu.PrefetchScalarGridSpec(
            num_scalar_prefetch=1, grid=(4, 2, nb),
            in_specs=[pl.BlockSpec((None, tm, Ns), lambda q, hb, i, pr: (other(q, pr), jnp.where(hb == pr[1], i, 0), 0)),
                      pl.BlockSpec((None, tm, Ns), lambda q, hb, i, pr: (other(q, pr), jnp.where(hb == pr[1], 0, i), 0)),
                      pl.BlockSpec((tm, Ns), lambda q, hb, i, pr: (jnp.where(q == pr[0], hb * nb + i, 0), 0))],
            out_specs=out_spec),
        compiler_params=_params(("arbitrary",) * 3),
    )(place, mine, theirs, own)


def _split_lanes(row, widths):
    out, off = [], 0
    for wd in widths:
        out.append(row[:, off:off + wd])
        off += wd
    return out


def _adamw_math(w, g, m, v):
    m = ADAM_B1 * m + (1.0 - ADAM_B1) * g
    v = ADAM_B2 * v + (1.0 - ADAM_B2) * (g * g)
    m_hat = m / (1.0 - ADAM_B1 ** ADAM_STEP)
    v_hat = v / (1.0 - ADAM_B2 ** ADAM_STEP)
    delta = -ADAM_LR * (m_hat / (jnp.sqrt(v_hat) + ADAM_EPS) + ADAM_WD * w)
    return delta, m, v


def _adamw_halves(w, mine, other, m, v, core, *, name):
    K, Ns = w.shape
    tm = _block_rows(K // 2, Ns * 4, target=1 << 20, align=F32_SUBLANES)
    nb = (K // 2) // tm

    def body(core_ref, w_ref, mine_ref, other_ref, m_ref, v_ref, g_out, d_out, m_out, v_out):
        g = jnp.where(pl.program_id(0) // nb == core_ref[0], mine_ref[...], other_ref[...])
        g_out[...] = g
        d_out[...], m_out[...], v_out[...] = _adamw_math(w_ref[...], g, m_ref[...], v_ref[...])

    row = pl.BlockSpec((tm, Ns), lambda i, cr: (i, 0))
    return pl.pallas_call(
        body, name=name, out_shape=[_sds((K, Ns), F32)] * 4,
        grid_spec=pltpu.PrefetchScalarGridSpec(
            num_scalar_prefetch=1, grid=(K // tm,),
            in_specs=[row,
                      pl.BlockSpec((tm, Ns), lambda i, cr: (jnp.where(i // nb == cr[0], i % nb, 0), 0)),
                      pl.BlockSpec((None, tm, Ns), lambda i, cr: (1 - cr[0], jnp.where(i // nb == cr[0], 0, i % nb), 0)),
                      row, row],
            out_specs=[row, row, row, row]),
        compiler_params=_params(("arbitrary",)),
    )(core, w, mine, other, m, v)


def _adamw(w, g, m, v, *, name, carry=None):
    C = w.shape[1]

    def fn(w, g, m, v):
        return _adamw_math(w, g, m, v), ()

    tm = max(F32_SUBLANES, min(512, (1 << 20) // (4 * C) // F32_SUBLANES * F32_SUBLANES))
    res = _rowwise(fn, [w, g, m, v], [], [(C, F32)] * 3, name=name, tm=tm, carry=carry)
    return tuple(res[0]) + ((res[2],) if carry is not None else ())


def _rope_tables(n):
    rows = n // GRID_W
    row = jnp.repeat(jnp.arange(rows, dtype=F32), GRID_W)
    col = jnp.tile(jnp.arange(GRID_W, dtype=F32), rows)
    nf = ROPE_DIM // 4
    freqs = ROPE_THETA ** (-jnp.arange(nf, dtype=F32) / nf)
    ang_r, ang_c = row[:, None] * freqs[None, :], col[:, None] * freqs[None, :]
    cr, sr, cc, sc = jnp.cos(ang_r), jnp.sin(ang_r), jnp.cos(ang_c), jnp.sin(ang_c)
    nope = HEAD_PAD - 2 * ROPE_DIM
    one, zero, z = jnp.ones((n, nope), F32), jnp.zeros((n, nope), F32), jnp.zeros((n, nf), F32)
    pad = jnp.zeros((n, ROPE_DIM), F32)
    cos = jnp.concatenate([one, cr, cr, cc, cc, pad], axis=1)
    s_lo = jnp.concatenate([zero, -sr, z, -sc, z, pad], axis=1)
    s_hi = jnp.concatenate([zero, z, sr, z, sc, pad], axis=1)
    return cos, s_lo, s_hi


def _rope(n, cos, s_lo, s_hi):
    q = ROPE_DIM // 4
    return n * cos + pltpu.roll(n, HEAD_PAD - q, 1) * s_lo + pltpu.roll(n, q, 1) * s_hi


def _rope_t(d, cos, s_lo, s_hi):
    q = ROPE_DIM // 4
    return d * cos + pltpu.roll(d * s_lo, q, 1) + pltpu.roll(d * s_hi, HEAD_PAD - q, 1)


def kernel(x, c, ctx, c_ctx, w_mod, b_mod, norm1_g, w_in, q_norm_g, kv_norm_g, w_uq, w_ukv, qk_norm_q, qk_norm_k, sgu_norm_g, sgu_norm_b, w_spatial, b_spatial, w_br_attn, w_br_sgu, w_out, norm2_g, w_ffn_in, w_ffn_out, loss_target, m_c_ctx, m_w_mod, m_b_mod, m_norm1_g, m_w_in, m_q_norm_g, m_kv_norm_g, m_w_uq, m_w_ukv, m_qk_norm_q, m_qk_norm_k, m_sgu_norm_g, m_sgu_norm_b, m_w_spatial, m_b_spatial, m_w_br_attn, m_w_br_sgu, m_w_out, m_norm2_g, m_w_ffn_in, m_w_ffn_out, v_c_ctx, v_w_mod, v_b_mod, v_norm1_g, v_w_in, v_q_norm_g, v_kv_norm_g, v_w_uq, v_w_ukv, v_qk_norm_q, v_qk_norm_k, v_sgu_norm_g, v_sgu_norm_b, v_w_spatial, v_b_spatial, v_w_br_attn, v_w_br_sgu, v_w_out, v_norm2_g, v_w_ffn_in, v_w_ffn_out):
    ax, ay, ac = _place()
    my_chip = 2 * ax + ay
    my_dev = 4 * ax + 2 * ay + ac

    N, D = x.shape[1], x.shape[2]
    CT = ctx.shape[1]
    M = N + CT
    QL, KVL, QK = q_norm_g.shape[-1], kv_norm_g.shape[-1], qk_norm_q.shape[-1]
    NOPE = QK - ROPE_DIM
    VD = NOPE
    H = 4 * w_uq.shape[-1] // QK
    SW, G, CH = sgu_norm_g.shape[-1], w_spatial.shape[1], w_spatial.shape[2]
    GD = SW // G
    DFF = 4 * w_ffn_out.shape[1]
    NMOD = 4 * w_mod.shape[-1]
    NM = w_mod.shape[-1]
    KVP = KVL + 2 * ROPE_DIM
    assert NOPE == LANES and GD == LANES and HEAD_PAD == NOPE + 2 * ROPE_DIM and CH == LANES
    scale = QK ** -0.5

    x2, ctx2, tgt2 = x[0], ctx[0], loss_target[0]

    c_all = _all_gather8([c], name="ag_c", in_vmem=True)[0][:, 0, :]
    c_rows = jnp.concatenate([c_all, c_ctx[None, :], jnp.zeros((BF16_SUBLANES - 9, D), F32)], axis=0)

    def silu_fn(t):
        s = _sigmoid(t)
        return (t * s, s * (1.0 + t * (1.0 - s))), ()

    (silu_c, dsilu_c), _ = _rowwise(silu_fn, [c_rows], [], [(D, F32), (D, F32)], name="silu_c", tm=16)
    wm = w_mod[0]
    mod_loc = _mm([(silu_c, wm)], name="mod_fwd", outs=(F32,),
                  extras=[(lax.dynamic_slice_in_dim(b_mod, my_chip * NM, NM, axis=1), "n")],
                  epi=lambda acc, b: (acc + b,))
    mod_all = _all_gather8([mod_loc], name="ag_mod", in_vmem=True)[0]
    mod_full = jnp.concatenate([mod_all[0], mod_all[2], mod_all[4], mod_all[6]], axis=1)
    mod_me = lax.dynamic_slice_in_dim(mod_full, my_dev, 1, axis=0)
    sh1, sc1, g1, sh2, sc2, g2 = [mod_me[:, i * D:(i + 1) * D] for i in range(6)]
    sh1c, sc1c = mod_full[8:9, :D], mod_full[8:9, D:2 * D]

    big = [w_in[0], w_uq[0], w_ukv[0], w_br_attn[0], w_br_sgu[0], w_out[0], w_ffn_in[0], w_ffn_out[0]]
    col_sharded = [True, True, True, True, True, False, True, False]
    halves = [lax.dynamic_slice_in_dim(a, ac * (a.shape[0] // 2), a.shape[0] // 2, axis=0).astype(BF16) for a in big]
    tags = ["w_in", "w_uq", "w_ukv", "w_br_attn", "w_br_sgu", "w_out", "w_ffn_in", "w_ffn_out"]
    first_group, attn_group, ffn_group = [0, 1, 2], [3, 4, 5, 6], [7]
    chip1 = jnp.reshape(my_chip, (1,)).astype(jnp.int32)
    place2 = jnp.stack([my_chip, ac]).astype(jnp.int32)

    def laid_out(seg, i):
        a = big[i]
        if col_sharded[i] and seg.ndim == 3:
            return seg.transpose(1, 0, 2).reshape(a.shape[0], 4 * a.shape[1])
        return seg if col_sharded[i] else seg.reshape(4 * a.shape[0], a.shape[1])

    def side_by_side(i):
        return col_sharded[i] and big[i].shape[1] % LANES == 0

    def finish_gather(idx, mine4, theirs4):
        return [laid_out(_assemble_halves(m, t, big[i], place2, name="assemble_" + tags[i], transpose=side_by_side(i)), i)
                for i, m, t in zip(idx, mine4, theirs4)]

    gathered = _gather_others([halves[i] for i in first_group], name="ag_weights")
    w_in_f, w_uq_f, w_ukv_f = [
        laid_out(_assemble(seg.reshape((4,) + big[i].shape), big[i], chip1, name="assemble_" + tags[i],
                           transpose=side_by_side(i)), i) for i, seg in zip(first_group, gathered)]
    o_kv, o_u = QL, QL + KVL + ROPE_DIM
    o_v, o_g = o_u + SW, o_u + 2 * SW
    w_q = w_in_f[:, :QL]
    w_kv = jnp.pad(w_in_f[:, o_kv:o_u], ((0, 0), (0, ROPE_DIM)))
    w_u, w_v = w_in_f[:, o_u:o_v], w_in_f[:, o_v:o_g]
    w_g1, w_g2 = w_in_f[:, o_g:o_g + D], w_in_f[:, o_g + D:]
    w_uq_p = jnp.pad(w_uq_f.reshape(QL, H, QK), ((0, 0), (0, 0), (0, HEAD_PAD - QK))).reshape(QL, H * HEAD_PAD)

    cos_t, slo_t, shi_t = _rope_tables(N)
    ones_c = jnp.concatenate([jnp.ones((CT, NOPE + ROPE_DIM), F32), jnp.zeros((CT, ROPE_DIM), F32)], axis=1)
    cos_k = jnp.concatenate([cos_t, ones_c], axis=0)
    slo_k = jnp.concatenate([slo_t, jnp.zeros((CT, HEAD_PAD), F32)], axis=0)
    shi_k = jnp.concatenate([shi_t, jnp.zeros((CT, HEAD_PAD), F32)], axis=0)
    gq_p = jnp.pad(qk_norm_q, ((0, 0), (0, HEAD_PAD - QK)))
    gk_p = jnp.pad(qk_norm_k, ((0, 0), (0, HEAD_PAD - QK)))

    def norm_mod_fn(t, g, sh, sc):
        r = _rms_stats(t, D)
        return (((t * r) * g) * (1.0 + sc) + sh,), ()

    (h,), _ = _rowwise(norm_mod_fn, [x2], [norm1_g, sh1, sc1], [(D, BF16)], name="norm1_x")
    (ctx_h,), _ = _rowwise(norm_mod_fn, [ctx2], [norm1_g, sh1c, sc1c], [(D, BF16)], name="norm1_ctx")

    def q_norm_epi(acc, g):
        return acc, (acc * _rms_stats(acc, QL)) * g

    qc, qn = _mm([(h, w_q)], name="proj_q", outs=(F32, BF16), tn=QL, extras=[(q_norm_g, "n")], epi=q_norm_epi)
    kvin = jnp.concatenate([_mm([(h, w_kv)], name="proj_kv", outs=(F32,)),
                            _mm([(ctx_h, w_kv)], name="proj_kv_ctx", outs=(F32,))], axis=0)
    def both(a, b):
        return a, b

    u_in, v_in = _mm([(h, w_u, w_v)], name="proj_uv", outs=(BF16, BF16), epi=both)
    (g1_in, g2_in), (mine_bra, mine_brs) = _mm([(h, w_g1, w_g2)], name="proj_gates", outs=(BF16, BF16), epi=both,
                                               carry=_ChipExchange([halves[3], halves[4]], gather=True))

    def kv_norm_fn(t, g):
        kvc = t[:, :KVL]
        return (((kvc * _rms_stats(kvc, KVL)) * g),), ()

    (kvn,), _ = _rowwise(kv_norm_fn, [kvin], [kv_norm_g], [(KVL, BF16)], name="kv_norm")
    q_raw = _mm([(qn, w_uq_p)], name="q_up", outs=(F32,))
    kv_rows = _tile(M, 2304)
    kv_raw = _mm([(kvn, w_ukv_f)], name="kv_up", outs=(F32,), tm=kv_rows)

    def q_post_fn(t, cos, slo, shi, g):
        outs = []
        for hd in range(H):
            th = t[:, hd * HEAD_PAD:(hd + 1) * HEAD_PAD]
            outs.append(_rope((th * _rms_stats(th, QK)) * g, cos, slo, shi) * scale)
        return (jnp.concatenate(outs, axis=1),), ()

    (q_att,), _ = _rowwise(q_post_fn, [q_raw, cos_t, slo_t, shi_t], [gq_p], [(H * HEAD_PAD, BF16)], name="q_post")

    def k_post_fn(t, kvi, cos, slo, shi, g):
        kr = kvi[:, KVL:]
        ks, vs = [], []
        for hd in range(H):
            th = jnp.concatenate([t[:, hd * HEAD_PAD:hd * HEAD_PAD + NOPE], kr], axis=1)
            ks.append(_rope((th * _rms_stats(th, QK)) * g, cos, slo, shi))
            vs.append(t[:, hd * HEAD_PAD + NOPE:(hd + 1) * HEAD_PAD])
        return (jnp.concatenate(ks, axis=1), jnp.concatenate(vs, axis=1)), ()

    (k_att, v_att), _, (mine_out,) = _rowwise(k_post_fn, [kv_raw, kvin, cos_k, slo_k, shi_k], [gk_p],
                                              [(H * HEAD_PAD, BF16), (H * VD, BF16)], name="k_post",
                                              carry=_ChipExchange([halves[5]], gather=True))
    attn_o, lse, (mine_ffi,) = _attn_fwd(q_att, k_att, v_att, heads=H, carry=_ChipExchange([halves[6]], gather=True))
    mine4 = [mine_bra, mine_brs, mine_out, mine_ffi]

    ws3 = w_spatial[0]
    bs_t = jnp.pad(b_spatial[0].T, ((0, 0), (0, LANES - G)))

    def sgu_parts(u_in, v_in, ng, nb):
        u, v = _gelu(u_in.astype(F32)), _gelu(v_in.astype(F32))
        mu = jnp.mean(v, axis=-1, keepdims=True)
        vc = v - mu
        rs = lax.rsqrt(jnp.mean(vc * vc, axis=-1, keepdims=True) + EPS)
        xhat = vc * rs
        return u, xhat, rs, (xhat * ng + nb).astype(BF16)

    def sgu_fwd_fn(u_in, v_in, ng, nb, ws, bst):
        u, _, _, vnb = sgu_parts(u_in, v_in, ng, nb)
        outs = []
        for g in range(G):
            sl = slice(g * GD, (g + 1) * GD)
            mixed = jnp.dot(ws[g].astype(BF16), vnb[:, sl], preferred_element_type=F32) + bst[:, g:g + 1]
            outs.append(u[:, sl] * mixed)
        return (jnp.concatenate(outs, axis=1),), ()

    (sgu_o,), _, theirs4 = _rowwise(sgu_fwd_fn, [u_in, v_in], [sgu_norm_g, sgu_norm_b, ws3, bs_t], [(SW, BF16)],
                                    name="sgu_fwd", tm=CH, carry=_PairExchange(mine4, "forward"))
    w_bra, w_brs, w_out_f, w_ffi = finish_gather(attn_group, mine4, theirs4)
    w_fa, w_fb = w_ffi[:, :DFF], w_ffi[:, DFF:]

    a1 = _mm([(attn_o, w_bra)], name="br_attn", outs=(BF16,))
    def merge_epi(acc, a1v, gi1, gi2):
        return acc, _sigmoid(gi1.astype(F32)) * a1v.astype(F32) + _sigmoid(gi2.astype(F32)) * acc

    a2, merged = _mm([(sgu_o, w_brs)], name="br_sgu", outs=(BF16, BF16),
                     extras=[(a1, "mn"), (g1_in, "mn"), (g2_in, "mn")], epi=merge_epi)

    def res_gate(acc, res, gate):
        return res + gate * acc, acc

    x1, mo = _mm([(merged, w_out_f)], name="out_proj", outs=(F32, BF16), tn=1024,
                 extras=[(x2, "mn"), (g1, "n")], epi=res_gate)
    (h2,), _ = _rowwise(norm_mod_fn, [x1], [norm2_g, sh2, sc2], [(D, BF16)], name="norm2")

    def swiglu_epi(a, b):
        return a, b, (a * _sigmoid(a)) * b

    (fa, fb, act), mine4 = _mm([(h2, w_fa, w_fb)], name="ffn_in", outs=(BF16, BF16, BF16), tn=512, epi=swiglu_epi,
                               carry=_ChipExchange([halves[i] for i in ffn_group], gather=True))
    (w_ffo,) = finish_gather(ffn_group, mine4, _exchange_alone(_PairExchange(mine4, "forward"), name="ag_forward_ffn"))
    def loss_epi(acc, res, t, gate):
        e = (res + gate * acc) - t
        dy = e * (1.0 / D)
        return dy, gate * dy, _colsum(e * e) * (0.5 / D), _colsum(dy * acc)

    dy, df, loss_part, dg2_part = _mm([(act, w_ffo)], name="ffn_out", outs=(F32, BF16), tn=1024, col_sums=2,
                                      extras=[(x1, "mn"), (tgt2, "mn"), (g2, "n")], epi=loss_epi)

    def fold_fn(a, b):
        return (), (_colsum(a), _colsum(b))

    _, (loss_cols, dg2) = _rowwise(fold_fn, [loss_part[:, 0, :], dg2_part[:, 0, :]], [], [], [(1, D), (1, D)],
                                   name="loss_fold", tm=loss_part.shape[0])

    def swiglu_bwd_epi(dact, a, b):
        a, b = a.astype(F32), b.astype(F32)
        s = _sigmoid(a)
        return dact * b * (s * (1.0 + a * (1.0 - s))), dact * (a * s)

    da, db = _mm([(df, w_ffo)], tb=True, name="ffn_out_dx", outs=(BF16, BF16), tn=512,
                 extras=[(fa, "mn"), (fb, "mn")], epi=swiglu_bwd_epi)
    dw_ffo = _mm([(act, df)], ta=True, name="ffn_out_dw", outs=(BF16,), tm=1408)
    dh2 = _mm([(da, w_fa), (db, w_fb)], tb=True, name="ffn_in_dx", outs=(F32,))
    ns_ffi = w_ffn_in.shape[-1]
    dw_ffi = _mm([(h2, da)], ta=True, name="ffn_in_dw_a", outs=(BF16,), tn=1408, split=ns_ffi,
                 into=(lax.empty((4, D, ns_ffi), BF16), 0))
    dw_ffi = _mm([(h2, db)], ta=True, name="ffn_in_dw_b", outs=(BF16,), tn=1408, split=ns_ffi, into=(dw_ffi, 2))

    def norm2_bwd_fn(dh, t, dyv, mov, g, sc, g1v):
        r = _rms_stats(t, D)
        tn = t * r
        dxg = dh * (1.0 + sc)
        dt = dyv + _rms_bwd(dxg * g, tn, r, D)
        return (dt, g1v * dt), (_colsum(dh), _colsum(dh * (tn * g)), _colsum(dxg * tn), _colsum(dt * mov.astype(F32)))

    (dx1, dmo), (dsh2, dsc2, dn2g, dg1) = _rowwise(
        norm2_bwd_fn, [dh2, x1, dy, mo], [norm2_g, sc2, g1], [(D, F32), (D, BF16)], [(1, D)] * 4, name="norm2_bwd")

    def merge_bwd_epi(dm, a1, a2, gi1, gi2):
        s1, s2 = _sigmoid(gi1.astype(F32)), _sigmoid(gi2.astype(F32))
        a1, a2 = a1.astype(F32), a2.astype(F32)
        return dm * s1, dm * s2, dm * a1 * (s1 * (1.0 - s1)), dm * a2 * (s2 * (1.0 - s2))

    da1, da2, dgi1, dgi2 = _mm([(dmo, w_out_f)], tb=True, name="out_proj_dx", outs=(BF16,) * 4, tn=512,
                               extras=[(a1, "mn"), (a2, "mn"), (g1_in, "mn"), (g2_in, "mn")], epi=merge_bwd_epi)
    dw_out = _mm([(merged, dmo)], ta=True, name="out_proj_dw", outs=(BF16,))
    dattn = _mm([(da1, w_bra)], tb=True, name="br_attn_dx", outs=(BF16,))
    dw_bra = _mm([(attn_o, da1)], ta=True, name="br_attn_dw", outs=(BF16,), split=w_br_attn.shape[-1])
    dsgu = _mm([(da2, w_brs)], tb=True, name="br_sgu_dx", outs=(BF16,))
    dw_brs = _mm([(sgu_o, da2)], ta=True, name="br_sgu_dw", outs=(BF16,), split=w_br_sgu.shape[-1])

    def sgu_bwd_fn(dso, u_in, v_in, ng, nb, ws, bst):
        u, xhat, rs, vnb = sgu_parts(u_in, v_in, ng, nb)
        dso = dso.astype(F32)
        lane = lax.broadcasted_iota(jnp.int32, (CH, LANES), 1)
        du, dvn, dws, dbs = [], [], [], jnp.zeros((CH, LANES), F32)
        for g in range(G):
            sl = slice(g * GD, (g + 1) * GD)
            wg = ws[g].astype(BF16)
            mixed = jnp.dot(wg, vnb[:, sl], preferred_element_type=F32) + bst[:, g:g + 1]
            du.append(dso[:, sl] * mixed)
            dmix = dso[:, sl] * u[:, sl]
            dmb = dmix.astype(BF16)
            dws.append(lax.dot_general(dmb, vnb[:, sl], (((1,), (1,)), ((), ())), preferred_element_type=F32))
            dbs = dbs + jnp.where(lane == g, jnp.sum(dmix, axis=1, keepdims=True), 0.0)
            dvn.append(lax.dot_general(wg, dmb, (((0,), (0,)), ((), ())), preferred_element_type=F32))
        du, dvn = jnp.concatenate(du, axis=1), jnp.concatenate(dvn, axis=1)
        dxh = dvn * ng
        dv = rs * (dxh - jnp.mean(dxh, axis=-1, keepdims=True) - xhat * jnp.mean(dxh * xhat, axis=-1, keepdims=True))
        return ((du * _gelu_grad(u_in.astype(F32)), dv * _gelu_grad(v_in.astype(F32))),
                (_colsum(dvn * xhat), _colsum(dvn), jnp.stack(dws), dbs))

    core = jnp.reshape(ac, (1,)).astype(jnp.int32)

    def dest_layout(dwf, i):
        K, Ns = big[i].shape
        if dwf.ndim == 2:
            dwf = dwf.reshape(K, 4, Ns).transpose(1, 0, 2) if col_sharded[i] else dwf.reshape(4, K, Ns)
        return dwf.reshape(4, 2, K // 2, Ns)

    def pair_sums(idx, g4, sib):
        return [_pair_add(g, s, core, name="rs_pair_add_" + tags[i]) for g, s, i in zip(g4, sib, idx)]

    early = [3, 4, 5, 6, 7]
    g4_early = [dest_layout(d, i) for d, i in zip([dw_bra, dw_brs, dw_out, dw_ffi, dw_ffo], early)]
    (du_in, dv_in), (d_sng, d_snb, d_ws, d_bs), sib_early = _rowwise(
        sgu_bwd_fn, [dsgu, u_in, v_in], [sgu_norm_g, sgu_norm_b, ws3, bs_t], [(SW, BF16), (SW, BF16)],
        [(1, SW), (1, SW), (G, CH, CH), (CH, LANES)], name="sgu_bwd", tm=CH, carry=_PairExchange(g4_early, "halves"))
    pair_early = pair_sums(early, g4_early, sib_early)
    dq_att, dk_att, dv_att, xchg_early = _attn_bwd(q_att, k_att, v_att, attn_o, lse, dattn, heads=H,
                                                   carry=_ChipExchange(pair_early, gather=False))

    def q_post_bwd_fn(dq, t, cos, slo, shi, g):
        outs, dg = [], jnp.zeros((1, HEAD_PAD), F32)
        for hd in range(H):
            sl = slice(hd * HEAD_PAD, (hd + 1) * HEAD_PAD)
            th = t[:, sl]
            r = _rms_stats(th, QK)
            tn = th * r
            dn = _rope_t(dq[:, sl] * scale, cos, slo, shi)
            dg = dg + _colsum(dn * tn)
            outs.append(_rms_bwd(dn * g, tn, r, QK))
        return (jnp.concatenate(outs, axis=1),), (dg,)

    (dq_raw,), (d_gq,) = _rowwise(q_post_bwd_fn, [dq_att, q_raw, cos_t, slo_t, shi_t], [gq_p],
                                  [(H * HEAD_PAD, BF16)], [(1, HEAD_PAD)], name="q_post_bwd")

    def k_post_bwd_fn(dk, dv, t, kvi, cos, slo, shi, g):
        kr = kvi[:, KVL:]
        outs, dg, dkr = [], jnp.zeros((1, HEAD_PAD), F32), jnp.zeros_like(kr)
        for hd in range(H):
            th = jnp.concatenate([t[:, hd * HEAD_PAD:hd * HEAD_PAD + NOPE], kr], axis=1)
            r = _rms_stats(th, QK)
            tn = th * r
            dn = _rope_t(dk[:, hd * HEAD_PAD:(hd + 1) * HEAD_PAD], cos, slo, shi)
            dg = dg + _colsum(dn * tn)
            dt = _rms_bwd(dn * g, tn, r, QK)
            dkr = dkr + dt[:, NOPE:]
            outs += [dt[:, :NOPE], dv[:, hd * VD:(hd + 1) * VD]]
        return (jnp.concatenate(outs, axis=1), dkr), (dg,)

    def reduced_halves(idx, xchg, pair):
        return [_sum_chips(t4, pr, chip1, name="rs_sum_" + tags[i]) for t4, pr, i in zip(xchg, pair, idx)]

    red_early = reduced_halves(early, xchg_early, pair_early)
    (dkv_raw, dkrope), (d_gk,), other_early = _rowwise(
        k_post_bwd_fn, [dk_att, dv_att, kv_raw, kvin, cos_k, slo_k, shi_k], [gk_p],
        [(H * HEAD_PAD, BF16), (2 * ROPE_DIM, F32)], [(1, HEAD_PAD)], name="k_post_bwd",
        carry=_PairExchange(red_early, "gather"))

    def q_norm_bwd_epi(dn, t, g):
        r = _rms_stats(t, QL)
        tn = t * r
        return _rms_bwd(dn * g, tn, r, QL), _colsum(dn * tn)

    dqc, d_qng_part = _mm([(dq_raw, w_uq_p)], tb=True, name="q_up_dx", outs=(BF16,), tn=QL, col_sums=1,
                          extras=[(qc, "mn"), (q_norm_g, "n")], epi=q_norm_bwd_epi)
    _, (d_qng,) = _rowwise(lambda a: ((), (_colsum(a),)), [d_qng_part[:, 0, :]], [], [], [(1, QL)],
                           name="q_norm_fold", tm=d_qng_part.shape[0])
    dw_uq_p = _mm([(qn, dq_raw)], ta=True, name="q_up_dw", outs=(BF16,))
    dkvn = _mm([(dkv_raw, w_ukv_f)], tb=True, name="kv_up_dx", outs=(F32,), tm=kv_rows)
    dw_ukv = _mm([(kvn, dkv_raw)], ta=True, name="kv_up_dw", outs=(BF16,), tk=kv_rows)

    def kv_norm_bwd_fn(dn, dkr, t, g):
        kvc = t[:, :KVL]
        r = _rms_stats(kvc, KVL)
        tn = kvc * r
        return (jnp.concatenate([_rms_bwd(dn * g, tn, r, KVL), dkr], axis=1),), (_colsum(dn * tn),)

    (dkvin,), (d_kvng,) = _rowwise(kv_norm_bwd_fn, [dkvn, dkrope, kvin], [kv_norm_g], [(KVP, BF16)], [(1, KVL)],
                                   name="kv_norm_bwd")
    dkvin_x, dkvin_c = dkvin[:N], dkvin[N:]

    dctx_h = _mm([(dkvin_c, w_kv)], tb=True, name="proj_kv_ctx_dx", outs=(F32,))
    dw_q = _mm([(h, dqc)], ta=True, name="proj_q_dw", outs=(BF16,))
    dw_kv = _mm([(h, dkvin_x), (ctx_h, dkvin_c)], ta=True, name="proj_kv_dw", outs=(BF16,))
    dw_u, dw_v = _mm([(h, du_in, dv_in)], ta=True, name="proj_uv_dw", outs=(BF16, BF16), epi=both)
    dw_g1, dw_g2 = _mm([(h, dgi1, dgi2)], ta=True, name="proj_gates_dw", outs=(BF16, BF16), epi=both)

    dw_in_f = jnp.concatenate([dw_q, dw_kv[:, :KVL + ROPE_DIM], dw_u, dw_v, dw_g1, dw_g2], axis=1)
    dw_uq_f = dw_uq_p.reshape(QL, H, HEAD_PAD)[:, :, :QK].reshape(QL, H * QK)
    late = [0, 1, 2]
    g4_late = [dest_layout(d, i) for d, i in zip([dw_in_f, dw_uq_f, dw_ukv], late)]
    pair_late = pair_sums(late, g4_late, _exchange_alone(_PairExchange(g4_late, "halves"), name="rs_pair_late"))
    dh, xchg_late = _mm([(dqc, w_q), (dkvin_x, w_kv), (du_in, w_u), (dv_in, w_v), (dgi1, w_g1), (dgi2, w_g2)],
                        tb=True, name="proj_dx", outs=(F32,), tn=1024, tk=512,
                        carry=_ChipExchange(pair_late, gather=False))

    def norm1_bwd_fn(dhv, t, dres, g, sc):
        r = _rms_stats(t, D)
        tn = t * r
        dxg = dhv * (1.0 + sc)
        return (dres + _rms_bwd(dxg * g, tn, r, D),), (_colsum(dhv), _colsum(dhv * (tn * g)), _colsum(dxg * tn))

    (grad_x,), (dsh1, dsc1, dn1g_x) = _rowwise(norm1_bwd_fn, [dh, x2, dx1], [norm1_g, sc1], [(D, F32)], [(1, D)] * 3,
                                               name="norm1_bwd")
    _, (dsh1c, dsc1c, dn1g_c) = _rowwise(norm1_bwd_fn, [dctx_h, ctx2, jnp.zeros_like(ctx2)], [norm1_g, sc1c],
                                         [(D, F32)], [(1, D)] * 3, name="norm1_ctx_bwd")

    small = [dsh1, dsc1, dg1, dsh2, dsc2, dg2,
             dsh1c, dsc1c, dn1g_x, dn1g_c, d_qng, d_kvng, d_gq, d_gk, d_sng, d_snb, dn2g, loss_cols]
    small_sizes = [a.shape[1] for a in small]
    sm_row = jnp.concatenate(small, axis=1)
    sm_mat = jnp.concatenate([d_ws.reshape(G * CH, CH), d_bs], axis=0).astype(BF16)
    row_all, mat_all = _all_gather8([sm_row, sm_mat], name="ag_small", in_vmem=True)
    row_sum = _sum_blocks(row_all, name="sum_small_rows", out_dtype=F32)
    mat_sum = _sum_blocks(mat_all, name="sum_small_mats", out_dtype=F32)
    dmod_rows = row_all[:, 0, :NMOD]
    (_, _, _, _, _, _, t_sh1c, t_sc1c, t_n1x, t_n1c, g_qng, g_kvng, t_gq, t_gk, g_sng, g_snb, g_n2g,
     t_loss) = _split_lanes(row_sum, small_sizes)
    g_ws, t_bs = mat_sum[:G * CH], mat_sum[G * CH:]
    dmodc_row = jnp.concatenate([t_sh1c, t_sc1c, jnp.zeros((1, NMOD - 2 * D), F32)], axis=1)
    dmod16 = jnp.concatenate([dmod_rows, dmodc_row, jnp.zeros((BF16_SUBLANES - 9, NMOD), F32)], axis=0)

    def small_fn(rows, n1x, n1c, lossv):
        return (), (_colsum(rows), n1x + n1c, jnp.sum(lossv, axis=1, keepdims=True))

    _, (g_bmod, g_n1g, loss11) = _rowwise(small_fn, [dmod16], [t_n1x, t_n1c, t_loss], [], [(1, NMOD), (1, D), (1, 1)],
                                          name="small_reduce", tm=16)
    dmod_loc = lax.dynamic_slice_in_dim(dmod16, my_chip * NM, NM, axis=1)
    g_wmod = _mm([(silu_c, dmod_loc)], ta=True, name="mod_dw", outs=(F32,))
    dsilu_part = _mm([(dmod_loc, wm)], tb=True, name="mod_dx", outs=(F32,))
    part_all = _all_gather8([dsilu_part[8:9]], name="ag_cctx", in_vmem=True)[0]

    def cctx_fn(parts, dsl):
        return (), ((parts[0:1] + parts[2:3] + parts[4:5] + parts[6:7]) * dsl,)

    _, (g_cctx,) = _rowwise(cctx_fn, [part_all[:, 0, :]], [dsilu_c[8:9]], [], [(1, D)], name="cctx_grad", tm=8)

    red_late = reduced_halves(late, xchg_late, pair_late)
    other_late = _exchange_alone(_PairExchange(red_late, "gather"), name="rs_halves_late")
    grad_halves = dict(zip(tags, zip(red_late + red_early, other_late + other_early)))

    mod_upd = _adamw(w_mod[0], g_wmod, m_w_mod[0], v_w_mod[0], name="adamw_w_mod")
    grads = dict(
        c_ctx=g_cctx.reshape(D), w_mod=g_wmod[None], b_mod=g_bmod, norm1_g=g_n1g,
        q_norm_g=g_qng, kv_norm_g=g_kvng, qk_norm_q=t_gq[:, :QK], qk_norm_k=t_gk[:, :QK], sgu_norm_g=g_sng,
        sgu_norm_b=g_snb, w_spatial=g_ws.reshape(w_spatial.shape), b_spatial=t_bs[:, :G].T[None], norm2_g=g_n2g)
    weights = dict(c_ctx=c_ctx, w_mod=w_mod, b_mod=b_mod, norm1_g=norm1_g, w_in=w_in, q_norm_g=q_norm_g,
                   kv_norm_g=kv_norm_g, w_uq=w_uq, w_ukv=w_ukv, qk_norm_q=qk_norm_q, qk_norm_k=qk_norm_k,
                   sgu_norm_g=sgu_norm_g, sgu_norm_b=sgu_norm_b, w_spatial=w_spatial, b_spatial=b_spatial,
                   w_br_attn=w_br_attn, w_br_sgu=w_br_sgu, w_out=w_out, norm2_g=norm2_g, w_ffn_in=w_ffn_in,
                   w_ffn_out=w_ffn_out)
    m_in = dict(c_ctx=m_c_ctx, w_mod=m_w_mod, b_mod=m_b_mod, norm1_g=m_norm1_g, w_in=m_w_in, q_norm_g=m_q_norm_g,
                kv_norm_g=m_kv_norm_g, w_uq=m_w_uq, w_ukv=m_w_ukv, qk_norm_q=m_qk_norm_q, qk_norm_k=m_qk_norm_k,
                sgu_norm_g=m_sgu_norm_g, sgu_norm_b=m_sgu_norm_b, w_spatial=m_w_spatial, b_spatial=m_b_spatial,
                w_br_attn=m_w_br_attn, w_br_sgu=m_w_br_sgu, w_out=m_w_out, norm2_g=m_norm2_g, w_ffn_in=m_w_ffn_in,
                w_ffn_out=m_w_ffn_out)
    v_in_ = dict(c_ctx=v_c_ctx, w_mod=v_w_mod, b_mod=v_b_mod, norm1_g=v_norm1_g, w_in=v_w_in, q_norm_g=v_q_norm_g,
                 kv_norm_g=v_kv_norm_g, w_uq=v_w_uq, w_ukv=v_w_ukv, qk_norm_q=v_qk_norm_q, qk_norm_k=v_qk_norm_k,
                 sgu_norm_g=v_sgu_norm_g, sgu_norm_b=v_sgu_norm_b, w_spatial=v_w_spatial, b_spatial=v_b_spatial,
                 w_br_attn=v_w_br_attn, w_br_sgu=v_w_br_sgu, w_out=v_w_out, norm2_g=v_norm2_g, w_ffn_in=v_w_ffn_in,
                 w_ffn_out=v_w_ffn_out)
    names = list(weights)
    big_names = ("w_mod", "w_in", "w_uq", "w_ukv", "w_br_attn", "w_br_sgu", "w_out", "w_ffn_in", "w_ffn_out")
    out_g, out_d, out_m, out_v = {}, {}, {}, {}
    out_g["w_mod"] = grads["w_mod"]
    out_d["w_mod"], out_m["w_mod"], out_v["w_mod"] = [t[None] for t in mod_upd[:3]]
    for nm in big_names[1:]:
        res = _adamw_halves(weights[nm][0], *grad_halves[nm], m_in[nm][0], v_in_[nm][0], core, name="adamw_" + nm)
        out_g[nm], out_d[nm], out_m[nm], out_v[nm] = [t[None] for t in res]
    row_names = [nm for nm in names if nm not in big_names and nm not in ("w_spatial", "b_spatial")]
    widths = [-(-weights[nm].size // LANES) * LANES for nm in row_names]

    def as_row(d):
        return jnp.concatenate([jnp.pad(d[nm].reshape(1, -1), ((0, 0), (0, wd - d[nm].size)))
                                for nm, wd in zip(row_names, widths)], axis=1)

    def as_mat(d):
        return jnp.concatenate([d["w_spatial"].reshape(G * CH, CH), d["b_spatial"].reshape(G, CH)], axis=0)

    row_res = _adamw(as_row(weights), as_row(grads), as_row(m_in), as_row(v_in_), name="adamw_rows")
    mat_res = _adamw(as_mat(weights), as_mat(grads), as_mat(m_in), as_mat(v_in_), name="adamw_spatial")
    for tgt, row, mat in zip((out_d, out_m, out_v), row_res, mat_res):
        for nm, seg in zip(row_names, _split_lanes(row, widths)):
            tgt[nm] = seg[:, :weights[nm].size].reshape(weights[nm].shape)
        tgt["w_spatial"] = mat[:G * CH].reshape(w_spatial.shape)
        tgt["b_spatial"] = mat[G * CH:].reshape(b_spatial.shape)
    for nm in row_names + ["w_spatial", "b_spatial"]:
        out_g[nm] = grads[nm].reshape(weights[nm].shape)

    loss = loss11.reshape(())
    return (loss, grad_x[None], *[out_g[n] for n in names], *[out_d[n] for n in names],
            *[out_m[n] for n in names], *[out_v[n] for n in names])
```

```python
import math

import jax
import jax.numpy as jnp
from jax import lax
from jax.experimental import pallas as pl
from jax.experimental.pallas import tpu as pltpu

F32, BF16 = jnp.float32, jnp.bfloat16
MESH = pl.DeviceIdType.MESH

LANES = 128
F32_SUBLANES = 8
BF16_SUBLANES = 16
MXU_DIM = 256
VMEM_LIMIT_BYTES = 56 * 1024 * 1024

EPS = 1e-6
ROPE_DIM = 64
ROPE_THETA = 10000.0
GRID_W = 64
HEAD_PAD = 256
ADAM_LR, ADAM_B1, ADAM_B2, ADAM_EPS, ADAM_WD, ADAM_STEP = 0.001, 0.9, 0.999, 1e-08, 0.01, 10


def _tile(dim, pref, align=LANES):
    if dim <= pref:
        return dim
    t = (pref // align) * align
    while t >= align:
        if dim % t == 0:
            return t
        t -= align
    return dim


def _params(sem=None):
    return pltpu.CompilerParams(dimension_semantics=sem, vmem_limit_bytes=VMEM_LIMIT_BYTES)


def _sds(shape, dtype):
    return jax.ShapeDtypeStruct(tuple(shape), dtype)


def _mm(pairs, *, name, ta=False, tb=False, outs=(F32,), tm=1024, tn=1024, tk=2048, extras=(), epi=None,
        split=None, into=None, carry=None, col_sums=0):
    dual = len(pairs[0]) == 3
    a0, b0 = pairs[0][0], pairs[0][1]
    M = a0.shape[1] if ta else a0.shape[0]
    N = b0.shape[0] if tb else b0.shape[1]
    tm, tn = _tile(M, tm), _tile(N if split is None else split, tn)
    ks = [(p[0].shape[0] if ta else p[0].shape[1]) for p in pairs]
    tks = [_tile(k, tk) for k in ks]
    nks = [k // t for k, t in zip(ks, tks)]
    offs = [sum(nks[:i]) for i in range(len(pairs))]
    nk_total = sum(nks)
    single = len(pairs) == 1

    def kidx(kk, p):
        return kk if single else jnp.clip(kk - offs[p], 0, nks[p] - 1)

    in_specs, operands = [], []
    for p, pr in enumerate(pairs):
        if ta:
            in_specs.append(pl.BlockSpec((tks[p], tm), lambda i, j, kk, p=p: (kidx(kk, p), i)))
        else:
            in_specs.append(pl.BlockSpec((tm, tks[p]), lambda i, j, kk, p=p: (i, kidx(kk, p))))
        operands.append(pr[0])
        for b in pr[1:]:
            if tb:
                in_specs.append(pl.BlockSpec((tn, tks[p]), lambda i, j, kk, p=p: (j, kidx(kk, p))))
            else:
                in_specs.append(pl.BlockSpec((tks[p], tn), lambda i, j, kk, p=p: (kidx(kk, p), j)))
            operands.append(b)
    for arr, kind in extras:
        if kind == "mn":
            in_specs.append(pl.BlockSpec((tm, tn), lambda i, j, kk: (i, j)))
        else:
            in_specs.append(pl.BlockSpec((1, tn), lambda i, j, kk: (0, j)))
        operands.append(arr)
    n_in = len(operands)
    n_ex = len(extras)
    per = 3 if dual else 2
    dims = (((0 if ta else 1,), (1 if tb else 0,)), ((), ()))

    n_acc = 2 if dual else 1

    def products(ins, p):
        a = ins[per * p][...].astype(BF16)
        return [lax.dot_general(a, ins[per * p + 1 + q][...].astype(BF16), dims, preferred_element_type=F32)
                for q in range(n_acc)]

    def finish(ins, out_refs, acc_vals):
        vals = acc_vals + [r[...] for r in ins[n_in - n_ex:]]
        res = epi(*vals) if epi is not None else (vals[0],)
        for o, r in zip(out_refs, res):
            o[...] = jnp.broadcast_to(r, o.shape).astype(o.dtype)

    out_specs = [pl.BlockSpec((tm, tn), lambda i, j, kk: (i, j)) for _ in outs]
    out_specs += [pl.BlockSpec((None, F32_SUBLANES, tn), lambda i, j, kk: (i, 0, j)) for _ in range(col_sums)]
    out_shape = [_sds((M, N), d) for d in outs] + [_sds((M // tm, F32_SUBLANES, N), F32) for _ in range(col_sums)]
    aliases = {}
    n_alias = 0
    if split is not None:
        nps = split // tn
        lead = 0 if into is None else into[1]
        out_specs = [pl.BlockSpec((None, tm, tn), lambda i, j, kk: (j // nps + lead, i, j % nps))]
        out_shape = [_sds((N // split if into is None else into[0].shape[0], M, split), outs[0])]
        if into is not None:
            in_specs.append(pl.BlockSpec(memory_space=pl.ANY))
            operands.append(into[0])
            aliases, n_alias = {n_in: 0}, 1

    grid = (M // tm, N // tn, nk_total)
    n_out = len(outs) + col_sums

    def at_step(first):
        ids = [pl.program_id(d) for d in range(3)]
        cond = None
        for d, g in zip(ids, grid):
            t = d == (0 if first else g - 1)
            cond = t if cond is None else cond & t
        return cond

    def body(*refs):
        ins, out_refs, accs, start, wait = _split_refs(refs, n_in + n_alias, n_out, carry)
        ins = ins[:n_in]
        if carry is not None:
            pl.when(at_step(True))(start)
        if nk_total == 1:
            finish(ins, out_refs, products(ins, 0))
        else:
            kk = pl.program_id(2)

            @pl.when(kk == 0)
            def _():
                for acc, v in zip(accs, products(ins, 0)):
                    acc[...] = v

            for p in range(len(pairs)):
                lo = max(offs[p], 1)

                @pl.when((kk >= lo) & (kk < offs[p] + nks[p]))
                def _(p=p):
                    for acc, v in zip(accs, products(ins, p)):
                        acc[...] += v

            @pl.when(kk == nk_total - 1)
            def _():
                finish(ins, out_refs, [acc[...] for acc in accs])
        if carry is not None:
            pl.when(at_step(False))(wait)

    ex = carry
    res = pl.pallas_call(
        body, name=name, grid=grid, in_specs=in_specs + ([] if ex is None else ex.in_specs),
        out_specs=out_specs + ([] if ex is None else ex.out_specs),
        out_shape=out_shape + ([] if ex is None else ex.out_shape), input_output_aliases=aliases,
        scratch_shapes=[pltpu.VMEM((tm, tn), F32) for _ in range(n_acc if nk_total > 1 else 0)]
        + ([] if ex is None else ex.scratch),
        compiler_params=_params(("arbitrary",) * 3 if ex is not None else ("parallel", "parallel", "arbitrary")),
    )(*operands, *([] if ex is None else ex.xs))
    if ex is not None:
        return (res[0] if n_out == 1 else res[:n_out]), list(res[n_out:])
    return res[0] if n_out == 1 else res


def _rowwise(fn, rows, vecs, out_rows, out_accs=(), *, name, tm=256, tc=None, carry=None):
    M = rows[0].shape[0]
    tm = _tile(M, tm, BF16_SUBLANES)
    nrow = M // tm
    C = rows[0].shape[1]
    ncol = 1 if tc is None else C // _tile(C, tc)
    tcol = None if tc is None else _tile(C, tc)

    def colwise(shape):
        return tc is not None and len(shape) == 2 and shape[0] == 1 and shape[1] == C

    def vspec(shape):
        if colwise(shape):
            return pl.BlockSpec((1, tcol), lambda j, i: (0, j))
        return pl.BlockSpec(tuple(shape), lambda j, i, n=len(shape): (0,) * n)

    def rspec(width):
        if tc is None:
            return pl.BlockSpec((tm, width), lambda j, i: (i, 0))
        return pl.BlockSpec((tm, tcol), lambda j, i: (i, j))

    in_specs = [rspec(r.shape[1]) for r in rows] + [vspec(v.shape) for v in vecs]
    out_specs = [rspec(c) for c, _ in out_rows] + [vspec(s) for s in out_accs]
    out_shape = [_sds((M, c), d) for c, d in out_rows] + [_sds(s, F32) for s in out_accs]
    n_in, n_or = len(rows) + len(vecs), len(out_rows)

    n_out = n_or + len(out_accs)
    ex = carry

    def body(*refs):
        ins, outs, _, start, wait = _split_refs(refs, n_in, n_out, ex)
        o_rows, o_accs = outs[:n_or], outs[n_or:]
        if ex is not None:
            pl.when((pl.program_id(0) == 0) & (pl.program_id(1) == 0))(start)
        r_out, a_out = fn(*[r[...] for r in ins])
        for o, r in zip(o_rows, r_out):
            o[...] = r.astype(o.dtype)
        i = pl.program_id(1)

        @pl.when(i == 0)
        def _():
            for o, a in zip(o_accs, a_out):
                o[...] = a

        @pl.when(i > 0)
        def _():
            for o, a in zip(o_accs, a_out):
                o[...] += a

        if ex is not None:
            pl.when((pl.program_id(0) == ncol - 1) & (pl.program_id(1) == nrow - 1))(wait)

    res = pl.pallas_call(
        body, name=name, grid=(ncol, nrow), in_specs=in_specs + ([] if ex is None else ex.in_specs),
        out_specs=out_specs + ([] if ex is None else ex.out_specs),
        out_shape=out_shape + ([] if ex is None else ex.out_shape),
        scratch_shapes=[] if ex is None else ex.scratch,
        compiler_params=_params(("arbitrary", "arbitrary") if ex is not None else ("parallel", "arbitrary")),
    )(*rows, *vecs, *([] if ex is None else ex.xs))
    if ex is not None:
        return res[:n_or], res[n_or:n_out], list(res[n_out:])
    return res[:n_or], res[n_or:]


def _colsum(t):
    return jnp.sum(t, axis=0, keepdims=True)


def _gelu(t):
    return 0.5 * t * (1.0 + lax.erf(t * math.sqrt(0.5)))


def _gelu_grad(t):
    return 0.5 * (1.0 + lax.erf(t * math.sqrt(0.5))) + t * jnp.exp(-0.5 * t * t) * (1.0 / math.sqrt(2.0 * math.pi))


def _sigmoid(t):
    return 1.0 / (1.0 + jnp.exp(-t))


def _rms_stats(t, width):
    return lax.rsqrt(jnp.sum(t * t, axis=-1, keepdims=True) * (1.0 / width) + EPS)


def _rms_bwd(dn, tn, r, width):
    return r * (dn - tn * (jnp.sum(dn * tn, axis=-1, keepdims=True) * (1.0 / width)))


def _place():
    return lax.axis_index("x"), lax.axis_index("y"), lax.axis_index("c")


class _ChipExchange:
    def __init__(self, xs, gather):
        self.xs, self.gather, self.n = list(xs), gather, len(xs)
        self.in_specs = [pl.BlockSpec(memory_space=pl.ANY)] * self.n
        self.out_specs = [pl.BlockSpec(memory_space=pl.ANY)] * self.n
        self.out_shape = [_sds((4,) + (x.shape if gather else x.shape[1:]), x.dtype) for x in self.xs]
        self.scratch = [pltpu.SemaphoreType.DMA((self.n, 3)), pltpu.SemaphoreType.DMA((self.n, 3))]

    def bind(self, x_refs, out_refs, send_sems, recv_sems):
        x, y, c = _place()
        p = 2 * x + y
        chips = [(1 - x, y), (x, 1 - y), (1 - x, 1 - y)]

        def copy(w, k, outgoing):
            qx, qy = chips[k]
            there = 2 * qx + qy
            if self.gather:
                src = x_refs[w]
            else:
                src = x_refs[w].at[there if outgoing else p]
            return pltpu.make_async_remote_copy(
                src_ref=src, dst_ref=out_refs[w].at[p if outgoing else there], send_sem=send_sems.at[w, k],
                recv_sem=recv_sems.at[w, k], device_id=(qx, qy, c), device_id_type=MESH)

        def start():
            for w in range(self.n):
                for k in range(3):
                    copy(w, k, True).start()

        def wait():
            for w in range(self.n):
                for k in range(3):
                    copy(w, k, False).wait_recv()
            for w in range(self.n):
                for k in range(3):
                    copy(w, k, True).wait_send()

        return start, wait


class _RelayGather:
    def __init__(self, xs):
        self.xs, self.n = list(xs), len(xs)
        self.in_specs = [pl.BlockSpec(memory_space=pl.ANY)] * self.n
        self.out_specs = [pl.BlockSpec(memory_space=pl.ANY)] * self.n
        self.out_shape = [_sds((4,) + x.shape, x.dtype) for x in self.xs]
        self.scratch = [pltpu.SemaphoreType.DMA((self.n, 4)), pltpu.SemaphoreType.DMA((self.n, 4))]
        self.relay = None

    def bind(self, x_refs, out_refs, send_sems, recv_sems):
        x, y, c = _place()
        own, xn, yn, dg = (x, y), (1 - x, y), (x, 1 - y), (1 - x, 1 - y)

        def slot(w, chip):
            return out_refs[w].at[2 * chip[0] + chip[1]]

        def cp(w, k, src, dst, chip):
            return pltpu.make_async_remote_copy(src_ref=src, dst_ref=dst, send_sem=send_sems.at[w, k],
                                                recv_sem=recv_sems.at[w, k], device_id=(*chip, c), device_id_type=MESH)

        def halves(w):
            h = x_refs[w].shape[0] // 2
            return pl.ds(0, h), pl.ds(h, h)

        def start():
            for w in range(self.n):
                cp(w, 0, x_refs[w], slot(w, own), xn).start()
                cp(w, 1, x_refs[w], slot(w, own), yn).start()

        def relay():
            for w in range(self.n):
                lo, hi = halves(w)
                cp(w, 1, x_refs[w], slot(w, yn), yn).wait_recv()
                cp(w, 2, slot(w, yn).at[lo], slot(w, yn).at[lo], xn).start()
                cp(w, 0, x_refs[w], slot(w, xn), xn).wait_recv()
                cp(w, 3, slot(w, xn).at[hi], slot(w, xn).at[hi], yn).start()

        def wait():
            for w in range(self.n):
                lo, hi = halves(w)
                cp(w, 2, slot(w, dg).at[lo], slot(w, dg).at[lo], xn).wait_recv()
                cp(w, 3, slot(w, dg).at[hi], slot(w, dg).at[hi], yn).wait_recv()
                cp(w, 0, x_refs[w], slot(w, own), xn).wait_send()
                cp(w, 1, x_refs[w], slot(w, own), yn).wait_send()
                cp(w, 2, slot(w, yn).at[lo], slot(w, yn).at[lo], xn).wait_send()
                cp(w, 3, slot(w, xn).at[hi], slot(w, xn).at[hi], yn).wait_send()

        self.relay = relay
        return start, wait


class _PairExchange:
    def __init__(self, xs, mode):
        self.xs, self.mode, self.n = list(xs), mode, len(xs)
        self.in_specs = [pl.BlockSpec(memory_space=pl.ANY)] * self.n
        self.out_specs = [pl.BlockSpec(memory_space=pl.ANY)] * self.n
        shape = {"halves": lambda s: (4,) + s[2:], "forward": lambda s: s, "gather": lambda s: (2,) + s}[mode]
        self.out_shape = [_sds(shape(x.shape), x.dtype) for x in self.xs]
        self.scratch = [pltpu.SemaphoreType.DMA((self.n, 3)), pltpu.SemaphoreType.DMA((self.n, 3))]

    def bind(self, x_refs, out_refs, send_sems, recv_sems):
        x, y, c = _place()
        chips = [(1 - x, y), (x, 1 - y), (1 - x, 1 - y)]

        def copy(w, src, dst, k):
            return pltpu.make_async_remote_copy(src_ref=src, dst_ref=dst, send_sem=send_sems.at[w, k],
                                                recv_sem=recv_sems.at[w, k], device_id=(x, y, 1 - c),
                                                device_id_type=MESH)

        def start():
            for w, (xr, orf) in enumerate(zip(x_refs, out_refs)):
                if self.mode == "halves":
                    for q in range(4):
                        copy(w, xr.at[q, 1 - c], orf.at[q], 0).start()
                elif self.mode == "forward":
                    for k, (qx, qy) in enumerate(chips):
                        copy(w, xr.at[2 * qx + qy], orf.at[2 * qx + qy], k).start()
                else:
                    copy(w, xr, orf.at[c], 0).start()

        def wait():
            for w, (xr, orf) in enumerate(zip(x_refs, out_refs)):
                if self.mode == "halves":
                    copy(w, orf, orf, 0).wait()
                elif self.mode == "forward":
                    for k, (qx, qy) in enumerate(chips):
                        copy(w, xr.at[2 * qx + qy], orf.at[2 * qx + qy], k).wait()
                else:
                    cp = copy(w, xr, orf.at[1 - c], 0)
                    cp.wait_recv()
                    cp.wait_send()

        return start, wait


def _split_refs(refs, n_in, n_out, ex):
    ne = 0 if ex is None else ex.n
    ins, xin = refs[:n_in], refs[n_in:n_in + ne]
    outs, xout = refs[n_in + ne:n_in + ne + n_out], refs[n_in + ne + n_out:n_in + 2 * ne + n_out]
    rest = refs[n_in + 2 * ne + n_out:]
    if ex is None:
        return ins, outs, rest, None, None
    start, wait = ex.bind(xin, xout, rest[-2], rest[-1])
    return ins, outs, rest[:-2], start, wait


def _attn_fwd(q, k, v, *, heads, tq=1024, carry=None):
    N, M = q.shape[0], k.shape[0]
    tq = _tile(N, tq)
    sub = _tile(tq, MXU_DIM)
    vd = v.shape[1] // heads
    nq = N // tq

    def body(*refs):
        (q_ref, k_ref, v_ref), (o_ref, lse_ref), _, start, wait = _split_refs(refs, 3, 2, carry)
        if carry is not None:
            pl.when((pl.program_id(0) == 0) & (pl.program_id(1) == 0))(start)
            if getattr(carry, "relay", None) is not None:
                pl.when((pl.program_id(0) == (5 * heads) // 8) & (pl.program_id(1) == 0))(carry.relay)
        for sb in range(tq // sub):
            rows = pl.ds(sb * sub, sub)
            s = lax.dot_general(q_ref[rows, :], k_ref[...], (((1,), (1,)), ((), ())), preferred_element_type=F32)
            m = jnp.max(s, axis=-1, keepdims=True)
            p = jnp.exp(s - m)
            l = jnp.sum(p, axis=-1, keepdims=True)
            o = jnp.dot(p.astype(BF16), v_ref[...], preferred_element_type=F32) / l
            o_ref[rows, :] = o.astype(o_ref.dtype)
            lse_ref[rows, :] = jnp.broadcast_to(m + jnp.log(l), (sub, vd))
        if carry is not None:
            pl.when((pl.program_id(0) == heads - 1) & (pl.program_id(1) == nq - 1))(wait)

    ex = carry
    res = pl.pallas_call(
        body, name="attn_fwd", grid=(heads, nq),
        in_specs=[pl.BlockSpec((tq, HEAD_PAD), lambda h, i: (i, h)),
                  pl.BlockSpec((M, HEAD_PAD), lambda h, i: (0, h)),
                  pl.BlockSpec((M, vd), lambda h, i: (0, h))] + ([] if ex is None else ex.in_specs),
        out_specs=[pl.BlockSpec((tq, vd), lambda h, i: (i, h)),
                   pl.BlockSpec((tq, vd), lambda h, i: (i, h))] + ([] if ex is None else ex.out_specs),
        out_shape=[_sds((N, heads * vd), BF16), _sds((N, heads * vd), F32)] + ([] if ex is None else ex.out_shape),
        scratch_shapes=[] if ex is None else ex.scratch,
        compiler_params=_params(("arbitrary", "arbitrary")),
    )(q, k, v, *([] if ex is None else ex.xs))
    return res[0], res[1], list(res[2:])


def _attn_bwd(q, k, v, o, lse, do, *, heads, tq=512, carry=None):
    N, M = q.shape[0], k.shape[0]
    tq = _tile(N, tq)
    vd = v.shape[1] // heads
    nq = N // tq
    sub = _tile(tq, MXU_DIM)
    nt = (((1,), (1,)), ((), ()))
    tn = (((0,), (0,)), ((), ()))

    def body(*refs):
        (q_ref, k_ref, v_ref, o_ref, lse_ref, do_ref), (dq_ref, dk_ref, dv_ref), _, start, wait = _split_refs(
            refs, 6, 3, carry)
        if carry is not None:
            pl.when((pl.program_id(0) == 0) & (pl.program_id(1) == 0))(start)
        i = pl.program_id(1)
        kb, vb = k_ref[...], v_ref[...]
        parts = []
        for sb in range(tq // sub):
            rows = pl.ds(sb * sub, sub)
            qb, dob = q_ref[rows, :], do_ref[rows, :]
            delta = jnp.sum(dob.astype(F32) * o_ref[rows, :].astype(F32), axis=-1, keepdims=True)
            s = lax.dot_general(qb, kb, nt, preferred_element_type=F32)
            p = jnp.exp(s - lse_ref[rows, :][:, :1])
            dp = lax.dot_general(dob, vb, nt, preferred_element_type=F32)
            ds = (p * (dp - delta)).astype(BF16)
            dq_ref[rows, :] = jnp.dot(ds, kb, preferred_element_type=F32)
            parts.append((lax.dot_general(ds, qb, tn, preferred_element_type=F32),
                          lax.dot_general(p.astype(BF16), dob, tn, preferred_element_type=F32)))

        dk_step, dv_step = parts[0]
        for dk_part, dv_part in parts[1:]:
            dk_step, dv_step = dk_step + dk_part, dv_step + dv_part

        @pl.when(i == 0)
        def _():
            dk_ref[...] = dk_step
            dv_ref[...] = dv_step

        @pl.when(i > 0)
        def _():
            dk_ref[...] += dk_step
            dv_ref[...] += dv_step

        if carry is not None:
            pl.when((pl.program_id(0) == heads - 1) & (pl.program_id(1) == nq - 1))(wait)

    ex = carry
    res = pl.pallas_call(
        body, name="attn_bwd", grid=(heads, nq),
        in_specs=[pl.BlockSpec((tq, HEAD_PAD), lambda h, i: (i, h)),
                  pl.BlockSpec((M, HEAD_PAD), lambda h, i: (0, h)),
                  pl.BlockSpec((M, vd), lambda h, i: (0, h)),
                  pl.BlockSpec((tq, vd), lambda h, i: (i, h)),
                  pl.BlockSpec((tq, vd), lambda h, i: (i, h)),
                  pl.BlockSpec((tq, vd), lambda h, i: (i, h))] + ([] if ex is None else ex.in_specs),
        out_specs=[pl.BlockSpec((tq, HEAD_PAD), lambda h, i: (i, h)),
                   pl.BlockSpec((M, HEAD_PAD), lambda h, i: (0, h)),
                   pl.BlockSpec((M, vd), lambda h, i: (0, h))] + ([] if ex is None else ex.out_specs),
        out_shape=[_sds((N, heads * HEAD_PAD), F32), _sds((M, heads * HEAD_PAD), F32),
                   _sds((M, heads * vd), F32)] + ([] if ex is None else ex.out_shape),
        scratch_shapes=[] if ex is None else ex.scratch,
        compiler_params=_params(("arbitrary", "arbitrary")),
    )(q, k, v, o, lse, do, *([] if ex is None else ex.xs))
    return res[0], res[1], res[2], list(res[3:])


def _comm_call(body, xs, out_shapes, n_sems, name, in_vmem):
    space = pltpu.VMEM if in_vmem else pl.ANY
    n = len(xs)

    def wrapped(*refs):
        body(refs[:n], refs[n:2 * n], *refs[2 * n:])

    return pl.pallas_call(
        wrapped, name=name, out_shape=list(out_shapes),
        in_specs=[pl.BlockSpec(memory_space=space)] * n, out_specs=[pl.BlockSpec(memory_space=space)] * n,
        scratch_shapes=[pltpu.SemaphoreType.DMA((n, n_sems)), pltpu.SemaphoreType.DMA((n, n_sems)),
                        pltpu.SemaphoreType.DMA((n,))],
        compiler_params=pltpu.CompilerParams(vmem_limit_bytes=VMEM_LIMIT_BYTES),
    )(*xs)


def _all_gather8(blks, *, name, in_vmem):
    def body(x_refs, out_refs, send_sems, recv_sems, local_sems):
        x, y, c = _place()
        me, sibling = (x, y, c), (x, y, 1 - c)
        chips = [(1 - x, y), (x, 1 - y), (1 - x, 1 - y)]
        waits = []
        for w, (x_ref, out_ref) in enumerate(zip(x_refs, out_refs)):
            def slot(px, py, pc, out_ref=out_ref):
                return out_ref.at[4 * px + 2 * py + pc]

            def copy(k, block, to, src=None, w=w, slot=slot):
                return pltpu.make_async_remote_copy(
                    src_ref=slot(*block) if src is None else src, dst_ref=slot(*block),
                    send_sem=send_sems.at[w, k], recv_sem=recv_sems.at[w, k], device_id=to, device_id_type=MESH)

            mine = pltpu.make_async_copy(x_ref, slot(*me), local_sems.at[w])
            mine.start()
            first = [copy(0, me, sibling, src=x_ref)]
            first += [copy(1 + j, me, (*chip, c), src=x_ref) for j, chip in enumerate(chips)]
            for cp in first:
                cp.start()
            waits.append((copy, mine, first))
        for copy, mine, first in waits:
            passed = [copy(4 + j, (*chip, c), sibling) for j, chip in enumerate(chips)]
            for j, chip in enumerate(chips):
                copy(1 + j, (*chip, c), me).wait_recv()
                passed[j].start()
            copy(0, sibling, me).wait_recv()
            for j, chip in enumerate(chips):
                copy(4 + j, (*chip, 1 - c), me).wait_recv()
            for cp in first + passed:
                cp.wait_send()
            mine.wait()

    return _comm_call(body, blks, [_sds((8,) + b.shape, b.dtype) for b in blks], 7, name, in_vmem)


def _gather_others(blks, *, name):
    def body(x_refs, out_refs, send_sems, recv_sems, local_sems):
        x, y, c = _place()
        own, xn, yn, dg = (x, y), (1 - x, y), (x, 1 - y), (1 - x, 1 - y)

        def slot(w, chip, core):
            return out_refs[w].at[4 * chip[0] + 2 * chip[1] + core]

        def cp(w, k, src, dst, chip, core):
            return pltpu.make_async_remote_copy(src_ref=src, dst_ref=dst, send_sem=send_sems.at[w, k],
                                                recv_sem=recv_sems.at[w, k], device_id=(*chip, core),
                                                device_id_type=MESH)

        def halves(w):
            h = x_refs[w].shape[0] // 2
            return pl.ds(0, h), pl.ds(h, h)

        sends = []
        for w, x_ref in enumerate(x_refs):
            sends += [cp(w, 0, x_ref, slot(w, own, c), xn, c), cp(w, 1, x_ref, slot(w, own, c), yn, c)]
        for s in sends:
            s.start()
        for w, x_ref in enumerate(x_refs):
            lo, hi = halves(w)
            cp(w, 1, x_ref, slot(w, yn, c), yn, c).wait_recv()
            passed = [cp(w, 2, slot(w, yn, c).at[lo], slot(w, yn, c).at[lo], xn, c),
                      cp(w, 4, slot(w, yn, c), slot(w, yn, c), own, 1 - c)]
            cp(w, 0, x_ref, slot(w, xn, c), xn, c).wait_recv()
            passed += [cp(w, 3, slot(w, xn, c).at[hi], slot(w, xn, c).at[hi], yn, c),
                       cp(w, 5, slot(w, xn, c), slot(w, xn, c), own, 1 - c)]
            for s in passed:
                s.start()
            sends += passed
        for w in range(len(x_refs)):
            lo, hi = halves(w)
            cp(w, 2, slot(w, dg, c).at[lo], slot(w, dg, c).at[lo], xn, c).wait_recv()
            cp(w, 3, slot(w, dg, c).at[hi], slot(w, dg, c).at[hi], yn, c).wait_recv()
            passed = [cp(w, 6, slot(w, dg, c), slot(w, dg, c), own, 1 - c)]
            passed[0].start()
            sends += passed
        for w in range(len(x_refs)):
            cp(w, 4, slot(w, yn, c), slot(w, yn, 1 - c), own, 1 - c).wait_recv()
            cp(w, 5, slot(w, xn, c), slot(w, xn, 1 - c), own, 1 - c).wait_recv()
            cp(w, 6, slot(w, dg, c), slot(w, dg, 1 - c), own, 1 - c).wait_recv()
        for s in sends:
            s.wait_send()

    return list(_comm_call(body, blks, [_sds((8,) + b.shape, b.dtype) for b in blks], 7, name, False))


def _exchange_alone(ex, *, name):
    def body(x_refs, out_refs, send_sems, recv_sems, local_sems):
        start, wait = ex.bind(x_refs, out_refs, send_sems, recv_sems)
        start()
        wait()

    return list(_comm_call(body, ex.xs, ex.out_shape, 3, name, False))


def _block_rows(rows, row_bytes, target=1 << 21, align=BF16_SUBLANES):
    return _tile(rows, max(align, target // row_bytes // align * align), align)


def _sum_blocks(buf, *, name, out_dtype):
    B, R, C = buf.shape
    tm = _block_rows(R, B * C * buf.dtype.itemsize)

    def body(x_ref, o_ref):
        acc = x_ref[0].astype(F32)
        for b in range(1, B):
            acc = acc + x_ref[b].astype(F32)
        o_ref[...] = acc.astype(o_ref.dtype)

    return pl.pallas_call(
        body, name=name, grid=(R // tm,), in_specs=[pl.BlockSpec((B, tm, C), lambda i: (0, i, 0))],
        out_specs=pl.BlockSpec((tm, C), lambda i: (i, 0)), out_shape=_sds((R, C), out_dtype),
        compiler_params=_params(("parallel",)),
    )(buf)


def _sum_chips(received, sent, chip, *, name):
    _, R, C = received.shape
    tm = _block_rows(R, 5 * C * received.dtype.itemsize)

    def body(chip_ref, r0, r1, r2, r3, own_ref, o_ref):
        acc = None
        for q, r in enumerate((r0, r1, r2, r3)):
            term = jnp.where(q == chip_ref[0], own_ref[...], r[...]).astype(F32)
            acc = term if acc is None else acc + term
        o_ref[...] = acc

    def slot(q):
        return pl.BlockSpec((None, tm, C), lambda i, ch, q=q: (jnp.where(q == ch[0], (q + 1) % 4, q), i, 0))

    return pl.pallas_call(
        body, name=name, out_shape=_sds((R, C), F32),
        grid_spec=pltpu.PrefetchScalarGridSpec(
            num_scalar_prefetch=1, grid=(R // tm,),
            in_specs=[slot(0), slot(1), slot(2), slot(3), pl.BlockSpec((None, tm, C), lambda i, ch: (ch[0], i, 0))],
            out_specs=pl.BlockSpec((tm, C), lambda i, ch: (i, 0))),
        compiler_params=_params(("arbitrary",)),
    )(chip, received, received, received, received, sent)


def _pair_add(mine, theirs, core, *, name):
    _, _, R, C = mine.shape
    tm = _block_rows(R, C * 2)

    def body(core_ref, a_ref, b_ref, o_ref):
        o_ref[...] = (a_ref[...].astype(F32) + b_ref[...].astype(F32)).astype(o_ref.dtype)

    return pl.pallas_call(
        body, name=name, out_shape=_sds(theirs.shape, BF16),
        grid_spec=pltpu.PrefetchScalarGridSpec(
            num_scalar_prefetch=1, grid=(4, R // tm),
            in_specs=[pl.BlockSpec((None, None, tm, C), lambda q, i, core_ref: (q, core_ref[0], i, 0)),
                      pl.BlockSpec((None, tm, C), lambda q, i, core_ref: (q, i, 0))],
            out_specs=pl.BlockSpec((None, tm, C), lambda q, i, core_ref: (q, i, 0))),
        compiler_params=_params(("parallel", "parallel")),
    )(core, mine, theirs)


def _assemble(gathered, own, chip, *, name, transpose):
    _, K, Ns = gathered.shape
    tm = _block_rows(K, Ns * 4)

    def body(chip_ref, g_ref, own_ref, o_ref):
        q = pl.program_id(0)

        @pl.when(q == chip_ref[0])
        def _():
            o_ref[...] = own_ref[...].astype(BF16)

        @pl.when(q != chip_ref[0])
        def _():
            o_ref[...] = g_ref[...]

    if transpose:
        out_spec = pl.BlockSpec((tm, Ns), lambda q, i, ch: (i, q))
        out_shape = _sds((K, 4 * Ns), BF16)
    else:
        out_spec = pl.BlockSpec((None, tm, Ns), lambda q, i, ch: (q, i, 0))
        out_shape = _sds((4, K, Ns), BF16)
    return pl.pallas_call(
        body, name=name, out_shape=out_shape,
        grid_spec=pltpu.PrefetchScalarGridSpec(
            num_scalar_prefetch=1, grid=(4, K // tm),
            in_specs=[pl.BlockSpec((None, tm, Ns), lambda q, i, ch: (jnp.where(q == ch[0], (q + 1) % 4, q), i, 0)),
                      pl.BlockSpec((tm, Ns), lambda q, i, ch: (jnp.where(q == ch[0], i, 0), 0))],
            out_specs=out_spec),
        compiler_params=_params(("arbitrary", "arbitrary")),
    )(chip, gathered, own)


def _assemble_halves(mine, theirs, own, place, *, name, transpose):
    _, K2, Ns = mine.shape
    tm = _block_rows(K2, Ns * 4, target=1 << 22)
    nb = K2 // tm

    def body(place_ref, m_ref, t_ref, own_ref, o_ref):
        q, hb = pl.program_id(0), pl.program_id(1)
        is_own = q == place_ref[0]
        is_mine = hb == place_ref[1]

        @pl.when(is_own)
        def _():
            o_ref[...] = own_ref[...].astype(BF16)

        @pl.when(jnp.logical_not(is_own) & is_mine)
        def _():
            o_ref[...] = m_ref[...]

        @pl.when(jnp.logical_not(is_own) & jnp.logical_not(is_mine))
        def _():
            o_ref[...] = t_ref[...]

    def other(q, pr):
        return jnp.where(q == pr[0], (q + 1) % 4, q)

    if transpose:
        out_spec = pl.BlockSpec((tm, Ns), lambda q, hb, i, pr: (hb * nb + i, q))
        out_shape = _sds((2 * K2, 4 * Ns), BF16)
    else:
        out_spec = pl.BlockSpec((None, tm, Ns), lambda q, hb, i, pr: (q, hb * nb + i, 0))
        out_shape = _sds((4, 2 * K2, Ns), BF16)
    return pl.pallas_call(
        body, name=name, out_shape=out_shape,
        grid_spec=pltpu.PrefetchScalarGridSpec(
            num_scalar_prefetch=1, grid=(4, 2, nb),
            in_specs=[pl.BlockSpec((None, tm, Ns), lambda q, hb, i, pr: (other(q, pr), jnp.where(hb == pr[1], i, 0), 0)),
                      pl.BlockSpec((None, tm, Ns), lambda q, hb, i, pr: (other(q, pr), jnp.where(hb == pr[1], 0, i), 0)),
                      pl.BlockSpec((tm, Ns), lambda q, hb, i, pr: (jnp.where(q == pr[0], hb * nb + i, 0), 0))],
            out_specs=out_spec),
        compiler_params=_params(("arbitrary",) * 3),
    )(place, mine, theirs, own)


def _split_lanes(row, widths):
    out, off = [], 0
    for wd in widths:
        out.append(row[:, off:off + wd])
        off += wd
    return out


def _adamw_math(w, g, m, v):
    m = ADAM_B1 * m + (1.0 - ADAM_B1) * g
    v = ADAM_B2 * v + (1.0 - ADAM_B2) * (g * g)
    m_hat = m / (1.0 - ADAM_B1 ** ADAM_STEP)
    v_hat = v / (1.0 - ADAM_B2 ** ADAM_STEP)
    delta = -ADAM_LR * (m_hat / (jnp.sqrt(v_hat) + ADAM_EPS) + ADAM_WD * w)
    return delta, m, v


def _adamw_halves(w, mine, other, m, v, core, *, name):
    K, Ns = w.shape
    tm = _block_rows(K // 2, Ns * 4, target=1 << 20, align=F32_SUBLANES)
    nb = (K // 2) // tm

    def body(core_ref, w_ref, mine_ref, other_ref, m_ref, v_ref, g_out, d_out, m_out, v_out):
        g = jnp.where(pl.program_id(0) // nb == core_ref[0], mine_ref[...], other_ref[...])
        g_out[...] = g
        d_out[...], m_out[...], v_out[...] = _adamw_math(w_ref[...], g, m_ref[...], v_ref[...])

    row = pl.BlockSpec((tm, Ns), lambda i, cr: (i, 0))
    return pl.pallas_call(
        body, name=name, out_shape=[_sds((K, Ns), F32)] * 4,
        grid_spec=pltpu.PrefetchScalarGridSpec(
            num_scalar_prefetch=1, grid=(K // tm,),
            in_specs=[row,
                      pl.BlockSpec((tm, Ns), lambda i, cr: (jnp.where(i // nb == cr[0], i % nb, 0), 0)),
                      pl.BlockSpec((None, tm, Ns), lambda i, cr: (1 - cr[0], jnp.where(i // nb == cr[0], 0, i % nb), 0)),
                      row, row],
            out_specs=[row, row, row, row]),
        compiler_params=_params(("arbitrary",)),
    )(core, w, mine, other, m, v)


def _adamw(w, g, m, v, *, name, carry=None):
    C = w.shape[1]

    def fn(w, g, m, v):
        return _adamw_math(w, g, m, v), ()

    tm = max(F32_SUBLANES, min(512, (1 << 20) // (4 * C) // F32_SUBLANES * F32_SUBLANES))
    res = _rowwise(fn, [w, g, m, v], [], [(C, F32)] * 3, name=name, tm=tm, carry=carry)
    return tuple(res[0]) + ((res[2],) if carry is not None else ())


def _rope_tables(n):
    rows = n // GRID_W
    row = jnp.repeat(jnp.arange(rows, dtype=F32), GRID_W)
    col = jnp.tile(jnp.arange(GRID_W, dtype=F32), rows)
    nf = ROPE_DIM // 4
    freqs = ROPE_THETA ** (-jnp.arange(nf, dtype=F32) / nf)
    ang_r, ang_c = row[:, None] * freqs[None, :], col[:, None] * freqs[None, :]
    cr, sr, cc, sc = jnp.cos(ang_r), jnp.sin(ang_r), jnp.cos(ang_c), jnp.sin(ang_c)
    nope = HEAD_PAD - 2 * ROPE_DIM
    one, zero, z = jnp.ones((n, nope), F32), jnp.zeros((n, nope), F32), jnp.zeros((n, nf), F32)
    pad = jnp.zeros((n, ROPE_DIM), F32)
    cos = jnp.concatenate([one, cr, cr, cc, cc, pad], axis=1)
    s_lo = jnp.concatenate([zero, -sr, z, -sc, z, pad], axis=1)
    s_hi = jnp.concatenate([zero, z, sr, z, sc, pad], axis=1)
    return cos, s_lo, s_hi


def _rope(n, cos, s_lo, s_hi):
    q = ROPE_DIM // 4
    return n * cos + pltpu.roll(n, HEAD_PAD - q, 1) * s_lo + pltpu.roll(n, q, 1) * s_hi


def _rope_t(d, cos, s_lo, s_hi):
    q = ROPE_DIM // 4
    return d * cos + pltpu.roll(d * s_lo, q, 1) + pltpu.roll(d * s_hi, HEAD_PAD - q, 1)


def kernel(x, c, ctx, c_ctx, w_mod, b_mod, norm1_g, w_in, q_norm_g, kv_norm_g, w_uq, w_ukv, qk_norm_q, qk_norm_k, sgu_norm_g, sgu_norm_b, w_spatial, b_spatial, w_br_attn, w_br_sgu, w_out, norm2_g, w_ffn_in, w_ffn_out, loss_target, m_c_ctx, m_w_mod, m_b_mod, m_norm1_g, m_w_in, m_q_norm_g, m_kv_norm_g, m_w_uq, m_w_ukv, m_qk_norm_q, m_qk_norm_k, m_sgu_norm_g, m_sgu_norm_b, m_w_spatial, m_b_spatial, m_w_br_attn, m_w_br_sgu, m_w_out, m_norm2_g, m_w_ffn_in, m_w_ffn_out, v_c_ctx, v_w_mod, v_b_mod, v_norm1_g, v_w_in, v_q_norm_g, v_kv_norm_g, v_w_uq, v_w_ukv, v_qk_norm_q, v_qk_norm_k, v_sgu_norm_g, v_sgu_norm_b, v_w_spatial, v_b_spatial, v_w_br_attn, v_w_br_sgu, v_w_out, v_norm2_g, v_w_ffn_in, v_w_ffn_out):
    ax, ay, ac = _place()
    my_chip = 2 * ax + ay
    my_dev = 4 * ax + 2 * ay + ac

    N, D = x.shape[1], x.shape[2]
    CT = ctx.shape[1]
    M = N + CT
    QL, KVL, QK = q_norm_g.shape[-1], kv_norm_g.shape[-1], qk_norm_q.shape[-1]
    NOPE = QK - ROPE_DIM
    VD = NOPE
    H = 4 * w_uq.shape[-1] // QK
    SW, G, CH = sgu_norm_g.shape[-1], w_spatial.shape[1], w_spatial.shape[2]
    GD = SW // G
    DFF = 4 * w_ffn_out.shape[1]
    NMOD = 4 * w_mod.shape[-1]
    NM = w_mod.shape[-1]
    KVP = KVL + 2 * ROPE_DIM
    assert NOPE == LANES and GD == LANES and HEAD_PAD == NOPE + 2 * ROPE_DIM and CH == LANES
    scale = QK ** -0.5

    x2, ctx2, tgt2 = x[0], ctx[0], loss_target[0]

    c_all = _all_gather8([c], name="ag_c", in_vmem=True)[0][:, 0, :]
    c_rows = jnp.concatenate([c_all, c_ctx[None, :], jnp.zeros((BF16_SUBLANES - 9, D), F32)], axis=0)

    def silu_fn(t):
        s = _sigmoid(t)
        return (t * s, s * (1.0 + t * (1.0 - s))), ()

    (silu_c, dsilu_c), _ = _rowwise(silu_fn, [c_rows], [], [(D, F32), (D, F32)], name="silu_c", tm=16)
    wm = w_mod[0]
    mod_loc = _mm([(silu_c, wm)], name="mod_fwd", outs=(F32,),
                  extras=[(lax.dynamic_slice_in_dim(b_mod, my_chip * NM, NM, axis=1), "n")],
                  epi=lambda acc, b: (acc + b,))
    mod_all = _all_gather8([mod_loc], name="ag_mod", in_vmem=True)[0]
    mod_full = jnp.concatenate([mod_all[0], mod_all[2], mod_all[4], mod_all[6]], axis=1)
    mod_me = lax.dynamic_slice_in_dim(mod_full, my_dev, 1, axis=0)
    sh1, sc1, g1, sh2, sc2, g2 = [mod_me[:, i * D:(i + 1) * D] for i in range(6)]
    sh1c, sc1c = mod_full[8:9, :D], mod_full[8:9, D:2 * D]

    big = [w_in[0], w_uq[0], w_ukv[0], w_br_attn[0], w_br_sgu[0], w_out[0], w_ffn_in[0], w_ffn_out[0]]
    col_sharded = [True, True, True, True, True, False, True, False]
    halves = [lax.dynamic_slice_in_dim(a, ac * (a.shape[0] // 2), a.shape[0] // 2, axis=0).astype(BF16) for a in big]
    tags = ["w_in", "w_uq", "w_ukv", "w_br_attn", "w_br_sgu", "w_out", "w_ffn_in", "w_ffn_out"]
    first_group, attn_group, ffn_group = [0, 1, 2], [3, 4, 5, 6], [7]
    chip1 = jnp.reshape(my_chip, (1,)).astype(jnp.int32)
    place2 = jnp.stack([my_chip, ac]).astype(jnp.int32)

    def laid_out(seg, i):
        a = big[i]
        if col_sharded[i] and seg.ndim == 3:
            return seg.transpose(1, 0, 2).reshape(a.shape[0], 4 * a.shape[1])
        return seg if col_sharded[i] else seg.reshape(4 * a.shape[0], a.shape[1])

    def side_by_side(i):
        return col_sharded[i] and big[i].shape[1] % LANES == 0

    def finish_gather(idx, mine4, theirs4):
        return [laid_out(_assemble_halves(m, t, big[i], place2, name="assemble_" + tags[i], transpose=side_by_side(i)), i)
                for i, m, t in zip(idx, mine4, theirs4)]

    gathered = _gather_others([halves[i] for i in first_group], name="ag_weights")
    w_in_f, w_uq_f, w_ukv_f = [
        laid_out(_assemble(seg.reshape((4,) + big[i].shape), big[i], chip1, name="assemble_" + tags[i],
                           transpose=side_by_side(i)), i) for i, seg in zip(first_group, gathered)]
    o_kv, o_u = QL, QL + KVL + ROPE_DIM
    o_v, o_g = o_u + SW, o_u + 2 * SW
    w_q = w_in_f[:, :QL]
    w_kv = jnp.pad(w_in_f[:, o_kv:o_u], ((0, 0), (0, ROPE_DIM)))
    w_u, w_v = w_in_f[:, o_u:o_v], w_in_f[:, o_v:o_g]
    w_g1, w_g2 = w_in_f[:, o_g:o_g + D], w_in_f[:, o_g + D:]
    w_uq_p = jnp.pad(w_uq_f.reshape(QL, H, QK), ((0, 0), (0, 0), (0, HEAD_PAD - QK))).reshape(QL, H * HEAD_PAD)

    cos_t, slo_t, shi_t = _rope_tables(N)
    ones_c = jnp.concatenate([jnp.ones((CT, NOPE + ROPE_DIM), F32), jnp.zeros((CT, ROPE_DIM), F32)], axis=1)
    cos_k = jnp.concatenate([cos_t, ones_c], axis=0)
    slo_k = jnp.concatenate([slo_t, jnp.zeros((CT, HEAD_PAD), F32)], axis=0)
    shi_k = jnp.concatenate([shi_t, jnp.zeros((CT, HEAD_PAD), F32)], axis=0)
    gq_p = jnp.pad(qk_norm_q, ((0, 0), (0, HEAD_PAD - QK)))
    gk_p = jnp.pad(qk_norm_k, ((0, 0), (0, HEAD_PAD - QK)))

    def norm_mod_fn(t, g, sh, sc):
        r = _rms_stats(t, D)
        return (((t * r) * g) * (1.0 + sc) + sh,), ()

    (h,), _ = _rowwise(norm_mod_fn, [x2], [norm1_g, sh1, sc1], [(D, BF16)], name="norm1_x")
    (ctx_h,), _ = _rowwise(norm_mod_fn, [ctx2], [norm1_g, sh1c, sc1c], [(D, BF16)], name="norm1_ctx")

    def q_norm_epi(acc, g):
        return acc, (acc * _rms_stats(acc, QL)) * g

    qc, qn = _mm([(h, w_q)], name="proj_q", outs=(F32, BF16), tn=QL, extras=[(q_norm_g, "n")], epi=q_norm_epi)
    kvin = jnp.concatenate([_mm([(h, w_kv)], name="proj_kv", outs=(F32,)),
                            _mm([(ctx_h, w_kv)], name="proj_kv_ctx", outs=(F32,))], axis=0)
    def both(a, b):
        return a, b

    u_in, v_in = _mm([(h, w_u, w_v)], name="proj_uv", outs=(BF16, BF16), epi=both)
    (g1_in, g2_in), (mine_bra, mine_brs) = _mm([(h, w_g1, w_g2)], name="proj_gates", outs=(BF16, BF16), epi=both,
                                               carry=_ChipExchange([halves[3], halves[4]], gather=True))

    def kv_norm_fn(t, g):
        kvc = t[:, :KVL]
        return (((kvc * _rms_stats(kvc, KVL)) * g),), ()

    (kvn,), _ = _rowwise(kv_norm_fn, [kvin], [kv_norm_g], [(KVL, BF16)], name="kv_norm")
    q_raw = _mm([(qn, w_uq_p)], name="q_up", outs=(F32,))
    kv_rows = _tile(M, 2304)
    kv_raw = _mm([(kvn, w_ukv_f)], name="kv_up", outs=(F32,), tm=kv_rows)

    def q_post_fn(t, cos, slo, shi, g):
        outs = []
        for hd in range(H):
            th = t[:, hd * HEAD_PAD:(hd + 1) * HEAD_PAD]
            outs.append(_rope((th * _rms_stats(th, QK)) * g, cos, slo, shi) * scale)
        return (jnp.concatenate(outs, axis=1),), ()

    (q_att,), _ = _rowwise(q_post_fn, [q_raw, cos_t, slo_t, shi_t], [gq_p], [(H * HEAD_PAD, BF16)], name="q_post")

    def k_post_fn(t, kvi, cos, slo, shi, g):
        kr = kvi[:, KVL:]
        ks, vs = [], []
        for hd in range(H):
            th = jnp.concatenate([t[:, hd * HEAD_PAD:hd * HEAD_PAD + NOPE], kr], axis=1)
            ks.append(_rope((th * _rms_stats(th, QK)) * g, cos, slo, shi))
            vs.append(t[:, hd * HEAD_PAD + NOPE:(hd + 1) * HEAD_PAD])
        return (jnp.concatenate(ks, axis=1), jnp.concatenate(vs, axis=1)), ()

    (k_att, v_att), _, (mine_out,) = _rowwise(k_post_fn, [kv_raw, kvin, cos_k, slo_k, shi_k], [gk_p],
                                              [(H * HEAD_PAD, BF16), (H * VD, BF16)], name="k_post",
                                              carry=_ChipExchange([halves[5]], gather=True))
    attn_o, lse, (mine_ffi,) = _attn_fwd(q_att, k_att, v_att, heads=H, carry=_RelayGather([halves[6]]))
    mine4 = [mine_bra, mine_brs, mine_out, mine_ffi]

    ws3 = w_spatial[0]
    bs_t = jnp.pad(b_spatial[0].T, ((0, 0), (0, LANES - G)))

    def sgu_parts(u_in, v_in, ng, nb):
        u, v = _gelu(u_in.astype(F32)), _gelu(v_in.astype(F32))
        mu = jnp.mean(v, axis=-1, keepdims=True)
        vc = v - mu
        rs = lax.rsqrt(jnp.mean(vc * vc, axis=-1, keepdims=True) + EPS)
        xhat = vc * rs
        return u, xhat, rs, (xhat * ng + nb).astype(BF16)

    def sgu_fwd_fn(u_in, v_in, ng, nb, ws, bst):
        u, _, _, vnb = sgu_parts(u_in, v_in, ng, nb)
        outs = []
        for g in range(G):
            sl = slice(g * GD, (g + 1) * GD)
            mixed = jnp.dot(ws[g].astype(BF16), vnb[:, sl], preferred_element_type=F32) + bst[:, g:g + 1]
            outs.append(u[:, sl] * mixed)
        return (jnp.concatenate(outs, axis=1),), ()

    (sgu_o,), _, theirs4 = _rowwise(sgu_fwd_fn, [u_in, v_in], [sgu_norm_g, sgu_norm_b, ws3, bs_t], [(SW, BF16)],
                                    name="sgu_fwd", tm=CH, carry=_PairExchange(mine4, "forward"))
    w_bra, w_brs, w_out_f, w_ffi = finish_gather(attn_group, mine4, theirs4)
    w_fa, w_fb = w_ffi[:, :DFF], w_ffi[:, DFF:]

    a1 = _mm([(attn_o, w_bra)], name="br_attn", outs=(BF16,))
    def merge_epi(acc, a1v, gi1, gi2):
        return acc, _sigmoid(gi1.astype(F32)) * a1v.astype(F32) + _sigmoid(gi2.astype(F32)) * acc

    a2, merged = _mm([(sgu_o, w_brs)], name="br_sgu", outs=(BF16, BF16),
                     extras=[(a1, "mn"), (g1_in, "mn"), (g2_in, "mn")], epi=merge_epi)

    def res_gate(acc, res, gate):
        return res + gate * acc, acc

    x1, mo = _mm([(merged, w_out_f)], name="out_proj", outs=(F32, BF16), tn=1024,
                 extras=[(x2, "mn"), (g1, "n")], epi=res_gate)
    (h2,), _ = _rowwise(norm_mod_fn, [x1], [norm2_g, sh2, sc2], [(D, BF16)], name="norm2")

    def swiglu_epi(a, b):
        return a, b, (a * _sigmoid(a)) * b

    (fa, fb, act), mine4 = _mm([(h2, w_fa, w_fb)], name="ffn_in", outs=(BF16, BF16, BF16), tn=512, epi=swiglu_epi,
                               carry=_ChipExchange([halves[i] for i in ffn_group], gather=True))
    (w_ffo,) = finish_gather(ffn_group, mine4, _exchange_alone(_PairExchange(mine4, "forward"), name="ag_forward_ffn"))
    def loss_epi(acc, res, t, gate):
        e = (res + gate * acc) - t
        dy = e * (1.0 / D)
        return dy, gate * dy, _colsum(e * e) * (0.5 / D), _colsum(dy * acc)

    dy, df, loss_part, dg2_part = _mm([(act, w_ffo)], name="ffn_out", outs=(F32, BF16), tn=1024, col_sums=2,
                                      extras=[(x1, "mn"), (tgt2, "mn"), (g2, "n")], epi=loss_epi)

    def fold_fn(a, b):
        return (), (_colsum(a), _colsum(b))

    _, (loss_cols, dg2) = _rowwise(fold_fn, [loss_part[:, 0, :], dg2_part[:, 0, :]], [], [], [(1, D), (1, D)],
                                   name="loss_fold", tm=loss_part.shape[0])

    def swiglu_bwd_epi(dact, a, b):
        a, b = a.astype(F32), b.astype(F32)
        s = _sigmoid(a)
        return dact * b * (s * (1.0 + a * (1.0 - s))), dact * (a * s)

    da, db = _mm([(df, w_ffo)], tb=True, name="ffn_out_dx", outs=(BF16, BF16), tn=512,
                 extras=[(fa, "mn"), (fb, "mn")], epi=swiglu_bwd_epi)
    dw_ffo = _mm([(act, df)], ta=True, name="ffn_out_dw", outs=(BF16,), tm=1408)
    dh2 = _mm([(da, w_fa), (db, w_fb)], tb=True, name="ffn_in_dx", outs=(F32,))
    ns_ffi = w_ffn_in.shape[-1]
    dw_ffi = _mm([(h2, da)], ta=True, name="ffn_in_dw_a", outs=(BF16,), tn=1408, split=ns_ffi,
                 into=(lax.empty((4, D, ns_ffi), BF16), 0))
    dw_ffi = _mm([(h2, db)], ta=True, name="ffn_in_dw_b", outs=(BF16,), tn=1408, split=ns_ffi, into=(dw_ffi, 2))

    def norm2_bwd_fn(dh, t, dyv, mov, g, sc, g1v):
        r = _rms_stats(t, D)
        tn = t * r
        dxg = dh * (1.0 + sc)
        dt = dyv + _rms_bwd(dxg * g, tn, r, D)
        return (dt, g1v * dt), (_colsum(dh), _colsum(dh * (tn * g)), _colsum(dxg * tn), _colsum(dt * mov.astype(F32)))

    (dx1, dmo), (dsh2, dsc2, dn2g, dg1) = _rowwise(
        norm2_bwd_fn, [dh2, x1, dy, mo], [norm2_g, sc2, g1], [(D, F32), (D, BF16)], [(1, D)] * 4, name="norm2_bwd")

    def merge_bwd_epi(dm, a1, a2, gi1, gi2):
        s1, s2 = _sigmoid(gi1.astype(F32)), _sigmoid(gi2.astype(F32))
        a1, a2 = a1.astype(F32), a2.astype(F32)
        return dm * s1, dm * s2, dm * a1 * (s1 * (1.0 - s1)), dm * a2 * (s2 * (1.0 - s2))

    da1, da2, dgi1, dgi2 = _mm([(dmo, w_out_f)], tb=True, name="out_proj_dx", outs=(BF16,) * 4, tn=512,
                               extras=[(a1, "mn"), (a2, "mn"), (g1_in, "mn"), (g2_in, "mn")], epi=merge_bwd_epi)
    dw_out = _mm([(merged, dmo)], ta=True, name="out_proj_dw", outs=(BF16,))
    dattn = _mm([(da1, w_bra)], tb=True, name="br_attn_dx", outs=(BF16,))
    dw_bra = _mm([(attn_o, da1)], ta=True, name="br_attn_dw", outs=(BF16,), split=w_br_attn.shape[-1])
    dsgu = _mm([(da2, w_brs)], tb=True, name="br_sgu_dx", outs=(BF16,))
    dw_brs = _mm([(sgu_o, da2)], ta=True, name="br_sgu_dw", outs=(BF16,), split=w_br_sgu.shape[-1])

    def sgu_bwd_fn(dso, u_in, v_in, ng, nb, ws, bst):
        u, xhat, rs, vnb = sgu_parts(u_in, v_in, ng, nb)
        dso = dso.astype(F32)
        lane = lax.broadcasted_iota(jnp.int32, (CH, LANES), 1)
        du, dvn, dws, dbs = [], [], [], jnp.zeros((CH, LANES), F32)
        for g in range(G):
            sl = slice(g * GD, (g + 1) * GD)
            wg = ws[g].astype(BF16)
            mixed = jnp.dot(wg, vnb[:, sl], preferred_element_type=F32) + bst[:, g:g + 1]
            du.append(dso[:, sl] * mixed)
            dmix = dso[:, sl] * u[:, sl]
            dmb = dmix.astype(BF16)
            dws.append(lax.dot_general(dmb, vnb[:, sl], (((1,), (1,)), ((), ())), preferred_element_type=F32))
            dbs = dbs + jnp.where(lane == g, jnp.sum(dmix, axis=1, keepdims=True), 0.0)
            dvn.append(lax.dot_general(wg, dmb, (((0,), (0,)), ((), ())), preferred_element_type=F32))
        du, dvn = jnp.concatenate(du, axis=1), jnp.concatenate(dvn, axis=1)
        dxh = dvn * ng
        dv = rs * (dxh - jnp.mean(dxh, axis=-1, keepdims=True) - xhat * jnp.mean(dxh * xhat, axis=-1, keepdims=True))
        return ((du * _gelu_grad(u_in.astype(F32)), dv * _gelu_grad(v_in.astype(F32))),
                (_colsum(dvn * xhat), _colsum(dvn), jnp.stack(dws), dbs))

    core = jnp.reshape(ac, (1,)).astype(jnp.int32)

    def dest_layout(dwf, i):
        K, Ns = big[i].shape
        if dwf.ndim == 2:
            dwf = dwf.reshape(K, 4, Ns).transpose(1, 0, 2) if col_sharded[i] else dwf.reshape(4, K, Ns)
        return dwf.reshape(4, 2, K // 2, Ns)

    def pair_sums(idx, g4, sib):
        return [_pair_add(g, s, core, name="rs_pair_add_" + tags[i]) for g, s, i in zip(g4, sib, idx)]

    early = [3, 4, 5, 6, 7]
    g4_early = [dest_layout(d, i) for d, i in zip([dw_bra, dw_brs, dw_out, dw_ffi, dw_ffo], early)]
    (du_in, dv_in), (d_sng, d_snb, d_ws, d_bs), sib_early = _rowwise(
        sgu_bwd_fn, [dsgu, u_in, v_in], [sgu_norm_g, sgu_norm_b, ws3, bs_t], [(SW, BF16), (SW, BF16)],
        [(1, SW), (1, SW), (G, CH, CH), (CH, LANES)], name="sgu_bwd", tm=CH, carry=_PairExchange(g4_early, "halves"))
    pair_early = pair_sums(early, g4_early, sib_early)
    dq_att, dk_att, dv_att, xchg_early = _attn_bwd(q_att, k_att, v_att, attn_o, lse, dattn, heads=H,
                                                   carry=_ChipExchange(pair_early, gather=False))

    def q_post_bwd_fn(dq, t, cos, slo, shi, g):
        outs, dg = [], jnp.zeros((1, HEAD_PAD), F32)
        for hd in range(H):
            sl = slice(hd * HEAD_PAD, (hd + 1) * HEAD_PAD)
            th = t[:, sl]
            r = _rms_stats(th, QK)
            tn = th * r
            dn = _rope_t(dq[:, sl] * scale, cos, slo, shi)
            dg = dg + _colsum(dn * tn)
            outs.append(_rms_bwd(dn * g, tn, r, QK))
        return (jnp.concatenate(outs, axis=1),), (dg,)

    (dq_raw,), (d_gq,) = _rowwise(q_post_bwd_fn, [dq_att, q_raw, cos_t, slo_t, shi_t], [gq_p],
                                  [(H * HEAD_PAD, BF16)], [(1, HEAD_PAD)], name="q_post_bwd")

    def k_post_bwd_fn(dk, dv, t, kvi, cos, slo, shi, g):
        kr = kvi[:, KVL:]
        outs, dg, dkr = [], jnp.zeros((1, HEAD_PAD), F32), jnp.zeros_like(kr)
        for hd in range(H):
            th = jnp.concatenate([t[:, hd * HEAD_PAD:hd * HEAD_PAD + NOPE], kr], axis=1)
            r = _rms_stats(th, QK)
            tn = th * r
            dn = _rope_t(dk[:, hd * HEAD_PAD:(hd + 1) * HEAD_PAD], cos, slo, shi)
            dg = dg + _colsum(dn * tn)
            dt = _rms_bwd(dn * g, tn, r, QK)
            dkr = dkr + dt[:, NOPE:]
            outs += [dt[:, :NOPE], dv[:, hd * VD:(hd + 1) * VD]]
        return (jnp.concatenate(outs, axis=1), dkr), (dg,)

    def reduced_halves(idx, xchg, pair):
        return [_sum_chips(t4, pr, chip1, name="rs_sum_" + tags[i]) for t4, pr, i in zip(xchg, pair, idx)]

    red_early = reduced_halves(early, xchg_early, pair_early)
    (dkv_raw, dkrope), (d_gk,), other_early = _rowwise(
        k_post_bwd_fn, [dk_att, dv_att, kv_raw, kvin, cos_k, slo_k, shi_k], [gk_p],
        [(H * HEAD_PAD, BF16), (2 * ROPE_DIM, F32)], [(1, HEAD_PAD)], name="k_post_bwd",
        carry=_PairExchange(red_early, "gather"))

    def q_norm_bwd_epi(dn, t, g):
        r = _rms_stats(t, QL)
        tn = t * r
        return _rms_bwd(dn * g, tn, r, QL), _colsum(dn * tn)

    dqc, d_qng_part = _mm([(dq_raw, w_uq_p)], tb=True, name="q_up_dx", outs=(BF16,), tn=QL, col_sums=1,
                          extras=[(qc, "mn"), (q_norm_g, "n")], epi=q_norm_bwd_epi)
    _, (d_qng,) = _rowwise(lambda a: ((), (_colsum(a),)), [d_qng_part[:, 0, :]], [], [], [(1, QL)],
                           name="q_norm_fold", tm=d_qng_part.shape[0])
    dw_uq_p = _mm([(qn, dq_raw)], ta=True, name="q_up_dw", outs=(BF16,))
    dkvn = _mm([(dkv_raw, w_ukv_f)], tb=True, name="kv_up_dx", outs=(F32,), tm=kv_rows)
    dw_ukv = _mm([(kvn, dkv_raw)], ta=True, name="kv_up_dw", outs=(BF16,), tk=kv_rows)

    def kv_norm_bwd_fn(dn, dkr, t, g):
        kvc = t[:, :KVL]
        r = _rms_stats(kvc, KVL)
        tn = kvc * r
        return (jnp.concatenate([_rms_bwd(dn * g, tn, r, KVL), dkr], axis=1),), (_colsum(dn * tn),)

    (dkvin,), (d_kvng,) = _rowwise(kv_norm_bwd_fn, [dkvn, dkrope, kvin], [kv_norm_g], [(KVP, BF16)], [(1, KVL)],
                                   name="kv_norm_bwd")
    dkvin_x, dkvin_c = dkvin[:N], dkvin[N:]

    dctx_h = _mm([(dkvin_c, w_kv)], tb=True, name="proj_kv_ctx_dx", outs=(F32,))
    dw_q = _mm([(h, dqc)], ta=True, name="proj_q_dw", outs=(BF16,))
    dw_kv = _mm([(h, dkvin_x), (ctx_h, dkvin_c)], ta=True, name="proj_kv_dw", outs=(BF16,))
    dw_u, dw_v = _mm([(h, du_in, dv_in)], ta=True, name="proj_uv_dw", outs=(BF16, BF16), epi=both)
    dw_g1, dw_g2 = _mm([(h, dgi1, dgi2)], ta=True, name="proj_gates_dw", outs=(BF16, BF16), epi=both)

    dw_in_f = jnp.concatenate([dw_q, dw_kv[:, :KVL + ROPE_DIM], dw_u, dw_v, dw_g1, dw_g2], axis=1)
    dw_uq_f = dw_uq_p.reshape(QL, H, HEAD_PAD)[:, :, :QK].reshape(QL, H * QK)
    late = [0, 1, 2]
    g4_late = [dest_layout(d, i) for d, i in zip([dw_in_f, dw_uq_f, dw_ukv], late)]
    pair_late = pair_sums(late, g4_late, _exchange_alone(_PairExchange(g4_late, "halves"), name="rs_pair_late"))
    dh, xchg_late = _mm([(dqc, w_q), (dkvin_x, w_kv), (du_in, w_u), (dv_in, w_v), (dgi1, w_g1), (dgi2, w_g2)],
                        tb=True, name="proj_dx", outs=(F32,), tn=1024, tk=512,
                        carry=_ChipExchange(pair_late, gather=False))

    def norm1_bwd_fn(dhv, t, dres, g, sc):
        r = _rms_stats(t, D)
        tn = t * r
        dxg = dhv * (1.0 + sc)
        return (dres + _rms_bwd(dxg * g, tn, r, D),), (_colsum(dhv), _colsum(dhv * (tn * g)), _colsum(dxg * tn))

    (grad_x,), (dsh1, dsc1, dn1g_x) = _rowwise(norm1_bwd_fn, [dh, x2, dx1], [norm1_g, sc1], [(D, F32)], [(1, D)] * 3,
                                               name="norm1_bwd")
    _, (dsh1c, dsc1c, dn1g_c) = _rowwise(norm1_bwd_fn, [dctx_h, ctx2, jnp.zeros_like(ctx2)], [norm1_g, sc1c],
                                         [(D, F32)], [(1, D)] * 3, name="norm1_ctx_bwd")

    small = [dsh1, dsc1, dg1, dsh2, dsc2, dg2,
             dsh1c, dsc1c, dn1g_x, dn1g_c, d_qng, d_kvng, d_gq, d_gk, d_sng, d_snb, dn2g, loss_cols]
    small_sizes = [a.shape[1] for a in small]
    sm_row = jnp.concatenate(small, axis=1)
    sm_mat = jnp.concatenate([d_ws.reshape(G * CH, CH), d_bs], axis=0).astype(BF16)
    row_all, mat_all = _all_gather8([sm_row, sm_mat], name="ag_small", in_vmem=True)
    row_sum = _sum_blocks(row_all, name="sum_small_rows", out_dtype=F32)
    mat_sum = _sum_blocks(mat_all, name="sum_small_mats", out_dtype=F32)
    dmod_rows = row_all[:, 0, :NMOD]
    (_, _, _, _, _, _, t_sh1c, t_sc1c, t_n1x, t_n1c, g_qng, g_kvng, t_gq, t_gk, g_sng, g_snb, g_n2g,
     t_loss) = _split_lanes(row_sum, small_sizes)
    g_ws, t_bs = mat_sum[:G * CH], mat_sum[G * CH:]
    dmodc_row = jnp.concatenate([t_sh1c, t_sc1c, jnp.zeros((1, NMOD - 2 * D), F32)], axis=1)
    dmod16 = jnp.concatenate([dmod_rows, dmodc_row, jnp.zeros((BF16_SUBLANES - 9, NMOD), F32)], axis=0)

    def small_fn(rows, n1x, n1c, lossv):
        return (), (_colsum(rows), n1x + n1c, jnp.sum(lossv, axis=1, keepdims=True))

    _, (g_bmod, g_n1g, loss11) = _rowwise(small_fn, [dmod16], [t_n1x, t_n1c, t_loss], [], [(1, NMOD), (1, D), (1, 1)],
                                          name="small_reduce", tm=16)
    dmod_loc = lax.dynamic_slice_in_dim(dmod16, my_chip * NM, NM, axis=1)
    g_wmod = _mm([(silu_c, dmod_loc)], ta=True, name="mod_dw", outs=(F32,))
    dsilu_part = _mm([(dmod_loc, wm)], tb=True, name="mod_dx", outs=(F32,))
    part_all = _all_gather8([dsilu_part[8:9]], name="ag_cctx", in_vmem=True)[0]

    def cctx_fn(parts, dsl):
        return (), ((parts[0:1] + parts[2:3] + parts[4:5] + parts[6:7]) * dsl,)

    _, (g_cctx,) = _rowwise(cctx_fn, [part_all[:, 0, :]], [dsilu_c[8:9]], [], [(1, D)], name="cctx_grad", tm=8)

    red_late = reduced_halves(late, xchg_late, pair_late)
    other_late = _exchange_alone(_PairExchange(red_late, "gather"), name="rs_halves_late")
    grad_halves = dict(zip(tags, zip(red_late + red_early, other_late + other_early)))

    mod_upd = _adamw(w_mod[0], g_wmod, m_w_mod[0], v_w_mod[0], name="adamw_w_mod")
    grads = dict(
        c_ctx=g_cctx.reshape(D), w_mod=g_wmod[None], b_mod=g_bmod, norm1_g=g_n1g,
        q_norm_g=g_qng, kv_norm_g=g_kvng, qk_norm_q=t_gq[:, :QK], qk_norm_k=t_gk[:, :QK], sgu_norm_g=g_sng,
        sgu_norm_b=g_snb, w_spatial=g_ws.reshape(w_spatial.shape), b_spatial=t_bs[:, :G].T[None], norm2_g=g_n2g)
    weights = dict(c_ctx=c_ctx, w_mod=w_mod, b_mod=b_mod, norm1_g=norm1_g, w_in=w_in, q_norm_g=q_norm_g,
                   kv_norm_g=kv_norm_g, w_uq=w_uq, w_ukv=w_ukv, qk_norm_q=qk_norm_q, qk_norm_k=qk_norm_k,
                   sgu_norm_g=sgu_norm_g, sgu_norm_b=sgu_norm_b, w_spatial=w_spatial, b_spatial=b_spatial,
                   w_br_attn=w_br_attn, w_br_sgu=w_br_sgu, w_out=w_out, norm2_g=norm2_g, w_ffn_in=w_ffn_in,
                   w_ffn_out=w_ffn_out)
    m_in = dict(c_ctx=m_c_ctx, w_mod=m_w_mod, b_mod=m_b_mod, norm1_g=m_norm1_g, w_in=m_w_in, q_norm_g=m_q_norm_g,
                kv_norm_g=m_kv_norm_g, w_uq=m_w_uq, w_ukv=m_w_ukv, qk_norm_q=m_qk_norm_q, qk_norm_k=m_qk_norm_k,
                sgu_norm_g=m_sgu_norm_g, sgu_norm_b=m_sgu_norm_b, w_spatial=m_w_spatial, b_spatial=m_b_spatial,
                w_br_attn=m_w_br_attn, w_br_sgu=m_w_br_sgu, w_out=m_w_out, norm2_g=m_norm2_g, w_ffn_in=m_w_ffn_in,
                w_ffn_out=m_w_ffn_out)
    v_in_ = dict(c_ctx=v_c_ctx, w_mod=v_w_mod, b_mod=v_b_mod, norm1_g=v_norm1_g, w_in=v_w_in, q_norm_g=v_q_norm_g,
                 kv_norm_g=v_kv_norm_g, w_uq=v_w_uq, w_ukv=v_w_ukv, qk_norm_q=v_qk_norm_q, qk_norm_k=v_qk_norm_k,
                 sgu_norm_g=v_sgu_norm_g, sgu_norm_b=v_sgu_norm_b, w_spatial=v_w_spatial, b_spatial=v_b_spatial,
                 w_br_attn=v_w_br_attn, w_br_sgu=v_w_br_sgu, w_out=v_w_out, norm2_g=v_norm2_g, w_ffn_in=v_w_ffn_in,
                 w_ffn_out=v_w_ffn_out)
    names = list(weights)
    big_names = ("w_mod", "w_in", "w_uq", "w_ukv", "w_br_attn", "w_br_sgu", "w_out", "w_ffn_in", "w_ffn_out")
    out_g, out_d, out_m, out_v = {}, {}, {}, {}
    out_g["w_mod"] = grads["w_mod"]
    out_d["w_mod"], out_m["w_mod"], out_v["w_mod"] = [t[None] for t in mod_upd[:3]]
    for nm in big_names[1:]:
        res = _adamw_halves(weights[nm][0], *grad_halves[nm], m_in[nm][0], v_in_[nm][0], core, name="adamw_" + nm)
        out_g[nm], out_d[nm], out_m[nm], out_v[nm] = [t[None] for t in res]
    row_names = [nm for nm in names if nm not in big_names and nm not in ("w_spatial", "b_spatial")]
    widths = [-(-weights[nm].size // LANES) * LANES for nm in row_names]

    def as_row(d):
        return jnp.concatenate([jnp.pad(d[nm].reshape(1, -1), ((0, 0), (0, wd - d[nm].size)))
                                for nm, wd in zip(row_names, widths)], axis=1)

    def as_mat(d):
        return jnp.concatenate([d["w_spatial"].reshape(G * CH, CH), d["b_spatial"].reshape(G, CH)], axis=0)

    row_res = _adamw(as_row(weights), as_row(grads), as_row(m_in), as_row(v_in_), name="adamw_rows")
    mat_res = _adamw(as_mat(weights), as_mat(grads), as_mat(m_in), as_mat(v_in_), name="adamw_spatial")
    for tgt, row, mat in zip((out_d, out_m, out_v), row_res, mat_res):
        for nm, seg in zip(row_names, _split_lanes(row, widths)):
            tgt[nm] = seg[:, :weights[nm].size].reshape(weights[nm].shape)
        tgt["w_spatial"] = mat[:G * CH].reshape(w_spatial.shape)
        tgt["b_spatial"] = mat[G * CH:].reshape(b_spatial.shape)
    for nm in row_names + ["w_spatial", "b_spatial"]:
        out_g[nm] = grads[nm].reshape(weights[nm].shape)

    loss = loss11.reshape(())
    return (loss, grad_x[None], *[out_g[n] for n in names], *[out_d[n] for n in names],
            *[out_m[n] for n in names], *[out_v[n] for n in names])
```

```python
import math

import jax
import jax.numpy as jnp
from jax import lax
from jax.experimental import pallas as pl
from jax.experimental.pallas import tpu as pltpu

F32, BF16 = jnp.float32, jnp.bfloat16
MESH = pl.DeviceIdType.MESH

LANES = 128
F32_SUBLANES = 8
BF16_SUBLANES = 16
MXU_DIM = 256
VMEM_LIMIT_BYTES = 56 * 1024 * 1024

EPS = 1e-6
ROPE_DIM = 64
ROPE_THETA = 10000.0
GRID_W = 64
HEAD_PAD = 256
ADAM_LR, ADAM_B1, ADAM_B2, ADAM_EPS, ADAM_WD, ADAM_STEP = 0.001, 0.9, 0.999, 1e-08, 0.01, 10


def _tile(dim, pref, align=LANES):
    if dim <= pref:
        return dim
    t = (pref // align) * align
    while t >= align:
        if dim % t == 0:
            return t
        t -= align
    return dim


def _params(sem=None):
    return pltpu.CompilerParams(dimension_semantics=sem, vmem_limit_bytes=VMEM_LIMIT_BYTES)


def _sds(shape, dtype):
    return jax.ShapeDtypeStruct(tuple(shape), dtype)


def _mm(pairs, *, name, ta=False, tb=False, outs=(F32,), tm=1024, tn=1024, tk=2048, extras=(), epi=None,
        split=None, into=None, carry=None, col_sums=0):
    dual = len(pairs[0]) == 3
    a0, b0 = pairs[0][0], pairs[0][1]
    M = a0.shape[1] if ta else a0.shape[0]
    N = b0.shape[0] if tb else b0.shape[1]
    tm, tn = _tile(M, tm), _tile(N if split is None else split, tn)
    ks = [(p[0].shape[0] if ta else p[0].shape[1]) for p in pairs]
    tks = [_tile(k, tk) for k in ks]
    nks = [k // t for k, t in zip(ks, tks)]
    offs = [sum(nks[:i]) for i in range(len(pairs))]
    nk_total = sum(nks)
    single = len(pairs) == 1

    def kidx(kk, p):
        return kk if single else jnp.clip(kk - offs[p], 0, nks[p] - 1)

    in_specs, operands = [], []
    for p, pr in enumerate(pairs):
        if ta:
            in_specs.append(pl.BlockSpec((tks[p], tm), lambda i, j, kk, p=p: (kidx(kk, p), i)))
        else:
            in_specs.append(pl.BlockSpec((tm, tks[p]), lambda i, j, kk, p=p: (i, kidx(kk, p))))
        operands.append(pr[0])
        for b in pr[1:]:
            if tb:
                in_specs.append(pl.BlockSpec((tn, tks[p]), lambda i, j, kk, p=p: (j, kidx(kk, p))))
            else:
                in_specs.append(pl.BlockSpec((tks[p], tn), lambda i, j, kk, p=p: (kidx(kk, p), j)))
            operands.append(b)
    for arr, kind in extras:
        if kind == "mn":
            in_specs.append(pl.BlockSpec((tm, tn), lambda i, j, kk: (i, j)))
        else:
            in_specs.append(pl.BlockSpec((1, tn), lambda i, j, kk: (0, j)))
        operands.append(arr)
    n_in = len(operands)
    n_ex = len(extras)
    per = 3 if dual else 2
    dims = (((0 if ta else 1,), (1 if tb else 0,)), ((), ()))

    n_acc = 2 if dual else 1

    def products(ins, p):
        a = ins[per * p][...].astype(BF16)
        return [lax.dot_general(a, ins[per * p + 1 + q][...].astype(BF16), dims, preferred_element_type=F32)
                for q in range(n_acc)]

    def finish(ins, out_refs, acc_vals):
        vals = acc_vals + [r[...] for r in ins[n_in - n_ex:]]
        res = epi(*vals) if epi is not None else (vals[0],)
        for o, r in zip(out_refs, res):
            o[...] = jnp.broadcast_to(r, o.shape).astype(o.dtype)

    out_specs = [pl.BlockSpec((tm, tn), lambda i, j, kk: (i, j)) for _ in outs]
    out_specs += [pl.BlockSpec((None, F32_SUBLANES, tn), lambda i, j, kk: (i, 0, j)) for _ in range(col_sums)]
    out_shape = [_sds((M, N), d) for d in outs] + [_sds((M // tm, F32_SUBLANES, N), F32) for _ in range(col_sums)]
    aliases = {}
    n_alias = 0
    if split is not None:
        nps = split // tn
        lead = 0 if into is None else into[1]
        out_specs = [pl.BlockSpec((None, tm, tn), lambda i, j, kk: (j // nps + lead, i, j % nps))]
        out_shape = [_sds((N // split if into is None else into[0].shape[0], M, split), outs[0])]
        if into is not None:
            in_specs.append(pl.BlockSpec(memory_space=pl.ANY))
            operands.append(into[0])
            aliases, n_alias = {n_in: 0}, 1

    grid = (M // tm, N // tn, nk_total)
    n_out = len(outs) + col_sums

    def at_step(first):
        ids = [pl.program_id(d) for d in range(3)]
        cond = None
        for d, g in zip(ids, grid):
            t = d == (0 if first else g - 1)
            cond = t if cond is None else cond & t
        return cond

    def body(*refs):
        ins, out_refs, accs, start, wait = _split_refs(refs, n_in + n_alias, n_out, carry)
        ins = ins[:n_in]
        if carry is not None:
            pl.when(at_step(True))(start)
        if nk_total == 1:
            finish(ins, out_refs, products(ins, 0))
        else:
            kk = pl.program_id(2)

            @pl.when(kk == 0)
            def _():
                for acc, v in zip(accs, products(ins, 0)):
                    acc[...] = v

            for p in range(len(pairs)):
                lo = max(offs[p], 1)

                @pl.when((kk >= lo) & (kk < offs[p] + nks[p]))
                def _(p=p):
                    for acc, v in zip(accs, products(ins, p)):
                        acc[...] += v

            @pl.when(kk == nk_total - 1)
            def _():
                finish(ins, out_refs, [acc[...] for acc in accs])
        if carry is not None:
            pl.when(at_step(False))(wait)

    ex = carry
    res = pl.pallas_call(
        body, name=name, grid=grid, in_specs=in_specs + ([] if ex is None else ex.in_specs),
        out_specs=out_specs + ([] if ex is None else ex.out_specs),
        out_shape=out_shape + ([] if ex is None else ex.out_shape), input_output_aliases=aliases,
        scratch_shapes=[pltpu.VMEM((tm, tn), F32) for _ in range(n_acc if nk_total > 1 else 0)]
        + ([] if ex is None else ex.scratch),
        compiler_params=_params(("arbitrary",) * 3 if ex is not None else ("parallel", "parallel", "arbitrary")),
    )(*operands, *([] if ex is None else ex.xs))
    if ex is not None:
        return (res[0] if n_out == 1 else res[:n_out]), list(res[n_out:])
    return res[0] if n_out == 1 else res


def _rowwise(fn, rows, vecs, out_rows, out_accs=(), *, name, tm=256, tc=None, carry=None):
    M = rows[0].shape[0]
    tm = _tile(M, tm, BF16_SUBLANES)
    nrow = M // tm
    C = rows[0].shape[1]
    ncol = 1 if tc is None else C // _tile(C, tc)
    tcol = None if tc is None else _tile(C, tc)

    def colwise(shape):
        return tc is not None and len(shape) == 2 and shape[0] == 1 and shape[1] == C

    def vspec(shape):
        if colwise(shape):
            return pl.BlockSpec((1, tcol), lambda j, i: (0, j))
        return pl.BlockSpec(tuple(shape), lambda j, i, n=len(shape): (0,) * n)

    def rspec(width):
        if tc is None:
            return pl.BlockSpec((tm, width), lambda j, i: (i, 0))
        return pl.BlockSpec((tm, tcol), lambda j, i: (i, j))

    in_specs = [rspec(r.shape[1]) for r in rows] + [vspec(v.shape) for v in vecs]
    out_specs = [rspec(c) for c, _ in out_rows] + [vspec(s) for s in out_accs]
    out_shape = [_sds((M, c), d) for c, d in out_rows] + [_sds(s, F32) for s in out_accs]
    n_in, n_or = len(rows) + len(vecs), len(out_rows)

    n_out = n_or + len(out_accs)
    ex = carry

    def body(*refs):
        ins, outs, _, start, wait = _split_refs(refs, n_in, n_out, ex)
        o_rows, o_accs = outs[:n_or], outs[n_or:]
        if ex is not None:
            pl.when((pl.program_id(0) == 0) & (pl.program_id(1) == 0))(start)
        r_out, a_out = fn(*[r[...] for r in ins])
        for o, r in zip(o_rows, r_out):
            o[...] = r.astype(o.dtype)
        i = pl.program_id(1)

        @pl.when(i == 0)
        def _():
            for o, a in zip(o_accs, a_out):
                o[...] = a

        @pl.when(i > 0)
        def _():
            for o, a in zip(o_accs, a_out):
                o[...] += a

        if ex is not None:
            pl.when((pl.program_id(0) == ncol - 1) & (pl.program_id(1) == nrow - 1))(wait)

    res = pl.pallas_call(
        body, name=name, grid=(ncol, nrow), in_specs=in_specs + ([] if ex is None else ex.in_specs),
        out_specs=out_specs + ([] if ex is None else ex.out_specs),
        out_shape=out_shape + ([] if ex is None else ex.out_shape),
        scratch_shapes=[] if ex is None else ex.scratch,
        compiler_params=_params(("arbitrary", "arbitrary") if ex is not None else ("parallel", "arbitrary")),
    )(*rows, *vecs, *([] if ex is None else ex.xs))
    if ex is not None:
        return res[:n_or], res[n_or:n_out], list(res[n_out:])
    return res[:n_or], res[n_or:]


def _colsum(t):
    return jnp.sum(t, axis=0, keepdims=True)


def _gelu(t):
    return 0.5 * t * (1.0 + lax.erf(t * math.sqrt(0.5)))


def _gelu_grad(t):
    return 0.5 * (1.0 + lax.erf(t * math.sqrt(0.5))) + t * jnp.exp(-0.5 * t * t) * (1.0 / math.sqrt(2.0 * math.pi))


def _sigmoid(t):
    return 1.0 / (1.0 + jnp.exp(-t))


def _rms_stats(t, width):
    return lax.rsqrt(jnp.sum(t * t, axis=-1, keepdims=True) * (1.0 / width) + EPS)


def _rms_bwd(dn, tn, r, width):
    return r * (dn - tn * (jnp.sum(dn * tn, axis=-1, keepdims=True) * (1.0 / width)))


def _place():
    return lax.axis_index("x"), lax.axis_index("y"), lax.axis_index("c")


class _ChipExchange:
    def __init__(self, xs, gather):
        self.xs, self.gather, self.n = list(xs), gather, len(xs)
        self.in_specs = [pl.BlockSpec(memory_space=pl.ANY)] * self.n
        self.out_specs = [pl.BlockSpec(memory_space=pl.ANY)] * self.n
        self.out_shape = [_sds((4,) + (x.shape if gather else x.shape[1:]), x.dtype) for x in self.xs]
        self.scratch = [pltpu.SemaphoreType.DMA((self.n, 3)), pltpu.SemaphoreType.DMA((self.n, 3))]

    def bind(self, x_refs, out_refs, send_sems, recv_sems):
        x, y, c = _place()
        p = 2 * x + y
        chips = [(1 - x, y), (x, 1 - y), (1 - x, 1 - y)]

        def copy(w, k, outgoing):
            qx, qy = chips[k]
            there = 2 * qx + qy
            if self.gather:
                src = x_refs[w]
            else:
                src = x_refs[w].at[there if outgoing else p]
            return pltpu.make_async_remote_copy(
                src_ref=src, dst_ref=out_refs[w].at[p if outgoing else there], send_sem=send_sems.at[w, k],
                recv_sem=recv_sems.at[w, k], device_id=(qx, qy, c), device_id_type=MESH)

        def start():
            for w in range(self.n):
                for k in range(3):
                    copy(w, k, True).start()

        def wait():
            for w in range(self.n):
                for k in range(3):
                    copy(w, k, False).wait_recv()
            for w in range(self.n):
                for k in range(3):
                    copy(w, k, True).wait_send()

        return start, wait


class _RelayGather:
    def __init__(self, xs):
        self.xs, self.n = list(xs), len(xs)
        self.in_specs = [pl.BlockSpec(memory_space=pl.ANY)] * self.n
        self.out_specs = [pl.BlockSpec(memory_space=pl.ANY)] * self.n
        self.out_shape = [_sds((4,) + x.shape, x.dtype) for x in self.xs]
        self.scratch = [pltpu.SemaphoreType.DMA((self.n, 4)), pltpu.SemaphoreType.DMA((self.n, 4))]
        self.relay = None

    def bind(self, x_refs, out_refs, send_sems, recv_sems):
        x, y, c = _place()
        own, xn, yn, dg = (x, y), (1 - x, y), (x, 1 - y), (1 - x, 1 - y)

        def slot(w, chip):
            return out_refs[w].at[2 * chip[0] + chip[1]]

        def cp(w, k, src, dst, chip):
            return pltpu.make_async_remote_copy(src_ref=src, dst_ref=dst, send_sem=send_sems.at[w, k],
                                                recv_sem=recv_sems.at[w, k], device_id=(*chip, c), device_id_type=MESH)

        def halves(w):
            h = x_refs[w].shape[0] // 2
            return pl.ds(0, h), pl.ds(h, h)

        def start():
            for w in range(self.n):
                cp(w, 0, x_refs[w], slot(w, own), xn).start()
                cp(w, 1, x_refs[w], slot(w, own), yn).start()

        def relay():
            for w in range(self.n):
                lo, hi = halves(w)
                cp(w, 1, x_refs[w], slot(w, yn), yn).wait_recv()
                cp(w, 2, slot(w, yn).at[lo], slot(w, yn).at[lo], xn).start()
                cp(w, 0, x_refs[w], slot(w, xn), xn).wait_recv()
                cp(w, 3, slot(w, xn).at[hi], slot(w, xn).at[hi], yn).start()

        def wait():
            for w in range(self.n):
                lo, hi = halves(w)
                cp(w, 2, slot(w, dg).at[lo], slot(w, dg).at[lo], xn).wait_recv()
                cp(w, 3, slot(w, dg).at[hi], slot(w, dg).at[hi], yn).wait_recv()
                cp(w, 0, x_refs[w], slot(w, own), xn).wait_send()
                cp(w, 1, x_refs[w], slot(w, own), yn).wait_send()
                cp(w, 2, slot(w, yn).at[lo], slot(w, yn).at[lo], xn).wait_send()
                cp(w, 3, slot(w, xn).at[hi], slot(w, xn).at[hi], yn).wait_send()

        self.relay = relay
        return start, wait


class _PairExchange:
    def __init__(self, xs, mode):
        self.xs, self.mode, self.n = list(xs), mode, len(xs)
        self.in_specs = [pl.BlockSpec(memory_space=pl.ANY)] * self.n
        self.out_specs = [pl.BlockSpec(memory_space=pl.ANY)] * self.n
        shape = {"halves": lambda s: (4,) + s[2:], "forward": lambda s: s, "gather": lambda s: (2,) + s}[mode]
        self.out_shape = [_sds(shape(x.shape), x.dtype) for x in self.xs]
        self.scratch = [pltpu.SemaphoreType.DMA((self.n, 3)), pltpu.SemaphoreType.DMA((self.n, 3))]

    def bind(self, x_refs, out_refs, send_sems, recv_sems):
        x, y, c = _place()
        chips = [(1 - x, y), (x, 1 - y), (1 - x, 1 - y)]

        def copy(w, src, dst, k):
            return pltpu.make_async_remote_copy(src_ref=src, dst_ref=dst, send_sem=send_sems.at[w, k],
                                                recv_sem=recv_sems.at[w, k], device_id=(x, y, 1 - c),
                                                device_id_type=MESH)

        def start():
            for w, (xr, orf) in enumerate(zip(x_refs, out_refs)):
                if self.mode == "halves":
                    for q in range(4):
                        copy(w, xr.at[q, 1 - c], orf.at[q], 0).start()
                elif self.mode == "forward":
                    for k, (qx, qy) in enumerate(chips):
                        copy(w, xr.at[2 * qx + qy], orf.at[2 * qx + qy], k).start()
                else:
                    copy(w, xr, orf.at[c], 0).start()

        def wait():
            for w, (xr, orf) in enumerate(zip(x_refs, out_refs)):
                if self.mode == "halves":
                    copy(w, orf, orf, 0).wait()
                elif self.mode == "forward":
                    for k, (qx, qy) in enumerate(chips):
                        copy(w, xr.at[2 * qx + qy], orf.at[2 * qx + qy], k).wait()
                else:
                    cp = copy(w, xr, orf.at[1 - c], 0)
                    cp.wait_recv()
                    cp.wait_send()

        return start, wait


def _split_refs(refs, n_in, n_out, ex):
    ne = 0 if ex is None else ex.n
    ins, xin = refs[:n_in], refs[n_in:n_in + ne]
    outs, xout = refs[n_in + ne:n_in + ne + n_out], refs[n_in + ne + n_out:n_in + 2 * ne + n_out]
    rest = refs[n_in + 2 * ne + n_out:]
    if ex is None:
        return ins, outs, rest, None, None
    start, wait = ex.bind(xin, xout, rest[-2], rest[-1])
    return ins, outs, rest[:-2], start, wait


def _attn_fwd(q, k, v, *, heads, tq=1024, carry=None):
    N, M = q.shape[0], k.shape[0]
    tq = _tile(N, tq)
    sub = _tile(tq, MXU_DIM)
    vd = v.shape[1] // heads
    nq = N // tq

    def body(*refs):
        (q_ref, k_ref, v_ref), (o_ref, lse_ref), _, start, wait = _split_refs(refs, 3, 2, carry)
        if carry is not None:
            pl.when((pl.program_id(0) == 0) & (pl.program_id(1) == 0))(start)
            if getattr(carry, "relay", None) is not None:
                pl.when((pl.program_id(0) == (5 * heads) // 8) & (pl.program_id(1) == 0))(carry.relay)
        for sb in range(tq // sub):
            rows = pl.ds(sb * sub, sub)
            s = lax.dot_general(q_ref[rows, :], k_ref[...], (((1,), (1,)), ((), ())), preferred_element_type=F32)
            m = jnp.max(s, axis=-1, keepdims=True)
            p = jnp.exp(s - m)
            l = jnp.sum(p, axis=-1, keepdims=True)
            o = jnp.dot(p.astype(BF16), v_ref[...], preferred_element_type=F32) / l
            o_ref[rows, :] = o.astype(o_ref.dtype)
            lse_ref[rows, :] = jnp.broadcast_to(m + jnp.log(l), (sub, vd))
        if carry is not None:
            pl.when((pl.program_id(0) == heads - 1) & (pl.program_id(1) == nq - 1))(wait)

    ex = carry
    res = pl.pallas_call(
        body, name="attn_fwd", grid=(heads, nq),
        in_specs=[pl.BlockSpec((tq, HEAD_PAD), lambda h, i: (i, h)),
                  pl.BlockSpec((M, HEAD_PAD), lambda h, i: (0, h)),
                  pl.BlockSpec((M, vd), lambda h, i: (0, h))] + ([] if ex is None else ex.in_specs),
        out_specs=[pl.BlockSpec((tq, vd), lambda h, i: (i, h)),
                   pl.BlockSpec((tq, vd), lambda h, i: (i, h))] + ([] if ex is None else ex.out_specs),
        out_shape=[_sds((N, heads * vd), BF16), _sds((N, heads * vd), F32)] + ([] if ex is None else ex.out_shape),
        scratch_shapes=[] if ex is None else ex.scratch,
        compiler_params=_params(("arbitrary", "arbitrary")),
    )(q, k, v, *([] if ex is None else ex.xs))
    return res[0], res[1], list(res[2:])


def _attn_bwd(q, k, v, o, lse, do, *, heads, tq=512, carry=None):
    N, M = q.shape[0], k.shape[0]
    tq = _tile(N, tq)
    vd = v.shape[1] // heads
    nq = N // tq
    sub = _tile(tq, MXU_DIM)
    nt = (((1,), (1,)), ((), ()))
    tn = (((0,), (0,)), ((), ()))

    def body(*refs):
        (q_ref, k_ref, v_ref, o_ref, lse_ref, do_ref), (dq_ref, dk_ref, dv_ref), _, start, wait = _split_refs(
            refs, 6, 3, carry)
        if carry is not None:
            pl.when((pl.program_id(0) == 0) & (pl.program_id(1) == 0))(start)
        i = pl.program_id(1)
        parts = []
        for sb in range(tq // sub):
            rows = pl.ds(sb * sub, sub)
            qb, dob = q_ref[rows, :], do_ref[rows, :]
            delta = jnp.sum(dob.astype(F32) * o_ref[rows, :].astype(F32), axis=-1, keepdims=True)
            s = lax.dot_general(qb, k_ref[...], nt, preferred_element_type=F32)
            p = jnp.exp(s - lse_ref[rows, :][:, :1])
            dp = lax.dot_general(dob, v_ref[...], nt, preferred_element_type=F32)
            ds = (p * (dp - delta)).astype(BF16)
            dq_ref[rows, :] = jnp.dot(ds, k_ref[...], preferred_element_type=F32)
            parts.append((lax.dot_general(ds, qb, tn, preferred_element_type=F32),
                          lax.dot_general(p.astype(BF16), dob, tn, preferred_element_type=F32)))

        dk_step, dv_step = parts[0]
        for dk_part, dv_part in parts[1:]:
            dk_step, dv_step = dk_step + dk_part, dv_step + dv_part

        @pl.when(i == 0)
        def _():
            dk_ref[...] = dk_step
            dv_ref[...] = dv_step

        @pl.when(i > 0)
        def _():
            dk_ref[...] += dk_step
            dv_ref[...] += dv_step

        if carry is not None:
            pl.when((pl.program_id(0) == heads - 1) & (pl.program_id(1) == nq - 1))(wait)

    ex = carry
    res = pl.pallas_call(
        body, name="attn_bwd", grid=(heads, nq),
        in_specs=[pl.BlockSpec((tq, HEAD_PAD), lambda h, i: (i, h)),
                  pl.BlockSpec((M, HEAD_PAD), lambda h, i: (0, h)),
                  pl.BlockSpec((M, vd), lambda h, i: (0, h)),
                  pl.BlockSpec((tq, vd), lambda h, i: (i, h)),
                  pl.BlockSpec((tq, vd), lambda h, i: (i, h)),
                  pl.BlockSpec((tq, vd), lambda h, i: (i, h))] + ([] if ex is None else ex.in_specs),
        out_specs=[pl.BlockSpec((tq, HEAD_PAD), lambda h, i: (i, h)),
                   pl.BlockSpec((M, HEAD_PAD), lambda h, i: (0, h)),
                   pl.BlockSpec((M, vd), lambda h, i: (0, h))] + ([] if ex is None else ex.out_specs),
        out_shape=[_sds((N, heads * HEAD_PAD), F32), _sds((M, heads * HEAD_PAD), F32),
                   _sds((M, heads * vd), F32)] + ([] if ex is None else ex.out_shape),
        scratch_shapes=[] if ex is None else ex.scratch,
        compiler_params=_params(("arbitrary", "arbitrary")),
    )(q, k, v, o, lse, do, *([] if ex is None else ex.xs))
    return res[0], res[1], res[2], list(res[3:])


def _comm_call(body, xs, out_shapes, n_sems, name, in_vmem):
    space = pltpu.VMEM if in_vmem else pl.ANY
    n = len(xs)

    def wrapped(*refs):
        body(refs[:n], refs[n:2 * n], *refs[2 * n:])

    return pl.pallas_call(
        wrapped, name=name, out_shape=list(out_shapes),
        in_specs=[pl.BlockSpec(memory_space=space)] * n, out_specs=[pl.BlockSpec(memory_space=space)] * n,
        scratch_shapes=[pltpu.SemaphoreType.DMA((n, n_sems)), pltpu.SemaphoreType.DMA((n, n_sems)),
                        pltpu.SemaphoreType.DMA((n,))],
        compiler_params=pltpu.CompilerParams(vmem_limit_bytes=VMEM_LIMIT_BYTES),
    )(*xs)


def _all_gather8(blks, *, name, in_vmem):
    def body(x_refs, out_refs, send_sems, recv_sems, local_sems):
        x, y, c = _place()
        me, sibling = (x, y, c), (x, y, 1 - c)
        chips = [(1 - x, y), (x, 1 - y), (1 - x, 1 - y)]
        waits = []
        for w, (x_ref, out_ref) in enumerate(zip(x_refs, out_refs)):
            def slot(px, py, pc, out_ref=out_ref):
                return out_ref.at[4 * px + 2 * py + pc]

            def copy(k, block, to, src=None, w=w, slot=slot):
                return pltpu.make_async_remote_copy(
                    src_ref=slot(*block) if src is None else src, dst_ref=slot(*block),
                    send_sem=send_sems.at[w, k], recv_sem=recv_sems.at[w, k], device_id=to, device_id_type=MESH)

            mine = pltpu.make_async_copy(x_ref, slot(*me), local_sems.at[w])
            mine.start()
            first = [copy(0, me, sibling, src=x_ref)]
            first += [copy(1 + j, me, (*chip, c), src=x_ref) for j, chip in enumerate(chips)]
            for cp in first:
                cp.start()
            waits.append((copy, mine, first))
        for copy, mine, first in waits:
            passed = [copy(4 + j, (*chip, c), sibling) for j, chip in enumerate(chips)]
            for j, chip in enumerate(chips):
                copy(1 + j, (*chip, c), me).wait_recv()
                passed[j].start()
            copy(0, sibling, me).wait_recv()
            for j, chip in enumerate(chips):
                copy(4 + j, (*chip, 1 - c), me).wait_recv()
            for cp in first + passed:
                cp.wait_send()
            mine.wait()

    return _comm_call(body, blks, [_sds((8,) + b.shape, b.dtype) for b in blks], 7, name, in_vmem)


def _gather_others(blks, *, name):
    def body(x_refs, out_refs, send_sems, recv_sems, local_sems):
        x, y, c = _place()
        own, xn, yn, dg = (x, y), (1 - x, y), (x, 1 - y), (1 - x, 1 - y)

        def slot(w, chip, core):
            return out_refs[w].at[4 * chip[0] + 2 * chip[1] + core]

        def cp(w, k, src, dst, chip, core):
            return pltpu.make_async_remote_copy(src_ref=src, dst_ref=dst, send_sem=send_sems.at[w, k],
                                                recv_sem=recv_sems.at[w, k], device_id=(*chip, core),
                                                device_id_type=MESH)

        def halves(w):
            h = x_refs[w].shape[0] // 2
            return pl.ds(0, h), pl.ds(h, h)

        sends = []
        for w, x_ref in enumerate(x_refs):
            sends += [cp(w, 0, x_ref, slot(w, own, c), xn, c), cp(w, 1, x_ref, slot(w, own, c), yn, c)]
        for s in sends:
            s.start()
        for w, x_ref in enumerate(x_refs):
            lo, hi = halves(w)
            cp(w, 1, x_ref, slot(w, yn, c), yn, c).wait_recv()
            passed = [cp(w, 2, slot(w, yn, c).at[lo], slot(w, yn, c).at[lo], xn, c),
                      cp(w, 4, slot(w, yn, c), slot(w, yn, c), own, 1 - c)]
            cp(w, 0, x_ref, slot(w, xn, c), xn, c).wait_recv()
            passed += [cp(w, 3, slot(w, xn, c).at[hi], slot(w, xn, c).at[hi], yn, c),
                       cp(w, 5, slot(w, xn, c), slot(w, xn, c), own, 1 - c)]
            for s in passed:
                s.start()
            sends += passed
        for w in range(len(x_refs)):
            lo, hi = halves(w)
            cp(w, 2, slot(w, dg, c).at[lo], slot(w, dg, c).at[lo], xn, c).wait_recv()
            cp(w, 3, slot(w, dg, c).at[hi], slot(w, dg, c).at[hi], yn, c).wait_recv()
            passed = [cp(w, 6, slot(w, dg, c), slot(w, dg, c), own, 1 - c)]
            passed[0].start()
            sends += passed
        for w in range(len(x_refs)):
            cp(w, 4, slot(w, yn, c), slot(w, yn, 1 - c), own, 1 - c).wait_recv()
            cp(w, 5, slot(w, xn, c), slot(w, xn, 1 - c), own, 1 - c).wait_recv()
            cp(w, 6, slot(w, dg, c), slot(w, dg, 1 - c), own, 1 - c).wait_recv()
        for s in sends:
            s.wait_send()

    return list(_comm_call(body, blks, [_sds((8,) + b.shape, b.dtype) for b in blks], 7, name, False))


def _exchange_alone(ex, *, name):
    def body(x_refs, out_refs, send_sems, recv_sems, local_sems):
        start, wait = ex.bind(x_refs, out_refs, send_sems, recv_sems)
        start()
        wait()

    return list(_comm_call(body, ex.xs, ex.out_shape, 3, name, False))


def _block_rows(rows, row_bytes, target=1 << 21, align=BF16_SUBLANES):
    return _tile(rows, max(align, target // row_bytes // align * align), align)


def _sum_blocks(buf, *, name, out_dtype):
    B, R, C = buf.shape
    tm = _block_rows(R, B * C * buf.dtype.itemsize)

    def body(x_ref, o_ref):
        acc = x_ref[0].astype(F32)
        for b in range(1, B):
            acc = acc + x_ref[b].astype(F32)
        o_ref[...] = acc.astype(o_ref.dtype)

    return pl.pallas_call(
        body, name=name, grid=(R // tm,), in_specs=[pl.BlockSpec((B, tm, C), lambda i: (0, i, 0))],
        out_specs=pl.BlockSpec((tm, C), lambda i: (i, 0)), out_shape=_sds((R, C), out_dtype),
        compiler_params=_params(("parallel",)),
    )(buf)


def _sum_chips(received, sent, chip, *, name):
    _, R, C = received.shape
    tm = _block_rows(R, 5 * C * received.dtype.itemsize)

    def body(chip_ref, r0, r1, r2, r3, own_ref, o_ref):
        acc = None
        for q, r in enumerate((r0, r1, r2, r3)):
            term = jnp.where(q == chip_ref[0], own_ref[...], r[...]).astype(F32)
            acc = term if acc is None else acc + term
        o_ref[...] = acc

    def slot(q):
        return pl.BlockSpec((None, tm, C), lambda i, ch, q=q: (jnp.where(q == ch[0], (q + 1) % 4, q), i, 0))

    return pl.pallas_call(
        body, name=name, out_shape=_sds((R, C), F32),
        grid_spec=pltpu.PrefetchScalarGridSpec(
            num_scalar_prefetch=1, grid=(R // tm,),
            in_specs=[slot(0), slot(1), slot(2), slot(3), pl.BlockSpec((None, tm, C), lambda i, ch: (ch[0], i, 0))],
            out_specs=pl.BlockSpec((tm, C), lambda i, ch: (i, 0))),
        compiler_params=_params(("arbitrary",)),
    )(chip, received, received, received, received, sent)


def _pair_add(mine, theirs, core, *, name):
    _, _, R, C = mine.shape
    tm = _block_rows(R, C * 2)

    def body(core_ref, a_ref, b_ref, o_ref):
        o_ref[...] = (a_ref[...].astype(F32) + b_ref[...].astype(F32)).astype(o_ref.dtype)

    return pl.pallas_call(
        body, name=name, out_shape=_sds(theirs.shape, BF16),
        grid_spec=pltpu.PrefetchScalarGridSpec(
            num_scalar_prefetch=1, grid=(4, R // tm),
            in_specs=[pl.BlockSpec((None, None, tm, C), lambda q, i, core_ref: (q, core_ref[0], i, 0)),
                      pl.BlockSpec((None, tm, C), lambda q, i, core_ref: (q, i, 0))],
            out_specs=pl.BlockSpec((None, tm, C), lambda q, i, core_ref: (q, i, 0))),
        compiler_params=_params(("parallel", "parallel")),
    )(core, mine, theirs)


def _assemble(gathered, own, chip, *, name, transpose):
    _, K, Ns = gathered.shape
    tm = _block_rows(K, Ns * 4)

    def body(chip_ref, g_ref, own_ref, o_ref):
        q = pl.program_id(0)

        @pl.when(q == chip_ref[0])
        def _():
            o_ref[...] = own_ref[...].astype(BF16)

        @pl.when(q != chip_ref[0])
        def _():
            o_ref[...] = g_ref[...]

    if transpose:
        out_spec = pl.BlockSpec((tm, Ns), lambda q, i, ch: (i, q))
        out_shape = _sds((K, 4 * Ns), BF16)
    else:
        out_spec = pl.BlockSpec((None, tm, Ns), lambda q, i, ch: (q, i, 0))
        out_shape = _sds((4, K, Ns), BF16)
    return pl.pallas_call(
        body, name=name, out_shape=out_shape,
        grid_spec=pltpu.PrefetchScalarGridSpec(
            num_scalar_prefetch=1, grid=(4, K // tm),
            in_specs=[pl.BlockSpec((None, tm, Ns), lambda q, i, ch: (jnp.where(q == ch[0], (q + 1) % 4, q), i, 0)),
                      pl.BlockSpec((tm, Ns), lambda q, i, ch: (jnp.where(q == ch[0], i, 0), 0))],
            out_specs=out_spec),
        compiler_params=_params(("arbitrary", "arbitrary")),
    )(chip, gathered, own)


def _assemble_halves(mine, theirs, own, place, *, name, transpose):
    _, K2, Ns = mine.shape
    tm = _block_rows(K2, Ns * 4, target=1 << 22)
    nb = K2 // tm

    def body(place_ref, m_ref, t_ref, own_ref, o_ref):
        q, hb = pl.program_id(0), pl.program_id(1)
        is_own = q == place_ref[0]
        is_mine = hb == place_ref[1]

        @pl.when(is_own)
        def _():
            o_ref[...] = own_ref[...].astype(BF16)

        @pl.when(jnp.logical_not(is_own) & is_mine)
        def _():
            o_ref[...] = m_ref[...]

        @pl.when(jnp.logical_not(is_own) & jnp.logical_not(is_mine))
        def _():
            o_ref[...] = t_ref[...]

    def other(q, pr):
        return jnp.where(q == pr[0], (q + 1) % 4, q)

    if transpose:
        out_spec = pl.BlockSpec((tm, Ns), lambda q, hb, i, pr: (hb * nb + i, q))
        out_shape = _sds((2 * K2, 4 * Ns), BF16)
    else:
        out_spec = pl.BlockSpec((None, tm, Ns), lambda q, hb, i, pr: (q, hb * nb + i, 0))
        out_shape = _sds((4, 2 * K2, Ns), BF16)
    return pl.pallas_call(
        body, name=name, out_shape=out_shape,
        grid_spec=pltpu.PrefetchScalarGridSpec(
            num_scalar_prefetch=1, grid=(4, 2, nb),
            in_specs=[pl.BlockSpec((None, tm, Ns), lambda q, hb, i, pr: (other(q, pr), jnp.where(hb == pr[1], i, 0), 0)),
                      pl.BlockSpec((None, tm, Ns), lambda q, hb, i, pr: (other(q, pr), jnp.where(hb == pr[1], 0, i), 0)),
                      pl.BlockSpec((tm, Ns), lambda q, hb, i, pr: (jnp.where(q == pr[0], hb * nb + i, 0), 0))],
            out_specs=out_spec),
        compiler_params=_params(("arbitrary",) * 3),
    )(place, mine, theirs, own)


def _split_lanes(row, widths):
    out, off = [], 0
    for wd in widths:
        out.append(row[:, off:off + wd])
        off += wd
    return out


def _adamw_math(w, g, m, v):
    m = ADAM_B1 * m + (1.0 - ADAM_B1) * g
    v = ADAM_B2 * v + (1.0 - ADAM_B2) * (g * g)
    m_hat = m / (1.0 - ADAM_B1 ** ADAM_STEP)
    v_hat = v / (1.0 - ADAM_B2 ** ADAM_STEP)
    delta = -ADAM_LR * (m_hat / (jnp.sqrt(v_hat) + ADAM_EPS) + ADAM_WD * w)
    return delta, m, v


def _adamw_halves(w, mine, other, m, v, core, *, name):
    K, Ns = w.shape
    tm = _block_rows(K // 2, Ns * 4, target=1 << 20, align=F32_SUBLANES)
    nb = (K // 2) // tm

    def body(core_ref, w_ref, mine_ref, other_ref, m_ref, v_ref, g_out, d_out, m_out, v_out):
        g = jnp.where(pl.program_id(0) // nb == core_ref[0], mine_ref[...], other_ref[...])
        g_out[...] = g
        d_out[...], m_out[...], v_out[...] = _adamw_math(w_ref[...], g, m_ref[...], v_ref[...])

    row = pl.BlockSpec((tm, Ns), lambda i, cr: (i, 0))
    return pl.pallas_call(
        body, name=name, out_shape=[_sds((K, Ns), F32)] * 4,
        grid_spec=pltpu.PrefetchScalarGridSpec(
            num_scalar_prefetch=1, grid=(K // tm,),
            in_specs=[row,
                      pl.BlockSpec((tm, Ns), lambda i, cr: (jnp.where(i // nb == cr[0], i % nb, 0), 0)),
                      pl.BlockSpec((None, tm, Ns), lambda i, cr: (1 - cr[0], jnp.where(i // nb == cr[0], 0, i % nb), 0)),
                      row, row],
            out_specs=[row, row, row, row]),
        compiler_params=_params(("arbitrary",)),
    )(core, w, mine, other, m, v)


def _adamw(w, g, m, v, *, name, carry=None):
    C = w.shape[1]

    def fn(w, g, m, v):
        return _adamw_math(w, g, m, v), ()

    tm = max(F32_SUBLANES, min(512, (1 << 20) // (4 * C) // F32_SUBLANES * F32_SUBLANES))
    res = _rowwise(fn, [w, g, m, v], [], [(C, F32)] * 3, name=name, tm=tm, carry=carry)
    return tuple(res[0]) + ((res[2],) if carry is not None else ())


def _rope_tables(n):
    rows = n // GRID_W
    row = jnp.repeat(jnp.arange(rows, dtype=F32), GRID_W)
    col = jnp.tile(jnp.arange(GRID_W, dtype=F32), rows)
    nf = ROPE_DIM // 4
    freqs = ROPE_THETA ** (-jnp.arange(nf, dtype=F32) / nf)
    ang_r, ang_c = row[:, None] * freqs[None, :], col[:, None] * freqs[None, :]
    cr, sr, cc, sc = jnp.cos(ang_r), jnp.sin(ang_r), jnp.cos(ang_c), jnp.sin(ang_c)
    nope = HEAD_PAD - 2 * ROPE_DIM
    one, zero, z = jnp.ones((n, nope), F32), jnp.zeros((n, nope), F32), jnp.zeros((n, nf), F32)
    pad = jnp.zeros((n, ROPE_DIM), F32)
    cos = jnp.concatenate([one, cr, cr, cc, cc, pad], axis=1)
    s_lo = jnp.concatenate([zero, -sr, z, -sc, z, pad], axis=1)
    s_hi = jnp.concatenate([zero, z, sr, z, sc, pad], axis=1)
    return cos, s_lo, s_hi


def _rope(n, cos, s_lo, s_hi):
    q = ROPE_DIM // 4
    return n * cos + pltpu.roll(n, HEAD_PAD - q, 1) * s_lo + pltpu.roll(n, q, 1) * s_hi


def _rope_t(d, cos, s_lo, s_hi):
    q = ROPE_DIM // 4
    return d * cos + pltpu.roll(d * s_lo, q, 1) + pltpu.roll(d * s_hi, HEAD_PAD - q, 1)


def kernel(x, c, ctx, c_ctx, w_mod, b_mod, norm1_g, w_in, q_norm_g, kv_norm_g, w_uq, w_ukv, qk_norm_q, qk_norm_k, sgu_norm_g, sgu_norm_b, w_spatial, b_spatial, w_br_attn, w_br_sgu, w_out, norm2_g, w_ffn_in, w_ffn_out, loss_target, m_c_ctx, m_w_mod, m_b_mod, m_norm1_g, m_w_in, m_q_norm_g, m_kv_norm_g, m_w_uq, m_w_ukv, m_qk_norm_q, m_qk_norm_k, m_sgu_norm_g, m_sgu_norm_b, m_w_spatial, m_b_spatial, m_w_br_attn, m_w_br_sgu, m_w_out, m_norm2_g, m_w_ffn_in, m_w_ffn_out, v_c_ctx, v_w_mod, v_b_mod, v_norm1_g, v_w_in, v_q_norm_g, v_kv_norm_g, v_w_uq, v_w_ukv, v_qk_norm_q, v_qk_norm_k, v_sgu_norm_g, v_sgu_norm_b, v_w_spatial, v_b_spatial, v_w_br_attn, v_w_br_sgu, v_w_out, v_norm2_g, v_w_ffn_in, v_w_ffn_out):
    ax, ay, ac = _place()
    my_chip = 2 * ax + ay
    my_dev = 4 * ax + 2 * ay + ac

    N, D = x.shape[1], x.shape[2]
    CT = ctx.shape[1]
    M = N + CT
    QL, KVL, QK = q_norm_g.shape[-1], kv_norm_g.shape[-1], qk_norm_q.shape[-1]
    NOPE = QK - ROPE_DIM
    VD = NOPE
    H = 4 * w_uq.shape[-1] // QK
    SW, G, CH = sgu_norm_g.shape[-1], w_spatial.shape[1], w_spatial.shape[2]
    GD = SW // G
    DFF = 4 * w_ffn_out.shape[1]
    NMOD = 4 * w_mod.shape[-1]
    NM = w_mod.shape[-1]
    KVP = KVL + 2 * ROPE_DIM
    assert NOPE == LANES and GD == LANES and HEAD_PAD == NOPE + 2 * ROPE_DIM and CH == LANES
    scale = QK ** -0.5

    x2, ctx2, tgt2 = x[0], ctx[0], loss_target[0]

    c_all = _all_gather8([c], name="ag_c", in_vmem=True)[0][:, 0, :]
    c_rows = jnp.concatenate([c_all, c_ctx[None, :], jnp.zeros((BF16_SUBLANES - 9, D), F32)], axis=0)

    def silu_fn(t):
        s = _sigmoid(t)
        return (t * s, s * (1.0 + t * (1.0 - s))), ()

    (silu_c, dsilu_c), _ = _rowwise(silu_fn, [c_rows], [], [(D, F32), (D, F32)], name="silu_c", tm=16)
    wm = w_mod[0]
    mod_loc = _mm([(silu_c, wm)], name="mod_fwd", outs=(F32,),
                  extras=[(lax.dynamic_slice_in_dim(b_mod, my_chip * NM, NM, axis=1), "n")],
                  epi=lambda acc, b: (acc + b,))
    mod_all = _all_gather8([mod_loc], name="ag_mod", in_vmem=True)[0]
    mod_full = jnp.concatenate([mod_all[0], mod_all[2], mod_all[4], mod_all[6]], axis=1)
    mod_me = lax.dynamic_slice_in_dim(mod_full, my_dev, 1, axis=0)
    sh1, sc1, g1, sh2, sc2, g2 = [mod_me[:, i * D:(i + 1) * D] for i in range(6)]
    sh1c, sc1c = mod_full[8:9, :D], mod_full[8:9, D:2 * D]

    big = [w_in[0], w_uq[0], w_ukv[0], w_br_attn[0], w_br_sgu[0], w_out[0], w_ffn_in[0], w_ffn_out[0]]
    col_sharded = [True, True, True, True, True, False, True, False]
    halves = [lax.dynamic_slice_in_dim(a, ac * (a.shape[0] // 2), a.shape[0] // 2, axis=0).astype(BF16) for a in big]
    tags = ["w_in", "w_uq", "w_ukv", "w_br_attn", "w_br_sgu", "w_out", "w_ffn_in", "w_ffn_out"]
    first_group, attn_group, ffn_group = [0, 1, 2], [3, 4, 5, 6], [7]
    chip1 = jnp.reshape(my_chip, (1,)).astype(jnp.int32)
    place2 = jnp.stack([my_chip, ac]).astype(jnp.int32)

    def laid_out(seg, i):
        a = big[i]
        if col_sharded[i] and seg.ndim == 3:
            return seg.transpose(1, 0, 2).reshape(a.shape[0], 4 * a.shape[1])
        return seg if col_sharded[i] else seg.reshape(4 * a.shape[0], a.shape[1])

    def side_by_side(i):
        return col_sharded[i] and big[i].shape[1] % LANES == 0

    def finish_gather(idx, mine4, theirs4):
        return [laid_out(_assemble_halves(m, t, big[i], place2, name="assemble_" + tags[i], transpose=side_by_side(i)), i)
                for i, m, t in zip(idx, mine4, theirs4)]

    gathered = _gather_others([halves[i] for i in first_group], name="ag_weights")
    w_in_f, w_uq_f, w_ukv_f = [
        laid_out(_assemble(seg.reshape((4,) + big[i].shape), big[i], chip1, name="assemble_" + tags[i],
                           transpose=side_by_side(i)), i) for i, seg in zip(first_group, gathered)]
    o_kv, o_u = QL, QL + KVL + ROPE_DIM
    o_v, o_g = o_u + SW, o_u + 2 * SW
    w_q = w_in_f[:, :QL]
    w_kv = jnp.pad(w_in_f[:, o_kv:o_u], ((0, 0), (0, ROPE_DIM)))
    w_u, w_v = w_in_f[:, o_u:o_v], w_in_f[:, o_v:o_g]
    w_g1, w_g2 = w_in_f[:, o_g:o_g + D], w_in_f[:, o_g + D:]
    w_uq_p = jnp.pad(w_uq_f.reshape(QL, H, QK), ((0, 0), (0, 0), (0, HEAD_PAD - QK))).reshape(QL, H * HEAD_PAD)

    cos_t, slo_t, shi_t = _rope_tables(N)
    ones_c = jnp.concatenate([jnp.ones((CT, NOPE + ROPE_DIM), F32), jnp.zeros((CT, ROPE_DIM), F32)], axis=1)
    cos_k = jnp.concatenate([cos_t, ones_c], axis=0)
    slo_k = jnp.concatenate([slo_t, jnp.zeros((CT, HEAD_PAD), F32)], axis=0)
    shi_k = jnp.concatenate([shi_t, jnp.zeros((CT, HEAD_PAD), F32)], axis=0)
    gq_p = jnp.pad(qk_norm_q, ((0, 0), (0, HEAD_PAD - QK)))
    gk_p = jnp.pad(qk_norm_k, ((0, 0), (0, HEAD_PAD - QK)))

    def norm_mod_fn(t, g, sh, sc):
        r = _rms_stats(t, D)
        return (((t * r) * g) * (1.0 + sc) + sh,), ()

    (h,), _ = _rowwise(norm_mod_fn, [x2], [norm1_g, sh1, sc1], [(D, BF16)], name="norm1_x")
    (ctx_h,), _ = _rowwise(norm_mod_fn, [ctx2], [norm1_g, sh1c, sc1c], [(D, BF16)], name="norm1_ctx")

    def q_norm_epi(acc, g):
        return acc, (acc * _rms_stats(acc, QL)) * g

    qc, qn = _mm([(h, w_q)], name="proj_q", outs=(F32, BF16), tn=QL, extras=[(q_norm_g, "n")], epi=q_norm_epi)
    kvin = jnp.concatenate([_mm([(h, w_kv)], name="proj_kv", outs=(F32,)),
                            _mm([(ctx_h, w_kv)], name="proj_kv_ctx", outs=(F32,))], axis=0)
    def both(a, b):
        return a, b

    u_in, v_in = _mm([(h, w_u, w_v)], name="proj_uv", outs=(BF16, BF16), epi=both)
    (g1_in, g2_in), (mine_bra, mine_brs) = _mm([(h, w_g1, w_g2)], name="proj_gates", outs=(BF16, BF16), epi=both,
                                               carry=_ChipExchange([halves[3], halves[4]], gather=True))

    def kv_norm_fn(t, g):
        kvc = t[:, :KVL]
        return (((kvc * _rms_stats(kvc, KVL)) * g),), ()

    (kvn,), _ = _rowwise(kv_norm_fn, [kvin], [kv_norm_g], [(KVL, BF16)], name="kv_norm")
    q_raw = _mm([(qn, w_uq_p)], name="q_up", outs=(F32,))
    kv_rows = _tile(M, 2304)
    kv_raw = _mm([(kvn, w_ukv_f)], name="kv_up", outs=(F32,), tm=kv_rows)

    def q_post_fn(t, cos, slo, shi, g):
        outs = []
        for hd in range(H):
            th = t[:, hd * HEAD_PAD:(hd + 1) * HEAD_PAD]
            outs.append(_rope((th * _rms_stats(th, QK)) * g, cos, slo, shi) * scale)
        return (jnp.concatenate(outs, axis=1),), ()

    (q_att,), _ = _rowwise(q_post_fn, [q_raw, cos_t, slo_t, shi_t], [gq_p], [(H * HEAD_PAD, BF16)], name="q_post")

    def k_post_fn(t, kvi, cos, slo, shi, g):
        kr = kvi[:, KVL:]
        ks, vs = [], []
        for hd in range(H):
            th = jnp.concatenate([t[:, hd * HEAD_PAD:hd * HEAD_PAD + NOPE], kr], axis=1)
            ks.append(_rope((th * _rms_stats(th, QK)) * g, cos, slo, shi))
            vs.append(t[:, hd * HEAD_PAD + NOPE:(hd + 1) * HEAD_PAD])
        return (jnp.concatenate(ks, axis=1), jnp.concatenate(vs, axis=1)), ()

    (k_att, v_att), _, (mine_out,) = _rowwise(k_post_fn, [kv_raw, kvin, cos_k, slo_k, shi_k], [gk_p],
                                              [(H * HEAD_PAD, BF16), (H * VD, BF16)], name="k_post",
                                              carry=_ChipExchange([halves[5]], gather=True))
    attn_o, lse, (mine_ffi,) = _attn_fwd(q_att, k_att, v_att, heads=H, carry=_RelayGather([halves[6]]))
    mine4 = [mine_bra, mine_brs, mine_out, mine_ffi]

    ws3 = w_spatial[0]
    bs_t = jnp.pad(b_spatial[0].T, ((0, 0), (0, LANES - G)))

    def sgu_parts(u_in, v_in, ng, nb):
        u, v = _gelu(u_in.astype(F32)), _gelu(v_in.astype(F32))
        mu = jnp.mean(v, axis=-1, keepdims=True)
        vc = v - mu
        rs = lax.rsqrt(jnp.mean(vc * vc, axis=-1, keepdims=True) + EPS)
        xhat = vc * rs
        return u, xhat, rs, (xhat * ng + nb).astype(BF16)

    def sgu_fwd_fn(u_in, v_in, ng, nb, ws, bst):
        u, _, _, vnb = sgu_parts(u_in, v_in, ng, nb)
        outs = []
        for g in range(G):
            sl = slice(g * GD, (g + 1) * GD)
            mixed = jnp.dot(ws[g].astype(BF16), vnb[:, sl], preferred_element_type=F32) + bst[:, g:g + 1]
            outs.append(u[:, sl] * mixed)
        return (jnp.concatenate(outs, axis=1),), ()

    (sgu_o,), _, theirs4 = _rowwise(sgu_fwd_fn, [u_in, v_in], [sgu_norm_g, sgu_norm_b, ws3, bs_t], [(SW, BF16)],
                                    name="sgu_fwd", tm=CH, carry=_PairExchange(mine4, "forward"))
    w_bra, w_brs, w_out_f, w_ffi = finish_gather(attn_group, mine4, theirs4)
    w_fa, w_fb = w_ffi[:, :DFF], w_ffi[:, DFF:]

    a1 = _mm([(attn_o, w_bra)], name="br_attn", outs=(BF16,))
    def merge_epi(acc, a1v, gi1, gi2):
        return acc, _sigmoid(gi1.astype(F32)) * a1v.astype(F32) + _sigmoid(gi2.astype(F32)) * acc

    a2, merged = _mm([(sgu_o, w_brs)], name="br_sgu", outs=(BF16, BF16),
                     extras=[(a1, "mn"), (g1_in, "mn"), (g2_in, "mn")], epi=merge_epi)

    def res_gate(acc, res, gate):
        return res + gate * acc, acc

    x1, mo = _mm([(merged, w_out_f)], name="out_proj", outs=(F32, BF16), tn=1024,
                 extras=[(x2, "mn"), (g1, "n")], epi=res_gate)
    (h2,), _ = _rowwise(norm_mod_fn, [x1], [norm2_g, sh2, sc2], [(D, BF16)], name="norm2")

    def swiglu_epi(a, b):
        return a, b, (a * _sigmoid(a)) * b

    (fa, fb, act), mine4 = _mm([(h2, w_fa, w_fb)], name="ffn_in", outs=(BF16, BF16, BF16), tn=512, epi=swiglu_epi,
                               carry=_ChipExchange([halves[i] for i in ffn_group], gather=True))
    (w_ffo,) = finish_gather(ffn_group, mine4, _exchange_alone(_PairExchange(mine4, "forward"), name="ag_forward_ffn"))
    def loss_epi(acc, res, t, gate):
        e = (res + gate * acc) - t
        dy = e * (1.0 / D)
        return dy, gate * dy, _colsum(e * e) * (0.5 / D), _colsum(dy * acc)

    dy, df, loss_part, dg2_part = _mm([(act, w_ffo)], name="ffn_out", outs=(F32, BF16), tn=1024, col_sums=2,
                                      extras=[(x1, "mn"), (tgt2, "mn"), (g2, "n")], epi=loss_epi)

    def fold_fn(a, b):
        return (), (_colsum(a), _colsum(b))

    _, (loss_cols, dg2) = _rowwise(fold_fn, [loss_part[:, 0, :], dg2_part[:, 0, :]], [], [], [(1, D), (1, D)],
                                   name="loss_fold", tm=loss_part.shape[0])

    def swiglu_bwd_epi(dact, a, b):
        a, b = a.astype(F32), b.astype(F32)
        s = _sigmoid(a)
        return dact * b * (s * (1.0 + a * (1.0 - s))), dact * (a * s)

    da, db = _mm([(df, w_ffo)], tb=True, name="ffn_out_dx", outs=(BF16, BF16), tn=512,
                 extras=[(fa, "mn"), (fb, "mn")], epi=swiglu_bwd_epi)
    dw_ffo = _mm([(act, df)], ta=True, name="ffn_out_dw", outs=(BF16,), tm=1408)
    dh2 = _mm([(da, w_fa), (db, w_fb)], tb=True, name="ffn_in_dx", outs=(F32,))
    ns_ffi = w_ffn_in.shape[-1]
    dw_ffi = _mm([(h2, da)], ta=True, name="ffn_in_dw_a", outs=(BF16,), tn=1408, split=ns_ffi,
                 into=(lax.empty((4, D, ns_ffi), BF16), 0))
    dw_ffi = _mm([(h2, db)], ta=True, name="ffn_in_dw_b", outs=(BF16,), tn=1408, split=ns_ffi, into=(dw_ffi, 2))

    def norm2_bwd_fn(dh, t, dyv, mov, g, sc, g1v):
        r = _rms_stats(t, D)
        tn = t * r
        dxg = dh * (1.0 + sc)
        dt = dyv + _rms_bwd(dxg * g, tn, r, D)
        return (dt, g1v * dt), (_colsum(dh), _colsum(dh * (tn * g)), _colsum(dxg * tn), _colsum(dt * mov.astype(F32)))

    (dx1, dmo), (dsh2, dsc2, dn2g, dg1) = _rowwise(
        norm2_bwd_fn, [dh2, x1, dy, mo], [norm2_g, sc2, g1], [(D, F32), (D, BF16)], [(1, D)] * 4, name="norm2_bwd")

    def merge_bwd_epi(dm, a1, a2, gi1, gi2):
        s1, s2 = _sigmoid(gi1.astype(F32)), _sigmoid(gi2.astype(F32))
        a1, a2 = a1.astype(F32), a2.astype(F32)
        return dm * s1, dm * s2, dm * a1 * (s1 * (1.0 - s1)), dm * a2 * (s2 * (1.0 - s2))

    da1, da2, dgi1, dgi2 = _mm([(dmo, w_out_f)], tb=True, name="out_proj_dx", outs=(BF16,) * 4, tn=512,
                               extras=[(a1, "mn"), (a2, "mn"), (g1_in, "mn"), (g2_in, "mn")], epi=merge_bwd_epi)
    dw_out = _mm([(merged, dmo)], ta=True, name="out_proj_dw", outs=(BF16,))
    dattn = _mm([(da1, w_bra)], tb=True, name="br_attn_dx", outs=(BF16,))
    dw_bra = _mm([(attn_o, da1)], ta=True, name="br_attn_dw", outs=(BF16,), split=w_br_attn.shape[-1])
    dsgu = _mm([(da2, w_brs)], tb=True, name="br_sgu_dx", outs=(BF16,))
    dw_brs = _mm([(sgu_o, da2)], ta=True, name="br_sgu_dw", outs=(BF16,), split=w_br_sgu.shape[-1])

    def sgu_bwd_fn(dso, u_in, v_in, ng, nb, ws, bst):
        u, xhat, rs, vnb = sgu_parts(u_in, v_in, ng, nb)
        dso = dso.astype(F32)
        lane = lax.broadcasted_iota(jnp.int32, (CH, LANES), 1)
        du, dvn, dws, dbs = [], [], [], jnp.zeros((CH, LANES), F32)
        for g in range(G):
            sl = slice(g * GD, (g + 1) * GD)
            wg = ws[g].astype(BF16)
            mixed = jnp.dot(wg, vnb[:, sl], preferred_element_type=F32) + bst[:, g:g + 1]
            du.append(dso[:, sl] * mixed)
            dmix = dso[:, sl] * u[:, sl]
            dmb = dmix.astype(BF16)
            dws.append(lax.dot_general(dmb, vnb[:, sl], (((1,), (1,)), ((), ())), preferred_element_type=F32))
            dbs = dbs + jnp.where(lane == g, jnp.sum(dmix, axis=1, keepdims=True), 0.0)
            dvn.append(lax.dot_general(wg, dmb, (((0,), (0,)), ((), ())), preferred_element_type=F32))
        du, dvn = jnp.concatenate(du, axis=1), jnp.concatenate(dvn, axis=1)
        dxh = dvn * ng
        dv = rs * (dxh - jnp.mean(dxh, axis=-1, keepdims=True) - xhat * jnp.mean(dxh * xhat, axis=-1, keepdims=True))
        return ((du * _gelu_grad(u_in.astype(F32)), dv * _gelu_grad(v_in.astype(F32))),
                (_colsum(dvn * xhat), _colsum(dvn), jnp.stack(dws), dbs))

    core = jnp.reshape(ac, (1,)).astype(jnp.int32)

    def dest_layout(dwf, i):
        K, Ns = big[i].shape
        if dwf.ndim == 2:
            dwf = dwf.reshape(K, 4, Ns).transpose(1, 0, 2) if col_sharded[i] else dwf.reshape(4, K, Ns)
        return dwf.reshape(4, 2, K // 2, Ns)

    def pair_sums(idx, g4, sib):
        return [_pair_add(g, s, core, name="rs_pair_add_" + tags[i]) for g, s, i in zip(g4, sib, idx)]

    early = [3, 4, 5, 6, 7]
    g4_early = [dest_layout(d, i) for d, i in zip([dw_bra, dw_brs, dw_out, dw_ffi, dw_ffo], early)]
    (du_in, dv_in), (d_sng, d_snb, d_ws, d_bs), sib_early = _rowwise(
        sgu_bwd_fn, [dsgu, u_in, v_in], [sgu_norm_g, sgu_norm_b, ws3, bs_t], [(SW, BF16), (SW, BF16)],
        [(1, SW), (1, SW), (G, CH, CH), (CH, LANES)], name="sgu_bwd", tm=CH, carry=_PairExchange(g4_early, "halves"))
    pair_early = pair_sums(early, g4_early, sib_early)
    dq_att, dk_att, dv_att, xchg_early = _attn_bwd(q_att, k_att, v_att, attn_o, lse, dattn, heads=H,
                                                   carry=_ChipExchange(pair_early, gather=False))

    def q_post_bwd_fn(dq, t, cos, slo, shi, g):
        outs, dg = [], jnp.zeros((1, HEAD_PAD), F32)
        for hd in range(H):
            sl = slice(hd * HEAD_PAD, (hd + 1) * HEAD_PAD)
            th = t[:, sl]
            r = _rms_stats(th, QK)
            tn = th * r
            dn = _rope_t(dq[:, sl] * scale, cos, slo, shi)
            dg = dg + _colsum(dn * tn)
            outs.append(_rms_bwd(dn * g, tn, r, QK))
        return (jnp.concatenate(outs, axis=1),), (dg,)

    (dq_raw,), (d_gq,) = _rowwise(q_post_bwd_fn, [dq_att, q_raw, cos_t, slo_t, shi_t], [gq_p],
                                  [(H * HEAD_PAD, BF16)], [(1, HEAD_PAD)], name="q_post_bwd")

    def k_post_bwd_fn(dk, dv, t, kvi, cos, slo, shi, g):
        kr = kvi[:, KVL:]
        outs, dg, dkr = [], jnp.zeros((1, HEAD_PAD), F32), jnp.zeros_like(kr)
        for hd in range(H):
            th = jnp.concatenate([t[:, hd * HEAD_PAD:hd * HEAD_PAD + NOPE], kr], axis=1)
            r = _rms_stats(th, QK)
            tn = th * r
            dn = _rope_t(dk[:, hd * HEAD_PAD:(hd + 1) * HEAD_PAD], cos, slo, shi)
            dg = dg + _colsum(dn * tn)
            dt = _rms_bwd(dn * g, tn, r, QK)
            dkr = dkr + dt[:, NOPE:]
            outs += [dt[:, :NOPE], dv[:, hd * VD:(hd + 1) * VD]]
        return (jnp.concatenate(outs, axis=1), dkr), (dg,)

    def reduced_halves(idx, xchg, pair):
        return [_sum_chips(t4, pr, chip1, name="rs_sum_" + tags[i]) for t4, pr, i in zip(xchg, pair, idx)]

    red_early = reduced_halves(early, xchg_early, pair_early)
    (dkv_raw, dkrope), (d_gk,), other_early = _rowwise(
        k_post_bwd_fn, [dk_att, dv_att, kv_raw, kvin, cos_k, slo_k, shi_k], [gk_p],
        [(H * HEAD_PAD, BF16), (2 * ROPE_DIM, F32)], [(1, HEAD_PAD)], name="k_post_bwd",
        carry=_PairExchange(red_early, "gather"))

    def q_norm_bwd_epi(dn, t, g):
        r = _rms_stats(t, QL)
        tn = t * r
        return _rms_bwd(dn * g, tn, r, QL), _colsum(dn * tn)

    dqc, d_qng_part = _mm([(dq_raw, w_uq_p)], tb=True, name="q_up_dx", outs=(BF16,), tn=QL, col_sums=1,
                          extras=[(qc, "mn"), (q_norm_g, "n")], epi=q_norm_bwd_epi)
    _, (d_qng,) = _rowwise(lambda a: ((), (_colsum(a),)), [d_qng_part[:, 0, :]], [], [], [(1, QL)],
                           name="q_norm_fold", tm=d_qng_part.shape[0])
    dw_uq_p = _mm([(qn, dq_raw)], ta=True, name="q_up_dw", outs=(BF16,))
    dkvn = _mm([(dkv_raw, w_ukv_f)], tb=True, name="kv_up_dx", outs=(F32,), tm=kv_rows)
    dw_ukv = _mm([(kvn, dkv_raw)], ta=True, name="kv_up_dw", outs=(BF16,), tk=kv_rows)

    def kv_norm_bwd_fn(dn, dkr, t, g):
        kvc = t[:, :KVL]
        r = _rms_stats(kvc, KVL)
        tn = kvc * r
        return (jnp.concatenate([_rms_bwd(dn * g, tn, r, KVL), dkr], axis=1),), (_colsum(dn * tn),)

    (dkvin,), (d_kvng,) = _rowwise(kv_norm_bwd_fn, [dkvn, dkrope, kvin], [kv_norm_g], [(KVP, BF16)], [(1, KVL)],
                                   name="kv_norm_bwd")
    dkvin_x, dkvin_c = dkvin[:N], dkvin[N:]

    dctx_h = _mm([(dkvin_c, w_kv)], tb=True, name="proj_kv_ctx_dx", outs=(F32,))
    dw_q = _mm([(h, dqc)], ta=True, name="proj_q_dw", outs=(BF16,))
    dw_kv = _mm([(h, dkvin_x), (ctx_h, dkvin_c)], ta=True, name="proj_kv_dw", outs=(BF16,))
    dw_u, dw_v = _mm([(h, du_in, dv_in)], ta=True, name="proj_uv_dw", outs=(BF16, BF16), epi=both)
    dw_g1, dw_g2 = _mm([(h, dgi1, dgi2)], ta=True, name="proj_gates_dw", outs=(BF16, BF16), epi=both)

    dw_in_f = jnp.concatenate([dw_q, dw_kv[:, :KVL + ROPE_DIM], dw_u, dw_v, dw_g1, dw_g2], axis=1)
    dw_uq_f = dw_uq_p.reshape(QL, H, HEAD_PAD)[:, :, :QK].reshape(QL, H * QK)
    late = [0, 1, 2]
    g4_late = [dest_layout(d, i) for d, i in zip([dw_in_f, dw_uq_f, dw_ukv], late)]
    pair_late = pair_sums(late, g4_late, _exchange_alone(_PairExchange(g4_late, "halves"), name="rs_pair_late"))
    dh, xchg_late = _mm([(dqc, w_q), (dkvin_x, w_kv), (du_in, w_u), (dv_in, w_v), (dgi1, w_g1), (dgi2, w_g2)],
                        tb=True, name="proj_dx", outs=(F32,), tn=1024, tk=512,
                        carry=_ChipExchange(pair_late, gather=False))

    def norm1_bwd_fn(dhv, t, dres, g, sc):
        r = _rms_stats(t, D)
        tn = t * r
        dxg = dhv * (1.0 + sc)
        return (dres + _rms_bwd(dxg * g, tn, r, D),), (_colsum(dhv), _colsum(dhv * (tn * g)), _colsum(dxg * tn))

    (grad_x,), (dsh1, dsc1, dn1g_x) = _rowwise(norm1_bwd_fn, [dh, x2, dx1], [norm1_g, sc1], [(D, F32)], [(1, D)] * 3,
                                               name="norm1_bwd")
    _, (dsh1c, dsc1c, dn1g_c) = _rowwise(norm1_bwd_fn, [dctx_h, ctx2, jnp.zeros_like(ctx2)], [norm1_g, sc1c],
                                         [(D, F32)], [(1, D)] * 3, name="norm1_ctx_bwd")

    small = [dsh1, dsc1, dg1, dsh2, dsc2, dg2,
             dsh1c, dsc1c, dn1g_x, dn1g_c, d_qng, d_kvng, d_gq, d_gk, d_sng, d_snb, dn2g, loss_cols]
    small_sizes = [a.shape[1] for a in small]
    sm_row = jnp.concatenate(small, axis=1)
    sm_mat = jnp.concatenate([d_ws.reshape(G * CH, CH), d_bs], axis=0).astype(BF16)
    row_all, mat_all = _all_gather8([sm_row, sm_mat], name="ag_small", in_vmem=True)
    row_sum = _sum_blocks(row_all, name="sum_small_rows", out_dtype=F32)
    mat_sum = _sum_blocks(mat_all, name="sum_small_mats", out_dtype=F32)
    dmod_rows = row_all[:, 0, :NMOD]
    (_, _, _, _, _, _, t_sh1c, t_sc1c, t_n1x, t_n1c, g_qng, g_kvng, t_gq, t_gk, g_sng, g_snb, g_n2g,
     t_loss) = _split_lanes(row_sum, small_sizes)
    g_ws, t_bs = mat_sum[:G * CH], mat_sum[G * CH:]
    dmodc_row = jnp.concatenate([t_sh1c, t_sc1c, jnp.zeros((1, NMOD - 2 * D), F32)], axis=1)
    dmod16 = jnp.concatenate([dmod_rows, dmodc_row, jnp.zeros((BF16_SUBLANES - 9, NMOD), F32)], axis=0)

    def small_fn(rows, n1x, n1c, lossv):
        return (), (_colsum(rows), n1x + n1c, jnp.sum(lossv, axis=1, keepdims=True))

    _, (g_bmod, g_n1g, loss11) = _rowwise(small_fn, [dmod16], [t_n1x, t_n1c, t_loss], [], [(1, NMOD), (1, D), (1, 1)],
                                          name="small_reduce", tm=16)
    dmod_loc = lax.dynamic_slice_in_dim(dmod16, my_chip * NM, NM, axis=1)
    g_wmod = _mm([(silu_c, dmod_loc)], ta=True, name="mod_dw", outs=(F32,))
    dsilu_part = _mm([(dmod_loc, wm)], tb=True, name="mod_dx", outs=(F32,))
    part_all = _all_gather8([dsilu_part[8:9]], name="ag_cctx", in_vmem=True)[0]

    def cctx_fn(parts, dsl):
        return (), ((parts[0:1] + parts[2:3] + parts[4:5] + parts[6:7]) * dsl,)

    _, (g_cctx,) = _rowwise(cctx_fn, [part_all[:, 0, :]], [dsilu_c[8:9]], [], [(1, D)], name="cctx_grad", tm=8)

    red_late = reduced_halves(late, xchg_late, pair_late)
    other_late = _exchange_alone(_PairExchange(red_late, "gather"), name="rs_halves_late")
    grad_halves = dict(zip(tags, zip(red_late + red_early, other_late + other_early)))

    mod_upd = _adamw(w_mod[0], g_wmod, m_w_mod[0], v_w_mod[0], name="adamw_w_mod")
    grads = dict(
        c_ctx=g_cctx.reshape(D), w_mod=g_wmod[None], b_mod=g_bmod, norm1_g=g_n1g,
        q_norm_g=g_qng, kv_norm_g=g_kvng, qk_norm_q=t_gq[:, :QK], qk_norm_k=t_gk[:, :QK], sgu_norm_g=g_sng,
        sgu_norm_b=g_snb, w_spatial=g_ws.reshape(w_spatial.shape), b_spatial=t_bs[:, :G].T[None], norm2_g=g_n2g)
    weights = dict(c_ctx=c_ctx, w_mod=w_mod, b_mod=b_mod, norm1_g=norm1_g, w_in=w_in, q_norm_g=q_norm_g,
                   kv_norm_g=kv_norm_g, w_uq=w_uq, w_ukv=w_ukv, qk_norm_q=qk_norm_q, qk_norm_k=qk_norm_k,
                   sgu_norm_g=sgu_norm_g, sgu_norm_b=sgu_norm_b, w_spatial=w_spatial, b_spatial=b_spatial,
                   w_br_attn=w_br_attn, w_br_sgu=w_br_sgu, w_out=w_out, norm2_g=norm2_g, w_ffn_in=w_ffn_in,
                   w_ffn_out=w_ffn_out)
    m_in = dict(c_ctx=m_c_ctx, w_mod=m_w_mod, b_mod=m_b_mod, norm1_g=m_norm1_g, w_in=m_w_in, q_norm_g=m_q_norm_g,
                kv_norm_g=m_kv_norm_g, w_uq=m_w_uq, w_ukv=m_w_ukv, qk_norm_q=m_qk_norm_q, qk_norm_k=m_qk_norm_k,
                sgu_norm_g=m_sgu_norm_g, sgu_norm_b=m_sgu_norm_b, w_spatial=m_w_spatial, b_spatial=m_b_spatial,
                w_br_attn=m_w_br_attn, w_br_sgu=m_w_br_sgu, w_out=m_w_out, norm2_g=m_norm2_g, w_ffn_in=m_w_ffn_in,
                w_ffn_out=m_w_ffn_out)
    v_in_ = dict(c_ctx=v_c_ctx, w_mod=v_w_mod, b_mod=v_b_mod, norm1_g=v_norm1_g, w_in=v_w_in, q_norm_g=v_q_norm_g,
                 kv_norm_g=v_kv_norm_g, w_uq=v_w_uq, w_ukv=v_w_ukv, qk_norm_q=v_qk_norm_q, qk_norm_k=v_qk_norm_k,
                 sgu_norm_g=v_sgu_norm_g, sgu_norm_b=v_sgu_norm_b, w_spatial=v_w_spatial, b_spatial=v_b_spatial,
                 w_br_attn=v_w_br_attn, w_br_sgu=v_w_br_sgu, w_out=v_w_out, norm2_g=v_norm2_g, w_ffn_in=v_w_ffn_in,
                 w_ffn_out=v_w_ffn_out)
    names = list(weights)
    big_names = ("w_mod", "w_in", "w_uq", "w_ukv", "w_br_attn", "w_br_sgu", "w_out", "w_ffn_in", "w_ffn_out")
    out_g, out_d, out_m, out_v = {}, {}, {}, {}
    out_g["w_mod"] = grads["w_mod"]
    out_d["w_mod"], out_m["w_mod"], out_v["w_mod"] = [t[None] for t in mod_upd[:3]]
    for nm in big_names[1:]:
        res = _adamw_halves(weights[nm][0], *grad_halves[nm], m_in[nm][0], v_in_[nm][0], core, name="adamw_" + nm)
        out_g[nm], out_d[nm], out_m[nm], out_v[nm] = [t[None] for t in res]
    row_names = [nm for nm in names if nm not in big_names and nm not in ("w_spatial", "b_spatial")]
    widths = [-(-weights[nm].size // LANES) * LANES for nm in row_names]

    def as_row(d):
        return jnp.concatenate([jnp.pad(d[nm].reshape(1, -1), ((0, 0), (0, wd - d[nm].size)))
                                for nm, wd in zip(row_names, widths)], axis=1)

    def as_mat(d):
        return jnp.concatenate([d["w_spatial"].reshape(G * CH, CH), d["b_spatial"].reshape(G, CH)], axis=0)

    row_res = _adamw(as_row(weights), as_row(grads), as_row(m_in), as_row(v_in_), name="adamw_rows")
    mat_res = _adamw(as_mat(weights), as_mat(grads), as_mat(m_in), as_mat(v_in_), name="adamw_spatial")
    for tgt, row, mat in zip((out_d, out_m, out_v), row_res, mat_res):
        for nm, seg in zip(row_names, _split_lanes(row, widths)):
            tgt[nm] = seg[:, :weights[nm].size].reshape(weights[nm].shape)
        tgt["w_spatial"] = mat[:G * CH].reshape(w_spatial.shape)
        tgt["b_spatial"] = mat[G * CH:].reshape(b_spatial.shape)
    for nm in row_names + ["w_spatial", "b_spatial"]:
        out_g[nm] = grads[nm].reshape(weights[nm].shape)

    loss = loss11.reshape(())
    return (loss, grad_x[None], *[out_g[n] for n in names], *[out_d[n] for n in names],
            *[out_m[n] for n in names], *[out_v[n] for n in names])
```
